```python
import math
import jax
import jax.numpy as jnp
from jax import lax
import numpy as np

D_MODEL = 1024
BATCH = 8
SEQ = 2048
DEPTH = 2

GRID_W = 64
CTX_LEN = 256
F32 = jnp.float32

HY_DIM = 1024
HY_ORDER = 2
HY_BANDS = 16
HY_POS_DIM = 1 + 2 * HY_BANDS
HY_FF = 64
HY_DECAY_TARGET = 1e-2
HY_DECAY_PCT_SHORT = 0.3
HY_DECAY_PCT_LONG = 1.5
HG_HEADS = 8
HG_DIM = 1024
HG_HK = HG_DIM // HG_HEADS
HG_HV = HG_DIM // HG_HEADS
HG_CHUNK = 64
RT_HEADS = 4
RT_HK = 256
RT_HV = 512
RT_QK = RT_HEADS * RT_HK
RT_V = RT_HEADS * RT_HV
RT_CHUNK = 128
RT_ROPE_BASE = 10000.0
N_BRANCH = 3
STATE_SIZES = (HG_DIM, HG_DIM, HG_DIM, RT_QK, RT_V)
OUTPUT_SIZES = (HG_DIM, HG_DIM, RT_QK, RT_V, 3 * HY_DIM, N_BRANCH * D_MODEL)
N_STATE_COLS = 3 * HG_DIM + RT_QK + RT_V
N_IN_COLS = N_STATE_COLS + 2 * HG_DIM + RT_QK + RT_V + 3 * HY_DIM + N_BRANCH * D_MODEL
D_FF = 2816
N_EXPERTS = 8
TOP_K = 2
D_EXPERT = 3584
MOE_BLOCK = 256
N_DENSE = (DEPTH + 1) // 2
N_MOE = DEPTH // 2
DN_ALPHA = (2 * DEPTH) ** 0.25
DN_BETA = (8 * DEPTH) ** -0.25
LN_EPS = 1e-5

kernel_name = 'hybrid_hyena_hgrn2_retention_moe_dit_trunk'


def _split(x, sizes):
    return jnp.split(x, np.cumsum(sizes)[:-1].tolist(), axis=-1)


def _layer_norm(x, g, b):
    xf = x.astype(F32)
    mu = jnp.mean(xf, axis=-1, keepdims=True)
    var = jnp.mean(jnp.square(xf - mu), axis=-1, keepdims=True)
    return ((xf - mu) * lax.rsqrt(var + LN_EPS)).astype(x.dtype) * g + b


def _rms_norm(x, w=None):
    xf = x.astype(F32)
    y = (xf * lax.rsqrt(jnp.mean(jnp.square(xf), axis=-1, keepdims=True) + LN_EPS)).astype(x.dtype)
    return y if w is None else y * w


def _heads(x, n):
    b, l, _ = x.shape
    return x.reshape(b, l, n, -1).transpose(0, 2, 1, 3)


def _merge_heads(x):
    b, h, l, d = x.shape
    return x.transpose(0, 2, 1, 3).reshape(b, l, h * d)


def _to_chunks(x, c):
    b, h, l, d = x.shape
    return jnp.moveaxis(x.reshape(b, h, l // c, c, d), 2, 0)


def _from_chunks(x):
    n, b, h, c, d = x.shape
    return jnp.moveaxis(x, 0, 2).reshape(b, h, n * c, d)


def _flip_seq(args):
    return tuple(None if a is None else jnp.flip(a, axis=2) for a in args)


def _hyena_filters(L, w1, b1, freq, w2, b2, w3):
    t01 = jnp.linspace(0.0, 1.0, L, dtype=F32)[:, None]
    ang = 2.0 * math.pi * jnp.arange(L, dtype=F32)[:, None] / L
    bands = jnp.linspace(1e-4, HY_BANDS - 1, HY_BANDS, dtype=F32)[None, :]
    z = jnp.concatenate([t01, jnp.cos(bands * ang), -jnp.sin(bands * ang)], axis=-1).astype(w1.dtype)
    hdn = jnp.sin(freq[0] * (z @ w1 + b1))
    hdn = jnp.sin(freq[1] * (hdn @ w2 + b2))
    filt = (hdn @ w3).reshape(L, HY_ORDER, 2, HY_DIM)
    deltas = jnp.abs(jnp.linspace(math.log(HY_DECAY_TARGET) / HY_DECAY_PCT_LONG,
                                  math.log(HY_DECAY_TARGET) / HY_DECAY_PCT_SHORT, HY_DIM, dtype=F32))
    filt = filt * jnp.exp(-t01[:, :, None, None] * deltas).astype(filt.dtype)
    fwd = filt[:, :, 0]
    bwd = jnp.flip(filt[1:, :, 1], axis=0)
    kern = jnp.concatenate([fwd, jnp.zeros((1, HY_ORDER, HY_DIM), filt.dtype), bwd], axis=0)
    return kern / jnp.sum(jnp.abs(kern), axis=0, keepdims=True)


def _fft_long_conv(u, filt):
    L = u.shape[1]
    U = jnp.fft.rfft(u.astype(F32), n=2 * L, axis=1)
    K = jnp.fft.rfft(filt.astype(F32), axis=0)
    return jnp.fft.irfft(U * K[None], n=2 * L, axis=1)[:, :L].astype(u.dtype)


def _hyena(u, lp, grid):
    b, L, c3 = u.shape
    w = lp['hy_conv_w'].astype(u.dtype)
    if grid:
        rows = L // GRID_W
        u4 = u.reshape(b, rows, GRID_W, c3)
        filt = w[:, :, None, :]
    else:
        u4 = u[:, None]
        filt = w[1:2, :, None, :]
    uc = lax.conv_general_dilated(u4, filt, (1, 1), 'SAME',
                                  dimension_numbers=('NHWC', 'HWIO', 'NHWC'),
                                  feature_group_count=c3).reshape(b, L, c3) + lp['hy_conv_b']
    v, x1, x2 = jnp.split(uc, 3, axis=-1)
    filters = _hyena_filters(L, lp['hy_ff_w1'], lp['hy_ff_b1'], lp['hy_ff_freq'],
                             lp['hy_ff_w2'], lp['hy_ff_b2'], lp['hy_ff_w3'])
    z = v
    for n, gate in enumerate((x1, x2)):
        z = gate * (_fft_long_conv(z, filters[:, n]) + lp['hy_skip'][n] * z)
    return z


def _gla_scan(q, k, v, log_f, s0, with_output):
    C = HG_CHUNK
    mask = jnp.tril(jnp.ones((C, C), dtype=bool))
    ks, vs, gs = _to_chunks(k, C), _to_chunks(v, C), _to_chunks(log_f, C)

    def update(S, kc, vc, bcum):
        b_end = bcum[:, :, -1:, :]
        return (jnp.exp(b_end[:, :, 0, :, None]) * S
                + jnp.einsum('bhsk,bhsv->bhkv', kc * jnp.exp(b_end - bcum), vc))

    if with_output:
        qs = _to_chunks(q, C)

        def step(S, inp):
            qc, kc, vc, gc = inp
            bcum = jnp.cumsum(gc.astype(F32), axis=2)
            o_inter = jnp.einsum('bhtk,bhkv->bhtv', qc * jnp.exp(bcum), S)
            diff = bcum[:, :, :, None, :] - bcum[:, :, None, :, :]
            decay = jnp.where(mask[:, :, None], jnp.exp(jnp.minimum(diff, 0.0)), 0.0)
            scores = jnp.einsum('bhtk,bhtsk->bhts', qc, decay * kc[:, :, None, :, :])
            o = o_inter + jnp.einsum('bhts,bhsv->bhtv', scores, vc)
            return update(S, kc, vc, bcum), o.astype(v.dtype)

        S, os = lax.scan(step, s0, (qs, ks, vs, gs))
        return _from_chunks(os), S

    def step_state(S, inp):
        kc, vc, gc = inp
        return update(S, kc, vc, jnp.cumsum(gc.astype(F32), axis=2)), None

    S, _ = lax.scan(step_state, s0, (ks, vs, gs))
    return None, S


def _retention_scan(q, k, v, log_gamma, s0, with_output):
    C = RT_CHUNK
    pos = jnp.arange(C, dtype=F32)
    lg = log_gamma.astype(F32)
    rel = pos[:, None] - pos[None, :]
    intra = jnp.where(rel >= 0, jnp.exp(jnp.maximum(rel, 0.0)[None] * lg[:, None, None]), 0.0)
    q_decay = jnp.exp((pos + 1.0)[None, :] * lg[:, None])[None, :, :, None]
    k_decay = jnp.exp((C - 1.0 - pos)[None, :] * lg[:, None])[None, :, :, None]
    s_decay = jnp.exp(C * lg)[None, :, None, None]

    def update(S, kc, vc):
        return s_decay * S + jnp.einsum('bhsk,bhsv->bhkv', kc * k_decay, vc)

    ks, vs = _to_chunks(k, C), _to_chunks(v, C)
    if with_output:
        qs = _to_chunks(q, C)

        def step(S, inp):
            qc, kc, vc = inp
            scores = jnp.einsum('bhtk,bhsk->bhts', qc, kc) * intra
            o = (jnp.einsum('bhtk,bhkv->bhtv', qc, S) * q_decay
                 + jnp.einsum('bhts,bhsv->bhtv', scores, vc))
            return update(S, kc, vc), o.astype(v.dtype)

        S, os = lax.scan(step, s0, (qs, ks, vs))
        return _from_chunks(os), S

    def step_state(S, inp):
        kc, vc = inp
        return update(S, kc, vc), None

    S, _ = lax.scan(step_state, s0, (ks, vs))
    return None, S


def _run_dirs(scan_fn, lat_in, ctx_in, s0, need_ctx_out):
    o_lat, o_ctx = None, None
    for d in range(2):
        o_c, s_c = scan_fn(*ctx_in[d], s0, need_ctx_out)
        o_l, _ = scan_fn(*lat_in[d], s_c, True)
        if d == 1:
            o_l = jnp.flip(o_l, axis=2)
            o_c = None if o_c is None else jnp.flip(o_c, axis=2)
        o_lat = o_l if o_lat is None else o_lat + o_l
        if need_ctx_out:
            o_ctx = o_c if o_ctx is None else o_ctx + o_c
    return o_lat, o_ctx


def _hgrn2_gates(f_logit, lb):
    lbb = lb[None, :, None, :]
    fl = f_logit.astype(F32)
    log_f = jnp.logaddexp(jnp.log(lbb), jnp.log1p(-lbb) + jax.nn.log_sigmoid(fl))
    key = ((1.0 - lbb) * jax.nn.sigmoid(-fl)).astype(f_logit.dtype)
    return log_f, key


def _hgrn2(lat, ctx, lb, norm_w, need_ctx_out):
    def prep(f_fwd, f_bwd, i, q):
        vh = _heads(i, HG_HEADS)
        qh = None if q is None else _heads(jax.nn.silu(q), HG_HEADS) * HG_HK ** -0.5
        dirs = []
        for d, f in enumerate((f_fwd, f_bwd)):
            log_f, key = _hgrn2_gates(_heads(f, HG_HEADS), lb[d].reshape(HG_HEADS, HG_HK))
            args = (qh, key, vh, log_f)
            dirs.append(args if d == 0 else _flip_seq(args))
        return dirs

    b = lat[0].shape[0]
    s0 = jnp.zeros((b, HG_HEADS, HG_HK, HG_HV), F32)
    o_l, o_c = _run_dirs(_gla_scan, prep(*lat[:4]), prep(*ctx[:4]), s0, need_ctx_out)

    def readout(o, g):
        return _rms_norm(_merge_heads(o), norm_w) * jax.nn.silu(g)

    return readout(o_l, lat[4]), (readout(o_c, ctx[4]) if need_ctx_out else None)


def _rotary(x, pos):
    half = x.shape[-1] // 2
    inv = 1.0 / (RT_ROPE_BASE ** jnp.linspace(0.0, 1.0, half, dtype=F32))
    ang = pos[:, None] * inv[None, :]
    cos, sin = jnp.cos(ang).astype(x.dtype), jnp.sin(ang).astype(x.dtype)
    x1, x2 = x[..., :half], x[..., half:]
    return jnp.concatenate([x1 * cos - x2 * sin, x1 * sin + x2 * cos], axis=-1)


def _retention_log_gammas():
    j = jnp.arange(2 * RT_HEADS, dtype=F32)
    lg = jnp.log1p(-jnp.exp2(-5.0 - j))
    return lg[0::2], lg[1::2]


def _retention(lat, ctx, need_ctx_out):
    lg_f, lg_b = _retention_log_gammas()

    def prep(q, k, v, pos):
        kh = _rotary(_heads(k, RT_HEADS), pos) * RT_HK ** -0.5
        qh = None if q is None else _rotary(_heads(q, RT_HEADS), pos)
        vh = _heads(v, RT_HEADS)
        return [(qh, kh, vh, lg_f), _flip_seq((qh, kh, vh)) + (lg_b,)]

    b, L = lat[1].shape[0], lat[1].shape[1]
    Lc = ctx[1].shape[1]
    pos_c = jnp.arange(Lc, dtype=F32)
    pos_l = Lc + jnp.arange(L, dtype=F32)
    s0 = jnp.zeros((b, RT_HEADS, RT_HK, RT_HV), F32)
    o_l, o_c = _run_dirs(_retention_scan, prep(lat[0], lat[1], lat[2], pos_l),
                         prep(ctx[0], ctx[1], ctx[2], pos_c), s0, need_ctx_out)

    def readout(o, g):
        return _merge_heads(_rms_norm(o)) * jax.nn.silu(g)

    return readout(o_l, lat[3]), (readout(o_c, ctx[3]) if need_ctx_out else None)


def _merge(y_hy, y_hg, y_rt, gate_logits, lp):
    g_hy, g_hg, g_rt = jnp.split(jax.nn.sigmoid(gate_logits), N_BRANCH, axis=-1)
    m = g_hy * (y_hy @ lp['p_hy']) + g_hg * (y_hg @ lp['p_hg']) + g_rt * (y_rt @ lp['p_rt'])
    return m @ lp['w_o']


def _mixer(h, hc, lp, lb, need_ctx_out):
    w_in = lp['w_in']
    f_f, f_b, hi, rk, rv, hq, hg, rq, rg, hy, br = _split(h @ w_in, STATE_SIZES + OUTPUT_SIZES)
    if need_ctx_out:
        cf_f, cf_b, chi, crk, crv, chq, chg, crq, crg, chy, cbr = _split(hc @ w_in, STATE_SIZES + OUTPUT_SIZES)
    else:
        cf_f, cf_b, chi, crk, crv = _split(hc @ w_in[:, :N_STATE_COLS], STATE_SIZES)
        chq = chg = crq = crg = chy = cbr = None
    y_hg, c_hg = _hgrn2((f_f, f_b, hi, hq, hg), (cf_f, cf_b, chi, chq, chg), lb, lp['hg_norm_w'], need_ctx_out)
    y_rt, c_rt = _retention((rq, rk, rv, rg), (crq, crk, crv, crg), need_ctx_out)
    y_hy = _hyena(hy, lp, True)
    m = _merge(y_hy, y_hg, y_rt, br, lp)
    if not need_ctx_out:
        return m, None
    c_hy = _hyena(chy, lp, False)
    return m, _merge(c_hy, c_hg, c_rt, cbr, lp)


def _swiglu(h, w1, w3, w2):
    return (jax.nn.silu(h @ w1) * (h @ w3)) @ w2


def _moe_swiglu(t, router, w1, w3, w2):
    n, d = t.shape
    e = router.shape[-1]
    logits = (t @ router).astype(F32)
    top_val, top_idx = lax.top_k(logits, TOP_K)
    gate = jax.nn.softmax(top_val, axis=-1).astype(t.dtype)
    flat_e = top_idx.reshape(-1)
    flat_t = jnp.repeat(jnp.arange(n, dtype=jnp.int32), TOP_K)
    flat_g = gate.reshape(-1)
    order = jnp.argsort(flat_e)
    se, st, sg = flat_e[order], flat_t[order], flat_g[order]
    counts = jnp.bincount(flat_e, length=e)
    starts = jnp.cumsum(counts) - counts
    padded = (counts + MOE_BLOCK - 1) // MOE_BLOCK * MOE_BLOCK
    pad_end = jnp.cumsum(padded)
    dest = (pad_end - padded)[se] + jnp.arange(n * TOP_K) - starts[se]
    n_blocks = -(-(n * TOP_K) // MOE_BLOCK) + e
    n_slots = n_blocks * MOE_BLOCK
    slot_tok = jnp.full((n_slots,), n, dtype=jnp.int32).at[dest].set(st)
    slot_g = jnp.zeros((n_slots,), t.dtype).at[dest].set(sg)
    block_start = jnp.arange(n_blocks) * MOE_BLOCK
    block_e = jnp.minimum(jnp.sum(block_start[:, None] >= pad_end[None, :], axis=1), e - 1)
    xb = jnp.concatenate([t, jnp.zeros((1, d), t.dtype)], axis=0)[slot_tok].reshape(n_blocks, MOE_BLOCK, d)

    def run_block(args):
        xblk, eid = args
        return (jax.nn.silu(xblk @ w1[eid]) * (xblk @ w3[eid])) @ w2[eid]

    yb = lax.map(run_block, (xb, block_e)).reshape(n_slots, d)
    y = jnp.zeros((n + 1, d), t.dtype).at[slot_tok].add(yb * slot_g[:, None])
    return y[:n]


def _layer(x, xc, c, c_ctx, lp, lb, ffn_params, use_moe, need_ctx_out):
    b, l, d = x.shape
    mod = jax.nn.silu(c) @ lp['ada_w'] + lp['ada_b']
    mod_c = jax.nn.silu(c_ctx) @ lp['ada_w'] + lp['ada_b']
    sh1, sc1, gt1, sh2, sc2, gt2 = [m[:, None, :] for m in jnp.split(mod, 6, axis=-1)]
    sh1c, sc1c, gt1c, sh2c, sc2c, gt2c = jnp.split(mod_c, 6, axis=-1)
    h = x * (1.0 + sc1) + sh1
    hc = xc * (1.0 + sc1c) + sh1c
    m, mc = _mixer(h, hc, lp, lb, need_ctx_out)
    x = _layer_norm(DN_ALPHA * x + gt1 * m, lp['ln1_g'], lp['ln1_b'])
    h2 = x * (1.0 + sc2) + sh2
    if need_ctx_out:
        xc = _layer_norm(DN_ALPHA * xc + gt1c * mc, lp['ln1_g'], lp['ln1_b'])
        h2c = xc * (1.0 + sc2c) + sh2c
    if use_moe:
        tok = h2.reshape(b * l, d)
        if need_ctx_out:
            tok = jnp.concatenate([tok, h2c.reshape(-1, d)], axis=0)
        f_all = _moe_swiglu(tok, *ffn_params)
        f = f_all[:b * l].reshape(b, l, d)
        fc = f_all[b * l:].reshape(xc.shape) if need_ctx_out else None
    else:
        f = _swiglu(h2, *ffn_params)
        fc = _swiglu(h2c, *ffn_params) if need_ctx_out else None
    x = _layer_norm(DN_ALPHA * x + gt2 * f, lp['ln2_g'], lp['ln2_b'])
    xc = _layer_norm(DN_ALPHA * xc + gt2c * fc, lp['ln2_g'], lp['ln2_b']) if need_ctx_out else None
    return x, xc


def setup_inputs(seed: int = 0) -> dict:
    key = jax.random.key(seed)
    keys = iter(jax.random.split(key, 48))

    def nrm(shape, scale):
        return jax.random.normal(next(keys), shape, F32) * scale

    D = D_MODEL
    return {
        'x': nrm((BATCH, SEQ, D), 1.0),
        'c': nrm((BATCH, D), 1.0),
        'ctx': nrm((BATCH, CTX_LEN, D), 1.0),
        'c_ctx': nrm((D,), 1.0),
        'ada_w': nrm((DEPTH, D, 6 * D), D ** -0.5),
        'ada_b': nrm((DEPTH, 6 * D), 0.02),
        'w_in': nrm((DEPTH, D, N_IN_COLS), D ** -0.5),
        'hy_conv_w': nrm((DEPTH, 3, 3, 3 * HY_DIM), 1.0 / 3.0),
        'hy_conv_b': nrm((DEPTH, 3 * HY_DIM), 0.02),
        'hy_ff_w1': nrm((DEPTH, HY_POS_DIM, HY_FF), HY_POS_DIM ** -0.5),
        'hy_ff_b1': nrm((DEPTH, HY_FF), 0.1),
        'hy_ff_freq': 1.0 + nrm((DEPTH, 2, HY_FF), 0.02),
        'hy_ff_w2': nrm((DEPTH, HY_FF, HY_FF), HY_FF ** -0.5),
        'hy_ff_b2': nrm((DEPTH, HY_FF), 0.1),
        'hy_ff_w3': nrm((DEPTH, HY_FF, HY_ORDER * 2 * HY_DIM), HY_FF ** -0.5),
        'hy_skip': nrm((DEPTH, HY_ORDER, HY_DIM), 0.5),
        'hg_lb_logits': nrm((2, DEPTH, HG_DIM), 0.1),
        'hg_norm_w': 1.0 + nrm((DEPTH, HG_DIM), 0.02),
        'p_hy': nrm((DEPTH, HY_DIM, D), HY_DIM ** -0.5),
        'p_hg': nrm((DEPTH, HG_DIM, D), HG_DIM ** -0.5),
        'p_rt': nrm((DEPTH, RT_V, D), RT_V ** -0.5),
        'w_o': nrm((DEPTH, D, D), D ** -0.5 * DN_BETA),
        'ln1_g': 1.0 + nrm((DEPTH, D), 0.02),
        'ln1_b': nrm((DEPTH, D), 0.02),
        'ln2_g': 1.0 + nrm((DEPTH, D), 0.02),
        'ln2_b': nrm((DEPTH, D), 0.02),
        'ffn_w1': nrm((N_DENSE, D, D_FF), D ** -0.5),
        'ffn_w3': nrm((N_DENSE, D, D_FF), D ** -0.5),
        'ffn_w2': nrm((N_DENSE, D_FF, D), D_FF ** -0.5 * DN_BETA),
        'moe_router': nrm((N_MOE, D, N_EXPERTS), D ** -0.5),
        'moe_w1': nrm((N_MOE, N_EXPERTS, D, D_EXPERT), D ** -0.5),
        'moe_w3': nrm((N_MOE, N_EXPERTS, D, D_EXPERT), D ** -0.5),
        'moe_w2': nrm((N_MOE, N_EXPERTS, D_EXPERT, D), D_EXPERT ** -0.5 * DN_BETA),
    }


def reference(x, c, ctx, c_ctx, ada_w, ada_b, w_in, hy_conv_w, hy_conv_b, hy_ff_w1, hy_ff_b1,
              hy_ff_freq, hy_ff_w2, hy_ff_b2, hy_ff_w3, hy_skip, hg_lb_logits, hg_norm_w,
              p_hy, p_hg, p_rt, w_o, ln1_g, ln1_b, ln2_g, ln2_b, ffn_w1, ffn_w3, ffn_w2,
              moe_router, moe_w1, moe_w3, moe_w2):
    cs = jnp.cumsum(jax.nn.softmax(hg_lb_logits.astype(F32), axis=1), axis=1)
    lower_bounds = cs - cs[:, :1]
    xc = ctx
    for i in range(DEPTH):
        lp = {
            'ada_w': ada_w[i], 'ada_b': ada_b[i], 'w_in': w_in[i],
            'hy_conv_w': hy_conv_w[i], 'hy_conv_b': hy_conv_b[i],
            'hy_ff_w1': hy_ff_w1[i], 'hy_ff_b1': hy_ff_b1[i], 'hy_ff_freq': hy_ff_freq[i],
            'hy_ff_w2': hy_ff_w2[i], 'hy_ff_b2': hy_ff_b2[i], 'hy_ff_w3': hy_ff_w3[i],
            'hy_skip': hy_skip[i], 'hg_norm_w': hg_norm_w[i],
            'p_hy': p_hy[i], 'p_hg': p_hg[i], 'p_rt': p_rt[i], 'w_o': w_o[i],
            'ln1_g': ln1_g[i], 'ln1_b': ln1_b[i], 'ln2_g': ln2_g[i], 'ln2_b': ln2_b[i],
        }
        g = i // 2
        if i % 2 == 0:
            ffn_params = (ffn_w1[g], ffn_w3[g], ffn_w2[g])
        else:
            ffn_params = (moe_router[g], moe_w1[g], moe_w3[g], moe_w2[g])
        x, xc = _layer(x, xc, c, c_ctx, lp, lower_bounds[:, i], ffn_params, i % 2 == 1, i < DEPTH - 1)
    return x
```

```python
import functools
import math

import numpy as np
import jax
import jax.numpy as jnp
from jax import lax
from jax.experimental import pallas as pl
from jax.experimental.pallas import tpu as pltpu

F32 = jnp.float32
BF16 = jnp.bfloat16

D_MODEL = 1024
DEPTH = 2
GRID_W = 64
HY_DIM = 1024
HY_ORDER = 2
HY_BANDS = 16
HY_DECAY_TARGET = 1e-2
HY_DECAY_PCT_SHORT = 0.3
HY_DECAY_PCT_LONG = 1.5
HG_HEADS = 8
HG_HK = 128
RT_HEADS = 4
RT_HK = 256
RT_HV = 512
RT_ROPE_BASE = 10000.0
N_STATE_COLS = 6144
N_IN_COLS = 17408
N_EXPERTS = 8
TOP_K = 2
DN_ALPHA = (2 * DEPTH) ** 0.25
LN_EPS = 1e-5

COL_FF, COL_FB, COL_HI, COL_RK, COL_RV = 0, 8, 16, 24, 32
COL_HQ, COL_HG, COL_RQ, COL_RG, COL_HY, COL_BR = 48, 56, 64, 72, 88, 112

GLA_CHUNK = 64
GLA_SUB = 16
RET_CHUNK = 128
DFT_HALF = 256
CONV_PAD = 72
MOE_ROWS = 512
VMEM_LIMIT = 56 * 1024 * 1024


def _cparams(sem):
    return pltpu.CompilerParams(dimension_semantics=sem, vmem_limit_bytes=VMEM_LIMIT)


def _silu(x):
    return x * jax.nn.sigmoid(x)


def _layer_norm_rows(r, g, b):
    mu = jnp.mean(r, axis=-1, keepdims=True)
    d = r - mu
    var = jnp.mean(d * d, axis=-1, keepdims=True)
    return d * lax.rsqrt(var + LN_EPS) * g + b


def _split3(x):
    h = x.astype(BF16)
    r = x - h.astype(F32)
    m = r.astype(BF16)
    l = (r - m.astype(F32)).astype(BF16)
    return h, m, l


def _dot(a, b):
    return jnp.dot(a, b, preferred_element_type=F32)


def _dot_nt(a, b):
    return lax.dot_general(a, b, (((1,), (1,)), ((), ())), preferred_element_type=F32)


def _dot_tn(a, b):
    return lax.dot_general(a, b, (((0,), (0,)), ((), ())), preferred_element_type=F32)


def _ada_kernel(c_ref, w_ref, b_ref, o_ref):
    s = _silu(c_ref[...])
    o_ref[...] = _dot(s.astype(BF16), w_ref[...].astype(BF16)) + b_ref[...]


def _ada(cc, w, b):
    n = w.shape[1]
    tn = 1024
    return pl.pallas_call(
        _ada_kernel,
        grid=(n // tn,),
        in_specs=[pl.BlockSpec((16, D_MODEL), lambda j: (0, 0)),
                  pl.BlockSpec((D_MODEL, tn), lambda j: (0, j)),
                  pl.BlockSpec((1, tn), lambda j: (0, j))],
        out_specs=pl.BlockSpec((16, tn), lambda j: (0, j)),
        out_shape=jax.ShapeDtypeStruct((16, n), F32),
        compiler_params=_cparams(("arbitrary",)),
        name="ada_mod",
    )(cc, w, b.reshape(1, n))


def _mod_index(rows_per_batch, tm):
    if rows_per_batch is None:
        return lambda i: 8
    return lambda i: (i * tm) // rows_per_batch


def _proj_kernel(x_ref, mod_ref, w_ref, o_ref):
    m = mod_ref[0]
    h = x_ref[...] * (1.0 + m[1:2]) + m[0:1]
    o_ref[...] = _dot(h.astype(BF16), w_ref[...]).astype(o_ref.dtype)


def _proj(x, modp, w_bf, n_cols, rows_per_batch):
    m = x.shape[0]
    tm = min(512, m)
    tn = 1024
    mi = _mod_index(rows_per_batch, tm)
    return pl.pallas_call(
        _proj_kernel,
        grid=(n_cols // tn, m // tm),
        in_specs=[pl.BlockSpec((tm, D_MODEL), lambda j, i: (i, 0)),
                  pl.BlockSpec((1, 8, D_MODEL), lambda j, i: (mi(i), 0, 0)),
                  pl.BlockSpec((D_MODEL, tn), lambda j, i: (0, j))],
        out_specs=pl.BlockSpec((tm, tn), lambda j, i: (i, j)),
        out_shape=jax.ShapeDtypeStruct((m, n_cols), BF16),
        compiler_params=_cparams(("arbitrary", "arbitrary")),
        name="in_proj",
    )(x, modp, w_bf)


def _gla_gates(fl, gl_ref, d):
    a = gl_ref[d, 0:1, :]
    l1p = gl_ref[d, 1:2, :]
    oml = gl_ref[d, 2:3, :]
    x2 = l1p + jnp.minimum(fl, 0.0) - jnp.log1p(jnp.exp(-jnp.abs(fl)))
    log_f = jnp.maximum(a, x2) + jnp.log1p(jnp.exp(-jnp.abs(a - x2)))
    key = oml * jax.nn.sigmoid(-fl)
    return log_f, key


def _gla_cumsum(g, rev):
    c = g.shape[0]
    r = lax.broadcasted_iota(jnp.int32, (c, c), 0)
    u = lax.broadcasted_iota(jnp.int32, (c, c), 1)
    tri = jnp.where((u >= r) if rev else (u <= r), 1.0, 0.0).astype(BF16)
    h, m, l = _split3(g)
    return _dot(tri, h) + _dot(tri, m) + _dot(tri, l)


def _gla_state_update(st, b, kk, vv, rev):
    c = b.shape[0]
    end = 0 if rev else c - 1
    b_end = b[end:end + 1, :]
    khat = (kk * jnp.exp(b_end - b)).astype(BF16)
    return st * jnp.exp(b_end) + _dot_tn(vv, khat)


def _gla_chunk_out(b, kk, vv, qq, st, rev, kpad_ref, bpad_ref, vpad_ref):
    c, sb = GLA_CHUNK, GLA_SUB
    nb = c // sb
    rowi = lax.broadcasted_iota(jnp.int32, (c, HG_HK), 0)
    blk = rowi // sb
    rin = rowi % sb

    o = _dot_nt((qq * jnp.exp(b)).astype(BF16), st.astype(BF16))

    segs = range(1, nb) if not rev else range(0, nb - 1)
    ref_rows = jnp.zeros_like(b)
    for i in segs:
        rr = sb * i - 1 if not rev else sb * (i + 1)
        ref_rows = jnp.where(blk == i, b[rr:rr + 1, :], ref_rows)
    has_ref = (blk >= 1) if not rev else (blk <= nb - 2)
    qt = jnp.where(has_ref, qq * jnp.exp(jnp.minimum(b - ref_rows, 0.0)), 0.0)
    qsegs, ksegs = [], []
    for i in segs:
        rr = sb * i - 1 if not rev else sb * (i + 1)
        km = (blk < i) if not rev else (blk > i)
        ks = jnp.where(km, kk * jnp.exp(jnp.minimum(b[rr:rr + 1, :] - b, 0.0)), 0.0)
        qsegs.append(jnp.where(blk == i, qt, 0.0).astype(BF16))
        ksegs.append(ks.astype(BF16))
    sc = _dot_nt(jnp.concatenate(qsegs, axis=1), jnp.concatenate(ksegs, axis=1))
    o = o + _dot(sc.astype(BF16), vv)

    kpad_ref[sb:sb + c, :] = kk
    bpad_ref[sb:sb + c, :] = b
    vpad_ref[sb:sb + c, :] = vv.astype(F32)
    ws = []
    for d in range(sb):
        off = sb + d if rev else sb - d
        valid = (rin <= sb - 1 - d) if rev else (rin >= d)
        w = qq * kpad_ref[off:off + c, :]
        if d > 0:
            w = w * jnp.exp(jnp.minimum(b - bpad_ref[off:off + c, :], 0.0))
        ws.append(jnp.where(valid, w, 0.0).astype(BF16))
    r_all = _dot(jnp.concatenate(ws, axis=0), jnp.ones((HG_HK, HG_HK), BF16))
    for d in range(sb):
        off = sb + d if rev else sb - d
        o = o + r_all[d * c:(d + 1) * c, :] * vpad_ref[off:off + c, :]
    return o


def _gla_kernel(*refs, need_ctx_out, n_lat, n_ctx):
    if need_ctx_out:
        (ffl, fbl, il, ql, ffc, fbc, ic, qc, gl_ref, ol_ref, oc_ref,
         kpad_ref, bpad_ref, vpad_ref) = refs
    else:
        (ffl, fbl, il, ql, ffc, fbc, ic, gl_ref, ol_ref,
         kpad_ref, bpad_ref, vpad_ref) = refs
        qc = oc_ref = None
    c = GLA_CHUNK
    zpad = jnp.zeros(kpad_ref.shape, F32)
    kpad_ref[...] = zpad
    bpad_ref[...] = zpad
    vpad_ref[...] = zpad
    q_scale = HG_HK ** -0.5

    for d in range(2):
        rev = d == 1
        f_l, f_c = (fbl, fbc) if rev else (ffl, ffc)

        def rows_of(j, n):
            idx = (n - 1 - j) if rev else j
            return pl.ds(pl.multiple_of(idx * c, c), c)

        def load(f_ref, i_ref, rows):
            g, kk = _gla_gates(f_ref[rows, :].astype(F32), gl_ref, d)
            return _gla_cumsum(g, rev), kk, i_ref[rows, :]

        def out_step(f_ref, i_ref, q_ref, o_ref, rows, st):
            b, kk, vv = load(f_ref, i_ref, rows)
            qq = _silu(q_ref[rows, :].astype(F32)) * q_scale
            o = _gla_chunk_out(b, kk, vv, qq, st, rev, kpad_ref, bpad_ref, vpad_ref)
            if rev:
                o_ref[rows, :] += o
            else:
                o_ref[rows, :] = o
            return _gla_state_update(st, b, kk, vv, rev)

        def ctx_body(j, st):
            rows = rows_of(j, n_ctx)
            if need_ctx_out:
                return out_step(f_c, ic, qc, oc_ref, rows, st)
            b, kk, vv = load(f_c, ic, rows)
            return _gla_state_update(st, b, kk, vv, rev)

        def lat_body(j, st):
            return out_step(f_l, il, ql, ol_ref, rows_of(j, n_lat), st)

        st = lax.fori_loop(0, n_ctx, ctx_body, jnp.zeros((HG_HK, HG_HK), F32))
        lax.fori_loop(0, n_lat, lat_body, st)


def _gla(p_lat, p_ctx, gl_tab, batch, need_ctx_out):
    l = p_lat.shape[0] // batch
    lc = p_ctx.shape[0] // batch
    w = HG_HK

    def col(base):
        return lambda b, h: (b, base + h)

    lat_specs = [pl.BlockSpec((l, w), col(cb)) for cb in (COL_FF, COL_FB, COL_HI, COL_HQ)]
    ctx_cols = (COL_FF, COL_FB, COL_HI) + ((COL_HQ,) if need_ctx_out else ())
    ctx_specs = [pl.BlockSpec((lc, w), col(cb)) for cb in ctx_cols]
    in_specs = lat_specs + ctx_specs + [pl.BlockSpec((2, 8, w), lambda b, h: (0, 0, h))]
    args = [p_lat] * 4 + [p_ctx] * len(ctx_cols) + [gl_tab]
    out_specs = [pl.BlockSpec((l, w), lambda b, h: (b, h))]
    out_shape = [jax.ShapeDtypeStruct((batch * l, HG_HEADS * w), F32)]
    if need_ctx_out:
        out_specs.append(pl.BlockSpec((lc, w), lambda b, h: (b, h)))
        out_shape.append(jax.ShapeDtypeStruct((batch * lc, HG_HEADS * w), F32))
    pad_rows = GLA_CHUNK + 2 * GLA_SUB
    outs = pl.pallas_call(
        functools.partial(_gla_kernel, need_ctx_out=need_ctx_out,
                          n_lat=l // GLA_CHUNK, n_ctx=lc // GLA_CHUNK),
        grid=(batch, HG_HEADS),
        in_specs=in_specs,
        out_specs=out_specs,
        out_shape=out_shape,
        scratch_shapes=[pltpu.VMEM((pad_rows, w), F32)] * 3,
        compiler_params=_cparams(("arbitrary", "arbitrary")),
        name="hgrn2_scan",
    )(*args)
    return outs[0], (outs[1] if need_ctx_out else None)


def _rotate(x, cos, sin):
    half = x.shape[-1] // 2
    x1, x2 = x[:, :half], x[:, half:]
    return jnp.concatenate([x1 * cos - x2 * sin, x1 * sin + x2 * cos], axis=-1)


def _ret_kernel(*refs, need_ctx_out, n_lat, n_ctx):
    if need_ctx_out:
        (ql, kl, vl, gl, qc, kc, vc, gc, cos_ref, sin_ref, intra_ref, qk_ref, sd_ref,
         yl_ref, yc_ref, qrl, krl, qrc, krc, ol, oc, s_ref) = refs
    else:
        (ql, kl, vl, gl, kc, vc, cos_ref, sin_ref, intra_ref, qk_ref, sd_ref,
         yl_ref, qrl, krl, krc, ol, s_ref) = refs
        qc = gc = yc_ref = qrc = oc = None
    c = RET_CHUNK
    lc = n_ctx * c
    k_scale = RT_HK ** -0.5

    def rot_all(src, dst, n, pos0, scale):
        def body(j, carry):
            rows = pl.ds(pl.multiple_of(j * c, c), c)
            prow = pl.ds(pl.multiple_of(pos0 + j * c, c), c)
            x = src[rows, :].astype(F32)
            dst[rows, :] = (_rotate(x, cos_ref[prow, :], sin_ref[prow, :]) * scale).astype(BF16)
            return carry
        lax.fori_loop(0, n, body, 0)

    rot_all(kl, krl, n_lat, lc, k_scale)
    rot_all(kc, krc, n_ctx, 0, k_scale)
    rot_all(ql, qrl, n_lat, lc, 1.0)
    if need_ctx_out:
        rot_all(qc, qrc, n_ctx, 0, 1.0)

    for d in range(2):
        rev = d == 1
        intra = intra_ref[d, 0]
        qd = qk_ref[d, 0, 0:c, :]
        kd = qk_ref[d, 0, c:2 * c, :]
        qd2 = jnp.concatenate([qd, qd], axis=1)
        kd2 = jnp.concatenate([kd, kd], axis=1)
        sdec = sd_ref[d, 0, 0:1, :]
        sdec4 = jnp.concatenate([sdec] * (RT_HV // 128), axis=1)
        s_ref[...] = jnp.zeros(s_ref.shape, F32)

        def rows_of(j, n):
            idx = (n - 1 - j) if rev else j
            return pl.ds(pl.multiple_of(idx * c, c), c)

        def update(kr, v_ref, rows):
            kh = (kr[rows, :].astype(F32) * kd2).astype(BF16)
            s_ref[...] = s_ref[...] * sdec4 + _dot_tn(kh, v_ref[rows, :])

        def out_step(qr, kr, v_ref, o_ref, rows):
            qcb = qr[rows, :]
            sc = _dot_nt(qcb, kr[rows, :]) * intra
            qh = (qcb.astype(F32) * qd2).astype(BF16)
            o = _dot(qh, s_ref[...].astype(BF16)) + _dot(sc.astype(BF16), v_ref[rows, :])
            if rev:
                o_ref[rows, :] += o
            else:
                o_ref[rows, :] = o
            update(kr, v_ref, rows)

        def ctx_body(j, carry):
            rows = rows_of(j, n_ctx)
            if need_ctx_out:
                out_step(qrc, krc, vc, oc, rows)
            else:
                update(krc, vc, rows)
            return carry

        def lat_body(j, carry):
            out_step(qrl, krl, vl, ol, rows_of(j, n_lat))
            return carry

        lax.fori_loop(0, n_ctx, ctx_body, 0)
        lax.fori_loop(0, n_lat, lat_body, 0)

    def readout(o_ref, g_ref, y_ref, n):
        def body(j, carry):
            rows = pl.ds(pl.multiple_of(j * c, c), c)
            o = o_ref[rows, :]
            y = o * lax.rsqrt(jnp.mean(o * o, axis=-1, keepdims=True) + LN_EPS)
            y_ref[rows, :] = (y * _silu(g_ref[rows, :].astype(F32))).astype(y_ref.dtype)
            return carry
        lax.fori_loop(0, n, body, 0)

    readout(ol, gl, yl_ref, n_lat)
    if need_ctx_out:
        readout(oc, gc, yc_ref, n_ctx)


def _retention_tables(lc, l):
    half = RT_HK // 2
    inv = 1.0 / (RT_ROPE_BASE ** jnp.linspace(0.0, 1.0, half, dtype=F32))
    ang = jnp.arange(lc + l, dtype=F32)[:, None] * inv[None, :]
    j = jnp.arange(2 * RT_HEADS, dtype=F32)
    lg_all = jnp.log1p(-jnp.exp2(-5.0 - j))
    c = RET_CHUNK
    pos = jnp.arange(c, dtype=F32)
    rel = pos[:, None] - pos[None, :]
    intra, qk, sd = [], [], []
    for d in range(2):
        lg = lg_all[d::2]
        m = jnp.where(rel >= 0, jnp.exp(jnp.maximum(rel, 0.0)[None] * lg[:, None, None]), 0.0)
        qdec = jnp.exp((pos + 1.0)[None, :] * lg[:, None])
        kdec = jnp.exp((c - 1.0 - pos)[None, :] * lg[:, None])
        if d == 1:
            m = jnp.swapaxes(m, 1, 2)
            qdec = qdec[:, ::-1]
            kdec = kdec[:, ::-1]
        intra.append(m)
        qk.append(jnp.broadcast_to(jnp.concatenate([qdec, kdec], axis=1)[:, :, None],
                                   (RT_HEADS, 2 * c, 128)))
        sd.append(jnp.broadcast_to(jnp.exp(c * lg)[:, None, None], (RT_HEADS, 8, 128)))
    return (jnp.cos(ang), jnp.sin(ang), jnp.stack(intra), jnp.stack(qk), jnp.stack(sd))


def _retention(p_lat, p_ctx, tables, batch, need_ctx_out):
    l = p_lat.shape[0] // batch
    lc = p_ctx.shape[0] // batch
    cos, sin, intra, qk, sd = tables
    c = RET_CHUNK

    def col(base):
        return lambda b, h: (b, base + h)

    kq, kk_, kv, kg = COL_RQ * 128 // RT_HK, COL_RK * 128 // RT_HK, COL_RV * 128 // RT_HV, COL_RG * 128 // RT_HV
    lat_specs = [pl.BlockSpec((l, RT_HK), col(kq)), pl.BlockSpec((l, RT_HK), col(kk_)),
                 pl.BlockSpec((l, RT_HV), col(kv)), pl.BlockSpec((l, RT_HV), col(kg))]
    if need_ctx_out:
        ctx_specs = [pl.BlockSpec((lc, RT_HK), col(kq)), pl.BlockSpec((lc, RT_HK), col(kk_)),
                     pl.BlockSpec((lc, RT_HV), col(kv)), pl.BlockSpec((lc, RT_HV), col(kg))]
    else:
        ctx_specs = [pl.BlockSpec((lc, RT_HK), col(kk_)), pl.BlockSpec((lc, RT_HV), col(kv))]
    tab_specs = [pl.BlockSpec((lc + l, 128), lambda b, h: (0, 0)),
                 pl.BlockSpec((lc + l, 128), lambda b, h: (0, 0)),
                 pl.BlockSpec((2, 1, c, c), lambda b, h: (0, h, 0, 0)),
                 pl.BlockSpec((2, 1, 2 * c, 128), lambda b, h: (0, h, 0, 0)),
                 pl.BlockSpec((2, 1, 8, 128), lambda b, h: (0, h, 0, 0))]
    args = [p_lat] * 4 + [p_ctx] * len(ctx_specs) + [cos, sin, intra, qk, sd]
    out_specs = [pl.BlockSpec((l, RT_HV), lambda b, h: (b, h))]
    out_shape = [jax.ShapeDtypeStruct((batch * l, RT_HEADS * RT_HV), BF16)]
    scratch = [pltpu.VMEM((l, RT_HK), BF16), pltpu.VMEM((l, RT_HK), BF16)]
    if need_ctx_out:
        out_specs.append(pl.BlockSpec((lc, RT_HV), lambda b, h: (b, h)))
        out_shape.append(jax.ShapeDtypeStruct((batch * lc, RT_HEADS * RT_HV), BF16))
        scratch += [pltpu.VMEM((lc, RT_HK), BF16), pltpu.VMEM((lc, RT_HK), BF16),
                    pltpu.VMEM((l, RT_HV), F32), pltpu.VMEM((lc, RT_HV), F32)]
    else:
        scratch += [pltpu.VMEM((lc, RT_HK), BF16), pltpu.VMEM((l, RT_HV), F32)]
    scratch.append(pltpu.VMEM((RT_HK, RT_HV), F32))
    outs = pl.pallas_call(
        functools.partial(_ret_kernel, need_ctx_out=need_ctx_out, n_lat=l // c, n_ctx=lc // c),
        grid=(batch, RT_HEADS),
        in_specs=lat_specs + ctx_specs + tab_specs,
        out_specs=out_specs,
        out_shape=out_shape,
        scratch_shapes=scratch,
        compiler_params=_cparams(("arbitrary", "arbitrary")),
        name="retention_scan",
    )(*args)
    return outs[0], (outs[1] if need_ctx_out else None)


def _short_conv_kernel(x_ref, w_ref, b_ref, o_ref, xpad_ref, *, grid):
    l, tc = x_ref.shape
    p = CONV_PAD
    zeros = jnp.zeros((p, tc), F32)
    xpad_ref[0:p, :] = zeros
    xpad_ref[p + l:p + l + p, :] = zeros
    xpad_ref[p:p + l, :] = x_ref[...].astype(F32)
    colw = lax.broadcasted_iota(jnp.int32, (l, tc), 0) % GRID_W
    out = jnp.broadcast_to(b_ref[...], (l, tc))
    for dj in range(3):
        acc = None
        for di in (range(3) if grid else (1,)):
            off = p + (di - 1) * GRID_W + (dj - 1)
            t = w_ref[di * 3 + dj:di * 3 + dj + 1, :] * xpad_ref[off:off + l, :]
            acc = t if acc is None else acc + t
        if grid and dj == 0:
            acc = jnp.where(colw >= 1, acc, 0.0)
        if grid and dj == 2:
            acc = jnp.where(colw <= GRID_W - 2, acc, 0.0)
        out = out + acc
    o_ref[...] = out.astype(o_ref.dtype)


def _short_conv(p_arr, w9, bias, batch, grid):
    l = p_arr.shape[0] // batch
    c3 = 3 * HY_DIM
    tc = 256
    base = COL_HY * 128 // tc
    return pl.pallas_call(
        functools.partial(_short_conv_kernel, grid=grid),
        grid=(batch, c3 // tc),
        in_specs=[pl.BlockSpec((l, tc), lambda b, j: (b, base + j)),
                  pl.BlockSpec((9, tc), lambda b, j: (0, j)),
                  pl.BlockSpec((1, tc), lambda b, j: (0, j))],
        out_specs=pl.BlockSpec((l, tc), lambda b, j: (b, j)),
        out_shape=jax.ShapeDtypeStruct((batch * l, c3), BF16),
        scratch_shapes=[pltpu.VMEM((l + 2 * CONV_PAD, tc), F32)],
        compiler_params=_cparams(("arbitrary", "arbitrary")),
        name="hyena_short_conv",
    )(p_arr, w9, bias.reshape(1, c3))


def _dft_matrices(l):
    n = 2 * l
    s = int(round(math.sqrt(n)))
    while n % s:
        s -= 1
    q = n // s
    f = np.arange(l, dtype=np.int64)
    ang_a = 2.0 * np.pi * ((f[:, None] * s * np.arange(q)[None, :]) % n) / n
    ang_b = 2.0 * np.pi * ((f[:, None] * np.arange(s)[None, :]) % n) / n
    ca, sa = jnp.asarray(np.cos(ang_a), F32), jnp.asarray(np.sin(ang_a), F32)
    cb, sb = jnp.asarray(np.cos(ang_b), F32), jnp.asarray(np.sin(ang_b), F32)
    cosm = (ca[:, :, None] * cb[:, None, :] - sa[:, :, None] * sb[:, None, :]).reshape(l, n)
    sinm = (sa[:, :, None] * cb[:, None, :] + ca[:, :, None] * sb[:, None, :]).reshape(l, n)
    nyq = jnp.asarray(np.where(np.arange(n) % 2 == 0, 1.0, -1.0), F32)
    imb = jnp.concatenate([nyq[None, :], -sinm[1:]], axis=0)
    h = DFT_HALF
    fwd = jnp.stack([cosm.reshape(l // h, h, n), imb.reshape(l // h, h, n)], axis=1).reshape(n, n)
    scale = np.full((l,), 2.0 / n)
    scale[0] = 1.0 / n
    sc = jnp.asarray(np.stack([scale.reshape(l // h, h), scale.reshape(l // h, h)], axis=1).reshape(n, 1), F32)
    inv = (fwd[:, :l] * sc).T
    return fwd.astype(BF16), inv.astype(BF16)


def _plain_mm_kernel(a_ref, b_ref, o_ref):
    o_ref[...] = _dot(a_ref[...], b_ref[...]).astype(o_ref.dtype)


def _filter_spectrum(fwd, kern_bf):
    n = fwd.shape[0]
    nc = kern_bf.shape[1]
    tm, tn = 2 * DFT_HALF, 1024
    return pl.pallas_call(
        _plain_mm_kernel,
        grid=(nc // tn, n // tm),
        in_specs=[pl.BlockSpec((tm, n), lambda j, i: (i, 0)),
                  pl.BlockSpec((n, tn), lambda j, i: (0, j))],
        out_specs=pl.BlockSpec((tm, tn), lambda j, i: (i, j)),
        out_shape=jax.ShapeDtypeStruct((n, nc), F32),
        compiler_params=_cparams(("arbitrary", "arbitrary")),
        name="hyena_filter_dft",
    )(fwd, kern_bf)


def _dft_fwd_kernel(f_ref, z_ref, k_ref, y_ref):
    h = DFT_HALF
    acc = _dot(f_ref[...], z_ref[...])
    ar, ai = acc[:h], acc[h:]
    kr, km = k_ref[0:h, :], k_ref[h:2 * h, :]
    row0 = (lax.broadcasted_iota(jnp.int32, kr.shape, 0) == 0) & (pl.program_id(1) == 0)
    ki = jnp.where(row0, 0.0, km)
    kr2 = jnp.where(row0, km, kr)
    y_ref[0:h, :] = (ar * kr - ai * ki).astype(y_ref.dtype)
    y_ref[h:2 * h, :] = (ar * ki + ai * kr2).astype(y_ref.dtype)


def _dft_fwd(fwd, z_arr, z_col, kf, k_col, batch):
    n = fwd.shape[0]
    l = n // 2
    tm, tn = 2 * DFT_HALF, HY_DIM
    return pl.pallas_call(
        _dft_fwd_kernel,
        grid=(batch, n // tm),
        in_specs=[pl.BlockSpec((tm, l), lambda b, i: (i, 0)),
                  pl.BlockSpec((l, tn), lambda b, i: (b, z_col)),
                  pl.BlockSpec((tm, tn), lambda b, i: (i, k_col))],
        out_specs=pl.BlockSpec((tm, tn), lambda b, i: (b * (n // tm) + i, 0)),
        out_shape=jax.ShapeDtypeStruct((batch * n, tn), BF16),
        compiler_params=_cparams(("arbitrary", "arbitrary")),
        name="hyena_dft_fwd",
    )(fwd, z_arr, kf)


def _dft_inv_kernel(g_ref, y_ref, gate_ref, z_ref, skip_ref, o_ref):
    conv = _dot(g_ref[...], y_ref[...])
    z = z_ref[...].astype(F32)
    o_ref[...] = (gate_ref[...].astype(F32) * (conv + skip_ref[...] * z)).astype(o_ref.dtype)


def _dft_inv(inv, y, gate_arr, gate_col, z_arr, z_col, skip, batch):
    l, n = inv.shape
    tm, tn = min(512, l), HY_DIM
    rt = l // tm
    return pl.pallas_call(
        _dft_inv_kernel,
        grid=(batch, rt),
        in_specs=[pl.BlockSpec((tm, n), lambda b, i: (i, 0)),
                  pl.BlockSpec((n, tn), lambda b, i: (b, 0)),
                  pl.BlockSpec((tm, tn), lambda b, i: (b * rt + i, gate_col)),
                  pl.BlockSpec((tm, tn), lambda b, i: (b * rt + i, z_col)),
                  pl.BlockSpec((1, tn), lambda b, i: (0, 0))],
        out_specs=pl.BlockSpec((tm, tn), lambda b, i: (b * rt + i, 0)),
        out_shape=jax.ShapeDtypeStruct((batch * l, tn), BF16),
        compiler_params=_cparams(("arbitrary", "arbitrary")),
        name="hyena_dft_inv",
    )(inv, y, gate_arr, z_arr, skip.reshape(1, tn))


def _hyena_filters(l, w1, b1, freq, w2, b2, w3):
    t01 = jnp.linspace(0.0, 1.0, l, dtype=F32)[:, None]
    ang = 2.0 * math.pi * jnp.arange(l, dtype=F32)[:, None] / l
    bands = jnp.linspace(1e-4, HY_BANDS - 1, HY_BANDS, dtype=F32)[None, :]
    z = jnp.concatenate([t01, jnp.cos(bands * ang), -jnp.sin(bands * ang)], axis=-1)
    hp = lax.Precision.HIGHEST
    hdn = jnp.sin(freq[0] * (jnp.dot(z, w1, precision=hp) + b1))
    hdn = jnp.sin(freq[1] * (jnp.dot(hdn, w2, precision=hp) + b2))
    filt = jnp.dot(hdn, w3, precision=hp).reshape(l, HY_ORDER, 2, HY_DIM)
    deltas = jnp.abs(jnp.linspace(math.log(HY_DECAY_TARGET) / HY_DECAY_PCT_LONG,
                                  math.log(HY_DECAY_TARGET) / HY_DECAY_PCT_SHORT, HY_DIM, dtype=F32))
    filt = filt * jnp.exp(-t01[:, :, None, None] * deltas)
    fwd = filt[:, :, 0]
    bwd = jnp.flip(filt[1:, :, 1], axis=0)
    kern = jnp.concatenate([fwd, jnp.zeros((1, HY_ORDER, HY_DIM), F32), bwd], axis=0)
    return kern / jnp.sum(jnp.abs(kern), axis=0, keepdims=True)


def _hyena(p_arr, lp, dft, batch, grid):
    l = p_arr.shape[0] // batch
    fwd, inv = dft
    uc = _short_conv(p_arr, lp['hy_conv_w'].reshape(9, 3 * HY_DIM), lp['hy_conv_b'], batch, grid)
    kern = _hyena_filters(l, lp['hy_ff_w1'], lp['hy_ff_b1'], lp['hy_ff_freq'],
                          lp['hy_ff_w2'], lp['hy_ff_b2'], lp['hy_ff_w3'])
    kf = _filter_spectrum(fwd, kern.reshape(2 * l, HY_ORDER * HY_DIM).astype(BF16))
    z_arr, z_col = uc, 0
    for n in range(HY_ORDER):
        y = _dft_fwd(fwd, z_arr, z_col, kf, n, batch)
        z_arr = _dft_inv(inv, y, uc, n + 1, z_arr, z_col, lp['hy_skip'][n], batch)
        z_col = 0
    return z_arr


def _merge_kernel(yhy_ref, ohg_ref, hgg_ref, yrt_ref, b0_ref, b1_ref, b2_ref, x_ref, mod_ref,
                  nw_ref, phy_ref, phg_ref, prt_ref, wo_ref, lng_ref, lnb_ref, o_ref):
    o = ohg_ref[...]
    yhg = (o * lax.rsqrt(jnp.mean(o * o, axis=-1, keepdims=True) + LN_EPS) * nw_ref[...]
           * _silu(hgg_ref[...].astype(F32)))
    m = (jax.nn.sigmoid(b0_ref[...].astype(F32)) * _dot(yhy_ref[...], phy_ref[...])
         + jax.nn.sigmoid(b1_ref[...].astype(F32)) * _dot(yhg.astype(BF16), phg_ref[...])
         + jax.nn.sigmoid(b2_ref[...].astype(F32)) * _dot(yrt_ref[...], prt_ref[...]))
    t = _dot(m.astype(BF16), wo_ref[...])
    gt1 = mod_ref[0][2:3]
    o_ref[...] = _layer_norm_rows(DN_ALPHA * x_ref[...] + gt1 * t, lng_ref[...], lnb_ref[...])


def _merge(y_hy, o_hg, y_rt, p_arr, x, modp, lp, wbf, rows_per_batch):
    m = x.shape[0]
    tm = min(256, m)
    d = D_MODEL
    mi = _mod_index(rows_per_batch, tm)
    row = lambda i: (i, 0)
    const = lambda i: (0, 0)
    gcol = COL_HG * 128 // d
    bcol = COL_BR * 128 // d
    return pl.pallas_call(
        _merge_kernel,
        grid=(m // tm,),
        in_specs=[pl.BlockSpec((tm, d), row), pl.BlockSpec((tm, d), row),
                  pl.BlockSpec((tm, d), lambda i: (i, gcol)),
                  pl.BlockSpec((tm, 2 * d), row),
                  pl.BlockSpec((tm, d), lambda i: (i, bcol)),
                  pl.BlockSpec((tm, d), lambda i: (i, bcol + 1)),
                  pl.BlockSpec((tm, d), lambda i: (i, bcol + 2)),
                  pl.BlockSpec((tm, d), row),
                  pl.BlockSpec((1, 8, d), lambda i: (mi(i), 0, 0)),
                  pl.BlockSpec((1, d), const),
                  pl.BlockSpec((d, d), const), pl.BlockSpec((d, d), const),
                  pl.BlockSpec((2 * d, d), const), pl.BlockSpec((d, d), const),
                  pl.BlockSpec((1, d), const), pl.BlockSpec((1, d), const)],
        out_specs=pl.BlockSpec((tm, d), row),
        out_shape=jax.ShapeDtypeStruct((m, d), F32),
        compiler_params=_cparams(("arbitrary",)),
        name="merge_out_ln",
    )(y_hy, o_hg, p_arr, y_rt, p_arr, p_arr, p_arr, x, modp, lp['hg_norm_w'].reshape(1, d),
      wbf['p_hy'], wbf['p_hg'], wbf['p_rt'], wbf['w_o'],
      lp['ln1_g'].reshape(1, d), lp['ln1_b'].reshape(1, d))


def _ffn_kernel(x_ref, mod_ref, w1_ref, w3_ref, w2_ref, lng_ref, lnb_ref, o_ref, h_ref, acc_ref):
    k = pl.program_id(1)
    m = mod_ref[0]

    @pl.when(k == 0)
    def _():
        h_ref[...] = (x_ref[...] * (1.0 + m[4:5]) + m[3:4]).astype(BF16)
        acc_ref[...] = jnp.zeros(acc_ref.shape, F32)

    h = h_ref[...]
    u = _silu(_dot(h, w1_ref[...])) * _dot(h, w3_ref[...])
    acc_ref[...] += _dot(u.astype(BF16), w2_ref[...])

    @pl.when(k == pl.num_programs(1) - 1)
    def _():
        o_ref[...] = _layer_norm_rows(DN_ALPHA * x_ref[...] + m[5:6] * acc_ref[...],
                                      lng_ref[...], lnb_ref[...])


def _ffn_dense(x, modp, w1, w3, w2, ln_g, ln_b, rows_per_batch):
    m = x.shape[0]
    d = D_MODEL
    dff = w1.shape[1]
    tm = min(512, m)
    tf = dff // 2
    mi = _mod_index(rows_per_batch, tm)
    return pl.pallas_call(
        _ffn_kernel,
        grid=(m // tm, dff // tf),
        in_specs=[pl.BlockSpec((tm, d), lambda i, k: (i, 0)),
                  pl.BlockSpec((1, 8, d), lambda i, k: (mi(i), 0, 0)),
                  pl.BlockSpec((d, tf), lambda i, k: (0, k)),
                  pl.BlockSpec((d, tf), lambda i, k: (0, k)),
                  pl.BlockSpec((tf, d), lambda i, k: (k, 0)),
                  pl.BlockSpec((1, d), lambda i, k: (0, 0)),
                  pl.BlockSpec((1, d), lambda i, k: (0, 0))],
        out_specs=pl.BlockSpec((tm, d), lambda i, k: (i, 0)),
        out_shape=jax.ShapeDtypeStruct((m, d), F32),
        scratch_shapes=[pltpu.VMEM((tm, d), BF16), pltpu.VMEM((tm, d), F32)],
        compiler_params=_cparams(("arbitrary", "arbitrary")),
        name="ffn_dense_ln",
    )(x, modp, w1, w3, w2, ln_g.reshape(1, d), ln_b.reshape(1, d))


def _router_kernel(x_ref, mod_ref, r_ref, h_ref, lg_ref):
    m = mod_ref[0]
    h = x_ref[...] * (1.0 + m[4:5]) + m[3:4]
    h_ref[...] = h.astype(BF16)
    a1, a2, a3 = _split3(h)
    r1, r2, r3 = _split3(r_ref[...])
    lg_ref[...] = (_dot(a1, r1) + _dot(a1, r2) + _dot(a2, r1)
                   + _dot(a2, r2) + _dot(a1, r3) + _dot(a3, r1))


def _router(x, modp, router_pad, rows_per_batch):
    m = x.shape[0]
    d = D_MODEL
    tm = min(512, m)
    mi = _mod_index(rows_per_batch, tm)
    return pl.pallas_call(
        _router_kernel,
        grid=(m // tm,),
        in_specs=[pl.BlockSpec((tm, d), lambda i: (i, 0)),
                  pl.BlockSpec((1, 8, d), lambda i: (mi(i), 0, 0)),
                  pl.BlockSpec((d, 128), lambda i: (0, 0))],
        out_specs=[pl.BlockSpec((tm, d), lambda i: (i, 0)),
                   pl.BlockSpec((tm, 128), lambda i: (i, 0))],
        out_shape=[jax.ShapeDtypeStruct((m, d), BF16), jax.ShapeDtypeStruct((m, 128), F32)],
        compiler_params=_cparams(("arbitrary",)),
        name="moe_router",
    )(x, modp, router_pad)


def _moe_ffn_kernel(be_ref, nu_ref, x_ref, w1_ref, w3_ref, w2_ref, o_ref, acc_ref):
    j = pl.program_id(0)
    k = pl.program_id(1)

    @pl.when(j < nu_ref[0])
    def _():
        @pl.when(k == 0)
        def _():
            acc_ref[...] = jnp.zeros(acc_ref.shape, F32)

        x = x_ref[...]
        u = _silu(_dot(x, w1_ref[0])) * _dot(x, w3_ref[0])
        acc_ref[...] += _dot(u.astype(BF16), w2_ref[0])

        @pl.when(k == pl.num_programs(1) - 1)
        def _():
            o_ref[...] = acc_ref[...]

    @pl.when((j >= nu_ref[0]) & (k == pl.num_programs(1) - 1))
    def _():
        o_ref[...] = jnp.zeros(o_ref.shape, F32)


def _moe_ffn(xb, block_e, n_used, w1, w3, w2):
    ns, d = xb.shape
    tm = MOE_ROWS
    dex = w1.shape[2]
    tf = 512
    grid_spec = pltpu.PrefetchScalarGridSpec(
        num_scalar_prefetch=2,
        grid=(ns // tm, dex // tf),
        in_specs=[pl.BlockSpec((tm, d), lambda j, k, be, nu: (j, 0)),
                  pl.BlockSpec((1, d, tf), lambda j, k, be, nu: (be[j], 0, k)),
                  pl.BlockSpec((1, d, tf), lambda j, k, be, nu: (be[j], 0, k)),
                  pl.BlockSpec((1, tf, d), lambda j, k, be, nu: (be[j], k, 0))],
        out_specs=pl.BlockSpec((tm, d), lambda j, k, be, nu: (j, 0)),
        scratch_shapes=[pltpu.VMEM((tm, d), F32)])
    return pl.pallas_call(
        _moe_ffn_kernel,
        grid_spec=grid_spec,
        out_shape=jax.ShapeDtypeStruct((ns, d), F32),
        compiler_params=_cparams(("arbitrary", "arbitrary")),
        name="moe_expert_ffn",
    )(block_e, n_used, xb, w1, w3, w2)


def _combine_kernel(x_ref, mod_ref, y0_ref, y1_ref, g_ref, lng_ref, lnb_ref, o_ref):
    m = mod_ref[0]
    g = g_ref[...]
    f = g[:, 0:1] * y0_ref[...] + g[:, 1:2] * y1_ref[...]
    o_ref[...] = _layer_norm_rows(DN_ALPHA * x_ref[...] + m[5:6] * f, lng_ref[...], lnb_ref[...])


def _moe_combine(x, modp, y0, y1, gate_pad, ln_g, ln_b, rows_per_batch):
    m = x.shape[0]
    d = D_MODEL
    tm = min(512, m)
    mi = _mod_index(rows_per_batch, tm)
    row = lambda i: (i, 0)
    return pl.pallas_call(
        _combine_kernel,
        grid=(m // tm,),
        in_specs=[pl.BlockSpec((tm, d), row),
                  pl.BlockSpec((1, 8, d), lambda i: (mi(i), 0, 0)),
                  pl.BlockSpec((tm, d), row), pl.BlockSpec((tm, d), row),
                  pl.BlockSpec((tm, 128), row),
                  pl.BlockSpec((1, d), lambda i: (0, 0)), pl.BlockSpec((1, d), lambda i: (0, 0))],
        out_specs=pl.BlockSpec((tm, d), row),
        out_shape=jax.ShapeDtypeStruct((m, d), F32),
        compiler_params=_cparams(("arbitrary",)),
        name="moe_combine_ln",
    )(x, modp, y0, y1, gate_pad, ln_g.reshape(1, d), ln_b.reshape(1, d))


def _moe(x, modp, router, w1, w3, w2, ln_g, ln_b, rows_per_batch):
    n, d = x.shape
    e = router.shape[1]
    h2, logits = _router(x, modp, jnp.pad(router, ((0, 0), (0, 128 - e))), rows_per_batch)
    top_val, top_idx = lax.top_k(logits[:, :e], TOP_K)
    gate = jax.nn.softmax(top_val, axis=-1)
    flat_e = top_idx.reshape(-1)
    flat_t = jnp.repeat(jnp.arange(n, dtype=jnp.int32), TOP_K)
    order = jnp.argsort(flat_e)
    se, st = flat_e[order], flat_t[order]
    counts = jnp.bincount(flat_e, length=e)
    starts = jnp.cumsum(counts) - counts
    padded = (counts + MOE_ROWS - 1) // MOE_ROWS * MOE_ROWS
    pad_end = jnp.cumsum(padded)
    dest = ((pad_end - padded)[se] + jnp.arange(n * TOP_K) - starts[se]).astype(jnp.int32)
    n_blocks = -(-(n * TOP_K) // MOE_ROWS) + e
    n_slots = n_blocks * MOE_ROWS
    slot_tok = jnp.zeros((n_slots,), jnp.int32).at[dest].set(st)
    pos = jnp.zeros((n * TOP_K,), jnp.int32).at[order].set(dest).reshape(n, TOP_K)
    block_start = jnp.arange(n_blocks) * MOE_ROWS
    block_e = jnp.minimum(jnp.sum(block_start[:, None] >= pad_end[None, :], axis=1), e - 1).astype(jnp.int32)
    n_used = (pad_end[-1] // MOE_ROWS).astype(jnp.int32).reshape(1)
    xb = jnp.take(h2, slot_tok, axis=0)
    yb = _moe_ffn(xb, block_e, n_used, w1, w3, w2)
    y0 = jnp.take(yb, pos[:, 0], axis=0)
    y1 = jnp.take(yb, pos[:, 1], axis=0)
    gate_pad = jnp.pad(gate, ((0, 0), (0, 128 - TOP_K)))
    return _moe_combine(x, modp, y0, y1, gate_pad, ln_g, ln_b, rows_per_batch)


def kernel(x, c, ctx, c_ctx, ada_w, ada_b, w_in, hy_conv_w, hy_conv_b, hy_ff_w1, hy_ff_b1, hy_ff_freq, hy_ff_w2, hy_ff_b2, hy_ff_w3, hy_skip, hg_lb_logits, hg_norm_w, p_hy, p_hg, p_rt, w_o, ln1_g, ln1_b, ln2_g, ln2_b, ffn_w1, ffn_w3, ffn_w2, moe_router, moe_w1, moe_w3, moe_w2):
    batch, l, d = x.shape
    lc = ctx.shape[1]
    assert d == D_MODEL and batch <= 8
    assert l % 512 == 0 and l % GRID_W == 0 and lc % max(RET_CHUNK, DFT_HALF) == 0

    cs = jnp.cumsum(jax.nn.softmax(hg_lb_logits.astype(F32), axis=1), axis=1)
    lower_bounds = cs - cs[:, :1]
    cc = jnp.zeros((16, d), F32).at[:batch].set(c).at[8].set(c_ctx)
    ret_tables = _retention_tables(lc, l)
    dft_lat = _dft_matrices(l)
    dft_ctx = _dft_matrices(lc)

    x_lat = x.reshape(batch * l, d)
    x_ctx = ctx.reshape(batch * lc, d)
    for i in range(DEPTH):
        need_ctx_out = i < DEPTH - 1
        use_moe = i % 2 == 1
        g = i // 2
        lp = {'hy_conv_w': hy_conv_w[i], 'hy_conv_b': hy_conv_b[i], 'hy_ff_w1': hy_ff_w1[i],
              'hy_ff_b1': hy_ff_b1[i], 'hy_ff_freq': hy_ff_freq[i], 'hy_ff_w2': hy_ff_w2[i],
              'hy_ff_b2': hy_ff_b2[i], 'hy_ff_w3': hy_ff_w3[i], 'hy_skip': hy_skip[i],
              'hg_norm_w': hg_norm_w[i], 'ln1_g': ln1_g[i], 'ln1_b': ln1_b[i]}
        wbf = {'p_hy': p_hy[i].astype(BF16), 'p_hg': p_hg[i].astype(BF16),
               'p_rt': p_rt[i].astype(BF16), 'w_o': w_o[i].astype(BF16)}
        mod = _ada(cc, ada_w[i], ada_b[i])
        modp = jnp.pad(mod.reshape(16, 6, d), ((0, 0), (0, 2), (0, 0)))
        w_in_bf = w_in[i].astype(BF16)
        lb = lower_bounds[:, i]
        gl_tab = jnp.pad(jnp.stack([jnp.maximum(jnp.log(lb), -1e30), jnp.log1p(-lb), 1.0 - lb], axis=1),
                         ((0, 0), (0, 5), (0, 0)))

        p_lat = _proj(x_lat, modp, w_in_bf, N_IN_COLS, l)
        p_ctx = _proj(x_ctx, modp, w_in_bf, N_IN_COLS if need_ctx_out else N_STATE_COLS, None)

        o_hg_l, o_hg_c = _gla(p_lat, p_ctx, gl_tab, batch, need_ctx_out)
        y_rt_l, y_rt_c = _retention(p_lat, p_ctx, ret_tables, batch, need_ctx_out)
        y_hy_l = _hyena(p_lat, lp, dft_lat, batch, True)
        x_lat_new = _merge(y_hy_l, o_hg_l, y_rt_l, p_lat, x_lat, modp, lp, wbf, l)
        if need_ctx_out:
            y_hy_c = _hyena(p_ctx, lp, dft_ctx, batch, False)
            x_ctx = _merge(y_hy_c, o_hg_c, y_rt_c, p_ctx, x_ctx, modp, lp, wbf, None)
        x_lat = x_lat_new

        if use_moe:
            w1, w3, w2 = moe_w1[g].astype(BF16), moe_w3[g].astype(BF16), moe_w2[g].astype(BF16)
            if need_ctx_out:
                tok = jnp.concatenate([x_lat, x_ctx], axis=0)
                raise NotImplementedError("MoE layer with a context output is not part of this trunk")
            x_lat = _moe(x_lat, modp, moe_router[g], w1, w3, w2, ln2_g[i], ln2_b[i], l)
        else:
            w1, w3, w2 = ffn_w1[g].astype(BF16), ffn_w3[g].astype(BF16), ffn_w2[g].astype(BF16)
            x_lat = _ffn_dense(x_lat, modp, w1, w3, w2, ln2_g[i], ln2_b[i], l)
            if need_ctx_out:
                x_ctx = _ffn_dense(x_ctx, modp, w1, w3, w2, ln2_g[i], ln2_b[i], None)
    return x_lat.reshape(batch, l, d)
```

```python
import functools
import math

import numpy as np
import jax
import jax.numpy as jnp
from jax import lax
from jax.experimental import pallas as pl
from jax.experimental.pallas import tpu as pltpu

F32 = jnp.float32
BF16 = jnp.bfloat16

D_MODEL = 1024
DEPTH = 2
GRID_W = 64
HY_DIM = 1024
HY_ORDER = 2
HY_BANDS = 16
HY_DECAY_TARGET = 1e-2
HY_DECAY_PCT_SHORT = 0.3
HY_DECAY_PCT_LONG = 1.5
HG_HEADS = 8
HG_HK = 128
RT_HEADS = 4
RT_HK = 256
RT_HV = 512
RT_ROPE_BASE = 10000.0
N_STATE_COLS = 6144
N_IN_COLS = 17408
N_EXPERTS = 8
TOP_K = 2
DN_ALPHA = (2 * DEPTH) ** 0.25
LN_EPS = 1e-5

COL_FF, COL_FB, COL_HI, COL_RK, COL_RV = 0, 8, 16, 24, 32
COL_HQ, COL_HG, COL_RQ, COL_RG, COL_HY, COL_BR = 48, 56, 64, 72, 88, 112

GLA_CHUNK = 128
GLA_LEVELS = 7
GLA_PAD = 8
RET_CHUNK = 128
DFT_HALF = 256
HY_BLOCK = 512
CONV_PAD = 72
MOE_ROWS = 512
VMEM_LIMIT = 56 * 1024 * 1024


def _cparams(sem, flags=None):
    return pltpu.CompilerParams(dimension_semantics=sem, vmem_limit_bytes=VMEM_LIMIT, flags=flags)


def _silu(x):
    return x * jax.nn.sigmoid(x)


def _layer_norm_rows(r, g, b):
    mu = jnp.mean(r, axis=-1, keepdims=True)
    d = r - mu
    var = jnp.mean(d * d, axis=-1, keepdims=True)
    return d * lax.rsqrt(var + LN_EPS) * g + b


def _split3(x):
    h = x.astype(BF16)
    r = x - h.astype(F32)
    m = r.astype(BF16)
    l = (r - m.astype(F32)).astype(BF16)
    return h, m, l


def _dot(a, b):
    return jnp.dot(a, b, preferred_element_type=F32)


def _dot_nt(a, b):
    return lax.dot_general(a, b, (((1,), (1,)), ((), ())), preferred_element_type=F32)


def _dot_tn(a, b):
    return lax.dot_general(a, b, (((0,), (0,)), ((), ())), preferred_element_type=F32)


def _ada_kernel(c_ref, w_ref, b_ref, o_ref):
    s = _silu(c_ref[...])
    o_ref[...] = _dot(s.astype(BF16), w_ref[...].astype(BF16)) + b_ref[...]


def _ada(cc, w, layer, b):
    n = w.shape[2]
    tn = 1024
    return pl.pallas_call(
        _ada_kernel,
        grid=(n // tn,),
        in_specs=[pl.BlockSpec((16, D_MODEL), lambda j: (0, 0)),
                  pl.BlockSpec((None, D_MODEL, tn), lambda j: (layer, 0, j)),
                  pl.BlockSpec((1, tn), lambda j: (0, j))],
        out_specs=pl.BlockSpec((16, tn), lambda j: (0, j)),
        out_shape=jax.ShapeDtypeStruct((16, n), F32),
        compiler_params=_cparams(("arbitrary",)),
        name="ada_mod",
    )(cc, w, b.reshape(1, n))


def _mod_index(rows_per_batch, tm):
    if rows_per_batch is None:
        return lambda i: 8
    return lambda i: (i * tm) // rows_per_batch


def _proj_kernel(x_ref, mod_ref, w_ref, o_ref, h_ref):
    @pl.when(pl.program_id(1) == 0)
    def _():
        m = mod_ref[0]
        h_ref[...] = (x_ref[...] * (1.0 + m[1:2]) + m[0:1]).astype(BF16)

    o_ref[...] = _dot(h_ref[...], w_ref[...].astype(BF16)).astype(o_ref.dtype)


def _proj(x, modp, w, layer, n_cols, rows_per_batch):
    m = x.shape[0]
    tm = min(2048, rows_per_batch or m)
    tn = 1024
    mi = _mod_index(rows_per_batch, tm)
    return pl.pallas_call(
        _proj_kernel,
        grid=(m // tm, n_cols // tn),
        in_specs=[pl.BlockSpec((tm, D_MODEL), lambda i, j: (i, 0)),
                  pl.BlockSpec((1, 8, D_MODEL), lambda i, j: (mi(i), 0, 0)),
                  pl.BlockSpec((None, D_MODEL, tn), lambda i, j: (layer, 0, j))],
        out_specs=pl.BlockSpec((tm, tn), lambda i, j: (i, j)),
        out_shape=jax.ShapeDtypeStruct((m, n_cols), BF16),
        scratch_shapes=[pltpu.VMEM((tm, D_MODEL), BF16)],
        compiler_params=_cparams(("arbitrary", "arbitrary")),
        name="in_proj",
    )(x, modp, w)


def _gla_gates(fl, gl_ref, d):
    a = gl_ref[d, 0:1, :]
    l1p = gl_ref[d, 1:2, :]
    oml = gl_ref[d, 2:3, :]
    x2 = l1p + jnp.minimum(fl, 0.0) - jnp.log1p(jnp.exp(-jnp.abs(fl)))
    log_f = jnp.maximum(a, x2) + jnp.log1p(jnp.exp(-jnp.abs(a - x2)))
    key = oml * jax.nn.sigmoid(-fl)
    return log_f, key


def _gla_cumsum(g, rev):
    c = g.shape[0]
    r = lax.broadcasted_iota(jnp.int32, (c, c), 0)
    u = lax.broadcasted_iota(jnp.int32, (c, c), 1)
    tri = jnp.where((u >= r) if rev else (u <= r), 1.0, 0.0).astype(BF16)
    h, m, l = _split3(g)
    return _dot(tri, h) + _dot(tri, m) + _dot(tri, l)


def _gla_state_update(st, b, kk, vv, rev):
    c = b.shape[0]
    end = 0 if rev else c - 1
    b_end = b[end:end + 1, :]
    khat = (kk * jnp.exp2(b_end - b)).astype(BF16)
    return st * jnp.exp2(b_end) + _dot_tn(vv, khat)


def _gla_level_table():
    c = GLA_CHUNK
    t = np.arange(c)[:, None]
    s = np.arange(c)[None, :]
    top_bit = np.floor(np.log2(np.maximum(t ^ s, 1))).astype(np.int32)
    fwd = np.where(t > s, top_bit, np.where(t == s, GLA_LEVELS, -1))
    return np.stack([fwd, fwd.T]).astype(np.int32)


def _gla_ref_rows(b, bpad_ref, level, rev):
    c = b.shape[0]
    half = 1 << level
    blk = 2 * half
    idx = half if rev else half - 1
    if blk % 8 == 0:
        r = b.reshape(c // blk, blk, HG_HK)[:, idx:idx + 1, :]
        return jnp.broadcast_to(r, (c // blk, blk, HG_HK)).reshape(c, HG_HK)
    m = lax.broadcasted_iota(jnp.int32, (c, HG_HK), 0) % blk
    out = b
    for v in range(blk):
        if v != idx:
            off = GLA_PAD + idx - v
            out = jnp.where(m == v, bpad_ref[off:off + c, :], out)
    return out


def _gla_chunk_out(b, kk, vv, qq, st, rev, lvt, bpad_ref):
    c = GLA_CHUNK
    o = _dot_nt((qq * jnp.exp2(b)).astype(BF16), st.astype(BF16))
    bpad_ref[GLA_PAD:GLA_PAD + c, :] = b
    sc = jnp.zeros((c, c), F32)
    for level in range(GLA_LEVELS):
        e = b - _gla_ref_rows(b, bpad_ref, level, rev)
        p = _dot_nt((qq * jnp.exp2(e)).astype(BF16), (kk * jnp.exp2(-e)).astype(BF16))
        sc = jnp.where(lvt == level, p, sc)
    sc = jnp.where(lvt == GLA_LEVELS, _dot_nt(qq.astype(BF16), kk.astype(BF16)), sc)
    return o + _dot(sc.astype(BF16), vv)


def _gla_kernel(*refs, need_ctx_out, n_lat, n_ctx):
    if need_ctx_out:
        ffl, fbl, il, ql, ffc, fbc, ic, qc, gl_ref, lv_ref, ol_ref, oc_ref = refs[:12]
        bl, kl, bc, kc, qsl, qsc, bpf, bpr = refs[12:]
    else:
        ffl, fbl, il, ql, ffc, fbc, ic, gl_ref, lv_ref, ol_ref = refs[:10]
        bl, kl, bc, kc, qsl, bpf, bpr = refs[10:]
        qc = oc_ref = qsc = None
    c = GLA_CHUNK
    q_scale = HG_HK ** -0.5
    bpf[...] = jnp.zeros(bpf.shape, F32)
    bpr[...] = jnp.zeros(bpr.shape, F32)
    ol_ref[...] = jnp.zeros(ol_ref.shape, F32)
    if need_ctx_out:
        oc_ref[...] = jnp.zeros(oc_ref.shape, F32)

    def chunk_rows(idx):
        return pl.ds(pl.multiple_of(idx * c, c), c)

    def prepare(f_refs, q_ref, b_sc, k_sc, q_sc, n):
        def body(j, carry):
            rows = chunk_rows(j)
            for d in range(2):
                g, kk = _gla_gates(f_refs[d][rows, :].astype(F32), gl_ref, d)
                b_sc[d, rows, :] = _gla_cumsum(g, d == 1) * math.log2(math.e)
                k_sc[d, rows, :] = kk
            if q_ref is not None:
                q_sc[rows, :] = _silu(q_ref[rows, :].astype(F32)) * q_scale
            return carry
        lax.fori_loop(0, n, body, 0, unroll=min(2, n))

    prepare((ffc, fbc), qc, bc, kc, qsc, n_ctx)
    prepare((ffl, fbl), ql, bl, kl, qsl, n_lat)

    def one_dir(d, b_sc, k_sc, i_ref, q_sc, o_ref, idx, st):
        rev = d == 1
        rows = chunk_rows(idx)
        b, kk, vv = b_sc[d, rows, :], k_sc[d, rows, :], i_ref[rows, :]
        if q_sc is not None:
            o_ref[rows, :] += _gla_chunk_out(b, kk, vv, q_sc[rows, :], st, rev, lv_ref[d],
                                             bpr if rev else bpf)
        return _gla_state_update(st, b, kk, vv, rev)

    def ctx_body(j, carry):
        return (one_dir(0, bc, kc, ic, qsc, oc_ref, j, carry[0]),
                one_dir(1, bc, kc, ic, qsc, oc_ref, n_ctx - 1 - j, carry[1]))

    def lat_body(j, carry):
        return (one_dir(0, bl, kl, il, qsl, ol_ref, j, carry[0]),
                one_dir(1, bl, kl, il, qsl, ol_ref, n_lat - 1 - j, carry[1]))

    zero = jnp.zeros((HG_HK, HG_HK), F32)
    carry = lax.fori_loop(0, n_ctx, ctx_body, (zero, zero), unroll=min(2, n_ctx))
    lax.fori_loop(0, n_lat, lat_body, carry, unroll=min(2, n_lat))


def _gla(p_lat, p_ctx, gl_tab, batch, need_ctx_out):
    l = p_lat.shape[0] // batch
    lc = p_ctx.shape[0] // batch
    w = HG_HK
    c = GLA_CHUNK

    def col(base):
        return lambda b, h: (b, base + h)

    lat_specs = [pl.BlockSpec((l, w), col(cb)) for cb in (COL_FF, COL_FB, COL_HI, COL_HQ)]
    ctx_cols = (COL_FF, COL_FB, COL_HI) + ((COL_HQ,) if need_ctx_out else ())
    ctx_specs = [pl.BlockSpec((lc, w), col(cb)) for cb in ctx_cols]
    in_specs = lat_specs + ctx_specs + [pl.BlockSpec((2, 8, w), lambda b, h: (0, 0, h)),
                                        pl.BlockSpec((2, c, c), lambda b, h: (0, 0, 0))]
    args = [p_lat] * 4 + [p_ctx] * len(ctx_cols) + [gl_tab, jnp.asarray(_gla_level_table())]
    out_specs = [pl.BlockSpec((l, w), lambda b, h: (b, h))]
    out_shape = [jax.ShapeDtypeStruct((batch * l, HG_HEADS * w), F32)]
    scratch = [pltpu.VMEM((2, l, w), F32), pltpu.VMEM((2, l, w), F32),
               pltpu.VMEM((2, lc, w), F32), pltpu.VMEM((2, lc, w), F32), pltpu.VMEM((l, w), F32)]
    if need_ctx_out:
        out_specs.append(pl.BlockSpec((lc, w), lambda b, h: (b, h)))
        out_shape.append(jax.ShapeDtypeStruct((batch * lc, HG_HEADS * w), F32))
        scratch.append(pltpu.VMEM((lc, w), F32))
    scratch += [pltpu.VMEM((c + 2 * GLA_PAD, w), F32)] * 2
    outs = pl.pallas_call(
        functools.partial(_gla_kernel, need_ctx_out=need_ctx_out, n_lat=l // c, n_ctx=lc // c),
        grid=(batch, HG_HEADS),
        in_specs=in_specs,
        out_specs=out_specs,
        out_shape=out_shape,
        scratch_shapes=scratch,
        compiler_params=_cparams(("arbitrary", "arbitrary")),
        name="hgrn2_scan",
    )(*args)
    return outs[0], (outs[1] if need_ctx_out else None)


def _rotate(x, cos, sin):
    half = x.shape[-1] // 2
    x1, x2 = x[:, :half], x[:, half:]
    return jnp.concatenate([x1 * cos - x2 * sin, x1 * sin + x2 * cos], axis=-1)


def _ret_kernel(*refs, need_ctx_out, n_lat, n_ctx):
    if need_ctx_out:
        (ql, kl, vl, gl, qc, kc, vc, gc, cos_ref, sin_ref, intra_ref, qk_ref, sd_ref,
         yl_ref, yc_ref, qrl, krl, qrc, krc, ol, oc, s_fwd, s_rev) = refs
    else:
        (ql, kl, vl, gl, kc, vc, cos_ref, sin_ref, intra_ref, qk_ref, sd_ref,
         yl_ref, qrl, krl, krc, ol, s_fwd, s_rev) = refs
        qc = gc = yc_ref = qrc = oc = None
    s_refs = (s_fwd, s_rev)
    c = RET_CHUNK
    lc = n_ctx * c
    k_scale = RT_HK ** -0.5

    def rot_all(src, dst, n, pos0, scale):
        def body(j, carry):
            rows = pl.ds(pl.multiple_of(j * c, c), c)
            prow = pl.ds(pl.multiple_of(pos0 + j * c, c), c)
            x = src[rows, :].astype(F32)
            dst[rows, :] = (_rotate(x, cos_ref[prow, :], sin_ref[prow, :]) * scale).astype(BF16)
            return carry
        lax.fori_loop(0, n, body, 0)

    rot_all(kl, krl, n_lat, lc, k_scale)
    rot_all(kc, krc, n_ctx, 0, k_scale)
    rot_all(ql, qrl, n_lat, lc, 1.0)
    if need_ctx_out:
        rot_all(qc, qrc, n_ctx, 0, 1.0)

    for s_ref in s_refs:
        s_ref[...] = jnp.zeros(s_ref.shape, F32)
    ol[...] = jnp.zeros(ol.shape, F32)
    if need_ctx_out:
        oc[...] = jnp.zeros(oc.shape, F32)

    def update(d, kr, v_ref, rows):
        kd = qk_ref[d, 0, c:2 * c, :]
        sdec = sd_ref[d, 0, 0:1, :]
        kh = (kr[rows, :].astype(F32) * jnp.concatenate([kd, kd], axis=1)).astype(BF16)
        s_ref = s_refs[d]
        s_ref[...] = (s_ref[...] * jnp.concatenate([sdec] * (RT_HV // 128), axis=1)
                      + _dot_tn(kh, v_ref[rows, :]))

    def out_step(d, qr, kr, v_ref, o_ref, rows):
        qd = qk_ref[d, 0, 0:c, :]
        qcb = qr[rows, :]
        sc = _dot_nt(qcb, kr[rows, :]) * intra_ref[d, 0]
        qh = (qcb.astype(F32) * jnp.concatenate([qd, qd], axis=1)).astype(BF16)
        o_ref[rows, :] += (_dot(qh, s_refs[d][...].astype(BF16))
                           + _dot(sc.astype(BF16), v_ref[rows, :]))
        update(d, kr, v_ref, rows)

    def ctx_body(j, carry):
        for d in range(2):
            idx = (n_ctx - 1 - j) if d == 1 else j
            rows = pl.ds(pl.multiple_of(idx * c, c), c)
            if need_ctx_out:
                out_step(d, qrc, krc, vc, oc, rows)
            else:
                update(d, krc, vc, rows)
        return carry

    def lat_body(j, carry):
        for d in range(2):
            idx = (n_lat - 1 - j) if d == 1 else j
            out_step(d, qrl, krl, vl, ol, pl.ds(pl.multiple_of(idx * c, c), c))
        return carry

    lax.fori_loop(0, n_ctx, ctx_body, 0)
    lax.fori_loop(0, n_lat, lat_body, 0)

    def readout(o_ref, g_ref, y_ref, n):
        def body(j, carry):
            rows = pl.ds(pl.multiple_of(j * c, c), c)
            o = o_ref[rows, :]
            y = o * lax.rsqrt(jnp.mean(o * o, axis=-1, keepdims=True) + LN_EPS)
            y_ref[rows, :] = (y * _silu(g_ref[rows, :].astype(F32))).astype(y_ref.dtype)
            return carry
        lax.fori_loop(0, n, body, 0)

    readout(ol, gl, yl_ref, n_lat)
    if need_ctx_out:
        readout(oc, gc, yc_ref, n_ctx)


def _retention_tables(lc, l):
    half = RT_HK // 2
    inv = 1.0 / (RT_ROPE_BASE ** jnp.linspace(0.0, 1.0, half, dtype=F32))
    ang = jnp.arange(lc + l, dtype=F32)[:, None] * inv[None, :]
    j = jnp.arange(2 * RT_HEADS, dtype=F32)
    lg_all = jnp.log1p(-jnp.exp2(-5.0 - j))
    c = RET_CHUNK
    pos = jnp.arange(c, dtype=F32)
    rel = pos[:, None] - pos[None, :]
    intra, qk, sd = [], [], []
    for d in range(2):
        lg = lg_all[d::2]
        m = jnp.where(rel >= 0, jnp.exp(jnp.maximum(rel, 0.0)[None] * lg[:, None, None]), 0.0)
        qdec = jnp.exp((pos + 1.0)[None, :] * lg[:, None])
        kdec = jnp.exp((c - 1.0 - pos)[None, :] * lg[:, None])
        if d == 1:
            m = jnp.swapaxes(m, 1, 2)
            qdec = qdec[:, ::-1]
            kdec = kdec[:, ::-1]
        intra.append(m)
        qk.append(jnp.broadcast_to(jnp.concatenate([qdec, kdec], axis=1)[:, :, None],
                                   (RT_HEADS, 2 * c, 128)))
        sd.append(jnp.broadcast_to(jnp.exp(c * lg)[:, None, None], (RT_HEADS, 8, 128)))
    return (jnp.cos(ang), jnp.sin(ang), jnp.stack(intra), jnp.stack(qk), jnp.stack(sd))


def _retention(p_lat, p_ctx, tables, batch, need_ctx_out):
    l = p_lat.shape[0] // batch
    lc = p_ctx.shape[0] // batch
    cos, sin, intra, qk, sd = tables
    c = RET_CHUNK

    def col(base):
        return lambda b, h: (b, base + h)

    kq, kk_, kv, kg = COL_RQ * 128 // RT_HK, COL_RK * 128 // RT_HK, COL_RV * 128 // RT_HV, COL_RG * 128 // RT_HV
    lat_specs = [pl.BlockSpec((l, RT_HK), col(kq)), pl.BlockSpec((l, RT_HK), col(kk_)),
                 pl.BlockSpec((l, RT_HV), col(kv)), pl.BlockSpec((l, RT_HV), col(kg))]
    if need_ctx_out:
        ctx_specs = [pl.BlockSpec((lc, RT_HK), col(kq)), pl.BlockSpec((lc, RT_HK), col(kk_)),
                     pl.BlockSpec((lc, RT_HV), col(kv)), pl.BlockSpec((lc, RT_HV), col(kg))]
    else:
        ctx_specs = [pl.BlockSpec((lc, RT_HK), col(kk_)), pl.BlockSpec((lc, RT_HV), col(kv))]
    tab_specs = [pl.BlockSpec((lc + l, 128), lambda b, h: (0, 0)),
                 pl.BlockSpec((lc + l, 128), lambda b, h: (0, 0)),
                 pl.BlockSpec((2, 1, c, c), lambda b, h: (0, h, 0, 0)),
                 pl.BlockSpec((2, 1, 2 * c, 128), lambda b, h: (0, h, 0, 0)),
                 pl.BlockSpec((2, 1, 8, 128), lambda b, h: (0, h, 0, 0))]
    args = [p_lat] * 4 + [p_ctx] * len(ctx_specs) + [cos, sin, intra, qk, sd]
    out_specs = [pl.BlockSpec((l, RT_HV), lambda b, h: (b, h))]
    out_shape = [jax.ShapeDtypeStruct((batch * l, RT_HEADS * RT_HV), BF16)]
    scratch = [pltpu.VMEM((l, RT_HK), BF16), pltpu.VMEM((l, RT_HK), BF16)]
    if need_ctx_out:
        out_specs.append(pl.BlockSpec((lc, RT_HV), lambda b, h: (b, h)))
        out_shape.append(jax.ShapeDtypeStruct((batch * lc, RT_HEADS * RT_HV), BF16))
        scratch += [pltpu.VMEM((lc, RT_HK), BF16), pltpu.VMEM((lc, RT_HK), BF16),
                    pltpu.VMEM((l, RT_HV), F32), pltpu.VMEM((lc, RT_HV), F32)]
    else:
        scratch += [pltpu.VMEM((lc, RT_HK), BF16), pltpu.VMEM((l, RT_HV), F32)]
    scratch += [pltpu.VMEM((RT_HK, RT_HV), F32)] * 2
    outs = pl.pallas_call(
        functools.partial(_ret_kernel, need_ctx_out=need_ctx_out, n_lat=l // c, n_ctx=lc // c),
        grid=(batch, RT_HEADS),
        in_specs=lat_specs + ctx_specs + tab_specs,
        out_specs=out_specs,
        out_shape=out_shape,
        scratch_shapes=scratch,
        compiler_params=_cparams(("arbitrary", "arbitrary")),
        name="retention_scan",
    )(*args)
    return outs[0], (outs[1] if need_ctx_out else None)


def _short_conv_kernel(x_ref, w_ref, b_ref, o_ref, xpad_ref, *, grid):
    l, tc = x_ref.shape
    p = CONV_PAD
    zeros = jnp.zeros((p, tc), F32)
    xpad_ref[0:p, :] = zeros
    xpad_ref[p + l:p + l + p, :] = zeros
    xpad_ref[p:p + l, :] = x_ref[...].astype(F32)
    colw = lax.broadcasted_iota(jnp.int32, (l, tc), 0) % GRID_W
    out = jnp.broadcast_to(b_ref[...], (l, tc))
    for dj in range(3):
        acc = None
        for di in (range(3) if grid else (1,)):
            off = p + (di - 1) * GRID_W + (dj - 1)
            t = w_ref[di * 3 + dj:di * 3 + dj + 1, :] * xpad_ref[off:off + l, :]
            acc = t if acc is None else acc + t
        if grid and dj == 0:
            acc = jnp.where(colw >= 1, acc, 0.0)
        if grid and dj == 2:
            acc = jnp.where(colw <= GRID_W - 2, acc, 0.0)
        out = out + acc
    o_ref[...] = out.astype(o_ref.dtype)


def _short_conv(p_arr, w9, bias, batch, grid):
    l = p_arr.shape[0] // batch
    c3 = 3 * HY_DIM
    tc = 256
    base = COL_HY * 128 // tc
    return pl.pallas_call(
        functools.partial(_short_conv_kernel, grid=grid),
        grid=(batch, c3 // tc),
        in_specs=[pl.BlockSpec((l, tc), lambda b, j: (b, base + j)),
                  pl.BlockSpec((9, tc), lambda b, j: (0, j)),
                  pl.BlockSpec((1, tc), lambda b, j: (0, j))],
        out_specs=pl.BlockSpec((l, tc), lambda b, j: (b, j)),
        out_shape=jax.ShapeDtypeStruct((batch * l, c3), BF16),
        scratch_shapes=[pltpu.VMEM((l + 2 * CONV_PAD, tc), F32)],
        compiler_params=_cparams(("arbitrary", "arbitrary")),
        name="hyena_short_conv",
    )(p_arr, w9, bias.reshape(1, c3))


def _dft_matrices(l):
    n = 2 * l
    s = int(round(math.sqrt(n)))
    while n % s:
        s -= 1
    q = n // s
    f = np.arange(l, dtype=np.int64)
    ang_a = 2.0 * np.pi * ((f[:, None] * s * np.arange(q)[None, :]) % n) / n
    ang_b = 2.0 * np.pi * ((f[:, None] * np.arange(s)[None, :]) % n) / n
    t = np.arange(n)
    rep = (t[None, :] // s == np.arange(q)[:, None]).astype(np.float32)
    til = (t[None, :] % s == np.arange(s)[:, None]).astype(np.float32)
    h = DFT_HALF
    row = lambda j: (j, 0)
    const = lambda j: (0, 0)
    fwd, inv = pl.pallas_call(
        _dft_build_kernel,
        grid=(l // h,),
        in_specs=[pl.BlockSpec((h, q), row), pl.BlockSpec((h, q), row),
                  pl.BlockSpec((h, s), row), pl.BlockSpec((h, s), row),
                  pl.BlockSpec((q, n), const), pl.BlockSpec((s, n), const)],
        out_specs=[pl.BlockSpec((1, 2, h, n), lambda j: (j, 0, 0, 0)),
                   pl.BlockSpec((h, n), row)],
        out_shape=[jax.ShapeDtypeStruct((l // h, 2, h, n), BF16),
                   jax.ShapeDtypeStruct((l, n), BF16)],
        compiler_params=_cparams(("arbitrary",)),
        name="dft_build",
    )(jnp.asarray(np.cos(ang_a), F32), jnp.asarray(np.sin(ang_a), F32),
      jnp.asarray(np.cos(ang_b), F32), jnp.asarray(np.sin(ang_b), F32),
      jnp.asarray(rep, BF16), jnp.asarray(til, BF16))
    return fwd.reshape(n, n), inv


def _dft_build_kernel(ca_ref, sa_ref, cb_ref, sb_ref, rep_ref, til_ref, fwd_ref, inv_ref):
    h, n = inv_ref.shape

    def spread(x_ref, m_ref):
        p1, p2, p3 = _split3(x_ref[...])
        m = m_ref[...]
        return _dot(p1, m) + _dot(p2, m) + _dot(p3, m)

    ca, sa = spread(ca_ref, rep_ref), spread(sa_ref, rep_ref)
    cb, sb = spread(cb_ref, til_ref), spread(sb_ref, til_ref)
    cosm = ca * cb - sa * sb
    nsin = -(sa * cb + ca * sb)
    first = pl.program_id(0) == 0
    rowi = lax.broadcasted_iota(jnp.int32, (h, n), 0)
    coli = lax.broadcasted_iota(jnp.int32, (h, n), 1)
    alt_col = (1 - 2 * (coli % 2)).astype(F32)
    fwd_ref[0, 0] = cosm.astype(BF16)
    fwd_ref[0, 1] = jnp.where(first & (rowi == 0), alt_col, nsin).astype(BF16)
    rowg = lax.broadcasted_iota(jnp.int32, (h, h), 0) + pl.program_id(0) * h
    col0 = lax.broadcasted_iota(jnp.int32, (h, h), 1) == 0
    alt_row = (1 - 2 * (rowg % 2)).astype(F32)
    pieces = []
    for jj in range(n // (2 * h)):
        cp = cosm[:, jj * h:(jj + 1) * h] * (2.0 / n)
        ip = nsin[:, jj * h:(jj + 1) * h] * (2.0 / n)
        if jj == 0:
            cp = jnp.where(col0, 1.0 / n, cp)
            ip = jnp.where(col0, alt_row * (1.0 / n), ip)
        pieces += [cp.astype(BF16), ip.astype(BF16)]
    inv_ref[...] = jnp.concatenate(pieces, axis=1)


def _filter_spectrum_kernel(f_ref, top_ref, bot_ref, o_ref):
    h = DFT_HALF
    bl = top_ref.shape[0]
    acc = _dot(f_ref[:, 0:bl], top_ref[...]) + _dot(f_ref[:, bl:2 * bl], bot_ref[...])
    kr, km = acc[:h], acc[h:]
    row0 = (lax.broadcasted_iota(jnp.int32, kr.shape, 0) == 0) & (pl.program_id(2) == 0)
    o_ref[0, 0:h, :] = kr
    o_ref[0, h:2 * h, :] = jnp.where(row0, 0.0, km)
    o_ref[0, 2 * h:3 * h, :] = jnp.where(row0, km, kr)


def _filter_spectrum(fwd, kern_bf, nb):
    n = fwd.shape[0]
    bl = n // 2
    nc = kern_bf.shape[1]
    tm, tn = 2 * DFT_HALF, 1024
    nd = 2 * nb - 1
    wrap = 2 * nb
    return pl.pallas_call(
        _filter_spectrum_kernel,
        grid=(nc // tn, nd, n // tm),
        in_specs=[pl.BlockSpec((tm, n), lambda c, d, t: (t, 0)),
                  pl.BlockSpec((bl, tn), lambda c, d, t: ((d - (nb - 1)) % wrap, c)),
                  pl.BlockSpec((bl, tn), lambda c, d, t: ((d - nb) % wrap, c))],
        out_specs=pl.BlockSpec((1, 3 * DFT_HALF, tn), lambda c, d, t: (d, t, c)),
        out_shape=jax.ShapeDtypeStruct((nd, (n // tm) * 3 * DFT_HALF, nc), F32),
        compiler_params=_cparams(("arbitrary", "arbitrary", "arbitrary")),
        name="hyena_filter_dft",
    )(fwd, kern_bf, kern_bf)


def _dft_fwd_kernel(f_ref, z_ref, k_ref, y_ref):
    h = DFT_HALF
    bl = f_ref.shape[1]
    nb = y_ref.shape[1]
    f = f_ref[...]
    spec = [_dot(f, z_ref[j * bl:(j + 1) * bl, :]) for j in range(nb)]
    for i in range(nb):
        yr = yi = None
        for j in range(nb):
            d = i - j + nb - 1
            kr, ki, kr2 = k_ref[d, 0:h, :], k_ref[d, h:2 * h, :], k_ref[d, 2 * h:3 * h, :]
            ur, ui = spec[j][:h], spec[j][h:]
            tr = ur * kr - ui * ki
            ti = ur * ki + ui * kr2
            yr = tr if yr is None else yr + tr
            yi = ti if yi is None else yi + ti
        y_ref[0, i, 0:h, :] = yr.astype(y_ref.dtype)
        y_ref[0, i, h:2 * h, :] = yi.astype(y_ref.dtype)


def _dft_fwd(fwd, z_arr, z_col, kf, k_col, batch, nb):
    n = fwd.shape[0]
    bl = n // 2
    l = nb * bl
    tm, tn = 2 * DFT_HALF, 512
    cpt = HY_DIM // tn
    nd = kf.shape[0]
    return pl.pallas_call(
        _dft_fwd_kernel,
        grid=(cpt, n // tm, batch),
        in_specs=[pl.BlockSpec((tm, bl), lambda c, t, b: (t, 0)),
                  pl.BlockSpec((l, tn), lambda c, t, b: (b, z_col * cpt + c)),
                  pl.BlockSpec((nd, 3 * DFT_HALF, tn), lambda c, t, b: (0, t, k_col * cpt + c))],
        out_specs=pl.BlockSpec((1, nb, tm, tn), lambda c, t, b: (b, 0, t, c)),
        out_shape=jax.ShapeDtypeStruct((batch, nb, n, HY_DIM), BF16),
        compiler_params=_cparams(("arbitrary", "arbitrary", "arbitrary")),
        name="hyena_dft_fwd",
    )(fwd, z_arr, kf)


def _dft_inv_kernel(g_ref, y_ref, gate_ref, z_ref, skip_ref, o_ref):
    conv = _dot(g_ref[...], y_ref[0, 0])
    z = z_ref[...].astype(F32)
    o_ref[...] = (gate_ref[...].astype(F32) * (conv + skip_ref[...] * z)).astype(o_ref.dtype)


def _dft_inv(inv, y, gate_arr, gate_col, z_arr, z_col, skip, batch, nb):
    bl, n = inv.shape
    tn = HY_DIM
    return pl.pallas_call(
        _dft_inv_kernel,
        grid=(batch, nb),
        in_specs=[pl.BlockSpec((bl, n), lambda b, i: (0, 0)),
                  pl.BlockSpec((1, 1, n, tn), lambda b, i: (b, i, 0, 0)),
                  pl.BlockSpec((bl, tn), lambda b, i: (b * nb + i, gate_col)),
                  pl.BlockSpec((bl, tn), lambda b, i: (b * nb + i, z_col)),
                  pl.BlockSpec((1, tn), lambda b, i: (0, 0))],
        out_specs=pl.BlockSpec((bl, tn), lambda b, i: (b * nb + i, 0)),
        out_shape=jax.ShapeDtypeStruct((batch * nb * bl, tn), BF16),
        compiler_params=_cparams(("arbitrary", "arbitrary")),
        name="hyena_dft_inv",
    )(inv, y, gate_arr, z_arr, skip.reshape(1, tn))


def _tap_features(l):
    f32 = np.float32
    t01 = np.linspace(0.0, 1.0, l, dtype=f32)[:, None]
    ang = f32(2.0 * math.pi) * np.arange(l, dtype=f32)[:, None] / f32(l)
    bands = np.linspace(1e-4, HY_BANDS - 1, HY_BANDS, dtype=f32)[None, :]
    z = np.concatenate([t01, np.cos(bands * ang), -np.sin(bands * ang)], axis=-1).astype(f32)
    pos = np.concatenate([np.arange(l), [0], np.arange(l - 1, 0, -1)])
    zz = np.zeros((2 * l, 128), f32)
    zz[:, :z.shape[1]] = z[pos]
    zz[l] = 0.0
    return zz


def _dot_hi(a, b):
    a1 = a.astype(BF16)
    a2 = (a - a1.astype(F32)).astype(BF16)
    b1 = b.astype(BF16)
    b2 = (b - b1.astype(F32)).astype(BF16)
    return _dot(a1, b1) + _dot(a1, b2) + _dot(a2, b1)


def _filter_kernel(zz_ref, w1_ref, b1_ref, fr_ref, w2_ref, b2_ref, w3f_ref, w3b_ref, dl_ref,
                   o_ref, hdn_ref):
    n = zz_ref.shape[0]
    l = n // 2

    @pl.when((pl.program_id(0) == 0) & (pl.program_id(1) == 0))
    def _():
        h1 = jnp.sin(fr_ref[0:1, :] * (_dot_hi(zz_ref[...], w1_ref[...]) + b1_ref[...]))
        hdn_ref[...] = jnp.sin(fr_ref[1:2, :] * (_dot_hi(h1, w2_ref[...]) + b2_ref[...]))

    filt = jnp.concatenate([_dot_hi(hdn_ref[0:l, :], w3f_ref[...]),
                            _dot_hi(hdn_ref[l:n, :], w3b_ref[...])], axis=0)
    decay = jnp.exp(-zz_ref[:, 0:1] * dl_ref[...])
    rowi = lax.broadcasted_iota(jnp.int32, filt.shape, 0)
    kern = jnp.where(rowi == l, 0.0, filt * decay)
    o_ref[...] = (kern / jnp.sum(jnp.abs(kern), axis=0, keepdims=True)).astype(o_ref.dtype)


def _hyena_filters(l, w1, b1, freq, w2, b2, w3):
    n = 2 * l
    tn = 256
    ff = w2.shape[0]
    cpo = HY_DIM // tn
    deltas = np.abs(np.linspace(math.log(HY_DECAY_TARGET) / HY_DECAY_PCT_LONG,
                                math.log(HY_DECAY_TARGET) / HY_DECAY_PCT_SHORT, HY_DIM,
                                dtype=np.float32)).reshape(1, HY_DIM)
    const = lambda o, c: (0, 0)
    return pl.pallas_call(
        _filter_kernel,
        grid=(HY_ORDER, cpo),
        in_specs=[pl.BlockSpec((n, 128), const), pl.BlockSpec((128, ff), const),
                  pl.BlockSpec((1, ff), const), pl.BlockSpec((2, ff), const),
                  pl.BlockSpec((ff, ff), const), pl.BlockSpec((1, ff), const),
                  pl.BlockSpec((ff, tn), lambda o, c: (0, o * 2 * cpo + c)),
                  pl.BlockSpec((ff, tn), lambda o, c: (0, o * 2 * cpo + cpo + c)),
                  pl.BlockSpec((1, tn), lambda o, c: (0, c))],
        out_specs=pl.BlockSpec((n, tn), lambda o, c: (0, o * cpo + c)),
        out_shape=jax.ShapeDtypeStruct((n, HY_ORDER * HY_DIM), BF16),
        scratch_shapes=[pltpu.VMEM((n, ff), F32)],
        compiler_params=_cparams(("arbitrary", "arbitrary")),
        name="hyena_filter_mlp",
    )(jnp.asarray(_tap_features(l)), jnp.pad(w1, ((0, 128 - w1.shape[0]), (0, 0))),
      b1.reshape(1, ff), freq, w2, b2.reshape(1, ff), w3, w3, jnp.asarray(deltas))


def _hyena(p_arr, lp, dft, batch, grid):
    l = p_arr.shape[0] // batch
    fwd, inv = dft
    uc = _short_conv(p_arr, lp['hy_conv_w'].reshape(9, 3 * HY_DIM), lp['hy_conv_b'], batch, grid)
    kern = _hyena_filters(l, lp['hy_ff_w1'], lp['hy_ff_b1'], lp['hy_ff_freq'],
                          lp['hy_ff_w2'], lp['hy_ff_b2'], lp['hy_ff_w3'])
    nb = l // inv.shape[0]
    kf = _filter_spectrum(fwd, kern, nb)
    z_arr, z_col = uc, 0
    for n in range(HY_ORDER):
        y = _dft_fwd(fwd, z_arr, z_col, kf, n, batch, nb)
        z_arr = _dft_inv(inv, y, uc, n + 1, z_arr, z_col, lp['hy_skip'][n], batch, nb)
        z_col = 0
    return z_arr


def _merge_kernel(yhy_ref, ohg_ref, hgg_ref, yrt_ref, b0_ref, b1_ref, b2_ref, x_ref, mod_ref,
                  nw_ref, phy_ref, phg_ref, prt_ref, wo_ref, lng_ref, lnb_ref, o_ref):
    o = ohg_ref[...]
    yhg = (o * lax.rsqrt(jnp.mean(o * o, axis=-1, keepdims=True) + LN_EPS) * nw_ref[...]
           * _silu(hgg_ref[...].astype(F32)))
    m = (jax.nn.sigmoid(b0_ref[...].astype(F32)) * _dot(yhy_ref[...], phy_ref[...])
         + jax.nn.sigmoid(b1_ref[...].astype(F32)) * _dot(yhg.astype(BF16), phg_ref[...])
         + jax.nn.sigmoid(b2_ref[...].astype(F32)) * _dot(yrt_ref[...], prt_ref[...]))
    t = _dot(m.astype(BF16), wo_ref[...])
    gt1 = mod_ref[0][2:3]
    o_ref[...] = _layer_norm_rows(DN_ALPHA * x_ref[...] + gt1 * t, lng_ref[...], lnb_ref[...])


def _merge(y_hy, o_hg, y_rt, p_arr, x, modp, lp, wbf, rows_per_batch):
    m = x.shape[0]
    tm = min(256, m)
    d = D_MODEL
    mi = _mod_index(rows_per_batch, tm)
    row = lambda i: (i, 0)
    const = lambda i: (0, 0)
    gcol = COL_HG * 128 // d
    bcol = COL_BR * 128 // d
    return pl.pallas_call(
        _merge_kernel,
        grid=(m // tm,),
        in_specs=[pl.BlockSpec((tm, d), row), pl.BlockSpec((tm, d), row),
                  pl.BlockSpec((tm, d), lambda i: (i, gcol)),
                  pl.BlockSpec((tm, 2 * d), row),
                  pl.BlockSpec((tm, d), lambda i: (i, bcol)),
                  pl.BlockSpec((tm, d), lambda i: (i, bcol + 1)),
                  pl.BlockSpec((tm, d), lambda i: (i, bcol + 2)),
                  pl.BlockSpec((tm, d), row),
                  pl.BlockSpec((1, 8, d), lambda i: (mi(i), 0, 0)),
                  pl.BlockSpec((1, d), const),
                  pl.BlockSpec((d, d), const), pl.BlockSpec((d, d), const),
                  pl.BlockSpec((2 * d, d), const), pl.BlockSpec((d, d), const),
                  pl.BlockSpec((1, d), const), pl.BlockSpec((1, d), const)],
        out_specs=pl.BlockSpec((tm, d), row),
        out_shape=jax.ShapeDtypeStruct((m, d), F32),
        compiler_params=_cparams(("arbitrary",)),
        name="merge_out_ln",
    )(y_hy, o_hg, p_arr, y_rt, p_arr, p_arr, p_arr, x, modp, lp['hg_norm_w'].reshape(1, d),
      wbf['p_hy'], wbf['p_hg'], wbf['p_rt'], wbf['w_o'],
      lp['ln1_g'].reshape(1, d), lp['ln1_b'].reshape(1, d))


def _ffn_kernel(x_ref, mod_ref, w1_ref, w3_ref, w2_ref, lng_ref, lnb_ref, o_ref, h_ref, acc_ref):
    k = pl.program_id(1)
    m = mod_ref[0]

    @pl.when(k == 0)
    def _():
        h_ref[...] = (x_ref[...] * (1.0 + m[4:5]) + m[3:4]).astype(BF16)
        acc_ref[...] = jnp.zeros(acc_ref.shape, F32)

    h = h_ref[...]
    u = _silu(_dot(h, w1_ref[...])) * _dot(h, w3_ref[...])
    acc_ref[...] += _dot(u.astype(BF16), w2_ref[...])

    @pl.when(k == pl.num_programs(1) - 1)
    def _():
        o_ref[...] = _layer_norm_rows(DN_ALPHA * x_ref[...] + m[5:6] * acc_ref[...],
                                      lng_ref[...], lnb_ref[...])


def _ffn_dense(x, modp, w1, w3, w2, ln_g, ln_b, rows_per_batch):
    m = x.shape[0]
    d = D_MODEL
    dff = w1.shape[1]
    tm = min(512, m)
    tf = dff // 2
    mi = _mod_index(rows_per_batch, tm)
    return pl.pallas_call(
        _ffn_kernel,
        grid=(m // tm, dff // tf),
        in_specs=[pl.BlockSpec((tm, d), lambda i, k: (i, 0)),
                  pl.BlockSpec((1, 8, d), lambda i, k: (mi(i), 0, 0)),
                  pl.BlockSpec((d, tf), lambda i, k: (0, k)),
                  pl.BlockSpec((d, tf), lambda i, k: (0, k)),
                  pl.BlockSpec((tf, d), lambda i, k: (k, 0)),
                  pl.BlockSpec((1, d), lambda i, k: (0, 0)),
                  pl.BlockSpec((1, d), lambda i, k: (0, 0))],
        out_specs=pl.BlockSpec((tm, d), lambda i, k: (i, 0)),
        out_shape=jax.ShapeDtypeStruct((m, d), F32),
        scratch_shapes=[pltpu.VMEM((tm, d), BF16), pltpu.VMEM((tm, d), F32)],
        compiler_params=_cparams(("arbitrary", "arbitrary")),
        name="ffn_dense_ln",
    )(x, modp, w1, w3, w2, ln_g.reshape(1, d), ln_b.reshape(1, d))


def _router_kernel(x_ref, mod_ref, r_ref, h_ref, lg_ref):
    m = mod_ref[0]
    h = x_ref[...] * (1.0 + m[4:5]) + m[3:4]
    h_ref[...] = h
    a1, a2, a3 = _split3(h)
    r1, r2, r3 = _split3(r_ref[...])
    lg_ref[...] = (_dot(a1, r1) + _dot(a1, r2) + _dot(a2, r1)
                   + _dot(a2, r2) + _dot(a1, r3) + _dot(a3, r1))


def _router(x, modp, router_pad, rows_per_batch):
    m = x.shape[0]
    d = D_MODEL
    tm = min(512, m)
    mi = _mod_index(rows_per_batch, tm)
    return pl.pallas_call(
        _router_kernel,
        grid=(m // tm,),
        in_specs=[pl.BlockSpec((tm, d), lambda i: (i, 0)),
                  pl.BlockSpec((1, 8, d), lambda i: (mi(i), 0, 0)),
                  pl.BlockSpec((d, 128), lambda i: (0, 0))],
        out_specs=[pl.BlockSpec((tm, d), lambda i: (i, 0)),
                   pl.BlockSpec((tm, 128), lambda i: (i, 0))],
        out_shape=[jax.ShapeDtypeStruct((m, d), F32), jax.ShapeDtypeStruct((m, 128), F32)],
        compiler_params=_cparams(("arbitrary",)),
        name="moe_router",
    )(x, modp, router_pad)


def _moe_ffn_kernel(be_ref, nu_ref, xp_ref, w1_ref, w3_ref, w2_ref, o_ref, x_ref, acc_ref):
    j = pl.program_id(0)
    k = pl.program_id(1)

    @pl.when(j < nu_ref[0])
    def _():
        @pl.when(k == 0)
        def _():
            x_ref[...] = xp_ref[...].astype(BF16)
            acc_ref[...] = jnp.zeros(acc_ref.shape, F32)

        x = x_ref[...]
        u = _silu(_dot(x, w1_ref[0])) * _dot(x, w3_ref[0])
        acc_ref[...] += _dot(u.astype(BF16), w2_ref[0])

        @pl.when(k == pl.num_programs(1) - 1)
        def _():
            o_ref[...] = acc_ref[...]

    @pl.when((j >= nu_ref[0]) & (k == pl.num_programs(1) - 1))
    def _():
        o_ref[...] = jnp.zeros(o_ref.shape, F32)


def _moe_ffn(xb, block_e, n_used, w1, w3, w2):
    ns = xb.shape[0]
    d = D_MODEL
    tm = MOE_ROWS
    dex = w1.shape[2]
    tf = dex // 2
    grid_spec = pltpu.PrefetchScalarGridSpec(
        num_scalar_prefetch=2,
        grid=(ns // tm, dex // tf),
        in_specs=[pl.BlockSpec((tm, d), lambda j, k, be, nu: (j, 0)),
                  pl.BlockSpec((1, d, tf), lambda j, k, be, nu: (be[j], 0, k)),
                  pl.BlockSpec((1, d, tf), lambda j, k, be, nu: (be[j], 0, k)),
                  pl.BlockSpec((1, tf, d), lambda j, k, be, nu: (be[j], k, 0))],
        out_specs=pl.BlockSpec((tm, d), lambda j, k, be, nu: (j, 0)),
        scratch_shapes=[pltpu.VMEM((tm, d), BF16), pltpu.VMEM((tm, d), F32)])
    return pl.pallas_call(
        _moe_ffn_kernel,
        grid_spec=grid_spec,
        out_shape=jax.ShapeDtypeStruct((ns, d), F32),
        compiler_params=_cparams(("arbitrary", "arbitrary")),
        name="moe_expert_ffn",
    )(block_e, n_used, xb, w1, w3, w2)


def _combine_kernel(x_ref, mod_ref, y0_ref, y1_ref, g_ref, lng_ref, lnb_ref, o_ref):
    m = mod_ref[0]
    g = g_ref[...]
    f = g[:, 0:1] * y0_ref[...] + g[:, 1:2] * y1_ref[...]
    o_ref[...] = _layer_norm_rows(DN_ALPHA * x_ref[...] + m[5:6] * f, lng_ref[...], lnb_ref[...])


def _moe_combine(x, modp, y0, y1, gate_pad, ln_g, ln_b, rows_per_batch):
    m = x.shape[0]
    d = D_MODEL
    tm = min(512, m)
    mi = _mod_index(rows_per_batch, tm)
    row = lambda i: (i, 0)
    return pl.pallas_call(
        _combine_kernel,
        grid=(m // tm,),
        in_specs=[pl.BlockSpec((tm, d), row),
                  pl.BlockSpec((1, 8, d), lambda i: (mi(i), 0, 0)),
                  pl.BlockSpec((tm, d), row), pl.BlockSpec((tm, d), row),
                  pl.BlockSpec((tm, 128), row),
                  pl.BlockSpec((1, d), lambda i: (0, 0)), pl.BlockSpec((1, d), lambda i: (0, 0))],
        out_specs=pl.BlockSpec((tm, d), row),
        out_shape=jax.ShapeDtypeStruct((m, d), F32),
        compiler_params=_cparams(("arbitrary",)),
        name="moe_combine_ln",
    )(x, modp, y0, y1, gate_pad, ln_g.reshape(1, d), ln_b.reshape(1, d))


def _moe(x, modp, router, w1, w3, w2, ln_g, ln_b, rows_per_batch):
    n, d = x.shape
    e = router.shape[1]
    h2, logits = _router(x, modp, jnp.pad(router, ((0, 0), (0, 128 - e))), rows_per_batch)
    top_val, top_idx = lax.top_k(logits[:, :e], TOP_K)
    gate = jax.nn.softmax(top_val, axis=-1)
    flat_e = top_idx.reshape(-1)
    flat_t = jnp.repeat(jnp.arange(n, dtype=jnp.int32), TOP_K)
    order = jnp.argsort(flat_e, stable=True).astype(jnp.int32)
    rank = jnp.argsort(order).astype(jnp.int32)
    counts = jnp.sum((flat_e[:, None] == jnp.arange(e)[None, :]).astype(jnp.int32), axis=0)
    starts = jnp.cumsum(counts) - counts
    padded = (counts + MOE_ROWS - 1) // MOE_ROWS * MOE_ROWS
    pad_end = jnp.cumsum(padded)
    pad_start = pad_end - padded
    n_blocks = -(-(n * TOP_K) // MOE_ROWS) + e
    n_slots = n_blocks * MOE_ROWS
    block_start = jnp.arange(n_blocks) * MOE_ROWS
    block_e = jnp.minimum(jnp.sum(block_start[:, None] >= pad_end[None, :], axis=1), e - 1).astype(jnp.int32)
    n_used = (pad_end[-1] // MOE_ROWS).astype(jnp.int32).reshape(1)
    slot_e = jnp.repeat(block_e, MOE_ROWS)
    slot_off = jnp.arange(n_slots, dtype=jnp.int32) - pad_start[slot_e]
    slot_valid = slot_off < counts[slot_e]
    slot_src = jnp.where(slot_valid, starts[slot_e] + slot_off, 0)
    slot_tok = jnp.where(slot_valid, flat_t[order[slot_src]], 0).astype(jnp.int32)
    pos = (pad_start[flat_e] + rank - starts[flat_e]).astype(jnp.int32).reshape(n, TOP_K)
    xb = jnp.take(h2, slot_tok, axis=0, mode="clip")
    yb = _moe_ffn(xb, block_e, n_used, w1, w3, w2)
    y0 = jnp.take(yb, pos[:, 0], axis=0, mode="clip")
    y1 = jnp.take(yb, pos[:, 1], axis=0, mode="clip")
    gate_pad = jnp.pad(gate, ((0, 0), (0, 128 - TOP_K)))
    return _moe_combine(x, modp, y0, y1, gate_pad, ln_g, ln_b, rows_per_batch)


def kernel(x, c, ctx, c_ctx, ada_w, ada_b, w_in, hy_conv_w, hy_conv_b, hy_ff_w1, hy_ff_b1, hy_ff_freq, hy_ff_w2, hy_ff_b2, hy_ff_w3, hy_skip, hg_lb_logits, hg_norm_w, p_hy, p_hg, p_rt, w_o, ln1_g, ln1_b, ln2_g, ln2_b, ffn_w1, ffn_w3, ffn_w2, moe_router, moe_w1, moe_w3, moe_w2):
    batch, l, d = x.shape
    lc = ctx.shape[1]
    assert d == D_MODEL and batch <= 8
    assert l % 512 == 0 and l % GRID_W == 0 and lc % max(RET_CHUNK, DFT_HALF) == 0

    cs = jnp.cumsum(jax.nn.softmax(hg_lb_logits.astype(F32), axis=1), axis=1)
    lower_bounds = cs - cs[:, :1]
    cc = jnp.zeros((16, d), F32).at[:batch].set(c).at[8].set(c_ctx)
    ret_tables = _retention_tables(lc, l)
    dft_lat = _dft_matrices(min(HY_BLOCK, l))
    dft_ctx = _dft_matrices(min(HY_BLOCK, lc))

    x_lat = x.reshape(batch * l, d)
    x_ctx = ctx.reshape(batch * lc, d)
    for i in range(DEPTH):
        need_ctx_out = i < DEPTH - 1
        use_moe = i % 2 == 1
        g = i // 2
        lp = {'hy_conv_w': hy_conv_w[i], 'hy_conv_b': hy_conv_b[i], 'hy_ff_w1': hy_ff_w1[i],
              'hy_ff_b1': hy_ff_b1[i], 'hy_ff_freq': hy_ff_freq[i], 'hy_ff_w2': hy_ff_w2[i],
              'hy_ff_b2': hy_ff_b2[i], 'hy_ff_w3': hy_ff_w3[i], 'hy_skip': hy_skip[i],
              'hg_norm_w': hg_norm_w[i], 'ln1_g': ln1_g[i], 'ln1_b': ln1_b[i]}
        wbf = {'p_hy': p_hy[i].astype(BF16), 'p_hg': p_hg[i].astype(BF16),
               'p_rt': p_rt[i].astype(BF16), 'w_o': w_o[i].astype(BF16)}
        mod = _ada(cc, ada_w, i, ada_b[i])
        modp = jnp.pad(mod.reshape(16, 6, d), ((0, 0), (0, 2), (0, 0)))
        lb = lower_bounds[:, i]
        gl_tab = jnp.pad(jnp.stack([jnp.maximum(jnp.log(lb), -1e30), jnp.log1p(-lb), 1.0 - lb], axis=1),
                         ((0, 0), (0, 5), (0, 0)))

        p_lat = _proj(x_lat, modp, w_in, i, N_IN_COLS, l)
        p_ctx = _proj(x_ctx, modp, w_in, i, N_IN_COLS if need_ctx_out else N_STATE_COLS, None)

        o_hg_l, o_hg_c = _gla(p_lat, p_ctx, gl_tab, batch, need_ctx_out)
        y_rt_l, y_rt_c = _retention(p_lat, p_ctx, ret_tables, batch, need_ctx_out)
        y_hy_l = _hyena(p_lat, lp, dft_lat, batch, True)
        x_lat_new = _merge(y_hy_l, o_hg_l, y_rt_l, p_lat, x_lat, modp, lp, wbf, l)
        if need_ctx_out:
            y_hy_c = _hyena(p_ctx, lp, dft_ctx, batch, False)
            x_ctx = _merge(y_hy_c, o_hg_c, y_rt_c, p_ctx, x_ctx, modp, lp, wbf, None)
        x_lat = x_lat_new

        if use_moe:
            w1, w3, w2 = moe_w1[g].astype(BF16), moe_w3[g].astype(BF16), moe_w2[g].astype(BF16)
            if need_ctx_out:
                tok = jnp.concatenate([x_lat, x_ctx], axis=0)
                raise NotImplementedError("MoE layer with a context output is not part of this trunk")
            x_lat = _moe(x_lat, modp, moe_router[g], w1, w3, w2, ln2_g[i], ln2_b[i], l)
        else:
            w1, w3, w2 = ffn_w1[g].astype(BF16), ffn_w3[g].astype(BF16), ffn_w2[g].astype(BF16)
            x_lat = _ffn_dense(x_lat, modp, w1, w3, w2, ln2_g[i], ln2_b[i], l)
            if need_ctx_out:
                x_ctx = _ffn_dense(x_ctx, modp, w1, w3, w2, ln2_g[i], ln2_b[i], None)
    return x_lat.reshape(batch, l, d)
```

```python
import functools
import math

import numpy as np
import jax
import jax.numpy as jnp
from jax import lax
from jax.experimental import pallas as pl
from jax.experimental.pallas import tpu as pltpu

F32 = jnp.float32
BF16 = jnp.bfloat16

D_MODEL = 1024
DEPTH = 2
GRID_W = 64
HY_DIM = 1024
HY_ORDER = 2
HY_BANDS = 16
HY_DECAY_TARGET = 1e-2
HY_DECAY_PCT_SHORT = 0.3
HY_DECAY_PCT_LONG = 1.5
HG_HEADS = 8
HG_HK = 128
RT_HEADS = 4
RT_HK = 256
RT_HV = 512
RT_ROPE_BASE = 10000.0
N_STATE_COLS = 6144
N_IN_COLS = 17408
N_EXPERTS = 8
TOP_K = 2
DN_ALPHA = (2 * DEPTH) ** 0.25
LN_EPS = 1e-5

COL_FF, COL_FB, COL_HI, COL_RK, COL_RV = 0, 8, 16, 24, 32
COL_HQ, COL_HG, COL_RQ, COL_RG, COL_HY, COL_BR = 48, 56, 64, 72, 88, 112

GLA_CHUNK = 128
GLA_LEVELS = 7
GLA_PAD = 8
RET_CHUNK = 128
DFT_HALF = 256
HY_BLOCK = 512
CONV_PAD = 72
MOE_ROWS = 512
VMEM_LIMIT = 56 * 1024 * 1024


def _cparams(sem, flags=None):
    return pltpu.CompilerParams(dimension_semantics=sem, vmem_limit_bytes=VMEM_LIMIT, flags=flags)


def _silu(x):
    return x * jax.nn.sigmoid(x)


def _layer_norm_rows(r, g, b):
    mu = jnp.mean(r, axis=-1, keepdims=True)
    d = r - mu
    var = jnp.mean(d * d, axis=-1, keepdims=True)
    return d * lax.rsqrt(var + LN_EPS) * g + b


def _split3(x):
    h = x.astype(BF16)
    r = x - h.astype(F32)
    m = r.astype(BF16)
    l = (r - m.astype(F32)).astype(BF16)
    return h, m, l


def _dot(a, b):
    return jnp.dot(a, b, preferred_element_type=F32)


def _dot_nt(a, b):
    return lax.dot_general(a, b, (((1,), (1,)), ((), ())), preferred_element_type=F32)


def _dot_tn(a, b):
    return lax.dot_general(a, b, (((0,), (0,)), ((), ())), preferred_element_type=F32)


def _ada_kernel(c_ref, w_ref, b_ref, o_ref):
    s = _silu(c_ref[...])
    o_ref[...] = _dot(s.astype(BF16), w_ref[...].astype(BF16)) + b_ref[...]


def _ada(cc, w, layer, b):
    n = w.shape[2]
    tn = 1024
    return pl.pallas_call(
        _ada_kernel,
        grid=(n // tn,),
        in_specs=[pl.BlockSpec((16, D_MODEL), lambda j: (0, 0)),
                  pl.BlockSpec((None, D_MODEL, tn), lambda j: (layer, 0, j)),
                  pl.BlockSpec((1, tn), lambda j: (0, j))],
        out_specs=pl.BlockSpec((16, tn), lambda j: (0, j)),
        out_shape=jax.ShapeDtypeStruct((16, n), F32),
        compiler_params=_cparams(("arbitrary",)),
        name="ada_mod",
    )(cc, w, b.reshape(1, n))


def _mod_index(rows_per_batch, tm):
    if rows_per_batch is None:
        return lambda i: 8
    return lambda i: (i * tm) // rows_per_batch


def _proj_kernel(x_ref, mod_ref, w_ref, o_ref, h_ref):
    @pl.when(pl.program_id(1) == 0)
    def _():
        m = mod_ref[0]
        h_ref[...] = (x_ref[...] * (1.0 + m[1:2]) + m[0:1]).astype(BF16)

    o_ref[...] = _dot(h_ref[...], w_ref[...].astype(BF16)).astype(o_ref.dtype)


def _proj(x, modp, w, layer, n_cols, rows_per_batch):
    m = x.shape[0]
    tm = min(2048, rows_per_batch or m)
    tn = 1024
    mi = _mod_index(rows_per_batch, tm)
    return pl.pallas_call(
        _proj_kernel,
        grid=(m // tm, n_cols // tn),
        in_specs=[pl.BlockSpec((tm, D_MODEL), lambda i, j: (i, 0)),
                  pl.BlockSpec((1, 8, D_MODEL), lambda i, j: (mi(i), 0, 0)),
                  pl.BlockSpec((None, D_MODEL, tn), lambda i, j: (layer, 0, j))],
        out_specs=pl.BlockSpec((tm, tn), lambda i, j: (i, j)),
        out_shape=jax.ShapeDtypeStruct((m, n_cols), BF16),
        scratch_shapes=[pltpu.VMEM((tm, D_MODEL), BF16)],
        compiler_params=_cparams(("arbitrary", "arbitrary")),
        name="in_proj",
    )(x, modp, w)


def _gla_gates(fl, gl_ref, d):
    a = gl_ref[d, 0:1, :]
    l1p = gl_ref[d, 1:2, :]
    oml = gl_ref[d, 2:3, :]
    t = jnp.exp(-jnp.abs(fl))
    u = 1.0 + t
    x2 = l1p + jnp.minimum(fl, 0.0) - jnp.log(u)
    log_f = jnp.maximum(a, x2) + jnp.log(1.0 + jnp.exp(-jnp.abs(a - x2)))
    key = oml * jnp.where(fl >= 0.0, t, 1.0) / u
    return log_f, key


def _gla_cumsum(g, rev):
    c = g.shape[0]
    r = lax.broadcasted_iota(jnp.int32, (c, c), 0)
    u = lax.broadcasted_iota(jnp.int32, (c, c), 1)
    tri = jnp.where((u >= r) if rev else (u <= r), 1.0, 0.0).astype(BF16)
    h, m, l = _split3(g)
    return _dot(tri, h) + _dot(tri, m) + _dot(tri, l)


def _gla_state_update(st, b, kk, vv, rev):
    c = b.shape[0]
    end = 0 if rev else c - 1
    b_end = b[end:end + 1, :]
    khat = (kk * jnp.exp2(b_end - b)).astype(BF16)
    return st * jnp.exp2(b_end) + _dot_tn(vv, khat)


def _gla_level_table():
    c = GLA_CHUNK
    t = np.arange(c)[:, None]
    s = np.arange(c)[None, :]
    top_bit = np.floor(np.log2(np.maximum(t ^ s, 1))).astype(np.int32)
    fwd = np.where(t > s, top_bit, np.where(t == s, GLA_LEVELS, -1))
    return np.stack([fwd, fwd.T]).astype(np.int32)


def _gla_ref_rows(b, bpad_ref, level, rev):
    c = b.shape[0]
    half = 1 << level
    blk = 2 * half
    idx = half if rev else half - 1
    if blk % 8 == 0:
        r = b.reshape(c // blk, blk, HG_HK)[:, idx:idx + 1, :]
        return jnp.broadcast_to(r, (c // blk, blk, HG_HK)).reshape(c, HG_HK)
    m = lax.broadcasted_iota(jnp.int32, (c, HG_HK), 0) % blk
    out = b
    for v in range(blk):
        if v != idx:
            off = GLA_PAD + idx - v
            out = jnp.where(m == v, bpad_ref[off:off + c, :], out)
    return out


def _gla_chunk_out(b, kk, vv, qq, st, rev, lvt, bpad_ref):
    c = GLA_CHUNK
    o = _dot_nt((qq * jnp.exp2(b)).astype(BF16), st.astype(BF16))
    bpad_ref[GLA_PAD:GLA_PAD + c, :] = b
    sc = jnp.zeros((c, c), F32)
    for level in range(GLA_LEVELS):
        e = b - _gla_ref_rows(b, bpad_ref, level, rev)
        p = _dot_nt((qq * jnp.exp2(e)).astype(BF16), (kk * jnp.exp2(-e)).astype(BF16))
        sc = jnp.where(lvt == level, p, sc)
    sc = jnp.where(lvt == GLA_LEVELS, _dot_nt(qq.astype(BF16), kk.astype(BF16)), sc)
    return o + _dot(sc.astype(BF16), vv)


def _gla_kernel(*refs, need_ctx_out, n_lat, n_ctx):
    if need_ctx_out:
        ffl, fbl, il, ql, ffc, fbc, ic, qc, gl_ref, lv_ref, ol_ref, oc_ref = refs[:12]
        bl, kl, bc, kc, qsl, qsc, bpf, bpr = refs[12:]
    else:
        ffl, fbl, il, ql, ffc, fbc, ic, gl_ref, lv_ref, ol_ref = refs[:10]
        bl, kl, bc, kc, qsl, bpf, bpr = refs[10:]
        qc = oc_ref = qsc = None
    c = GLA_CHUNK
    q_scale = HG_HK ** -0.5
    bpf[...] = jnp.zeros(bpf.shape, F32)
    bpr[...] = jnp.zeros(bpr.shape, F32)
    ol_ref[...] = jnp.zeros(ol_ref.shape, F32)
    if need_ctx_out:
        oc_ref[...] = jnp.zeros(oc_ref.shape, F32)

    def chunk_rows(idx):
        return pl.ds(pl.multiple_of(idx * c, c), c)

    def prepare(f_refs, q_ref, b_sc, k_sc, q_sc, n):
        def body(j, carry):
            rows = chunk_rows(j)
            for d in range(2):
                g, kk = _gla_gates(f_refs[d][rows, :].astype(F32), gl_ref, d)
                b_sc[d, rows, :] = _gla_cumsum(g, d == 1) * math.log2(math.e)
                k_sc[d, rows, :] = kk
            if q_ref is not None:
                q_sc[rows, :] = _silu(q_ref[rows, :].astype(F32)) * q_scale
            return carry
        lax.fori_loop(0, n, body, 0, unroll=min(2, n))

    prepare((ffc, fbc), qc, bc, kc, qsc, n_ctx)
    prepare((ffl, fbl), ql, bl, kl, qsl, n_lat)

    def one_dir(d, b_sc, k_sc, i_ref, q_sc, o_ref, idx, st):
        rev = d == 1
        rows = chunk_rows(idx)
        b, kk, vv = b_sc[d, rows, :], k_sc[d, rows, :], i_ref[rows, :]
        if q_sc is not None:
            o_ref[rows, :] += _gla_chunk_out(b, kk, vv, q_sc[rows, :], st, rev, lv_ref[d],
                                             bpr if rev else bpf)
        return _gla_state_update(st, b, kk, vv, rev)

    def ctx_body(j, carry):
        return (one_dir(0, bc, kc, ic, qsc, oc_ref, j, carry[0]),
                one_dir(1, bc, kc, ic, qsc, oc_ref, n_ctx - 1 - j, carry[1]))

    def lat_body(j, carry):
        return (one_dir(0, bl, kl, il, qsl, ol_ref, j, carry[0]),
                one_dir(1, bl, kl, il, qsl, ol_ref, n_lat - 1 - j, carry[1]))

    zero = jnp.zeros((HG_HK, HG_HK), F32)
    carry = lax.fori_loop(0, n_ctx, ctx_body, (zero, zero), unroll=min(2, n_ctx))
    lax.fori_loop(0, n_lat, lat_body, carry, unroll=min(2, n_lat))


def _gla(p_lat, p_ctx, gl_tab, batch, need_ctx_out):
    l = p_lat.shape[0] // batch
    lc = p_ctx.shape[0] // batch
    w = HG_HK
    c = GLA_CHUNK

    def col(base):
        return lambda b, h: (b, base + h)

    lat_specs = [pl.BlockSpec((l, w), col(cb)) for cb in (COL_FF, COL_FB, COL_HI, COL_HQ)]
    ctx_cols = (COL_FF, COL_FB, COL_HI) + ((COL_HQ,) if need_ctx_out else ())
    ctx_specs = [pl.BlockSpec((lc, w), col(cb)) for cb in ctx_cols]
    in_specs = lat_specs + ctx_specs + [pl.BlockSpec((2, 8, w), lambda b, h: (0, 0, h)),
                                        pl.BlockSpec((2, c, c), lambda b, h: (0, 0, 0))]
    args = [p_lat] * 4 + [p_ctx] * len(ctx_cols) + [gl_tab, jnp.asarray(_gla_level_table())]
    out_specs = [pl.BlockSpec((l, w), lambda b, h: (b, h))]
    out_shape = [jax.ShapeDtypeStruct((batch * l, HG_HEADS * w), F32)]
    scratch = [pltpu.VMEM((2, l, w), F32), pltpu.VMEM((2, l, w), F32),
               pltpu.VMEM((2, lc, w), F32), pltpu.VMEM((2, lc, w), F32), pltpu.VMEM((l, w), F32)]
    if need_ctx_out:
        out_specs.append(pl.BlockSpec((lc, w), lambda b, h: (b, h)))
        out_shape.append(jax.ShapeDtypeStruct((batch * lc, HG_HEADS * w), F32))
        scratch.append(pltpu.VMEM((lc, w), F32))
    scratch += [pltpu.VMEM((c + 2 * GLA_PAD, w), F32)] * 2
    outs = pl.pallas_call(
        functools.partial(_gla_kernel, need_ctx_out=need_ctx_out, n_lat=l // c, n_ctx=lc // c),
        grid=(batch, HG_HEADS),
        in_specs=in_specs,
        out_specs=out_specs,
        out_shape=out_shape,
        scratch_shapes=scratch,
        compiler_params=_cparams(("arbitrary", "arbitrary")),
        name="hgrn2_scan",
    )(*args)
    return outs[0], (outs[1] if need_ctx_out else None)


def _rotate(x, cos, sin):
    half = x.shape[-1] // 2
    x1, x2 = x[:, :half], x[:, half:]
    return jnp.concatenate([x1 * cos - x2 * sin, x1 * sin + x2 * cos], axis=-1)


def _ret_kernel(*refs, need_ctx_out, n_lat, n_ctx):
    if need_ctx_out:
        (ql, kl, vl, gl, qc, kc, vc, gc, cos_ref, sin_ref, intra_ref, qk_ref, sd_ref,
         yl_ref, yc_ref, qrl, krl, qrc, krc, ol, oc, s_fwd, s_rev) = refs
    else:
        (ql, kl, vl, gl, kc, vc, cos_ref, sin_ref, intra_ref, qk_ref, sd_ref,
         yl_ref, qrl, krl, krc, ol, s_fwd, s_rev) = refs
        qc = gc = yc_ref = qrc = oc = None
    s_refs = (s_fwd, s_rev)
    c = RET_CHUNK
    lc = n_ctx * c
    k_scale = RT_HK ** -0.5

    def rot_all(src, dst, n, pos0, scale):
        def body(j, carry):
            rows = pl.ds(pl.multiple_of(j * c, c), c)
            prow = pl.ds(pl.multiple_of(pos0 + j * c, c), c)
            x = src[rows, :].astype(F32)
            dst[rows, :] = (_rotate(x, cos_ref[prow, :], sin_ref[prow, :]) * scale).astype(BF16)
            return carry
        lax.fori_loop(0, n, body, 0)

    rot_all(kl, krl, n_lat, lc, k_scale)
    rot_all(kc, krc, n_ctx, 0, k_scale)
    rot_all(ql, qrl, n_lat, lc, 1.0)
    if need_ctx_out:
        rot_all(qc, qrc, n_ctx, 0, 1.0)

    for s_ref in s_refs:
        s_ref[...] = jnp.zeros(s_ref.shape, F32)
    ol[...] = jnp.zeros(ol.shape, F32)
    if need_ctx_out:
        oc[...] = jnp.zeros(oc.shape, F32)

    def update(d, kr, v_ref, rows):
        kd = qk_ref[d, 0, c:2 * c, :]
        sdec = sd_ref[d, 0, 0:1, :]
        kh = (kr[rows, :].astype(F32) * jnp.concatenate([kd, kd], axis=1)).astype(BF16)
        s_ref = s_refs[d]
        s_ref[...] = (s_ref[...] * jnp.concatenate([sdec] * (RT_HV // 128), axis=1)
                      + _dot_tn(kh, v_ref[rows, :]))

    def out_step(d, qr, kr, v_ref, o_ref, rows):
        qd = qk_ref[d, 0, 0:c, :]
        qcb = qr[rows, :]
        sc = _dot_nt(qcb, kr[rows, :]) * intra_ref[d, 0]
        qh = (qcb.astype(F32) * jnp.concatenate([qd, qd], axis=1)).astype(BF16)
        o_ref[rows, :] += (_dot(qh, s_refs[d][...].astype(BF16))
                           + _dot(sc.astype(BF16), v_ref[rows, :]))
        update(d, kr, v_ref, rows)

    def ctx_body(j, carry):
        for d in range(2):
            idx = (n_ctx - 1 - j) if d == 1 else j
            rows = pl.ds(pl.multiple_of(idx * c, c), c)
            if need_ctx_out:
                out_step(d, qrc, krc, vc, oc, rows)
            else:
                update(d, krc, vc, rows)
        return carry

    def lat_body(j, carry):
        for d in range(2):
            idx = (n_lat - 1 - j) if d == 1 else j
            out_step(d, qrl, krl, vl, ol, pl.ds(pl.multiple_of(idx * c, c), c))
        return carry

    lax.fori_loop(0, n_ctx, ctx_body, 0)
    lax.fori_loop(0, n_lat, lat_body, 0, unroll=2)

    def readout(o_ref, g_ref, y_ref, n):
        def body(j, carry):
            rows = pl.ds(pl.multiple_of(j * c, c), c)
            o = o_ref[rows, :]
            y = o * lax.rsqrt(jnp.mean(o * o, axis=-1, keepdims=True) + LN_EPS)
            y_ref[rows, :] = (y * _silu(g_ref[rows, :].astype(F32))).astype(y_ref.dtype)
            return carry
        lax.fori_loop(0, n, body, 0)

    readout(ol, gl, yl_ref, n_lat)
    if need_ctx_out:
        readout(oc, gc, yc_ref, n_ctx)


def _retention_tables(lc, l):
    half = RT_HK // 2
    inv = 1.0 / (RT_ROPE_BASE ** jnp.linspace(0.0, 1.0, half, dtype=F32))
    ang = jnp.arange(lc + l, dtype=F32)[:, None] * inv[None, :]
    j = jnp.arange(2 * RT_HEADS, dtype=F32)
    lg_all = jnp.log1p(-jnp.exp2(-5.0 - j))
    c = RET_CHUNK
    pos = jnp.arange(c, dtype=F32)
    rel = pos[:, None] - pos[None, :]
    intra, qk, sd = [], [], []
    for d in range(2):
        lg = lg_all[d::2]
        m = jnp.where(rel >= 0, jnp.exp(jnp.maximum(rel, 0.0)[None] * lg[:, None, None]), 0.0)
        qdec = jnp.exp((pos + 1.0)[None, :] * lg[:, None])
        kdec = jnp.exp((c - 1.0 - pos)[None, :] * lg[:, None])
        if d == 1:
            m = jnp.swapaxes(m, 1, 2)
            qdec = qdec[:, ::-1]
            kdec = kdec[:, ::-1]
        intra.append(m)
        qk.append(jnp.broadcast_to(jnp.concatenate([qdec, kdec], axis=1)[:, :, None],
                                   (RT_HEADS, 2 * c, 128)))
        sd.append(jnp.broadcast_to(jnp.exp(c * lg)[:, None, None], (RT_HEADS, 8, 128)))
    return (jnp.cos(ang), jnp.sin(ang), jnp.stack(intra), jnp.stack(qk), jnp.stack(sd))


def _retention(p_lat, p_ctx, tables, batch, need_ctx_out):
    l = p_lat.shape[0] // batch
    lc = p_ctx.shape[0] // batch
    cos, sin, intra, qk, sd = tables
    c = RET_CHUNK

    def col(base):
        return lambda b, h: (b, base + h)

    kq, kk_, kv, kg = COL_RQ * 128 // RT_HK, COL_RK * 128 // RT_HK, COL_RV * 128 // RT_HV, COL_RG * 128 // RT_HV
    lat_specs = [pl.BlockSpec((l, RT_HK), col(kq)), pl.BlockSpec((l, RT_HK), col(kk_)),
                 pl.BlockSpec((l, RT_HV), col(kv)), pl.BlockSpec((l, RT_HV), col(kg))]
    if need_ctx_out:
        ctx_specs = [pl.BlockSpec((lc, RT_HK), col(kq)), pl.BlockSpec((lc, RT_HK), col(kk_)),
                     pl.BlockSpec((lc, RT_HV), col(kv)), pl.BlockSpec((lc, RT_HV), col(kg))]
    else:
        ctx_specs = [pl.BlockSpec((lc, RT_HK), col(kk_)), pl.BlockSpec((lc, RT_HV), col(kv))]
    tab_specs = [pl.BlockSpec((lc + l, 128), lambda b, h: (0, 0)),
                 pl.BlockSpec((lc + l, 128), lambda b, h: (0, 0)),
                 pl.BlockSpec((2, 1, c, c), lambda b, h: (0, h, 0, 0)),
                 pl.BlockSpec((2, 1, 2 * c, 128), lambda b, h: (0, h, 0, 0)),
                 pl.BlockSpec((2, 1, 8, 128), lambda b, h: (0, h, 0, 0))]
    args = [p_lat] * 4 + [p_ctx] * len(ctx_specs) + [cos, sin, intra, qk, sd]
    out_specs = [pl.BlockSpec((l, RT_HV), lambda b, h: (b, h))]
    out_shape = [jax.ShapeDtypeStruct((batch * l, RT_HEADS * RT_HV), BF16)]
    scratch = [pltpu.VMEM((l, RT_HK), BF16), pltpu.VMEM((l, RT_HK), BF16)]
    if need_ctx_out:
        out_specs.append(pl.BlockSpec((lc, RT_HV), lambda b, h: (b, h)))
        out_shape.append(jax.ShapeDtypeStruct((batch * lc, RT_HEADS * RT_HV), BF16))
        scratch += [pltpu.VMEM((lc, RT_HK), BF16), pltpu.VMEM((lc, RT_HK), BF16),
                    pltpu.VMEM((l, RT_HV), F32), pltpu.VMEM((lc, RT_HV), F32)]
    else:
        scratch += [pltpu.VMEM((lc, RT_HK), BF16), pltpu.VMEM((l, RT_HV), F32)]
    scratch += [pltpu.VMEM((RT_HK, RT_HV), F32)] * 2
    outs = pl.pallas_call(
        functools.partial(_ret_kernel, need_ctx_out=need_ctx_out, n_lat=l // c, n_ctx=lc // c),
        grid=(batch, RT_HEADS),
        in_specs=lat_specs + ctx_specs + tab_specs,
        out_specs=out_specs,
        out_shape=out_shape,
        scratch_shapes=scratch,
        compiler_params=_cparams(("arbitrary", "arbitrary")),
        name="retention_scan",
    )(*args)
    return outs[0], (outs[1] if need_ctx_out else None)


def _short_conv_kernel(x_ref, w_ref, b_ref, o_ref, xpad_ref, *, grid):
    l, tc = x_ref.shape
    p = CONV_PAD
    zeros = jnp.zeros((p, tc), F32)
    xpad_ref[0:p, :] = zeros
    xpad_ref[p + l:p + l + p, :] = zeros
    xpad_ref[p:p + l, :] = x_ref[...].astype(F32)
    colw = lax.broadcasted_iota(jnp.int32, (l, tc), 0) % GRID_W
    out = jnp.broadcast_to(b_ref[...], (l, tc))
    for dj in range(3):
        acc = None
        for di in (range(3) if grid else (1,)):
            off = p + (di - 1) * GRID_W + (dj - 1)
            t = w_ref[di * 3 + dj:di * 3 + dj + 1, :] * xpad_ref[off:off + l, :]
            acc = t if acc is None else acc + t
        if grid and dj == 0:
            acc = jnp.where(colw >= 1, acc, 0.0)
        if grid and dj == 2:
            acc = jnp.where(colw <= GRID_W - 2, acc, 0.0)
        out = out + acc
    o_ref[...] = out.astype(o_ref.dtype)


def _short_conv(p_arr, w9, bias, batch, grid):
    l = p_arr.shape[0] // batch
    c3 = 3 * HY_DIM
    tc = 256
    base = COL_HY * 128 // tc
    return pl.pallas_call(
        functools.partial(_short_conv_kernel, grid=grid),
        grid=(batch, c3 // tc),
        in_specs=[pl.BlockSpec((l, tc), lambda b, j: (b, base + j)),
                  pl.BlockSpec((9, tc), lambda b, j: (0, j)),
                  pl.BlockSpec((1, tc), lambda b, j: (0, j))],
        out_specs=pl.BlockSpec((l, tc), lambda b, j: (b, j)),
        out_shape=jax.ShapeDtypeStruct((batch * l, c3), BF16),
        scratch_shapes=[pltpu.VMEM((l + 2 * CONV_PAD, tc), F32)],
        compiler_params=_cparams(("arbitrary", "arbitrary")),
        name="hyena_short_conv",
    )(p_arr, w9, bias.reshape(1, c3))


def _dft_matrices(l):
    n = 2 * l
    s = int(round(math.sqrt(n)))
    while n % s:
        s -= 1
    q = n // s
    f = np.arange(l, dtype=np.int64)
    ang_a = 2.0 * np.pi * ((f[:, None] * s * np.arange(q)[None, :]) % n) / n
    ang_b = 2.0 * np.pi * ((f[:, None] * np.arange(s)[None, :]) % n) / n
    t = np.arange(n)
    rep = (t[None, :] // s == np.arange(q)[:, None]).astype(np.float32)
    til = (t[None, :] % s == np.arange(s)[:, None]).astype(np.float32)
    h = DFT_HALF
    row = lambda j: (j, 0)
    const = lambda j: (0, 0)
    fwd, inv = pl.pallas_call(
        _dft_build_kernel,
        grid=(l // h,),
        in_specs=[pl.BlockSpec((h, q), row), pl.BlockSpec((h, q), row),
                  pl.BlockSpec((h, s), row), pl.BlockSpec((h, s), row),
                  pl.BlockSpec((q, n), const), pl.BlockSpec((s, n), const)],
        out_specs=[pl.BlockSpec((1, 2, h, n), lambda j: (j, 0, 0, 0)),
                   pl.BlockSpec((h, n), row)],
        out_shape=[jax.ShapeDtypeStruct((l // h, 2, h, n), BF16),
                   jax.ShapeDtypeStruct((l, n), BF16)],
        compiler_params=_cparams(("arbitrary",)),
        name="dft_build",
    )(jnp.asarray(np.cos(ang_a), F32), jnp.asarray(np.sin(ang_a), F32),
      jnp.asarray(np.cos(ang_b), F32), jnp.asarray(np.sin(ang_b), F32),
      jnp.asarray(rep, BF16), jnp.asarray(til, BF16))
    return fwd.reshape(n, n), inv


def _dft_build_kernel(ca_ref, sa_ref, cb_ref, sb_ref, rep_ref, til_ref, fwd_ref, inv_ref):
    h, n = inv_ref.shape

    def spread(x_ref, m_ref):
        p1, p2, p3 = _split3(x_ref[...])
        m = m_ref[...]
        return _dot(p1, m) + _dot(p2, m) + _dot(p3, m)

    ca, sa = spread(ca_ref, rep_ref), spread(sa_ref, rep_ref)
    cb, sb = spread(cb_ref, til_ref), spread(sb_ref, til_ref)
    cosm = ca * cb - sa * sb
    nsin = -(sa * cb + ca * sb)
    first = pl.program_id(0) == 0
    rowi = lax.broadcasted_iota(jnp.int32, (h, n), 0)
    coli = lax.broadcasted_iota(jnp.int32, (h, n), 1)
    alt_col = (1 - 2 * (coli % 2)).astype(F32)
    fwd_ref[0, 0] = cosm.astype(BF16)
    fwd_ref[0, 1] = jnp.where(first & (rowi == 0), alt_col, nsin).astype(BF16)
    rowg = lax.broadcasted_iota(jnp.int32, (h, h), 0) + pl.program_id(0) * h
    col0 = lax.broadcasted_iota(jnp.int32, (h, h), 1) == 0
    alt_row = (1 - 2 * (rowg % 2)).astype(F32)
    pieces = []
    for jj in range(n // (2 * h)):
        cp = cosm[:, jj * h:(jj + 1) * h] * (2.0 / n)
        ip = nsin[:, jj * h:(jj + 1) * h] * (2.0 / n)
        if jj == 0:
            cp = jnp.where(col0, 1.0 / n, cp)
            ip = jnp.where(col0, alt_row * (1.0 / n), ip)
        pieces += [cp.astype(BF16), ip.astype(BF16)]
    inv_ref[...] = jnp.concatenate(pieces, axis=1)


def _filter_spectrum_kernel(f_ref, top_ref, bot_ref, o_ref):
    h = DFT_HALF
    bl = top_ref.shape[0]
    acc = _dot(f_ref[:, 0:bl], top_ref[...]) + _dot(f_ref[:, bl:2 * bl], bot_ref[...])
    kr, km = acc[:h], acc[h:]
    row0 = (lax.broadcasted_iota(jnp.int32, kr.shape, 0) == 0) & (pl.program_id(2) == 0)
    o_ref[0, 0:h, :] = kr.astype(o_ref.dtype)
    o_ref[0, h:2 * h, :] = jnp.where(row0, 0.0, km).astype(o_ref.dtype)
    o_ref[0, 2 * h:3 * h, :] = jnp.where(row0, km, kr).astype(o_ref.dtype)


def _filter_spectrum(fwd, kern_bf, nb):
    n = fwd.shape[0]
    bl = n // 2
    nc = kern_bf.shape[1]
    tm, tn = 2 * DFT_HALF, 1024
    nd = 2 * nb - 1
    wrap = 2 * nb
    return pl.pallas_call(
        _filter_spectrum_kernel,
        grid=(nc // tn, nd, n // tm),
        in_specs=[pl.BlockSpec((tm, n), lambda c, d, t: (t, 0)),
                  pl.BlockSpec((bl, tn), lambda c, d, t: ((d - (nb - 1)) % wrap, c)),
                  pl.BlockSpec((bl, tn), lambda c, d, t: ((d - nb) % wrap, c))],
        out_specs=pl.BlockSpec((1, 3 * DFT_HALF, tn), lambda c, d, t: (d, t, c)),
        out_shape=jax.ShapeDtypeStruct((nd, (n // tm) * 3 * DFT_HALF, nc), BF16),
        compiler_params=_cparams(("arbitrary", "arbitrary", "arbitrary")),
        name="hyena_filter_dft",
    )(fwd, kern_bf, kern_bf)


def _dft_fwd_kernel(f_ref, z_ref, k_ref, y_ref):
    h = DFT_HALF
    bl = f_ref.shape[1]
    nb = y_ref.shape[1]
    f = f_ref[...]
    spec = [_dot(f, z_ref[j * bl:(j + 1) * bl, :]).astype(BF16) for j in range(nb)]
    for i in range(nb):
        yr = yi = None
        for j in range(nb):
            d = i - j + nb - 1
            kr, ki, kr2 = k_ref[d, 0:h, :], k_ref[d, h:2 * h, :], k_ref[d, 2 * h:3 * h, :]
            ur, ui = spec[j][:h], spec[j][h:]
            tr = ur * kr - ui * ki
            ti = ur * ki + ui * kr2
            yr = tr if yr is None else yr + tr
            yi = ti if yi is None else yi + ti
        y_ref[0, i, 0:h, :] = yr.astype(y_ref.dtype)
        y_ref[0, i, h:2 * h, :] = yi.astype(y_ref.dtype)


def _dft_fwd(fwd, z_arr, z_col, kf, k_col, batch, nb):
    n = fwd.shape[0]
    bl = n // 2
    l = nb * bl
    tm, tn = 2 * DFT_HALF, 512
    cpt = HY_DIM // tn
    nd = kf.shape[0]
    return pl.pallas_call(
        _dft_fwd_kernel,
        grid=(cpt, n // tm, batch),
        in_specs=[pl.BlockSpec((tm, bl), lambda c, t, b: (t, 0)),
                  pl.BlockSpec((l, tn), lambda c, t, b: (b, z_col * cpt + c)),
                  pl.BlockSpec((nd, 3 * DFT_HALF, tn), lambda c, t, b: (0, t, k_col * cpt + c))],
        out_specs=pl.BlockSpec((1, nb, tm, tn), lambda c, t, b: (b, 0, t, c)),
        out_shape=jax.ShapeDtypeStruct((batch, nb, n, HY_DIM), BF16),
        compiler_params=_cparams(("arbitrary", "arbitrary", "arbitrary")),
        name="hyena_dft_fwd",
    )(fwd, z_arr, kf)


def _dft_inv_kernel(g_ref, y_ref, gate_ref, z_ref, skip_ref, o_ref):
    conv = _dot(g_ref[...], y_ref[0, 0])
    z = z_ref[...].astype(F32)
    o_ref[...] = (gate_ref[...].astype(F32) * (conv + skip_ref[...] * z)).astype(o_ref.dtype)


def _dft_inv(inv, y, gate_arr, gate_col, z_arr, z_col, skip, batch, nb):
    bl, n = inv.shape
    tn = HY_DIM
    return pl.pallas_call(
        _dft_inv_kernel,
        grid=(batch, nb),
        in_specs=[pl.BlockSpec((bl, n), lambda b, i: (0, 0)),
                  pl.BlockSpec((1, 1, n, tn), lambda b, i: (b, i, 0, 0)),
                  pl.BlockSpec((bl, tn), lambda b, i: (b * nb + i, gate_col)),
                  pl.BlockSpec((bl, tn), lambda b, i: (b * nb + i, z_col)),
                  pl.BlockSpec((1, tn), lambda b, i: (0, 0))],
        out_specs=pl.BlockSpec((bl, tn), lambda b, i: (b * nb + i, 0)),
        out_shape=jax.ShapeDtypeStruct((batch * nb * bl, tn), BF16),
        compiler_params=_cparams(("arbitrary", "arbitrary")),
        name="hyena_dft_inv",
    )(inv, y, gate_arr, z_arr, skip.reshape(1, tn))


def _tap_features(l):
    f32 = np.float32
    t01 = np.linspace(0.0, 1.0, l, dtype=f32)[:, None]
    ang = f32(2.0 * math.pi) * np.arange(l, dtype=f32)[:, None] / f32(l)
    bands = np.linspace(1e-4, HY_BANDS - 1, HY_BANDS, dtype=f32)[None, :]
    z = np.concatenate([t01, np.cos(bands * ang), -np.sin(bands * ang)], axis=-1).astype(f32)
    pos = np.concatenate([np.arange(l), [0], np.arange(l - 1, 0, -1)])
    zz = np.zeros((2 * l, 128), f32)
    zz[:, :z.shape[1]] = z[pos]
    zz[l] = 0.0
    return zz


def _dot_hi(a, b):
    a1 = a.astype(BF16)
    a2 = (a - a1.astype(F32)).astype(BF16)
    b1 = b.astype(BF16)
    b2 = (b - b1.astype(F32)).astype(BF16)
    return _dot(a1, b1) + _dot(a1, b2) + _dot(a2, b1)


def _filter_kernel(zz_ref, w1_ref, b1_ref, fr_ref, w2_ref, b2_ref, w3f_ref, w3b_ref, dl_ref,
                   o_ref, hdn_ref):
    n = zz_ref.shape[0]
    l = n // 2

    @pl.when((pl.program_id(0) == 0) & (pl.program_id(1) == 0))
    def _():
        h1 = jnp.sin(fr_ref[0:1, :] * (_dot_hi(zz_ref[...], w1_ref[...]) + b1_ref[...]))
        hdn_ref[...] = jnp.sin(fr_ref[1:2, :] * (_dot_hi(h1, w2_ref[...]) + b2_ref[...]))

    filt = jnp.concatenate([_dot_hi(hdn_ref[0:l, :], w3f_ref[...]),
                            _dot_hi(hdn_ref[l:n, :], w3b_ref[...])], axis=0)
    decay = jnp.exp(-zz_ref[:, 0:1] * dl_ref[...])
    rowi = lax.broadcasted_iota(jnp.int32, filt.shape, 0)
    kern = jnp.where(rowi == l, 0.0, filt * decay)
    o_ref[...] = (kern / jnp.sum(jnp.abs(kern), axis=0, keepdims=True)).astype(o_ref.dtype)


def _hyena_filters(l, w1, b1, freq, w2, b2, w3):
    n = 2 * l
    tn = 256
    ff = w2.shape[0]
    cpo = HY_DIM // tn
    deltas = np.abs(np.linspace(math.log(HY_DECAY_TARGET) / HY_DECAY_PCT_LONG,
                                math.log(HY_DECAY_TARGET) / HY_DECAY_PCT_SHORT, HY_DIM,
                                dtype=np.float32)).reshape(1, HY_DIM)
    const = lambda o, c: (0, 0)
    return pl.pallas_call(
        _filter_kernel,
        grid=(HY_ORDER, cpo),
        in_specs=[pl.BlockSpec((n, 128), const), pl.BlockSpec((128, ff), const),
                  pl.BlockSpec((1, ff), const), pl.BlockSpec((2, ff), const),
                  pl.BlockSpec((ff, ff), const), pl.BlockSpec((1, ff), const),
                  pl.BlockSpec((ff, tn), lambda o, c: (0, o * 2 * cpo + c)),
                  pl.BlockSpec((ff, tn), lambda o, c: (0, o * 2 * cpo + cpo + c)),
                  pl.BlockSpec((1, tn), lambda o, c: (0, c))],
        out_specs=pl.BlockSpec((n, tn), lambda o, c: (0, o * cpo + c)),
        out_shape=jax.ShapeDtypeStruct((n, HY_ORDER * HY_DIM), BF16),
        scratch_shapes=[pltpu.VMEM((n, ff), F32)],
        compiler_params=_cparams(("arbitrary", "arbitrary")),
        name="hyena_filter_mlp",
    )(jnp.asarray(_tap_features(l)), jnp.pad(w1, ((0, 128 - w1.shape[0]), (0, 0))),
      b1.reshape(1, ff), freq, w2, b2.reshape(1, ff), w3, w3, jnp.asarray(deltas))


def _hyena(p_arr, lp, dft, batch, grid):
    l = p_arr.shape[0] // batch
    fwd, inv = dft
    uc = _short_conv(p_arr, lp['hy_conv_w'].reshape(9, 3 * HY_DIM), lp['hy_conv_b'], batch, grid)
    kern = _hyena_filters(l, lp['hy_ff_w1'], lp['hy_ff_b1'], lp['hy_ff_freq'],
                          lp['hy_ff_w2'], lp['hy_ff_b2'], lp['hy_ff_w3'])
    nb = l // inv.shape[0]
    kf = _filter_spectrum(fwd, kern, nb)
    z_arr, z_col = uc, 0
    for n in range(HY_ORDER):
        y = _dft_fwd(fwd, z_arr, z_col, kf, n, batch, nb)
        z_arr = _dft_inv(inv, y, uc, n + 1, z_arr, z_col, lp['hy_skip'][n], batch, nb)
        z_col = 0
    return z_arr


def _merge_kernel(yhy_ref, ohg_ref, hgg_ref, yrt_ref, b0_ref, b1_ref, b2_ref, x_ref, mod_ref,
                  nw_ref, phy_ref, phg_ref, prt_ref, wo_ref, lng_ref, lnb_ref, o_ref):
    o = ohg_ref[...]
    yhg = (o * lax.rsqrt(jnp.mean(o * o, axis=-1, keepdims=True) + LN_EPS) * nw_ref[...]
           * _silu(hgg_ref[...].astype(F32)))
    m = (jax.nn.sigmoid(b0_ref[...].astype(F32)) * _dot(yhy_ref[...], phy_ref[...])
         + jax.nn.sigmoid(b1_ref[...].astype(F32)) * _dot(yhg.astype(BF16), phg_ref[...])
         + jax.nn.sigmoid(b2_ref[...].astype(F32)) * _dot(yrt_ref[...], prt_ref[...]))
    t = _dot(m.astype(BF16), wo_ref[...])
    gt1 = mod_ref[0][2:3]
    o_ref[...] = _layer_norm_rows(DN_ALPHA * x_ref[...] + gt1 * t, lng_ref[...], lnb_ref[...])


def _merge(y_hy, o_hg, y_rt, p_arr, x, modp, lp, wbf, rows_per_batch):
    m = x.shape[0]
    tm = min(256, m)
    d = D_MODEL
    mi = _mod_index(rows_per_batch, tm)
    row = lambda i: (i, 0)
    const = lambda i: (0, 0)
    gcol = COL_HG * 128 // d
    bcol = COL_BR * 128 // d
    return pl.pallas_call(
        _merge_kernel,
        grid=(m // tm,),
        in_specs=[pl.BlockSpec((tm, d), row), pl.BlockSpec((tm, d), row),
                  pl.BlockSpec((tm, d), lambda i: (i, gcol)),
                  pl.BlockSpec((tm, 2 * d), row),
                  pl.BlockSpec((tm, d), lambda i: (i, bcol)),
                  pl.BlockSpec((tm, d), lambda i: (i, bcol + 1)),
                  pl.BlockSpec((tm, d), lambda i: (i, bcol + 2)),
                  pl.BlockSpec((tm, d), row),
                  pl.BlockSpec((1, 8, d), lambda i: (mi(i), 0, 0)),
                  pl.BlockSpec((1, d), const),
                  pl.BlockSpec((d, d), const), pl.BlockSpec((d, d), const),
                  pl.BlockSpec((2 * d, d), const), pl.BlockSpec((d, d), const),
                  pl.BlockSpec((1, d), const), pl.BlockSpec((1, d), const)],
        out_specs=pl.BlockSpec((tm, d), row),
        out_shape=jax.ShapeDtypeStruct((m, d), F32),
        compiler_params=_cparams(("arbitrary",)),
        name="merge_out_ln",
    )(y_hy, o_hg, p_arr, y_rt, p_arr, p_arr, p_arr, x, modp, lp['hg_norm_w'].reshape(1, d),
      wbf['p_hy'], wbf['p_hg'], wbf['p_rt'], wbf['w_o'],
      lp['ln1_g'].reshape(1, d), lp['ln1_b'].reshape(1, d))


def _ffn_kernel(x_ref, mod_ref, w1_ref, w3_ref, w2_ref, lng_ref, lnb_ref, o_ref, h_ref, acc_ref):
    k = pl.program_id(1)
    m = mod_ref[0]

    @pl.when(k == 0)
    def _():
        h_ref[...] = (x_ref[...] * (1.0 + m[4:5]) + m[3:4]).astype(BF16)
        acc_ref[...] = jnp.zeros(acc_ref.shape, F32)

    h = h_ref[...]
    u = _silu(_dot(h, w1_ref[...])) * _dot(h, w3_ref[...])
    acc_ref[...] += _dot(u.astype(BF16), w2_ref[...])

    @pl.when(k == pl.num_programs(1) - 1)
    def _():
        o_ref[...] = _layer_norm_rows(DN_ALPHA * x_ref[...] + m[5:6] * acc_ref[...],
                                      lng_ref[...], lnb_ref[...])


def _ffn_dense(x, modp, w1, w3, w2, ln_g, ln_b, rows_per_batch):
    m = x.shape[0]
    d = D_MODEL
    dff = w1.shape[1]
    tm = min(512, m)
    tf = dff // 2
    mi = _mod_index(rows_per_batch, tm)
    return pl.pallas_call(
        _ffn_kernel,
        grid=(m // tm, dff // tf),
        in_specs=[pl.BlockSpec((tm, d), lambda i, k: (i, 0)),
                  pl.BlockSpec((1, 8, d), lambda i, k: (mi(i), 0, 0)),
                  pl.BlockSpec((d, tf), lambda i, k: (0, k)),
                  pl.BlockSpec((d, tf), lambda i, k: (0, k)),
                  pl.BlockSpec((tf, d), lambda i, k: (k, 0)),
                  pl.BlockSpec((1, d), lambda i, k: (0, 0)),
                  pl.BlockSpec((1, d), lambda i, k: (0, 0))],
        out_specs=pl.BlockSpec((tm, d), lambda i, k: (i, 0)),
        out_shape=jax.ShapeDtypeStruct((m, d), F32),
        scratch_shapes=[pltpu.VMEM((tm, d), BF16), pltpu.VMEM((tm, d), F32)],
        compiler_params=_cparams(("arbitrary", "arbitrary")),
        name="ffn_dense_ln",
    )(x, modp, w1, w3, w2, ln_g.reshape(1, d), ln_b.reshape(1, d))


def _router_kernel(x_ref, mod_ref, r_ref, h_ref, lg_ref):
    m = mod_ref[0]
    h = x_ref[...] * (1.0 + m[4:5]) + m[3:4]
    h_ref[...] = h
    a1, a2, a3 = _split3(h)
    r1, r2, r3 = _split3(r_ref[...])
    lg_ref[...] = (_dot(a1, r1) + _dot(a1, r2) + _dot(a2, r1)
                   + _dot(a2, r2) + _dot(a1, r3) + _dot(a3, r1))


def _router(x, modp, router_pad, rows_per_batch):
    m = x.shape[0]
    d = D_MODEL
    tm = min(512, m)
    mi = _mod_index(rows_per_batch, tm)
    return pl.pallas_call(
        _router_kernel,
        grid=(m // tm,),
        in_specs=[pl.BlockSpec((tm, d), lambda i: (i, 0)),
                  pl.BlockSpec((1, 8, d), lambda i: (mi(i), 0, 0)),
                  pl.BlockSpec((d, 128), lambda i: (0, 0))],
        out_specs=[pl.BlockSpec((tm, d), lambda i: (i, 0)),
                   pl.BlockSpec((tm, 128), lambda i: (i, 0))],
        out_shape=[jax.ShapeDtypeStruct((m, d), F32), jax.ShapeDtypeStruct((m, 128), F32)],
        compiler_params=_cparams(("arbitrary",)),
        name="moe_router",
    )(x, modp, router_pad)


def _moe_ffn_kernel(be_ref, nu_ref, xp_ref, w1_ref, w3_ref, w2_ref, o_ref, x_ref, acc_ref):
    j = pl.program_id(0)
    k = pl.program_id(1)

    @pl.when(j < nu_ref[0])
    def _():
        @pl.when(k == 0)
        def _():
            x_ref[...] = xp_ref[...].astype(BF16)
            acc_ref[...] = jnp.zeros(acc_ref.shape, F32)

        x = x_ref[...]
        u = _silu(_dot(x, w1_ref[0])) * _dot(x, w3_ref[0])
        acc_ref[...] += _dot(u.astype(BF16), w2_ref[0])

        @pl.when(k == pl.num_programs(1) - 1)
        def _():
            o_ref[...] = acc_ref[...]

    @pl.when((j >= nu_ref[0]) & (k == pl.num_programs(1) - 1))
    def _():
        o_ref[...] = jnp.zeros(o_ref.shape, F32)


def _moe_ffn(xb, block_e, n_used, w1, w3, w2):
    ns = xb.shape[0]
    d = D_MODEL
    tm = MOE_ROWS
    dex = w1.shape[2]
    tf = dex // 2
    grid_spec = pltpu.PrefetchScalarGridSpec(
        num_scalar_prefetch=2,
        grid=(ns // tm, dex // tf),
        in_specs=[pl.BlockSpec((tm, d), lambda j, k, be, nu: (j, 0)),
                  pl.BlockSpec((1, d, tf), lambda j, k, be, nu: (be[j], 0, k)),
                  pl.BlockSpec((1, d, tf), lambda j, k, be, nu: (be[j], 0, k)),
                  pl.BlockSpec((1, tf, d), lambda j, k, be, nu: (be[j], k, 0))],
        out_specs=pl.BlockSpec((tm, d), lambda j, k, be, nu: (j, 0)),
        scratch_shapes=[pltpu.VMEM((tm, d), BF16), pltpu.VMEM((tm, d), F32)])
    return pl.pallas_call(
        _moe_ffn_kernel,
        grid_spec=grid_spec,
        out_shape=jax.ShapeDtypeStruct((ns, d), F32),
        compiler_params=_cparams(("arbitrary", "arbitrary")),
        name="moe_expert_ffn",
    )(block_e, n_used, xb, w1, w3, w2)


def _combine_kernel(x_ref, mod_ref, y0_ref, y1_ref, g_ref, lng_ref, lnb_ref, o_ref):
    m = mod_ref[0]
    g = g_ref[...]
    f = g[:, 0:1] * y0_ref[...] + g[:, 1:2] * y1_ref[...]
    o_ref[...] = _layer_norm_rows(DN_ALPHA * x_ref[...] + m[5:6] * f, lng_ref[...], lnb_ref[...])


def _moe_combine(x, modp, y0, y1, gate_pad, ln_g, ln_b, rows_per_batch):
    m = x.shape[0]
    d = D_MODEL
    tm = min(512, m)
    mi = _mod_index(rows_per_batch, tm)
    row = lambda i: (i, 0)
    return pl.pallas_call(
        _combine_kernel,
        grid=(m // tm,),
        in_specs=[pl.BlockSpec((tm, d), row),
                  pl.BlockSpec((1, 8, d), lambda i: (mi(i), 0, 0)),
                  pl.BlockSpec((tm, d), row), pl.BlockSpec((tm, d), row),
                  pl.BlockSpec((tm, 128), row),
                  pl.BlockSpec((1, d), lambda i: (0, 0)), pl.BlockSpec((1, d), lambda i: (0, 0))],
        out_specs=pl.BlockSpec((tm, d), row),
        out_shape=jax.ShapeDtypeStruct((m, d), F32),
        compiler_params=_cparams(("arbitrary",)),
        name="moe_combine_ln",
    )(x, modp, y0, y1, gate_pad, ln_g.reshape(1, d), ln_b.reshape(1, d))


def _cast_kernel(x_ref, o_ref):
    o_ref[...] = x_ref[...].astype(o_ref.dtype)


def _to_bf16(w, group):
    _, e, a, b = w.shape
    ta = a // 4
    return pl.pallas_call(
        _cast_kernel,
        grid=(e, a // ta),
        in_specs=[pl.BlockSpec((None, 1, ta, b), lambda i, r: (group, i, r, 0))],
        out_specs=pl.BlockSpec((1, ta, b), lambda i, r: (i, r, 0)),
        out_shape=jax.ShapeDtypeStruct((e, a, b), BF16),
        compiler_params=_cparams(("arbitrary", "arbitrary")),
        name="weight_cast",
    )(w)


def _moe(x, modp, router, w1, w3, w2, ln_g, ln_b, rows_per_batch):
    n, d = x.shape
    e = router.shape[1]
    h2, logits = _router(x, modp, jnp.pad(router, ((0, 0), (0, 128 - e))), rows_per_batch)
    top_val, top_idx = lax.top_k(logits[:, :e], TOP_K)
    gate = jax.nn.softmax(top_val, axis=-1)
    flat_e = top_idx.reshape(-1)
    flat_t = jnp.repeat(jnp.arange(n, dtype=jnp.int32), TOP_K)
    order = jnp.argsort(flat_e, stable=True).astype(jnp.int32)
    rank = jnp.argsort(order).astype(jnp.int32)
    counts = jnp.sum((flat_e[:, None] == jnp.arange(e)[None, :]).astype(jnp.int32), axis=0)
    starts = jnp.cumsum(counts) - counts
    padded = (counts + MOE_ROWS - 1) // MOE_ROWS * MOE_ROWS
    pad_end = jnp.cumsum(padded)
    pad_start = pad_end - padded
    n_blocks = -(-(n * TOP_K) // MOE_ROWS) + e
    n_slots = n_blocks * MOE_ROWS
    block_start = jnp.arange(n_blocks) * MOE_ROWS
    block_e = jnp.minimum(jnp.sum(block_start[:, None] >= pad_end[None, :], axis=1), e - 1).astype(jnp.int32)
    n_used = (pad_end[-1] // MOE_ROWS).astype(jnp.int32).reshape(1)
    slot_e = jnp.repeat(block_e, MOE_ROWS)
    slot_off = jnp.arange(n_slots, dtype=jnp.int32) - pad_start[slot_e]
    slot_valid = slot_off < counts[slot_e]
    slot_src = jnp.where(slot_valid, starts[slot_e] + slot_off, 0)
    slot_tok = jnp.where(slot_valid, flat_t[order[slot_src]], 0).astype(jnp.int32)
    pos = (pad_start[flat_e] + rank - starts[flat_e]).astype(jnp.int32).reshape(n, TOP_K)
    xb = jnp.take(h2, slot_tok, axis=0, mode="clip")
    yb = _moe_ffn(xb, block_e, n_used, w1, w3, w2)
    y0 = jnp.take(yb, pos[:, 0], axis=0, mode="clip")
    y1 = jnp.take(yb, pos[:, 1], axis=0, mode="clip")
    gate_pad = jnp.pad(gate, ((0, 0), (0, 128 - TOP_K)))
    return _moe_combine(x, modp, y0, y1, gate_pad, ln_g, ln_b, rows_per_batch)


def kernel(x, c, ctx, c_ctx, ada_w, ada_b, w_in, hy_conv_w, hy_conv_b, hy_ff_w1, hy_ff_b1, hy_ff_freq, hy_ff_w2, hy_ff_b2, hy_ff_w3, hy_skip, hg_lb_logits, hg_norm_w, p_hy, p_hg, p_rt, w_o, ln1_g, ln1_b, ln2_g, ln2_b, ffn_w1, ffn_w3, ffn_w2, moe_router, moe_w1, moe_w3, moe_w2):
    batch, l, d = x.shape
    lc = ctx.shape[1]
    assert d == D_MODEL and batch <= 8
    assert l % 512 == 0 and l % GRID_W == 0 and lc % max(RET_CHUNK, DFT_HALF) == 0

    cs = jnp.cumsum(jax.nn.softmax(hg_lb_logits.astype(F32), axis=1), axis=1)
    lower_bounds = cs - cs[:, :1]
    cc = jnp.zeros((16, d), F32).at[:batch].set(c).at[8].set(c_ctx)
    ret_tables = _retention_tables(lc, l)
    dft_lat = _dft_matrices(min(HY_BLOCK, l))
    dft_ctx = _dft_matrices(min(HY_BLOCK, lc))

    x_lat = x.reshape(batch * l, d)
    x_ctx = ctx.reshape(batch * lc, d)
    for i in range(DEPTH):
        need_ctx_out = i < DEPTH - 1
        use_moe = i % 2 == 1
        g = i // 2
        lp = {'hy_conv_w': hy_conv_w[i], 'hy_conv_b': hy_conv_b[i], 'hy_ff_w1': hy_ff_w1[i],
              'hy_ff_b1': hy_ff_b1[i], 'hy_ff_freq': hy_ff_freq[i], 'hy_ff_w2': hy_ff_w2[i],
              'hy_ff_b2': hy_ff_b2[i], 'hy_ff_w3': hy_ff_w3[i], 'hy_skip': hy_skip[i],
              'hg_norm_w': hg_norm_w[i], 'ln1_g': ln1_g[i], 'ln1_b': ln1_b[i]}
        wbf = {'p_hy': p_hy[i].astype(BF16), 'p_hg': p_hg[i].astype(BF16),
               'p_rt': p_rt[i].astype(BF16), 'w_o': w_o[i].astype(BF16)}
        mod = _ada(cc, ada_w, i, ada_b[i])
        modp = jnp.pad(mod.reshape(16, 6, d), ((0, 0), (0, 2), (0, 0)))
        lb = lower_bounds[:, i]
        gl_tab = jnp.pad(jnp.stack([jnp.maximum(jnp.log(lb), -1e30), jnp.log1p(-lb), 1.0 - lb], axis=1),
                         ((0, 0), (0, 5), (0, 0)))

        p_lat = _proj(x_lat, modp, w_in, i, N_IN_COLS, l)
        p_ctx = _proj(x_ctx, modp, w_in, i, N_IN_COLS if need_ctx_out else N_STATE_COLS, None)

        o_hg_l, o_hg_c = _gla(p_lat, p_ctx, gl_tab, batch, need_ctx_out)
        y_rt_l, y_rt_c = _retention(p_lat, p_ctx, ret_tables, batch, need_ctx_out)
        y_hy_l = _hyena(p_lat, lp, dft_lat, batch, True)
        x_lat_new = _merge(y_hy_l, o_hg_l, y_rt_l, p_lat, x_lat, modp, lp, wbf, l)
        if need_ctx_out:
            y_hy_c = _hyena(p_ctx, lp, dft_ctx, batch, False)
            x_ctx = _merge(y_hy_c, o_hg_c, y_rt_c, p_ctx, x_ctx, modp, lp, wbf, None)
        x_lat = x_lat_new

        if use_moe:
            w1, w3, w2 = _to_bf16(moe_w1, g), _to_bf16(moe_w3, g), _to_bf16(moe_w2, g)
            if need_ctx_out:
                tok = jnp.concatenate([x_lat, x_ctx], axis=0)
                raise NotImplementedError("MoE layer with a context output is not part of this trunk")
            x_lat = _moe(x_lat, modp, moe_router[g], w1, w3, w2, ln2_g[i], ln2_b[i], l)
        else:
            w1, w3, w2 = ffn_w1[g].astype(BF16), ffn_w3[g].astype(BF16), ffn_w2[g].astype(BF16)
            x_lat = _ffn_dense(x_lat, modp, w1, w3, w2, ln2_g[i], ln2_b[i], l)
            if need_ctx_out:
                x_ctx = _ffn_dense(x_ctx, modp, w1, w3, w2, ln2_g[i], ln2_b[i], None)
    return x_lat.reshape(batch, l, d)
```

```python
import functools
import math

import numpy as np
import jax
import jax.numpy as jnp
from jax import lax
from jax.experimental import pallas as pl
from jax.experimental.pallas import tpu as pltpu

F32 = jnp.float32
BF16 = jnp.bfloat16

D_MODEL = 1024
DEPTH = 2
GRID_W = 64
HY_DIM = 1024
HY_ORDER = 2
HY_BANDS = 16
HY_DECAY_TARGET = 1e-2
HY_DECAY_PCT_SHORT = 0.3
HY_DECAY_PCT_LONG = 1.5
HG_HEADS = 8
HG_HK = 128
RT_HEADS = 4
RT_HK = 256
RT_HV = 512
RT_ROPE_BASE = 10000.0
N_STATE_COLS = 6144
N_IN_COLS = 17408
N_EXPERTS = 8
TOP_K = 2
DN_ALPHA = (2 * DEPTH) ** 0.25
LN_EPS = 1e-5

COL_FF, COL_FB, COL_HI, COL_RK, COL_RV = 0, 8, 16, 24, 32
COL_HQ, COL_HG, COL_RQ, COL_RG, COL_HY, COL_BR = 48, 56, 64, 72, 88, 112

GLA_CHUNK = 128
GLA_LEVELS = 7
GLA_PAD = 8
RET_CHUNK = 128
DFT_HALF = 256
HY_BLOCK = 512
CONV_PAD = 72
MOE_ROWS = 1024
VMEM_LIMIT = 56 * 1024 * 1024


def _cparams(sem, flags=None):
    return pltpu.CompilerParams(dimension_semantics=sem, vmem_limit_bytes=VMEM_LIMIT, flags=flags)


def _silu(x):
    return x * jax.nn.sigmoid(x)


def _layer_norm_rows(r, g, b):
    mu = jnp.mean(r, axis=-1, keepdims=True)
    d = r - mu
    var = jnp.mean(d * d, axis=-1, keepdims=True)
    return d * lax.rsqrt(var + LN_EPS) * g + b


def _split3(x):
    h = x.astype(BF16)
    r = x - h.astype(F32)
    m = r.astype(BF16)
    l = (r - m.astype(F32)).astype(BF16)
    return h, m, l


def _dot(a, b):
    return jnp.dot(a, b, preferred_element_type=F32)


def _dot_nt(a, b):
    return lax.dot_general(a, b, (((1,), (1,)), ((), ())), preferred_element_type=F32)


def _dot_tn(a, b):
    return lax.dot_general(a, b, (((0,), (0,)), ((), ())), preferred_element_type=F32)


def _ada_kernel(c_ref, w_ref, b_ref, o_ref):
    s = _silu(c_ref[...])
    o_ref[...] = _dot(s.astype(BF16), w_ref[...].astype(BF16)) + b_ref[...]


def _ada(cc, w, layer, b):
    n = w.shape[2]
    tn = 1024
    return pl.pallas_call(
        _ada_kernel,
        grid=(n // tn,),
        in_specs=[pl.BlockSpec((16, D_MODEL), lambda j: (0, 0)),
                  pl.BlockSpec((None, D_MODEL, tn), lambda j: (layer, 0, j)),
                  pl.BlockSpec((1, tn), lambda j: (0, j))],
        out_specs=pl.BlockSpec((16, tn), lambda j: (0, j)),
        out_shape=jax.ShapeDtypeStruct((16, n), F32),
        compiler_params=_cparams(("arbitrary",)),
        name="ada_mod",
    )(cc, w, b.reshape(1, n))


def _mod_index(rows_per_batch, tm):
    if rows_per_batch is None:
        return lambda i: 8
    return lambda i: (i * tm) // rows_per_batch


def _proj_kernel(x_ref, mod_ref, w_ref, o_ref, h_ref):
    @pl.when(pl.program_id(1) == 0)
    def _():
        m = mod_ref[0]
        h_ref[...] = (x_ref[...] * (1.0 + m[1:2]) + m[0:1]).astype(BF16)

    o_ref[...] = _dot(h_ref[...], w_ref[...].astype(BF16)).astype(o_ref.dtype)


def _proj(x, modp, w, layer, n_cols, rows_per_batch):
    m = x.shape[0]
    tm = min(2048, rows_per_batch or m)
    tn = 1024
    mi = _mod_index(rows_per_batch, tm)
    return pl.pallas_call(
        _proj_kernel,
        grid=(m // tm, n_cols // tn),
        in_specs=[pl.BlockSpec((tm, D_MODEL), lambda i, j: (i, 0)),
                  pl.BlockSpec((1, 8, D_MODEL), lambda i, j: (mi(i), 0, 0)),
                  pl.BlockSpec((None, D_MODEL, tn), lambda i, j: (layer, 0, j))],
        out_specs=pl.BlockSpec((tm, tn), lambda i, j: (i, j)),
        out_shape=jax.ShapeDtypeStruct((m, n_cols), BF16),
        scratch_shapes=[pltpu.VMEM((tm, D_MODEL), BF16)],
        compiler_params=_cparams(("arbitrary", "arbitrary")),
        name="in_proj",
    )(x, modp, w)


def _gla_gates(fl, gl_ref, d):
    a = gl_ref[d, 0:1, :]
    l1p = gl_ref[d, 1:2, :]
    oml = gl_ref[d, 2:3, :]
    t = jnp.exp(-jnp.abs(fl))
    u = 1.0 + t
    x2 = l1p + jnp.minimum(fl, 0.0) - jnp.log(u)
    log_f = jnp.maximum(a, x2) + jnp.log(1.0 + jnp.exp(-jnp.abs(a - x2)))
    key = oml * jnp.where(fl >= 0.0, t, 1.0) / u
    return log_f, key


def _gla_cumsum(g, rev):
    c = g.shape[0]
    r = lax.broadcasted_iota(jnp.int32, (c, c), 0)
    u = lax.broadcasted_iota(jnp.int32, (c, c), 1)
    tri = jnp.where((u >= r) if rev else (u <= r), 1.0, 0.0).astype(BF16)
    h, m, l = _split3(g)
    return _dot(tri, h) + _dot(tri, m) + _dot(tri, l)


def _gla_state_update(st, b, kk, vv, rev):
    c = b.shape[0]
    end = 0 if rev else c - 1
    b_end = b[end:end + 1, :]
    khat = (kk * jnp.exp2(b_end - b)).astype(BF16)
    return st * jnp.exp2(b_end) + _dot_tn(vv, khat)


def _gla_level_table():
    c = GLA_CHUNK
    t = np.arange(c)[:, None]
    s = np.arange(c)[None, :]
    top_bit = np.floor(np.log2(np.maximum(t ^ s, 1))).astype(np.int32)
    fwd = np.where(t > s, top_bit, np.where(t == s, GLA_LEVELS, -1))
    return np.stack([fwd, fwd.T]).astype(np.int32)


def _gla_ref_rows(b, bpad_ref, level, rev):
    c = b.shape[0]
    half = 1 << level
    blk = 2 * half
    idx = half if rev else half - 1
    if blk % 8 == 0:
        r = b.reshape(c // blk, blk, HG_HK)[:, idx:idx + 1, :]
        return jnp.broadcast_to(r, (c // blk, blk, HG_HK)).reshape(c, HG_HK)
    m = lax.broadcasted_iota(jnp.int32, (c, HG_HK), 0) % blk
    out = b
    for v in range(blk):
        if v != idx:
            off = GLA_PAD + idx - v
            out = jnp.where(m == v, bpad_ref[off:off + c, :], out)
    return out


def _gla_chunk_out(b, kk, vv, qq, st, rev, lvt, bpad_ref):
    c = GLA_CHUNK
    o = _dot_nt((qq * jnp.exp2(b)).astype(BF16), st.astype(BF16))
    bpad_ref[GLA_PAD:GLA_PAD + c, :] = b
    sc = jnp.zeros((c, c), F32)
    for level in range(GLA_LEVELS):
        e = b - _gla_ref_rows(b, bpad_ref, level, rev)
        p = _dot_nt((qq * jnp.exp2(e)).astype(BF16), (kk * jnp.exp2(-e)).astype(BF16))
        sc = jnp.where(lvt == level, p, sc)
    sc = jnp.where(lvt == GLA_LEVELS, _dot_nt(qq.astype(BF16), kk.astype(BF16)), sc)
    return o + _dot(sc.astype(BF16), vv)


def _gla_kernel(*refs, need_ctx_out, n_lat, n_ctx):
    if need_ctx_out:
        ffl, fbl, il, ql, ffc, fbc, ic, qc, gl_ref, lv_ref, ol_ref, oc_ref = refs[:12]
        bl, kl, bc, kc, qsl, qsc, bpf, bpr = refs[12:]
    else:
        ffl, fbl, il, ql, ffc, fbc, ic, gl_ref, lv_ref, ol_ref = refs[:10]
        bl, kl, bc, kc, qsl, bpf, bpr = refs[10:]
        qc = oc_ref = qsc = None
    c = GLA_CHUNK
    q_scale = HG_HK ** -0.5
    bpf[...] = jnp.zeros(bpf.shape, F32)
    bpr[...] = jnp.zeros(bpr.shape, F32)
    ol_ref[...] = jnp.zeros(ol_ref.shape, F32)
    if need_ctx_out:
        oc_ref[...] = jnp.zeros(oc_ref.shape, F32)

    def chunk_rows(idx):
        return pl.ds(pl.multiple_of(idx * c, c), c)

    def prepare(f_refs, q_ref, b_sc, k_sc, q_sc, n):
        def body(j, carry):
            rows = chunk_rows(j)
            for d in range(2):
                g, kk = _gla_gates(f_refs[d][rows, :].astype(F32), gl_ref, d)
                b_sc[d, rows, :] = _gla_cumsum(g, d == 1) * math.log2(math.e)
                k_sc[d, rows, :] = kk
            if q_ref is not None:
                q_sc[rows, :] = _silu(q_ref[rows, :].astype(F32)) * q_scale
            return carry
        lax.fori_loop(0, n, body, 0, unroll=min(2, n))

    prepare((ffc, fbc), qc, bc, kc, qsc, n_ctx)
    prepare((ffl, fbl), ql, bl, kl, qsl, n_lat)

    def one_dir(d, b_sc, k_sc, i_ref, q_sc, o_ref, idx, st):
        rev = d == 1
        rows = chunk_rows(idx)
        b, kk, vv = b_sc[d, rows, :], k_sc[d, rows, :], i_ref[rows, :]
        if q_sc is not None:
            o_ref[rows, :] += _gla_chunk_out(b, kk, vv, q_sc[rows, :], st, rev, lv_ref[d],
                                             bpr if rev else bpf)
        return _gla_state_update(st, b, kk, vv, rev)

    def ctx_body(j, carry):
        return (one_dir(0, bc, kc, ic, qsc, oc_ref, j, carry[0]),
                one_dir(1, bc, kc, ic, qsc, oc_ref, n_ctx - 1 - j, carry[1]))

    def lat_body(j, carry):
        return (one_dir(0, bl, kl, il, qsl, ol_ref, j, carry[0]),
                one_dir(1, bl, kl, il, qsl, ol_ref, n_lat - 1 - j, carry[1]))

    zero = jnp.zeros((HG_HK, HG_HK), F32)
    carry = lax.fori_loop(0, n_ctx, ctx_body, (zero, zero), unroll=min(2, n_ctx))
    lax.fori_loop(0, n_lat, lat_body, carry, unroll=min(2, n_lat))


def _gla(p_lat, p_ctx, gl_tab, batch, need_ctx_out):
    l = p_lat.shape[0] // batch
    lc = p_ctx.shape[0] // batch
    w = HG_HK
    c = GLA_CHUNK

    def col(base):
        return lambda b, h: (b, base + h)

    lat_specs = [pl.BlockSpec((l, w), col(cb)) for cb in (COL_FF, COL_FB, COL_HI, COL_HQ)]
    ctx_cols = (COL_FF, COL_FB, COL_HI) + ((COL_HQ,) if need_ctx_out else ())
    ctx_specs = [pl.BlockSpec((lc, w), col(cb)) for cb in ctx_cols]
    in_specs = lat_specs + ctx_specs + [pl.BlockSpec((2, 8, w), lambda b, h: (0, 0, h)),
                                        pl.BlockSpec((2, c, c), lambda b, h: (0, 0, 0))]
    args = [p_lat] * 4 + [p_ctx] * len(ctx_cols) + [gl_tab, jnp.asarray(_gla_level_table())]
    out_specs = [pl.BlockSpec((l, w), lambda b, h: (b, h))]
    out_shape = [jax.ShapeDtypeStruct((batch * l, HG_HEADS * w), F32)]
    scratch = [pltpu.VMEM((2, l, w), F32), pltpu.VMEM((2, l, w), F32),
               pltpu.VMEM((2, lc, w), F32), pltpu.VMEM((2, lc, w), F32), pltpu.VMEM((l, w), F32)]
    if need_ctx_out:
        out_specs.append(pl.BlockSpec((lc, w), lambda b, h: (b, h)))
        out_shape.append(jax.ShapeDtypeStruct((batch * lc, HG_HEADS * w), F32))
        scratch.append(pltpu.VMEM((lc, w), F32))
    scratch += [pltpu.VMEM((c + 2 * GLA_PAD, w), F32)] * 2
    outs = pl.pallas_call(
        functools.partial(_gla_kernel, need_ctx_out=need_ctx_out, n_lat=l // c, n_ctx=lc // c),
        grid=(batch, HG_HEADS),
        in_specs=in_specs,
        out_specs=out_specs,
        out_shape=out_shape,
        scratch_shapes=scratch,
        compiler_params=_cparams(("arbitrary", "arbitrary")),
        name="hgrn2_scan",
    )(*args)
    return outs[0], (outs[1] if need_ctx_out else None)


def _rotate(x, cos, sin):
    half = x.shape[-1] // 2
    x1, x2 = x[:, :half], x[:, half:]
    return jnp.concatenate([x1 * cos - x2 * sin, x1 * sin + x2 * cos], axis=-1)


def _ret_kernel(*refs, need_ctx_out, n_lat, n_ctx):
    if need_ctx_out:
        (ql, kl, vl, gl, qc, kc, vc, gc, cos_ref, sin_ref, intra_ref, qk_ref, sd_ref,
         yl_ref, yc_ref, qrl, krl, qrc, krc, ol, oc, s_fwd, s_rev) = refs
    else:
        (ql, kl, vl, gl, kc, vc, cos_ref, sin_ref, intra_ref, qk_ref, sd_ref,
         yl_ref, qrl, krl, krc, ol, s_fwd, s_rev) = refs
        qc = gc = yc_ref = qrc = oc = None
    s_refs = (s_fwd, s_rev)
    c = RET_CHUNK
    lc = n_ctx * c
    k_scale = RT_HK ** -0.5

    def rot_all(src, dst, n, pos0, scale):
        def body(j, carry):
            rows = pl.ds(pl.multiple_of(j * c, c), c)
            prow = pl.ds(pl.multiple_of(pos0 + j * c, c), c)
            x = src[rows, :].astype(F32)
            dst[rows, :] = (_rotate(x, cos_ref[prow, :], sin_ref[prow, :]) * scale).astype(BF16)
            return carry
        lax.fori_loop(0, n, body, 0)

    rot_all(kl, krl, n_lat, lc, k_scale)
    rot_all(kc, krc, n_ctx, 0, k_scale)
    rot_all(ql, qrl, n_lat, lc, 1.0)
    if need_ctx_out:
        rot_all(qc, qrc, n_ctx, 0, 1.0)

    for s_ref in s_refs:
        s_ref[...] = jnp.zeros(s_ref.shape, F32)
    ol[...] = jnp.zeros(ol.shape, F32)
    if need_ctx_out:
        oc[...] = jnp.zeros(oc.shape, F32)

    def update(d, kr, v_ref, rows):
        kd = qk_ref[d, 0, c:2 * c, :]
        sdec = sd_ref[d, 0, 0:1, :]
        kh = (kr[rows, :].astype(F32) * jnp.concatenate([kd, kd], axis=1)).astype(BF16)
        s_ref = s_refs[d]
        s_ref[...] = (s_ref[...] * jnp.concatenate([sdec] * (RT_HV // 128), axis=1)
                      + _dot_tn(kh, v_ref[rows, :]))

    def out_step(d, qr, kr, v_ref, o_ref, rows):
        qd = qk_ref[d, 0, 0:c, :]
        qcb = qr[rows, :]
        sc = _dot_nt(qcb, kr[rows, :]) * intra_ref[d, 0]
        qh = (qcb.astype(F32) * jnp.concatenate([qd, qd], axis=1)).astype(BF16)
        o_ref[rows, :] += (_dot(qh, s_refs[d][...].astype(BF16))
                           + _dot(sc.astype(BF16), v_ref[rows, :]))
        update(d, kr, v_ref, rows)

    def ctx_body(j, carry):
        for d in range(2):
            idx = (n_ctx - 1 - j) if d == 1 else j
            rows = pl.ds(pl.multiple_of(idx * c, c), c)
            if need_ctx_out:
                out_step(d, qrc, krc, vc, oc, rows)
            else:
                update(d, krc, vc, rows)
        return carry

    def lat_body(j, carry):
        for d in range(2):
            idx = (n_lat - 1 - j) if d == 1 else j
            out_step(d, qrl, krl, vl, ol, pl.ds(pl.multiple_of(idx * c, c), c))
        return carry

    lax.fori_loop(0, n_ctx, ctx_body, 0)
    lax.fori_loop(0, n_lat, lat_body, 0, unroll=2)

    def readout(o_ref, g_ref, y_ref, n):
        def body(j, carry):
            rows = pl.ds(pl.multiple_of(j * c, c), c)
            o = o_ref[rows, :]
            y = o * lax.rsqrt(jnp.mean(o * o, axis=-1, keepdims=True) + LN_EPS)
            y_ref[rows, :] = (y * _silu(g_ref[rows, :].astype(F32))).astype(y_ref.dtype)
            return carry
        lax.fori_loop(0, n, body, 0)

    readout(ol, gl, yl_ref, n_lat)
    if need_ctx_out:
        readout(oc, gc, yc_ref, n_ctx)


def _retention_tables(lc, l):
    half = RT_HK // 2
    inv = 1.0 / (RT_ROPE_BASE ** jnp.linspace(0.0, 1.0, half, dtype=F32))
    ang = jnp.arange(lc + l, dtype=F32)[:, None] * inv[None, :]
    j = jnp.arange(2 * RT_HEADS, dtype=F32)
    lg_all = jnp.log1p(-jnp.exp2(-5.0 - j))
    c = RET_CHUNK
    pos = jnp.arange(c, dtype=F32)
    rel = pos[:, None] - pos[None, :]
    intra, qk, sd = [], [], []
    for d in range(2):
        lg = lg_all[d::2]
        m = jnp.where(rel >= 0, jnp.exp(jnp.maximum(rel, 0.0)[None] * lg[:, None, None]), 0.0)
        qdec = jnp.exp((pos + 1.0)[None, :] * lg[:, None])
        kdec = jnp.exp((c - 1.0 - pos)[None, :] * lg[:, None])
        if d == 1:
            m = jnp.swapaxes(m, 1, 2)
            qdec = qdec[:, ::-1]
            kdec = kdec[:, ::-1]
        intra.append(m)
        qk.append(jnp.broadcast_to(jnp.concatenate([qdec, kdec], axis=1)[:, :, None],
                                   (RT_HEADS, 2 * c, 128)))
        sd.append(jnp.broadcast_to(jnp.exp(c * lg)[:, None, None], (RT_HEADS, 8, 128)))
    return (jnp.cos(ang), jnp.sin(ang), jnp.stack(intra), jnp.stack(qk), jnp.stack(sd))


def _retention(p_lat, p_ctx, tables, batch, need_ctx_out):
    l = p_lat.shape[0] // batch
    lc = p_ctx.shape[0] // batch
    cos, sin, intra, qk, sd = tables
    c = RET_CHUNK

    def col(base):
        return lambda b, h: (b, base + h)

    kq, kk_, kv, kg = COL_RQ * 128 // RT_HK, COL_RK * 128 // RT_HK, COL_RV * 128 // RT_HV, COL_RG * 128 // RT_HV
    lat_specs = [pl.BlockSpec((l, RT_HK), col(kq)), pl.BlockSpec((l, RT_HK), col(kk_)),
                 pl.BlockSpec((l, RT_HV), col(kv)), pl.BlockSpec((l, RT_HV), col(kg))]
    if need_ctx_out:
        ctx_specs = [pl.BlockSpec((lc, RT_HK), col(kq)), pl.BlockSpec((lc, RT_HK), col(kk_)),
                     pl.BlockSpec((lc, RT_HV), col(kv)), pl.BlockSpec((lc, RT_HV), col(kg))]
    else:
        ctx_specs = [pl.BlockSpec((lc, RT_HK), col(kk_)), pl.BlockSpec((lc, RT_HV), col(kv))]
    tab_specs = [pl.BlockSpec((lc + l, 128), lambda b, h: (0, 0)),
                 pl.BlockSpec((lc + l, 128), lambda b, h: (0, 0)),
                 pl.BlockSpec((2, 1, c, c), lambda b, h: (0, h, 0, 0)),
                 pl.BlockSpec((2, 1, 2 * c, 128), lambda b, h: (0, h, 0, 0)),
                 pl.BlockSpec((2, 1, 8, 128), lambda b, h: (0, h, 0, 0))]
    args = [p_lat] * 4 + [p_ctx] * len(ctx_specs) + [cos, sin, intra, qk, sd]
    out_specs = [pl.BlockSpec((l, RT_HV), lambda b, h: (b, h))]
    out_shape = [jax.ShapeDtypeStruct((batch * l, RT_HEADS * RT_HV), BF16)]
    scratch = [pltpu.VMEM((l, RT_HK), BF16), pltpu.VMEM((l, RT_HK), BF16)]
    if need_ctx_out:
        out_specs.append(pl.BlockSpec((lc, RT_HV), lambda b, h: (b, h)))
        out_shape.append(jax.ShapeDtypeStruct((batch * lc, RT_HEADS * RT_HV), BF16))
        scratch += [pltpu.VMEM((lc, RT_HK), BF16), pltpu.VMEM((lc, RT_HK), BF16),
                    pltpu.VMEM((l, RT_HV), F32), pltpu.VMEM((lc, RT_HV), F32)]
    else:
        scratch += [pltpu.VMEM((lc, RT_HK), BF16), pltpu.VMEM((l, RT_HV), F32)]
    scratch += [pltpu.VMEM((RT_HK, RT_HV), F32)] * 2
    outs = pl.pallas_call(
        functools.partial(_ret_kernel, need_ctx_out=need_ctx_out, n_lat=l // c, n_ctx=lc // c),
        grid=(batch, RT_HEADS),
        in_specs=lat_specs + ctx_specs + tab_specs,
        out_specs=out_specs,
        out_shape=out_shape,
        scratch_shapes=scratch,
        compiler_params=_cparams(("arbitrary", "arbitrary")),
        name="retention_scan",
    )(*args)
    return outs[0], (outs[1] if need_ctx_out else None)


def _short_conv_kernel(x_ref, w_ref, b_ref, o_ref, xpad_ref, *, grid):
    l, tc = x_ref.shape
    p = CONV_PAD
    zeros = jnp.zeros((p, tc), F32)
    xpad_ref[0:p, :] = zeros
    xpad_ref[p + l:p + l + p, :] = zeros
    xpad_ref[p:p + l, :] = x_ref[...].astype(F32)
    colw = lax.broadcasted_iota(jnp.int32, (l, tc), 0) % GRID_W
    out = jnp.broadcast_to(b_ref[...], (l, tc))
    for dj in range(3):
        acc = None
        for di in (range(3) if grid else (1,)):
            off = p + (di - 1) * GRID_W + (dj - 1)
            t = w_ref[di * 3 + dj:di * 3 + dj + 1, :] * xpad_ref[off:off + l, :]
            acc = t if acc is None else acc + t
        if grid and dj == 0:
            acc = jnp.where(colw >= 1, acc, 0.0)
        if grid and dj == 2:
            acc = jnp.where(colw <= GRID_W - 2, acc, 0.0)
        out = out + acc
    o_ref[...] = out.astype(o_ref.dtype)


def _short_conv(p_arr, w9, bias, batch, grid):
    l = p_arr.shape[0] // batch
    c3 = 3 * HY_DIM
    tc = 256
    base = COL_HY * 128 // tc
    return pl.pallas_call(
        functools.partial(_short_conv_kernel, grid=grid),
        grid=(batch, c3 // tc),
        in_specs=[pl.BlockSpec((l, tc), lambda b, j: (b, base + j)),
                  pl.BlockSpec((9, tc), lambda b, j: (0, j)),
                  pl.BlockSpec((1, tc), lambda b, j: (0, j))],
        out_specs=pl.BlockSpec((l, tc), lambda b, j: (b, j)),
        out_shape=jax.ShapeDtypeStruct((batch * l, c3), BF16),
        scratch_shapes=[pltpu.VMEM((l + 2 * CONV_PAD, tc), F32)],
        compiler_params=_cparams(("arbitrary", "arbitrary")),
        name="hyena_short_conv",
    )(p_arr, w9, bias.reshape(1, c3))


def _dft_matrices(l):
    n = 2 * l
    s = int(round(math.sqrt(n)))
    while n % s:
        s -= 1
    q = n // s
    f = np.arange(l, dtype=np.int64)
    ang_a = 2.0 * np.pi * ((f[:, None] * s * np.arange(q)[None, :]) % n) / n
    ang_b = 2.0 * np.pi * ((f[:, None] * np.arange(s)[None, :]) % n) / n
    t = np.arange(n)
    rep = (t[None, :] // s == np.arange(q)[:, None]).astype(np.float32)
    til = (t[None, :] % s == np.arange(s)[:, None]).astype(np.float32)
    h = DFT_HALF
    row = lambda j: (j, 0)
    const = lambda j: (0, 0)
    fwd, inv = pl.pallas_call(
        _dft_build_kernel,
        grid=(l // h,),
        in_specs=[pl.BlockSpec((h, q), row), pl.BlockSpec((h, q), row),
                  pl.BlockSpec((h, s), row), pl.BlockSpec((h, s), row),
                  pl.BlockSpec((q, n), const), pl.BlockSpec((s, n), const)],
        out_specs=[pl.BlockSpec((1, 2, h, n), lambda j: (j, 0, 0, 0)),
                   pl.BlockSpec((h, n), row)],
        out_shape=[jax.ShapeDtypeStruct((l // h, 2, h, n), BF16),
                   jax.ShapeDtypeStruct((l, n), BF16)],
        compiler_params=_cparams(("arbitrary",)),
        name="dft_build",
    )(jnp.asarray(np.cos(ang_a), F32), jnp.asarray(np.sin(ang_a), F32),
      jnp.asarray(np.cos(ang_b), F32), jnp.asarray(np.sin(ang_b), F32),
      jnp.asarray(rep, BF16), jnp.asarray(til, BF16))
    return fwd.reshape(n, n), inv


def _dft_build_kernel(ca_ref, sa_ref, cb_ref, sb_ref, rep_ref, til_ref, fwd_ref, inv_ref):
    h, n = inv_ref.shape

    def spread(x_ref, m_ref):
        p1, p2, p3 = _split3(x_ref[...])
        m = m_ref[...]
        return _dot(p1, m) + _dot(p2, m) + _dot(p3, m)

    ca, sa = spread(ca_ref, rep_ref), spread(sa_ref, rep_ref)
    cb, sb = spread(cb_ref, til_ref), spread(sb_ref, til_ref)
    cosm = ca * cb - sa * sb
    nsin = -(sa * cb + ca * sb)
    first = pl.program_id(0) == 0
    rowi = lax.broadcasted_iota(jnp.int32, (h, n), 0)
    coli = lax.broadcasted_iota(jnp.int32, (h, n), 1)
    alt_col = (1 - 2 * (coli % 2)).astype(F32)
    fwd_ref[0, 0] = cosm.astype(BF16)
    fwd_ref[0, 1] = jnp.where(first & (rowi == 0), alt_col, nsin).astype(BF16)
    rowg = lax.broadcasted_iota(jnp.int32, (h, h), 0) + pl.program_id(0) * h
    col0 = lax.broadcasted_iota(jnp.int32, (h, h), 1) == 0
    alt_row = (1 - 2 * (rowg % 2)).astype(F32)
    pieces = []
    for jj in range(n // (2 * h)):
        cp = cosm[:, jj * h:(jj + 1) * h] * (2.0 / n)
        ip = nsin[:, jj * h:(jj + 1) * h] * (2.0 / n)
        if jj == 0:
            cp = jnp.where(col0, 1.0 / n, cp)
            ip = jnp.where(col0, alt_row * (1.0 / n), ip)
        pieces += [cp.astype(BF16), ip.astype(BF16)]
    inv_ref[...] = jnp.concatenate(pieces, axis=1)


def _filter_spectrum_kernel(f_ref, top_ref, bot_ref, o_ref):
    h = DFT_HALF
    bl = top_ref.shape[0]
    acc = _dot(f_ref[:, 0:bl], top_ref[...]) + _dot(f_ref[:, bl:2 * bl], bot_ref[...])
    kr, km = acc[:h], acc[h:]
    row0 = (lax.broadcasted_iota(jnp.int32, kr.shape, 0) == 0) & (pl.program_id(2) == 0)
    o_ref[0, 0:h, :] = kr.astype(o_ref.dtype)
    o_ref[0, h:2 * h, :] = jnp.where(row0, 0.0, km).astype(o_ref.dtype)
    o_ref[0, 2 * h:3 * h, :] = jnp.where(row0, km, kr).astype(o_ref.dtype)


def _filter_spectrum(fwd, kern_bf, nb):
    n = fwd.shape[0]
    bl = n // 2
    nc = kern_bf.shape[1]
    tm, tn = 2 * DFT_HALF, 1024
    nd = 2 * nb - 1
    wrap = 2 * nb
    return pl.pallas_call(
        _filter_spectrum_kernel,
        grid=(nc // tn, nd, n // tm),
        in_specs=[pl.BlockSpec((tm, n), lambda c, d, t: (t, 0)),
                  pl.BlockSpec((bl, tn), lambda c, d, t: ((d - (nb - 1)) % wrap, c)),
                  pl.BlockSpec((bl, tn), lambda c, d, t: ((d - nb) % wrap, c))],
        out_specs=pl.BlockSpec((1, 3 * DFT_HALF, tn), lambda c, d, t: (d, t, c)),
        out_shape=jax.ShapeDtypeStruct((nd, (n // tm) * 3 * DFT_HALF, nc), BF16),
        compiler_params=_cparams(("arbitrary", "arbitrary", "arbitrary")),
        name="hyena_filter_dft",
    )(fwd, kern_bf, kern_bf)


def _dft_fwd_kernel(f_ref, z_ref, k_ref, y_ref):
    h = DFT_HALF
    bl = f_ref.shape[1]
    nb = y_ref.shape[1]
    f = f_ref[...]
    spec = [_dot(f, z_ref[j * bl:(j + 1) * bl, :]).astype(BF16) for j in range(nb)]
    for i in range(nb):
        yr = yi = None
        for j in range(nb):
            d = i - j + nb - 1
            kr, ki, kr2 = k_ref[d, 0:h, :], k_ref[d, h:2 * h, :], k_ref[d, 2 * h:3 * h, :]
            ur, ui = spec[j][:h], spec[j][h:]
            tr = ur * kr - ui * ki
            ti = ur * ki + ui * kr2
            yr = tr if yr is None else yr + tr
            yi = ti if yi is None else yi + ti
        y_ref[0, i, 0:h, :] = yr.astype(y_ref.dtype)
        y_ref[0, i, h:2 * h, :] = yi.astype(y_ref.dtype)


def _dft_fwd(fwd, z_arr, z_col, kf, k_col, batch, nb):
    n = fwd.shape[0]
    bl = n // 2
    l = nb * bl
    tm, tn = 2 * DFT_HALF, 512
    cpt = HY_DIM // tn
    nd = kf.shape[0]
    return pl.pallas_call(
        _dft_fwd_kernel,
        grid=(cpt, n // tm, batch),
        in_specs=[pl.BlockSpec((tm, bl), lambda c, t, b: (t, 0)),
                  pl.BlockSpec((l, tn), lambda c, t, b: (b, z_col * cpt + c)),
                  pl.BlockSpec((nd, 3 * DFT_HALF, tn), lambda c, t, b: (0, t, k_col * cpt + c))],
        out_specs=pl.BlockSpec((1, nb, tm, tn), lambda c, t, b: (b, 0, t, c)),
        out_shape=jax.ShapeDtypeStruct((batch, nb, n, HY_DIM), BF16),
        compiler_params=_cparams(("arbitrary", "arbitrary", "arbitrary")),
        name="hyena_dft_fwd",
    )(fwd, z_arr, kf)


def _dft_inv_kernel(g_ref, y_ref, gate_ref, z_ref, skip_ref, o_ref):
    conv = _dot(g_ref[...], y_ref[0, 0])
    z = z_ref[...].astype(F32)
    o_ref[...] = (gate_ref[...].astype(F32) * (conv + skip_ref[...] * z)).astype(o_ref.dtype)


def _dft_inv(inv, y, gate_arr, gate_col, z_arr, z_col, skip, batch, nb):
    bl, n = inv.shape
    tn = HY_DIM
    return pl.pallas_call(
        _dft_inv_kernel,
        grid=(batch, nb),
        in_specs=[pl.BlockSpec((bl, n), lambda b, i: (0, 0)),
                  pl.BlockSpec((1, 1, n, tn), lambda b, i: (b, i, 0, 0)),
                  pl.BlockSpec((bl, tn), lambda b, i: (b * nb + i, gate_col)),
                  pl.BlockSpec((bl, tn), lambda b, i: (b * nb + i, z_col)),
                  pl.BlockSpec((1, tn), lambda b, i: (0, 0))],
        out_specs=pl.BlockSpec((bl, tn), lambda b, i: (b * nb + i, 0)),
        out_shape=jax.ShapeDtypeStruct((batch * nb * bl, tn), BF16),
        compiler_params=_cparams(("arbitrary", "arbitrary")),
        name="hyena_dft_inv",
    )(inv, y, gate_arr, z_arr, skip.reshape(1, tn))


def _tap_features(l):
    f32 = np.float32
    t01 = np.linspace(0.0, 1.0, l, dtype=f32)[:, None]
    ang = f32(2.0 * math.pi) * np.arange(l, dtype=f32)[:, None] / f32(l)
    bands = np.linspace(1e-4, HY_BANDS - 1, HY_BANDS, dtype=f32)[None, :]
    z = np.concatenate([t01, np.cos(bands * ang), -np.sin(bands * ang)], axis=-1).astype(f32)
    pos = np.concatenate([np.arange(l), [0], np.arange(l - 1, 0, -1)])
    zz = np.zeros((2 * l, 128), f32)
    zz[:, :z.shape[1]] = z[pos]
    zz[l] = 0.0
    return zz


def _dot_hi(a, b):
    a1 = a.astype(BF16)
    a2 = (a - a1.astype(F32)).astype(BF16)
    b1 = b.astype(BF16)
    b2 = (b - b1.astype(F32)).astype(BF16)
    return _dot(a1, b1) + _dot(a1, b2) + _dot(a2, b1)


def _filter_kernel(zz_ref, w1_ref, b1_ref, fr_ref, w2_ref, b2_ref, w3f_ref, w3b_ref, dl_ref,
                   o_ref, hdn_ref):
    n = zz_ref.shape[0]
    l = n // 2

    @pl.when((pl.program_id(0) == 0) & (pl.program_id(1) == 0))
    def _():
        h1 = jnp.sin(fr_ref[0:1, :] * (_dot_hi(zz_ref[...], w1_ref[...]) + b1_ref[...]))
        hdn_ref[...] = jnp.sin(fr_ref[1:2, :] * (_dot_hi(h1, w2_ref[...]) + b2_ref[...]))

    filt = jnp.concatenate([_dot_hi(hdn_ref[0:l, :], w3f_ref[...]),
                            _dot_hi(hdn_ref[l:n, :], w3b_ref[...])], axis=0)
    decay = jnp.exp(-zz_ref[:, 0:1] * dl_ref[...])
    rowi = lax.broadcasted_iota(jnp.int32, filt.shape, 0)
    kern = jnp.where(rowi == l, 0.0, filt * decay)
    o_ref[...] = (kern / jnp.sum(jnp.abs(kern), axis=0, keepdims=True)).astype(o_ref.dtype)


def _hyena_filters(l, w1, b1, freq, w2, b2, w3):
    n = 2 * l
    tn = 256
    ff = w2.shape[0]
    cpo = HY_DIM // tn
    deltas = np.abs(np.linspace(math.log(HY_DECAY_TARGET) / HY_DECAY_PCT_LONG,
                                math.log(HY_DECAY_TARGET) / HY_DECAY_PCT_SHORT, HY_DIM,
                                dtype=np.float32)).reshape(1, HY_DIM)
    const = lambda o, c: (0, 0)
    return pl.pallas_call(
        _filter_kernel,
        grid=(HY_ORDER, cpo),
        in_specs=[pl.BlockSpec((n, 128), const), pl.BlockSpec((128, ff), const),
                  pl.BlockSpec((1, ff), const), pl.BlockSpec((2, ff), const),
                  pl.BlockSpec((ff, ff), const), pl.BlockSpec((1, ff), const),
                  pl.BlockSpec((ff, tn), lambda o, c: (0, o * 2 * cpo + c)),
                  pl.BlockSpec((ff, tn), lambda o, c: (0, o * 2 * cpo + cpo + c)),
                  pl.BlockSpec((1, tn), lambda o, c: (0, c))],
        out_specs=pl.BlockSpec((n, tn), lambda o, c: (0, o * cpo + c)),
        out_shape=jax.ShapeDtypeStruct((n, HY_ORDER * HY_DIM), BF16),
        scratch_shapes=[pltpu.VMEM((n, ff), F32)],
        compiler_params=_cparams(("arbitrary", "arbitrary")),
        name="hyena_filter_mlp",
    )(jnp.asarray(_tap_features(l)), jnp.pad(w1, ((0, 128 - w1.shape[0]), (0, 0))),
      b1.reshape(1, ff), freq, w2, b2.reshape(1, ff), w3, w3, jnp.asarray(deltas))


def _hyena(p_arr, lp, dft, batch, grid):
    l = p_arr.shape[0] // batch
    fwd, inv = dft
    uc = _short_conv(p_arr, lp['hy_conv_w'].reshape(9, 3 * HY_DIM), lp['hy_conv_b'], batch, grid)
    kern = _hyena_filters(l, lp['hy_ff_w1'], lp['hy_ff_b1'], lp['hy_ff_freq'],
                          lp['hy_ff_w2'], lp['hy_ff_b2'], lp['hy_ff_w3'])
    nb = l // inv.shape[0]
    kf = _filter_spectrum(fwd, kern, nb)
    z_arr, z_col = uc, 0
    for n in range(HY_ORDER):
        y = _dft_fwd(fwd, z_arr, z_col, kf, n, batch, nb)
        z_arr = _dft_inv(inv, y, uc, n + 1, z_arr, z_col, lp['hy_skip'][n], batch, nb)
        z_col = 0
    return z_arr


def _merge_kernel(yhy_ref, ohg_ref, hgg_ref, yrt_ref, b0_ref, b1_ref, b2_ref, x_ref, mod_ref,
                  nw_ref, phy_ref, phg_ref, prt_ref, wo_ref, lng_ref, lnb_ref, o_ref):
    o = ohg_ref[...]
    yhg = (o * lax.rsqrt(jnp.mean(o * o, axis=-1, keepdims=True) + LN_EPS) * nw_ref[...]
           * _silu(hgg_ref[...].astype(F32)))
    m = (jax.nn.sigmoid(b0_ref[...].astype(F32)) * _dot(yhy_ref[...], phy_ref[...])
         + jax.nn.sigmoid(b1_ref[...].astype(F32)) * _dot(yhg.astype(BF16), phg_ref[...])
         + jax.nn.sigmoid(b2_ref[...].astype(F32)) * _dot(yrt_ref[...], prt_ref[...]))
    t = _dot(m.astype(BF16), wo_ref[...])
    gt1 = mod_ref[0][2:3]
    o_ref[...] = _layer_norm_rows(DN_ALPHA * x_ref[...] + gt1 * t, lng_ref[...], lnb_ref[...])


def _merge(y_hy, o_hg, y_rt, p_arr, x, modp, lp, wbf, rows_per_batch):
    m = x.shape[0]
    tm = min(256, m)
    d = D_MODEL
    mi = _mod_index(rows_per_batch, tm)
    row = lambda i: (i, 0)
    const = lambda i: (0, 0)
    gcol = COL_HG * 128 // d
    bcol = COL_BR * 128 // d
    return pl.pallas_call(
        _merge_kernel,
        grid=(m // tm,),
        in_specs=[pl.BlockSpec((tm, d), row), pl.BlockSpec((tm, d), row),
                  pl.BlockSpec((tm, d), lambda i: (i, gcol)),
                  pl.BlockSpec((tm, 2 * d), row),
                  pl.BlockSpec((tm, d), lambda i: (i, bcol)),
                  pl.BlockSpec((tm, d), lambda i: (i, bcol + 1)),
                  pl.BlockSpec((tm, d), lambda i: (i, bcol + 2)),
                  pl.BlockSpec((tm, d), row),
                  pl.BlockSpec((1, 8, d), lambda i: (mi(i), 0, 0)),
                  pl.BlockSpec((1, d), const),
                  pl.BlockSpec((d, d), const), pl.BlockSpec((d, d), const),
                  pl.BlockSpec((2 * d, d), const), pl.BlockSpec((d, d), const),
                  pl.BlockSpec((1, d), const), pl.BlockSpec((1, d), const)],
        out_specs=pl.BlockSpec((tm, d), row),
        out_shape=jax.ShapeDtypeStruct((m, d), F32),
        compiler_params=_cparams(("arbitrary",)),
        name="merge_out_ln",
    )(y_hy, o_hg, p_arr, y_rt, p_arr, p_arr, p_arr, x, modp, lp['hg_norm_w'].reshape(1, d),
      wbf['p_hy'], wbf['p_hg'], wbf['p_rt'], wbf['w_o'],
      lp['ln1_g'].reshape(1, d), lp['ln1_b'].reshape(1, d))


def _ffn_kernel(x_ref, mod_ref, w1_ref, w3_ref, w2_ref, lng_ref, lnb_ref, o_ref, h_ref, acc_ref):
    k = pl.program_id(1)
    m = mod_ref[0]

    @pl.when(k == 0)
    def _():
        h_ref[...] = (x_ref[...] * (1.0 + m[4:5]) + m[3:4]).astype(BF16)
        acc_ref[...] = jnp.zeros(acc_ref.shape, F32)

    h = h_ref[...]
    u = _silu(_dot(h, w1_ref[...])) * _dot(h, w3_ref[...])
    acc_ref[...] += _dot(u.astype(BF16), w2_ref[...])

    @pl.when(k == pl.num_programs(1) - 1)
    def _():
        o_ref[...] = _layer_norm_rows(DN_ALPHA * x_ref[...] + m[5:6] * acc_ref[...],
                                      lng_ref[...], lnb_ref[...])


def _ffn_dense(x, modp, w1, w3, w2, ln_g, ln_b, rows_per_batch):
    m = x.shape[0]
    d = D_MODEL
    dff = w1.shape[1]
    tm = min(512, m)
    tf = dff // 2
    mi = _mod_index(rows_per_batch, tm)
    return pl.pallas_call(
        _ffn_kernel,
        grid=(m // tm, dff // tf),
        in_specs=[pl.BlockSpec((tm, d), lambda i, k: (i, 0)),
                  pl.BlockSpec((1, 8, d), lambda i, k: (mi(i), 0, 0)),
                  pl.BlockSpec((d, tf), lambda i, k: (0, k)),
                  pl.BlockSpec((d, tf), lambda i, k: (0, k)),
                  pl.BlockSpec((tf, d), lambda i, k: (k, 0)),
                  pl.BlockSpec((1, d), lambda i, k: (0, 0)),
                  pl.BlockSpec((1, d), lambda i, k: (0, 0))],
        out_specs=pl.BlockSpec((tm, d), lambda i, k: (i, 0)),
        out_shape=jax.ShapeDtypeStruct((m, d), F32),
        scratch_shapes=[pltpu.VMEM((tm, d), BF16), pltpu.VMEM((tm, d), F32)],
        compiler_params=_cparams(("arbitrary", "arbitrary")),
        name="ffn_dense_ln",
    )(x, modp, w1, w3, w2, ln_g.reshape(1, d), ln_b.reshape(1, d))


def _router_kernel(x_ref, mod_ref, r_ref, h_ref, lg_ref):
    m = mod_ref[0]
    h = x_ref[...] * (1.0 + m[4:5]) + m[3:4]
    h_ref[...] = h
    a1, a2, a3 = _split3(h)
    r1, r2, r3 = _split3(r_ref[...])
    lg_ref[...] = (_dot(a1, r1) + _dot(a1, r2) + _dot(a2, r1)
                   + _dot(a2, r2) + _dot(a1, r3) + _dot(a3, r1))


def _router(x, modp, router_pad, rows_per_batch):
    m = x.shape[0]
    d = D_MODEL
    tm = min(512, m)
    mi = _mod_index(rows_per_batch, tm)
    return pl.pallas_call(
        _router_kernel,
        grid=(m // tm,),
        in_specs=[pl.BlockSpec((tm, d), lambda i: (i, 0)),
                  pl.BlockSpec((1, 8, d), lambda i: (mi(i), 0, 0)),
                  pl.BlockSpec((d, 128), lambda i: (0, 0))],
        out_specs=[pl.BlockSpec((tm, d), lambda i: (i, 0)),
                   pl.BlockSpec((tm, 128), lambda i: (i, 0))],
        out_shape=[jax.ShapeDtypeStruct((m, d), F32), jax.ShapeDtypeStruct((m, 128), F32)],
        compiler_params=_cparams(("arbitrary",)),
        name="moe_router",
    )(x, modp, router_pad)


def _moe_ffn_kernel(be_ref, nu_ref, xp_ref, w1_ref, w3_ref, w2_ref, o_ref, x_ref, acc_ref):
    j = pl.program_id(0)
    k = pl.program_id(1)

    @pl.when(j < nu_ref[0])
    def _():
        @pl.when(k == 0)
        def _():
            x_ref[...] = xp_ref[...].astype(BF16)
            acc_ref[...] = jnp.zeros(acc_ref.shape, F32)

        x = x_ref[...]
        u = _silu(_dot(x, w1_ref[0].astype(BF16))) * _dot(x, w3_ref[0].astype(BF16))
        acc_ref[...] += _dot(u.astype(BF16), w2_ref[0])

        @pl.when(k == pl.num_programs(1) - 1)
        def _():
            o_ref[...] = acc_ref[...]

    @pl.when((j >= nu_ref[0]) & (k == pl.num_programs(1) - 1))
    def _():
        o_ref[...] = jnp.zeros(o_ref.shape, F32)


def _moe_ffn(xb, block_e, n_used, w1, w3, group, w2):
    ns = xb.shape[0]
    d = D_MODEL
    tm = MOE_ROWS
    dex = w1.shape[3]
    tf = dex // 4
    grid_spec = pltpu.PrefetchScalarGridSpec(
        num_scalar_prefetch=2,
        grid=(ns // tm, dex // tf),
        in_specs=[pl.BlockSpec((tm, d), lambda j, k, be, nu: (j, 0)),
                  pl.BlockSpec((None, 1, d, tf), lambda j, k, be, nu: (group, be[j], 0, k)),
                  pl.BlockSpec((None, 1, d, tf), lambda j, k, be, nu: (group, be[j], 0, k)),
                  pl.BlockSpec((1, tf, d), lambda j, k, be, nu: (be[j], k, 0))],
        out_specs=pl.BlockSpec((tm, d), lambda j, k, be, nu: (j, 0)),
        scratch_shapes=[pltpu.VMEM((tm, d), BF16), pltpu.VMEM((tm, d), F32)])
    return pl.pallas_call(
        _moe_ffn_kernel,
        grid_spec=grid_spec,
        out_shape=jax.ShapeDtypeStruct((ns, d), F32),
        compiler_params=_cparams(("arbitrary", "arbitrary")),
        name="moe_expert_ffn",
    )(block_e, n_used, xb, w1, w3, w2)


def _combine_kernel(x_ref, mod_ref, y0_ref, y1_ref, g_ref, lng_ref, lnb_ref, o_ref):
    m = mod_ref[0]
    g = g_ref[...]
    f = g[:, 0:1] * y0_ref[...] + g[:, 1:2] * y1_ref[...]
    o_ref[...] = _layer_norm_rows(DN_ALPHA * x_ref[...] + m[5:6] * f, lng_ref[...], lnb_ref[...])


def _moe_combine(x, modp, y0, y1, gate_pad, ln_g, ln_b, rows_per_batch):
    m = x.shape[0]
    d = D_MODEL
    tm = min(512, m)
    mi = _mod_index(rows_per_batch, tm)
    row = lambda i: (i, 0)
    return pl.pallas_call(
        _combine_kernel,
        grid=(m // tm,),
        in_specs=[pl.BlockSpec((tm, d), row),
                  pl.BlockSpec((1, 8, d), lambda i: (mi(i), 0, 0)),
                  pl.BlockSpec((tm, d), row), pl.BlockSpec((tm, d), row),
                  pl.BlockSpec((tm, 128), row),
                  pl.BlockSpec((1, d), lambda i: (0, 0)), pl.BlockSpec((1, d), lambda i: (0, 0))],
        out_specs=pl.BlockSpec((tm, d), row),
        out_shape=jax.ShapeDtypeStruct((m, d), F32),
        compiler_params=_cparams(("arbitrary",)),
        name="moe_combine_ln",
    )(x, modp, y0, y1, gate_pad, ln_g.reshape(1, d), ln_b.reshape(1, d))


def _cast_kernel(x_ref, o_ref):
    o_ref[...] = x_ref[...].astype(o_ref.dtype)


def _to_bf16(w, group):
    _, e, a, b = w.shape
    ta = a // 4
    return pl.pallas_call(
        _cast_kernel,
        grid=(e, a // ta),
        in_specs=[pl.BlockSpec((None, 1, ta, b), lambda i, r: (group, i, r, 0))],
        out_specs=pl.BlockSpec((1, ta, b), lambda i, r: (i, r, 0)),
        out_shape=jax.ShapeDtypeStruct((e, a, b), BF16),
        compiler_params=_cparams(("arbitrary", "arbitrary")),
        name="weight_cast",
    )(w)


def _moe(x, modp, router, w1, w3, group, w2, ln_g, ln_b, rows_per_batch):
    n, d = x.shape
    e = router.shape[1]
    h2, logits = _router(x, modp, jnp.pad(router, ((0, 0), (0, 128 - e))), rows_per_batch)
    top_val, top_idx = lax.top_k(logits[:, :e], TOP_K)
    gate = jax.nn.softmax(top_val, axis=-1)
    flat_e = top_idx.reshape(-1)
    flat_t = jnp.repeat(jnp.arange(n, dtype=jnp.int32), TOP_K)
    order = jnp.argsort(flat_e, stable=True).astype(jnp.int32)
    rank = jnp.argsort(order).astype(jnp.int32)
    counts = jnp.sum((flat_e[:, None] == jnp.arange(e)[None, :]).astype(jnp.int32), axis=0)
    starts = jnp.cumsum(counts) - counts
    padded = (counts + MOE_ROWS - 1) // MOE_ROWS * MOE_ROWS
    pad_end = jnp.cumsum(padded)
    pad_start = pad_end - padded
    n_blocks = -(-(n * TOP_K) // MOE_ROWS) + e
    n_slots = n_blocks * MOE_ROWS
    block_start = jnp.arange(n_blocks) * MOE_ROWS
    block_e = jnp.minimum(jnp.sum(block_start[:, None] >= pad_end[None, :], axis=1), e - 1).astype(jnp.int32)
    n_used = (pad_end[-1] // MOE_ROWS).astype(jnp.int32).reshape(1)
    slot_e = jnp.repeat(block_e, MOE_ROWS)
    slot_off = jnp.arange(n_slots, dtype=jnp.int32) - pad_start[slot_e]
    slot_valid = slot_off < counts[slot_e]
    slot_src = jnp.where(slot_valid, starts[slot_e] + slot_off, 0)
    slot_tok = jnp.where(slot_valid, flat_t[order[slot_src]], 0).astype(jnp.int32)
    pos = (pad_start[flat_e] + rank - starts[flat_e]).astype(jnp.int32).reshape(n, TOP_K)
    xb = jnp.take(h2, slot_tok, axis=0, mode="clip")
    yb = _moe_ffn(xb, block_e, n_used, w1, w3, group, w2)
    y0 = jnp.take(yb, pos[:, 0], axis=0, mode="clip")
    y1 = jnp.take(yb, pos[:, 1], axis=0, mode="clip")
    gate_pad = jnp.pad(gate, ((0, 0), (0, 128 - TOP_K)))
    return _moe_combine(x, modp, y0, y1, gate_pad, ln_g, ln_b, rows_per_batch)


def kernel(x, c, ctx, c_ctx, ada_w, ada_b, w_in, hy_conv_w, hy_conv_b, hy_ff_w1, hy_ff_b1, hy_ff_freq, hy_ff_w2, hy_ff_b2, hy_ff_w3, hy_skip, hg_lb_logits, hg_norm_w, p_hy, p_hg, p_rt, w_o, ln1_g, ln1_b, ln2_g, ln2_b, ffn_w1, ffn_w3, ffn_w2, moe_router, moe_w1, moe_w3, moe_w2):
    batch, l, d = x.shape
    lc = ctx.shape[1]
    assert d == D_MODEL and batch <= 8
    assert l % 512 == 0 and l % GRID_W == 0 and lc % max(RET_CHUNK, DFT_HALF) == 0

    cs = jnp.cumsum(jax.nn.softmax(hg_lb_logits.astype(F32), axis=1), axis=1)
    lower_bounds = cs - cs[:, :1]
    cc = jnp.zeros((16, d), F32).at[:batch].set(c).at[8].set(c_ctx)
    ret_tables = _retention_tables(lc, l)
    dft_lat = _dft_matrices(min(HY_BLOCK, l))
    dft_ctx = _dft_matrices(min(HY_BLOCK, lc))

    x_lat = x.reshape(batch * l, d)
    x_ctx = ctx.reshape(batch * lc, d)
    for i in range(DEPTH):
        need_ctx_out = i < DEPTH - 1
        use_moe = i % 2 == 1
        g = i // 2
        lp = {'hy_conv_w': hy_conv_w[i], 'hy_conv_b': hy_conv_b[i], 'hy_ff_w1': hy_ff_w1[i],
              'hy_ff_b1': hy_ff_b1[i], 'hy_ff_freq': hy_ff_freq[i], 'hy_ff_w2': hy_ff_w2[i],
              'hy_ff_b2': hy_ff_b2[i], 'hy_ff_w3': hy_ff_w3[i], 'hy_skip': hy_skip[i],
              'hg_norm_w': hg_norm_w[i], 'ln1_g': ln1_g[i], 'ln1_b': ln1_b[i]}
        wbf = {'p_hy': p_hy[i].astype(BF16), 'p_hg': p_hg[i].astype(BF16),
               'p_rt': p_rt[i].astype(BF16), 'w_o': w_o[i].astype(BF16)}
        mod = _ada(cc, ada_w, i, ada_b[i])
        modp = jnp.pad(mod.reshape(16, 6, d), ((0, 0), (0, 2), (0, 0)))
        lb = lower_bounds[:, i]
        gl_tab = jnp.pad(jnp.stack([jnp.maximum(jnp.log(lb), -1e30), jnp.log1p(-lb), 1.0 - lb], axis=1),
                         ((0, 0), (0, 5), (0, 0)))

        p_lat = _proj(x_lat, modp, w_in, i, N_IN_COLS, l)
        p_ctx = _proj(x_ctx, modp, w_in, i, N_IN_COLS if need_ctx_out else N_STATE_COLS, None)

        o_hg_l, o_hg_c = _gla(p_lat, p_ctx, gl_tab, batch, need_ctx_out)
        y_rt_l, y_rt_c = _retention(p_lat, p_ctx, ret_tables, batch, need_ctx_out)
        y_hy_l = _hyena(p_lat, lp, dft_lat, batch, True)
        x_lat_new = _merge(y_hy_l, o_hg_l, y_rt_l, p_lat, x_lat, modp, lp, wbf, l)
        if need_ctx_out:
            y_hy_c = _hyena(p_ctx, lp, dft_ctx, batch, False)
            x_ctx = _merge(y_hy_c, o_hg_c, y_rt_c, p_ctx, x_ctx, modp, lp, wbf, None)
        x_lat = x_lat_new

        if use_moe:
            if need_ctx_out:
                raise NotImplementedError("MoE layer with a context output is not part of this trunk")
            x_lat = _moe(x_lat, modp, moe_router[g], moe_w1, moe_w3, g, _to_bf16(moe_w2, g),
                         ln2_g[i], ln2_b[i], l)
        else:
            w1, w3, w2 = ffn_w1[g].astype(BF16), ffn_w3[g].astype(BF16), ffn_w2[g].astype(BF16)
            x_lat = _ffn_dense(x_lat, modp, w1, w3, w2, ln2_g[i], ln2_b[i], l)
            if need_ctx_out:
                x_ctx = _ffn_dense(x_ctx, modp, w1, w3, w2, ln2_g[i], ln2_b[i], None)
    return x_lat.reshape(batch, l, d)
```

```python
import functools
import math

import numpy as np
import jax
import jax.numpy as jnp
from jax import lax
from jax.experimental import pallas as pl
from jax.experimental.pallas import tpu as pltpu

F32 = jnp.float32
BF16 = jnp.bfloat16

D_MODEL = 1024
DEPTH = 2
GRID_W = 64
HY_DIM = 1024
HY_ORDER = 2
HY_BANDS = 16
HY_DECAY_TARGET = 1e-2
HY_DECAY_PCT_SHORT = 0.3
HY_DECAY_PCT_LONG = 1.5
HG_HEADS = 8
HG_HK = 128
RT_HEADS = 4
RT_HK = 256
RT_HV = 512
RT_ROPE_BASE = 10000.0
N_STATE_COLS = 6144
N_IN_COLS = 17408
N_EXPERTS = 8
TOP_K = 2
DN_ALPHA = (2 * DEPTH) ** 0.25
LN_EPS = 1e-5

COL_FF, COL_FB, COL_HI, COL_RK, COL_RV = 0, 8, 16, 24, 32
COL_HQ, COL_HG, COL_RQ, COL_RG, COL_HY, COL_BR = 48, 56, 64, 72, 88, 112

GLA_CHUNK = 128
GLA_LEVELS = 7
GLA_PAD = 8
RET_CHUNK = 128
DFT_HALF = 256
HY_BLOCK = 512
CONV_PAD = 72
MOE_ROWS = 512
VMEM_LIMIT = 56 * 1024 * 1024


def _cparams(sem, flags=None):
    return pltpu.CompilerParams(dimension_semantics=sem, vmem_limit_bytes=VMEM_LIMIT, flags=flags)


def _silu(x):
    return x * jax.nn.sigmoid(x)


def _layer_norm_rows(r, g, b):
    mu = jnp.mean(r, axis=-1, keepdims=True)
    d = r - mu
    var = jnp.mean(d * d, axis=-1, keepdims=True)
    return d * lax.rsqrt(var + LN_EPS) * g + b


def _split3(x):
    h = x.astype(BF16)
    r = x - h.astype(F32)
    m = r.astype(BF16)
    l = (r - m.astype(F32)).astype(BF16)
    return h, m, l


def _dot(a, b):
    return jnp.dot(a, b, preferred_element_type=F32)


def _dot_nt(a, b):
    return lax.dot_general(a, b, (((1,), (1,)), ((), ())), preferred_element_type=F32)


def _dot_tn(a, b):
    return lax.dot_general(a, b, (((0,), (0,)), ((), ())), preferred_element_type=F32)


def _ada_kernel(c_ref, w_ref, b_ref, o_ref):
    s = _silu(c_ref[...])
    o_ref[...] = _dot(s.astype(BF16), w_ref[...].astype(BF16)) + b_ref[...]


def _ada(cc, w, layer, b):
    n = w.shape[2]
    tn = 1024
    return pl.pallas_call(
        _ada_kernel,
        grid=(n // tn,),
        in_specs=[pl.BlockSpec((16, D_MODEL), lambda j: (0, 0)),
                  pl.BlockSpec((None, D_MODEL, tn), lambda j: (layer, 0, j)),
                  pl.BlockSpec((1, tn), lambda j: (0, j))],
        out_specs=pl.BlockSpec((16, tn), lambda j: (0, j)),
        out_shape=jax.ShapeDtypeStruct((16, n), F32),
        compiler_params=_cparams(("arbitrary",)),
        name="ada_mod",
    )(cc, w, b.reshape(1, n))


def _mod_index(rows_per_batch, tm):
    if rows_per_batch is None:
        return lambda i: 8
    return lambda i: (i * tm) // rows_per_batch


def _proj_kernel(x_ref, mod_ref, w_ref, o_ref, h_ref):
    @pl.when(pl.program_id(1) == 0)
    def _():
        m = mod_ref[0]
        h_ref[...] = (x_ref[...] * (1.0 + m[1:2]) + m[0:1]).astype(BF16)

    o_ref[...] = _dot(h_ref[...], w_ref[...].astype(BF16)).astype(o_ref.dtype)


def _proj(x, modp, w, layer, n_cols, rows_per_batch):
    m = x.shape[0]
    tm = min(2048, rows_per_batch or m)
    tn = 1024
    mi = _mod_index(rows_per_batch, tm)
    return pl.pallas_call(
        _proj_kernel,
        grid=(m // tm, n_cols // tn),
        in_specs=[pl.BlockSpec((tm, D_MODEL), lambda i, j: (i, 0)),
                  pl.BlockSpec((1, 8, D_MODEL), lambda i, j: (mi(i), 0, 0)),
                  pl.BlockSpec((None, D_MODEL, tn), lambda i, j: (layer, 0, j))],
        out_specs=pl.BlockSpec((tm, tn), lambda i, j: (i, j)),
        out_shape=jax.ShapeDtypeStruct((m, n_cols), BF16),
        scratch_shapes=[pltpu.VMEM((tm, D_MODEL), BF16)],
        compiler_params=_cparams(("arbitrary", "arbitrary")),
        name="in_proj",
    )(x, modp, w)


def _gla_gates(fl, gl_ref, d):
    a = gl_ref[d, 0:1, :]
    l1p = gl_ref[d, 1:2, :]
    oml = gl_ref[d, 2:3, :]
    t = jnp.exp(-jnp.abs(fl))
    u = 1.0 + t
    x2 = l1p + jnp.minimum(fl, 0.0) - jnp.log(u)
    log_f = jnp.maximum(a, x2) + jnp.log(1.0 + jnp.exp(-jnp.abs(a - x2)))
    key = oml * jnp.where(fl >= 0.0, t, 1.0) / u
    return log_f, key


def _gla_cumsum(g, rev):
    c = g.shape[0]
    r = lax.broadcasted_iota(jnp.int32, (c, c), 0)
    u = lax.broadcasted_iota(jnp.int32, (c, c), 1)
    tri = jnp.where((u >= r) if rev else (u <= r), 1.0, 0.0).astype(BF16)
    h, m, l = _split3(g)
    return _dot(tri, h) + _dot(tri, m) + _dot(tri, l)


def _gla_state_update(st, b, kk, vv, rev):
    c = b.shape[0]
    end = 0 if rev else c - 1
    b_end = b[end:end + 1, :]
    khat = (kk * jnp.exp2(b_end - b)).astype(BF16)
    return st * jnp.exp2(b_end) + _dot_tn(vv, khat)


def _gla_level_table():
    c = GLA_CHUNK
    t = np.arange(c)[:, None]
    s = np.arange(c)[None, :]
    top_bit = np.floor(np.log2(np.maximum(t ^ s, 1))).astype(np.int32)
    fwd = np.where(t > s, top_bit, np.where(t == s, GLA_LEVELS, -1))
    return np.stack([fwd, fwd.T]).astype(np.int32)


def _gla_ref_rows(b, bpad_ref, level, rev):
    c = b.shape[0]
    half = 1 << level
    blk = 2 * half
    idx = half if rev else half - 1
    if blk % 8 == 0:
        r = b.reshape(c // blk, blk, HG_HK)[:, idx:idx + 1, :]
        return jnp.broadcast_to(r, (c // blk, blk, HG_HK)).reshape(c, HG_HK)
    m = lax.broadcasted_iota(jnp.int32, (c, HG_HK), 0) % blk
    out = b
    for v in range(blk):
        if v != idx:
            off = GLA_PAD + idx - v
            out = jnp.where(m == v, bpad_ref[off:off + c, :], out)
    return out


def _gla_chunk_out(b, kk, vv, qq, st, rev, lvt, bpad_ref):
    c = GLA_CHUNK
    o = _dot_nt((qq * jnp.exp2(b)).astype(BF16), st.astype(BF16))
    bpad_ref[GLA_PAD:GLA_PAD + c, :] = b
    sc = jnp.zeros((c, c), F32)
    for level in range(GLA_LEVELS):
        half = 1 << level
        blk = 2 * half
        ref = _gla_ref_rows(b, bpad_ref, level, rev)
        kt = (kk * jnp.exp2(ref - b)).astype(BF16)
        if half % 8:
            p = _dot_nt((qq * jnp.exp2(b - ref)).astype(BF16), kt)
            sc = jnp.where(lvt == level, p, sc)
        else:
            nblk = c // blk
            lo, hi = (0, half) if rev else (half, blk)

            def part(x):
                return x.reshape(nblk, blk, x.shape[-1])[:, lo:hi, :]

            qt = (part(qq) * jnp.exp2(part(b) - part(ref))).reshape(c // 2, HG_HK)
            p = _dot_nt(qt.astype(BF16), kt).reshape(nblk, half, c)
            new = jnp.where(part(lvt) == level, p, part(sc))
            rest = sc.reshape(nblk, blk, c)[:, half:blk, :] if rev else sc.reshape(nblk, blk, c)[:, 0:half, :]
            sc = jnp.concatenate([new, rest] if rev else [rest, new], axis=1).reshape(c, c)
    sc = jnp.where(lvt == GLA_LEVELS, _dot_nt(qq.astype(BF16), kk.astype(BF16)), sc)
    return o + _dot(sc.astype(BF16), vv)


def _gla_kernel(*refs, need_ctx_out, n_lat, n_ctx):
    if need_ctx_out:
        ffl, fbl, il, ql, ffc, fbc, ic, qc, gl_ref, lv_ref, ol_ref, oc_ref = refs[:12]
        bl, kl, bc, kc, qsl, qsc, bpf, bpr = refs[12:]
    else:
        ffl, fbl, il, ql, ffc, fbc, ic, gl_ref, lv_ref, ol_ref = refs[:10]
        bl, kl, bc, kc, qsl, bpf, bpr = refs[10:]
        qc = oc_ref = qsc = None
    c = GLA_CHUNK
    q_scale = HG_HK ** -0.5
    bpf[...] = jnp.zeros(bpf.shape, F32)
    bpr[...] = jnp.zeros(bpr.shape, F32)
    ol_ref[...] = jnp.zeros(ol_ref.shape, F32)
    if need_ctx_out:
        oc_ref[...] = jnp.zeros(oc_ref.shape, F32)

    def chunk_rows(idx):
        return pl.ds(pl.multiple_of(idx * c, c), c)

    def prepare(f_refs, q_ref, b_sc, k_sc, q_sc, n):
        def body(j, carry):
            rows = chunk_rows(j)
            for d in range(2):
                g, kk = _gla_gates(f_refs[d][rows, :].astype(F32), gl_ref, d)
                b_sc[d, rows, :] = _gla_cumsum(g, d == 1) * math.log2(math.e)
                k_sc[d, rows, :] = kk
            if q_ref is not None:
                q_sc[rows, :] = _silu(q_ref[rows, :].astype(F32)) * q_scale
            return carry
        lax.fori_loop(0, n, body, 0, unroll=min(2, n))

    prepare((ffc, fbc), qc, bc, kc, qsc, n_ctx)
    prepare((ffl, fbl), ql, bl, kl, qsl, n_lat)

    def one_dir(d, b_sc, k_sc, i_ref, q_sc, o_ref, idx, st):
        rev = d == 1
        rows = chunk_rows(idx)
        b, kk, vv = b_sc[d, rows, :], k_sc[d, rows, :], i_ref[rows, :]
        if q_sc is not None:
            o_ref[rows, :] += _gla_chunk_out(b, kk, vv, q_sc[rows, :], st, rev, lv_ref[d],
                                             bpr if rev else bpf)
        return _gla_state_update(st, b, kk, vv, rev)

    def ctx_body(j, carry):
        return (one_dir(0, bc, kc, ic, qsc, oc_ref, j, carry[0]),
                one_dir(1, bc, kc, ic, qsc, oc_ref, n_ctx - 1 - j, carry[1]))

    def lat_body(j, carry):
        return (one_dir(0, bl, kl, il, qsl, ol_ref, j, carry[0]),
                one_dir(1, bl, kl, il, qsl, ol_ref, n_lat - 1 - j, carry[1]))

    zero = jnp.zeros((HG_HK, HG_HK), F32)
    carry = lax.fori_loop(0, n_ctx, ctx_body, (zero, zero), unroll=min(2, n_ctx))
    lax.fori_loop(0, n_lat, lat_body, carry, unroll=min(2, n_lat))


def _gla(p_lat, p_ctx, gl_tab, batch, need_ctx_out):
    l = p_lat.shape[0] // batch
    lc = p_ctx.shape[0] // batch
    w = HG_HK
    c = GLA_CHUNK

    def col(base):
        return lambda b, h: (b, base + h)

    lat_specs = [pl.BlockSpec((l, w), col(cb)) for cb in (COL_FF, COL_FB, COL_HI, COL_HQ)]
    ctx_cols = (COL_FF, COL_FB, COL_HI) + ((COL_HQ,) if need_ctx_out else ())
    ctx_specs = [pl.BlockSpec((lc, w), col(cb)) for cb in ctx_cols]
    in_specs = lat_specs + ctx_specs + [pl.BlockSpec((2, 8, w), lambda b, h: (0, 0, h)),
                                        pl.BlockSpec((2, c, c), lambda b, h: (0, 0, 0))]
    args = [p_lat] * 4 + [p_ctx] * len(ctx_cols) + [gl_tab, jnp.asarray(_gla_level_table())]
    out_specs = [pl.BlockSpec((l, w), lambda b, h: (b, h))]
    out_shape = [jax.ShapeDtypeStruct((batch * l, HG_HEADS * w), F32)]
    scratch = [pltpu.VMEM((2, l, w), F32), pltpu.VMEM((2, l, w), F32),
               pltpu.VMEM((2, lc, w), F32), pltpu.VMEM((2, lc, w), F32), pltpu.VMEM((l, w), F32)]
    if need_ctx_out:
        out_specs.append(pl.BlockSpec((lc, w), lambda b, h: (b, h)))
        out_shape.append(jax.ShapeDtypeStruct((batch * lc, HG_HEADS * w), F32))
        scratch.append(pltpu.VMEM((lc, w), F32))
    scratch += [pltpu.VMEM((c + 2 * GLA_PAD, w), F32)] * 2
    outs = pl.pallas_call(
        functools.partial(_gla_kernel, need_ctx_out=need_ctx_out, n_lat=l // c, n_ctx=lc // c),
        grid=(batch, HG_HEADS),
        in_specs=in_specs,
        out_specs=out_specs,
        out_shape=out_shape,
        scratch_shapes=scratch,
        compiler_params=_cparams(("arbitrary", "arbitrary")),
        name="hgrn2_scan",
    )(*args)
    return outs[0], (outs[1] if need_ctx_out else None)


def _rotate(x, cos, sin):
    half = x.shape[-1] // 2
    x1, x2 = x[:, :half], x[:, half:]
    return jnp.concatenate([x1 * cos - x2 * sin, x1 * sin + x2 * cos], axis=-1)


def _ret_kernel(*refs, need_ctx_out, n_lat, n_ctx):
    if need_ctx_out:
        (ql, kl, vl, gl, qc, kc, vc, gc, cos_ref, sin_ref, intra_ref, qk_ref, sd_ref,
         yl_ref, yc_ref, qrl, krl, qrc, krc, ol, oc, s_fwd, s_rev) = refs
    else:
        (ql, kl, vl, gl, kc, vc, cos_ref, sin_ref, intra_ref, qk_ref, sd_ref,
         yl_ref, qrl, krl, krc, ol, s_fwd, s_rev) = refs
        qc = gc = yc_ref = qrc = oc = None
    s_refs = (s_fwd, s_rev)
    c = RET_CHUNK
    lc = n_ctx * c
    k_scale = RT_HK ** -0.5

    def rot_all(src, dst, n, pos0, scale):
        def body(j, carry):
            rows = pl.ds(pl.multiple_of(j * c, c), c)
            prow = pl.ds(pl.multiple_of(pos0 + j * c, c), c)
            x = src[rows, :].astype(F32)
            dst[rows, :] = (_rotate(x, cos_ref[prow, :], sin_ref[prow, :]) * scale).astype(BF16)
            return carry
        lax.fori_loop(0, n, body, 0)

    rot_all(kl, krl, n_lat, lc, k_scale)
    rot_all(kc, krc, n_ctx, 0, k_scale)
    rot_all(ql, qrl, n_lat, lc, 1.0)
    if need_ctx_out:
        rot_all(qc, qrc, n_ctx, 0, 1.0)

    for s_ref in s_refs:
        s_ref[...] = jnp.zeros(s_ref.shape, F32)
    ol[...] = jnp.zeros(ol.shape, F32)
    if need_ctx_out:
        oc[...] = jnp.zeros(oc.shape, F32)

    def update(d, kr, v_ref, rows):
        kd = qk_ref[d, 0, c:2 * c, :]
        sdec = sd_ref[d, 0, 0:1, :]
        kh = (kr[rows, :].astype(F32) * jnp.concatenate([kd, kd], axis=1)).astype(BF16)
        s_ref = s_refs[d]
        s_ref[...] = (s_ref[...] * jnp.concatenate([sdec] * (RT_HV // 128), axis=1)
                      + _dot_tn(kh, v_ref[rows, :]))

    def out_step(d, qr, kr, v_ref, o_ref, rows):
        qd = qk_ref[d, 0, 0:c, :]
        qcb = qr[rows, :]
        sc = _dot_nt(qcb, kr[rows, :]) * intra_ref[d, 0]
        qh = (qcb.astype(F32) * jnp.concatenate([qd, qd], axis=1)).astype(BF16)
        o_ref[rows, :] += (_dot(qh, s_refs[d][...].astype(BF16))
                           + _dot(sc.astype(BF16), v_ref[rows, :]))
        update(d, kr, v_ref, rows)

    def ctx_body(j, carry):
        for d in range(2):
            idx = (n_ctx - 1 - j) if d == 1 else j
            rows = pl.ds(pl.multiple_of(idx * c, c), c)
            if need_ctx_out:
                out_step(d, qrc, krc, vc, oc, rows)
            else:
                update(d, krc, vc, rows)
        return carry

    def lat_body(j, carry):
        for d in range(2):
            idx = (n_lat - 1 - j) if d == 1 else j
            out_step(d, qrl, krl, vl, ol, pl.ds(pl.multiple_of(idx * c, c), c))
        return carry

    lax.fori_loop(0, n_ctx, ctx_body, 0)
    lax.fori_loop(0, n_lat, lat_body, 0, unroll=2)

    def readout(o_ref, g_ref, y_ref, n):
        def body(j, carry):
            rows = pl.ds(pl.multiple_of(j * c, c), c)
            o = o_ref[rows, :]
            y = o * lax.rsqrt(jnp.mean(o * o, axis=-1, keepdims=True) + LN_EPS)
            y_ref[rows, :] = (y * _silu(g_ref[rows, :].astype(F32))).astype(y_ref.dtype)
            return carry
        lax.fori_loop(0, n, body, 0)

    readout(ol, gl, yl_ref, n_lat)
    if need_ctx_out:
        readout(oc, gc, yc_ref, n_ctx)


def _retention_tables(lc, l):
    half = RT_HK // 2
    inv = 1.0 / (RT_ROPE_BASE ** jnp.linspace(0.0, 1.0, half, dtype=F32))
    ang = jnp.arange(lc + l, dtype=F32)[:, None] * inv[None, :]
    j = jnp.arange(2 * RT_HEADS, dtype=F32)
    lg_all = jnp.log1p(-jnp.exp2(-5.0 - j))
    c = RET_CHUNK
    pos = jnp.arange(c, dtype=F32)
    rel = pos[:, None] - pos[None, :]
    intra, qk, sd = [], [], []
    for d in range(2):
        lg = lg_all[d::2]
        m = jnp.where(rel >= 0, jnp.exp(jnp.maximum(rel, 0.0)[None] * lg[:, None, None]), 0.0)
        qdec = jnp.exp((pos + 1.0)[None, :] * lg[:, None])
        kdec = jnp.exp((c - 1.0 - pos)[None, :] * lg[:, None])
        if d == 1:
            m = jnp.swapaxes(m, 1, 2)
            qdec = qdec[:, ::-1]
            kdec = kdec[:, ::-1]
        intra.append(m)
        qk.append(jnp.broadcast_to(jnp.concatenate([qdec, kdec], axis=1)[:, :, None],
                                   (RT_HEADS, 2 * c, 128)))
        sd.append(jnp.broadcast_to(jnp.exp(c * lg)[:, None, None], (RT_HEADS, 8, 128)))
    return (jnp.cos(ang), jnp.sin(ang), jnp.stack(intra), jnp.stack(qk), jnp.stack(sd))


def _retention(p_lat, p_ctx, tables, batch, need_ctx_out):
    l = p_lat.shape[0] // batch
    lc = p_ctx.shape[0] // batch
    cos, sin, intra, qk, sd = tables
    c = RET_CHUNK

    def col(base):
        return lambda b, h: (b, base + h)

    kq, kk_, kv, kg = COL_RQ * 128 // RT_HK, COL_RK * 128 // RT_HK, COL_RV * 128 // RT_HV, COL_RG * 128 // RT_HV
    lat_specs = [pl.BlockSpec((l, RT_HK), col(kq)), pl.BlockSpec((l, RT_HK), col(kk_)),
                 pl.BlockSpec((l, RT_HV), col(kv)), pl.BlockSpec((l, RT_HV), col(kg))]
    if need_ctx_out:
        ctx_specs = [pl.BlockSpec((lc, RT_HK), col(kq)), pl.BlockSpec((lc, RT_HK), col(kk_)),
                     pl.BlockSpec((lc, RT_HV), col(kv)), pl.BlockSpec((lc, RT_HV), col(kg))]
    else:
        ctx_specs = [pl.BlockSpec((lc, RT_HK), col(kk_)), pl.BlockSpec((lc, RT_HV), col(kv))]
    tab_specs = [pl.BlockSpec((lc + l, 128), lambda b, h: (0, 0)),
                 pl.BlockSpec((lc + l, 128), lambda b, h: (0, 0)),
                 pl.BlockSpec((2, 1, c, c), lambda b, h: (0, h, 0, 0)),
                 pl.BlockSpec((2, 1, 2 * c, 128), lambda b, h: (0, h, 0, 0)),
                 pl.BlockSpec((2, 1, 8, 128), lambda b, h: (0, h, 0, 0))]
    args = [p_lat] * 4 + [p_ctx] * len(ctx_specs) + [cos, sin, intra, qk, sd]
    out_specs = [pl.BlockSpec((l, RT_HV), lambda b, h: (b, h))]
    out_shape = [jax.ShapeDtypeStruct((batch * l, RT_HEADS * RT_HV), BF16)]
    scratch = [pltpu.VMEM((l, RT_HK), BF16), pltpu.VMEM((l, RT_HK), BF16)]
    if need_ctx_out:
        out_specs.append(pl.BlockSpec((lc, RT_HV), lambda b, h: (b, h)))
        out_shape.append(jax.ShapeDtypeStruct((batch * lc, RT_HEADS * RT_HV), BF16))
        scratch += [pltpu.VMEM((lc, RT_HK), BF16), pltpu.VMEM((lc, RT_HK), BF16),
                    pltpu.VMEM((l, RT_HV), F32), pltpu.VMEM((lc, RT_HV), F32)]
    else:
        scratch += [pltpu.VMEM((lc, RT_HK), BF16), pltpu.VMEM((l, RT_HV), F32)]
    scratch += [pltpu.VMEM((RT_HK, RT_HV), F32)] * 2
    outs = pl.pallas_call(
        functools.partial(_ret_kernel, need_ctx_out=need_ctx_out, n_lat=l // c, n_ctx=lc // c),
        grid=(batch, RT_HEADS),
        in_specs=lat_specs + ctx_specs + tab_specs,
        out_specs=out_specs,
        out_shape=out_shape,
        scratch_shapes=scratch,
        compiler_params=_cparams(("arbitrary", "arbitrary")),
        name="retention_scan",
    )(*args)
    return outs[0], (outs[1] if need_ctx_out else None)


def _short_conv_kernel(x_ref, w_ref, b_ref, o_ref, xpad_ref, *, grid):
    l, tc = x_ref.shape
    p = CONV_PAD
    zeros = jnp.zeros((p, tc), F32)
    xpad_ref[0:p, :] = zeros
    xpad_ref[p + l:p + l + p, :] = zeros
    xpad_ref[p:p + l, :] = x_ref[...].astype(F32)
    colw = lax.broadcasted_iota(jnp.int32, (l, tc), 0) % GRID_W
    out = jnp.broadcast_to(b_ref[...], (l, tc))
    for dj in range(3):
        acc = None
        for di in (range(3) if grid else (1,)):
            off = p + (di - 1) * GRID_W + (dj - 1)
            t = w_ref[di * 3 + dj:di * 3 + dj + 1, :] * xpad_ref[off:off + l, :]
            acc = t if acc is None else acc + t
        if grid and dj == 0:
            acc = jnp.where(colw >= 1, acc, 0.0)
        if grid and dj == 2:
            acc = jnp.where(colw <= GRID_W - 2, acc, 0.0)
        out = out + acc
    o_ref[...] = out.astype(o_ref.dtype)


def _short_conv(p_arr, w9, bias, batch, grid):
    l = p_arr.shape[0] // batch
    c3 = 3 * HY_DIM
    tc = 256
    base = COL_HY * 128 // tc
    return pl.pallas_call(
        functools.partial(_short_conv_kernel, grid=grid),
        grid=(batch, c3 // tc),
        in_specs=[pl.BlockSpec((l, tc), lambda b, j: (b, base + j)),
                  pl.BlockSpec((9, tc), lambda b, j: (0, j)),
                  pl.BlockSpec((1, tc), lambda b, j: (0, j))],
        out_specs=pl.BlockSpec((l, tc), lambda b, j: (b, j)),
        out_shape=jax.ShapeDtypeStruct((batch * l, c3), BF16),
        scratch_shapes=[pltpu.VMEM((l + 2 * CONV_PAD, tc), F32)],
        compiler_params=_cparams(("arbitrary", "arbitrary")),
        name="hyena_short_conv",
    )(p_arr, w9, bias.reshape(1, c3))


def _dft_matrices(l):
    n = 2 * l
    s = int(round(math.sqrt(n)))
    while n % s:
        s -= 1
    q = n // s
    f = np.arange(l, dtype=np.int64)
    ang_a = 2.0 * np.pi * ((f[:, None] * s * np.arange(q)[None, :]) % n) / n
    ang_b = 2.0 * np.pi * ((f[:, None] * np.arange(s)[None, :]) % n) / n
    t = np.arange(n)
    rep = (t[None, :] // s == np.arange(q)[:, None]).astype(np.float32)
    til = (t[None, :] % s == np.arange(s)[:, None]).astype(np.float32)
    h = DFT_HALF
    row = lambda j: (j, 0)
    const = lambda j: (0, 0)
    fwd, inv = pl.pallas_call(
        _dft_build_kernel,
        grid=(l // h,),
        in_specs=[pl.BlockSpec((h, q), row), pl.BlockSpec((h, q), row),
                  pl.BlockSpec((h, s), row), pl.BlockSpec((h, s), row),
                  pl.BlockSpec((q, n), const), pl.BlockSpec((s, n), const)],
        out_specs=[pl.BlockSpec((1, 2, h, n), lambda j: (j, 0, 0, 0)),
                   pl.BlockSpec((h, n), row)],
        out_shape=[jax.ShapeDtypeStruct((l // h, 2, h, n), BF16),
                   jax.ShapeDtypeStruct((l, n), BF16)],
        compiler_params=_cparams(("arbitrary",)),
        name="dft_build",
    )(jnp.asarray(np.cos(ang_a), F32), jnp.asarray(np.sin(ang_a), F32),
      jnp.asarray(np.cos(ang_b), F32), jnp.asarray(np.sin(ang_b), F32),
      jnp.asarray(rep, BF16), jnp.asarray(til, BF16))
    return fwd.reshape(n, n), inv


def _dft_build_kernel(ca_ref, sa_ref, cb_ref, sb_ref, rep_ref, til_ref, fwd_ref, inv_ref):
    h, n = inv_ref.shape

    def spread(x_ref, m_ref):
        p1, p2, p3 = _split3(x_ref[...])
        m = m_ref[...]
        return _dot(p1, m) + _dot(p2, m) + _dot(p3, m)

    ca, sa = spread(ca_ref, rep_ref), spread(sa_ref, rep_ref)
    cb, sb = spread(cb_ref, til_ref), spread(sb_ref, til_ref)
    cosm = ca * cb - sa * sb
    nsin = -(sa * cb + ca * sb)
    first = pl.program_id(0) == 0
    rowi = lax.broadcasted_iota(jnp.int32, (h, n), 0)
    coli = lax.broadcasted_iota(jnp.int32, (h, n), 1)
    alt_col = (1 - 2 * (coli % 2)).astype(F32)
    fwd_ref[0, 0] = cosm.astype(BF16)
    fwd_ref[0, 1] = jnp.where(first & (rowi == 0), alt_col, nsin).astype(BF16)
    rowg = lax.broadcasted_iota(jnp.int32, (h, h), 0) + pl.program_id(0) * h
    col0 = lax.broadcasted_iota(jnp.int32, (h, h), 1) == 0
    alt_row = (1 - 2 * (rowg % 2)).astype(F32)
    pieces = []
    for jj in range(n // (2 * h)):
        cp = cosm[:, jj * h:(jj + 1) * h] * (2.0 / n)
        ip = nsin[:, jj * h:(jj + 1) * h] * (2.0 / n)
        if jj == 0:
            cp = jnp.where(col0, 1.0 / n, cp)
            ip = jnp.where(col0, alt_row * (1.0 / n), ip)
        pieces += [cp.astype(BF16), ip.astype(BF16)]
    inv_ref[...] = jnp.concatenate(pieces, axis=1)


def _filter_spectrum_kernel(f_ref, top_ref, bot_ref, o_ref):
    h = DFT_HALF
    bl = top_ref.shape[0]
    acc = _dot(f_ref[:, 0:bl], top_ref[...]) + _dot(f_ref[:, bl:2 * bl], bot_ref[...])
    kr, km = acc[:h], acc[h:]
    row0 = (lax.broadcasted_iota(jnp.int32, kr.shape, 0) == 0) & (pl.program_id(2) == 0)
    o_ref[0, 0:h, :] = kr.astype(o_ref.dtype)
    o_ref[0, h:2 * h, :] = jnp.where(row0, 0.0, km).astype(o_ref.dtype)
    o_ref[0, 2 * h:3 * h, :] = jnp.where(row0, km, kr).astype(o_ref.dtype)


def _filter_spectrum(fwd, kern_bf, nb):
    n = fwd.shape[0]
    bl = n // 2
    nc = kern_bf.shape[1]
    tm, tn = 2 * DFT_HALF, 1024
    nd = 2 * nb - 1
    wrap = 2 * nb
    return pl.pallas_call(
        _filter_spectrum_kernel,
        grid=(nc // tn, nd, n // tm),
        in_specs=[pl.BlockSpec((tm, n), lambda c, d, t: (t, 0)),
                  pl.BlockSpec((bl, tn), lambda c, d, t: ((d - (nb - 1)) % wrap, c)),
                  pl.BlockSpec((bl, tn), lambda c, d, t: ((d - nb) % wrap, c))],
        out_specs=pl.BlockSpec((1, 3 * DFT_HALF, tn), lambda c, d, t: (d, t, c)),
        out_shape=jax.ShapeDtypeStruct((nd, (n // tm) * 3 * DFT_HALF, nc), BF16),
        compiler_params=_cparams(("arbitrary", "arbitrary", "arbitrary")),
        name="hyena_filter_dft",
    )(fwd, kern_bf, kern_bf)


def _dft_fwd_kernel(f_ref, z_ref, k_ref, y_ref):
    h = DFT_HALF
    bl = f_ref.shape[1]
    nb = y_ref.shape[1]
    f = f_ref[...]
    spec = [_dot(f, z_ref[j * bl:(j + 1) * bl, :]).astype(BF16) for j in range(nb)]
    for i in range(nb):
        yr = yi = None
        for j in range(nb):
            d = i - j + nb - 1
            kr, ki, kr2 = k_ref[d, 0:h, :], k_ref[d, h:2 * h, :], k_ref[d, 2 * h:3 * h, :]
            ur, ui = spec[j][:h], spec[j][h:]
            tr = ur * kr - ui * ki
            ti = ur * ki + ui * kr2
            yr = tr if yr is None else yr + tr
            yi = ti if yi is None else yi + ti
        y_ref[0, i, 0:h, :] = yr.astype(y_ref.dtype)
        y_ref[0, i, h:2 * h, :] = yi.astype(y_ref.dtype)


def _dft_fwd(fwd, z_arr, z_col, kf, k_col, batch, nb):
    n = fwd.shape[0]
    bl = n // 2
    l = nb * bl
    tm, tn = 2 * DFT_HALF, 512
    cpt = HY_DIM // tn
    nd = kf.shape[0]
    return pl.pallas_call(
        _dft_fwd_kernel,
        grid=(cpt, n // tm, batch),
        in_specs=[pl.BlockSpec((tm, bl), lambda c, t, b: (t, 0)),
                  pl.BlockSpec((l, tn), lambda c, t, b: (b, z_col * cpt + c)),
                  pl.BlockSpec((nd, 3 * DFT_HALF, tn), lambda c, t, b: (0, t, k_col * cpt + c))],
        out_specs=pl.BlockSpec((1, nb, tm, tn), lambda c, t, b: (b, 0, t, c)),
        out_shape=jax.ShapeDtypeStruct((batch, nb, n, HY_DIM), BF16),
        compiler_params=_cparams(("arbitrary", "arbitrary", "arbitrary")),
        name="hyena_dft_fwd",
    )(fwd, z_arr, kf)


def _dft_inv_kernel(g_ref, y_ref, gate_ref, z_ref, skip_ref, o_ref):
    conv = _dot(g_ref[...], y_ref[0, 0])
    z = z_ref[...].astype(F32)
    o_ref[...] = (gate_ref[...].astype(F32) * (conv + skip_ref[...] * z)).astype(o_ref.dtype)


def _dft_inv(inv, y, gate_arr, gate_col, z_arr, z_col, skip, batch, nb):
    bl, n = inv.shape
    tn = HY_DIM
    return pl.pallas_call(
        _dft_inv_kernel,
        grid=(batch, nb),
        in_specs=[pl.BlockSpec((bl, n), lambda b, i: (0, 0)),
                  pl.BlockSpec((1, 1, n, tn), lambda b, i: (b, i, 0, 0)),
                  pl.BlockSpec((bl, tn), lambda b, i: (b * nb + i, gate_col)),
                  pl.BlockSpec((bl, tn), lambda b, i: (b * nb + i, z_col)),
                  pl.BlockSpec((1, tn), lambda b, i: (0, 0))],
        out_specs=pl.BlockSpec((bl, tn), lambda b, i: (b * nb + i, 0)),
        out_shape=jax.ShapeDtypeStruct((batch * nb * bl, tn), BF16),
        compiler_params=_cparams(("arbitrary", "arbitrary")),
        name="hyena_dft_inv",
    )(inv, y, gate_arr, z_arr, skip.reshape(1, tn))


def _tap_features(l):
    f32 = np.float32
    t01 = np.linspace(0.0, 1.0, l, dtype=f32)[:, None]
    ang = f32(2.0 * math.pi) * np.arange(l, dtype=f32)[:, None] / f32(l)
    bands = np.linspace(1e-4, HY_BANDS - 1, HY_BANDS, dtype=f32)[None, :]
    z = np.concatenate([t01, np.cos(bands * ang), -np.sin(bands * ang)], axis=-1).astype(f32)
    pos = np.concatenate([np.arange(l), [0], np.arange(l - 1, 0, -1)])
    zz = np.zeros((2 * l, 128), f32)
    zz[:, :z.shape[1]] = z[pos]
    zz[l] = 0.0
    return zz


def _dot_hi(a, b):
    a1 = a.astype(BF16)
    a2 = (a - a1.astype(F32)).astype(BF16)
    b1 = b.astype(BF16)
    b2 = (b - b1.astype(F32)).astype(BF16)
    return _dot(a1, b1) + _dot(a1, b2) + _dot(a2, b1)


def _filter_kernel(zz_ref, w1_ref, b1_ref, fr_ref, w2_ref, b2_ref, w3f_ref, w3b_ref, dl_ref,
                   o_ref, hdn_ref):
    n = zz_ref.shape[0]
    l = n // 2

    @pl.when((pl.program_id(0) == 0) & (pl.program_id(1) == 0))
    def _():
        h1 = jnp.sin(fr_ref[0:1, :] * (_dot_hi(zz_ref[...], w1_ref[...]) + b1_ref[...]))
        hdn_ref[...] = jnp.sin(fr_ref[1:2, :] * (_dot_hi(h1, w2_ref[...]) + b2_ref[...]))

    filt = jnp.concatenate([_dot_hi(hdn_ref[0:l, :], w3f_ref[...]),
                            _dot_hi(hdn_ref[l:n, :], w3b_ref[...])], axis=0)
    decay = jnp.exp(-zz_ref[:, 0:1] * dl_ref[...])
    rowi = lax.broadcasted_iota(jnp.int32, filt.shape, 0)
    kern = jnp.where(rowi == l, 0.0, filt * decay)
    o_ref[...] = (kern / jnp.sum(jnp.abs(kern), axis=0, keepdims=True)).astype(o_ref.dtype)


def _hyena_filters(l, w1, b1, freq, w2, b2, w3):
    n = 2 * l
    tn = 256
    ff = w2.shape[0]
    cpo = HY_DIM // tn
    deltas = np.abs(np.linspace(math.log(HY_DECAY_TARGET) / HY_DECAY_PCT_LONG,
                                math.log(HY_DECAY_TARGET) / HY_DECAY_PCT_SHORT, HY_DIM,
                                dtype=np.float32)).reshape(1, HY_DIM)
    const = lambda o, c: (0, 0)
    return pl.pallas_call(
        _filter_kernel,
        grid=(HY_ORDER, cpo),
        in_specs=[pl.BlockSpec((n, 128), const), pl.BlockSpec((128, ff), const),
                  pl.BlockSpec((1, ff), const), pl.BlockSpec((2, ff), const),
                  pl.BlockSpec((ff, ff), const), pl.BlockSpec((1, ff), const),
                  pl.BlockSpec((ff, tn), lambda o, c: (0, o * 2 * cpo + c)),
                  pl.BlockSpec((ff, tn), lambda o, c: (0, o * 2 * cpo + cpo + c)),
                  pl.BlockSpec((1, tn), lambda o, c: (0, c))],
        out_specs=pl.BlockSpec((n, tn), lambda o, c: (0, o * cpo + c)),
        out_shape=jax.ShapeDtypeStruct((n, HY_ORDER * HY_DIM), BF16),
        scratch_shapes=[pltpu.VMEM((n, ff), F32)],
        compiler_params=_cparams(("arbitrary", "arbitrary")),
        name="hyena_filter_mlp",
    )(jnp.asarray(_tap_features(l)), jnp.pad(w1, ((0, 128 - w1.shape[0]), (0, 0))),
      b1.reshape(1, ff), freq, w2, b2.reshape(1, ff), w3, w3, jnp.asarray(deltas))


def _hyena(p_arr, lp, dft, batch, grid):
    l = p_arr.shape[0] // batch
    fwd, inv = dft
    uc = _short_conv(p_arr, lp['hy_conv_w'].reshape(9, 3 * HY_DIM), lp['hy_conv_b'], batch, grid)
    kern = _hyena_filters(l, lp['hy_ff_w1'], lp['hy_ff_b1'], lp['hy_ff_freq'],
                          lp['hy_ff_w2'], lp['hy_ff_b2'], lp['hy_ff_w3'])
    nb = l // inv.shape[0]
    kf = _filter_spectrum(fwd, kern, nb)
    z_arr, z_col = uc, 0
    for n in range(HY_ORDER):
        y = _dft_fwd(fwd, z_arr, z_col, kf, n, batch, nb)
        z_arr = _dft_inv(inv, y, uc, n + 1, z_arr, z_col, lp['hy_skip'][n], batch, nb)
        z_col = 0
    return z_arr


def _merge_kernel(yhy_ref, ohg_ref, hgg_ref, yrt_ref, b0_ref, b1_ref, b2_ref, x_ref, mod_ref,
                  nw_ref, phy_ref, phg_ref, prt_ref, wo_ref, lng_ref, lnb_ref, o_ref):
    o = ohg_ref[...]
    yhg = (o * lax.rsqrt(jnp.mean(o * o, axis=-1, keepdims=True) + LN_EPS) * nw_ref[...]
           * _silu(hgg_ref[...].astype(F32)))
    m = (jax.nn.sigmoid(b0_ref[...].astype(F32)) * _dot(yhy_ref[...], phy_ref[...])
         + jax.nn.sigmoid(b1_ref[...].astype(F32)) * _dot(yhg.astype(BF16), phg_ref[...])
         + jax.nn.sigmoid(b2_ref[...].astype(F32)) * _dot(yrt_ref[...], prt_ref[...]))
    t = _dot(m.astype(BF16), wo_ref[...])
    gt1 = mod_ref[0][2:3]
    o_ref[...] = _layer_norm_rows(DN_ALPHA * x_ref[...] + gt1 * t, lng_ref[...], lnb_ref[...])


def _merge(y_hy, o_hg, y_rt, p_arr, x, modp, lp, wbf, rows_per_batch):
    m = x.shape[0]
    tm = min(512, m)
    d = D_MODEL
    mi = _mod_index(rows_per_batch, tm)
    row = lambda i: (i, 0)
    const = lambda i: (0, 0)
    gcol = COL_HG * 128 // d
    bcol = COL_BR * 128 // d
    return pl.pallas_call(
        _merge_kernel,
        grid=(m // tm,),
        in_specs=[pl.BlockSpec((tm, d), row), pl.BlockSpec((tm, d), row),
                  pl.BlockSpec((tm, d), lambda i: (i, gcol)),
                  pl.BlockSpec((tm, 2 * d), row),
                  pl.BlockSpec((tm, d), lambda i: (i, bcol)),
                  pl.BlockSpec((tm, d), lambda i: (i, bcol + 1)),
                  pl.BlockSpec((tm, d), lambda i: (i, bcol + 2)),
                  pl.BlockSpec((tm, d), row),
                  pl.BlockSpec((1, 8, d), lambda i: (mi(i), 0, 0)),
                  pl.BlockSpec((1, d), const),
                  pl.BlockSpec((d, d), const), pl.BlockSpec((d, d), const),
                  pl.BlockSpec((2 * d, d), const), pl.BlockSpec((d, d), const),
                  pl.BlockSpec((1, d), const), pl.BlockSpec((1, d), const)],
        out_specs=pl.BlockSpec((tm, d), row),
        out_shape=jax.ShapeDtypeStruct((m, d), F32),
        compiler_params=_cparams(("arbitrary",)),
        name="merge_out_ln",
    )(y_hy, o_hg, p_arr, y_rt, p_arr, p_arr, p_arr, x, modp, lp['hg_norm_w'].reshape(1, d),
      wbf['p_hy'], wbf['p_hg'], wbf['p_rt'], wbf['w_o'],
      lp['ln1_g'].reshape(1, d), lp['ln1_b'].reshape(1, d))


def _ffn_kernel(x_ref, mod_ref, w1_ref, w3_ref, w2_ref, lng_ref, lnb_ref, o_ref, h_ref, acc_ref):
    k = pl.program_id(1)
    m = mod_ref[0]

    @pl.when(k == 0)
    def _():
        h_ref[...] = (x_ref[...] * (1.0 + m[4:5]) + m[3:4]).astype(BF16)
        acc_ref[...] = jnp.zeros(acc_ref.shape, F32)

    h = h_ref[...]
    u = _silu(_dot(h, w1_ref[...])) * _dot(h, w3_ref[...])
    acc_ref[...] += _dot(u.astype(BF16), w2_ref[...])

    @pl.when(k == pl.num_programs(1) - 1)
    def _():
        o_ref[...] = _layer_norm_rows(DN_ALPHA * x_ref[...] + m[5:6] * acc_ref[...],
                                      lng_ref[...], lnb_ref[...])


def _ffn_dense(x, modp, w1, w3, w2, ln_g, ln_b, rows_per_batch):
    m = x.shape[0]
    d = D_MODEL
    dff = w1.shape[1]
    tm = min(1024, rows_per_batch or m)
    tf = dff // 2
    mi = _mod_index(rows_per_batch, tm)
    return pl.pallas_call(
        _ffn_kernel,
        grid=(m // tm, dff // tf),
        in_specs=[pl.BlockSpec((tm, d), lambda i, k: (i, 0)),
                  pl.BlockSpec((1, 8, d), lambda i, k: (mi(i), 0, 0)),
                  pl.BlockSpec((d, tf), lambda i, k: (0, k)),
                  pl.BlockSpec((d, tf), lambda i, k: (0, k)),
                  pl.BlockSpec((tf, d), lambda i, k: (k, 0)),
                  pl.BlockSpec((1, d), lambda i, k: (0, 0)),
                  pl.BlockSpec((1, d), lambda i, k: (0, 0))],
        out_specs=pl.BlockSpec((tm, d), lambda i, k: (i, 0)),
        out_shape=jax.ShapeDtypeStruct((m, d), F32),
        scratch_shapes=[pltpu.VMEM((tm, d), BF16), pltpu.VMEM((tm, d), F32)],
        compiler_params=_cparams(("arbitrary", "arbitrary")),
        name="ffn_dense_ln",
    )(x, modp, w1, w3, w2, ln_g.reshape(1, d), ln_b.reshape(1, d))


def _router_kernel(x_ref, mod_ref, r_ref, h_ref, lg_ref):
    m = mod_ref[0]
    h = x_ref[...] * (1.0 + m[4:5]) + m[3:4]
    h_ref[...] = h
    a1, a2, a3 = _split3(h)
    r1, r2, r3 = _split3(r_ref[...])
    lg_ref[...] = (_dot(a1, r1) + _dot(a1, r2) + _dot(a2, r1)
                   + _dot(a2, r2) + _dot(a1, r3) + _dot(a3, r1))


def _router(x, modp, router_pad, rows_per_batch):
    m = x.shape[0]
    d = D_MODEL
    tm = min(512, m)
    mi = _mod_index(rows_per_batch, tm)
    return pl.pallas_call(
        _router_kernel,
        grid=(m // tm,),
        in_specs=[pl.BlockSpec((tm, d), lambda i: (i, 0)),
                  pl.BlockSpec((1, 8, d), lambda i: (mi(i), 0, 0)),
                  pl.BlockSpec((d, 128), lambda i: (0, 0))],
        out_specs=[pl.BlockSpec((tm, d), lambda i: (i, 0)),
                   pl.BlockSpec((tm, 128), lambda i: (i, 0))],
        out_shape=[jax.ShapeDtypeStruct((m, d), F32), jax.ShapeDtypeStruct((m, 128), F32)],
        compiler_params=_cparams(("arbitrary",)),
        name="moe_router",
    )(x, modp, router_pad)


def _moe_ffn_kernel(be_ref, nu_ref, xp_ref, w1_ref, w3_ref, w2_ref, o_ref, x_ref, acc_ref):
    j = pl.program_id(0)
    k = pl.program_id(1)

    @pl.when(j < nu_ref[0])
    def _():
        @pl.when(k == 0)
        def _():
            x_ref[...] = xp_ref[...].astype(BF16)
            acc_ref[...] = jnp.zeros(acc_ref.shape, F32)

        x = x_ref[...]
        u = _silu(_dot(x, w1_ref[0])) * _dot(x, w3_ref[0])
        acc_ref[...] += _dot(u.astype(BF16), w2_ref[0])

        @pl.when(k == pl.num_programs(1) - 1)
        def _():
            o_ref[...] = acc_ref[...]

    @pl.when((j >= nu_ref[0]) & (k == pl.num_programs(1) - 1))
    def _():
        o_ref[...] = jnp.zeros(o_ref.shape, F32)


def _moe_ffn(xb, block_e, n_used, w1, w3, w2):
    ns = xb.shape[0]
    d = D_MODEL
    tm = MOE_ROWS
    dex = w1.shape[2]
    tf = dex // 2
    grid_spec = pltpu.PrefetchScalarGridSpec(
        num_scalar_prefetch=2,
        grid=(ns // tm, dex // tf),
        in_specs=[pl.BlockSpec((tm, d), lambda j, k, be, nu: (j, 0)),
                  pl.BlockSpec((1, d, tf), lambda j, k, be, nu: (be[j], 0, k)),
                  pl.BlockSpec((1, d, tf), lambda j, k, be, nu: (be[j], 0, k)),
                  pl.BlockSpec((1, tf, d), lambda j, k, be, nu: (be[j], k, 0))],
        out_specs=pl.BlockSpec((tm, d), lambda j, k, be, nu: (j, 0)),
        scratch_shapes=[pltpu.VMEM((tm, d), BF16), pltpu.VMEM((tm, d), F32)])
    return pl.pallas_call(
        _moe_ffn_kernel,
        grid_spec=grid_spec,
        out_shape=jax.ShapeDtypeStruct((ns, d), F32),
        compiler_params=_cparams(("arbitrary", "arbitrary")),
        name="moe_expert_ffn",
    )(block_e, n_used, xb, w1, w3, w2)


def _combine_kernel(x_ref, mod_ref, y0_ref, y1_ref, g_ref, lng_ref, lnb_ref, o_ref):
    m = mod_ref[0]
    g = g_ref[...]
    f = g[:, 0:1] * y0_ref[...] + g[:, 1:2] * y1_ref[...]
    o_ref[...] = _layer_norm_rows(DN_ALPHA * x_ref[...] + m[5:6] * f, lng_ref[...], lnb_ref[...])


def _moe_combine(x, modp, y0, y1, gate_pad, ln_g, ln_b, rows_per_batch):
    m = x.shape[0]
    d = D_MODEL
    tm = min(512, m)
    mi = _mod_index(rows_per_batch, tm)
    row = lambda i: (i, 0)
    return pl.pallas_call(
        _combine_kernel,
        grid=(m // tm,),
        in_specs=[pl.BlockSpec((tm, d), row),
                  pl.BlockSpec((1, 8, d), lambda i: (mi(i), 0, 0)),
                  pl.BlockSpec((tm, d), row), pl.BlockSpec((tm, d), row),
                  pl.BlockSpec((tm, 128), row),
                  pl.BlockSpec((1, d), lambda i: (0, 0)), pl.BlockSpec((1, d), lambda i: (0, 0))],
        out_specs=pl.BlockSpec((tm, d), row),
        out_shape=jax.ShapeDtypeStruct((m, d), F32),
        compiler_params=_cparams(("arbitrary",)),
        name="moe_combine_ln",
    )(x, modp, y0, y1, gate_pad, ln_g.reshape(1, d), ln_b.reshape(1, d))


def _cast_kernel(x_ref, o_ref):
    o_ref[...] = x_ref[...].astype(o_ref.dtype)


def _to_bf16(w, group):
    _, e, a, b = w.shape
    ta = a // 4
    return pl.pallas_call(
        _cast_kernel,
        grid=(e, a // ta),
        in_specs=[pl.BlockSpec((None, 1, ta, b), lambda i, r: (group, i, r, 0))],
        out_specs=pl.BlockSpec((1, ta, b), lambda i, r: (i, r, 0)),
        out_shape=jax.ShapeDtypeStruct((e, a, b), BF16),
        compiler_params=_cparams(("arbitrary", "arbitrary")),
        name="weight_cast",
    )(w)


def _moe(x, modp, router, w1, w3, w2, ln_g, ln_b, rows_per_batch):
    n, d = x.shape
    e = router.shape[1]
    h2, logits = _router(x, modp, jnp.pad(router, ((0, 0), (0, 128 - e))), rows_per_batch)
    top_val, top_idx = lax.top_k(logits[:, :e], TOP_K)
    gate = jax.nn.softmax(top_val, axis=-1)
    flat_e = top_idx.reshape(-1)
    flat_t = jnp.repeat(jnp.arange(n, dtype=jnp.int32), TOP_K)
    order = jnp.argsort(flat_e, stable=True).astype(jnp.int32)
    rank = jnp.argsort(order).astype(jnp.int32)
    counts = jnp.sum((flat_e[:, None] == jnp.arange(e)[None, :]).astype(jnp.int32), axis=0)
    starts = jnp.cumsum(counts) - counts
    padded = (counts + MOE_ROWS - 1) // MOE_ROWS * MOE_ROWS
    pad_end = jnp.cumsum(padded)
    pad_start = pad_end - padded
    n_blocks = -(-(n * TOP_K) // MOE_ROWS) + e
    n_slots = n_blocks * MOE_ROWS
    block_start = jnp.arange(n_blocks) * MOE_ROWS
    block_e = jnp.minimum(jnp.sum(block_start[:, None] >= pad_end[None, :], axis=1), e - 1).astype(jnp.int32)
    n_used = (pad_end[-1] // MOE_ROWS).astype(jnp.int32).reshape(1)
    slot_e = jnp.repeat(block_e, MOE_ROWS)
    slot_off = jnp.arange(n_slots, dtype=jnp.int32) - pad_start[slot_e]
    slot_valid = slot_off < counts[slot_e]
    slot_src = jnp.where(slot_valid, starts[slot_e] + slot_off, 0)
    slot_tok = jnp.where(slot_valid, flat_t[order[slot_src]], 0).astype(jnp.int32)
    pos = (pad_start[flat_e] + rank - starts[flat_e]).astype(jnp.int32).reshape(n, TOP_K)
    xb = jnp.take(h2, slot_tok, axis=0, mode="clip")
    yb = _moe_ffn(xb, block_e, n_used, w1, w3, w2)
    y0 = jnp.take(yb, pos[:, 0], axis=0, mode="clip")
    y1 = jnp.take(yb, pos[:, 1], axis=0, mode="clip")
    gate_pad = jnp.pad(gate, ((0, 0), (0, 128 - TOP_K)))
    return _moe_combine(x, modp, y0, y1, gate_pad, ln_g, ln_b, rows_per_batch)


def kernel(x, c, ctx, c_ctx, ada_w, ada_b, w_in, hy_conv_w, hy_conv_b, hy_ff_w1, hy_ff_b1, hy_ff_freq, hy_ff_w2, hy_ff_b2, hy_ff_w3, hy_skip, hg_lb_logits, hg_norm_w, p_hy, p_hg, p_rt, w_o, ln1_g, ln1_b, ln2_g, ln2_b, ffn_w1, ffn_w3, ffn_w2, moe_router, moe_w1, moe_w3, moe_w2):
    batch, l, d = x.shape
    lc = ctx.shape[1]
    assert d == D_MODEL and batch <= 8
    assert l % 512 == 0 and l % GRID_W == 0 and lc % max(RET_CHUNK, DFT_HALF) == 0

    cs = jnp.cumsum(jax.nn.softmax(hg_lb_logits.astype(F32), axis=1), axis=1)
    lower_bounds = cs - cs[:, :1]
    cc = jnp.zeros((16, d), F32).at[:batch].set(c).at[8].set(c_ctx)
    ret_tables = _retention_tables(lc, l)
    dft_lat = _dft_matrices(min(HY_BLOCK, l))
    dft_ctx = _dft_matrices(min(HY_BLOCK, lc))

    x_lat = x.reshape(batch * l, d)
    x_ctx = ctx.reshape(batch * lc, d)
    for i in range(DEPTH):
        need_ctx_out = i < DEPTH - 1
        use_moe = i % 2 == 1
        g = i // 2
        lp = {'hy_conv_w': hy_conv_w[i], 'hy_conv_b': hy_conv_b[i], 'hy_ff_w1': hy_ff_w1[i],
              'hy_ff_b1': hy_ff_b1[i], 'hy_ff_freq': hy_ff_freq[i], 'hy_ff_w2': hy_ff_w2[i],
              'hy_ff_b2': hy_ff_b2[i], 'hy_ff_w3': hy_ff_w3[i], 'hy_skip': hy_skip[i],
              'hg_norm_w': hg_norm_w[i], 'ln1_g': ln1_g[i], 'ln1_b': ln1_b[i]}
        wbf = {'p_hy': p_hy[i].astype(BF16), 'p_hg': p_hg[i].astype(BF16),
               'p_rt': p_rt[i].astype(BF16), 'w_o': w_o[i].astype(BF16)}
        mod = _ada(cc, ada_w, i, ada_b[i])
        modp = jnp.pad(mod.reshape(16, 6, d), ((0, 0), (0, 2), (0, 0)))
        lb = lower_bounds[:, i]
        gl_tab = jnp.pad(jnp.stack([jnp.maximum(jnp.log(lb), -1e30), jnp.log1p(-lb), 1.0 - lb], axis=1),
                         ((0, 0), (0, 5), (0, 0)))

        p_lat = _proj(x_lat, modp, w_in, i, N_IN_COLS, l)
        p_ctx = _proj(x_ctx, modp, w_in, i, N_IN_COLS if need_ctx_out else N_STATE_COLS, None)

        o_hg_l, o_hg_c = _gla(p_lat, p_ctx, gl_tab, batch, need_ctx_out)
        y_rt_l, y_rt_c = _retention(p_lat, p_ctx, ret_tables, batch, need_ctx_out)
        y_hy_l = _hyena(p_lat, lp, dft_lat, batch, True)
        x_lat_new = _merge(y_hy_l, o_hg_l, y_rt_l, p_lat, x_lat, modp, lp, wbf, l)
        if need_ctx_out:
            y_hy_c = _hyena(p_ctx, lp, dft_ctx, batch, False)
            x_ctx = _merge(y_hy_c, o_hg_c, y_rt_c, p_ctx, x_ctx, modp, lp, wbf, None)
        x_lat = x_lat_new

        if use_moe:
            if need_ctx_out:
                raise NotImplementedError("MoE layer with a context output is not part of this trunk")
            w1, w3, w2 = _to_bf16(moe_w1, g), _to_bf16(moe_w3, g), _to_bf16(moe_w2, g)
            x_lat = _moe(x_lat, modp, moe_router[g], w1, w3, w2, ln2_g[i], ln2_b[i], l)
        else:
            w1, w3, w2 = ffn_w1[g].astype(BF16), ffn_w3[g].astype(BF16), ffn_w2[g].astype(BF16)
            x_lat = _ffn_dense(x_lat, modp, w1, w3, w2, ln2_g[i], ln2_b[i], l)
            if need_ctx_out:
                x_ctx = _ffn_dense(x_ctx, modp, w1, w3, w2, ln2_g[i], ln2_b[i], None)
    return x_lat.reshape(batch, l, d)
```

```python
import functools
import math

import numpy as np
import jax
import jax.numpy as jnp
from jax import lax
from jax.experimental import pallas as pl
from jax.experimental.pallas import tpu as pltpu

F32 = jnp.float32
BF16 = jnp.bfloat16

D_MODEL = 1024
DEPTH = 2
GRID_W = 64
HY_DIM = 1024
HY_ORDER = 2
HY_BANDS = 16
HY_DECAY_TARGET = 1e-2
HY_DECAY_PCT_SHORT = 0.3
HY_DECAY_PCT_LONG = 1.5
HG_HEADS = 8
HG_HK = 128
RT_HEADS = 4
RT_HK = 256
RT_HV = 512
RT_ROPE_BASE = 10000.0
N_STATE_COLS = 6144
N_IN_COLS = 17408
N_EXPERTS = 8
TOP_K = 2
DN_ALPHA = (2 * DEPTH) ** 0.25
LN_EPS = 1e-5

COL_FF, COL_FB, COL_HI, COL_RK, COL_RV = 0, 8, 16, 24, 32
COL_HQ, COL_HG, COL_RQ, COL_RG, COL_HY, COL_BR = 48, 56, 64, 72, 88, 112

GLA_CHUNK = 128
GLA_LEVELS = 7
GLA_PAD = 8
RET_CHUNK = 256
DFT_HALF = 256
HY_BLOCK = 512
CONV_ROWS = 256
MOE_ROWS = 512
VMEM_LIMIT = 56 * 1024 * 1024


def _cparams(sem, flags=None):
    return pltpu.CompilerParams(dimension_semantics=sem, vmem_limit_bytes=VMEM_LIMIT, flags=flags)


def _silu(x):
    return x * jax.nn.sigmoid(x)


def _layer_norm_rows(r, g, b):
    mu = jnp.mean(r, axis=-1, keepdims=True)
    d = r - mu
    var = jnp.mean(d * d, axis=-1, keepdims=True)
    return d * lax.rsqrt(var + LN_EPS) * g + b


def _split3(x):
    h = x.astype(BF16)
    r = x - h.astype(F32)
    m = r.astype(BF16)
    l = (r - m.astype(F32)).astype(BF16)
    return h, m, l


def _dot(a, b):
    return jnp.dot(a, b, preferred_element_type=F32)


def _dot_nt(a, b):
    return lax.dot_general(a, b, (((1,), (1,)), ((), ())), preferred_element_type=F32)


def _dot_tn(a, b):
    return lax.dot_general(a, b, (((0,), (0,)), ((), ())), preferred_element_type=F32)


def _ada_kernel(c_ref, w_ref, b_ref, o_ref):
    s = _silu(c_ref[...])
    o_ref[...] = _dot(s.astype(BF16), w_ref[...].astype(BF16)) + b_ref[...]


def _ada(cc, w, layer, b):
    n = w.shape[2]
    tn = 1024
    return pl.pallas_call(
        _ada_kernel,
        grid=(n // tn,),
        in_specs=[pl.BlockSpec((16, D_MODEL), lambda j: (0, 0)),
                  pl.BlockSpec((None, D_MODEL, tn), lambda j: (layer, 0, j)),
                  pl.BlockSpec((1, tn), lambda j: (0, j))],
        out_specs=pl.BlockSpec((16, tn), lambda j: (0, j)),
        out_shape=jax.ShapeDtypeStruct((16, n), F32),
        compiler_params=_cparams(("arbitrary",)),
        name="ada_mod",
    )(cc, w, b.reshape(1, n))


def _mod_index(rows_per_batch, tm):
    if rows_per_batch is None:
        return lambda i: 8
    return lambda i: (i * tm) // rows_per_batch


def _proj_kernel(x_ref, mod_ref, w_ref, o_ref, h_ref):
    @pl.when(pl.program_id(1) == 0)
    def _():
        m = mod_ref[0]
        h_ref[...] = (x_ref[...] * (1.0 + m[1:2]) + m[0:1]).astype(BF16)

    o_ref[...] = _dot(h_ref[...], w_ref[...].astype(BF16)).astype(o_ref.dtype)


def _proj(x, modp, w, layer, n_cols, rows_per_batch):
    m = x.shape[0]
    tm = min(2048, rows_per_batch or m)
    tn = 1024
    mi = _mod_index(rows_per_batch, tm)
    return pl.pallas_call(
        _proj_kernel,
        grid=(m // tm, n_cols // tn),
        in_specs=[pl.BlockSpec((tm, D_MODEL), lambda i, j: (i, 0)),
                  pl.BlockSpec((1, 8, D_MODEL), lambda i, j: (mi(i), 0, 0)),
                  pl.BlockSpec((None, D_MODEL, tn), lambda i, j: (layer, 0, j))],
        out_specs=pl.BlockSpec((tm, tn), lambda i, j: (i, j)),
        out_shape=jax.ShapeDtypeStruct((m, n_cols), BF16),
        scratch_shapes=[pltpu.VMEM((tm, D_MODEL), BF16)],
        compiler_params=_cparams(("arbitrary", "arbitrary")),
        name="in_proj",
    )(x, modp, w)


def _gla_gates(fl, gl_ref, d):
    a = gl_ref[d, 0:1, :]
    l1p = gl_ref[d, 1:2, :]
    oml = gl_ref[d, 2:3, :]
    t = jnp.exp(-jnp.abs(fl))
    u = 1.0 + t
    x2 = l1p + jnp.minimum(fl, 0.0) - jnp.log(u)
    log_f = jnp.maximum(a, x2) + jnp.log(1.0 + jnp.exp(-jnp.abs(a - x2)))
    key = oml * jnp.where(fl >= 0.0, t, 1.0) / u
    return log_f, key


def _gla_cumsum(g, rev):
    c = g.shape[0]
    r = lax.broadcasted_iota(jnp.int32, (c, c), 0)
    u = lax.broadcasted_iota(jnp.int32, (c, c), 1)
    tri = jnp.where((u >= r) if rev else (u <= r), 1.0, 0.0).astype(BF16)
    h, m, l = _split3(g)
    return _dot(tri, h) + _dot(tri, m) + _dot(tri, l)


def _gla_state_update(st, b, kk, vv, rev):
    c = b.shape[0]
    end = 0 if rev else c - 1
    b_end = b[end:end + 1, :]
    khat = (kk * jnp.exp2(b_end - b)).astype(BF16)
    return st * jnp.exp2(b_end) + _dot_tn(vv, khat)


def _gla_level_table():
    c = GLA_CHUNK
    t = np.arange(c)[:, None]
    s = np.arange(c)[None, :]
    top_bit = np.floor(np.log2(np.maximum(t ^ s, 1))).astype(np.int32)
    fwd = np.where(t > s, top_bit, np.where(t == s, GLA_LEVELS, -1))
    return np.stack([fwd, fwd.T]).astype(np.int32)


def _gla_ref_rows(b, bpad_ref, level, rev):
    c = b.shape[0]
    half = 1 << level
    blk = 2 * half
    idx = half if rev else half - 1
    if blk % 8 == 0:
        r = b.reshape(c // blk, blk, HG_HK)[:, idx:idx + 1, :]
        return jnp.broadcast_to(r, (c // blk, blk, HG_HK)).reshape(c, HG_HK)
    m = lax.broadcasted_iota(jnp.int32, (c, HG_HK), 0) % blk
    out = b
    for v in range(blk):
        if v != idx:
            off = GLA_PAD + idx - v
            out = jnp.where(m == v, bpad_ref[off:off + c, :], out)
    return out


def _gla_chunk_out(b, kk, vv, qq, st, rev, lvt, bpad_ref):
    c = GLA_CHUNK
    o = _dot_nt((qq * jnp.exp2(b)).astype(BF16), st.astype(BF16))
    bpad_ref[GLA_PAD:GLA_PAD + c, :] = b
    sc = jnp.zeros((c, c), F32)
    for level in range(GLA_LEVELS):
        half = 1 << level
        blk = 2 * half
        ref = _gla_ref_rows(b, bpad_ref, level, rev)
        kt = (kk * jnp.exp2(ref - b)).astype(BF16)
        if half % 8:
            p = _dot_nt((qq * jnp.exp2(b - ref)).astype(BF16), kt)
            sc = jnp.where(lvt == level, p, sc)
        else:
            nblk = c // blk
            lo, hi = (0, half) if rev else (half, blk)

            def part(x):
                return x.reshape(nblk, blk, x.shape[-1])[:, lo:hi, :]

            qt = (part(qq) * jnp.exp2(part(b) - part(ref))).reshape(c // 2, HG_HK)
            p = _dot_nt(qt.astype(BF16), kt).reshape(nblk, half, c)
            new = jnp.where(part(lvt) == level, p, part(sc))
            rest = sc.reshape(nblk, blk, c)[:, half:blk, :] if rev else sc.reshape(nblk, blk, c)[:, 0:half, :]
            sc = jnp.concatenate([new, rest] if rev else [rest, new], axis=1).reshape(c, c)
    sc = jnp.where(lvt == GLA_LEVELS, _dot_nt(qq.astype(BF16), kk.astype(BF16)), sc)
    return o + _dot(sc.astype(BF16), vv)


def _gla_kernel(*refs, need_ctx_out, n_lat, n_ctx):
    if need_ctx_out:
        ffl, fbl, il, ql, ffc, fbc, ic, qc, gl_ref, lv_ref, ol_ref, oc_ref = refs[:12]
        bl, kl, bc, kc, qsl, qsc, bpf, bpr = refs[12:]
    else:
        ffl, fbl, il, ql, ffc, fbc, ic, gl_ref, lv_ref, ol_ref = refs[:10]
        bl, kl, bc, kc, qsl, bpf, bpr = refs[10:]
        qc = oc_ref = qsc = None
    c = GLA_CHUNK
    q_scale = HG_HK ** -0.5
    bpf[...] = jnp.zeros(bpf.shape, F32)
    bpr[...] = jnp.zeros(bpr.shape, F32)
    ol_ref[...] = jnp.zeros(ol_ref.shape, F32)
    if need_ctx_out:
        oc_ref[...] = jnp.zeros(oc_ref.shape, F32)

    def chunk_rows(idx):
        return pl.ds(pl.multiple_of(idx * c, c), c)

    def prepare(f_refs, q_ref, b_sc, k_sc, q_sc, n):
        def body(j, carry):
            rows = chunk_rows(j)
            for d in range(2):
                g, kk = _gla_gates(f_refs[d][rows, :].astype(F32), gl_ref, d)
                b_sc[d, rows, :] = _gla_cumsum(g, d == 1) * math.log2(math.e)
                k_sc[d, rows, :] = kk
            if q_ref is not None:
                q_sc[rows, :] = _silu(q_ref[rows, :].astype(F32)) * q_scale
            return carry
        lax.fori_loop(0, n, body, 0, unroll=min(2, n))

    prepare((ffc, fbc), qc, bc, kc, qsc, n_ctx)
    prepare((ffl, fbl), ql, bl, kl, qsl, n_lat)

    def one_dir(d, b_sc, k_sc, i_ref, q_sc, o_ref, idx, st):
        rev = d == 1
        rows = chunk_rows(idx)
        b, kk, vv = b_sc[d, rows, :], k_sc[d, rows, :], i_ref[rows, :]
        if q_sc is not None:
            o_ref[rows, :] += _gla_chunk_out(b, kk, vv, q_sc[rows, :], st, rev, lv_ref[d],
                                             bpr if rev else bpf)
        return _gla_state_update(st, b, kk, vv, rev)

    def ctx_body(j, carry):
        return (one_dir(0, bc, kc, ic, qsc, oc_ref, j, carry[0]),
                one_dir(1, bc, kc, ic, qsc, oc_ref, n_ctx - 1 - j, carry[1]))

    def lat_body(j, carry):
        return (one_dir(0, bl, kl, il, qsl, ol_ref, j, carry[0]),
                one_dir(1, bl, kl, il, qsl, ol_ref, n_lat - 1 - j, carry[1]))

    zero = jnp.zeros((HG_HK, HG_HK), F32)
    carry = lax.fori_loop(0, n_ctx, ctx_body, (zero, zero), unroll=min(2, n_ctx))
    lax.fori_loop(0, n_lat, lat_body, carry, unroll=min(2, n_lat))


def _gla(p_lat, p_ctx, gl_tab, batch, need_ctx_out):
    l = p_lat.shape[0] // batch
    lc = p_ctx.shape[0] // batch
    w = HG_HK
    c = GLA_CHUNK

    def col(base):
        return lambda b, h: (b, base + h)

    lat_specs = [pl.BlockSpec((l, w), col(cb)) for cb in (COL_FF, COL_FB, COL_HI, COL_HQ)]
    ctx_cols = (COL_FF, COL_FB, COL_HI) + ((COL_HQ,) if need_ctx_out else ())
    ctx_specs = [pl.BlockSpec((lc, w), col(cb)) for cb in ctx_cols]
    in_specs = lat_specs + ctx_specs + [pl.BlockSpec((2, 8, w), lambda b, h: (0, 0, h)),
                                        pl.BlockSpec((2, c, c), lambda b, h: (0, 0, 0))]
    args = [p_lat] * 4 + [p_ctx] * len(ctx_cols) + [gl_tab, jnp.asarray(_gla_level_table())]
    out_specs = [pl.BlockSpec((l, w), lambda b, h: (b, h))]
    out_shape = [jax.ShapeDtypeStruct((batch * l, HG_HEADS * w), F32)]
    scratch = [pltpu.VMEM((2, l, w), F32), pltpu.VMEM((2, l, w), F32),
               pltpu.VMEM((2, lc, w), F32), pltpu.VMEM((2, lc, w), F32), pltpu.VMEM((l, w), F32)]
    if need_ctx_out:
        out_specs.append(pl.BlockSpec((lc, w), lambda b, h: (b, h)))
        out_shape.append(jax.ShapeDtypeStruct((batch * lc, HG_HEADS * w), F32))
        scratch.append(pltpu.VMEM((lc, w), F32))
    scratch += [pltpu.VMEM((c + 2 * GLA_PAD, w), F32)] * 2
    outs = pl.pallas_call(
        functools.partial(_gla_kernel, need_ctx_out=need_ctx_out, n_lat=l // c, n_ctx=lc // c),
        grid=(batch, HG_HEADS),
        in_specs=in_specs,
        out_specs=out_specs,
        out_shape=out_shape,
        scratch_shapes=scratch,
        compiler_params=_cparams(("arbitrary", "arbitrary")),
        name="hgrn2_scan",
    )(*args)
    return outs[0], (outs[1] if need_ctx_out else None)


def _rotate(x, cos, sin):
    half = x.shape[-1] // 2
    x1, x2 = x[:, :half], x[:, half:]
    return jnp.concatenate([x1 * cos - x2 * sin, x1 * sin + x2 * cos], axis=-1)


def _ret_kernel(*refs, need_ctx_out, n_lat, n_ctx):
    if need_ctx_out:
        (ql, kl, vl, gl, qc, kc, vc, gc, cos_ref, sin_ref, intra_ref, qk_ref, sd_ref,
         yl_ref, yc_ref, qrl, krl, qrc, krc, ol, oc, s_fwd, s_rev) = refs
    else:
        (ql, kl, vl, gl, kc, vc, cos_ref, sin_ref, intra_ref, qk_ref, sd_ref,
         yl_ref, qrl, krl, krc, ol, s_fwd, s_rev) = refs
        qc = gc = yc_ref = qrc = oc = None
    s_refs = (s_fwd, s_rev)
    c = RET_CHUNK
    lc = n_ctx * c
    k_scale = RT_HK ** -0.5

    def rot_all(src, dst, n, pos0, scale):
        def body(j, carry):
            rows = pl.ds(pl.multiple_of(j * c, c), c)
            prow = pl.ds(pl.multiple_of(pos0 + j * c, c), c)
            x = src[rows, :].astype(F32)
            dst[rows, :] = (_rotate(x, cos_ref[prow, :], sin_ref[prow, :]) * scale).astype(BF16)
            return carry
        lax.fori_loop(0, n, body, 0)

    rot_all(kl, krl, n_lat, lc, k_scale)
    rot_all(kc, krc, n_ctx, 0, k_scale)
    rot_all(ql, qrl, n_lat, lc, 1.0)
    if need_ctx_out:
        rot_all(qc, qrc, n_ctx, 0, 1.0)

    for s_ref in s_refs:
        s_ref[...] = jnp.zeros(s_ref.shape, F32)
    ol[...] = jnp.zeros(ol.shape, F32)
    if need_ctx_out:
        oc[...] = jnp.zeros(oc.shape, F32)

    def update(d, kr, v_ref, rows):
        kd = qk_ref[d, 0, c:2 * c, :]
        sdec = sd_ref[d, 0, 0:1, :]
        kh = (kr[rows, :].astype(F32) * jnp.concatenate([kd, kd], axis=1)).astype(BF16)
        s_ref = s_refs[d]
        s_ref[...] = (s_ref[...] * jnp.concatenate([sdec] * (RT_HV // 128), axis=1)
                      + _dot_tn(kh, v_ref[rows, :]))

    def out_step(d, qr, kr, v_ref, o_ref, rows):
        qd = qk_ref[d, 0, 0:c, :]
        qcb = qr[rows, :]
        sc = _dot_nt(qcb, kr[rows, :]) * intra_ref[d, 0]
        qh = (qcb.astype(F32) * jnp.concatenate([qd, qd], axis=1)).astype(BF16)
        o_ref[rows, :] += (_dot(qh, s_refs[d][...].astype(BF16))
                           + _dot(sc.astype(BF16), v_ref[rows, :]))
        update(d, kr, v_ref, rows)

    def ctx_body(j, carry):
        for d in range(2):
            idx = (n_ctx - 1 - j) if d == 1 else j
            rows = pl.ds(pl.multiple_of(idx * c, c), c)
            if need_ctx_out:
                out_step(d, qrc, krc, vc, oc, rows)
            else:
                update(d, krc, vc, rows)
        return carry

    def lat_body(j, carry):
        for d in range(2):
            idx = (n_lat - 1 - j) if d == 1 else j
            out_step(d, qrl, krl, vl, ol, pl.ds(pl.multiple_of(idx * c, c), c))
        return carry

    lax.fori_loop(0, n_ctx, ctx_body, 0)
    lax.fori_loop(0, n_lat, lat_body, 0, unroll=2)

    def readout(o_ref, g_ref, y_ref, n):
        def body(j, carry):
            rows = pl.ds(pl.multiple_of(j * c, c), c)
            o = o_ref[rows, :]
            y = o * lax.rsqrt(jnp.mean(o * o, axis=-1, keepdims=True) + LN_EPS)
            y_ref[rows, :] = (y * _silu(g_ref[rows, :].astype(F32))).astype(y_ref.dtype)
            return carry
        lax.fori_loop(0, n, body, 0)

    readout(ol, gl, yl_ref, n_lat)
    if need_ctx_out:
        readout(oc, gc, yc_ref, n_ctx)


def _retention_tables(lc, l):
    half = RT_HK // 2
    inv = 1.0 / (RT_ROPE_BASE ** jnp.linspace(0.0, 1.0, half, dtype=F32))
    ang = jnp.arange(lc + l, dtype=F32)[:, None] * inv[None, :]
    j = jnp.arange(2 * RT_HEADS, dtype=F32)
    lg_all = jnp.log1p(-jnp.exp2(-5.0 - j))
    c = RET_CHUNK
    pos = jnp.arange(c, dtype=F32)
    rel = pos[:, None] - pos[None, :]
    intra, qk, sd = [], [], []
    for d in range(2):
        lg = lg_all[d::2]
        m = jnp.where(rel >= 0, jnp.exp(jnp.maximum(rel, 0.0)[None] * lg[:, None, None]), 0.0)
        qdec = jnp.exp((pos + 1.0)[None, :] * lg[:, None])
        kdec = jnp.exp((c - 1.0 - pos)[None, :] * lg[:, None])
        if d == 1:
            m = jnp.swapaxes(m, 1, 2)
            qdec = qdec[:, ::-1]
            kdec = kdec[:, ::-1]
        intra.append(m)
        qk.append(jnp.broadcast_to(jnp.concatenate([qdec, kdec], axis=1)[:, :, None],
                                   (RT_HEADS, 2 * c, 128)))
        sd.append(jnp.broadcast_to(jnp.exp(c * lg)[:, None, None], (RT_HEADS, 8, 128)))
    return (jnp.cos(ang), jnp.sin(ang), jnp.stack(intra), jnp.stack(qk), jnp.stack(sd))


def _retention(p_lat, p_ctx, tables, batch, need_ctx_out):
    l = p_lat.shape[0] // batch
    lc = p_ctx.shape[0] // batch
    cos, sin, intra, qk, sd = tables
    c = RET_CHUNK

    def col(base):
        return lambda b, h: (b, base + h)

    kq, kk_, kv, kg = COL_RQ * 128 // RT_HK, COL_RK * 128 // RT_HK, COL_RV * 128 // RT_HV, COL_RG * 128 // RT_HV
    lat_specs = [pl.BlockSpec((l, RT_HK), col(kq)), pl.BlockSpec((l, RT_HK), col(kk_)),
                 pl.BlockSpec((l, RT_HV), col(kv)), pl.BlockSpec((l, RT_HV), col(kg))]
    if need_ctx_out:
        ctx_specs = [pl.BlockSpec((lc, RT_HK), col(kq)), pl.BlockSpec((lc, RT_HK), col(kk_)),
                     pl.BlockSpec((lc, RT_HV), col(kv)), pl.BlockSpec((lc, RT_HV), col(kg))]
    else:
        ctx_specs = [pl.BlockSpec((lc, RT_HK), col(kk_)), pl.BlockSpec((lc, RT_HV), col(kv))]
    tab_specs = [pl.BlockSpec((lc + l, 128), lambda b, h: (0, 0)),
                 pl.BlockSpec((lc + l, 128), lambda b, h: (0, 0)),
                 pl.BlockSpec((2, 1, c, c), lambda b, h: (0, h, 0, 0)),
                 pl.BlockSpec((2, 1, 2 * c, 128), lambda b, h: (0, h, 0, 0)),
                 pl.BlockSpec((2, 1, 8, 128), lambda b, h: (0, h, 0, 0))]
    args = [p_lat] * 4 + [p_ctx] * len(ctx_specs) + [cos, sin, intra, qk, sd]
    out_specs = [pl.BlockSpec((l, RT_HV), lambda b, h: (b, h))]
    out_shape = [jax.ShapeDtypeStruct((batch * l, RT_HEADS * RT_HV), BF16)]
    scratch = [pltpu.VMEM((l, RT_HK), BF16), pltpu.VMEM((l, RT_HK), BF16)]
    if need_ctx_out:
        out_specs.append(pl.BlockSpec((lc, RT_HV), lambda b, h: (b, h)))
        out_shape.append(jax.ShapeDtypeStruct((batch * lc, RT_HEADS * RT_HV), BF16))
        scratch += [pltpu.VMEM((lc, RT_HK), BF16), pltpu.VMEM((lc, RT_HK), BF16),
                    pltpu.VMEM((l, RT_HV), F32), pltpu.VMEM((lc, RT_HV), F32)]
    else:
        scratch += [pltpu.VMEM((lc, RT_HK), BF16), pltpu.VMEM((l, RT_HV), F32)]
    scratch += [pltpu.VMEM((RT_HK, RT_HV), F32)] * 2
    outs = pl.pallas_call(
        functools.partial(_ret_kernel, need_ctx_out=need_ctx_out, n_lat=l // c, n_ctx=lc // c),
        grid=(batch, RT_HEADS),
        in_specs=lat_specs + ctx_specs + tab_specs,
        out_specs=out_specs,
        out_shape=out_shape,
        scratch_shapes=scratch,
        compiler_params=_cparams(("arbitrary", "arbitrary")),
        name="retention_scan",
    )(*args)
    return outs[0], (outs[1] if need_ctx_out else None)


def _conv_shift_matrices(grid):
    rb = CONV_ROWS
    t = np.arange(rb)
    edge_lo = (t % GRID_W == 0) if grid else (t == 0)
    edge_hi = (t % GRID_W == GRID_W - 1) if grid else (t == rb - 1)
    s0 = ((t[:, None] - 1 == t[None, :]) & ~edge_lo[:, None]).astype(np.float32)
    s2 = ((t[:, None] + 1 == t[None, :]) & ~edge_hi[:, None]).astype(np.float32)
    return s0, s2


def _short_conv_kernel(x_ref, w_ref, b_ref, s0_ref, s2_ref, o_ref, xpad_ref, *, grid):
    l, tc = x_ref.shape
    p = GRID_W
    rb = CONV_ROWS
    zeros = jnp.zeros((p, tc), BF16)
    xpad_ref[0:p, :] = zeros
    xpad_ref[p + l:p + l + p, :] = zeros
    xpad_ref[p:p + l, :] = x_ref[...]
    w = [w_ref[k:k + 1, :].astype(BF16) for k in range(9)]
    rows = range(3) if grid else (1,)

    def body(i, carry):
        r0 = i * rb
        part = []
        for dj in range(3):
            acc = None
            for di in rows:
                start = pl.multiple_of(r0 + p + (di - 1) * GRID_W, GRID_W)
                t = xpad_ref[pl.ds(start, rb), :] * w[di * 3 + dj]
                acc = t if acc is None else acc + t
            part.append(acc)
        out = (part[1].astype(F32) + _dot(s0_ref[...], part[0]) + _dot(s2_ref[...], part[2])
               + b_ref[...])
        o_ref[pl.ds(pl.multiple_of(r0, rb), rb), :] = out.astype(o_ref.dtype)
        return carry

    lax.fori_loop(0, l // rb, body, 0, unroll=min(4, l // rb))


def _short_conv(p_arr, w9, bias, batch, grid):
    l = p_arr.shape[0] // batch
    assert l % CONV_ROWS == 0 and (grid or l == CONV_ROWS)
    c3 = 3 * HY_DIM
    tc = 256
    base = COL_HY * 128 // tc
    s0, s2 = _conv_shift_matrices(grid)
    return pl.pallas_call(
        functools.partial(_short_conv_kernel, grid=grid),
        grid=(batch, c3 // tc),
        in_specs=[pl.BlockSpec((l, tc), lambda b, j: (b, base + j)),
                  pl.BlockSpec((9, tc), lambda b, j: (0, j)),
                  pl.BlockSpec((1, tc), lambda b, j: (0, j)),
                  pl.BlockSpec((CONV_ROWS, CONV_ROWS), lambda b, j: (0, 0)),
                  pl.BlockSpec((CONV_ROWS, CONV_ROWS), lambda b, j: (0, 0))],
        out_specs=pl.BlockSpec((l, tc), lambda b, j: (b, j)),
        out_shape=jax.ShapeDtypeStruct((batch * l, c3), BF16),
        scratch_shapes=[pltpu.VMEM((l + 2 * GRID_W, tc), BF16)],
        compiler_params=_cparams(("arbitrary", "arbitrary")),
        name="hyena_short_conv",
    )(p_arr, w9, bias.reshape(1, c3), jnp.asarray(s0, BF16), jnp.asarray(s2, BF16))


def _dft_matrices(l):
    n = 2 * l
    s = int(round(math.sqrt(n)))
    while n % s:
        s -= 1
    q = n // s
    f = np.arange(l, dtype=np.int64)
    ang_a = 2.0 * np.pi * ((f[:, None] * s * np.arange(q)[None, :]) % n) / n
    ang_b = 2.0 * np.pi * ((f[:, None] * np.arange(s)[None, :]) % n) / n
    t = np.arange(n)
    rep = (t[None, :] // s == np.arange(q)[:, None]).astype(np.float32)
    til = (t[None, :] % s == np.arange(s)[:, None]).astype(np.float32)
    h = DFT_HALF
    row = lambda j: (j, 0)
    const = lambda j: (0, 0)
    fwd, inv = pl.pallas_call(
        _dft_build_kernel,
        grid=(l // h,),
        in_specs=[pl.BlockSpec((h, q), row), pl.BlockSpec((h, q), row),
                  pl.BlockSpec((h, s), row), pl.BlockSpec((h, s), row),
                  pl.BlockSpec((q, n), const), pl.BlockSpec((s, n), const)],
        out_specs=[pl.BlockSpec((1, 2, h, n), lambda j: (j, 0, 0, 0)),
                   pl.BlockSpec((h, n), row)],
        out_shape=[jax.ShapeDtypeStruct((l // h, 2, h, n), BF16),
                   jax.ShapeDtypeStruct((l, n), BF16)],
        compiler_params=_cparams(("arbitrary",)),
        name="dft_build",
    )(jnp.asarray(np.cos(ang_a), F32), jnp.asarray(np.sin(ang_a), F32),
      jnp.asarray(np.cos(ang_b), F32), jnp.asarray(np.sin(ang_b), F32),
      jnp.asarray(rep, BF16), jnp.asarray(til, BF16))
    return fwd.reshape(n, n), inv


def _dft_build_kernel(ca_ref, sa_ref, cb_ref, sb_ref, rep_ref, til_ref, fwd_ref, inv_ref):
    h, n = inv_ref.shape

    def spread(x_ref, m_ref):
        p1, p2, p3 = _split3(x_ref[...])
        m = m_ref[...]
        return _dot(p1, m) + _dot(p2, m) + _dot(p3, m)

    ca, sa = spread(ca_ref, rep_ref), spread(sa_ref, rep_ref)
    cb, sb = spread(cb_ref, til_ref), spread(sb_ref, til_ref)
    cosm = ca * cb - sa * sb
    nsin = -(sa * cb + ca * sb)
    first = pl.program_id(0) == 0
    rowi = lax.broadcasted_iota(jnp.int32, (h, n), 0)
    coli = lax.broadcasted_iota(jnp.int32, (h, n), 1)
    alt_col = (1 - 2 * (coli % 2)).astype(F32)
    fwd_ref[0, 0] = cosm.astype(BF16)
    fwd_ref[0, 1] = jnp.where(first & (rowi == 0), alt_col, nsin).astype(BF16)
    rowg = lax.broadcasted_iota(jnp.int32, (h, h), 0) + pl.program_id(0) * h
    col0 = lax.broadcasted_iota(jnp.int32, (h, h), 1) == 0
    alt_row = (1 - 2 * (rowg % 2)).astype(F32)
    pieces = []
    for jj in range(n // (2 * h)):
        cp = cosm[:, jj * h:(jj + 1) * h] * (2.0 / n)
        ip = nsin[:, jj * h:(jj + 1) * h] * (2.0 / n)
        if jj == 0:
            cp = jnp.where(col0, 1.0 / n, cp)
            ip = jnp.where(col0, alt_row * (1.0 / n), ip)
        pieces += [cp.astype(BF16), ip.astype(BF16)]
    inv_ref[...] = jnp.concatenate(pieces, axis=1)


def _filter_spectrum_kernel(f_ref, top_ref, bot_ref, o_ref):
    h = DFT_HALF
    bl = top_ref.shape[0]
    acc = _dot(f_ref[:, 0:bl], top_ref[...]) + _dot(f_ref[:, bl:2 * bl], bot_ref[...])
    kr, km = acc[:h], acc[h:]
    row0 = (lax.broadcasted_iota(jnp.int32, kr.shape, 0) == 0) & (pl.program_id(2) == 0)
    o_ref[0, 0:h, :] = kr.astype(o_ref.dtype)
    o_ref[0, h:2 * h, :] = jnp.where(row0, 0.0, km).astype(o_ref.dtype)
    o_ref[0, 2 * h:3 * h, :] = jnp.where(row0, km, kr).astype(o_ref.dtype)


def _filter_spectrum(fwd, kern_bf, nb):
    n = fwd.shape[0]
    bl = n // 2
    nc = kern_bf.shape[1]
    tm, tn = 2 * DFT_HALF, 1024
    nd = 2 * nb - 1
    wrap = 2 * nb
    return pl.pallas_call(
        _filter_spectrum_kernel,
        grid=(nc // tn, nd, n // tm),
        in_specs=[pl.BlockSpec((tm, n), lambda c, d, t: (t, 0)),
                  pl.BlockSpec((bl, tn), lambda c, d, t: ((d - (nb - 1)) % wrap, c)),
                  pl.BlockSpec((bl, tn), lambda c, d, t: ((d - nb) % wrap, c))],
        out_specs=pl.BlockSpec((1, 3 * DFT_HALF, tn), lambda c, d, t: (d, t, c)),
        out_shape=jax.ShapeDtypeStruct((nd, (n // tm) * 3 * DFT_HALF, nc), BF16),
        compiler_params=_cparams(("arbitrary", "arbitrary", "arbitrary")),
        name="hyena_filter_dft",
    )(fwd, kern_bf, kern_bf)


def _dft_fwd_kernel(f_ref, z_ref, k_ref, y_ref):
    h = DFT_HALF
    bl = f_ref.shape[1]
    nb = y_ref.shape[1]
    f = f_ref[...]
    spec = [_dot(f, z_ref[j * bl:(j + 1) * bl, :]).astype(BF16) for j in range(nb)]
    for i in range(nb):
        yr = yi = None
        for j in range(nb):
            d = i - j + nb - 1
            kr, ki, kr2 = k_ref[d, 0:h, :], k_ref[d, h:2 * h, :], k_ref[d, 2 * h:3 * h, :]
            ur, ui = spec[j][:h], spec[j][h:]
            tr = ur * kr - ui * ki
            ti = ur * ki + ui * kr2
            yr = tr if yr is None else yr + tr
            yi = ti if yi is None else yi + ti
        y_ref[0, i, 0:h, :] = yr.astype(y_ref.dtype)
        y_ref[0, i, h:2 * h, :] = yi.astype(y_ref.dtype)


def _dft_fwd(fwd, z_arr, z_col, kf, k_col, batch, nb):
    n = fwd.shape[0]
    bl = n // 2
    l = nb * bl
    tm, tn = 2 * DFT_HALF, 512
    cpt = HY_DIM // tn
    nd = kf.shape[0]
    return pl.pallas_call(
        _dft_fwd_kernel,
        grid=(cpt, n // tm, batch),
        in_specs=[pl.BlockSpec((tm, bl), lambda c, t, b: (t, 0)),
                  pl.BlockSpec((l, tn), lambda c, t, b: (b, z_col * cpt + c)),
                  pl.BlockSpec((nd, 3 * DFT_HALF, tn), lambda c, t, b: (0, t, k_col * cpt + c))],
        out_specs=pl.BlockSpec((1, nb, tm, tn), lambda c, t, b: (b, 0, t, c)),
        out_shape=jax.ShapeDtypeStruct((batch, nb, n, HY_DIM), BF16),
        compiler_params=_cparams(("arbitrary", "arbitrary", "arbitrary")),
        name="hyena_dft_fwd",
    )(fwd, z_arr, kf)


def _dft_inv_kernel(g_ref, y_ref, gate_ref, z_ref, skip_ref, o_ref):
    conv = _dot(g_ref[...], y_ref[0, 0])
    z = z_ref[...].astype(F32)
    o_ref[...] = (gate_ref[...].astype(F32) * (conv + skip_ref[...] * z)).astype(o_ref.dtype)


def _dft_inv(inv, y, gate_arr, gate_col, z_arr, z_col, skip, batch, nb):
    bl, n = inv.shape
    tn = HY_DIM
    return pl.pallas_call(
        _dft_inv_kernel,
        grid=(batch, nb),
        in_specs=[pl.BlockSpec((bl, n), lambda b, i: (0, 0)),
                  pl.BlockSpec((1, 1, n, tn), lambda b, i: (b, i, 0, 0)),
                  pl.BlockSpec((bl, tn), lambda b, i: (b * nb + i, gate_col)),
                  pl.BlockSpec((bl, tn), lambda b, i: (b * nb + i, z_col)),
                  pl.BlockSpec((1, tn), lambda b, i: (0, 0))],
        out_specs=pl.BlockSpec((bl, tn), lambda b, i: (b * nb + i, 0)),
        out_shape=jax.ShapeDtypeStruct((batch * nb * bl, tn), BF16),
        compiler_params=_cparams(("arbitrary", "arbitrary")),
        name="hyena_dft_inv",
    )(inv, y, gate_arr, z_arr, skip.reshape(1, tn))


def _tap_features(l):
    f32 = np.float32
    t01 = np.linspace(0.0, 1.0, l, dtype=f32)[:, None]
    ang = f32(2.0 * math.pi) * np.arange(l, dtype=f32)[:, None] / f32(l)
    bands = np.linspace(1e-4, HY_BANDS - 1, HY_BANDS, dtype=f32)[None, :]
    z = np.concatenate([t01, np.cos(bands * ang), -np.sin(bands * ang)], axis=-1).astype(f32)
    pos = np.concatenate([np.arange(l), [0], np.arange(l - 1, 0, -1)])
    zz = np.zeros((2 * l, 128), f32)
    zz[:, :z.shape[1]] = z[pos]
    zz[l] = 0.0
    return zz


def _dot_hi(a, b):
    a1 = a.astype(BF16)
    a2 = (a - a1.astype(F32)).astype(BF16)
    b1 = b.astype(BF16)
    b2 = (b - b1.astype(F32)).astype(BF16)
    return _dot(a1, b1) + _dot(a1, b2) + _dot(a2, b1)


def _filter_kernel(zz_ref, w1_ref, b1_ref, fr_ref, w2_ref, b2_ref, w3f_ref, w3b_ref, dl_ref,
                   o_ref, hdn_ref):
    n = zz_ref.shape[0]
    l = n // 2

    @pl.when((pl.program_id(0) == 0) & (pl.program_id(1) == 0))
    def _():
        h1 = jnp.sin(fr_ref[0:1, :] * (_dot_hi(zz_ref[...], w1_ref[...]) + b1_ref[...]))
        hdn_ref[...] = jnp.sin(fr_ref[1:2, :] * (_dot_hi(h1, w2_ref[...]) + b2_ref[...]))

    filt = jnp.concatenate([_dot_hi(hdn_ref[0:l, :], w3f_ref[...]),
                            _dot_hi(hdn_ref[l:n, :], w3b_ref[...])], axis=0)
    decay = jnp.exp(-zz_ref[:, 0:1] * dl_ref[...])
    rowi = lax.broadcasted_iota(jnp.int32, filt.shape, 0)
    kern = jnp.where(rowi == l, 0.0, filt * decay)
    o_ref[...] = (kern / jnp.sum(jnp.abs(kern), axis=0, keepdims=True)).astype(o_ref.dtype)


def _hyena_filters(l, w1, b1, freq, w2, b2, w3):
    n = 2 * l
    tn = 256
    ff = w2.shape[0]
    cpo = HY_DIM // tn
    deltas = np.abs(np.linspace(math.log(HY_DECAY_TARGET) / HY_DECAY_PCT_LONG,
                                math.log(HY_DECAY_TARGET) / HY_DECAY_PCT_SHORT, HY_DIM,
                                dtype=np.float32)).reshape(1, HY_DIM)
    const = lambda o, c: (0, 0)
    return pl.pallas_call(
        _filter_kernel,
        grid=(HY_ORDER, cpo),
        in_specs=[pl.BlockSpec((n, 128), const), pl.BlockSpec((128, ff), const),
                  pl.BlockSpec((1, ff), const), pl.BlockSpec((2, ff), const),
                  pl.BlockSpec((ff, ff), const), pl.BlockSpec((1, ff), const),
                  pl.BlockSpec((ff, tn), lambda o, c: (0, o * 2 * cpo + c)),
                  pl.BlockSpec((ff, tn), lambda o, c: (0, o * 2 * cpo + cpo + c)),
                  pl.BlockSpec((1, tn), lambda o, c: (0, c))],
        out_specs=pl.BlockSpec((n, tn), lambda o, c: (0, o * cpo + c)),
        out_shape=jax.ShapeDtypeStruct((n, HY_ORDER * HY_DIM), BF16),
        scratch_shapes=[pltpu.VMEM((n, ff), F32)],
        compiler_params=_cparams(("arbitrary", "arbitrary")),
        name="hyena_filter_mlp",
    )(jnp.asarray(_tap_features(l)), jnp.pad(w1, ((0, 128 - w1.shape[0]), (0, 0))),
      b1.reshape(1, ff), freq, w2, b2.reshape(1, ff), w3, w3, jnp.asarray(deltas))


def _hyena(p_arr, lp, dft, batch, grid):
    l = p_arr.shape[0] // batch
    fwd, inv = dft
    uc = _short_conv(p_arr, lp['hy_conv_w'].reshape(9, 3 * HY_DIM), lp['hy_conv_b'], batch, grid)
    kern = _hyena_filters(l, lp['hy_ff_w1'], lp['hy_ff_b1'], lp['hy_ff_freq'],
                          lp['hy_ff_w2'], lp['hy_ff_b2'], lp['hy_ff_w3'])
    nb = l // inv.shape[0]
    kf = _filter_spectrum(fwd, kern, nb)
    z_arr, z_col = uc, 0
    for n in range(HY_ORDER):
        y = _dft_fwd(fwd, z_arr, z_col, kf, n, batch, nb)
        z_arr = _dft_inv(inv, y, uc, n + 1, z_arr, z_col, lp['hy_skip'][n], batch, nb)
        z_col = 0
    return z_arr


def _merge_kernel(yhy_ref, ohg_ref, hgg_ref, yrt_ref, b0_ref, b1_ref, b2_ref, x_ref, mod_ref,
                  nw_ref, phy_ref, phg_ref, prt_ref, wo_ref, lng_ref, lnb_ref, o_ref):
    o = ohg_ref[...]
    yhg = (o * lax.rsqrt(jnp.mean(o * o, axis=-1, keepdims=True) + LN_EPS) * nw_ref[...]
           * _silu(hgg_ref[...].astype(F32)))
    m = (jax.nn.sigmoid(b0_ref[...].astype(F32)) * _dot(yhy_ref[...], phy_ref[...])
         + jax.nn.sigmoid(b1_ref[...].astype(F32)) * _dot(yhg.astype(BF16), phg_ref[...])
         + jax.nn.sigmoid(b2_ref[...].astype(F32)) * _dot(yrt_ref[...], prt_ref[...]))
    t = _dot(m.astype(BF16), wo_ref[...])
    gt1 = mod_ref[0][2:3]
    o_ref[...] = _layer_norm_rows(DN_ALPHA * x_ref[...] + gt1 * t, lng_ref[...], lnb_ref[...])


def _merge(y_hy, o_hg, y_rt, p_arr, x, modp, lp, wbf, rows_per_batch):
    m = x.shape[0]
    tm = min(512, m)
    d = D_MODEL
    mi = _mod_index(rows_per_batch, tm)
    row = lambda i: (i, 0)
    const = lambda i: (0, 0)
    gcol = COL_HG * 128 // d
    bcol = COL_BR * 128 // d
    return pl.pallas_call(
        _merge_kernel,
        grid=(m // tm,),
        in_specs=[pl.BlockSpec((tm, d), row), pl.BlockSpec((tm, d), row),
                  pl.BlockSpec((tm, d), lambda i: (i, gcol)),
                  pl.BlockSpec((tm, 2 * d), row),
                  pl.BlockSpec((tm, d), lambda i: (i, bcol)),
                  pl.BlockSpec((tm, d), lambda i: (i, bcol + 1)),
                  pl.BlockSpec((tm, d), lambda i: (i, bcol + 2)),
                  pl.BlockSpec((tm, d), row),
                  pl.BlockSpec((1, 8, d), lambda i: (mi(i), 0, 0)),
                  pl.BlockSpec((1, d), const),
                  pl.BlockSpec((d, d), const), pl.BlockSpec((d, d), const),
                  pl.BlockSpec((2 * d, d), const), pl.BlockSpec((d, d), const),
                  pl.BlockSpec((1, d), const), pl.BlockSpec((1, d), const)],
        out_specs=pl.BlockSpec((tm, d), row),
        out_shape=jax.ShapeDtypeStruct((m, d), F32),
        compiler_params=_cparams(("arbitrary",)),
        name="merge_out_ln",
    )(y_hy, o_hg, p_arr, y_rt, p_arr, p_arr, p_arr, x, modp, lp['hg_norm_w'].reshape(1, d),
      wbf['p_hy'], wbf['p_hg'], wbf['p_rt'], wbf['w_o'],
      lp['ln1_g'].reshape(1, d), lp['ln1_b'].reshape(1, d))


def _ffn_kernel(x_ref, mod_ref, w1_ref, w3_ref, w2_ref, lng_ref, lnb_ref, o_ref, h_ref, acc_ref):
    k = pl.program_id(1)
    m = mod_ref[0]

    @pl.when(k == 0)
    def _():
        h_ref[...] = (x_ref[...] * (1.0 + m[4:5]) + m[3:4]).astype(BF16)
        acc_ref[...] = jnp.zeros(acc_ref.shape, F32)

    h = h_ref[...]
    u = _silu(_dot(h, w1_ref[...])) * _dot(h, w3_ref[...])
    acc_ref[...] += _dot(u.astype(BF16), w2_ref[...])

    @pl.when(k == pl.num_programs(1) - 1)
    def _():
        o_ref[...] = _layer_norm_rows(DN_ALPHA * x_ref[...] + m[5:6] * acc_ref[...],
                                      lng_ref[...], lnb_ref[...])


def _ffn_dense(x, modp, w1, w3, w2, ln_g, ln_b, rows_per_batch):
    m = x.shape[0]
    d = D_MODEL
    dff = w1.shape[1]
    tm = min(1024, rows_per_batch or m)
    tf = dff // 2
    mi = _mod_index(rows_per_batch, tm)
    return pl.pallas_call(
        _ffn_kernel,
        grid=(m // tm, dff // tf),
        in_specs=[pl.BlockSpec((tm, d), lambda i, k: (i, 0)),
                  pl.BlockSpec((1, 8, d), lambda i, k: (mi(i), 0, 0)),
                  pl.BlockSpec((d, tf), lambda i, k: (0, k)),
                  pl.BlockSpec((d, tf), lambda i, k: (0, k)),
                  pl.BlockSpec((tf, d), lambda i, k: (k, 0)),
                  pl.BlockSpec((1, d), lambda i, k: (0, 0)),
                  pl.BlockSpec((1, d), lambda i, k: (0, 0))],
        out_specs=pl.BlockSpec((tm, d), lambda i, k: (i, 0)),
        out_shape=jax.ShapeDtypeStruct((m, d), F32),
        scratch_shapes=[pltpu.VMEM((tm, d), BF16), pltpu.VMEM((tm, d), F32)],
        compiler_params=_cparams(("arbitrary", "arbitrary")),
        name="ffn_dense_ln",
    )(x, modp, w1, w3, w2, ln_g.reshape(1, d), ln_b.reshape(1, d))


def _router_kernel(x_ref, mod_ref, r_ref, h_ref, lg_ref):
    m = mod_ref[0]
    h = x_ref[...] * (1.0 + m[4:5]) + m[3:4]
    h_ref[...] = h
    a1, a2, a3 = _split3(h)
    r1, r2, r3 = _split3(r_ref[...])
    lg_ref[...] = (_dot(a1, r1) + _dot(a1, r2) + _dot(a2, r1)
                   + _dot(a2, r2) + _dot(a1, r3) + _dot(a3, r1))


def _router(x, modp, router_pad, rows_per_batch):
    m = x.shape[0]
    d = D_MODEL
    tm = min(512, m)
    mi = _mod_index(rows_per_batch, tm)
    return pl.pallas_call(
        _router_kernel,
        grid=(m // tm,),
        in_specs=[pl.BlockSpec((tm, d), lambda i: (i, 0)),
                  pl.BlockSpec((1, 8, d), lambda i: (mi(i), 0, 0)),
                  pl.BlockSpec((d, 128), lambda i: (0, 0))],
        out_specs=[pl.BlockSpec((tm, d), lambda i: (i, 0)),
                   pl.BlockSpec((tm, 128), lambda i: (i, 0))],
        out_shape=[jax.ShapeDtypeStruct((m, d), F32), jax.ShapeDtypeStruct((m, 128), F32)],
        compiler_params=_cparams(("arbitrary",)),
        name="moe_router",
    )(x, modp, router_pad)


def _moe_ffn_kernel(be_ref, nu_ref, xp_ref, w1_ref, w3_ref, w2_ref, o_ref, x_ref, acc_ref):
    j = pl.program_id(0)
    k = pl.program_id(1)

    @pl.when(j < nu_ref[0])
    def _():
        @pl.when(k == 0)
        def _():
            x_ref[...] = xp_ref[...].astype(BF16)
            acc_ref[...] = jnp.zeros(acc_ref.shape, F32)

        x = x_ref[...]
        u = _silu(_dot(x, w1_ref[0])) * _dot(x, w3_ref[0])
        acc_ref[...] += _dot(u.astype(BF16), w2_ref[0])

        @pl.when(k == pl.num_programs(1) - 1)
        def _():
            o_ref[...] = acc_ref[...]

    @pl.when((j >= nu_ref[0]) & (k == pl.num_programs(1) - 1))
    def _():
        o_ref[...] = jnp.zeros(o_ref.shape, F32)


def _moe_ffn(xb, block_e, n_used, w1, w3, w2):
    ns = xb.shape[0]
    d = D_MODEL
    tm = MOE_ROWS
    dex = w1.shape[2]
    tf = dex // 2
    grid_spec = pltpu.PrefetchScalarGridSpec(
        num_scalar_prefetch=2,
        grid=(ns // tm, dex // tf),
        in_specs=[pl.BlockSpec((tm, d), lambda j, k, be, nu: (j, 0)),
                  pl.BlockSpec((1, d, tf), lambda j, k, be, nu: (be[j], 0, k)),
                  pl.BlockSpec((1, d, tf), lambda j, k, be, nu: (be[j], 0, k)),
                  pl.BlockSpec((1, tf, d), lambda j, k, be, nu: (be[j], k, 0))],
        out_specs=pl.BlockSpec((tm, d), lambda j, k, be, nu: (j, 0)),
        scratch_shapes=[pltpu.VMEM((tm, d), BF16), pltpu.VMEM((tm, d), F32)])
    return pl.pallas_call(
        _moe_ffn_kernel,
        grid_spec=grid_spec,
        out_shape=jax.ShapeDtypeStruct((ns, d), F32),
        compiler_params=_cparams(("arbitrary", "arbitrary")),
        name="moe_expert_ffn",
    )(block_e, n_used, xb, w1, w3, w2)


def _combine_kernel(x_ref, mod_ref, y0_ref, y1_ref, g_ref, lng_ref, lnb_ref, o_ref):
    m = mod_ref[0]
    g = g_ref[...]
    f = g[:, 0:1] * y0_ref[...] + g[:, 1:2] * y1_ref[...]
    o_ref[...] = _layer_norm_rows(DN_ALPHA * x_ref[...] + m[5:6] * f, lng_ref[...], lnb_ref[...])


def _moe_combine(x, modp, y0, y1, gate_pad, ln_g, ln_b, rows_per_batch):
    m = x.shape[0]
    d = D_MODEL
    tm = min(512, m)
    mi = _mod_index(rows_per_batch, tm)
    row = lambda i: (i, 0)
    return pl.pallas_call(
        _combine_kernel,
        grid=(m // tm,),
        in_specs=[pl.BlockSpec((tm, d), row),
                  pl.BlockSpec((1, 8, d), lambda i: (mi(i), 0, 0)),
                  pl.BlockSpec((tm, d), row), pl.BlockSpec((tm, d), row),
                  pl.BlockSpec((tm, 128), row),
                  pl.BlockSpec((1, d), lambda i: (0, 0)), pl.BlockSpec((1, d), lambda i: (0, 0))],
        out_specs=pl.BlockSpec((tm, d), row),
        out_shape=jax.ShapeDtypeStruct((m, d), F32),
        compiler_params=_cparams(("arbitrary",)),
        name="moe_combine_ln",
    )(x, modp, y0, y1, gate_pad, ln_g.reshape(1, d), ln_b.reshape(1, d))


def _cast_kernel(x_ref, o_ref):
    o_ref[...] = x_ref[...].astype(o_ref.dtype)


def _to_bf16(w, group):
    _, e, a, b = w.shape
    ta = a // 4
    return pl.pallas_call(
        _cast_kernel,
        grid=(e, a // ta),
        in_specs=[pl.BlockSpec((None, 1, ta, b), lambda i, r: (group, i, r, 0))],
        out_specs=pl.BlockSpec((1, ta, b), lambda i, r: (i, r, 0)),
        out_shape=jax.ShapeDtypeStruct((e, a, b), BF16),
        compiler_params=_cparams(("arbitrary", "arbitrary")),
        name="weight_cast",
    )(w)


def _moe(x, modp, router, w1, w3, w2, ln_g, ln_b, rows_per_batch):
    n, d = x.shape
    e = router.shape[1]
    h2, logits = _router(x, modp, jnp.pad(router, ((0, 0), (0, 128 - e))), rows_per_batch)
    top_val, top_idx = lax.top_k(logits[:, :e], TOP_K)
    gate = jax.nn.softmax(top_val, axis=-1)
    flat_e = top_idx.reshape(-1)
    flat_t = jnp.repeat(jnp.arange(n, dtype=jnp.int32), TOP_K)
    order = jnp.argsort(flat_e, stable=True).astype(jnp.int32)
    rank = jnp.argsort(order).astype(jnp.int32)
    counts = jnp.sum((flat_e[:, None] == jnp.arange(e)[None, :]).astype(jnp.int32), axis=0)
    starts = jnp.cumsum(counts) - counts
    padded = (counts + MOE_ROWS - 1) // MOE_ROWS * MOE_ROWS
    pad_end = jnp.cumsum(padded)
    pad_start = pad_end - padded
    n_blocks = -(-(n * TOP_K) // MOE_ROWS) + e
    n_slots = n_blocks * MOE_ROWS
    block_start = jnp.arange(n_blocks) * MOE_ROWS
    block_e = jnp.minimum(jnp.sum(block_start[:, None] >= pad_end[None, :], axis=1), e - 1).astype(jnp.int32)
    n_used = (pad_end[-1] // MOE_ROWS).astype(jnp.int32).reshape(1)
    slot_e = jnp.repeat(block_e, MOE_ROWS)
    slot_off = jnp.arange(n_slots, dtype=jnp.int32) - pad_start[slot_e]
    slot_valid = slot_off < counts[slot_e]
    slot_src = jnp.where(slot_valid, starts[slot_e] + slot_off, 0)
    slot_tok = jnp.where(slot_valid, flat_t[order[slot_src]], 0).astype(jnp.int32)
    pos = (pad_start[flat_e] + rank - starts[flat_e]).astype(jnp.int32).reshape(n, TOP_K)
    xb = jnp.take(h2, slot_tok, axis=0, mode="clip")
    yb = _moe_ffn(xb, block_e, n_used, w1, w3, w2)
    y0 = jnp.take(yb, pos[:, 0], axis=0, mode="clip")
    y1 = jnp.take(yb, pos[:, 1], axis=0, mode="clip")
    gate_pad = jnp.pad(gate, ((0, 0), (0, 128 - TOP_K)))
    return _moe_combine(x, modp, y0, y1, gate_pad, ln_g, ln_b, rows_per_batch)


def kernel(x, c, ctx, c_ctx, ada_w, ada_b, w_in, hy_conv_w, hy_conv_b, hy_ff_w1, hy_ff_b1, hy_ff_freq, hy_ff_w2, hy_ff_b2, hy_ff_w3, hy_skip, hg_lb_logits, hg_norm_w, p_hy, p_hg, p_rt, w_o, ln1_g, ln1_b, ln2_g, ln2_b, ffn_w1, ffn_w3, ffn_w2, moe_router, moe_w1, moe_w3, moe_w2):
    batch, l, d = x.shape
    lc = ctx.shape[1]
    assert d == D_MODEL and batch <= 8
    assert l % 512 == 0 and l % GRID_W == 0 and lc % max(RET_CHUNK, DFT_HALF) == 0

    cs = jnp.cumsum(jax.nn.softmax(hg_lb_logits.astype(F32), axis=1), axis=1)
    lower_bounds = cs - cs[:, :1]
    cc = jnp.zeros((16, d), F32).at[:batch].set(c).at[8].set(c_ctx)
    ret_tables = _retention_tables(lc, l)
    dft_lat = _dft_matrices(min(HY_BLOCK, l))
    dft_ctx = _dft_matrices(min(HY_BLOCK, lc))

    x_lat = x.reshape(batch * l, d)
    x_ctx = ctx.reshape(batch * lc, d)
    for i in range(DEPTH):
        need_ctx_out = i < DEPTH - 1
        use_moe = i % 2 == 1
        g = i // 2
        lp = {'hy_conv_w': hy_conv_w[i], 'hy_conv_b': hy_conv_b[i], 'hy_ff_w1': hy_ff_w1[i],
              'hy_ff_b1': hy_ff_b1[i], 'hy_ff_freq': hy_ff_freq[i], 'hy_ff_w2': hy_ff_w2[i],
              'hy_ff_b2': hy_ff_b2[i], 'hy_ff_w3': hy_ff_w3[i], 'hy_skip': hy_skip[i],
              'hg_norm_w': hg_norm_w[i], 'ln1_g': ln1_g[i], 'ln1_b': ln1_b[i]}
        wbf = {'p_hy': p_hy[i].astype(BF16), 'p_hg': p_hg[i].astype(BF16),
               'p_rt': p_rt[i].astype(BF16), 'w_o': w_o[i].astype(BF16)}
        mod = _ada(cc, ada_w, i, ada_b[i])
        modp = jnp.pad(mod.reshape(16, 6, d), ((0, 0), (0, 2), (0, 0)))
        lb = lower_bounds[:, i]
        gl_tab = jnp.pad(jnp.stack([jnp.maximum(jnp.log(lb), -1e30), jnp.log1p(-lb), 1.0 - lb], axis=1),
                         ((0, 0), (0, 5), (0, 0)))

        p_lat = _proj(x_lat, modp, w_in, i, N_IN_COLS, l)
        p_ctx = _proj(x_ctx, modp, w_in, i, N_IN_COLS if need_ctx_out else N_STATE_COLS, None)

        o_hg_l, o_hg_c = _gla(p_lat, p_ctx, gl_tab, batch, need_ctx_out)
        y_rt_l, y_rt_c = _retention(p_lat, p_ctx, ret_tables, batch, need_ctx_out)
        y_hy_l = _hyena(p_lat, lp, dft_lat, batch, True)
        x_lat_new = _merge(y_hy_l, o_hg_l, y_rt_l, p_lat, x_lat, modp, lp, wbf, l)
        if need_ctx_out:
            y_hy_c = _hyena(p_ctx, lp, dft_ctx, batch, False)
            x_ctx = _merge(y_hy_c, o_hg_c, y_rt_c, p_ctx, x_ctx, modp, lp, wbf, None)
        x_lat = x_lat_new

        if use_moe:
            if need_ctx_out:
                raise NotImplementedError("MoE layer with a context output is not part of this trunk")
            w1, w3, w2 = _to_bf16(moe_w1, g), _to_bf16(moe_w3, g), _to_bf16(moe_w2, g)
            x_lat = _moe(x_lat, modp, moe_router[g], w1, w3, w2, ln2_g[i], ln2_b[i], l)
        else:
            w1, w3, w2 = ffn_w1[g].astype(BF16), ffn_w3[g].astype(BF16), ffn_w2[g].astype(BF16)
            x_lat = _ffn_dense(x_lat, modp, w1, w3, w2, ln2_g[i], ln2_b[i], l)
            if need_ctx_out:
                x_ctx = _ffn_dense(x_ctx, modp, w1, w3, w2, ln2_g[i], ln2_b[i], None)
    return x_lat.reshape(batch, l, d)
```

```python
import functools
import math

import numpy as np
import jax
import jax.numpy as jnp
from jax import lax
from jax.experimental import pallas as pl
from jax.experimental.pallas import tpu as pltpu

F32 = jnp.float32
BF16 = jnp.bfloat16

D_MODEL = 1024
DEPTH = 2
GRID_W = 64
HY_DIM = 1024
HY_ORDER = 2
HY_BANDS = 16
HY_DECAY_TARGET = 1e-2
HY_DECAY_PCT_SHORT = 0.3
HY_DECAY_PCT_LONG = 1.5
HG_HEADS = 8
HG_HK = 128
RT_HEADS = 4
RT_HK = 256
RT_HV = 512
RT_ROPE_BASE = 10000.0
N_STATE_COLS = 6144
N_IN_COLS = 17408
N_EXPERTS = 8
TOP_K = 2
DN_ALPHA = (2 * DEPTH) ** 0.25
LN_EPS = 1e-5

COL_FF, COL_FB, COL_HI, COL_RK, COL_RV = 0, 8, 16, 24, 32
COL_HQ, COL_HG, COL_RQ, COL_RG, COL_HY, COL_BR = 48, 56, 64, 72, 88, 112

GLA_CHUNK = 128
GLA_LEVELS = 7
GLA_PAD = 8
GLA_MIN_LOG2 = -1e30
RET_CHUNK = 256
DFT_HALF = 256
HY_BLOCK = 512
CONV_ROWS = 256
MOE_ROWS = 512
VMEM_LIMIT = 56 * 1024 * 1024


def _cparams(sem, flags=None):
    return pltpu.CompilerParams(dimension_semantics=sem, vmem_limit_bytes=VMEM_LIMIT, flags=flags)


def _silu(x):
    return x * jax.nn.sigmoid(x)


def _layer_norm_rows(r, g, b):
    mu = jnp.mean(r, axis=-1, keepdims=True)
    d = r - mu
    var = jnp.mean(d * d, axis=-1, keepdims=True)
    return d * lax.rsqrt(var + LN_EPS) * g + b


def _split3(x):
    h = x.astype(BF16)
    r = x - h.astype(F32)
    m = r.astype(BF16)
    l = (r - m.astype(F32)).astype(BF16)
    return h, m, l


def _dot(a, b):
    return jnp.dot(a, b, preferred_element_type=F32)


def _dot_nt(a, b):
    return lax.dot_general(a, b, (((1,), (1,)), ((), ())), preferred_element_type=F32)


def _dot_tn(a, b):
    return lax.dot_general(a, b, (((0,), (0,)), ((), ())), preferred_element_type=F32)


def _ada_kernel(c_ref, w_ref, b_ref, o_ref):
    s = _silu(c_ref[...])
    o_ref[...] = _dot(s.astype(BF16), w_ref[...].astype(BF16)) + b_ref[...]


def _ada(cc, w, layer, b):
    n = w.shape[2]
    tn = 1024
    return pl.pallas_call(
        _ada_kernel,
        grid=(n // tn,),
        in_specs=[pl.BlockSpec((16, D_MODEL), lambda j: (0, 0)),
                  pl.BlockSpec((None, D_MODEL, tn), lambda j: (layer, 0, j)),
                  pl.BlockSpec((1, tn), lambda j: (0, j))],
        out_specs=pl.BlockSpec((16, tn), lambda j: (0, j)),
        out_shape=jax.ShapeDtypeStruct((16, n), F32),
        compiler_params=_cparams(("arbitrary",)),
        name="ada_mod",
    )(cc, w, b.reshape(1, n))


def _mod_index(rows_per_batch, tm):
    if rows_per_batch is None:
        return lambda i: 8
    return lambda i: (i * tm) // rows_per_batch


def _proj_kernel(x_ref, mod_ref, w_ref, o_ref, h_ref):
    @pl.when(pl.program_id(1) == 0)
    def _():
        m = mod_ref[0]
        h_ref[...] = (x_ref[...] * (1.0 + m[1:2]) + m[0:1]).astype(BF16)

    o_ref[...] = _dot(h_ref[...], w_ref[...].astype(BF16)).astype(o_ref.dtype)


def _proj(x, modp, w, layer, n_cols, rows_per_batch):
    m = x.shape[0]
    tm = min(2048, rows_per_batch or m)
    tn = 1024
    mi = _mod_index(rows_per_batch, tm)
    return pl.pallas_call(
        _proj_kernel,
        grid=(m // tm, n_cols // tn),
        in_specs=[pl.BlockSpec((tm, D_MODEL), lambda i, j: (i, 0)),
                  pl.BlockSpec((1, 8, D_MODEL), lambda i, j: (mi(i), 0, 0)),
                  pl.BlockSpec((None, D_MODEL, tn), lambda i, j: (layer, 0, j))],
        out_specs=pl.BlockSpec((tm, tn), lambda i, j: (i, j)),
        out_shape=jax.ShapeDtypeStruct((m, n_cols), BF16),
        scratch_shapes=[pltpu.VMEM((tm, D_MODEL), BF16)],
        compiler_params=_cparams(("arbitrary", "arbitrary")),
        name="in_proj",
    )(x, modp, w)


def _gla_gates(fl, gl_ref, d):
    lb = gl_ref[d, 0:1, :]
    oml = gl_ref[d, 1:2, :]
    t = jnp.exp(-jnp.abs(fl))
    r = 1.0 / (1.0 + t)
    tr = t * r
    pos = fl >= 0.0
    f = lb + oml * jnp.where(pos, r, tr)
    key = oml * jnp.where(pos, tr, r)
    return jnp.maximum(jnp.log2(f), GLA_MIN_LOG2), key


def _gla_cumsum(g, rev):
    c = g.shape[0]
    r = lax.broadcasted_iota(jnp.int32, (c, c), 0)
    u = lax.broadcasted_iota(jnp.int32, (c, c), 1)
    tri = jnp.where((u >= r) if rev else (u <= r), 1.0, 0.0).astype(BF16)
    return _dot(jnp.concatenate([tri, tri, tri], axis=1), jnp.concatenate(_split3(g), axis=0))


def _gla_state_update(st, b, kk, vv, rev):
    c = b.shape[0]
    end = 0 if rev else c - 1
    b_end = b[end:end + 1, :]
    khat = (kk * jnp.exp2(b_end - b)).astype(BF16)
    return st * jnp.exp2(b_end) + _dot_tn(vv, khat)


def _gla_level_table():
    c = GLA_CHUNK
    t = np.arange(c)[:, None]
    s = np.arange(c)[None, :]
    top_bit = np.floor(np.log2(np.maximum(t ^ s, 1))).astype(np.int32)
    fwd = np.where(t > s, top_bit, np.where(t == s, GLA_LEVELS, -1))
    return np.stack([fwd, fwd.T]).astype(np.int32)


def _gla_ref_rows(b, bpad_ref, level, rev):
    c = b.shape[0]
    half = 1 << level
    blk = 2 * half
    idx = half if rev else half - 1
    if blk % 8 == 0:
        r = b.reshape(c // blk, blk, HG_HK)[:, idx:idx + 1, :]
        return jnp.broadcast_to(r, (c // blk, blk, HG_HK)).reshape(c, HG_HK)
    m = lax.broadcasted_iota(jnp.int32, (c, HG_HK), 0) % blk
    out = b
    for v in range(blk):
        if v != idx:
            off = GLA_PAD + idx - v
            out = jnp.where(m == v, bpad_ref[off:off + c, :], out)
    return out


def _gla_chunk_out(b, kk, vv, qq, st, rev, lvt, bpad_ref):
    c = GLA_CHUNK
    o = _dot_nt((qq * jnp.exp2(b)).astype(BF16), st.astype(BF16))
    bpad_ref[GLA_PAD:GLA_PAD + c, :] = b
    sc = jnp.zeros((c, c), F32)
    for level in range(GLA_LEVELS):
        half = 1 << level
        blk = 2 * half
        ref = _gla_ref_rows(b, bpad_ref, level, rev)
        kt = (kk * jnp.exp2(ref - b)).astype(BF16)
        if half % 8:
            p = _dot_nt((qq * jnp.exp2(b - ref)).astype(BF16), kt)
            sc = jnp.where(lvt == level, p, sc)
        else:
            nblk = c // blk
            lo, hi = (0, half) if rev else (half, blk)

            def part(x):
                return x.reshape(nblk, blk, x.shape[-1])[:, lo:hi, :]

            qt = (part(qq) * jnp.exp2(part(b) - part(ref))).reshape(c // 2, HG_HK)
            p = _dot_nt(qt.astype(BF16), kt).reshape(nblk, half, c)
            new = jnp.where(part(lvt) == level, p, part(sc))
            rest = sc.reshape(nblk, blk, c)[:, half:blk, :] if rev else sc.reshape(nblk, blk, c)[:, 0:half, :]
            sc = jnp.concatenate([new, rest] if rev else [rest, new], axis=1).reshape(c, c)
    sc = jnp.where(lvt == GLA_LEVELS, _dot_nt(qq.astype(BF16), kk.astype(BF16)), sc)
    return o + _dot(sc.astype(BF16), vv)


def _gla_kernel(*refs, need_ctx_out, n_lat, n_ctx):
    if need_ctx_out:
        ffl, fbl, il, ql, ffc, fbc, ic, qc, gl_ref, lv_ref, ol_ref, oc_ref = refs[:12]
        bl, kl, bc, kc, qsl, qsc, bpf, bpr = refs[12:]
    else:
        ffl, fbl, il, ql, ffc, fbc, ic, gl_ref, lv_ref, ol_ref = refs[:10]
        bl, kl, bc, kc, qsl, bpf, bpr = refs[10:]
        qc = oc_ref = qsc = None
    c = GLA_CHUNK
    q_scale = HG_HK ** -0.5
    bpf[...] = jnp.zeros(bpf.shape, F32)
    bpr[...] = jnp.zeros(bpr.shape, F32)
    ol_ref[...] = jnp.zeros(ol_ref.shape, F32)
    if need_ctx_out:
        oc_ref[...] = jnp.zeros(oc_ref.shape, F32)

    def chunk_rows(idx):
        return pl.ds(pl.multiple_of(idx * c, c), c)

    def prepare(f_refs, q_ref, b_sc, k_sc, q_sc, n):
        def body(j, carry):
            rows = chunk_rows(j)
            for d in range(2):
                g, kk = _gla_gates(f_refs[d][rows, :].astype(F32), gl_ref, d)
                b_sc[d, rows, :] = _gla_cumsum(g, d == 1)
                k_sc[d, rows, :] = kk
            if q_ref is not None:
                q_sc[rows, :] = _silu(q_ref[rows, :].astype(F32)) * q_scale
            return carry
        lax.fori_loop(0, n, body, 0, unroll=min(2, n))

    prepare((ffc, fbc), qc, bc, kc, qsc, n_ctx)
    prepare((ffl, fbl), ql, bl, kl, qsl, n_lat)

    def one_dir(d, b_sc, k_sc, i_ref, q_sc, o_ref, idx, st):
        rev = d == 1
        rows = chunk_rows(idx)
        b, kk, vv = b_sc[d, rows, :], k_sc[d, rows, :], i_ref[rows, :]
        if q_sc is not None:
            o_ref[rows, :] += _gla_chunk_out(b, kk, vv, q_sc[rows, :], st, rev, lv_ref[d],
                                             bpr if rev else bpf)
        return _gla_state_update(st, b, kk, vv, rev)

    def ctx_body(j, carry):
        return (one_dir(0, bc, kc, ic, qsc, oc_ref, j, carry[0]),
                one_dir(1, bc, kc, ic, qsc, oc_ref, n_ctx - 1 - j, carry[1]))

    def lat_body(j, carry):
        return (one_dir(0, bl, kl, il, qsl, ol_ref, j, carry[0]),
                one_dir(1, bl, kl, il, qsl, ol_ref, n_lat - 1 - j, carry[1]))

    zero = jnp.zeros((HG_HK, HG_HK), F32)
    carry = lax.fori_loop(0, n_ctx, ctx_body, (zero, zero), unroll=min(2, n_ctx))
    lax.fori_loop(0, n_lat, lat_body, carry, unroll=min(2, n_lat))


def _gla(p_lat, p_ctx, gl_tab, batch, need_ctx_out):
    l = p_lat.shape[0] // batch
    lc = p_ctx.shape[0] // batch
    w = HG_HK
    c = GLA_CHUNK

    def col(base):
        return lambda b, h: (b, base + h)

    lat_specs = [pl.BlockSpec((l, w), col(cb)) for cb in (COL_FF, COL_FB, COL_HI, COL_HQ)]
    ctx_cols = (COL_FF, COL_FB, COL_HI) + ((COL_HQ,) if need_ctx_out else ())
    ctx_specs = [pl.BlockSpec((lc, w), col(cb)) for cb in ctx_cols]
    in_specs = lat_specs + ctx_specs + [pl.BlockSpec((2, 8, w), lambda b, h: (0, 0, h)),
                                        pl.BlockSpec((2, c, c), lambda b, h: (0, 0, 0))]
    args = [p_lat] * 4 + [p_ctx] * len(ctx_cols) + [gl_tab, jnp.asarray(_gla_level_table())]
    out_specs = [pl.BlockSpec((l, w), lambda b, h: (b, h))]
    out_shape = [jax.ShapeDtypeStruct((batch * l, HG_HEADS * w), F32)]
    scratch = [pltpu.VMEM((2, l, w), F32), pltpu.VMEM((2, l, w), F32),
               pltpu.VMEM((2, lc, w), F32), pltpu.VMEM((2, lc, w), F32), pltpu.VMEM((l, w), F32)]
    if need_ctx_out:
        out_specs.append(pl.BlockSpec((lc, w), lambda b, h: (b, h)))
        out_shape.append(jax.ShapeDtypeStruct((batch * lc, HG_HEADS * w), F32))
        scratch.append(pltpu.VMEM((lc, w), F32))
    scratch += [pltpu.VMEM((c + 2 * GLA_PAD, w), F32)] * 2
    outs = pl.pallas_call(
        functools.partial(_gla_kernel, need_ctx_out=need_ctx_out, n_lat=l // c, n_ctx=lc // c),
        grid=(batch, HG_HEADS),
        in_specs=in_specs,
        out_specs=out_specs,
        out_shape=out_shape,
        scratch_shapes=scratch,
        compiler_params=_cparams(("arbitrary", "arbitrary")),
        name="hgrn2_scan",
    )(*args)
    return outs[0], (outs[1] if need_ctx_out else None)


def _rotate(x, cos, sin):
    half = x.shape[-1] // 2
    x1, x2 = x[:, :half], x[:, half:]
    return jnp.concatenate([x1 * cos - x2 * sin, x1 * sin + x2 * cos], axis=-1)


def _ret_kernel(*refs, need_ctx_out, n_lat, n_ctx):
    if need_ctx_out:
        (ql, kl, vl, gl, qc, kc, vc, gc, cos_ref, sin_ref, intra_ref, qk_ref, sd_ref,
         yl_ref, yc_ref, qrl, krl, qrc, krc, ol, oc, s_fwd, s_rev) = refs
    else:
        (ql, kl, vl, gl, kc, vc, cos_ref, sin_ref, intra_ref, qk_ref, sd_ref,
         yl_ref, qrl, krl, krc, ol, s_fwd, s_rev) = refs
        qc = gc = yc_ref = qrc = oc = None
    s_refs = (s_fwd, s_rev)
    c = RET_CHUNK
    lc = n_ctx * c
    k_scale = RT_HK ** -0.5

    def rot_all(src, dst, n, pos0, scale):
        def body(j, carry):
            rows = pl.ds(pl.multiple_of(j * c, c), c)
            prow = pl.ds(pl.multiple_of(pos0 + j * c, c), c)
            x = src[rows, :].astype(F32)
            dst[rows, :] = (_rotate(x, cos_ref[prow, :], sin_ref[prow, :]) * scale).astype(BF16)
            return carry
        lax.fori_loop(0, n, body, 0)

    rot_all(kl, krl, n_lat, lc, k_scale)
    rot_all(kc, krc, n_ctx, 0, k_scale)
    rot_all(ql, qrl, n_lat, lc, 1.0)
    if need_ctx_out:
        rot_all(qc, qrc, n_ctx, 0, 1.0)

    for s_ref in s_refs:
        s_ref[...] = jnp.zeros(s_ref.shape, F32)
    ol[...] = jnp.zeros(ol.shape, F32)
    if need_ctx_out:
        oc[...] = jnp.zeros(oc.shape, F32)

    def update(d, kr, v_ref, rows):
        kd = qk_ref[d, 0, c:2 * c, :]
        sdec = sd_ref[d, 0, 0:1, :]
        kh = (kr[rows, :].astype(F32) * jnp.concatenate([kd, kd], axis=1)).astype(BF16)
        s_ref = s_refs[d]
        s_ref[...] = (s_ref[...] * jnp.concatenate([sdec] * (RT_HV // 128), axis=1)
                      + _dot_tn(kh, v_ref[rows, :]))

    def out_step(d, qr, kr, v_ref, o_ref, rows):
        qd = qk_ref[d, 0, 0:c, :]
        qcb = qr[rows, :]
        sc = _dot_nt(qcb, kr[rows, :]) * intra_ref[d, 0]
        qh = (qcb.astype(F32) * jnp.concatenate([qd, qd], axis=1)).astype(BF16)
        o_ref[rows, :] += (_dot(qh, s_refs[d][...].astype(BF16))
                           + _dot(sc.astype(BF16), v_ref[rows, :]))
        update(d, kr, v_ref, rows)

    def ctx_body(j, carry):
        for d in range(2):
            idx = (n_ctx - 1 - j) if d == 1 else j
            rows = pl.ds(pl.multiple_of(idx * c, c), c)
            if need_ctx_out:
                out_step(d, qrc, krc, vc, oc, rows)
            else:
                update(d, krc, vc, rows)
        return carry

    def lat_body(j, carry):
        for d in range(2):
            idx = (n_lat - 1 - j) if d == 1 else j
            out_step(d, qrl, krl, vl, ol, pl.ds(pl.multiple_of(idx * c, c), c))
        return carry

    lax.fori_loop(0, n_ctx, ctx_body, 0)
    lax.fori_loop(0, n_lat, lat_body, 0, unroll=2)

    def readout(o_ref, g_ref, y_ref, n):
        def body(j, carry):
            rows = pl.ds(pl.multiple_of(j * c, c), c)
            o = o_ref[rows, :]
            y = o * lax.rsqrt(jnp.mean(o * o, axis=-1, keepdims=True) + LN_EPS)
            y_ref[rows, :] = (y * _silu(g_ref[rows, :].astype(F32))).astype(y_ref.dtype)
            return carry
        lax.fori_loop(0, n, body, 0)

    readout(ol, gl, yl_ref, n_lat)
    if need_ctx_out:
        readout(oc, gc, yc_ref, n_ctx)


def _retention_tables(lc, l):
    half = RT_HK // 2
    inv = 1.0 / (RT_ROPE_BASE ** jnp.linspace(0.0, 1.0, half, dtype=F32))
    ang = jnp.arange(lc + l, dtype=F32)[:, None] * inv[None, :]
    j = jnp.arange(2 * RT_HEADS, dtype=F32)
    lg_all = jnp.log1p(-jnp.exp2(-5.0 - j))
    c = RET_CHUNK
    pos = jnp.arange(c, dtype=F32)
    rel = pos[:, None] - pos[None, :]
    intra, qk, sd = [], [], []
    for d in range(2):
        lg = lg_all[d::2]
        m = jnp.where(rel >= 0, jnp.exp(jnp.maximum(rel, 0.0)[None] * lg[:, None, None]), 0.0)
        qdec = jnp.exp((pos + 1.0)[None, :] * lg[:, None])
        kdec = jnp.exp((c - 1.0 - pos)[None, :] * lg[:, None])
        if d == 1:
            m = jnp.swapaxes(m, 1, 2)
            qdec = qdec[:, ::-1]
            kdec = kdec[:, ::-1]
        intra.append(m)
        qk.append(jnp.broadcast_to(jnp.concatenate([qdec, kdec], axis=1)[:, :, None],
                                   (RT_HEADS, 2 * c, 128)))
        sd.append(jnp.broadcast_to(jnp.exp(c * lg)[:, None, None], (RT_HEADS, 8, 128)))
    return (jnp.cos(ang), jnp.sin(ang), jnp.stack(intra), jnp.stack(qk), jnp.stack(sd))


def _retention(p_lat, p_ctx, tables, batch, need_ctx_out):
    l = p_lat.shape[0] // batch
    lc = p_ctx.shape[0] // batch
    cos, sin, intra, qk, sd = tables
    c = RET_CHUNK

    def col(base):
        return lambda b, h: (b, base + h)

    kq, kk_, kv, kg = COL_RQ * 128 // RT_HK, COL_RK * 128 // RT_HK, COL_RV * 128 // RT_HV, COL_RG * 128 // RT_HV
    lat_specs = [pl.BlockSpec((l, RT_HK), col(kq)), pl.BlockSpec((l, RT_HK), col(kk_)),
                 pl.BlockSpec((l, RT_HV), col(kv)), pl.BlockSpec((l, RT_HV), col(kg))]
    if need_ctx_out:
        ctx_specs = [pl.BlockSpec((lc, RT_HK), col(kq)), pl.BlockSpec((lc, RT_HK), col(kk_)),
                     pl.BlockSpec((lc, RT_HV), col(kv)), pl.BlockSpec((lc, RT_HV), col(kg))]
    else:
        ctx_specs = [pl.BlockSpec((lc, RT_HK), col(kk_)), pl.BlockSpec((lc, RT_HV), col(kv))]
    tab_specs = [pl.BlockSpec((lc + l, 128), lambda b, h: (0, 0)),
                 pl.BlockSpec((lc + l, 128), lambda b, h: (0, 0)),
                 pl.BlockSpec((2, 1, c, c), lambda b, h: (0, h, 0, 0)),
                 pl.BlockSpec((2, 1, 2 * c, 128), lambda b, h: (0, h, 0, 0)),
                 pl.BlockSpec((2, 1, 8, 128), lambda b, h: (0, h, 0, 0))]
    args = [p_lat] * 4 + [p_ctx] * len(ctx_specs) + [cos, sin, intra, qk, sd]
    out_specs = [pl.BlockSpec((l, RT_HV), lambda b, h: (b, h))]
    out_shape = [jax.ShapeDtypeStruct((batch * l, RT_HEADS * RT_HV), BF16)]
    scratch = [pltpu.VMEM((l, RT_HK), BF16), pltpu.VMEM((l, RT_HK), BF16)]
    if need_ctx_out:
        out_specs.append(pl.BlockSpec((lc, RT_HV), lambda b, h: (b, h)))
        out_shape.append(jax.ShapeDtypeStruct((batch * lc, RT_HEADS * RT_HV), BF16))
        scratch += [pltpu.VMEM((lc, RT_HK), BF16), pltpu.VMEM((lc, RT_HK), BF16),
                    pltpu.VMEM((l, RT_HV), F32), pltpu.VMEM((lc, RT_HV), F32)]
    else:
        scratch += [pltpu.VMEM((lc, RT_HK), BF16), pltpu.VMEM((l, RT_HV), F32)]
    scratch += [pltpu.VMEM((RT_HK, RT_HV), F32)] * 2
    outs = pl.pallas_call(
        functools.partial(_ret_kernel, need_ctx_out=need_ctx_out, n_lat=l // c, n_ctx=lc // c),
        grid=(batch, RT_HEADS),
        in_specs=lat_specs + ctx_specs + tab_specs,
        out_specs=out_specs,
        out_shape=out_shape,
        scratch_shapes=scratch,
        compiler_params=_cparams(("arbitrary", "arbitrary")),
        name="retention_scan",
    )(*args)
    return outs[0], (outs[1] if need_ctx_out else None)


def _conv_shift_matrices(grid):
    rb = CONV_ROWS
    t = np.arange(rb)
    edge_lo = (t % GRID_W == 0) if grid else (t == 0)
    edge_hi = (t % GRID_W == GRID_W - 1) if grid else (t == rb - 1)
    s0 = ((t[:, None] - 1 == t[None, :]) & ~edge_lo[:, None]).astype(np.float32)
    s2 = ((t[:, None] + 1 == t[None, :]) & ~edge_hi[:, None]).astype(np.float32)
    return s0, s2


def _short_conv_kernel(x_ref, w_ref, b_ref, s0_ref, s2_ref, o_ref, xpad_ref, *, grid):
    l, tc = x_ref.shape
    p = GRID_W
    rb = CONV_ROWS
    zeros = jnp.zeros((p, tc), BF16)
    xpad_ref[0:p, :] = zeros
    xpad_ref[p + l:p + l + p, :] = zeros
    xpad_ref[p:p + l, :] = x_ref[...]
    w = [w_ref[k:k + 1, :].astype(BF16) for k in range(9)]
    rows = range(3) if grid else (1,)

    def body(i, carry):
        r0 = i * rb
        part = []
        for dj in range(3):
            acc = None
            for di in rows:
                start = pl.multiple_of(r0 + p + (di - 1) * GRID_W, GRID_W)
                t = xpad_ref[pl.ds(start, rb), :] * w[di * 3 + dj]
                acc = t if acc is None else acc + t
            part.append(acc)
        out = (part[1].astype(F32) + _dot(s0_ref[...], part[0]) + _dot(s2_ref[...], part[2])
               + b_ref[...])
        o_ref[pl.ds(pl.multiple_of(r0, rb), rb), :] = out.astype(o_ref.dtype)
        return carry

    lax.fori_loop(0, l // rb, body, 0, unroll=min(4, l // rb))


def _short_conv(p_arr, w9, bias, batch, grid):
    l = p_arr.shape[0] // batch
    assert l % CONV_ROWS == 0 and (grid or l == CONV_ROWS)
    c3 = 3 * HY_DIM
    tc = 256
    base = COL_HY * 128 // tc
    s0, s2 = _conv_shift_matrices(grid)
    return pl.pallas_call(
        functools.partial(_short_conv_kernel, grid=grid),
        grid=(batch, c3 // tc),
        in_specs=[pl.BlockSpec((l, tc), lambda b, j: (b, base + j)),
                  pl.BlockSpec((9, tc), lambda b, j: (0, j)),
                  pl.BlockSpec((1, tc), lambda b, j: (0, j)),
                  pl.BlockSpec((CONV_ROWS, CONV_ROWS), lambda b, j: (0, 0)),
                  pl.BlockSpec((CONV_ROWS, CONV_ROWS), lambda b, j: (0, 0))],
        out_specs=pl.BlockSpec((l, tc), lambda b, j: (b, j)),
        out_shape=jax.ShapeDtypeStruct((batch * l, c3), BF16),
        scratch_shapes=[pltpu.VMEM((l + 2 * GRID_W, tc), BF16)],
        compiler_params=_cparams(("arbitrary", "arbitrary")),
        name="hyena_short_conv",
    )(p_arr, w9, bias.reshape(1, c3), jnp.asarray(s0, BF16), jnp.asarray(s2, BF16))


def _dft_matrices(l):
    n = 2 * l
    s = int(round(math.sqrt(n)))
    while n % s:
        s -= 1
    q = n // s
    f = np.arange(l, dtype=np.int64)
    ang_a = 2.0 * np.pi * ((f[:, None] * s * np.arange(q)[None, :]) % n) / n
    ang_b = 2.0 * np.pi * ((f[:, None] * np.arange(s)[None, :]) % n) / n
    t = np.arange(n)
    rep = (t[None, :] // s == np.arange(q)[:, None]).astype(np.float32)
    til = (t[None, :] % s == np.arange(s)[:, None]).astype(np.float32)
    h = DFT_HALF
    row = lambda j: (j, 0)
    const = lambda j: (0, 0)
    fwd, inv = pl.pallas_call(
        _dft_build_kernel,
        grid=(l // h,),
        in_specs=[pl.BlockSpec((h, q), row), pl.BlockSpec((h, q), row),
                  pl.BlockSpec((h, s), row), pl.BlockSpec((h, s), row),
                  pl.BlockSpec((q, n), const), pl.BlockSpec((s, n), const)],
        out_specs=[pl.BlockSpec((1, 2, h, n), lambda j: (j, 0, 0, 0)),
                   pl.BlockSpec((h, n), row)],
        out_shape=[jax.ShapeDtypeStruct((l // h, 2, h, n), BF16),
                   jax.ShapeDtypeStruct((l, n), BF16)],
        compiler_params=_cparams(("arbitrary",)),
        name="dft_build",
    )(jnp.asarray(np.cos(ang_a), F32), jnp.asarray(np.sin(ang_a), F32),
      jnp.asarray(np.cos(ang_b), F32), jnp.asarray(np.sin(ang_b), F32),
      jnp.asarray(rep, BF16), jnp.asarray(til, BF16))
    return fwd.reshape(n, n), inv


def _dft_build_kernel(ca_ref, sa_ref, cb_ref, sb_ref, rep_ref, til_ref, fwd_ref, inv_ref):
    h, n = inv_ref.shape

    def spread(x_ref, m_ref):
        p1, p2, p3 = _split3(x_ref[...])
        m = m_ref[...]
        return _dot(p1, m) + _dot(p2, m) + _dot(p3, m)

    ca, sa = spread(ca_ref, rep_ref), spread(sa_ref, rep_ref)
    cb, sb = spread(cb_ref, til_ref), spread(sb_ref, til_ref)
    cosm = ca * cb - sa * sb
    nsin = -(sa * cb + ca * sb)
    first = pl.program_id(0) == 0
    rowi = lax.broadcasted_iota(jnp.int32, (h, n), 0)
    coli = lax.broadcasted_iota(jnp.int32, (h, n), 1)
    alt_col = (1 - 2 * (coli % 2)).astype(F32)
    fwd_ref[0, 0] = cosm.astype(BF16)
    fwd_ref[0, 1] = jnp.where(first & (rowi == 0), alt_col, nsin).astype(BF16)
    rowg = lax.broadcasted_iota(jnp.int32, (h, h), 0) + pl.program_id(0) * h
    col0 = lax.broadcasted_iota(jnp.int32, (h, h), 1) == 0
    alt_row = (1 - 2 * (rowg % 2)).astype(F32)
    pieces = []
    for jj in range(n // (2 * h)):
        cp = cosm[:, jj * h:(jj + 1) * h] * (2.0 / n)
        ip = nsin[:, jj * h:(jj + 1) * h] * (2.0 / n)
        if jj == 0:
            cp = jnp.where(col0, 1.0 / n, cp)
            ip = jnp.where(col0, alt_row * (1.0 / n), ip)
        pieces += [cp.astype(BF16), ip.astype(BF16)]
    inv_ref[...] = jnp.concatenate(pieces, axis=1)


def _filter_spectrum_kernel(f_ref, top_ref, bot_ref, o_ref):
    h = DFT_HALF
    bl = top_ref.shape[0]
    acc = _dot(f_ref[:, 0:bl], top_ref[...]) + _dot(f_ref[:, bl:2 * bl], bot_ref[...])
    kr, km = acc[:h], acc[h:]
    row0 = (lax.broadcasted_iota(jnp.int32, kr.shape, 0) == 0) & (pl.program_id(2) == 0)
    o_ref[0, 0:h, :] = kr.astype(o_ref.dtype)
    o_ref[0, h:2 * h, :] = jnp.where(row0, 0.0, km).astype(o_ref.dtype)
    o_ref[0, 2 * h:3 * h, :] = jnp.where(row0, km, kr).astype(o_ref.dtype)


def _filter_spectrum(fwd, kern_bf, nb):
    n = fwd.shape[0]
    bl = n // 2
    nc = kern_bf.shape[1]
    tm, tn = 2 * DFT_HALF, 1024
    nd = 2 * nb - 1
    wrap = 2 * nb
    return pl.pallas_call(
        _filter_spectrum_kernel,
        grid=(nc // tn, nd, n // tm),
        in_specs=[pl.BlockSpec((tm, n), lambda c, d, t: (t, 0)),
                  pl.BlockSpec((bl, tn), lambda c, d, t: ((d - (nb - 1)) % wrap, c)),
                  pl.BlockSpec((bl, tn), lambda c, d, t: ((d - nb) % wrap, c))],
        out_specs=pl.BlockSpec((1, 3 * DFT_HALF, tn), lambda c, d, t: (d, t, c)),
        out_shape=jax.ShapeDtypeStruct((nd, (n // tm) * 3 * DFT_HALF, nc), BF16),
        compiler_params=_cparams(("arbitrary", "arbitrary", "arbitrary")),
        name="hyena_filter_dft",
    )(fwd, kern_bf, kern_bf)


def _dft_fwd_kernel(f_ref, z_ref, k_ref, y_ref):
    h = DFT_HALF
    bl = f_ref.shape[1]
    nb = y_ref.shape[1]
    f = f_ref[...]
    spec = [_dot(f, z_ref[j * bl:(j + 1) * bl, :]).astype(BF16) for j in range(nb)]
    for i in range(nb):
        yr = yi = None
        for j in range(nb):
            d = i - j + nb - 1
            kr, ki, kr2 = k_ref[d, 0:h, :], k_ref[d, h:2 * h, :], k_ref[d, 2 * h:3 * h, :]
            ur, ui = spec[j][:h], spec[j][h:]
            tr = ur * kr - ui * ki
            ti = ur * ki + ui * kr2
            yr = tr if yr is None else yr + tr
            yi = ti if yi is None else yi + ti
        y_ref[0, i, 0:h, :] = yr.astype(y_ref.dtype)
        y_ref[0, i, h:2 * h, :] = yi.astype(y_ref.dtype)


def _dft_fwd(fwd, z_arr, z_col, kf, k_col, batch, nb):
    n = fwd.shape[0]
    bl = n // 2
    l = nb * bl
    tm, tn = 2 * DFT_HALF, 512
    cpt = HY_DIM // tn
    nd = kf.shape[0]
    return pl.pallas_call(
        _dft_fwd_kernel,
        grid=(cpt, n // tm, batch),
        in_specs=[pl.BlockSpec((tm, bl), lambda c, t, b: (t, 0)),
                  pl.BlockSpec((l, tn), lambda c, t, b: (b, z_col * cpt + c)),
                  pl.BlockSpec((nd, 3 * DFT_HALF, tn), lambda c, t, b: (0, t, k_col * cpt + c))],
        out_specs=pl.BlockSpec((1, nb, tm, tn), lambda c, t, b: (b, 0, t, c)),
        out_shape=jax.ShapeDtypeStruct((batch, nb, n, HY_DIM), BF16),
        compiler_params=_cparams(("arbitrary", "arbitrary", "arbitrary")),
        name="hyena_dft_fwd",
    )(fwd, z_arr, kf)


def _dft_inv_kernel(g_ref, y_ref, gate_ref, z_ref, skip_ref, o_ref):
    conv = _dot(g_ref[...], y_ref[0, 0])
    z = z_ref[...].astype(F32)
    o_ref[...] = (gate_ref[...].astype(F32) * (conv + skip_ref[...] * z)).astype(o_ref.dtype)


def _dft_inv(inv, y, gate_arr, gate_col, z_arr, z_col, skip, batch, nb):
    bl, n = inv.shape
    tn = HY_DIM
    return pl.pallas_call(
        _dft_inv_kernel,
        grid=(batch, nb),
        in_specs=[pl.BlockSpec((bl, n), lambda b, i: (0, 0)),
                  pl.BlockSpec((1, 1, n, tn), lambda b, i: (b, i, 0, 0)),
                  pl.BlockSpec((bl, tn), lambda b, i: (b * nb + i, gate_col)),
                  pl.BlockSpec((bl, tn), lambda b, i: (b * nb + i, z_col)),
                  pl.BlockSpec((1, tn), lambda b, i: (0, 0))],
        out_specs=pl.BlockSpec((bl, tn), lambda b, i: (b * nb + i, 0)),
        out_shape=jax.ShapeDtypeStruct((batch * nb * bl, tn), BF16),
        compiler_params=_cparams(("arbitrary", "arbitrary")),
        name="hyena_dft_inv",
    )(inv, y, gate_arr, z_arr, skip.reshape(1, tn))


def _tap_features(l):
    f32 = np.float32
    t01 = np.linspace(0.0, 1.0, l, dtype=f32)[:, None]
    ang = f32(2.0 * math.pi) * np.arange(l, dtype=f32)[:, None] / f32(l)
    bands = np.linspace(1e-4, HY_BANDS - 1, HY_BANDS, dtype=f32)[None, :]
    z = np.concatenate([t01, np.cos(bands * ang), -np.sin(bands * ang)], axis=-1).astype(f32)
    pos = np.concatenate([np.arange(l), [0], np.arange(l - 1, 0, -1)])
    zz = np.zeros((2 * l, 128), f32)
    zz[:, :z.shape[1]] = z[pos]
    zz[l] = 0.0
    return zz


def _dot_hi(a, b):
    a1 = a.astype(BF16)
    a2 = (a - a1.astype(F32)).astype(BF16)
    b1 = b.astype(BF16)
    b2 = (b - b1.astype(F32)).astype(BF16)
    return _dot(a1, b1) + _dot(a1, b2) + _dot(a2, b1)


def _filter_kernel(zz_ref, w1_ref, b1_ref, fr_ref, w2_ref, b2_ref, w3f_ref, w3b_ref, dl_ref,
                   o_ref, hdn_ref):
    n = zz_ref.shape[0]
    l = n // 2

    @pl.when((pl.program_id(0) == 0) & (pl.program_id(1) == 0))
    def _():
        h1 = jnp.sin(fr_ref[0:1, :] * (_dot_hi(zz_ref[...], w1_ref[...]) + b1_ref[...]))
        hdn_ref[...] = jnp.sin(fr_ref[1:2, :] * (_dot_hi(h1, w2_ref[...]) + b2_ref[...]))

    filt = jnp.concatenate([_dot_hi(hdn_ref[0:l, :], w3f_ref[...]),
                            _dot_hi(hdn_ref[l:n, :], w3b_ref[...])], axis=0)
    decay = jnp.exp(-zz_ref[:, 0:1] * dl_ref[...])
    rowi = lax.broadcasted_iota(jnp.int32, filt.shape, 0)
    kern = jnp.where(rowi == l, 0.0, filt * decay)
    o_ref[...] = (kern / jnp.sum(jnp.abs(kern), axis=0, keepdims=True)).astype(o_ref.dtype)


def _hyena_filters(l, w1, b1, freq, w2, b2, w3):
    n = 2 * l
    tn = 256
    ff = w2.shape[0]
    cpo = HY_DIM // tn
    deltas = np.abs(np.linspace(math.log(HY_DECAY_TARGET) / HY_DECAY_PCT_LONG,
                                math.log(HY_DECAY_TARGET) / HY_DECAY_PCT_SHORT, HY_DIM,
                                dtype=np.float32)).reshape(1, HY_DIM)
    const = lambda o, c: (0, 0)
    return pl.pallas_call(
        _filter_kernel,
        grid=(HY_ORDER, cpo),
        in_specs=[pl.BlockSpec((n, 128), const), pl.BlockSpec((128, ff), const),
                  pl.BlockSpec((1, ff), const), pl.BlockSpec((2, ff), const),
                  pl.BlockSpec((ff, ff), const), pl.BlockSpec((1, ff), const),
                  pl.BlockSpec((ff, tn), lambda o, c: (0, o * 2 * cpo + c)),
                  pl.BlockSpec((ff, tn), lambda o, c: (0, o * 2 * cpo + cpo + c)),
                  pl.BlockSpec((1, tn), lambda o, c: (0, c))],
        out_specs=pl.BlockSpec((n, tn), lambda o, c: (0, o * cpo + c)),
        out_shape=jax.ShapeDtypeStruct((n, HY_ORDER * HY_DIM), BF16),
        scratch_shapes=[pltpu.VMEM((n, ff), F32)],
        compiler_params=_cparams(("arbitrary", "arbitrary")),
        name="hyena_filter_mlp",
    )(jnp.asarray(_tap_features(l)), jnp.pad(w1, ((0, 128 - w1.shape[0]), (0, 0))),
      b1.reshape(1, ff), freq, w2, b2.reshape(1, ff), w3, w3, jnp.asarray(deltas))


def _hyena(p_arr, lp, dft, batch, grid):
    l = p_arr.shape[0] // batch
    fwd, inv = dft
    uc = _short_conv(p_arr, lp['hy_conv_w'].reshape(9, 3 * HY_DIM), lp['hy_conv_b'], batch, grid)
    kern = _hyena_filters(l, lp['hy_ff_w1'], lp['hy_ff_b1'], lp['hy_ff_freq'],
                          lp['hy_ff_w2'], lp['hy_ff_b2'], lp['hy_ff_w3'])
    nb = l // inv.shape[0]
    kf = _filter_spectrum(fwd, kern, nb)
    z_arr, z_col = uc, 0
    for n in range(HY_ORDER):
        y = _dft_fwd(fwd, z_arr, z_col, kf, n, batch, nb)
        z_arr = _dft_inv(inv, y, uc, n + 1, z_arr, z_col, lp['hy_skip'][n], batch, nb)
        z_col = 0
    return z_arr


def _merge_kernel(yhy_ref, ohg_ref, hgg_ref, yrt_ref, b0_ref, b1_ref, b2_ref, x_ref, mod_ref,
                  nw_ref, phy_ref, phg_ref, prt_ref, wo_ref, lng_ref, lnb_ref, o_ref):
    o = ohg_ref[...]
    yhg = (o * lax.rsqrt(jnp.mean(o * o, axis=-1, keepdims=True) + LN_EPS) * nw_ref[...]
           * _silu(hgg_ref[...].astype(F32)))
    m = (jax.nn.sigmoid(b0_ref[...].astype(F32)) * _dot(yhy_ref[...], phy_ref[...])
         + jax.nn.sigmoid(b1_ref[...].astype(F32)) * _dot(yhg.astype(BF16), phg_ref[...])
         + jax.nn.sigmoid(b2_ref[...].astype(F32)) * _dot(yrt_ref[...], prt_ref[...]))
    t = _dot(m.astype(BF16), wo_ref[...])
    gt1 = mod_ref[0][2:3]
    o_ref[...] = _layer_norm_rows(DN_ALPHA * x_ref[...] + gt1 * t, lng_ref[...], lnb_ref[...])


def _merge(y_hy, o_hg, y_rt, p_arr, x, modp, lp, wbf, rows_per_batch):
    m = x.shape[0]
    tm = min(512, m)
    d = D_MODEL
    mi = _mod_index(rows_per_batch, tm)
    row = lambda i: (i, 0)
    const = lambda i: (0, 0)
    gcol = COL_HG * 128 // d
    bcol = COL_BR * 128 // d
    return pl.pallas_call(
        _merge_kernel,
        grid=(m // tm,),
        in_specs=[pl.BlockSpec((tm, d), row), pl.BlockSpec((tm, d), row),
                  pl.BlockSpec((tm, d), lambda i: (i, gcol)),
                  pl.BlockSpec((tm, 2 * d), row),
                  pl.BlockSpec((tm, d), lambda i: (i, bcol)),
                  pl.BlockSpec((tm, d), lambda i: (i, bcol + 1)),
                  pl.BlockSpec((tm, d), lambda i: (i, bcol + 2)),
                  pl.BlockSpec((tm, d), row),
                  pl.BlockSpec((1, 8, d), lambda i: (mi(i), 0, 0)),
                  pl.BlockSpec((1, d), const),
                  pl.BlockSpec((d, d), const), pl.BlockSpec((d, d), const),
                  pl.BlockSpec((2 * d, d), const), pl.BlockSpec((d, d), const),
                  pl.BlockSpec((1, d), const), pl.BlockSpec((1, d), const)],
        out_specs=pl.BlockSpec((tm, d), row),
        out_shape=jax.ShapeDtypeStruct((m, d), F32),
        compiler_params=_cparams(("arbitrary",)),
        name="merge_out_ln",
    )(y_hy, o_hg, p_arr, y_rt, p_arr, p_arr, p_arr, x, modp, lp['hg_norm_w'].reshape(1, d),
      wbf['p_hy'], wbf['p_hg'], wbf['p_rt'], wbf['w_o'],
      lp['ln1_g'].reshape(1, d), lp['ln1_b'].reshape(1, d))


def _ffn_kernel(x_ref, mod_ref, w1_ref, w3_ref, w2_ref, lng_ref, lnb_ref, o_ref, h_ref, acc_ref):
    k = pl.program_id(1)
    m = mod_ref[0]

    @pl.when(k == 0)
    def _():
        h_ref[...] = (x_ref[...] * (1.0 + m[4:5]) + m[3:4]).astype(BF16)
        acc_ref[...] = jnp.zeros(acc_ref.shape, F32)

    h = h_ref[...]
    u = _silu(_dot(h, w1_ref[...])) * _dot(h, w3_ref[...])
    acc_ref[...] += _dot(u.astype(BF16), w2_ref[...])

    @pl.when(k == pl.num_programs(1) - 1)
    def _():
        o_ref[...] = _layer_norm_rows(DN_ALPHA * x_ref[...] + m[5:6] * acc_ref[...],
                                      lng_ref[...], lnb_ref[...])


def _ffn_dense(x, modp, w1, w3, w2, ln_g, ln_b, rows_per_batch):
    m = x.shape[0]
    d = D_MODEL
    dff = w1.shape[1]
    tm = min(1024, rows_per_batch or m)
    tf = dff // 2
    mi = _mod_index(rows_per_batch, tm)
    return pl.pallas_call(
        _ffn_kernel,
        grid=(m // tm, dff // tf),
        in_specs=[pl.BlockSpec((tm, d), lambda i, k: (i, 0)),
                  pl.BlockSpec((1, 8, d), lambda i, k: (mi(i), 0, 0)),
                  pl.BlockSpec((d, tf), lambda i, k: (0, k)),
                  pl.BlockSpec((d, tf), lambda i, k: (0, k)),
                  pl.BlockSpec((tf, d), lambda i, k: (k, 0)),
                  pl.BlockSpec((1, d), lambda i, k: (0, 0)),
                  pl.BlockSpec((1, d), lambda i, k: (0, 0))],
        out_specs=pl.BlockSpec((tm, d), lambda i, k: (i, 0)),
        out_shape=jax.ShapeDtypeStruct((m, d), F32),
        scratch_shapes=[pltpu.VMEM((tm, d), BF16), pltpu.VMEM((tm, d), F32)],
        compiler_params=_cparams(("arbitrary", "arbitrary")),
        name="ffn_dense_ln",
    )(x, modp, w1, w3, w2, ln_g.reshape(1, d), ln_b.reshape(1, d))


def _router_kernel(x_ref, mod_ref, r_ref, h_ref, lg_ref):
    m = mod_ref[0]
    h = x_ref[...] * (1.0 + m[4:5]) + m[3:4]
    h_ref[...] = h
    a1, a2, a3 = _split3(h)
    r1, r2, r3 = _split3(r_ref[...])
    lg_ref[...] = (_dot(a1, r1) + _dot(a1, r2) + _dot(a2, r1)
                   + _dot(a2, r2) + _dot(a1, r3) + _dot(a3, r1))


def _router(x, modp, router_pad, rows_per_batch):
    m = x.shape[0]
    d = D_MODEL
    tm = min(512, m)
    mi = _mod_index(rows_per_batch, tm)
    return pl.pallas_call(
        _router_kernel,
        grid=(m // tm,),
        in_specs=[pl.BlockSpec((tm, d), lambda i: (i, 0)),
                  pl.BlockSpec((1, 8, d), lambda i: (mi(i), 0, 0)),
                  pl.BlockSpec((d, 128), lambda i: (0, 0))],
        out_specs=[pl.BlockSpec((tm, d), lambda i: (i, 0)),
                   pl.BlockSpec((tm, 128), lambda i: (i, 0))],
        out_shape=[jax.ShapeDtypeStruct((m, d), F32), jax.ShapeDtypeStruct((m, 128), F32)],
        compiler_params=_cparams(("arbitrary",)),
        name="moe_router",
    )(x, modp, router_pad)


def _moe_ffn_kernel(be_ref, nu_ref, xp_ref, w1_ref, w3_ref, w2_ref, o_ref, x_ref, acc_ref):
    j = pl.program_id(0)
    k = pl.program_id(1)

    @pl.when(j < nu_ref[0])
    def _():
        @pl.when(k == 0)
        def _():
            x_ref[...] = xp_ref[...].astype(BF16)
            acc_ref[...] = jnp.zeros(acc_ref.shape, F32)

        x = x_ref[...]
        u = _silu(_dot(x, w1_ref[0])) * _dot(x, w3_ref[0])
        acc_ref[...] += _dot(u.astype(BF16), w2_ref[0])

        @pl.when(k == pl.num_programs(1) - 1)
        def _():
            o_ref[...] = acc_ref[...]

    @pl.when((j >= nu_ref[0]) & (k == pl.num_programs(1) - 1))
    def _():
        o_ref[...] = jnp.zeros(o_ref.shape, F32)


def _moe_ffn(xb, block_e, n_used, w1, w3, w2):
    ns = xb.shape[0]
    d = D_MODEL
    tm = MOE_ROWS
    dex = w1.shape[2]
    tf = dex // 2
    grid_spec = pltpu.PrefetchScalarGridSpec(
        num_scalar_prefetch=2,
        grid=(ns // tm, dex // tf),
        in_specs=[pl.BlockSpec((tm, d), lambda j, k, be, nu: (j, 0)),
                  pl.BlockSpec((1, d, tf), lambda j, k, be, nu: (be[j], 0, k)),
                  pl.BlockSpec((1, d, tf), lambda j, k, be, nu: (be[j], 0, k)),
                  pl.BlockSpec((1, tf, d), lambda j, k, be, nu: (be[j], k, 0))],
        out_specs=pl.BlockSpec((tm, d), lambda j, k, be, nu: (j, 0)),
        scratch_shapes=[pltpu.VMEM((tm, d), BF16), pltpu.VMEM((tm, d), F32)])
    return pl.pallas_call(
        _moe_ffn_kernel,
        grid_spec=grid_spec,
        out_shape=jax.ShapeDtypeStruct((ns, d), F32),
        compiler_params=_cparams(("arbitrary", "arbitrary")),
        name="moe_expert_ffn",
    )(block_e, n_used, xb, w1, w3, w2)


def _combine_kernel(x_ref, mod_ref, y0_ref, y1_ref, g_ref, lng_ref, lnb_ref, o_ref):
    m = mod_ref[0]
    g = g_ref[...]
    f = g[:, 0:1] * y0_ref[...] + g[:, 1:2] * y1_ref[...]
    o_ref[...] = _layer_norm_rows(DN_ALPHA * x_ref[...] + m[5:6] * f, lng_ref[...], lnb_ref[...])


def _moe_combine(x, modp, y0, y1, gate_pad, ln_g, ln_b, rows_per_batch):
    m = x.shape[0]
    d = D_MODEL
    tm = min(512, m)
    mi = _mod_index(rows_per_batch, tm)
    row = lambda i: (i, 0)
    return pl.pallas_call(
        _combine_kernel,
        grid=(m // tm,),
        in_specs=[pl.BlockSpec((tm, d), row),
                  pl.BlockSpec((1, 8, d), lambda i: (mi(i), 0, 0)),
                  pl.BlockSpec((tm, d), row), pl.BlockSpec((tm, d), row),
                  pl.BlockSpec((tm, 128), row),
                  pl.BlockSpec((1, d), lambda i: (0, 0)), pl.BlockSpec((1, d), lambda i: (0, 0))],
        out_specs=pl.BlockSpec((tm, d), row),
        out_shape=jax.ShapeDtypeStruct((m, d), F32),
        compiler_params=_cparams(("arbitrary",)),
        name="moe_combine_ln",
    )(x, modp, y0, y1, gate_pad, ln_g.reshape(1, d), ln_b.reshape(1, d))


def _cast_kernel(x_ref, o_ref):
    o_ref[...] = x_ref[...].astype(o_ref.dtype)


def _to_bf16(w, group):
    _, e, a, b = w.shape
    ta, tb = (a // 4, b) if a >= b else (a, b // 4)
    cut_rows = a >= b
    return pl.pallas_call(
        _cast_kernel,
        grid=(e, 4),
        in_specs=[pl.BlockSpec((None, 1, ta, tb),
                               lambda i, r: (group, i, r, 0) if cut_rows else (group, i, 0, r))],
        out_specs=pl.BlockSpec((1, ta, tb), lambda i, r: (i, r, 0) if cut_rows else (i, 0, r)),
        out_shape=jax.ShapeDtypeStruct((e, a, b), BF16),
        compiler_params=_cparams(("arbitrary", "arbitrary")),
        name="weight_cast",
    )(w)


def _moe(x, modp, router, w1, w3, w2, ln_g, ln_b, rows_per_batch):
    n, d = x.shape
    e = router.shape[1]
    h2, logits = _router(x, modp, jnp.pad(router, ((0, 0), (0, 128 - e))), rows_per_batch)
    top_val, top_idx = lax.top_k(logits[:, :e], TOP_K)
    gate = jax.nn.softmax(top_val, axis=-1)
    flat_e = top_idx.reshape(-1)
    flat_t = jnp.repeat(jnp.arange(n, dtype=jnp.int32), TOP_K)
    order = jnp.argsort(flat_e, stable=True).astype(jnp.int32)
    rank = jnp.argsort(order).astype(jnp.int32)
    counts = jnp.sum((flat_e[:, None] == jnp.arange(e)[None, :]).astype(jnp.int32), axis=0)
    starts = jnp.cumsum(counts) - counts
    padded = (counts + MOE_ROWS - 1) // MOE_ROWS * MOE_ROWS
    pad_end = jnp.cumsum(padded)
    pad_start = pad_end - padded
    n_blocks = -(-(n * TOP_K) // MOE_ROWS) + e
    n_slots = n_blocks * MOE_ROWS
    block_start = jnp.arange(n_blocks) * MOE_ROWS
    block_e = jnp.minimum(jnp.sum(block_start[:, None] >= pad_end[None, :], axis=1), e - 1).astype(jnp.int32)
    n_used = (pad_end[-1] // MOE_ROWS).astype(jnp.int32).reshape(1)
    slot_e = jnp.repeat(block_e, MOE_ROWS)
    slot_off = jnp.arange(n_slots, dtype=jnp.int32) - pad_start[slot_e]
    slot_valid = slot_off < counts[slot_e]
    slot_src = jnp.where(slot_valid, starts[slot_e] + slot_off, 0)
    slot_tok = jnp.where(slot_valid, flat_t[order[slot_src]], 0).astype(jnp.int32)
    pos = (pad_start[flat_e] + rank - starts[flat_e]).astype(jnp.int32).reshape(n, TOP_K)
    xb = jnp.take(h2, slot_tok, axis=0, mode="clip")
    yb = _moe_ffn(xb, block_e, n_used, w1, w3, w2)
    y0 = jnp.take(yb, pos[:, 0], axis=0, mode="clip")
    y1 = jnp.take(yb, pos[:, 1], axis=0, mode="clip")
    gate_pad = jnp.pad(gate, ((0, 0), (0, 128 - TOP_K)))
    return _moe_combine(x, modp, y0, y1, gate_pad, ln_g, ln_b, rows_per_batch)


def kernel(x, c, ctx, c_ctx, ada_w, ada_b, w_in, hy_conv_w, hy_conv_b, hy_ff_w1, hy_ff_b1, hy_ff_freq, hy_ff_w2, hy_ff_b2, hy_ff_w3, hy_skip, hg_lb_logits, hg_norm_w, p_hy, p_hg, p_rt, w_o, ln1_g, ln1_b, ln2_g, ln2_b, ffn_w1, ffn_w3, ffn_w2, moe_router, moe_w1, moe_w3, moe_w2):
    batch, l, d = x.shape
    lc = ctx.shape[1]
    assert d == D_MODEL and batch <= 8
    assert l % 512 == 0 and l % GRID_W == 0 and lc % max(RET_CHUNK, DFT_HALF) == 0

    cs = jnp.cumsum(jax.nn.softmax(hg_lb_logits.astype(F32), axis=1), axis=1)
    lower_bounds = cs - cs[:, :1]
    cc = jnp.zeros((16, d), F32).at[:batch].set(c).at[8].set(c_ctx)
    ret_tables = _retention_tables(lc, l)
    dft_lat = _dft_matrices(min(HY_BLOCK, l))
    dft_ctx = _dft_matrices(min(HY_BLOCK, lc))

    x_lat = x.reshape(batch * l, d)
    x_ctx = ctx.reshape(batch * lc, d)
    for i in range(DEPTH):
        need_ctx_out = i < DEPTH - 1
        use_moe = i % 2 == 1
        g = i // 2
        lp = {'hy_conv_w': hy_conv_w[i], 'hy_conv_b': hy_conv_b[i], 'hy_ff_w1': hy_ff_w1[i],
              'hy_ff_b1': hy_ff_b1[i], 'hy_ff_freq': hy_ff_freq[i], 'hy_ff_w2': hy_ff_w2[i],
              'hy_ff_b2': hy_ff_b2[i], 'hy_ff_w3': hy_ff_w3[i], 'hy_skip': hy_skip[i],
              'hg_norm_w': hg_norm_w[i], 'ln1_g': ln1_g[i], 'ln1_b': ln1_b[i]}
        wbf = {'p_hy': p_hy[i].astype(BF16), 'p_hg': p_hg[i].astype(BF16),
               'p_rt': p_rt[i].astype(BF16), 'w_o': w_o[i].astype(BF16)}
        mod = _ada(cc, ada_w, i, ada_b[i])
        modp = jnp.pad(mod.reshape(16, 6, d), ((0, 0), (0, 2), (0, 0)))
        lb = lower_bounds[:, i]
        gl_tab = jnp.pad(jnp.stack([lb, 1.0 - lb], axis=1), ((0, 0), (0, 6), (0, 0)))

        p_lat = _proj(x_lat, modp, w_in, i, N_IN_COLS, l)
        p_ctx = _proj(x_ctx, modp, w_in, i, N_IN_COLS if need_ctx_out else N_STATE_COLS, None)

        o_hg_l, o_hg_c = _gla(p_lat, p_ctx, gl_tab, batch, need_ctx_out)
        y_rt_l, y_rt_c = _retention(p_lat, p_ctx, ret_tables, batch, need_ctx_out)
        y_hy_l = _hyena(p_lat, lp, dft_lat, batch, True)
        x_lat_new = _merge(y_hy_l, o_hg_l, y_rt_l, p_lat, x_lat, modp, lp, wbf, l)
        if need_ctx_out:
            y_hy_c = _hyena(p_ctx, lp, dft_ctx, batch, False)
            x_ctx = _merge(y_hy_c, o_hg_c, y_rt_c, p_ctx, x_ctx, modp, lp, wbf, None)
        x_lat = x_lat_new

        if use_moe:
            if need_ctx_out:
                raise NotImplementedError("MoE layer with a context output is not part of this trunk")
            w1, w3, w2 = _to_bf16(moe_w1, g), _to_bf16(moe_w3, g), _to_bf16(moe_w2, g)
            x_lat = _moe(x_lat, modp, moe_router[g], w1, w3, w2, ln2_g[i], ln2_b[i], l)
        else:
            w1, w3, w2 = ffn_w1[g].astype(BF16), ffn_w3[g].astype(BF16), ffn_w2[g].astype(BF16)
            x_lat = _ffn_dense(x_lat, modp, w1, w3, w2, ln2_g[i], ln2_b[i], l)
            if need_ctx_out:
                x_ctx = _ffn_dense(x_ctx, modp, w1, w3, w2, ln2_g[i], ln2_b[i], None)
    return x_lat.reshape(batch, l, d)
```

```python
import functools
import math

import numpy as np
import jax
import jax.numpy as jnp
from jax import lax
from jax.experimental import pallas as pl
from jax.experimental.pallas import tpu as pltpu

F32 = jnp.float32
BF16 = jnp.bfloat16

D_MODEL = 1024
DEPTH = 2
GRID_W = 64
HY_DIM = 1024
HY_ORDER = 2
HY_BANDS = 16
HY_DECAY_TARGET = 1e-2
HY_DECAY_PCT_SHORT = 0.3
HY_DECAY_PCT_LONG = 1.5
HG_HEADS = 8
HG_HK = 128
RT_HEADS = 4
RT_HK = 256
RT_HV = 512
RT_ROPE_BASE = 10000.0
N_STATE_COLS = 6144
N_IN_COLS = 17408
N_EXPERTS = 8
TOP_K = 2
DN_ALPHA = (2 * DEPTH) ** 0.25
LN_EPS = 1e-5

COL_FF, COL_FB, COL_HI, COL_RK, COL_RV = 0, 8, 16, 24, 32
COL_HQ, COL_HG, COL_RQ, COL_RG, COL_HY, COL_BR = 48, 56, 64, 72, 88, 112

GLA_CHUNK = 256
GLA_LEVELS = 8
GLA_PAD = 8
GLA_MIN_LOG2 = -1e30
RET_CHUNK = 256
DFT_HALF = 256
HY_BLOCK = 512
CONV_ROWS = 256
MOE_ROWS = 512
VMEM_LIMIT = 56 * 1024 * 1024


def _cparams(sem, flags=None):
    return pltpu.CompilerParams(dimension_semantics=sem, vmem_limit_bytes=VMEM_LIMIT, flags=flags)


def _silu(x):
    return x * jax.nn.sigmoid(x)


def _layer_norm_rows(r, g, b):
    mu = jnp.mean(r, axis=-1, keepdims=True)
    d = r - mu
    var = jnp.mean(d * d, axis=-1, keepdims=True)
    return d * lax.rsqrt(var + LN_EPS) * g + b


def _split3(x):
    h = x.astype(BF16)
    r = x - h.astype(F32)
    m = r.astype(BF16)
    l = (r - m.astype(F32)).astype(BF16)
    return h, m, l


def _dot(a, b):
    return jnp.dot(a, b, preferred_element_type=F32)


def _dot_nt(a, b):
    return lax.dot_general(a, b, (((1,), (1,)), ((), ())), preferred_element_type=F32)


def _dot_tn(a, b):
    return lax.dot_general(a, b, (((0,), (0,)), ((), ())), preferred_element_type=F32)


def _ada_kernel(c_ref, w_ref, b_ref, o_ref):
    s = _silu(c_ref[...])
    o_ref[...] = _dot(s.astype(BF16), w_ref[...].astype(BF16)) + b_ref[...]


def _ada(cc, w, layer, b):
    n = w.shape[2]
    tn = 1024
    return pl.pallas_call(
        _ada_kernel,
        grid=(n // tn,),
        in_specs=[pl.BlockSpec((16, D_MODEL), lambda j: (0, 0)),
                  pl.BlockSpec((None, D_MODEL, tn), lambda j: (layer, 0, j)),
                  pl.BlockSpec((1, tn), lambda j: (0, j))],
        out_specs=pl.BlockSpec((16, tn), lambda j: (0, j)),
        out_shape=jax.ShapeDtypeStruct((16, n), F32),
        compiler_params=_cparams(("arbitrary",)),
        name="ada_mod",
    )(cc, w, b.reshape(1, n))


def _mod_index(rows_per_batch, tm):
    if rows_per_batch is None:
        return lambda i: 8
    return lambda i: (i * tm) // rows_per_batch


def _proj_kernel(x_ref, mod_ref, w_ref, o_ref, h_ref):
    @pl.when(pl.program_id(1) == 0)
    def _():
        m = mod_ref[0]
        h_ref[...] = (x_ref[...] * (1.0 + m[1:2]) + m[0:1]).astype(BF16)

    o_ref[...] = _dot(h_ref[...], w_ref[...].astype(BF16)).astype(o_ref.dtype)


def _proj(x, modp, w, layer, n_cols, rows_per_batch):
    m = x.shape[0]
    tm = min(2048, rows_per_batch or m)
    tn = 1024
    mi = _mod_index(rows_per_batch, tm)
    return pl.pallas_call(
        _proj_kernel,
        grid=(m // tm, n_cols // tn),
        in_specs=[pl.BlockSpec((tm, D_MODEL), lambda i, j: (i, 0)),
                  pl.BlockSpec((1, 8, D_MODEL), lambda i, j: (mi(i), 0, 0)),
                  pl.BlockSpec((None, D_MODEL, tn), lambda i, j: (layer, 0, j))],
        out_specs=pl.BlockSpec((tm, tn), lambda i, j: (i, j)),
        out_shape=jax.ShapeDtypeStruct((m, n_cols), BF16),
        scratch_shapes=[pltpu.VMEM((tm, D_MODEL), BF16)],
        compiler_params=_cparams(("arbitrary", "arbitrary")),
        name="in_proj",
    )(x, modp, w)


def _gla_gates(fl, gl_ref, d):
    lb = gl_ref[d, 0:1, :]
    oml = gl_ref[d, 1:2, :]
    t = jnp.exp(-jnp.abs(fl))
    r = 1.0 / (1.0 + t)
    tr = t * r
    pos = fl >= 0.0
    f = lb + oml * jnp.where(pos, r, tr)
    key = oml * jnp.where(pos, tr, r)
    return jnp.maximum(jnp.log2(f), GLA_MIN_LOG2), key


def _gla_cumsum(g, rev):
    c = g.shape[0]
    r = lax.broadcasted_iota(jnp.int32, (c, c), 0)
    u = lax.broadcasted_iota(jnp.int32, (c, c), 1)
    tri = jnp.where((u >= r) if rev else (u <= r), 1.0, 0.0).astype(BF16)
    return _dot(jnp.concatenate([tri, tri, tri], axis=1), jnp.concatenate(_split3(g), axis=0))


def _gla_state_update(st, b, kk, vv, rev):
    c = b.shape[0]
    end = 0 if rev else c - 1
    b_end = b[end:end + 1, :]
    khat = (kk * jnp.exp2(b_end - b)).astype(BF16)
    return st * jnp.exp2(b_end) + _dot_tn(vv, khat)


def _gla_level_table():
    c = GLA_CHUNK
    t = np.arange(c)[:, None]
    s = np.arange(c)[None, :]
    top_bit = np.floor(np.log2(np.maximum(t ^ s, 1))).astype(np.int32)
    fwd = np.where(t > s, top_bit, np.where(t == s, GLA_LEVELS, -1))
    return np.stack([fwd, fwd.T]).astype(np.int32)


def _gla_ref_rows(b, bpad_ref, level, rev):
    c = b.shape[0]
    half = 1 << level
    blk = 2 * half
    idx = half if rev else half - 1
    if blk % 8 == 0:
        r = b.reshape(c // blk, blk, HG_HK)[:, idx:idx + 1, :]
        return jnp.broadcast_to(r, (c // blk, blk, HG_HK)).reshape(c, HG_HK)
    m = lax.broadcasted_iota(jnp.int32, (c, HG_HK), 0) % blk
    out = b
    for v in range(blk):
        if v != idx:
            off = GLA_PAD + idx - v
            out = jnp.where(m == v, bpad_ref[off:off + c, :], out)
    return out


def _gla_chunk_out(b, kk, vv, qq, st, rev, lvt, bpad_ref):
    c = GLA_CHUNK
    o = _dot_nt((qq * jnp.exp2(b)).astype(BF16), st.astype(BF16))
    bpad_ref[GLA_PAD:GLA_PAD + c, :] = b
    sc = jnp.zeros((c, c), F32)
    for level in range(GLA_LEVELS):
        half = 1 << level
        blk = 2 * half
        ref = _gla_ref_rows(b, bpad_ref, level, rev)
        kt = (kk * jnp.exp2(ref - b)).astype(BF16)
        if half % 8:
            p = _dot_nt((qq * jnp.exp2(b - ref)).astype(BF16), kt)
            sc = jnp.where(lvt == level, p, sc)
        else:
            nblk = c // blk
            lo, hi = (0, half) if rev else (half, blk)

            def part(x):
                return x.reshape(nblk, blk, x.shape[-1])[:, lo:hi, :]

            qt = (part(qq) * jnp.exp2(part(b) - part(ref))).reshape(c // 2, HG_HK)
            p = _dot_nt(qt.astype(BF16), kt).reshape(nblk, half, c)
            new = jnp.where(part(lvt) == level, p, part(sc))
            rest = sc.reshape(nblk, blk, c)[:, half:blk, :] if rev else sc.reshape(nblk, blk, c)[:, 0:half, :]
            sc = jnp.concatenate([new, rest] if rev else [rest, new], axis=1).reshape(c, c)
    sc = jnp.where(lvt == GLA_LEVELS, _dot_nt(qq.astype(BF16), kk.astype(BF16)), sc)
    return o + _dot(sc.astype(BF16), vv)


def _gla_kernel(*refs, need_ctx_out, n_lat, n_ctx):
    if need_ctx_out:
        ffl, fbl, il, ql, ffc, fbc, ic, qc, gl_ref, lv_ref, ol_ref, oc_ref = refs[:12]
        bl, kl, bc, kc, qsl, qsc, bpf, bpr = refs[12:]
    else:
        ffl, fbl, il, ql, ffc, fbc, ic, gl_ref, lv_ref, ol_ref = refs[:10]
        bl, kl, bc, kc, qsl, bpf, bpr = refs[10:]
        qc = oc_ref = qsc = None
    c = GLA_CHUNK
    q_scale = HG_HK ** -0.5
    bpf[...] = jnp.zeros(bpf.shape, F32)
    bpr[...] = jnp.zeros(bpr.shape, F32)
    ol_ref[...] = jnp.zeros(ol_ref.shape, F32)
    if need_ctx_out:
        oc_ref[...] = jnp.zeros(oc_ref.shape, F32)

    def chunk_rows(idx):
        return pl.ds(pl.multiple_of(idx * c, c), c)

    def prepare(f_refs, q_ref, b_sc, k_sc, q_sc, n):
        def body(j, carry):
            rows = chunk_rows(j)
            for d in range(2):
                g, kk = _gla_gates(f_refs[d][rows, :].astype(F32), gl_ref, d)
                b_sc[d, rows, :] = _gla_cumsum(g, d == 1)
                k_sc[d, rows, :] = kk
            if q_ref is not None:
                q_sc[rows, :] = _silu(q_ref[rows, :].astype(F32)) * q_scale
            return carry
        lax.fori_loop(0, n, body, 0, unroll=min(2, n))

    prepare((ffc, fbc), qc, bc, kc, qsc, n_ctx)
    prepare((ffl, fbl), ql, bl, kl, qsl, n_lat)

    def one_dir(d, b_sc, k_sc, i_ref, q_sc, o_ref, idx, st):
        rev = d == 1
        rows = chunk_rows(idx)
        b, kk, vv = b_sc[d, rows, :], k_sc[d, rows, :], i_ref[rows, :]
        if q_sc is not None:
            o_ref[rows, :] += _gla_chunk_out(b, kk, vv, q_sc[rows, :], st, rev, lv_ref[d],
                                             bpr if rev else bpf)
        return _gla_state_update(st, b, kk, vv, rev)

    def ctx_body(j, carry):
        return (one_dir(0, bc, kc, ic, qsc, oc_ref, j, carry[0]),
                one_dir(1, bc, kc, ic, qsc, oc_ref, n_ctx - 1 - j, carry[1]))

    def lat_body(j, carry):
        return (one_dir(0, bl, kl, il, qsl, ol_ref, j, carry[0]),
                one_dir(1, bl, kl, il, qsl, ol_ref, n_lat - 1 - j, carry[1]))

    zero = jnp.zeros((HG_HK, HG_HK), F32)
    carry = lax.fori_loop(0, n_ctx, ctx_body, (zero, zero), unroll=min(2, n_ctx))
    lax.fori_loop(0, n_lat, lat_body, carry)


def _gla(p_lat, p_ctx, gl_tab, batch, need_ctx_out):
    l = p_lat.shape[0] // batch
    lc = p_ctx.shape[0] // batch
    w = HG_HK
    c = GLA_CHUNK

    def col(base):
        return lambda b, h: (b, base + h)

    lat_specs = [pl.BlockSpec((l, w), col(cb)) for cb in (COL_FF, COL_FB, COL_HI, COL_HQ)]
    ctx_cols = (COL_FF, COL_FB, COL_HI) + ((COL_HQ,) if need_ctx_out else ())
    ctx_specs = [pl.BlockSpec((lc, w), col(cb)) for cb in ctx_cols]
    in_specs = lat_specs + ctx_specs + [pl.BlockSpec((2, 8, w), lambda b, h: (0, 0, h)),
                                        pl.BlockSpec((2, c, c), lambda b, h: (0, 0, 0))]
    args = [p_lat] * 4 + [p_ctx] * len(ctx_cols) + [gl_tab, jnp.asarray(_gla_level_table())]
    out_specs = [pl.BlockSpec((l, w), lambda b, h: (b, h))]
    out_shape = [jax.ShapeDtypeStruct((batch * l, HG_HEADS * w), F32)]
    scratch = [pltpu.VMEM((2, l, w), F32), pltpu.VMEM((2, l, w), F32),
               pltpu.VMEM((2, lc, w), F32), pltpu.VMEM((2, lc, w), F32), pltpu.VMEM((l, w), F32)]
    if need_ctx_out:
        out_specs.append(pl.BlockSpec((lc, w), lambda b, h: (b, h)))
        out_shape.append(jax.ShapeDtypeStruct((batch * lc, HG_HEADS * w), F32))
        scratch.append(pltpu.VMEM((lc, w), F32))
    scratch += [pltpu.VMEM((c + 2 * GLA_PAD, w), F32)] * 2
    outs = pl.pallas_call(
        functools.partial(_gla_kernel, need_ctx_out=need_ctx_out, n_lat=l // c, n_ctx=lc // c),
        grid=(batch, HG_HEADS),
        in_specs=in_specs,
        out_specs=out_specs,
        out_shape=out_shape,
        scratch_shapes=scratch,
        compiler_params=_cparams(("arbitrary", "arbitrary")),
        name="hgrn2_scan",
    )(*args)
    return outs[0], (outs[1] if need_ctx_out else None)


def _rotate(x, cos, sin):
    half = x.shape[-1] // 2
    x1, x2 = x[:, :half], x[:, half:]
    return jnp.concatenate([x1 * cos - x2 * sin, x1 * sin + x2 * cos], axis=-1)


def _ret_kernel(*refs, need_ctx_out, n_lat, n_ctx):
    if need_ctx_out:
        (ql, kl, vl, gl, qc, kc, vc, gc, cos_ref, sin_ref, intra_ref, qk_ref, sd_ref,
         yl_ref, yc_ref, qrl, krl, qrc, krc, ol, oc, s_fwd, s_rev) = refs
    else:
        (ql, kl, vl, gl, kc, vc, cos_ref, sin_ref, intra_ref, qk_ref, sd_ref,
         yl_ref, qrl, krl, krc, ol, s_fwd, s_rev) = refs
        qc = gc = yc_ref = qrc = oc = None
    s_refs = (s_fwd, s_rev)
    c = RET_CHUNK
    lc = n_ctx * c
    k_scale = RT_HK ** -0.5

    def rot_all(src, dst, n, pos0, scale):
        def body(j, carry):
            rows = pl.ds(pl.multiple_of(j * c, c), c)
            prow = pl.ds(pl.multiple_of(pos0 + j * c, c), c)
            x = src[rows, :].astype(F32)
            dst[rows, :] = (_rotate(x, cos_ref[prow, :], sin_ref[prow, :]) * scale).astype(BF16)
            return carry
        lax.fori_loop(0, n, body, 0)

    rot_all(kl, krl, n_lat, lc, k_scale)
    rot_all(kc, krc, n_ctx, 0, k_scale)
    rot_all(ql, qrl, n_lat, lc, 1.0)
    if need_ctx_out:
        rot_all(qc, qrc, n_ctx, 0, 1.0)

    for s_ref in s_refs:
        s_ref[...] = jnp.zeros(s_ref.shape, F32)
    ol[...] = jnp.zeros(ol.shape, F32)
    if need_ctx_out:
        oc[...] = jnp.zeros(oc.shape, F32)

    def update(d, kr, v_ref, rows):
        kd = qk_ref[d, 0, c:2 * c, :]
        sdec = sd_ref[d, 0, 0:1, :]
        kh = (kr[rows, :].astype(F32) * jnp.concatenate([kd, kd], axis=1)).astype(BF16)
        s_ref = s_refs[d]
        s_ref[...] = (s_ref[...] * jnp.concatenate([sdec] * (RT_HV // 128), axis=1)
                      + _dot_tn(kh, v_ref[rows, :]))

    def out_step(d, qr, kr, v_ref, o_ref, rows):
        qd = qk_ref[d, 0, 0:c, :]
        qcb = qr[rows, :]
        sc = _dot_nt(qcb, kr[rows, :]) * intra_ref[d, 0]
        qh = (qcb.astype(F32) * jnp.concatenate([qd, qd], axis=1)).astype(BF16)
        o_ref[rows, :] += (_dot(qh, s_refs[d][...].astype(BF16))
                           + _dot(sc.astype(BF16), v_ref[rows, :]))
        update(d, kr, v_ref, rows)

    def ctx_body(j, carry):
        for d in range(2):
            idx = (n_ctx - 1 - j) if d == 1 else j
            rows = pl.ds(pl.multiple_of(idx * c, c), c)
            if need_ctx_out:
                out_step(d, qrc, krc, vc, oc, rows)
            else:
                update(d, krc, vc, rows)
        return carry

    def lat_body(j, carry):
        for d in range(2):
            idx = (n_lat - 1 - j) if d == 1 else j
            out_step(d, qrl, krl, vl, ol, pl.ds(pl.multiple_of(idx * c, c), c))
        return carry

    lax.fori_loop(0, n_ctx, ctx_body, 0)
    lax.fori_loop(0, n_lat, lat_body, 0, unroll=2)

    def readout(o_ref, g_ref, y_ref, n):
        def body(j, carry):
            rows = pl.ds(pl.multiple_of(j * c, c), c)
            o = o_ref[rows, :]
            y = o * lax.rsqrt(jnp.mean(o * o, axis=-1, keepdims=True) + LN_EPS)
            y_ref[rows, :] = (y * _silu(g_ref[rows, :].astype(F32))).astype(y_ref.dtype)
            return carry
        lax.fori_loop(0, n, body, 0)

    readout(ol, gl, yl_ref, n_lat)
    if need_ctx_out:
        readout(oc, gc, yc_ref, n_ctx)


def _retention_tables(lc, l):
    half = RT_HK // 2
    inv = 1.0 / (RT_ROPE_BASE ** jnp.linspace(0.0, 1.0, half, dtype=F32))
    ang = jnp.arange(lc + l, dtype=F32)[:, None] * inv[None, :]
    j = jnp.arange(2 * RT_HEADS, dtype=F32)
    lg_all = jnp.log1p(-jnp.exp2(-5.0 - j))
    c = RET_CHUNK
    pos = jnp.arange(c, dtype=F32)
    rel = pos[:, None] - pos[None, :]
    intra, qk, sd = [], [], []
    for d in range(2):
        lg = lg_all[d::2]
        m = jnp.where(rel >= 0, jnp.exp(jnp.maximum(rel, 0.0)[None] * lg[:, None, None]), 0.0)
        qdec = jnp.exp((pos + 1.0)[None, :] * lg[:, None])
        kdec = jnp.exp((c - 1.0 - pos)[None, :] * lg[:, None])
        if d == 1:
            m = jnp.swapaxes(m, 1, 2)
            qdec = qdec[:, ::-1]
            kdec = kdec[:, ::-1]
        intra.append(m)
        qk.append(jnp.broadcast_to(jnp.concatenate([qdec, kdec], axis=1)[:, :, None],
                                   (RT_HEADS, 2 * c, 128)))
        sd.append(jnp.broadcast_to(jnp.exp(c * lg)[:, None, None], (RT_HEADS, 8, 128)))
    return (jnp.cos(ang), jnp.sin(ang), jnp.stack(intra), jnp.stack(qk), jnp.stack(sd))


def _retention(p_lat, p_ctx, tables, batch, need_ctx_out):
    l = p_lat.shape[0] // batch
    lc = p_ctx.shape[0] // batch
    cos, sin, intra, qk, sd = tables
    c = RET_CHUNK

    def col(base):
        return lambda b, h: (b, base + h)

    kq, kk_, kv, kg = COL_RQ * 128 // RT_HK, COL_RK * 128 // RT_HK, COL_RV * 128 // RT_HV, COL_RG * 128 // RT_HV
    lat_specs = [pl.BlockSpec((l, RT_HK), col(kq)), pl.BlockSpec((l, RT_HK), col(kk_)),
                 pl.BlockSpec((l, RT_HV), col(kv)), pl.BlockSpec((l, RT_HV), col(kg))]
    if need_ctx_out:
        ctx_specs = [pl.BlockSpec((lc, RT_HK), col(kq)), pl.BlockSpec((lc, RT_HK), col(kk_)),
                     pl.BlockSpec((lc, RT_HV), col(kv)), pl.BlockSpec((lc, RT_HV), col(kg))]
    else:
        ctx_specs = [pl.BlockSpec((lc, RT_HK), col(kk_)), pl.BlockSpec((lc, RT_HV), col(kv))]
    tab_specs = [pl.BlockSpec((lc + l, 128), lambda b, h: (0, 0)),
                 pl.BlockSpec((lc + l, 128), lambda b, h: (0, 0)),
                 pl.BlockSpec((2, 1, c, c), lambda b, h: (0, h, 0, 0)),
                 pl.BlockSpec((2, 1, 2 * c, 128), lambda b, h: (0, h, 0, 0)),
                 pl.BlockSpec((2, 1, 8, 128), lambda b, h: (0, h, 0, 0))]
    args = [p_lat] * 4 + [p_ctx] * len(ctx_specs) + [cos, sin, intra, qk, sd]
    out_specs = [pl.BlockSpec((l, RT_HV), lambda b, h: (b, h))]
    out_shape = [jax.ShapeDtypeStruct((batch * l, RT_HEADS * RT_HV), BF16)]
    scratch = [pltpu.VMEM((l, RT_HK), BF16), pltpu.VMEM((l, RT_HK), BF16)]
    if need_ctx_out:
        out_specs.append(pl.BlockSpec((lc, RT_HV), lambda b, h: (b, h)))
        out_shape.append(jax.ShapeDtypeStruct((batch * lc, RT_HEADS * RT_HV), BF16))
        scratch += [pltpu.VMEM((lc, RT_HK), BF16), pltpu.VMEM((lc, RT_HK), BF16),
                    pltpu.VMEM((l, RT_HV), F32), pltpu.VMEM((lc, RT_HV), F32)]
    else:
        scratch += [pltpu.VMEM((lc, RT_HK), BF16), pltpu.VMEM((l, RT_HV), F32)]
    scratch += [pltpu.VMEM((RT_HK, RT_HV), F32)] * 2
    outs = pl.pallas_call(
        functools.partial(_ret_kernel, need_ctx_out=need_ctx_out, n_lat=l // c, n_ctx=lc // c),
        grid=(batch, RT_HEADS),
        in_specs=lat_specs + ctx_specs + tab_specs,
        out_specs=out_specs,
        out_shape=out_shape,
        scratch_shapes=scratch,
        compiler_params=_cparams(("arbitrary", "arbitrary")),
        name="retention_scan",
    )(*args)
    return outs[0], (outs[1] if need_ctx_out else None)


def _conv_shift_matrices(grid):
    rb = CONV_ROWS
    t = np.arange(rb)
    edge_lo = (t % GRID_W == 0) if grid else (t == 0)
    edge_hi = (t % GRID_W == GRID_W - 1) if grid else (t == rb - 1)
    s0 = ((t[:, None] - 1 == t[None, :]) & ~edge_lo[:, None]).astype(np.float32)
    s2 = ((t[:, None] + 1 == t[None, :]) & ~edge_hi[:, None]).astype(np.float32)
    return s0, s2


def _short_conv_kernel(x_ref, w_ref, b_ref, s0_ref, s2_ref, o_ref, xpad_ref, *, grid):
    l, tc = x_ref.shape
    p = GRID_W
    rb = CONV_ROWS
    zeros = jnp.zeros((p, tc), BF16)
    xpad_ref[0:p, :] = zeros
    xpad_ref[p + l:p + l + p, :] = zeros
    xpad_ref[p:p + l, :] = x_ref[...]
    w = [w_ref[k:k + 1, :].astype(BF16) for k in range(9)]
    rows = range(3) if grid else (1,)

    def body(i, carry):
        r0 = i * rb
        part = []
        for dj in range(3):
            acc = None
            for di in rows:
                start = pl.multiple_of(r0 + p + (di - 1) * GRID_W, GRID_W)
                t = xpad_ref[pl.ds(start, rb), :] * w[di * 3 + dj]
                acc = t if acc is None else acc + t
            part.append(acc)
        out = (part[1].astype(F32) + _dot(s0_ref[...], part[0]) + _dot(s2_ref[...], part[2])
               + b_ref[...])
        o_ref[pl.ds(pl.multiple_of(r0, rb), rb), :] = out.astype(o_ref.dtype)
        return carry

    lax.fori_loop(0, l // rb, body, 0, unroll=min(4, l // rb))


def _short_conv(p_arr, w9, bias, batch, grid):
    l = p_arr.shape[0] // batch
    assert l % CONV_ROWS == 0 and (grid or l == CONV_ROWS)
    c3 = 3 * HY_DIM
    tc = 256
    base = COL_HY * 128 // tc
    s0, s2 = _conv_shift_matrices(grid)
    return pl.pallas_call(
        functools.partial(_short_conv_kernel, grid=grid),
        grid=(batch, c3 // tc),
        in_specs=[pl.BlockSpec((l, tc), lambda b, j: (b, base + j)),
                  pl.BlockSpec((9, tc), lambda b, j: (0, j)),
                  pl.BlockSpec((1, tc), lambda b, j: (0, j)),
                  pl.BlockSpec((CONV_ROWS, CONV_ROWS), lambda b, j: (0, 0)),
                  pl.BlockSpec((CONV_ROWS, CONV_ROWS), lambda b, j: (0, 0))],
        out_specs=pl.BlockSpec((l, tc), lambda b, j: (b, j)),
        out_shape=jax.ShapeDtypeStruct((batch * l, c3), BF16),
        scratch_shapes=[pltpu.VMEM((l + 2 * GRID_W, tc), BF16)],
        compiler_params=_cparams(("arbitrary", "arbitrary")),
        name="hyena_short_conv",
    )(p_arr, w9, bias.reshape(1, c3), jnp.asarray(s0, BF16), jnp.asarray(s2, BF16))


def _dft_matrices(l):
    n = 2 * l
    s = int(round(math.sqrt(n)))
    while n % s:
        s -= 1
    q = n // s
    f = np.arange(l, dtype=np.int64)
    ang_a = 2.0 * np.pi * ((f[:, None] * s * np.arange(q)[None, :]) % n) / n
    ang_b = 2.0 * np.pi * ((f[:, None] * np.arange(s)[None, :]) % n) / n
    t = np.arange(n)
    rep = (t[None, :] // s == np.arange(q)[:, None]).astype(np.float32)
    til = (t[None, :] % s == np.arange(s)[:, None]).astype(np.float32)
    h = DFT_HALF
    row = lambda j: (j, 0)
    const = lambda j: (0, 0)
    fwd, inv = pl.pallas_call(
        _dft_build_kernel,
        grid=(l // h,),
        in_specs=[pl.BlockSpec((h, q), row), pl.BlockSpec((h, q), row),
                  pl.BlockSpec((h, s), row), pl.BlockSpec((h, s), row),
                  pl.BlockSpec((q, n), const), pl.BlockSpec((s, n), const)],
        out_specs=[pl.BlockSpec((1, 2, h, n), lambda j: (j, 0, 0, 0)),
                   pl.BlockSpec((h, n), row)],
        out_shape=[jax.ShapeDtypeStruct((l // h, 2, h, n), BF16),
                   jax.ShapeDtypeStruct((l, n), BF16)],
        compiler_params=_cparams(("arbitrary",)),
        name="dft_build",
    )(jnp.asarray(np.cos(ang_a), F32), jnp.asarray(np.sin(ang_a), F32),
      jnp.asarray(np.cos(ang_b), F32), jnp.asarray(np.sin(ang_b), F32),
      jnp.asarray(rep, BF16), jnp.asarray(til, BF16))
    return fwd.reshape(n, n), inv


def _dft_build_kernel(ca_ref, sa_ref, cb_ref, sb_ref, rep_ref, til_ref, fwd_ref, inv_ref):
    h, n = inv_ref.shape

    def spread(x_ref, m_ref):
        p1, p2, p3 = _split3(x_ref[...])
        m = m_ref[...]
        return _dot(p1, m) + _dot(p2, m) + _dot(p3, m)

    ca, sa = spread(ca_ref, rep_ref), spread(sa_ref, rep_ref)
    cb, sb = spread(cb_ref, til_ref), spread(sb_ref, til_ref)
    cosm = ca * cb - sa * sb
    nsin = -(sa * cb + ca * sb)
    first = pl.program_id(0) == 0
    rowi = lax.broadcasted_iota(jnp.int32, (h, n), 0)
    coli = lax.broadcasted_iota(jnp.int32, (h, n), 1)
    alt_col = (1 - 2 * (coli % 2)).astype(F32)
    fwd_ref[0, 0] = cosm.astype(BF16)
    fwd_ref[0, 1] = jnp.where(first & (rowi == 0), alt_col, nsin).astype(BF16)
    rowg = lax.broadcasted_iota(jnp.int32, (h, h), 0) + pl.program_id(0) * h
    col0 = lax.broadcasted_iota(jnp.int32, (h, h), 1) == 0
    alt_row = (1 - 2 * (rowg % 2)).astype(F32)
    pieces = []
    for jj in range(n // (2 * h)):
        cp = cosm[:, jj * h:(jj + 1) * h] * (2.0 / n)
        ip = nsin[:, jj * h:(jj + 1) * h] * (2.0 / n)
        if jj == 0:
            cp = jnp.where(col0, 1.0 / n, cp)
            ip = jnp.where(col0, alt_row * (1.0 / n), ip)
        pieces += [cp.astype(BF16), ip.astype(BF16)]
    inv_ref[...] = jnp.concatenate(pieces, axis=1)


def _filter_spectrum_kernel(f_ref, top_ref, bot_ref, o_ref):
    h = DFT_HALF
    bl = top_ref.shape[0]
    acc = _dot(f_ref[:, 0:bl], top_ref[...]) + _dot(f_ref[:, bl:2 * bl], bot_ref[...])
    kr, km = acc[:h], acc[h:]
    row0 = (lax.broadcasted_iota(jnp.int32, kr.shape, 0) == 0) & (pl.program_id(2) == 0)
    o_ref[0, 0:h, :] = kr.astype(o_ref.dtype)
    o_ref[0, h:2 * h, :] = jnp.where(row0, 0.0, km).astype(o_ref.dtype)
    o_ref[0, 2 * h:3 * h, :] = jnp.where(row0, km, kr).astype(o_ref.dtype)


def _filter_spectrum(fwd, kern_bf, nb):
    n = fwd.shape[0]
    bl = n // 2
    nc = kern_bf.shape[1]
    tm, tn = 2 * DFT_HALF, 1024
    nd = 2 * nb - 1
    wrap = 2 * nb
    return pl.pallas_call(
        _filter_spectrum_kernel,
        grid=(nc // tn, nd, n // tm),
        in_specs=[pl.BlockSpec((tm, n), lambda c, d, t: (t, 0)),
                  pl.BlockSpec((bl, tn), lambda c, d, t: ((d - (nb - 1)) % wrap, c)),
                  pl.BlockSpec((bl, tn), lambda c, d, t: ((d - nb) % wrap, c))],
        out_specs=pl.BlockSpec((1, 3 * DFT_HALF, tn), lambda c, d, t: (d, t, c)),
        out_shape=jax.ShapeDtypeStruct((nd, (n // tm) * 3 * DFT_HALF, nc), BF16),
        compiler_params=_cparams(("arbitrary", "arbitrary", "arbitrary")),
        name="hyena_filter_dft",
    )(fwd, kern_bf, kern_bf)


def _dft_fwd_kernel(f_ref, z_ref, k_ref, y_ref):
    h = DFT_HALF
    bl = f_ref.shape[1]
    nb = y_ref.shape[1]
    f = f_ref[...]
    spec = [_dot(f, z_ref[j * bl:(j + 1) * bl, :]).astype(BF16) for j in range(nb)]
    for i in range(nb):
        yr = yi = None
        for j in range(nb):
            d = i - j + nb - 1
            kr, ki, kr2 = k_ref[d, 0:h, :], k_ref[d, h:2 * h, :], k_ref[d, 2 * h:3 * h, :]
            ur, ui = spec[j][:h], spec[j][h:]
            tr = ur * kr - ui * ki
            ti = ur * ki + ui * kr2
            yr = tr if yr is None else yr + tr
            yi = ti if yi is None else yi + ti
        y_ref[0, i, 0:h, :] = yr.astype(y_ref.dtype)
        y_ref[0, i, h:2 * h, :] = yi.astype(y_ref.dtype)


def _dft_fwd(fwd, z_arr, z_col, kf, k_col, batch, nb):
    n = fwd.shape[0]
    bl = n // 2
    l = nb * bl
    tm, tn = 2 * DFT_HALF, 512
    cpt = HY_DIM // tn
    nd = kf.shape[0]
    return pl.pallas_call(
        _dft_fwd_kernel,
        grid=(cpt, n // tm, batch),
        in_specs=[pl.BlockSpec((tm, bl), lambda c, t, b: (t, 0)),
                  pl.BlockSpec((l, tn), lambda c, t, b: (b, z_col * cpt + c)),
                  pl.BlockSpec((nd, 3 * DFT_HALF, tn), lambda c, t, b: (0, t, k_col * cpt + c))],
        out_specs=pl.BlockSpec((1, nb, tm, tn), lambda c, t, b: (b, 0, t, c)),
        out_shape=jax.ShapeDtypeStruct((batch, nb, n, HY_DIM), BF16),
        compiler_params=_cparams(("arbitrary", "arbitrary", "arbitrary")),
        name="hyena_dft_fwd",
    )(fwd, z_arr, kf)


def _dft_inv_kernel(g_ref, y_ref, gate_ref, z_ref, skip_ref, o_ref):
    conv = _dot(g_ref[...], y_ref[0, 0])
    z = z_ref[...].astype(F32)
    o_ref[...] = (gate_ref[...].astype(F32) * (conv + skip_ref[...] * z)).astype(o_ref.dtype)


def _dft_inv(inv, y, gate_arr, gate_col, z_arr, z_col, skip, batch, nb):
    bl, n = inv.shape
    tn = HY_DIM
    return pl.pallas_call(
        _dft_inv_kernel,
        grid=(batch, nb),
        in_specs=[pl.BlockSpec((bl, n), lambda b, i: (0, 0)),
                  pl.BlockSpec((1, 1, n, tn), lambda b, i: (b, i, 0, 0)),
                  pl.BlockSpec((bl, tn), lambda b, i: (b * nb + i, gate_col)),
                  pl.BlockSpec((bl, tn), lambda b, i: (b * nb + i, z_col)),
                  pl.BlockSpec((1, tn), lambda b, i: (0, 0))],
        out_specs=pl.BlockSpec((bl, tn), lambda b, i: (b * nb + i, 0)),
        out_shape=jax.ShapeDtypeStruct((batch * nb * bl, tn), BF16),
        compiler_params=_cparams(("arbitrary", "arbitrary")),
        name="hyena_dft_inv",
    )(inv, y, gate_arr, z_arr, skip.reshape(1, tn))


def _tap_features(l):
    f32 = np.float32
    t01 = np.linspace(0.0, 1.0, l, dtype=f32)[:, None]
    ang = f32(2.0 * math.pi) * np.arange(l, dtype=f32)[:, None] / f32(l)
    bands = np.linspace(1e-4, HY_BANDS - 1, HY_BANDS, dtype=f32)[None, :]
    z = np.concatenate([t01, np.cos(bands * ang), -np.sin(bands * ang)], axis=-1).astype(f32)
    pos = np.concatenate([np.arange(l), [0], np.arange(l - 1, 0, -1)])
    zz = np.zeros((2 * l, 128), f32)
    zz[:, :z.shape[1]] = z[pos]
    zz[l] = 0.0
    return zz


def _dot_hi(a, b):
    a1 = a.astype(BF16)
    a2 = (a - a1.astype(F32)).astype(BF16)
    b1 = b.astype(BF16)
    b2 = (b - b1.astype(F32)).astype(BF16)
    return _dot(a1, b1) + _dot(a1, b2) + _dot(a2, b1)


def _filter_kernel(zz_ref, w1_ref, b1_ref, fr_ref, w2_ref, b2_ref, w3f_ref, w3b_ref, dl_ref,
                   o_ref, hdn_ref):
    n = zz_ref.shape[0]
    l = n // 2

    @pl.when((pl.program_id(0) == 0) & (pl.program_id(1) == 0))
    def _():
        h1 = jnp.sin(fr_ref[0:1, :] * (_dot_hi(zz_ref[...], w1_ref[...]) + b1_ref[...]))
        hdn_ref[...] = jnp.sin(fr_ref[1:2, :] * (_dot_hi(h1, w2_ref[...]) + b2_ref[...]))

    filt = jnp.concatenate([_dot_hi(hdn_ref[0:l, :], w3f_ref[...]),
                            _dot_hi(hdn_ref[l:n, :], w3b_ref[...])], axis=0)
    decay = jnp.exp(-zz_ref[:, 0:1] * dl_ref[...])
    rowi = lax.broadcasted_iota(jnp.int32, filt.shape, 0)
    kern = jnp.where(rowi == l, 0.0, filt * decay)
    o_ref[...] = (kern / jnp.sum(jnp.abs(kern), axis=0, keepdims=True)).astype(o_ref.dtype)


def _hyena_filters(l, w1, b1, freq, w2, b2, w3):
    n = 2 * l
    tn = 256
    ff = w2.shape[0]
    cpo = HY_DIM // tn
    deltas = np.abs(np.linspace(math.log(HY_DECAY_TARGET) / HY_DECAY_PCT_LONG,
                                math.log(HY_DECAY_TARGET) / HY_DECAY_PCT_SHORT, HY_DIM,
                                dtype=np.float32)).reshape(1, HY_DIM)
    const = lambda o, c: (0, 0)
    return pl.pallas_call(
        _filter_kernel,
        grid=(HY_ORDER, cpo),
        in_specs=[pl.BlockSpec((n, 128), const), pl.BlockSpec((128, ff), const),
                  pl.BlockSpec((1, ff), const), pl.BlockSpec((2, ff), const),
                  pl.BlockSpec((ff, ff), const), pl.BlockSpec((1, ff), const),
                  pl.BlockSpec((ff, tn), lambda o, c: (0, o * 2 * cpo + c)),
                  pl.BlockSpec((ff, tn), lambda o, c: (0, o * 2 * cpo + cpo + c)),
                  pl.BlockSpec((1, tn), lambda o, c: (0, c))],
        out_specs=pl.BlockSpec((n, tn), lambda o, c: (0, o * cpo + c)),
        out_shape=jax.ShapeDtypeStruct((n, HY_ORDER * HY_DIM), BF16),
        scratch_shapes=[pltpu.VMEM((n, ff), F32)],
        compiler_params=_cparams(("arbitrary", "arbitrary")),
        name="hyena_filter_mlp",
    )(jnp.asarray(_tap_features(l)), jnp.pad(w1, ((0, 128 - w1.shape[0]), (0, 0))),
      b1.reshape(1, ff), freq, w2, b2.reshape(1, ff), w3, w3, jnp.asarray(deltas))


def _hyena(p_arr, lp, dft, batch, grid):
    l = p_arr.shape[0] // batch
    fwd, inv = dft
    uc = _short_conv(p_arr, lp['hy_conv_w'].reshape(9, 3 * HY_DIM), lp['hy_conv_b'], batch, grid)
    kern = _hyena_filters(l, lp['hy_ff_w1'], lp['hy_ff_b1'], lp['hy_ff_freq'],
                          lp['hy_ff_w2'], lp['hy_ff_b2'], lp['hy_ff_w3'])
    nb = l // inv.shape[0]
    kf = _filter_spectrum(fwd, kern, nb)
    z_arr, z_col = uc, 0
    for n in range(HY_ORDER):
        y = _dft_fwd(fwd, z_arr, z_col, kf, n, batch, nb)
        z_arr = _dft_inv(inv, y, uc, n + 1, z_arr, z_col, lp['hy_skip'][n], batch, nb)
        z_col = 0
    return z_arr


def _merge_kernel(yhy_ref, ohg_ref, hgg_ref, yrt_ref, b0_ref, b1_ref, b2_ref, x_ref, mod_ref,
                  nw_ref, phy_ref, phg_ref, prt_ref, wo_ref, lng_ref, lnb_ref, o_ref):
    o = ohg_ref[...]
    yhg = (o * lax.rsqrt(jnp.mean(o * o, axis=-1, keepdims=True) + LN_EPS) * nw_ref[...]
           * _silu(hgg_ref[...].astype(F32)))
    m = (jax.nn.sigmoid(b0_ref[...].astype(F32)) * _dot(yhy_ref[...], phy_ref[...])
         + jax.nn.sigmoid(b1_ref[...].astype(F32)) * _dot(yhg.astype(BF16), phg_ref[...])
         + jax.nn.sigmoid(b2_ref[...].astype(F32)) * _dot(yrt_ref[...], prt_ref[...]))
    t = _dot(m.astype(BF16), wo_ref[...])
    gt1 = mod_ref[0][2:3]
    o_ref[...] = _layer_norm_rows(DN_ALPHA * x_ref[...] + gt1 * t, lng_ref[...], lnb_ref[...])


def _merge(y_hy, o_hg, y_rt, p_arr, x, modp, lp, wbf, rows_per_batch):
    m = x.shape[0]
    tm = min(512, m)
    d = D_MODEL
    mi = _mod_index(rows_per_batch, tm)
    row = lambda i: (i, 0)
    const = lambda i: (0, 0)
    gcol = COL_HG * 128 // d
    bcol = COL_BR * 128 // d
    return pl.pallas_call(
        _merge_kernel,
        grid=(m // tm,),
        in_specs=[pl.BlockSpec((tm, d), row), pl.BlockSpec((tm, d), row),
                  pl.BlockSpec((tm, d), lambda i: (i, gcol)),
                  pl.BlockSpec((tm, 2 * d), row),
                  pl.BlockSpec((tm, d), lambda i: (i, bcol)),
                  pl.BlockSpec((tm, d), lambda i: (i, bcol + 1)),
                  pl.BlockSpec((tm, d), lambda i: (i, bcol + 2)),
                  pl.BlockSpec((tm, d), row),
                  pl.BlockSpec((1, 8, d), lambda i: (mi(i), 0, 0)),
                  pl.BlockSpec((1, d), const),
                  pl.BlockSpec((d, d), const), pl.BlockSpec((d, d), const),
                  pl.BlockSpec((2 * d, d), const), pl.BlockSpec((d, d), const),
                  pl.BlockSpec((1, d), const), pl.BlockSpec((1, d), const)],
        out_specs=pl.BlockSpec((tm, d), row),
        out_shape=jax.ShapeDtypeStruct((m, d), F32),
        compiler_params=_cparams(("arbitrary",)),
        name="merge_out_ln",
    )(y_hy, o_hg, p_arr, y_rt, p_arr, p_arr, p_arr, x, modp, lp['hg_norm_w'].reshape(1, d),
      wbf['p_hy'], wbf['p_hg'], wbf['p_rt'], wbf['w_o'],
      lp['ln1_g'].reshape(1, d), lp['ln1_b'].reshape(1, d))


def _ffn_kernel(x_ref, mod_ref, w1_ref, w3_ref, w2_ref, lng_ref, lnb_ref, o_ref, h_ref, acc_ref):
    k = pl.program_id(1)
    m = mod_ref[0]

    @pl.when(k == 0)
    def _():
        h_ref[...] = (x_ref[...] * (1.0 + m[4:5]) + m[3:4]).astype(BF16)
        acc_ref[...] = jnp.zeros(acc_ref.shape, F32)

    h = h_ref[...]
    u = _silu(_dot(h, w1_ref[...])) * _dot(h, w3_ref[...])
    acc_ref[...] += _dot(u.astype(BF16), w2_ref[...])

    @pl.when(k == pl.num_programs(1) - 1)
    def _():
        o_ref[...] = _layer_norm_rows(DN_ALPHA * x_ref[...] + m[5:6] * acc_ref[...],
                                      lng_ref[...], lnb_ref[...])


def _ffn_dense(x, modp, w1, w3, w2, ln_g, ln_b, rows_per_batch):
    m = x.shape[0]
    d = D_MODEL
    dff = w1.shape[1]
    tm = min(1024, rows_per_batch or m)
    tf = dff // 2
    mi = _mod_index(rows_per_batch, tm)
    return pl.pallas_call(
        _ffn_kernel,
        grid=(m // tm, dff // tf),
        in_specs=[pl.BlockSpec((tm, d), lambda i, k: (i, 0)),
                  pl.BlockSpec((1, 8, d), lambda i, k: (mi(i), 0, 0)),
                  pl.BlockSpec((d, tf), lambda i, k: (0, k)),
                  pl.BlockSpec((d, tf), lambda i, k: (0, k)),
                  pl.BlockSpec((tf, d), lambda i, k: (k, 0)),
                  pl.BlockSpec((1, d), lambda i, k: (0, 0)),
                  pl.BlockSpec((1, d), lambda i, k: (0, 0))],
        out_specs=pl.BlockSpec((tm, d), lambda i, k: (i, 0)),
        out_shape=jax.ShapeDtypeStruct((m, d), F32),
        scratch_shapes=[pltpu.VMEM((tm, d), BF16), pltpu.VMEM((tm, d), F32)],
        compiler_params=_cparams(("arbitrary", "arbitrary")),
        name="ffn_dense_ln",
    )(x, modp, w1, w3, w2, ln_g.reshape(1, d), ln_b.reshape(1, d))


def _router_kernel(x_ref, mod_ref, r_ref, h_ref, lg_ref):
    m = mod_ref[0]
    h = x_ref[...] * (1.0 + m[4:5]) + m[3:4]
    h_ref[...] = h
    a1, a2, a3 = _split3(h)
    r1, r2, r3 = _split3(r_ref[...])
    lg_ref[...] = (_dot(a1, r1) + _dot(a1, r2) + _dot(a2, r1)
                   + _dot(a2, r2) + _dot(a1, r3) + _dot(a3, r1))


def _router(x, modp, router_pad, rows_per_batch):
    m = x.shape[0]
    d = D_MODEL
    tm = min(512, m)
    mi = _mod_index(rows_per_batch, tm)
    return pl.pallas_call(
        _router_kernel,
        grid=(m // tm,),
        in_specs=[pl.BlockSpec((tm, d), lambda i: (i, 0)),
                  pl.BlockSpec((1, 8, d), lambda i: (mi(i), 0, 0)),
                  pl.BlockSpec((d, 128), lambda i: (0, 0))],
        out_specs=[pl.BlockSpec((tm, d), lambda i: (i, 0)),
                   pl.BlockSpec((tm, 128), lambda i: (i, 0))],
        out_shape=[jax.ShapeDtypeStruct((m, d), F32), jax.ShapeDtypeStruct((m, 128), F32)],
        compiler_params=_cparams(("arbitrary",)),
        name="moe_router",
    )(x, modp, router_pad)


def _moe_ffn_kernel(be_ref, nu_ref, xp_ref, w1_ref, w3_ref, w2_ref, o_ref, x_ref, acc_ref):
    j = pl.program_id(0)
    k = pl.program_id(1)

    @pl.when(j < nu_ref[0])
    def _():
        @pl.when(k == 0)
        def _():
            x_ref[...] = xp_ref[...].astype(BF16)
            acc_ref[...] = jnp.zeros(acc_ref.shape, F32)

        x = x_ref[...]
        u = _silu(_dot(x, w1_ref[0])) * _dot(x, w3_ref[0])
        acc_ref[...] += _dot(u.astype(BF16), w2_ref[0])

        @pl.when(k == pl.num_programs(1) - 1)
        def _():
            o_ref[...] = acc_ref[...]

    @pl.when((j >= nu_ref[0]) & (k == pl.num_programs(1) - 1))
    def _():
        o_ref[...] = jnp.zeros(o_ref.shape, F32)


def _moe_ffn(xb, block_e, n_used, w1, w3, w2):
    ns = xb.shape[0]
    d = D_MODEL
    tm = MOE_ROWS
    dex = w1.shape[2]
    tf = dex // 2
    grid_spec = pltpu.PrefetchScalarGridSpec(
        num_scalar_prefetch=2,
        grid=(ns // tm, dex // tf),
        in_specs=[pl.BlockSpec((tm, d), lambda j, k, be, nu: (j, 0)),
                  pl.BlockSpec((1, d, tf), lambda j, k, be, nu: (be[j], 0, k)),
                  pl.BlockSpec((1, d, tf), lambda j, k, be, nu: (be[j], 0, k)),
                  pl.BlockSpec((1, tf, d), lambda j, k, be, nu: (be[j], k, 0))],
        out_specs=pl.BlockSpec((tm, d), lambda j, k, be, nu: (j, 0)),
        scratch_shapes=[pltpu.VMEM((tm, d), BF16), pltpu.VMEM((tm, d), F32)])
    return pl.pallas_call(
        _moe_ffn_kernel,
        grid_spec=grid_spec,
        out_shape=jax.ShapeDtypeStruct((ns, d), F32),
        compiler_params=_cparams(("arbitrary", "arbitrary")),
        name="moe_expert_ffn",
    )(block_e, n_used, xb, w1, w3, w2)


def _combine_kernel(x_ref, mod_ref, y0_ref, y1_ref, g_ref, lng_ref, lnb_ref, o_ref):
    m = mod_ref[0]
    g = g_ref[...]
    f = g[:, 0:1] * y0_ref[...] + g[:, 1:2] * y1_ref[...]
    o_ref[...] = _layer_norm_rows(DN_ALPHA * x_ref[...] + m[5:6] * f, lng_ref[...], lnb_ref[...])


def _moe_combine(x, modp, y0, y1, gate_pad, ln_g, ln_b, rows_per_batch):
    m = x.shape[0]
    d = D_MODEL
    tm = min(512, m)
    mi = _mod_index(rows_per_batch, tm)
    row = lambda i: (i, 0)
    return pl.pallas_call(
        _combine_kernel,
        grid=(m // tm,),
        in_specs=[pl.BlockSpec((tm, d), row),
                  pl.BlockSpec((1, 8, d), lambda i: (mi(i), 0, 0)),
                  pl.BlockSpec((tm, d), row), pl.BlockSpec((tm, d), row),
                  pl.BlockSpec((tm, 128), row),
                  pl.BlockSpec((1, d), lambda i: (0, 0)), pl.BlockSpec((1, d), lambda i: (0, 0))],
        out_specs=pl.BlockSpec((tm, d), row),
        out_shape=jax.ShapeDtypeStruct((m, d), F32),
        compiler_params=_cparams(("arbitrary",)),
        name="moe_combine_ln",
    )(x, modp, y0, y1, gate_pad, ln_g.reshape(1, d), ln_b.reshape(1, d))


def _cast_kernel(x_ref, o_ref):
    o_ref[...] = x_ref[...].astype(o_ref.dtype)


def _to_bf16(w, group):
    _, e, a, b = w.shape
    ta, tb = (a // 4, b) if a >= b else (a, b // 4)
    cut_rows = a >= b
    return pl.pallas_call(
        _cast_kernel,
        grid=(e, 4),
        in_specs=[pl.BlockSpec((None, 1, ta, tb),
                               lambda i, r: (group, i, r, 0) if cut_rows else (group, i, 0, r))],
        out_specs=pl.BlockSpec((1, ta, tb), lambda i, r: (i, r, 0) if cut_rows else (i, 0, r)),
        out_shape=jax.ShapeDtypeStruct((e, a, b), BF16),
        compiler_params=_cparams(("arbitrary", "arbitrary")),
        name="weight_cast",
    )(w)


def _moe(x, modp, router, w1, w3, w2, ln_g, ln_b, rows_per_batch):
    n, d = x.shape
    e = router.shape[1]
    h2, logits = _router(x, modp, jnp.pad(router, ((0, 0), (0, 128 - e))), rows_per_batch)
    top_val, top_idx = lax.top_k(logits[:, :e], TOP_K)
    gate = jax.nn.softmax(top_val, axis=-1)
    flat_e = top_idx.reshape(-1)
    flat_t = jnp.repeat(jnp.arange(n, dtype=jnp.int32), TOP_K)
    order = jnp.argsort(flat_e, stable=True).astype(jnp.int32)
    rank = jnp.argsort(order).astype(jnp.int32)
    counts = jnp.sum((flat_e[:, None] == jnp.arange(e)[None, :]).astype(jnp.int32), axis=0)
    starts = jnp.cumsum(counts) - counts
    padded = (counts + MOE_ROWS - 1) // MOE_ROWS * MOE_ROWS
    pad_end = jnp.cumsum(padded)
    pad_start = pad_end - padded
    n_blocks = -(-(n * TOP_K) // MOE_ROWS) + e
    n_slots = n_blocks * MOE_ROWS
    block_start = jnp.arange(n_blocks) * MOE_ROWS
    block_e = jnp.minimum(jnp.sum(block_start[:, None] >= pad_end[None, :], axis=1), e - 1).astype(jnp.int32)
    n_used = (pad_end[-1] // MOE_ROWS).astype(jnp.int32).reshape(1)
    slot_e = jnp.repeat(block_e, MOE_ROWS)
    slot_off = jnp.arange(n_slots, dtype=jnp.int32) - pad_start[slot_e]
    slot_valid = slot_off < counts[slot_e]
    slot_src = jnp.where(slot_valid, starts[slot_e] + slot_off, 0)
    slot_tok = jnp.where(slot_valid, flat_t[order[slot_src]], 0).astype(jnp.int32)
    pos = (pad_start[flat_e] + rank - starts[flat_e]).astype(jnp.int32).reshape(n, TOP_K)
    xb = jnp.take(h2, slot_tok, axis=0, mode="clip")
    yb = _moe_ffn(xb, block_e, n_used, w1, w3, w2)
    y0 = jnp.take(yb, pos[:, 0], axis=0, mode="clip")
    y1 = jnp.take(yb, pos[:, 1], axis=0, mode="clip")
    gate_pad = jnp.pad(gate, ((0, 0), (0, 128 - TOP_K)))
    return _moe_combine(x, modp, y0, y1, gate_pad, ln_g, ln_b, rows_per_batch)


def kernel(x, c, ctx, c_ctx, ada_w, ada_b, w_in, hy_conv_w, hy_conv_b, hy_ff_w1, hy_ff_b1, hy_ff_freq, hy_ff_w2, hy_ff_b2, hy_ff_w3, hy_skip, hg_lb_logits, hg_norm_w, p_hy, p_hg, p_rt, w_o, ln1_g, ln1_b, ln2_g, ln2_b, ffn_w1, ffn_w3, ffn_w2, moe_router, moe_w1, moe_w3, moe_w2):
    batch, l, d = x.shape
    lc = ctx.shape[1]
    assert d == D_MODEL and batch <= 8
    assert l % 512 == 0 and l % GRID_W == 0 and lc % max(RET_CHUNK, DFT_HALF) == 0

    cs = jnp.cumsum(jax.nn.softmax(hg_lb_logits.astype(F32), axis=1), axis=1)
    lower_bounds = cs - cs[:, :1]
    cc = jnp.zeros((16, d), F32).at[:batch].set(c).at[8].set(c_ctx)
    ret_tables = _retention_tables(lc, l)
    dft_lat = _dft_matrices(min(HY_BLOCK, l))
    dft_ctx = _dft_matrices(min(HY_BLOCK, lc))

    x_lat = x.reshape(batch * l, d)
    x_ctx = ctx.reshape(batch * lc, d)
    for i in range(DEPTH):
        need_ctx_out = i < DEPTH - 1
        use_moe = i % 2 == 1
        g = i // 2
        lp = {'hy_conv_w': hy_conv_w[i], 'hy_conv_b': hy_conv_b[i], 'hy_ff_w1': hy_ff_w1[i],
              'hy_ff_b1': hy_ff_b1[i], 'hy_ff_freq': hy_ff_freq[i], 'hy_ff_w2': hy_ff_w2[i],
              'hy_ff_b2': hy_ff_b2[i], 'hy_ff_w3': hy_ff_w3[i], 'hy_skip': hy_skip[i],
              'hg_norm_w': hg_norm_w[i], 'ln1_g': ln1_g[i], 'ln1_b': ln1_b[i]}
        wbf = {'p_hy': p_hy[i].astype(BF16), 'p_hg': p_hg[i].astype(BF16),
               'p_rt': p_rt[i].astype(BF16), 'w_o': w_o[i].astype(BF16)}
        mod = _ada(cc, ada_w, i, ada_b[i])
        modp = jnp.pad(mod.reshape(16, 6, d), ((0, 0), (0, 2), (0, 0)))
        lb = lower_bounds[:, i]
        gl_tab = jnp.pad(jnp.stack([lb, 1.0 - lb], axis=1), ((0, 0), (0, 6), (0, 0)))

        p_lat = _proj(x_lat, modp, w_in, i, N_IN_COLS, l)
        p_ctx = _proj(x_ctx, modp, w_in, i, N_IN_COLS if need_ctx_out else N_STATE_COLS, None)

        o_hg_l, o_hg_c = _gla(p_lat, p_ctx, gl_tab, batch, need_ctx_out)
        y_rt_l, y_rt_c = _retention(p_lat, p_ctx, ret_tables, batch, need_ctx_out)
        y_hy_l = _hyena(p_lat, lp, dft_lat, batch, True)
        x_lat_new = _merge(y_hy_l, o_hg_l, y_rt_l, p_lat, x_lat, modp, lp, wbf, l)
        if need_ctx_out:
            y_hy_c = _hyena(p_ctx, lp, dft_ctx, batch, False)
            x_ctx = _merge(y_hy_c, o_hg_c, y_rt_c, p_ctx, x_ctx, modp, lp, wbf, None)
        x_lat = x_lat_new

        if use_moe:
            if need_ctx_out:
                raise NotImplementedError("MoE layer with a context output is not part of this trunk")
            w1, w3, w2 = _to_bf16(moe_w1, g), _to_bf16(moe_w3, g), _to_bf16(moe_w2, g)
            x_lat = _moe(x_lat, modp, moe_router[g], w1, w3, w2, ln2_g[i], ln2_b[i], l)
        else:
            w1, w3, w2 = ffn_w1[g].astype(BF16), ffn_w3[g].astype(BF16), ffn_w2[g].astype(BF16)
            x_lat = _ffn_dense(x_lat, modp, w1, w3, w2, ln2_g[i], ln2_b[i], l)
            if need_ctx_out:
                x_ctx = _ffn_dense(x_ctx, modp, w1, w3, w2, ln2_g[i], ln2_b[i], None)
    return x_lat.reshape(batch, l, d)
```

```python
import functools
import math

import numpy as np
import jax
import jax.numpy as jnp
from jax import lax
from jax.experimental import pallas as pl
from jax.experimental.pallas import tpu as pltpu

F32 = jnp.float32
BF16 = jnp.bfloat16

D_MODEL = 1024
DEPTH = 2
GRID_W = 64
HY_DIM = 1024
HY_ORDER = 2
HY_BANDS = 16
HY_DECAY_TARGET = 1e-2
HY_DECAY_PCT_SHORT = 0.3
HY_DECAY_PCT_LONG = 1.5
HG_HEADS = 8
HG_HK = 128
RT_HEADS = 4
RT_HK = 256
RT_HV = 512
RT_ROPE_BASE = 10000.0
N_STATE_COLS = 6144
N_IN_COLS = 17408
N_EXPERTS = 8
TOP_K = 2
DN_ALPHA = (2 * DEPTH) ** 0.25
LN_EPS = 1e-5

COL_FF, COL_FB, COL_HI, COL_RK, COL_RV = 0, 8, 16, 24, 32
COL_HQ, COL_HG, COL_RQ, COL_RG, COL_HY, COL_BR = 48, 56, 64, 72, 88, 112

GLA_CHUNK = 256
GLA_LEVELS = 8
GLA_PAD = 8
GLA_MIN_LOG2 = -1e30
RET_CHUNK = 256
DFT_HALF = 256
HY_BLOCK = 512
CONV_ROWS = 256
MOE_ROWS = 512
VMEM_LIMIT = 56 * 1024 * 1024


def _cparams(sem, flags=None):
    return pltpu.CompilerParams(dimension_semantics=sem, vmem_limit_bytes=VMEM_LIMIT, flags=flags)


def _silu(x):
    return x * jax.nn.sigmoid(x)


def _layer_norm_rows(r, g, b):
    mu = jnp.mean(r, axis=-1, keepdims=True)
    d = r - mu
    var = jnp.mean(d * d, axis=-1, keepdims=True)
    return d * lax.rsqrt(var + LN_EPS) * g + b


def _split3(x):
    h = x.astype(BF16)
    r = x - h.astype(F32)
    m = r.astype(BF16)
    l = (r - m.astype(F32)).astype(BF16)
    return h, m, l


def _dot(a, b):
    return jnp.dot(a, b, preferred_element_type=F32)


def _dot_nt(a, b):
    return lax.dot_general(a, b, (((1,), (1,)), ((), ())), preferred_element_type=F32)


def _dot_tn(a, b):
    return lax.dot_general(a, b, (((0,), (0,)), ((), ())), preferred_element_type=F32)


def _ada_kernel(c_ref, w_ref, b_ref, o_ref):
    s = _silu(c_ref[...])
    o_ref[...] = _dot(s.astype(BF16), w_ref[...].astype(BF16)) + b_ref[...]


def _ada(cc, w, layer, b):
    n = w.shape[2]
    tn = 1024
    return pl.pallas_call(
        _ada_kernel,
        grid=(n // tn,),
        in_specs=[pl.BlockSpec((16, D_MODEL), lambda j: (0, 0)),
                  pl.BlockSpec((None, D_MODEL, tn), lambda j: (layer, 0, j)),
                  pl.BlockSpec((1, tn), lambda j: (0, j))],
        out_specs=pl.BlockSpec((16, tn), lambda j: (0, j)),
        out_shape=jax.ShapeDtypeStruct((16, n), F32),
        compiler_params=_cparams(("arbitrary",)),
        name="ada_mod",
    )(cc, w, b.reshape(1, n))


def _mod_index(rows_per_batch, tm):
    if rows_per_batch is None:
        return lambda i: 8
    return lambda i: (i * tm) // rows_per_batch


def _proj_kernel(x_ref, mod_ref, w_ref, o_ref, h_ref):
    @pl.when(pl.program_id(1) == 0)
    def _():
        m = mod_ref[0]
        h_ref[...] = (x_ref[...] * (1.0 + m[1:2]) + m[0:1]).astype(BF16)

    o_ref[...] = _dot(h_ref[...], w_ref[...].astype(BF16)).astype(o_ref.dtype)


def _proj(x, modp, w, layer, n_cols, rows_per_batch):
    m = x.shape[0]
    tm = min(2048, rows_per_batch or m)
    tn = 1024
    mi = _mod_index(rows_per_batch, tm)
    return pl.pallas_call(
        _proj_kernel,
        grid=(m // tm, n_cols // tn),
        in_specs=[pl.BlockSpec((tm, D_MODEL), lambda i, j: (i, 0)),
                  pl.BlockSpec((1, 8, D_MODEL), lambda i, j: (mi(i), 0, 0)),
                  pl.BlockSpec((None, D_MODEL, tn), lambda i, j: (layer, 0, j))],
        out_specs=pl.BlockSpec((tm, tn), lambda i, j: (i, j)),
        out_shape=jax.ShapeDtypeStruct((m, n_cols), BF16),
        scratch_shapes=[pltpu.VMEM((tm, D_MODEL), BF16)],
        compiler_params=_cparams(("arbitrary", "arbitrary")),
        name="in_proj",
    )(x, modp, w)


def _gla_gates(fl, gl_ref, d):
    lb = gl_ref[d, 0:1, :]
    oml = gl_ref[d, 1:2, :]
    t = jnp.exp(-jnp.abs(fl))
    r = 1.0 / (1.0 + t)
    tr = t * r
    pos = fl >= 0.0
    f = lb + oml * jnp.where(pos, r, tr)
    key = oml * jnp.where(pos, tr, r)
    return jnp.maximum(jnp.log2(f), GLA_MIN_LOG2), key


def _gla_cumsum(g, rev):
    c = g.shape[0]
    r = lax.broadcasted_iota(jnp.int32, (c, c), 0)
    u = lax.broadcasted_iota(jnp.int32, (c, c), 1)
    tri = jnp.where((u >= r) if rev else (u <= r), 1.0, 0.0).astype(BF16)
    return _dot(jnp.concatenate([tri, tri, tri], axis=1), jnp.concatenate(_split3(g), axis=0))


def _gla_state_update(st, b, kk, vv, rev):
    c = b.shape[0]
    end = 0 if rev else c - 1
    b_end = b[end:end + 1, :]
    khat = (kk * jnp.exp2(b_end - b)).astype(BF16)
    return st * jnp.exp2(b_end) + _dot_tn(vv, khat)


def _gla_level_table():
    c = GLA_CHUNK
    t = np.arange(c)[:, None]
    s = np.arange(c)[None, :]
    top_bit = np.floor(np.log2(np.maximum(t ^ s, 1))).astype(np.int32)
    fwd = np.where(t > s, top_bit, np.where(t == s, GLA_LEVELS, -1))
    return np.stack([fwd, fwd.T]).astype(np.int32)


def _gla_ref_rows(b, bpad_ref, level, rev):
    c = b.shape[0]
    half = 1 << level
    blk = 2 * half
    idx = half if rev else half - 1
    if blk % 8 == 0:
        r = b.reshape(c // blk, blk, HG_HK)[:, idx:idx + 1, :]
        return jnp.broadcast_to(r, (c // blk, blk, HG_HK)).reshape(c, HG_HK)
    m = lax.broadcasted_iota(jnp.int32, (c, HG_HK), 0) % blk
    out = b
    for v in range(blk):
        if v != idx:
            off = GLA_PAD + idx - v
            out = jnp.where(m == v, bpad_ref[off:off + c, :], out)
    return out


def _gla_chunk_out(b, kk, vv, qq, st, rev, lvt, bpad_ref):
    c = GLA_CHUNK
    o = _dot_nt((qq * jnp.exp2(b)).astype(BF16), st.astype(BF16))
    bpad_ref[GLA_PAD:GLA_PAD + c, :] = b
    qq_b, kk_b = qq.astype(BF16), kk.astype(BF16)
    sc = jnp.zeros((c, c), F32)
    for level in range(GLA_LEVELS):
        half = 1 << level
        blk = 2 * half
        ref = _gla_ref_rows(b, bpad_ref, level, rev)
        kt = kk_b * jnp.exp2(ref - b).astype(BF16)
        if half % 8:
            p = _dot_nt(qq_b * jnp.exp2(b - ref).astype(BF16), kt)
            sc = jnp.where(lvt == level, p, sc)
        else:
            nblk = c // blk
            lo, hi = (0, half) if rev else (half, blk)

            def part(x):
                return x.reshape(nblk, blk, x.shape[-1])[:, lo:hi, :]

            qt = (part(qq) * jnp.exp2(part(b) - part(ref))).reshape(c // 2, HG_HK)
            p = _dot_nt(qt.astype(BF16), kt).reshape(nblk, half, c)
            new = jnp.where(part(lvt) == level, p, part(sc))
            rest = sc.reshape(nblk, blk, c)[:, half:blk, :] if rev else sc.reshape(nblk, blk, c)[:, 0:half, :]
            sc = jnp.concatenate([new, rest] if rev else [rest, new], axis=1).reshape(c, c)
    sc = jnp.where(lvt == GLA_LEVELS, _dot_nt(qq_b, kk_b), sc)
    return o + _dot(sc.astype(BF16), vv)


def _gla_kernel(*refs, need_ctx_out, n_lat, n_ctx):
    if need_ctx_out:
        ffl, fbl, il, ql, ffc, fbc, ic, qc, gl_ref, lv_ref, ol_ref, oc_ref = refs[:12]
        bl, kl, bc, kc, qsl, qsc, bpf, bpr = refs[12:]
    else:
        ffl, fbl, il, ql, ffc, fbc, ic, gl_ref, lv_ref, ol_ref = refs[:10]
        bl, kl, bc, kc, qsl, bpf, bpr = refs[10:]
        qc = oc_ref = qsc = None
    c = GLA_CHUNK
    q_scale = HG_HK ** -0.5
    bpf[...] = jnp.zeros(bpf.shape, F32)
    bpr[...] = jnp.zeros(bpr.shape, F32)
    ol_ref[...] = jnp.zeros(ol_ref.shape, F32)
    if need_ctx_out:
        oc_ref[...] = jnp.zeros(oc_ref.shape, F32)

    def chunk_rows(idx):
        return pl.ds(pl.multiple_of(idx * c, c), c)

    def prepare(f_refs, q_ref, b_sc, k_sc, q_sc, n):
        def body(j, carry):
            rows = chunk_rows(j)
            for d in range(2):
                g, kk = _gla_gates(f_refs[d][rows, :].astype(F32), gl_ref, d)
                b_sc[d, rows, :] = _gla_cumsum(g, d == 1)
                k_sc[d, rows, :] = kk
            if q_ref is not None:
                q_sc[rows, :] = _silu(q_ref[rows, :].astype(F32)) * q_scale
            return carry
        lax.fori_loop(0, n, body, 0, unroll=min(2, n))

    prepare((ffc, fbc), qc, bc, kc, qsc, n_ctx)
    prepare((ffl, fbl), ql, bl, kl, qsl, n_lat)

    def one_dir(d, b_sc, k_sc, i_ref, q_sc, o_ref, idx, st):
        rev = d == 1
        rows = chunk_rows(idx)
        b, kk, vv = b_sc[d, rows, :], k_sc[d, rows, :], i_ref[rows, :]
        if q_sc is not None:
            o_ref[rows, :] += _gla_chunk_out(b, kk, vv, q_sc[rows, :], st, rev, lv_ref[d],
                                             bpr if rev else bpf)
        return _gla_state_update(st, b, kk, vv, rev)

    def ctx_body(j, carry):
        return (one_dir(0, bc, kc, ic, qsc, oc_ref, j, carry[0]),
                one_dir(1, bc, kc, ic, qsc, oc_ref, n_ctx - 1 - j, carry[1]))

    def lat_body(j, carry):
        return (one_dir(0, bl, kl, il, qsl, ol_ref, j, carry[0]),
                one_dir(1, bl, kl, il, qsl, ol_ref, n_lat - 1 - j, carry[1]))

    zero = jnp.zeros((HG_HK, HG_HK), F32)
    carry = lax.fori_loop(0, n_ctx, ctx_body, (zero, zero), unroll=min(2, n_ctx))
    lax.fori_loop(0, n_lat, lat_body, carry)


def _gla(p_lat, p_ctx, gl_tab, batch, need_ctx_out):
    l = p_lat.shape[0] // batch
    lc = p_ctx.shape[0] // batch
    w = HG_HK
    c = GLA_CHUNK

    def col(base):
        return lambda b, h: (b, base + h)

    lat_specs = [pl.BlockSpec((l, w), col(cb)) for cb in (COL_FF, COL_FB, COL_HI, COL_HQ)]
    ctx_cols = (COL_FF, COL_FB, COL_HI) + ((COL_HQ,) if need_ctx_out else ())
    ctx_specs = [pl.BlockSpec((lc, w), col(cb)) for cb in ctx_cols]
    in_specs = lat_specs + ctx_specs + [pl.BlockSpec((2, 8, w), lambda b, h: (0, 0, h)),
                                        pl.BlockSpec((2, c, c), lambda b, h: (0, 0, 0))]
    args = [p_lat] * 4 + [p_ctx] * len(ctx_cols) + [gl_tab, jnp.asarray(_gla_level_table())]
    out_specs = [pl.BlockSpec((l, w), lambda b, h: (b, h))]
    out_shape = [jax.ShapeDtypeStruct((batch * l, HG_HEADS * w), F32)]
    scratch = [pltpu.VMEM((2, l, w), F32), pltpu.VMEM((2, l, w), F32),
               pltpu.VMEM((2, lc, w), F32), pltpu.VMEM((2, lc, w), F32), pltpu.VMEM((l, w), F32)]
    if need_ctx_out:
        out_specs.append(pl.BlockSpec((lc, w), lambda b, h: (b, h)))
        out_shape.append(jax.ShapeDtypeStruct((batch * lc, HG_HEADS * w), F32))
        scratch.append(pltpu.VMEM((lc, w), F32))
    scratch += [pltpu.VMEM((c + 2 * GLA_PAD, w), F32)] * 2
    outs = pl.pallas_call(
        functools.partial(_gla_kernel, need_ctx_out=need_ctx_out, n_lat=l // c, n_ctx=lc // c),
        grid=(batch, HG_HEADS),
        in_specs=in_specs,
        out_specs=out_specs,
        out_shape=out_shape,
        scratch_shapes=scratch,
        compiler_params=_cparams(("arbitrary", "arbitrary")),
        name="hgrn2_scan",
    )(*args)
    return outs[0], (outs[1] if need_ctx_out else None)


def _rotate(x, cos, sin):
    half = x.shape[-1] // 2
    x1, x2 = x[:, :half], x[:, half:]
    return jnp.concatenate([x1 * cos - x2 * sin, x1 * sin + x2 * cos], axis=-1)


def _ret_kernel(*refs, need_ctx_out, n_lat, n_ctx):
    if need_ctx_out:
        (ql, kl, vl, gl, qc, kc, vc, gc, cos_ref, sin_ref, intra_ref, qk_ref, sd_ref,
         yl_ref, yc_ref, qrl, krl, qrc, krc, ol, oc, s_fwd, s_rev) = refs
    else:
        (ql, kl, vl, gl, kc, vc, cos_ref, sin_ref, intra_ref, qk_ref, sd_ref,
         yl_ref, qrl, krl, krc, ol, s_fwd, s_rev) = refs
        qc = gc = yc_ref = qrc = oc = None
    s_refs = (s_fwd, s_rev)
    c = RET_CHUNK
    lc = n_ctx * c
    k_scale = RT_HK ** -0.5

    def rot_all(src, dst, n, pos0, scale):
        def body(j, carry):
            rows = pl.ds(pl.multiple_of(j * c, c), c)
            prow = pl.ds(pl.multiple_of(pos0 + j * c, c), c)
            x = src[rows, :].astype(F32)
            dst[rows, :] = (_rotate(x, cos_ref[prow, :], sin_ref[prow, :]) * scale).astype(BF16)
            return carry
        lax.fori_loop(0, n, body, 0)

    rot_all(kl, krl, n_lat, lc, k_scale)
    rot_all(kc, krc, n_ctx, 0, k_scale)
    rot_all(ql, qrl, n_lat, lc, 1.0)
    if need_ctx_out:
        rot_all(qc, qrc, n_ctx, 0, 1.0)

    for s_ref in s_refs:
        s_ref[...] = jnp.zeros(s_ref.shape, F32)
    ol[...] = jnp.zeros(ol.shape, F32)
    if need_ctx_out:
        oc[...] = jnp.zeros(oc.shape, F32)

    def update(d, kr, v_ref, rows):
        kd = qk_ref[d, 0, c:2 * c, :]
        sdec = sd_ref[d, 0, 0:1, :]
        kh = (kr[rows, :].astype(F32) * jnp.concatenate([kd, kd], axis=1)).astype(BF16)
        s_ref = s_refs[d]
        s_ref[...] = (s_ref[...] * jnp.concatenate([sdec] * (RT_HV // 128), axis=1)
                      + _dot_tn(kh, v_ref[rows, :]))

    def out_step(d, qr, kr, v_ref, o_ref, rows):
        qd = qk_ref[d, 0, 0:c, :]
        qcb = qr[rows, :]
        sc = _dot_nt(qcb, kr[rows, :]) * intra_ref[d, 0]
        qh = (qcb.astype(F32) * jnp.concatenate([qd, qd], axis=1)).astype(BF16)
        o_ref[rows, :] += (_dot(qh, s_refs[d][...].astype(BF16))
                           + _dot(sc.astype(BF16), v_ref[rows, :]))
        update(d, kr, v_ref, rows)

    def ctx_body(j, carry):
        for d in range(2):
            idx = (n_ctx - 1 - j) if d == 1 else j
            rows = pl.ds(pl.multiple_of(idx * c, c), c)
            if need_ctx_out:
                out_step(d, qrc, krc, vc, oc, rows)
            else:
                update(d, krc, vc, rows)
        return carry

    def lat_body(j, carry):
        for d in range(2):
            idx = (n_lat - 1 - j) if d == 1 else j
            out_step(d, qrl, krl, vl, ol, pl.ds(pl.multiple_of(idx * c, c), c))
        return carry

    lax.fori_loop(0, n_ctx, ctx_body, 0)
    lax.fori_loop(0, n_lat, lat_body, 0, unroll=4)

    def readout(o_ref, g_ref, y_ref, n):
        def body(j, carry):
            rows = pl.ds(pl.multiple_of(j * c, c), c)
            o = o_ref[rows, :]
            y = o * lax.rsqrt(jnp.mean(o * o, axis=-1, keepdims=True) + LN_EPS)
            y_ref[rows, :] = (y * _silu(g_ref[rows, :].astype(F32))).astype(y_ref.dtype)
            return carry
        lax.fori_loop(0, n, body, 0)

    readout(ol, gl, yl_ref, n_lat)
    if need_ctx_out:
        readout(oc, gc, yc_ref, n_ctx)


def _retention_tables(lc, l):
    half = RT_HK // 2
    inv = 1.0 / (RT_ROPE_BASE ** jnp.linspace(0.0, 1.0, half, dtype=F32))
    ang = jnp.arange(lc + l, dtype=F32)[:, None] * inv[None, :]
    j = jnp.arange(2 * RT_HEADS, dtype=F32)
    lg_all = jnp.log1p(-jnp.exp2(-5.0 - j))
    c = RET_CHUNK
    pos = jnp.arange(c, dtype=F32)
    rel = pos[:, None] - pos[None, :]
    intra, qk, sd = [], [], []
    for d in range(2):
        lg = lg_all[d::2]
        m = jnp.where(rel >= 0, jnp.exp(jnp.maximum(rel, 0.0)[None] * lg[:, None, None]), 0.0)
        qdec = jnp.exp((pos + 1.0)[None, :] * lg[:, None])
        kdec = jnp.exp((c - 1.0 - pos)[None, :] * lg[:, None])
        if d == 1:
            m = jnp.swapaxes(m, 1, 2)
            qdec = qdec[:, ::-1]
            kdec = kdec[:, ::-1]
        intra.append(m)
        qk.append(jnp.broadcast_to(jnp.concatenate([qdec, kdec], axis=1)[:, :, None],
                                   (RT_HEADS, 2 * c, 128)))
        sd.append(jnp.broadcast_to(jnp.exp(c * lg)[:, None, None], (RT_HEADS, 8, 128)))
    return (jnp.cos(ang), jnp.sin(ang), jnp.stack(intra), jnp.stack(qk), jnp.stack(sd))


def _retention(p_lat, p_ctx, tables, batch, need_ctx_out):
    l = p_lat.shape[0] // batch
    lc = p_ctx.shape[0] // batch
    cos, sin, intra, qk, sd = tables
    c = RET_CHUNK

    def col(base):
        return lambda b, h: (b, base + h)

    kq, kk_, kv, kg = COL_RQ * 128 // RT_HK, COL_RK * 128 // RT_HK, COL_RV * 128 // RT_HV, COL_RG * 128 // RT_HV
    lat_specs = [pl.BlockSpec((l, RT_HK), col(kq)), pl.BlockSpec((l, RT_HK), col(kk_)),
                 pl.BlockSpec((l, RT_HV), col(kv)), pl.BlockSpec((l, RT_HV), col(kg))]
    if need_ctx_out:
        ctx_specs = [pl.BlockSpec((lc, RT_HK), col(kq)), pl.BlockSpec((lc, RT_HK), col(kk_)),
                     pl.BlockSpec((lc, RT_HV), col(kv)), pl.BlockSpec((lc, RT_HV), col(kg))]
    else:
        ctx_specs = [pl.BlockSpec((lc, RT_HK), col(kk_)), pl.BlockSpec((lc, RT_HV), col(kv))]
    tab_specs = [pl.BlockSpec((lc + l, 128), lambda b, h: (0, 0)),
                 pl.BlockSpec((lc + l, 128), lambda b, h: (0, 0)),
                 pl.BlockSpec((2, 1, c, c), lambda b, h: (0, h, 0, 0)),
                 pl.BlockSpec((2, 1, 2 * c, 128), lambda b, h: (0, h, 0, 0)),
                 pl.BlockSpec((2, 1, 8, 128), lambda b, h: (0, h, 0, 0))]
    args = [p_lat] * 4 + [p_ctx] * len(ctx_specs) + [cos, sin, intra, qk, sd]
    out_specs = [pl.BlockSpec((l, RT_HV), lambda b, h: (b, h))]
    out_shape = [jax.ShapeDtypeStruct((batch * l, RT_HEADS * RT_HV), BF16)]
    scratch = [pltpu.VMEM((l, RT_HK), BF16), pltpu.VMEM((l, RT_HK), BF16)]
    if need_ctx_out:
        out_specs.append(pl.BlockSpec((lc, RT_HV), lambda b, h: (b, h)))
        out_shape.append(jax.ShapeDtypeStruct((batch * lc, RT_HEADS * RT_HV), BF16))
        scratch += [pltpu.VMEM((lc, RT_HK), BF16), pltpu.VMEM((lc, RT_HK), BF16),
                    pltpu.VMEM((l, RT_HV), F32), pltpu.VMEM((lc, RT_HV), F32)]
    else:
        scratch += [pltpu.VMEM((lc, RT_HK), BF16), pltpu.VMEM((l, RT_HV), F32)]
    scratch += [pltpu.VMEM((RT_HK, RT_HV), F32)] * 2
    outs = pl.pallas_call(
        functools.partial(_ret_kernel, need_ctx_out=need_ctx_out, n_lat=l // c, n_ctx=lc // c),
        grid=(batch, RT_HEADS),
        in_specs=lat_specs + ctx_specs + tab_specs,
        out_specs=out_specs,
        out_shape=out_shape,
        scratch_shapes=scratch,
        compiler_params=_cparams(("arbitrary", "arbitrary")),
        name="retention_scan",
    )(*args)
    return outs[0], (outs[1] if need_ctx_out else None)


def _conv_shift_matrices(grid):
    rb = CONV_ROWS
    t = np.arange(rb)
    edge_lo = (t % GRID_W == 0) if grid else (t == 0)
    edge_hi = (t % GRID_W == GRID_W - 1) if grid else (t == rb - 1)
    s0 = ((t[:, None] - 1 == t[None, :]) & ~edge_lo[:, None]).astype(np.float32)
    s2 = ((t[:, None] + 1 == t[None, :]) & ~edge_hi[:, None]).astype(np.float32)
    return s0, s2


def _short_conv_kernel(x_ref, w_ref, b_ref, s0_ref, s2_ref, o_ref, xpad_ref, *, grid):
    l, tc = x_ref.shape
    p = GRID_W
    rb = CONV_ROWS
    zeros = jnp.zeros((p, tc), BF16)
    xpad_ref[0:p, :] = zeros
    xpad_ref[p + l:p + l + p, :] = zeros
    xpad_ref[p:p + l, :] = x_ref[...]
    w = [w_ref[k:k + 1, :].astype(BF16) for k in range(9)]
    rows = range(3) if grid else (1,)

    def body(i, carry):
        r0 = i * rb
        part = []
        for dj in range(3):
            acc = None
            for di in rows:
                start = pl.multiple_of(r0 + p + (di - 1) * GRID_W, GRID_W)
                t = xpad_ref[pl.ds(start, rb), :] * w[di * 3 + dj]
                acc = t if acc is None else acc + t
            part.append(acc)
        out = (part[1].astype(F32) + _dot(s0_ref[...], part[0]) + _dot(s2_ref[...], part[2])
               + b_ref[...])
        o_ref[pl.ds(pl.multiple_of(r0, rb), rb), :] = out.astype(o_ref.dtype)
        return carry

    lax.fori_loop(0, l // rb, body, 0, unroll=min(4, l // rb))


def _short_conv(p_arr, w9, bias, batch, grid):
    l = p_arr.shape[0] // batch
    assert l % CONV_ROWS == 0 and (grid or l == CONV_ROWS)
    c3 = 3 * HY_DIM
    tc = 256
    base = COL_HY * 128 // tc
    s0, s2 = _conv_shift_matrices(grid)
    return pl.pallas_call(
        functools.partial(_short_conv_kernel, grid=grid),
        grid=(batch, c3 // tc),
        in_specs=[pl.BlockSpec((l, tc), lambda b, j: (b, base + j)),
                  pl.BlockSpec((9, tc), lambda b, j: (0, j)),
                  pl.BlockSpec((1, tc), lambda b, j: (0, j)),
                  pl.BlockSpec((CONV_ROWS, CONV_ROWS), lambda b, j: (0, 0)),
                  pl.BlockSpec((CONV_ROWS, CONV_ROWS), lambda b, j: (0, 0))],
        out_specs=pl.BlockSpec((l, tc), lambda b, j: (b, j)),
        out_shape=jax.ShapeDtypeStruct((batch * l, c3), BF16),
        scratch_shapes=[pltpu.VMEM((l + 2 * GRID_W, tc), BF16)],
        compiler_params=_cparams(("arbitrary", "arbitrary")),
        name="hyena_short_conv",
    )(p_arr, w9, bias.reshape(1, c3), jnp.asarray(s0, BF16), jnp.asarray(s2, BF16))


def _dft_matrices(l):
    n = 2 * l
    s = int(round(math.sqrt(n)))
    while n % s:
        s -= 1
    q = n // s
    f = np.arange(l, dtype=np.int64)
    ang_a = 2.0 * np.pi * ((f[:, None] * s * np.arange(q)[None, :]) % n) / n
    ang_b = 2.0 * np.pi * ((f[:, None] * np.arange(s)[None, :]) % n) / n
    t = np.arange(n)
    rep = (t[None, :] // s == np.arange(q)[:, None]).astype(np.float32)
    til = (t[None, :] % s == np.arange(s)[:, None]).astype(np.float32)
    h = DFT_HALF
    row = lambda j: (j, 0)
    const = lambda j: (0, 0)
    fwd, inv = pl.pallas_call(
        _dft_build_kernel,
        grid=(l // h,),
        in_specs=[pl.BlockSpec((h, q), row), pl.BlockSpec((h, q), row),
                  pl.BlockSpec((h, s), row), pl.BlockSpec((h, s), row),
                  pl.BlockSpec((q, n), const), pl.BlockSpec((s, n), const)],
        out_specs=[pl.BlockSpec((1, 2, h, n), lambda j: (j, 0, 0, 0)),
                   pl.BlockSpec((h, n), row)],
        out_shape=[jax.ShapeDtypeStruct((l // h, 2, h, n), BF16),
                   jax.ShapeDtypeStruct((l, n), BF16)],
        compiler_params=_cparams(("arbitrary",)),
        name="dft_build",
    )(jnp.asarray(np.cos(ang_a), F32), jnp.asarray(np.sin(ang_a), F32),
      jnp.asarray(np.cos(ang_b), F32), jnp.asarray(np.sin(ang_b), F32),
      jnp.asarray(rep, BF16), jnp.asarray(til, BF16))
    return fwd.reshape(n, n), inv


def _dft_build_kernel(ca_ref, sa_ref, cb_ref, sb_ref, rep_ref, til_ref, fwd_ref, inv_ref):
    h, n = inv_ref.shape

    def spread(x_ref, m_ref):
        p1, p2, p3 = _split3(x_ref[...])
        m = m_ref[...]
        return _dot(p1, m) + _dot(p2, m) + _dot(p3, m)

    ca, sa = spread(ca_ref, rep_ref), spread(sa_ref, rep_ref)
    cb, sb = spread(cb_ref, til_ref), spread(sb_ref, til_ref)
    cosm = ca * cb - sa * sb
    nsin = -(sa * cb + ca * sb)
    first = pl.program_id(0) == 0
    rowi = lax.broadcasted_iota(jnp.int32, (h, n), 0)
    coli = lax.broadcasted_iota(jnp.int32, (h, n), 1)
    alt_col = (1 - 2 * (coli % 2)).astype(F32)
    fwd_ref[0, 0] = cosm.astype(BF16)
    fwd_ref[0, 1] = jnp.where(first & (rowi == 0), alt_col, nsin).astype(BF16)
    rowg = lax.broadcasted_iota(jnp.int32, (h, h), 0) + pl.program_id(0) * h
    col0 = lax.broadcasted_iota(jnp.int32, (h, h), 1) == 0
    alt_row = (1 - 2 * (rowg % 2)).astype(F32)
    pieces = []
    for jj in range(n // (2 * h)):
        cp = cosm[:, jj * h:(jj + 1) * h] * (2.0 / n)
        ip = nsin[:, jj * h:(jj + 1) * h] * (2.0 / n)
        if jj == 0:
            cp = jnp.where(col0, 1.0 / n, cp)
            ip = jnp.where(col0, alt_row * (1.0 / n), ip)
        pieces += [cp.astype(BF16), ip.astype(BF16)]
    inv_ref[...] = jnp.concatenate(pieces, axis=1)


def _filter_spectrum_kernel(f_ref, top_ref, bot_ref, o_ref):
    h = DFT_HALF
    bl = top_ref.shape[0]
    acc = _dot(f_ref[:, 0:bl], top_ref[...]) + _dot(f_ref[:, bl:2 * bl], bot_ref[...])
    kr, km = acc[:h], acc[h:]
    row0 = (lax.broadcasted_iota(jnp.int32, kr.shape, 0) == 0) & (pl.program_id(2) == 0)
    o_ref[0, 0:h, :] = kr.astype(o_ref.dtype)
    o_ref[0, h:2 * h, :] = jnp.where(row0, 0.0, km).astype(o_ref.dtype)
    o_ref[0, 2 * h:3 * h, :] = jnp.where(row0, km, kr).astype(o_ref.dtype)


def _filter_spectrum(fwd, kern_bf, nb):
    n = fwd.shape[0]
    bl = n // 2
    nc = kern_bf.shape[1]
    tm, tn = 2 * DFT_HALF, 1024
    nd = 2 * nb - 1
    wrap = 2 * nb
    return pl.pallas_call(
        _filter_spectrum_kernel,
        grid=(nc // tn, nd, n // tm),
        in_specs=[pl.BlockSpec((tm, n), lambda c, d, t: (t, 0)),
                  pl.BlockSpec((bl, tn), lambda c, d, t: ((d - (nb - 1)) % wrap, c)),
                  pl.BlockSpec((bl, tn), lambda c, d, t: ((d - nb) % wrap, c))],
        out_specs=pl.BlockSpec((1, 3 * DFT_HALF, tn), lambda c, d, t: (d, t, c)),
        out_shape=jax.ShapeDtypeStruct((nd, (n // tm) * 3 * DFT_HALF, nc), BF16),
        compiler_params=_cparams(("arbitrary", "arbitrary", "arbitrary")),
        name="hyena_filter_dft",
    )(fwd, kern_bf, kern_bf)


def _dft_fwd_kernel(f_ref, z_ref, k_ref, y_ref):
    h = DFT_HALF
    bl = f_ref.shape[1]
    nb = y_ref.shape[1]
    f = f_ref[...]
    spec = [_dot(f, z_ref[j * bl:(j + 1) * bl, :]).astype(BF16) for j in range(nb)]
    for i in range(nb):
        yr = yi = None
        for j in range(nb):
            d = i - j + nb - 1
            kr, ki, kr2 = k_ref[d, 0:h, :], k_ref[d, h:2 * h, :], k_ref[d, 2 * h:3 * h, :]
            ur, ui = spec[j][:h], spec[j][h:]
            tr = ur * kr - ui * ki
            ti = ur * ki + ui * kr2
            yr = tr if yr is None else yr + tr
            yi = ti if yi is None else yi + ti
        y_ref[0, i, 0:h, :] = yr.astype(y_ref.dtype)
        y_ref[0, i, h:2 * h, :] = yi.astype(y_ref.dtype)


def _dft_fwd(fwd, z_arr, z_col, kf, k_col, batch, nb):
    n = fwd.shape[0]
    bl = n // 2
    l = nb * bl
    tm, tn = 2 * DFT_HALF, 512
    cpt = HY_DIM // tn
    nd = kf.shape[0]
    return pl.pallas_call(
        _dft_fwd_kernel,
        grid=(cpt, n // tm, batch),
        in_specs=[pl.BlockSpec((tm, bl), lambda c, t, b: (t, 0)),
                  pl.BlockSpec((l, tn), lambda c, t, b: (b, z_col * cpt + c)),
                  pl.BlockSpec((nd, 3 * DFT_HALF, tn), lambda c, t, b: (0, t, k_col * cpt + c))],
        out_specs=pl.BlockSpec((1, nb, tm, tn), lambda c, t, b: (b, 0, t, c)),
        out_shape=jax.ShapeDtypeStruct((batch, nb, n, HY_DIM), BF16),
        compiler_params=_cparams(("arbitrary", "arbitrary", "arbitrary")),
        name="hyena_dft_fwd",
    )(fwd, z_arr, kf)


def _dft_inv_kernel(g_ref, y_ref, gate_ref, z_ref, skip_ref, o_ref):
    conv = _dot(g_ref[...], y_ref[0, 0])
    z = z_ref[...].astype(F32)
    o_ref[...] = (gate_ref[...].astype(F32) * (conv + skip_ref[...] * z)).astype(o_ref.dtype)


def _dft_inv(inv, y, gate_arr, gate_col, z_arr, z_col, skip, batch, nb):
    bl, n = inv.shape
    tn = HY_DIM
    return pl.pallas_call(
        _dft_inv_kernel,
        grid=(batch, nb),
        in_specs=[pl.BlockSpec((bl, n), lambda b, i: (0, 0)),
                  pl.BlockSpec((1, 1, n, tn), lambda b, i: (b, i, 0, 0)),
                  pl.BlockSpec((bl, tn), lambda b, i: (b * nb + i, gate_col)),
                  pl.BlockSpec((bl, tn), lambda b, i: (b * nb + i, z_col)),
                  pl.BlockSpec((1, tn), lambda b, i: (0, 0))],
        out_specs=pl.BlockSpec((bl, tn), lambda b, i: (b * nb + i, 0)),
        out_shape=jax.ShapeDtypeStruct((batch * nb * bl, tn), BF16),
        compiler_params=_cparams(("arbitrary", "arbitrary")),
        name="hyena_dft_inv",
    )(inv, y, gate_arr, z_arr, skip.reshape(1, tn))


def _tap_features(l):
    f32 = np.float32
    t01 = np.linspace(0.0, 1.0, l, dtype=f32)[:, None]
    ang = f32(2.0 * math.pi) * np.arange(l, dtype=f32)[:, None] / f32(l)
    bands = np.linspace(1e-4, HY_BANDS - 1, HY_BANDS, dtype=f32)[None, :]
    z = np.concatenate([t01, np.cos(bands * ang), -np.sin(bands * ang)], axis=-1).astype(f32)
    pos = np.concatenate([np.arange(l), [0], np.arange(l - 1, 0, -1)])
    zz = np.zeros((2 * l, 128), f32)
    zz[:, :z.shape[1]] = z[pos]
    zz[l] = 0.0
    return zz


def _dot_hi(a, b):
    a1 = a.astype(BF16)
    a2 = (a - a1.astype(F32)).astype(BF16)
    b1 = b.astype(BF16)
    b2 = (b - b1.astype(F32)).astype(BF16)
    return _dot(a1, b1) + _dot(a1, b2) + _dot(a2, b1)


def _filter_kernel(zz_ref, w1_ref, b1_ref, fr_ref, w2_ref, b2_ref, w3f_ref, w3b_ref, dl_ref,
                   o_ref, hdn_ref):
    n = zz_ref.shape[0]
    l = n // 2

    @pl.when((pl.program_id(0) == 0) & (pl.program_id(1) == 0))
    def _():
        h1 = jnp.sin(fr_ref[0:1, :] * (_dot_hi(zz_ref[...], w1_ref[...]) + b1_ref[...]))
        hdn_ref[...] = jnp.sin(fr_ref[1:2, :] * (_dot_hi(h1, w2_ref[...]) + b2_ref[...]))

    filt = jnp.concatenate([_dot_hi(hdn_ref[0:l, :], w3f_ref[...]),
                            _dot_hi(hdn_ref[l:n, :], w3b_ref[...])], axis=0)
    decay = jnp.exp(-zz_ref[:, 0:1] * dl_ref[...])
    rowi = lax.broadcasted_iota(jnp.int32, filt.shape, 0)
    kern = jnp.where(rowi == l, 0.0, filt * decay)
    o_ref[...] = (kern / jnp.sum(jnp.abs(kern), axis=0, keepdims=True)).astype(o_ref.dtype)


def _hyena_filters(l, w1, b1, freq, w2, b2, w3):
    n = 2 * l
    tn = 256
    ff = w2.shape[0]
    cpo = HY_DIM // tn
    deltas = np.abs(np.linspace(math.log(HY_DECAY_TARGET) / HY_DECAY_PCT_LONG,
                                math.log(HY_DECAY_TARGET) / HY_DECAY_PCT_SHORT, HY_DIM,
                                dtype=np.float32)).reshape(1, HY_DIM)
    const = lambda o, c: (0, 0)
    return pl.pallas_call(
        _filter_kernel,
        grid=(HY_ORDER, cpo),
        in_specs=[pl.BlockSpec((n, 128), const), pl.BlockSpec((128, ff), const),
                  pl.BlockSpec((1, ff), const), pl.BlockSpec((2, ff), const),
                  pl.BlockSpec((ff, ff), const), pl.BlockSpec((1, ff), const),
                  pl.BlockSpec((ff, tn), lambda o, c: (0, o * 2 * cpo + c)),
                  pl.BlockSpec((ff, tn), lambda o, c: (0, o * 2 * cpo + cpo + c)),
                  pl.BlockSpec((1, tn), lambda o, c: (0, c))],
        out_specs=pl.BlockSpec((n, tn), lambda o, c: (0, o * cpo + c)),
        out_shape=jax.ShapeDtypeStruct((n, HY_ORDER * HY_DIM), BF16),
        scratch_shapes=[pltpu.VMEM((n, ff), F32)],
        compiler_params=_cparams(("arbitrary", "arbitrary")),
        name="hyena_filter_mlp",
    )(jnp.asarray(_tap_features(l)), jnp.pad(w1, ((0, 128 - w1.shape[0]), (0, 0))),
      b1.reshape(1, ff), freq, w2, b2.reshape(1, ff), w3, w3, jnp.asarray(deltas))


def _hyena(p_arr, lp, dft, batch, grid):
    l = p_arr.shape[0] // batch
    fwd, inv = dft
    uc = _short_conv(p_arr, lp['hy_conv_w'].reshape(9, 3 * HY_DIM), lp['hy_conv_b'], batch, grid)
    kern = _hyena_filters(l, lp['hy_ff_w1'], lp['hy_ff_b1'], lp['hy_ff_freq'],
                          lp['hy_ff_w2'], lp['hy_ff_b2'], lp['hy_ff_w3'])
    nb = l // inv.shape[0]
    kf = _filter_spectrum(fwd, kern, nb)
    z_arr, z_col = uc, 0
    for n in range(HY_ORDER):
        y = _dft_fwd(fwd, z_arr, z_col, kf, n, batch, nb)
        z_arr = _dft_inv(inv, y, uc, n + 1, z_arr, z_col, lp['hy_skip'][n], batch, nb)
        z_col = 0
    return z_arr


def _merge_kernel(yhy_ref, ohg_ref, hgg_ref, yrt_ref, b0_ref, b1_ref, b2_ref, x_ref, mod_ref,
                  nw_ref, phy_ref, phg_ref, prt_ref, wo_ref, lng_ref, lnb_ref, o_ref):
    o = ohg_ref[...]
    yhg = (o * lax.rsqrt(jnp.mean(o * o, axis=-1, keepdims=True) + LN_EPS) * nw_ref[...]
           * _silu(hgg_ref[...].astype(F32)))
    m = (jax.nn.sigmoid(b0_ref[...].astype(F32)) * _dot(yhy_ref[...], phy_ref[...])
         + jax.nn.sigmoid(b1_ref[...].astype(F32)) * _dot(yhg.astype(BF16), phg_ref[...])
         + jax.nn.sigmoid(b2_ref[...].astype(F32)) * _dot(yrt_ref[...], prt_ref[...]))
    t = _dot(m.astype(BF16), wo_ref[...])
    gt1 = mod_ref[0][2:3]
    o_ref[...] = _layer_norm_rows(DN_ALPHA * x_ref[...] + gt1 * t, lng_ref[...], lnb_ref[...])


def _merge(y_hy, o_hg, y_rt, p_arr, x, modp, lp, wbf, rows_per_batch):
    m = x.shape[0]
    tm = min(512, m)
    d = D_MODEL
    mi = _mod_index(rows_per_batch, tm)
    row = lambda i: (i, 0)
    const = lambda i: (0, 0)
    gcol = COL_HG * 128 // d
    bcol = COL_BR * 128 // d
    return pl.pallas_call(
        _merge_kernel,
        grid=(m // tm,),
        in_specs=[pl.BlockSpec((tm, d), row), pl.BlockSpec((tm, d), row),
                  pl.BlockSpec((tm, d), lambda i: (i, gcol)),
                  pl.BlockSpec((tm, 2 * d), row),
                  pl.BlockSpec((tm, d), lambda i: (i, bcol)),
                  pl.BlockSpec((tm, d), lambda i: (i, bcol + 1)),
                  pl.BlockSpec((tm, d), lambda i: (i, bcol + 2)),
                  pl.BlockSpec((tm, d), row),
                  pl.BlockSpec((1, 8, d), lambda i: (mi(i), 0, 0)),
                  pl.BlockSpec((1, d), const),
                  pl.BlockSpec((d, d), const), pl.BlockSpec((d, d), const),
                  pl.BlockSpec((2 * d, d), const), pl.BlockSpec((d, d), const),
                  pl.BlockSpec((1, d), const), pl.BlockSpec((1, d), const)],
        out_specs=pl.BlockSpec((tm, d), row),
        out_shape=jax.ShapeDtypeStruct((m, d), F32),
        compiler_params=_cparams(("arbitrary",)),
        name="merge_out_ln",
    )(y_hy, o_hg, p_arr, y_rt, p_arr, p_arr, p_arr, x, modp, lp['hg_norm_w'].reshape(1, d),
      wbf['p_hy'], wbf['p_hg'], wbf['p_rt'], wbf['w_o'],
      lp['ln1_g'].reshape(1, d), lp['ln1_b'].reshape(1, d))


def _ffn_kernel(x_ref, mod_ref, w1_ref, w3_ref, w2_ref, lng_ref, lnb_ref, o_ref, h_ref, acc_ref):
    k = pl.program_id(1)
    m = mod_ref[0]

    @pl.when(k == 0)
    def _():
        h_ref[...] = (x_ref[...] * (1.0 + m[4:5]) + m[3:4]).astype(BF16)
        acc_ref[...] = jnp.zeros(acc_ref.shape, F32)

    h = h_ref[...]
    u = _silu(_dot(h, w1_ref[...])) * _dot(h, w3_ref[...])
    acc_ref[...] += _dot(u.astype(BF16), w2_ref[...])

    @pl.when(k == pl.num_programs(1) - 1)
    def _():
        o_ref[...] = _layer_norm_rows(DN_ALPHA * x_ref[...] + m[5:6] * acc_ref[...],
                                      lng_ref[...], lnb_ref[...])


def _ffn_dense(x, modp, w1, w3, w2, ln_g, ln_b, rows_per_batch):
    m = x.shape[0]
    d = D_MODEL
    dff = w1.shape[1]
    tm = min(1024, rows_per_batch or m)
    tf = dff // 2
    mi = _mod_index(rows_per_batch, tm)
    return pl.pallas_call(
        _ffn_kernel,
        grid=(m // tm, dff // tf),
        in_specs=[pl.BlockSpec((tm, d), lambda i, k: (i, 0)),
                  pl.BlockSpec((1, 8, d), lambda i, k: (mi(i), 0, 0)),
                  pl.BlockSpec((d, tf), lambda i, k: (0, k)),
                  pl.BlockSpec((d, tf), lambda i, k: (0, k)),
                  pl.BlockSpec((tf, d), lambda i, k: (k, 0)),
                  pl.BlockSpec((1, d), lambda i, k: (0, 0)),
                  pl.BlockSpec((1, d), lambda i, k: (0, 0))],
        out_specs=pl.BlockSpec((tm, d), lambda i, k: (i, 0)),
        out_shape=jax.ShapeDtypeStruct((m, d), F32),
        scratch_shapes=[pltpu.VMEM((tm, d), BF16), pltpu.VMEM((tm, d), F32)],
        compiler_params=_cparams(("arbitrary", "arbitrary")),
        name="ffn_dense_ln",
    )(x, modp, w1, w3, w2, ln_g.reshape(1, d), ln_b.reshape(1, d))


def _router_kernel(x_ref, mod_ref, r_ref, h_ref, lg_ref):
    m = mod_ref[0]
    h = x_ref[...] * (1.0 + m[4:5]) + m[3:4]
    h_ref[...] = h
    a1, a2, a3 = _split3(h)
    r1, r2, r3 = _split3(r_ref[...])
    lg_ref[...] = (_dot(a1, r1) + _dot(a1, r2) + _dot(a2, r1)
                   + _dot(a2, r2) + _dot(a1, r3) + _dot(a3, r1))


def _router(x, modp, router_pad, rows_per_batch):
    m = x.shape[0]
    d = D_MODEL
    tm = min(512, m)
    mi = _mod_index(rows_per_batch, tm)
    return pl.pallas_call(
        _router_kernel,
        grid=(m // tm,),
        in_specs=[pl.BlockSpec((tm, d), lambda i: (i, 0)),
                  pl.BlockSpec((1, 8, d), lambda i: (mi(i), 0, 0)),
                  pl.BlockSpec((d, 128), lambda i: (0, 0))],
        out_specs=[pl.BlockSpec((tm, d), lambda i: (i, 0)),
                   pl.BlockSpec((tm, 128), lambda i: (i, 0))],
        out_shape=[jax.ShapeDtypeStruct((m, d), F32), jax.ShapeDtypeStruct((m, 128), F32)],
        compiler_params=_cparams(("arbitrary",)),
        name="moe_router",
    )(x, modp, router_pad)


def _moe_ffn_kernel(be_ref, nu_ref, xp_ref, w1_ref, w3_ref, w2_ref, o_ref, x_ref, acc_ref):
    j = pl.program_id(0)
    k = pl.program_id(1)

    @pl.when(j < nu_ref[0])
    def _():
        @pl.when(k == 0)
        def _():
            x_ref[...] = xp_ref[...].astype(BF16)
            acc_ref[...] = jnp.zeros(acc_ref.shape, F32)

        x = x_ref[...]
        u = _silu(_dot(x, w1_ref[0])) * _dot(x, w3_ref[0])
        acc_ref[...] += _dot(u.astype(BF16), w2_ref[0])

        @pl.when(k == pl.num_programs(1) - 1)
        def _():
            o_ref[...] = acc_ref[...]

    @pl.when((j >= nu_ref[0]) & (k == pl.num_programs(1) - 1))
    def _():
        o_ref[...] = jnp.zeros(o_ref.shape, F32)


def _moe_ffn(xb, block_e, n_used, w1, w3, w2):
    ns = xb.shape[0]
    d = D_MODEL
    tm = MOE_ROWS
    dex = w1.shape[2]
    tf = dex // 2
    grid_spec = pltpu.PrefetchScalarGridSpec(
        num_scalar_prefetch=2,
        grid=(ns // tm, dex // tf),
        in_specs=[pl.BlockSpec((tm, d), lambda j, k, be, nu: (j, 0)),
                  pl.BlockSpec((1, d, tf), lambda j, k, be, nu: (be[j], 0, k)),
                  pl.BlockSpec((1, d, tf), lambda j, k, be, nu: (be[j], 0, k)),
                  pl.BlockSpec((1, tf, d), lambda j, k, be, nu: (be[j], k, 0))],
        out_specs=pl.BlockSpec((tm, d), lambda j, k, be, nu: (j, 0)),
        scratch_shapes=[pltpu.VMEM((tm, d), BF16), pltpu.VMEM((tm, d), F32)])
    return pl.pallas_call(
        _moe_ffn_kernel,
        grid_spec=grid_spec,
        out_shape=jax.ShapeDtypeStruct((ns, d), F32),
        compiler_params=_cparams(("arbitrary", "arbitrary")),
        name="moe_expert_ffn",
    )(block_e, n_used, xb, w1, w3, w2)


def _combine_kernel(x_ref, mod_ref, y0_ref, y1_ref, g_ref, lng_ref, lnb_ref, o_ref):
    m = mod_ref[0]
    g = g_ref[...]
    f = g[:, 0:1] * y0_ref[...] + g[:, 1:2] * y1_ref[...]
    o_ref[...] = _layer_norm_rows(DN_ALPHA * x_ref[...] + m[5:6] * f, lng_ref[...], lnb_ref[...])


def _moe_combine(x, modp, y0, y1, gate_pad, ln_g, ln_b, rows_per_batch):
    m = x.shape[0]
    d = D_MODEL
    tm = min(512, m)
    mi = _mod_index(rows_per_batch, tm)
    row = lambda i: (i, 0)
    return pl.pallas_call(
        _combine_kernel,
        grid=(m // tm,),
        in_specs=[pl.BlockSpec((tm, d), row),
                  pl.BlockSpec((1, 8, d), lambda i: (mi(i), 0, 0)),
                  pl.BlockSpec((tm, d), row), pl.BlockSpec((tm, d), row),
                  pl.BlockSpec((tm, 128), row),
                  pl.BlockSpec((1, d), lambda i: (0, 0)), pl.BlockSpec((1, d), lambda i: (0, 0))],
        out_specs=pl.BlockSpec((tm, d), row),
        out_shape=jax.ShapeDtypeStruct((m, d), F32),
        compiler_params=_cparams(("arbitrary",)),
        name="moe_combine_ln",
    )(x, modp, y0, y1, gate_pad, ln_g.reshape(1, d), ln_b.reshape(1, d))


def _cast_kernel(x_ref, o_ref):
    o_ref[...] = x_ref[...].astype(o_ref.dtype)


def _to_bf16(w, group):
    _, e, a, b = w.shape
    ta, tb = (a // 4, b) if a >= b else (a, b // 4)
    cut_rows = a >= b
    return pl.pallas_call(
        _cast_kernel,
        grid=(e, 4),
        in_specs=[pl.BlockSpec((None, 1, ta, tb),
                               lambda i, r: (group, i, r, 0) if cut_rows else (group, i, 0, r))],
        out_specs=pl.BlockSpec((1, ta, tb), lambda i, r: (i, r, 0) if cut_rows else (i, 0, r)),
        out_shape=jax.ShapeDtypeStruct((e, a, b), BF16),
        compiler_params=_cparams(("arbitrary", "arbitrary")),
        name="weight_cast",
    )(w)


def _moe(x, modp, router, w1, w3, w2, ln_g, ln_b, rows_per_batch):
    n, d = x.shape
    e = router.shape[1]
    h2, logits = _router(x, modp, jnp.pad(router, ((0, 0), (0, 128 - e))), rows_per_batch)
    top_val, top_idx = lax.top_k(logits[:, :e], TOP_K)
    gate = jax.nn.softmax(top_val, axis=-1)
    flat_e = top_idx.reshape(-1)
    flat_t = jnp.repeat(jnp.arange(n, dtype=jnp.int32), TOP_K)
    order = jnp.argsort(flat_e, stable=True).astype(jnp.int32)
    rank = jnp.argsort(order).astype(jnp.int32)
    counts = jnp.sum((flat_e[:, None] == jnp.arange(e)[None, :]).astype(jnp.int32), axis=0)
    starts = jnp.cumsum(counts) - counts
    padded = (counts + MOE_ROWS - 1) // MOE_ROWS * MOE_ROWS
    pad_end = jnp.cumsum(padded)
    pad_start = pad_end - padded
    n_blocks = -(-(n * TOP_K) // MOE_ROWS) + e
    n_slots = n_blocks * MOE_ROWS
    block_start = jnp.arange(n_blocks) * MOE_ROWS
    block_e = jnp.minimum(jnp.sum(block_start[:, None] >= pad_end[None, :], axis=1), e - 1).astype(jnp.int32)
    n_used = (pad_end[-1] // MOE_ROWS).astype(jnp.int32).reshape(1)
    slot_e = jnp.repeat(block_e, MOE_ROWS)
    slot_off = jnp.arange(n_slots, dtype=jnp.int32) - pad_start[slot_e]
    slot_valid = slot_off < counts[slot_e]
    slot_src = jnp.where(slot_valid, starts[slot_e] + slot_off, 0)
    slot_tok = jnp.where(slot_valid, flat_t[order[slot_src]], 0).astype(jnp.int32)
    pos = (pad_start[flat_e] + rank - starts[flat_e]).astype(jnp.int32).reshape(n, TOP_K)
    xb = jnp.take(h2, slot_tok, axis=0, mode="clip")
    yb = _moe_ffn(xb, block_e, n_used, w1, w3, w2)
    y0 = jnp.take(yb, pos[:, 0], axis=0, mode="clip")
    y1 = jnp.take(yb, pos[:, 1], axis=0, mode="clip")
    gate_pad = jnp.pad(gate, ((0, 0), (0, 128 - TOP_K)))
    return _moe_combine(x, modp, y0, y1, gate_pad, ln_g, ln_b, rows_per_batch)


def kernel(x, c, ctx, c_ctx, ada_w, ada_b, w_in, hy_conv_w, hy_conv_b, hy_ff_w1, hy_ff_b1, hy_ff_freq, hy_ff_w2, hy_ff_b2, hy_ff_w3, hy_skip, hg_lb_logits, hg_norm_w, p_hy, p_hg, p_rt, w_o, ln1_g, ln1_b, ln2_g, ln2_b, ffn_w1, ffn_w3, ffn_w2, moe_router, moe_w1, moe_w3, moe_w2):
    batch, l, d = x.shape
    lc = ctx.shape[1]
    assert d == D_MODEL and batch <= 8
    assert l % 512 == 0 and l % GRID_W == 0 and lc % max(RET_CHUNK, DFT_HALF) == 0

    cs = jnp.cumsum(jax.nn.softmax(hg_lb_logits.astype(F32), axis=1), axis=1)
    lower_bounds = cs - cs[:, :1]
    cc = jnp.zeros((16, d), F32).at[:batch].set(c).at[8].set(c_ctx)
    ret_tables = _retention_tables(lc, l)
    dft_lat = _dft_matrices(min(HY_BLOCK, l))
    dft_ctx = _dft_matrices(min(HY_BLOCK, lc))

    x_lat = x.reshape(batch * l, d)
    x_ctx = ctx.reshape(batch * lc, d)
    for i in range(DEPTH):
        need_ctx_out = i < DEPTH - 1
        use_moe = i % 2 == 1
        g = i // 2
        lp = {'hy_conv_w': hy_conv_w[i], 'hy_conv_b': hy_conv_b[i], 'hy_ff_w1': hy_ff_w1[i],
              'hy_ff_b1': hy_ff_b1[i], 'hy_ff_freq': hy_ff_freq[i], 'hy_ff_w2': hy_ff_w2[i],
              'hy_ff_b2': hy_ff_b2[i], 'hy_ff_w3': hy_ff_w3[i], 'hy_skip': hy_skip[i],
              'hg_norm_w': hg_norm_w[i], 'ln1_g': ln1_g[i], 'ln1_b': ln1_b[i]}
        wbf = {'p_hy': p_hy[i].astype(BF16), 'p_hg': p_hg[i].astype(BF16),
               'p_rt': p_rt[i].astype(BF16), 'w_o': w_o[i].astype(BF16)}
        mod = _ada(cc, ada_w, i, ada_b[i])
        modp = jnp.pad(mod.reshape(16, 6, d), ((0, 0), (0, 2), (0, 0)))
        lb = lower_bounds[:, i]
        gl_tab = jnp.pad(jnp.stack([lb, 1.0 - lb], axis=1), ((0, 0), (0, 6), (0, 0)))

        p_lat = _proj(x_lat, modp, w_in, i, N_IN_COLS, l)
        p_ctx = _proj(x_ctx, modp, w_in, i, N_IN_COLS if need_ctx_out else N_STATE_COLS, None)

        o_hg_l, o_hg_c = _gla(p_lat, p_ctx, gl_tab, batch, need_ctx_out)
        y_rt_l, y_rt_c = _retention(p_lat, p_ctx, ret_tables, batch, need_ctx_out)
        y_hy_l = _hyena(p_lat, lp, dft_lat, batch, True)
        x_lat_new = _merge(y_hy_l, o_hg_l, y_rt_l, p_lat, x_lat, modp, lp, wbf, l)
        if need_ctx_out:
            y_hy_c = _hyena(p_ctx, lp, dft_ctx, batch, False)
            x_ctx = _merge(y_hy_c, o_hg_c, y_rt_c, p_ctx, x_ctx, modp, lp, wbf, None)
        x_lat = x_lat_new

        if use_moe:
            if need_ctx_out:
                raise NotImplementedError("MoE layer with a context output is not part of this trunk")
            w1, w3, w2 = _to_bf16(moe_w1, g), _to_bf16(moe_w3, g), _to_bf16(moe_w2, g)
            x_lat = _moe(x_lat, modp, moe_router[g], w1, w3, w2, ln2_g[i], ln2_b[i], l)
        else:
            w1, w3, w2 = ffn_w1[g].astype(BF16), ffn_w3[g].astype(BF16), ffn_w2[g].astype(BF16)
            x_lat = _ffn_dense(x_lat, modp, w1, w3, w2, ln2_g[i], ln2_b[i], l)
            if need_ctx_out:
                x_ctx = _ffn_dense(x_ctx, modp, w1, w3, w2, ln2_g[i], ln2_b[i], None)
    return x_lat.reshape(batch, l, d)
```

```python
import functools
import math

import numpy as np
import jax
import jax.numpy as jnp
from jax import lax
from jax.experimental import pallas as pl
from jax.experimental.pallas import tpu as pltpu

F32 = jnp.float32
BF16 = jnp.bfloat16

D_MODEL = 1024
DEPTH = 2
GRID_W = 64
HY_DIM = 1024
HY_ORDER = 2
HY_BANDS = 16
HY_DECAY_TARGET = 1e-2
HY_DECAY_PCT_SHORT = 0.3
HY_DECAY_PCT_LONG = 1.5
HG_HEADS = 8
HG_HK = 128
RT_HEADS = 4
RT_HK = 256
RT_HV = 512
RT_ROPE_BASE = 10000.0
N_STATE_COLS = 6144
N_IN_COLS = 17408
N_EXPERTS = 8
TOP_K = 2
DN_ALPHA = (2 * DEPTH) ** 0.25
LN_EPS = 1e-5

COL_FF, COL_FB, COL_HI, COL_RK, COL_RV = 0, 8, 16, 24, 32
COL_HQ, COL_HG, COL_RQ, COL_RG, COL_HY, COL_BR = 48, 56, 64, 72, 88, 112

GLA_CHUNK = 256
GLA_LEVELS = 8
GLA_PAD = 8
GLA_MIN_LOG2 = -1e30
RET_CHUNK = 256
DFT_HALF = 256
HY_BLOCK = 512
MIX_ROWS = 16
CONV_ROWS = 256
MOE_ROWS = 512
VMEM_LIMIT = 56 * 1024 * 1024


def _cparams(sem, flags=None):
    return pltpu.CompilerParams(dimension_semantics=sem, vmem_limit_bytes=VMEM_LIMIT, flags=flags)


def _silu(x):
    return x * jax.nn.sigmoid(x)


def _layer_norm_rows(r, g, b):
    mu = jnp.mean(r, axis=-1, keepdims=True)
    d = r - mu
    var = jnp.mean(d * d, axis=-1, keepdims=True)
    return d * lax.rsqrt(var + LN_EPS) * g + b


def _split3(x):
    h = x.astype(BF16)
    r = x - h.astype(F32)
    m = r.astype(BF16)
    l = (r - m.astype(F32)).astype(BF16)
    return h, m, l


def _dot(a, b):
    return jnp.dot(a, b, preferred_element_type=F32)


def _dot_nt(a, b):
    return lax.dot_general(a, b, (((1,), (1,)), ((), ())), preferred_element_type=F32)


def _dot_tn(a, b):
    return lax.dot_general(a, b, (((0,), (0,)), ((), ())), preferred_element_type=F32)


def _ada_kernel(c_ref, w_ref, b_ref, o_ref):
    s = _silu(c_ref[...])
    o_ref[...] = _dot(s.astype(BF16), w_ref[...].astype(BF16)) + b_ref[...]


def _ada(cc, w, layer, b):
    n = w.shape[2]
    tn = 1024
    return pl.pallas_call(
        _ada_kernel,
        grid=(n // tn,),
        in_specs=[pl.BlockSpec((16, D_MODEL), lambda j: (0, 0)),
                  pl.BlockSpec((None, D_MODEL, tn), lambda j: (layer, 0, j)),
                  pl.BlockSpec((1, tn), lambda j: (0, j))],
        out_specs=pl.BlockSpec((16, tn), lambda j: (0, j)),
        out_shape=jax.ShapeDtypeStruct((16, n), F32),
        compiler_params=_cparams(("arbitrary",)),
        name="ada_mod",
    )(cc, w, b.reshape(1, n))


def _mod_index(rows_per_batch, tm):
    if rows_per_batch is None:
        return lambda i: 8
    return lambda i: (i * tm) // rows_per_batch


def _proj_kernel(x_ref, mod_ref, w_ref, o_ref, h_ref):
    @pl.when(pl.program_id(1) == 0)
    def _():
        m = mod_ref[0]
        h_ref[...] = (x_ref[...] * (1.0 + m[1:2]) + m[0:1]).astype(BF16)

    o_ref[...] = _dot(h_ref[...], w_ref[...].astype(BF16)).astype(o_ref.dtype)


def _proj(x, modp, w, layer, n_cols, rows_per_batch):
    m = x.shape[0]
    tm = min(2048, rows_per_batch or m)
    tn = 1024
    mi = _mod_index(rows_per_batch, tm)
    return pl.pallas_call(
        _proj_kernel,
        grid=(m // tm, n_cols // tn),
        in_specs=[pl.BlockSpec((tm, D_MODEL), lambda i, j: (i, 0)),
                  pl.BlockSpec((1, 8, D_MODEL), lambda i, j: (mi(i), 0, 0)),
                  pl.BlockSpec((None, D_MODEL, tn), lambda i, j: (layer, 0, j))],
        out_specs=pl.BlockSpec((tm, tn), lambda i, j: (i, j)),
        out_shape=jax.ShapeDtypeStruct((m, n_cols), BF16),
        scratch_shapes=[pltpu.VMEM((tm, D_MODEL), BF16)],
        compiler_params=_cparams(("arbitrary", "arbitrary")),
        name="in_proj",
    )(x, modp, w)


def _gla_gates(fl, gl_ref, d):
    lb = gl_ref[d, 0:1, :]
    oml = gl_ref[d, 1:2, :]
    t = jnp.exp(-jnp.abs(fl))
    r = 1.0 / (1.0 + t)
    tr = t * r
    pos = fl >= 0.0
    f = lb + oml * jnp.where(pos, r, tr)
    key = oml * jnp.where(pos, tr, r)
    return jnp.maximum(jnp.log2(f), GLA_MIN_LOG2), key


def _gla_cumsum(g, rev):
    c = g.shape[0]
    r = lax.broadcasted_iota(jnp.int32, (c, c), 0)
    u = lax.broadcasted_iota(jnp.int32, (c, c), 1)
    tri = jnp.where((u >= r) if rev else (u <= r), 1.0, 0.0).astype(BF16)
    return _dot(jnp.concatenate([tri, tri, tri], axis=1), jnp.concatenate(_split3(g), axis=0))


def _gla_state_update(st, b, kk, vv, rev):
    c = b.shape[0]
    end = 0 if rev else c - 1
    b_end = b[end:end + 1, :]
    khat = (kk * jnp.exp2(b_end - b)).astype(BF16)
    return st * jnp.exp2(b_end) + _dot_tn(vv, khat)


def _gla_level_table():
    c = GLA_CHUNK
    t = np.arange(c)[:, None]
    s = np.arange(c)[None, :]
    top_bit = np.floor(np.log2(np.maximum(t ^ s, 1))).astype(np.int32)
    fwd = np.where(t > s, top_bit, np.where(t == s, GLA_LEVELS, -1))
    return np.stack([fwd, fwd.T]).astype(np.int32)


def _gla_ref_rows(b, bpad_ref, level, rev):
    c = b.shape[0]
    half = 1 << level
    blk = 2 * half
    idx = half if rev else half - 1
    if blk % 8 == 0:
        r = b.reshape(c // blk, blk, HG_HK)[:, idx:idx + 1, :]
        return jnp.broadcast_to(r, (c // blk, blk, HG_HK)).reshape(c, HG_HK)
    m = lax.broadcasted_iota(jnp.int32, (c, HG_HK), 0) % blk
    out = b
    for v in range(blk):
        if v != idx:
            off = GLA_PAD + idx - v
            out = jnp.where(m == v, bpad_ref[off:off + c, :], out)
    return out


def _gla_chunk_out(b, kk, vv, qq, st, rev, lvt, bpad_ref):
    c = GLA_CHUNK
    o = _dot_nt((qq * jnp.exp2(b)).astype(BF16), st.astype(BF16))
    bpad_ref[GLA_PAD:GLA_PAD + c, :] = b
    qq_b, kk_b = qq.astype(BF16), kk.astype(BF16)
    sc = jnp.zeros((c, c), F32)
    for level in range(GLA_LEVELS):
        half = 1 << level
        blk = 2 * half
        ref = _gla_ref_rows(b, bpad_ref, level, rev)
        kt = kk_b * jnp.exp2(ref - b).astype(BF16)
        if half % 8:
            p = _dot_nt(qq_b * jnp.exp2(b - ref).astype(BF16), kt)
            sc = jnp.where(lvt == level, p, sc)
        else:
            nblk = c // blk
            lo, hi = (0, half) if rev else (half, blk)

            def part(x):
                return x.reshape(nblk, blk, x.shape[-1])[:, lo:hi, :]

            qt = (part(qq) * jnp.exp2(part(b) - part(ref))).reshape(c // 2, HG_HK)
            p = _dot_nt(qt.astype(BF16), kt).reshape(nblk, half, c)
            new = jnp.where(part(lvt) == level, p, part(sc))
            rest = sc.reshape(nblk, blk, c)[:, half:blk, :] if rev else sc.reshape(nblk, blk, c)[:, 0:half, :]
            sc = jnp.concatenate([new, rest] if rev else [rest, new], axis=1).reshape(c, c)
    sc = jnp.where(lvt == GLA_LEVELS, _dot_nt(qq_b, kk_b), sc)
    return o + _dot(sc.astype(BF16), vv)


def _gla_kernel(*refs, need_ctx_out, n_lat, n_ctx):
    if need_ctx_out:
        ffl, fbl, il, ql, ffc, fbc, ic, qc, gl_ref, lv_ref, ol_ref, oc_ref = refs[:12]
        bl, kl, bc, kc, qsl, qsc, bpf, bpr = refs[12:]
    else:
        ffl, fbl, il, ql, ffc, fbc, ic, gl_ref, lv_ref, ol_ref = refs[:10]
        bl, kl, bc, kc, qsl, bpf, bpr = refs[10:]
        qc = oc_ref = qsc = None
    c = GLA_CHUNK
    q_scale = HG_HK ** -0.5
    bpf[...] = jnp.zeros(bpf.shape, F32)
    bpr[...] = jnp.zeros(bpr.shape, F32)
    ol_ref[...] = jnp.zeros(ol_ref.shape, F32)
    if need_ctx_out:
        oc_ref[...] = jnp.zeros(oc_ref.shape, F32)

    def chunk_rows(idx):
        return pl.ds(pl.multiple_of(idx * c, c), c)

    def prepare(f_refs, q_ref, b_sc, k_sc, q_sc, n):
        def body(j, carry):
            rows = chunk_rows(j)
            for d in range(2):
                g, kk = _gla_gates(f_refs[d][rows, :].astype(F32), gl_ref, d)
                b_sc[d, rows, :] = _gla_cumsum(g, d == 1)
                k_sc[d, rows, :] = kk
            if q_ref is not None:
                q_sc[rows, :] = _silu(q_ref[rows, :].astype(F32)) * q_scale
            return carry
        lax.fori_loop(0, n, body, 0, unroll=min(2, n))

    prepare((ffc, fbc), qc, bc, kc, qsc, n_ctx)
    prepare((ffl, fbl), ql, bl, kl, qsl, n_lat)

    def one_dir(d, b_sc, k_sc, i_ref, q_sc, o_ref, idx, st):
        rev = d == 1
        rows = chunk_rows(idx)
        b, kk, vv = b_sc[d, rows, :], k_sc[d, rows, :], i_ref[rows, :]
        if q_sc is not None:
            o_ref[rows, :] += _gla_chunk_out(b, kk, vv, q_sc[rows, :], st, rev, lv_ref[d],
                                             bpr if rev else bpf)
        return _gla_state_update(st, b, kk, vv, rev)

    def ctx_body(j, carry):
        return (one_dir(0, bc, kc, ic, qsc, oc_ref, j, carry[0]),
                one_dir(1, bc, kc, ic, qsc, oc_ref, n_ctx - 1 - j, carry[1]))

    def lat_body(j, carry):
        return (one_dir(0, bl, kl, il, qsl, ol_ref, j, carry[0]),
                one_dir(1, bl, kl, il, qsl, ol_ref, n_lat - 1 - j, carry[1]))

    zero = jnp.zeros((HG_HK, HG_HK), F32)
    carry = lax.fori_loop(0, n_ctx, ctx_body, (zero, zero), unroll=min(2, n_ctx))
    lax.fori_loop(0, n_lat, lat_body, carry)


def _gla(p_lat, p_ctx, gl_tab, batch, need_ctx_out):
    l = p_lat.shape[0] // batch
    lc = p_ctx.shape[0] // batch
    w = HG_HK
    c = GLA_CHUNK

    def col(base):
        return lambda b, h: (b, base + h)

    lat_specs = [pl.BlockSpec((l, w), col(cb)) for cb in (COL_FF, COL_FB, COL_HI, COL_HQ)]
    ctx_cols = (COL_FF, COL_FB, COL_HI) + ((COL_HQ,) if need_ctx_out else ())
    ctx_specs = [pl.BlockSpec((lc, w), col(cb)) for cb in ctx_cols]
    in_specs = lat_specs + ctx_specs + [pl.BlockSpec((2, 8, w), lambda b, h: (0, 0, h)),
                                        pl.BlockSpec((2, c, c), lambda b, h: (0, 0, 0))]
    args = [p_lat] * 4 + [p_ctx] * len(ctx_cols) + [gl_tab, jnp.asarray(_gla_level_table())]
    out_specs = [pl.BlockSpec((l, w), lambda b, h: (b, h))]
    out_shape = [jax.ShapeDtypeStruct((batch * l, HG_HEADS * w), F32)]
    scratch = [pltpu.VMEM((2, l, w), F32), pltpu.VMEM((2, l, w), F32),
               pltpu.VMEM((2, lc, w), F32), pltpu.VMEM((2, lc, w), F32), pltpu.VMEM((l, w), F32)]
    if need_ctx_out:
        out_specs.append(pl.BlockSpec((lc, w), lambda b, h: (b, h)))
        out_shape.append(jax.ShapeDtypeStruct((batch * lc, HG_HEADS * w), F32))
        scratch.append(pltpu.VMEM((lc, w), F32))
    scratch += [pltpu.VMEM((c + 2 * GLA_PAD, w), F32)] * 2
    outs = pl.pallas_call(
        functools.partial(_gla_kernel, need_ctx_out=need_ctx_out, n_lat=l // c, n_ctx=lc // c),
        grid=(batch, HG_HEADS),
        in_specs=in_specs,
        out_specs=out_specs,
        out_shape=out_shape,
        scratch_shapes=scratch,
        compiler_params=_cparams(("arbitrary", "arbitrary")),
        name="hgrn2_scan",
    )(*args)
    return outs[0], (outs[1] if need_ctx_out else None)


def _rotate(x, cos, sin):
    half = x.shape[-1] // 2
    x1, x2 = x[:, :half], x[:, half:]
    return jnp.concatenate([x1 * cos - x2 * sin, x1 * sin + x2 * cos], axis=-1)


def _ret_kernel(*refs, need_ctx_out, n_lat, n_ctx):
    if need_ctx_out:
        (ql, kl, vl, gl, qc, kc, vc, gc, cos_ref, sin_ref, intra_ref, qk_ref, sd_ref,
         yl_ref, yc_ref, qrl, krl, qrc, krc, ol, oc, s_fwd, s_rev) = refs
    else:
        (ql, kl, vl, gl, kc, vc, cos_ref, sin_ref, intra_ref, qk_ref, sd_ref,
         yl_ref, qrl, krl, krc, ol, s_fwd, s_rev) = refs
        qc = gc = yc_ref = qrc = oc = None
    s_refs = (s_fwd, s_rev)
    c = RET_CHUNK
    lc = n_ctx * c
    k_scale = RT_HK ** -0.5

    def rot_all(src, dst, n, pos0, scale):
        def body(j, carry):
            rows = pl.ds(pl.multiple_of(j * c, c), c)
            prow = pl.ds(pl.multiple_of(pos0 + j * c, c), c)
            x = src[rows, :].astype(F32)
            dst[rows, :] = (_rotate(x, cos_ref[prow, :], sin_ref[prow, :]) * scale).astype(BF16)
            return carry
        lax.fori_loop(0, n, body, 0)

    rot_all(kl, krl, n_lat, lc, k_scale)
    rot_all(kc, krc, n_ctx, 0, k_scale)
    rot_all(ql, qrl, n_lat, lc, 1.0)
    if need_ctx_out:
        rot_all(qc, qrc, n_ctx, 0, 1.0)

    for s_ref in s_refs:
        s_ref[...] = jnp.zeros(s_ref.shape, F32)
    ol[...] = jnp.zeros(ol.shape, F32)
    if need_ctx_out:
        oc[...] = jnp.zeros(oc.shape, F32)

    def update(d, kr, v_ref, rows):
        kd = qk_ref[d, 0, c:2 * c, :]
        sdec = sd_ref[d, 0, 0:1, :]
        kh = (kr[rows, :].astype(F32) * jnp.concatenate([kd, kd], axis=1)).astype(BF16)
        s_ref = s_refs[d]
        s_ref[...] = (s_ref[...] * jnp.concatenate([sdec] * (RT_HV // 128), axis=1)
                      + _dot_tn(kh, v_ref[rows, :]))

    def out_step(d, qr, kr, v_ref, o_ref, rows):
        qd = qk_ref[d, 0, 0:c, :]
        qcb = qr[rows, :]
        sc = _dot_nt(qcb, kr[rows, :]) * intra_ref[d, 0]
        qh = (qcb.astype(F32) * jnp.concatenate([qd, qd], axis=1)).astype(BF16)
        o_ref[rows, :] += (_dot(qh, s_refs[d][...].astype(BF16))
                           + _dot(sc.astype(BF16), v_ref[rows, :]))
        update(d, kr, v_ref, rows)

    def ctx_body(j, carry):
        for d in range(2):
            idx = (n_ctx - 1 - j) if d == 1 else j
            rows = pl.ds(pl.multiple_of(idx * c, c), c)
            if need_ctx_out:
                out_step(d, qrc, krc, vc, oc, rows)
            else:
                update(d, krc, vc, rows)
        return carry

    def lat_body(j, carry):
        for d in range(2):
            idx = (n_lat - 1 - j) if d == 1 else j
            out_step(d, qrl, krl, vl, ol, pl.ds(pl.multiple_of(idx * c, c), c))
        return carry

    lax.fori_loop(0, n_ctx, ctx_body, 0)
    lax.fori_loop(0, n_lat, lat_body, 0, unroll=4)

    def readout(o_ref, g_ref, y_ref, n):
        def body(j, carry):
            rows = pl.ds(pl.multiple_of(j * c, c), c)
            o = o_ref[rows, :]
            y = o * lax.rsqrt(jnp.mean(o * o, axis=-1, keepdims=True) + LN_EPS)
            y_ref[rows, :] = (y * _silu(g_ref[rows, :].astype(F32))).astype(y_ref.dtype)
            return carry
        lax.fori_loop(0, n, body, 0)

    readout(ol, gl, yl_ref, n_lat)
    if need_ctx_out:
        readout(oc, gc, yc_ref, n_ctx)


def _retention_tables(lc, l):
    half = RT_HK // 2
    inv = 1.0 / (RT_ROPE_BASE ** jnp.linspace(0.0, 1.0, half, dtype=F32))
    ang = jnp.arange(lc + l, dtype=F32)[:, None] * inv[None, :]
    j = jnp.arange(2 * RT_HEADS, dtype=F32)
    lg_all = jnp.log1p(-jnp.exp2(-5.0 - j))
    c = RET_CHUNK
    pos = jnp.arange(c, dtype=F32)
    rel = pos[:, None] - pos[None, :]
    intra, qk, sd = [], [], []
    for d in range(2):
        lg = lg_all[d::2]
        m = jnp.where(rel >= 0, jnp.exp(jnp.maximum(rel, 0.0)[None] * lg[:, None, None]), 0.0)
        qdec = jnp.exp((pos + 1.0)[None, :] * lg[:, None])
        kdec = jnp.exp((c - 1.0 - pos)[None, :] * lg[:, None])
        if d == 1:
            m = jnp.swapaxes(m, 1, 2)
            qdec = qdec[:, ::-1]
            kdec = kdec[:, ::-1]
        intra.append(m)
        qk.append(jnp.broadcast_to(jnp.concatenate([qdec, kdec], axis=1)[:, :, None],
                                   (RT_HEADS, 2 * c, 128)))
        sd.append(jnp.broadcast_to(jnp.exp(c * lg)[:, None, None], (RT_HEADS, 8, 128)))
    return (jnp.cos(ang), jnp.sin(ang), jnp.stack(intra), jnp.stack(qk), jnp.stack(sd))


def _retention(p_lat, p_ctx, tables, batch, need_ctx_out):
    l = p_lat.shape[0] // batch
    lc = p_ctx.shape[0] // batch
    cos, sin, intra, qk, sd = tables
    c = RET_CHUNK

    def col(base):
        return lambda b, h: (b, base + h)

    kq, kk_, kv, kg = COL_RQ * 128 // RT_HK, COL_RK * 128 // RT_HK, COL_RV * 128 // RT_HV, COL_RG * 128 // RT_HV
    lat_specs = [pl.BlockSpec((l, RT_HK), col(kq)), pl.BlockSpec((l, RT_HK), col(kk_)),
                 pl.BlockSpec((l, RT_HV), col(kv)), pl.BlockSpec((l, RT_HV), col(kg))]
    if need_ctx_out:
        ctx_specs = [pl.BlockSpec((lc, RT_HK), col(kq)), pl.BlockSpec((lc, RT_HK), col(kk_)),
                     pl.BlockSpec((lc, RT_HV), col(kv)), pl.BlockSpec((lc, RT_HV), col(kg))]
    else:
        ctx_specs = [pl.BlockSpec((lc, RT_HK), col(kk_)), pl.BlockSpec((lc, RT_HV), col(kv))]
    tab_specs = [pl.BlockSpec((lc + l, 128), lambda b, h: (0, 0)),
                 pl.BlockSpec((lc + l, 128), lambda b, h: (0, 0)),
                 pl.BlockSpec((2, 1, c, c), lambda b, h: (0, h, 0, 0)),
                 pl.BlockSpec((2, 1, 2 * c, 128), lambda b, h: (0, h, 0, 0)),
                 pl.BlockSpec((2, 1, 8, 128), lambda b, h: (0, h, 0, 0))]
    args = [p_lat] * 4 + [p_ctx] * len(ctx_specs) + [cos, sin, intra, qk, sd]
    out_specs = [pl.BlockSpec((l, RT_HV), lambda b, h: (b, h))]
    out_shape = [jax.ShapeDtypeStruct((batch * l, RT_HEADS * RT_HV), BF16)]
    scratch = [pltpu.VMEM((l, RT_HK), BF16), pltpu.VMEM((l, RT_HK), BF16)]
    if need_ctx_out:
        out_specs.append(pl.BlockSpec((lc, RT_HV), lambda b, h: (b, h)))
        out_shape.append(jax.ShapeDtypeStruct((batch * lc, RT_HEADS * RT_HV), BF16))
        scratch += [pltpu.VMEM((lc, RT_HK), BF16), pltpu.VMEM((lc, RT_HK), BF16),
                    pltpu.VMEM((l, RT_HV), F32), pltpu.VMEM((lc, RT_HV), F32)]
    else:
        scratch += [pltpu.VMEM((lc, RT_HK), BF16), pltpu.VMEM((l, RT_HV), F32)]
    scratch += [pltpu.VMEM((RT_HK, RT_HV), F32)] * 2
    outs = pl.pallas_call(
        functools.partial(_ret_kernel, need_ctx_out=need_ctx_out, n_lat=l // c, n_ctx=lc // c),
        grid=(batch, RT_HEADS),
        in_specs=lat_specs + ctx_specs + tab_specs,
        out_specs=out_specs,
        out_shape=out_shape,
        scratch_shapes=scratch,
        compiler_params=_cparams(("arbitrary", "arbitrary")),
        name="retention_scan",
    )(*args)
    return outs[0], (outs[1] if need_ctx_out else None)


def _conv_shift_matrices(grid):
    rb = CONV_ROWS
    t = np.arange(rb)
    edge_lo = (t % GRID_W == 0) if grid else (t == 0)
    edge_hi = (t % GRID_W == GRID_W - 1) if grid else (t == rb - 1)
    s0 = ((t[:, None] - 1 == t[None, :]) & ~edge_lo[:, None]).astype(np.float32)
    s2 = ((t[:, None] + 1 == t[None, :]) & ~edge_hi[:, None]).astype(np.float32)
    return s0, s2


def _short_conv_kernel(x_ref, w_ref, b_ref, s0_ref, s2_ref, o_ref, xpad_ref, *, grid):
    l, tc = x_ref.shape
    p = GRID_W
    rb = CONV_ROWS
    zeros = jnp.zeros((p, tc), BF16)
    xpad_ref[0:p, :] = zeros
    xpad_ref[p + l:p + l + p, :] = zeros
    xpad_ref[p:p + l, :] = x_ref[...]
    w = [w_ref[k:k + 1, :].astype(BF16) for k in range(9)]
    rows = range(3) if grid else (1,)

    def body(i, carry):
        r0 = i * rb
        part = []
        for dj in range(3):
            acc = None
            for di in rows:
                start = pl.multiple_of(r0 + p + (di - 1) * GRID_W, GRID_W)
                t = xpad_ref[pl.ds(start, rb), :] * w[di * 3 + dj]
                acc = t if acc is None else acc + t
            part.append(acc)
        out = (part[1].astype(F32) + _dot(s0_ref[...], part[0]) + _dot(s2_ref[...], part[2])
               + b_ref[...])
        o_ref[pl.ds(pl.multiple_of(r0, rb), rb), :] = out.astype(o_ref.dtype)
        return carry

    lax.fori_loop(0, l // rb, body, 0, unroll=min(4, l // rb))


def _short_conv(p_arr, w9, bias, batch, grid):
    l = p_arr.shape[0] // batch
    assert l % CONV_ROWS == 0 and (grid or l == CONV_ROWS)
    c3 = 3 * HY_DIM
    tc = 256
    base = COL_HY * 128 // tc
    s0, s2 = _conv_shift_matrices(grid)
    return pl.pallas_call(
        functools.partial(_short_conv_kernel, grid=grid),
        grid=(batch, c3 // tc),
        in_specs=[pl.BlockSpec((l, tc), lambda b, j: (b, base + j)),
                  pl.BlockSpec((9, tc), lambda b, j: (0, j)),
                  pl.BlockSpec((1, tc), lambda b, j: (0, j)),
                  pl.BlockSpec((CONV_ROWS, CONV_ROWS), lambda b, j: (0, 0)),
                  pl.BlockSpec((CONV_ROWS, CONV_ROWS), lambda b, j: (0, 0))],
        out_specs=pl.BlockSpec((l, tc), lambda b, j: (b, j)),
        out_shape=jax.ShapeDtypeStruct((batch * l, c3), BF16),
        scratch_shapes=[pltpu.VMEM((l + 2 * GRID_W, tc), BF16)],
        compiler_params=_cparams(("arbitrary", "arbitrary")),
        name="hyena_short_conv",
    )(p_arr, w9, bias.reshape(1, c3), jnp.asarray(s0, BF16), jnp.asarray(s2, BF16))


def _dft_matrices(l):
    n = 2 * l
    s = int(round(math.sqrt(n)))
    while n % s:
        s -= 1
    q = n // s
    f = np.arange(l, dtype=np.int64)
    ang_a = 2.0 * np.pi * ((f[:, None] * s * np.arange(q)[None, :]) % n) / n
    ang_b = 2.0 * np.pi * ((f[:, None] * np.arange(s)[None, :]) % n) / n
    t = np.arange(n)
    rep = (t[None, :] // s == np.arange(q)[:, None]).astype(np.float32)
    til = (t[None, :] % s == np.arange(s)[:, None]).astype(np.float32)
    h = DFT_HALF
    row = lambda j: (j, 0)
    const = lambda j: (0, 0)
    fwd, inv = pl.pallas_call(
        _dft_build_kernel,
        grid=(l // h,),
        in_specs=[pl.BlockSpec((h, q), row), pl.BlockSpec((h, q), row),
                  pl.BlockSpec((h, s), row), pl.BlockSpec((h, s), row),
                  pl.BlockSpec((q, n), const), pl.BlockSpec((s, n), const)],
        out_specs=[pl.BlockSpec((1, 2, h, n), lambda j: (j, 0, 0, 0)),
                   pl.BlockSpec((h, n), row)],
        out_shape=[jax.ShapeDtypeStruct((l // h, 2, h, n), BF16),
                   jax.ShapeDtypeStruct((l, n), BF16)],
        compiler_params=_cparams(("arbitrary",)),
        name="dft_build",
    )(jnp.asarray(np.cos(ang_a), F32), jnp.asarray(np.sin(ang_a), F32),
      jnp.asarray(np.cos(ang_b), F32), jnp.asarray(np.sin(ang_b), F32),
      jnp.asarray(rep, BF16), jnp.asarray(til, BF16))
    return fwd.reshape(n, n), inv


def _dft_build_kernel(ca_ref, sa_ref, cb_ref, sb_ref, rep_ref, til_ref, fwd_ref, inv_ref):
    h, n = inv_ref.shape

    def spread(x_ref, m_ref):
        p1, p2, p3 = _split3(x_ref[...])
        m = m_ref[...]
        return _dot(p1, m) + _dot(p2, m) + _dot(p3, m)

    ca, sa = spread(ca_ref, rep_ref), spread(sa_ref, rep_ref)
    cb, sb = spread(cb_ref, til_ref), spread(sb_ref, til_ref)
    cosm = ca * cb - sa * sb
    nsin = -(sa * cb + ca * sb)
    first = pl.program_id(0) == 0
    rowi = lax.broadcasted_iota(jnp.int32, (h, n), 0)
    coli = lax.broadcasted_iota(jnp.int32, (h, n), 1)
    alt_col = (1 - 2 * (coli % 2)).astype(F32)
    fwd_ref[0, 0] = cosm.astype(BF16)
    fwd_ref[0, 1] = jnp.where(first & (rowi == 0), alt_col, nsin).astype(BF16)
    rowg = lax.broadcasted_iota(jnp.int32, (h, h), 0) + pl.program_id(0) * h
    col0 = lax.broadcasted_iota(jnp.int32, (h, h), 1) == 0
    alt_row = (1 - 2 * (rowg % 2)).astype(F32)
    pieces = []
    for jj in range(n // (2 * h)):
        cp = cosm[:, jj * h:(jj + 1) * h] * (2.0 / n)
        ip = nsin[:, jj * h:(jj + 1) * h] * (2.0 / n)
        if jj == 0:
            cp = jnp.where(col0, 1.0 / n, cp)
            ip = jnp.where(col0, alt_row * (1.0 / n), ip)
        pieces += [cp.astype(BF16), ip.astype(BF16)]
    inv_ref[...] = jnp.concatenate(pieces, axis=1)


def _filter_spectrum_kernel(f_ref, top_ref, bot_ref, o_ref):
    h = DFT_HALF
    bl = top_ref.shape[0]
    acc = _dot(f_ref[:, 0:bl], top_ref[...]) + _dot(f_ref[:, bl:2 * bl], bot_ref[...])
    kr, km = acc[:h], acc[h:]
    row0 = (lax.broadcasted_iota(jnp.int32, kr.shape, 0) == 0) & (pl.program_id(2) == 0)
    o_ref[0, 0:h, :] = kr.astype(o_ref.dtype)
    o_ref[0, h:2 * h, :] = jnp.where(row0, 0.0, km).astype(o_ref.dtype)
    o_ref[0, 2 * h:3 * h, :] = jnp.where(row0, km, kr).astype(o_ref.dtype)


def _filter_spectrum(fwd, kern_bf, nb):
    n = fwd.shape[0]
    bl = n // 2
    nc = kern_bf.shape[1]
    tm, tn = 2 * DFT_HALF, 1024
    nd = 2 * nb - 1
    wrap = 2 * nb
    return pl.pallas_call(
        _filter_spectrum_kernel,
        grid=(nc // tn, nd, n // tm),
        in_specs=[pl.BlockSpec((tm, n), lambda c, d, t: (t, 0)),
                  pl.BlockSpec((bl, tn), lambda c, d, t: ((d - (nb - 1)) % wrap, c)),
                  pl.BlockSpec((bl, tn), lambda c, d, t: ((d - nb) % wrap, c))],
        out_specs=pl.BlockSpec((1, 3 * DFT_HALF, tn), lambda c, d, t: (d, t, c)),
        out_shape=jax.ShapeDtypeStruct((nd, (n // tm) * 3 * DFT_HALF, nc), BF16),
        compiler_params=_cparams(("arbitrary", "arbitrary", "arbitrary")),
        name="hyena_filter_dft",
    )(fwd, kern_bf, kern_bf)


def _dft_fwd_kernel(f_ref, z_ref, k_ref, y_ref, u_ref):
    h = DFT_HALF
    bl = f_ref.shape[1]
    nb = y_ref.shape[1]
    tn = y_ref.shape[3]
    f = f_ref[...]
    for j in range(nb):
        u_ref[j] = _dot(f, z_ref[j * bl:(j + 1) * bl, :]).astype(BF16)
    def mix(t, carry):
        r0 = t * MIX_ROWS
        rr = pl.ds(pl.multiple_of(r0, MIX_ROWS), MIX_ROWS)
        ri = pl.ds(pl.multiple_of(h + r0, MIX_ROWS), MIX_ROWS)
        ri2 = pl.ds(pl.multiple_of(2 * h + r0, MIX_ROWS), MIX_ROWS)
        ur = [u_ref[j, rr, :] for j in range(nb)]
        ui = [u_ref[j, ri, :] for j in range(nb)]
        for i in range(nb):
            yr = yi = None
            for j in range(nb):
                d = i - j + nb - 1
                kr, ki, kr2 = k_ref[d, rr, :], k_ref[d, ri, :], k_ref[d, ri2, :]
                tr = ur[j] * kr - ui[j] * ki
                ti = ur[j] * ki + ui[j] * kr2
                yr = tr if yr is None else yr + tr
                yi = ti if yi is None else yi + ti
            y_ref[0, i, rr, :] = yr.astype(y_ref.dtype)
            y_ref[0, i, ri, :] = yi.astype(y_ref.dtype)
        return carry

    lax.fori_loop(0, h // MIX_ROWS, mix, 0)


def _dft_fwd(fwd, z_arr, z_col, kf, k_col, batch, nb):
    n = fwd.shape[0]
    bl = n // 2
    l = nb * bl
    tm, tn = 2 * DFT_HALF, 512
    cpt = HY_DIM // tn
    nd = kf.shape[0]
    return pl.pallas_call(
        _dft_fwd_kernel,
        grid=(cpt, n // tm, batch),
        in_specs=[pl.BlockSpec((tm, bl), lambda c, t, b: (t, 0)),
                  pl.BlockSpec((l, tn), lambda c, t, b: (b, z_col * cpt + c)),
                  pl.BlockSpec((nd, 3 * DFT_HALF, tn), lambda c, t, b: (0, t, k_col * cpt + c))],
        out_specs=pl.BlockSpec((1, nb, tm, tn), lambda c, t, b: (b, 0, t, c)),
        out_shape=jax.ShapeDtypeStruct((batch, nb, n, HY_DIM), BF16),
        scratch_shapes=[pltpu.VMEM((nb, tm, tn), BF16)],
        compiler_params=_cparams(("arbitrary", "arbitrary", "arbitrary")),
        name="hyena_dft_fwd",
    )(fwd, z_arr, kf)


def _dft_inv_kernel(g_ref, y_ref, gate_ref, z_ref, skip_ref, o_ref):
    conv = _dot(g_ref[...], y_ref[0, 0])
    z = z_ref[...].astype(F32)
    o_ref[...] = (gate_ref[...].astype(F32) * (conv + skip_ref[...] * z)).astype(o_ref.dtype)


def _dft_inv(inv, y, gate_arr, gate_col, z_arr, z_col, skip, batch, nb):
    bl, n = inv.shape
    tn = HY_DIM
    return pl.pallas_call(
        _dft_inv_kernel,
        grid=(batch, nb),
        in_specs=[pl.BlockSpec((bl, n), lambda b, i: (0, 0)),
                  pl.BlockSpec((1, 1, n, tn), lambda b, i: (b, i, 0, 0)),
                  pl.BlockSpec((bl, tn), lambda b, i: (b * nb + i, gate_col)),
                  pl.BlockSpec((bl, tn), lambda b, i: (b * nb + i, z_col)),
                  pl.BlockSpec((1, tn), lambda b, i: (0, 0))],
        out_specs=pl.BlockSpec((bl, tn), lambda b, i: (b * nb + i, 0)),
        out_shape=jax.ShapeDtypeStruct((batch * nb * bl, tn), BF16),
        compiler_params=_cparams(("arbitrary", "arbitrary")),
        name="hyena_dft_inv",
    )(inv, y, gate_arr, z_arr, skip.reshape(1, tn))


def _tap_features(l):
    f32 = np.float32
    t01 = np.linspace(0.0, 1.0, l, dtype=f32)[:, None]
    ang = f32(2.0 * math.pi) * np.arange(l, dtype=f32)[:, None] / f32(l)
    bands = np.linspace(1e-4, HY_BANDS - 1, HY_BANDS, dtype=f32)[None, :]
    z = np.concatenate([t01, np.cos(bands * ang), -np.sin(bands * ang)], axis=-1).astype(f32)
    pos = np.concatenate([np.arange(l), [0], np.arange(l - 1, 0, -1)])
    zz = np.zeros((2 * l, 128), f32)
    zz[:, :z.shape[1]] = z[pos]
    zz[l] = 0.0
    return zz


def _dot_hi(a, b):
    a1 = a.astype(BF16)
    a2 = (a - a1.astype(F32)).astype(BF16)
    b1 = b.astype(BF16)
    b2 = (b - b1.astype(F32)).astype(BF16)
    return _dot(a1, b1) + _dot(a1, b2) + _dot(a2, b1)


def _filter_kernel(zz_ref, w1_ref, b1_ref, fr_ref, w2_ref, b2_ref, w3f_ref, w3b_ref, dl_ref,
                   o_ref, hdn_ref):
    n = zz_ref.shape[0]
    l = n // 2

    @pl.when((pl.program_id(0) == 0) & (pl.program_id(1) == 0))
    def _():
        h1 = jnp.sin(fr_ref[0:1, :] * (_dot_hi(zz_ref[...], w1_ref[...]) + b1_ref[...]))
        hdn_ref[...] = jnp.sin(fr_ref[1:2, :] * (_dot_hi(h1, w2_ref[...]) + b2_ref[...]))

    filt = jnp.concatenate([_dot_hi(hdn_ref[0:l, :], w3f_ref[...]),
                            _dot_hi(hdn_ref[l:n, :], w3b_ref[...])], axis=0)
    decay = jnp.exp(-zz_ref[:, 0:1] * dl_ref[...])
    rowi = lax.broadcasted_iota(jnp.int32, filt.shape, 0)
    kern = jnp.where(rowi == l, 0.0, filt * decay)
    o_ref[...] = (kern / jnp.sum(jnp.abs(kern), axis=0, keepdims=True)).astype(o_ref.dtype)


def _hyena_filters(l, w1, b1, freq, w2, b2, w3):
    n = 2 * l
    tn = 256
    ff = w2.shape[0]
    cpo = HY_DIM // tn
    deltas = np.abs(np.linspace(math.log(HY_DECAY_TARGET) / HY_DECAY_PCT_LONG,
                                math.log(HY_DECAY_TARGET) / HY_DECAY_PCT_SHORT, HY_DIM,
                                dtype=np.float32)).reshape(1, HY_DIM)
    const = lambda o, c: (0, 0)
    return pl.pallas_call(
        _filter_kernel,
        grid=(HY_ORDER, cpo),
        in_specs=[pl.BlockSpec((n, 128), const), pl.BlockSpec((128, ff), const),
                  pl.BlockSpec((1, ff), const), pl.BlockSpec((2, ff), const),
                  pl.BlockSpec((ff, ff), const), pl.BlockSpec((1, ff), const),
                  pl.BlockSpec((ff, tn), lambda o, c: (0, o * 2 * cpo + c)),
                  pl.BlockSpec((ff, tn), lambda o, c: (0, o * 2 * cpo + cpo + c)),
                  pl.BlockSpec((1, tn), lambda o, c: (0, c))],
        out_specs=pl.BlockSpec((n, tn), lambda o, c: (0, o * cpo + c)),
        out_shape=jax.ShapeDtypeStruct((n, HY_ORDER * HY_DIM), BF16),
        scratch_shapes=[pltpu.VMEM((n, ff), F32)],
        compiler_params=_cparams(("arbitrary", "arbitrary")),
        name="hyena_filter_mlp",
    )(jnp.asarray(_tap_features(l)), jnp.pad(w1, ((0, 128 - w1.shape[0]), (0, 0))),
      b1.reshape(1, ff), freq, w2, b2.reshape(1, ff), w3, w3, jnp.asarray(deltas))


def _hyena(p_arr, lp, dft, batch, grid):
    l = p_arr.shape[0] // batch
    fwd, inv = dft
    uc = _short_conv(p_arr, lp['hy_conv_w'].reshape(9, 3 * HY_DIM), lp['hy_conv_b'], batch, grid)
    kern = _hyena_filters(l, lp['hy_ff_w1'], lp['hy_ff_b1'], lp['hy_ff_freq'],
                          lp['hy_ff_w2'], lp['hy_ff_b2'], lp['hy_ff_w3'])
    nb = l // inv.shape[0]
    kf = _filter_spectrum(fwd, kern, nb)
    z_arr, z_col = uc, 0
    for n in range(HY_ORDER):
        y = _dft_fwd(fwd, z_arr, z_col, kf, n, batch, nb)
        z_arr = _dft_inv(inv, y, uc, n + 1, z_arr, z_col, lp['hy_skip'][n], batch, nb)
        z_col = 0
    return z_arr


def _merge_kernel(yhy_ref, ohg_ref, hgg_ref, yrt_ref, b0_ref, b1_ref, b2_ref, x_ref, mod_ref,
                  nw_ref, phy_ref, phg_ref, prt_ref, wo_ref, lng_ref, lnb_ref, o_ref):
    o = ohg_ref[...]
    yhg = (o * lax.rsqrt(jnp.mean(o * o, axis=-1, keepdims=True) + LN_EPS) * nw_ref[...]
           * _silu(hgg_ref[...].astype(F32)))
    m = (jax.nn.sigmoid(b0_ref[...].astype(F32)) * _dot(yhy_ref[...], phy_ref[...])
         + jax.nn.sigmoid(b1_ref[...].astype(F32)) * _dot(yhg.astype(BF16), phg_ref[...])
         + jax.nn.sigmoid(b2_ref[...].astype(F32)) * _dot(yrt_ref[...], prt_ref[...]))
    t = _dot(m.astype(BF16), wo_ref[...])
    gt1 = mod_ref[0][2:3]
    o_ref[...] = _layer_norm_rows(DN_ALPHA * x_ref[...] + gt1 * t, lng_ref[...], lnb_ref[...])


def _merge(y_hy, o_hg, y_rt, p_arr, x, modp, lp, wbf, rows_per_batch):
    m = x.shape[0]
    tm = min(512, m)
    d = D_MODEL
    mi = _mod_index(rows_per_batch, tm)
    row = lambda i: (i, 0)
    const = lambda i: (0, 0)
    gcol = COL_HG * 128 // d
    bcol = COL_BR * 128 // d
    return pl.pallas_call(
        _merge_kernel,
        grid=(m // tm,),
        in_specs=[pl.BlockSpec((tm, d), row), pl.BlockSpec((tm, d), row),
                  pl.BlockSpec((tm, d), lambda i: (i, gcol)),
                  pl.BlockSpec((tm, 2 * d), row),
                  pl.BlockSpec((tm, d), lambda i: (i, bcol)),
                  pl.BlockSpec((tm, d), lambda i: (i, bcol + 1)),
                  pl.BlockSpec((tm, d), lambda i: (i, bcol + 2)),
                  pl.BlockSpec((tm, d), row),
                  pl.BlockSpec((1, 8, d), lambda i: (mi(i), 0, 0)),
                  pl.BlockSpec((1, d), const),
                  pl.BlockSpec((d, d), const), pl.BlockSpec((d, d), const),
                  pl.BlockSpec((2 * d, d), const), pl.BlockSpec((d, d), const),
                  pl.BlockSpec((1, d), const), pl.BlockSpec((1, d), const)],
        out_specs=pl.BlockSpec((tm, d), row),
        out_shape=jax.ShapeDtypeStruct((m, d), F32),
        compiler_params=_cparams(("arbitrary",)),
        name="merge_out_ln",
    )(y_hy, o_hg, p_arr, y_rt, p_arr, p_arr, p_arr, x, modp, lp['hg_norm_w'].reshape(1, d),
      wbf['p_hy'], wbf['p_hg'], wbf['p_rt'], wbf['w_o'],
      lp['ln1_g'].reshape(1, d), lp['ln1_b'].reshape(1, d))


def _ffn_kernel(x_ref, mod_ref, w1_ref, w3_ref, w2_ref, lng_ref, lnb_ref, o_ref, h_ref, acc_ref):
    k = pl.program_id(1)
    m = mod_ref[0]

    @pl.when(k == 0)
    def _():
        h_ref[...] = (x_ref[...] * (1.0 + m[4:5]) + m[3:4]).astype(BF16)
        acc_ref[...] = jnp.zeros(acc_ref.shape, F32)

    h = h_ref[...]
    u = _silu(_dot(h, w1_ref[...])) * _dot(h, w3_ref[...])
    acc_ref[...] += _dot(u.astype(BF16), w2_ref[...])

    @pl.when(k == pl.num_programs(1) - 1)
    def _():
        o_ref[...] = _layer_norm_rows(DN_ALPHA * x_ref[...] + m[5:6] * acc_ref[...],
                                      lng_ref[...], lnb_ref[...])


def _ffn_dense(x, modp, w1, w3, w2, ln_g, ln_b, rows_per_batch):
    m = x.shape[0]
    d = D_MODEL
    dff = w1.shape[1]
    tm = min(1024, rows_per_batch or m)
    tf = dff // 2
    mi = _mod_index(rows_per_batch, tm)
    return pl.pallas_call(
        _ffn_kernel,
        grid=(m // tm, dff // tf),
        in_specs=[pl.BlockSpec((tm, d), lambda i, k: (i, 0)),
                  pl.BlockSpec((1, 8, d), lambda i, k: (mi(i), 0, 0)),
                  pl.BlockSpec((d, tf), lambda i, k: (0, k)),
                  pl.BlockSpec((d, tf), lambda i, k: (0, k)),
                  pl.BlockSpec((tf, d), lambda i, k: (k, 0)),
                  pl.BlockSpec((1, d), lambda i, k: (0, 0)),
                  pl.BlockSpec((1, d), lambda i, k: (0, 0))],
        out_specs=pl.BlockSpec((tm, d), lambda i, k: (i, 0)),
        out_shape=jax.ShapeDtypeStruct((m, d), F32),
        scratch_shapes=[pltpu.VMEM((tm, d), BF16), pltpu.VMEM((tm, d), F32)],
        compiler_params=_cparams(("arbitrary", "arbitrary")),
        name="ffn_dense_ln",
    )(x, modp, w1, w3, w2, ln_g.reshape(1, d), ln_b.reshape(1, d))


def _router_kernel(x_ref, mod_ref, r_ref, h_ref, lg_ref):
    m = mod_ref[0]
    h = x_ref[...] * (1.0 + m[4:5]) + m[3:4]
    h_ref[...] = h
    a1, a2, a3 = _split3(h)
    r1, r2, r3 = _split3(r_ref[...])
    lg_ref[...] = (_dot(a1, r1) + _dot(a1, r2) + _dot(a2, r1)
                   + _dot(a2, r2) + _dot(a1, r3) + _dot(a3, r1))


def _router(x, modp, router_pad, rows_per_batch):
    m = x.shape[0]
    d = D_MODEL
    tm = min(512, m)
    mi = _mod_index(rows_per_batch, tm)
    return pl.pallas_call(
        _router_kernel,
        grid=(m // tm,),
        in_specs=[pl.BlockSpec((tm, d), lambda i: (i, 0)),
                  pl.BlockSpec((1, 8, d), lambda i: (mi(i), 0, 0)),
                  pl.BlockSpec((d, 128), lambda i: (0, 0))],
        out_specs=[pl.BlockSpec((tm, d), lambda i: (i, 0)),
                   pl.BlockSpec((tm, 128), lambda i: (i, 0))],
        out_shape=[jax.ShapeDtypeStruct((m, d), F32), jax.ShapeDtypeStruct((m, 128), F32)],
        compiler_params=_cparams(("arbitrary",)),
        name="moe_router",
    )(x, modp, router_pad)


def _moe_ffn_kernel(be_ref, nu_ref, xp_ref, w1_ref, w3_ref, w2_ref, o_ref, x_ref, acc_ref):
    j = pl.program_id(0)
    k = pl.program_id(1)

    @pl.when(j < nu_ref[0])
    def _():
        @pl.when(k == 0)
        def _():
            x_ref[...] = xp_ref[...].astype(BF16)
            acc_ref[...] = jnp.zeros(acc_ref.shape, F32)

        x = x_ref[...]
        u = _silu(_dot(x, w1_ref[0])) * _dot(x, w3_ref[0])
        acc_ref[...] += _dot(u.astype(BF16), w2_ref[0])

        @pl.when(k == pl.num_programs(1) - 1)
        def _():
            o_ref[...] = acc_ref[...]

    @pl.when((j >= nu_ref[0]) & (k == pl.num_programs(1) - 1))
    def _():
        o_ref[...] = jnp.zeros(o_ref.shape, F32)


def _moe_ffn(xb, block_e, n_used, w1, w3, w2):
    ns = xb.shape[0]
    d = D_MODEL
    tm = MOE_ROWS
    dex = w1.shape[2]
    tf = dex // 2
    grid_spec = pltpu.PrefetchScalarGridSpec(
        num_scalar_prefetch=2,
        grid=(ns // tm, dex // tf),
        in_specs=[pl.BlockSpec((tm, d), lambda j, k, be, nu: (j, 0)),
                  pl.BlockSpec((1, d, tf), lambda j, k, be, nu: (be[j], 0, k)),
                  pl.BlockSpec((1, d, tf), lambda j, k, be, nu: (be[j], 0, k)),
                  pl.BlockSpec((1, tf, d), lambda j, k, be, nu: (be[j], k, 0))],
        out_specs=pl.BlockSpec((tm, d), lambda j, k, be, nu: (j, 0)),
        scratch_shapes=[pltpu.VMEM((tm, d), BF16), pltpu.VMEM((tm, d), F32)])
    return pl.pallas_call(
        _moe_ffn_kernel,
        grid_spec=grid_spec,
        out_shape=jax.ShapeDtypeStruct((ns, d), F32),
        compiler_params=_cparams(("arbitrary", "arbitrary")),
        name="moe_expert_ffn",
    )(block_e, n_used, xb, w1, w3, w2)


def _combine_kernel(x_ref, mod_ref, y0_ref, y1_ref, g_ref, lng_ref, lnb_ref, o_ref):
    m = mod_ref[0]
    g = g_ref[...]
    f = g[:, 0:1] * y0_ref[...] + g[:, 1:2] * y1_ref[...]
    o_ref[...] = _layer_norm_rows(DN_ALPHA * x_ref[...] + m[5:6] * f, lng_ref[...], lnb_ref[...])


def _moe_combine(x, modp, y0, y1, gate_pad, ln_g, ln_b, rows_per_batch):
    m = x.shape[0]
    d = D_MODEL
    tm = min(512, m)
    mi = _mod_index(rows_per_batch, tm)
    row = lambda i: (i, 0)
    return pl.pallas_call(
        _combine_kernel,
        grid=(m // tm,),
        in_specs=[pl.BlockSpec((tm, d), row),
                  pl.BlockSpec((1, 8, d), lambda i: (mi(i), 0, 0)),
                  pl.BlockSpec((tm, d), row), pl.BlockSpec((tm, d), row),
                  pl.BlockSpec((tm, 128), row),
                  pl.BlockSpec((1, d), lambda i: (0, 0)), pl.BlockSpec((1, d), lambda i: (0, 0))],
        out_specs=pl.BlockSpec((tm, d), row),
        out_shape=jax.ShapeDtypeStruct((m, d), F32),
        compiler_params=_cparams(("arbitrary",)),
        name="moe_combine_ln",
    )(x, modp, y0, y1, gate_pad, ln_g.reshape(1, d), ln_b.reshape(1, d))


def _cast_kernel(x_ref, o_ref):
    o_ref[...] = x_ref[...].astype(o_ref.dtype)


def _to_bf16(w, group):
    _, e, a, b = w.shape
    ta, tb = (a // 4, b) if a >= b else (a, b // 4)
    cut_rows = a >= b
    return pl.pallas_call(
        _cast_kernel,
        grid=(e, 4),
        in_specs=[pl.BlockSpec((None, 1, ta, tb),
                               lambda i, r: (group, i, r, 0) if cut_rows else (group, i, 0, r))],
        out_specs=pl.BlockSpec((1, ta, tb), lambda i, r: (i, r, 0) if cut_rows else (i, 0, r)),
        out_shape=jax.ShapeDtypeStruct((e, a, b), BF16),
        compiler_params=_cparams(("arbitrary", "arbitrary")),
        name="weight_cast",
    )(w)


def _moe(x, modp, router, w1, w3, w2, ln_g, ln_b, rows_per_batch):
    n, d = x.shape
    e = router.shape[1]
    h2, logits = _router(x, modp, jnp.pad(router, ((0, 0), (0, 128 - e))), rows_per_batch)
    top_val, top_idx = lax.top_k(logits[:, :e], TOP_K)
    gate = jax.nn.softmax(top_val, axis=-1)
    flat_e = top_idx.reshape(-1)
    flat_t = jnp.repeat(jnp.arange(n, dtype=jnp.int32), TOP_K)
    order = jnp.argsort(flat_e, stable=True).astype(jnp.int32)
    rank = jnp.argsort(order).astype(jnp.int32)
    counts = jnp.sum((flat_e[:, None] == jnp.arange(e)[None, :]).astype(jnp.int32), axis=0)
    starts = jnp.cumsum(counts) - counts
    padded = (counts + MOE_ROWS - 1) // MOE_ROWS * MOE_ROWS
    pad_end = jnp.cumsum(padded)
    pad_start = pad_end - padded
    n_blocks = -(-(n * TOP_K) // MOE_ROWS) + e
    n_slots = n_blocks * MOE_ROWS
    block_start = jnp.arange(n_blocks) * MOE_ROWS
    block_e = jnp.minimum(jnp.sum(block_start[:, None] >= pad_end[None, :], axis=1), e - 1).astype(jnp.int32)
    n_used = (pad_end[-1] // MOE_ROWS).astype(jnp.int32).reshape(1)
    slot_e = jnp.repeat(block_e, MOE_ROWS)
    slot_off = jnp.arange(n_slots, dtype=jnp.int32) - pad_start[slot_e]
    slot_valid = slot_off < counts[slot_e]
    slot_src = jnp.where(slot_valid, starts[slot_e] + slot_off, 0)
    slot_tok = jnp.where(slot_valid, flat_t[order[slot_src]], 0).astype(jnp.int32)
    pos = (pad_start[flat_e] + rank - starts[flat_e]).astype(jnp.int32).reshape(n, TOP_K)
    xb = jnp.take(h2, slot_tok, axis=0, mode="clip")
    yb = _moe_ffn(xb, block_e, n_used, w1, w3, w2)
    y0 = jnp.take(yb, pos[:, 0], axis=0, mode="clip")
    y1 = jnp.take(yb, pos[:, 1], axis=0, mode="clip")
    gate_pad = jnp.pad(gate, ((0, 0), (0, 128 - TOP_K)))
    return _moe_combine(x, modp, y0, y1, gate_pad, ln_g, ln_b, rows_per_batch)


def kernel(x, c, ctx, c_ctx, ada_w, ada_b, w_in, hy_conv_w, hy_conv_b, hy_ff_w1, hy_ff_b1, hy_ff_freq, hy_ff_w2, hy_ff_b2, hy_ff_w3, hy_skip, hg_lb_logits, hg_norm_w, p_hy, p_hg, p_rt, w_o, ln1_g, ln1_b, ln2_g, ln2_b, ffn_w1, ffn_w3, ffn_w2, moe_router, moe_w1, moe_w3, moe_w2):
    batch, l, d = x.shape
    lc = ctx.shape[1]
    assert d == D_MODEL and batch <= 8
    assert l % 512 == 0 and l % GRID_W == 0 and lc % max(RET_CHUNK, DFT_HALF) == 0

    cs = jnp.cumsum(jax.nn.softmax(hg_lb_logits.astype(F32), axis=1), axis=1)
    lower_bounds = cs - cs[:, :1]
    cc = jnp.zeros((16, d), F32).at[:batch].set(c).at[8].set(c_ctx)
    ret_tables = _retention_tables(lc, l)
    dft_lat = _dft_matrices(min(HY_BLOCK, l))
    dft_ctx = _dft_matrices(min(HY_BLOCK, lc))

    x_lat = x.reshape(batch * l, d)
    x_ctx = ctx.reshape(batch * lc, d)
    for i in range(DEPTH):
        need_ctx_out = i < DEPTH - 1
        use_moe = i % 2 == 1
        g = i // 2
        lp = {'hy_conv_w': hy_conv_w[i], 'hy_conv_b': hy_conv_b[i], 'hy_ff_w1': hy_ff_w1[i],
              'hy_ff_b1': hy_ff_b1[i], 'hy_ff_freq': hy_ff_freq[i], 'hy_ff_w2': hy_ff_w2[i],
              'hy_ff_b2': hy_ff_b2[i], 'hy_ff_w3': hy_ff_w3[i], 'hy_skip': hy_skip[i],
              'hg_norm_w': hg_norm_w[i], 'ln1_g': ln1_g[i], 'ln1_b': ln1_b[i]}
        wbf = {'p_hy': p_hy[i].astype(BF16), 'p_hg': p_hg[i].astype(BF16),
               'p_rt': p_rt[i].astype(BF16), 'w_o': w_o[i].astype(BF16)}
        mod = _ada(cc, ada_w, i, ada_b[i])
        modp = jnp.pad(mod.reshape(16, 6, d), ((0, 0), (0, 2), (0, 0)))
        lb = lower_bounds[:, i]
        gl_tab = jnp.pad(jnp.stack([lb, 1.0 - lb], axis=1), ((0, 0), (0, 6), (0, 0)))

        p_lat = _proj(x_lat, modp, w_in, i, N_IN_COLS, l)
        p_ctx = _proj(x_ctx, modp, w_in, i, N_IN_COLS if need_ctx_out else N_STATE_COLS, None)

        o_hg_l, o_hg_c = _gla(p_lat, p_ctx, gl_tab, batch, need_ctx_out)
        y_rt_l, y_rt_c = _retention(p_lat, p_ctx, ret_tables, batch, need_ctx_out)
        y_hy_l = _hyena(p_lat, lp, dft_lat, batch, True)
        x_lat_new = _merge(y_hy_l, o_hg_l, y_rt_l, p_lat, x_lat, modp, lp, wbf, l)
        if need_ctx_out:
            y_hy_c = _hyena(p_ctx, lp, dft_ctx, batch, False)
            x_ctx = _merge(y_hy_c, o_hg_c, y_rt_c, p_ctx, x_ctx, modp, lp, wbf, None)
        x_lat = x_lat_new

        if use_moe:
            if need_ctx_out:
                raise NotImplementedError("MoE layer with a context output is not part of this trunk")
            w1, w3, w2 = _to_bf16(moe_w1, g), _to_bf16(moe_w3, g), _to_bf16(moe_w2, g)
            x_lat = _moe(x_lat, modp, moe_router[g], w1, w3, w2, ln2_g[i], ln2_b[i], l)
        else:
            w1, w3, w2 = ffn_w1[g].astype(BF16), ffn_w3[g].astype(BF16), ffn_w2[g].astype(BF16)
            x_lat = _ffn_dense(x_lat, modp, w1, w3, w2, ln2_g[i], ln2_b[i], l)
            if need_ctx_out:
                x_ctx = _ffn_dense(x_ctx, modp, w1, w3, w2, ln2_g[i], ln2_b[i], None)
    return x_lat.reshape(batch, l, d)
```

```python
import functools
import math

import numpy as np
import jax
import jax.numpy as jnp
from jax import lax
from jax.experimental import pallas as pl
from jax.experimental.pallas import tpu as pltpu

F32 = jnp.float32
BF16 = jnp.bfloat16

D_MODEL = 1024
DEPTH = 2
GRID_W = 64
HY_DIM = 1024
HY_ORDER = 2
HY_BANDS = 16
HY_DECAY_TARGET = 1e-2
HY_DECAY_PCT_SHORT = 0.3
HY_DECAY_PCT_LONG = 1.5
HG_HEADS = 8
HG_HK = 128
RT_HEADS = 4
RT_HK = 256
RT_HV = 512
RT_ROPE_BASE = 10000.0
N_STATE_COLS = 6144
N_IN_COLS = 17408
N_EXPERTS = 8
TOP_K = 2
DN_ALPHA = (2 * DEPTH) ** 0.25
LN_EPS = 1e-5

COL_FF, COL_FB, COL_HI, COL_RK, COL_RV = 0, 8, 16, 24, 32
COL_HQ, COL_HG, COL_RQ, COL_RG, COL_HY, COL_BR = 48, 56, 64, 72, 88, 112

GLA_CHUNK = 256
GLA_LEVELS = 8
GLA_PAD = 8
GLA_MIN_LOG2 = -150.0
RET_CHUNK = 256
DFT_HALF = 256
HY_BLOCK = 512
MIX_ROWS = 16
CONV_ROWS = 256
MOE_ROWS = 512
VMEM_LIMIT = 56 * 1024 * 1024


def _cparams(sem, flags=None):
    return pltpu.CompilerParams(dimension_semantics=sem, vmem_limit_bytes=VMEM_LIMIT, flags=flags)


def _silu(x):
    return x * jax.nn.sigmoid(x)


def _layer_norm_rows(r, g, b):
    mu = jnp.mean(r, axis=-1, keepdims=True)
    d = r - mu
    var = jnp.mean(d * d, axis=-1, keepdims=True)
    return d * lax.rsqrt(var + LN_EPS) * g + b


def _split3(x):
    h = x.astype(BF16)
    r = x - h.astype(F32)
    m = r.astype(BF16)
    l = (r - m.astype(F32)).astype(BF16)
    return h, m, l


def _dot(a, b):
    return jnp.dot(a, b, preferred_element_type=F32)


def _dot_nt(a, b):
    return lax.dot_general(a, b, (((1,), (1,)), ((), ())), preferred_element_type=F32)


def _dot_tn(a, b):
    return lax.dot_general(a, b, (((0,), (0,)), ((), ())), preferred_element_type=F32)


def _ada_kernel(c_ref, w_ref, b_ref, o_ref):
    s = _silu(c_ref[...])
    o_ref[...] = _dot(s.astype(BF16), w_ref[...].astype(BF16)) + b_ref[...]


def _ada(cc, w, layer, b):
    n = w.shape[2]
    tn = 1024
    return pl.pallas_call(
        _ada_kernel,
        grid=(n // tn,),
        in_specs=[pl.BlockSpec((16, D_MODEL), lambda j: (0, 0)),
                  pl.BlockSpec((None, D_MODEL, tn), lambda j: (layer, 0, j)),
                  pl.BlockSpec((1, tn), lambda j: (0, j))],
        out_specs=pl.BlockSpec((16, tn), lambda j: (0, j)),
        out_shape=jax.ShapeDtypeStruct((16, n), F32),
        compiler_params=_cparams(("arbitrary",)),
        name="ada_mod",
    )(cc, w, b.reshape(1, n))


def _mod_index(rows_per_batch, tm):
    if rows_per_batch is None:
        return lambda i: 8
    return lambda i: (i * tm) // rows_per_batch


def _proj_kernel(x_ref, mod_ref, w_ref, o_ref, h_ref):
    @pl.when(pl.program_id(1) == 0)
    def _():
        m = mod_ref[0]
        h_ref[...] = (x_ref[...] * (1.0 + m[1:2]) + m[0:1]).astype(BF16)

    o_ref[...] = _dot(h_ref[...], w_ref[...].astype(BF16)).astype(o_ref.dtype)


def _proj(x, modp, w, layer, n_cols, rows_per_batch):
    m = x.shape[0]
    tm = min(2048, rows_per_batch or m)
    tn = 1024
    mi = _mod_index(rows_per_batch, tm)
    return pl.pallas_call(
        _proj_kernel,
        grid=(m // tm, n_cols // tn),
        in_specs=[pl.BlockSpec((tm, D_MODEL), lambda i, j: (i, 0)),
                  pl.BlockSpec((1, 8, D_MODEL), lambda i, j: (mi(i), 0, 0)),
                  pl.BlockSpec((None, D_MODEL, tn), lambda i, j: (layer, 0, j))],
        out_specs=pl.BlockSpec((tm, tn), lambda i, j: (i, j)),
        out_shape=jax.ShapeDtypeStruct((m, n_cols), BF16),
        scratch_shapes=[pltpu.VMEM((tm, D_MODEL), BF16)],
        compiler_params=_cparams(("arbitrary", "arbitrary")),
        name="in_proj",
    )(x, modp, w)


def _gla_gates(fl, gl_ref, d):
    lb = gl_ref[d, 0:1, :]
    oml = gl_ref[d, 1:2, :]
    t = jnp.exp(-jnp.abs(fl))
    r = 1.0 / (1.0 + t)
    tr = t * r
    pos = fl >= 0.0
    f = lb + oml * jnp.where(pos, r, tr)
    key = oml * jnp.where(pos, tr, r)
    return jnp.maximum(jnp.log2(f), GLA_MIN_LOG2), key


def _gla_cumsum(g, rev):
    c = g.shape[0]
    r = lax.broadcasted_iota(jnp.int32, (c, c), 0)
    u = lax.broadcasted_iota(jnp.int32, (c, c), 1)
    tri = jnp.where((u >= r) if rev else (u <= r), 1.0, 0.0).astype(BF16)
    return _dot(jnp.concatenate([tri, tri, tri], axis=1), jnp.concatenate(_split3(g), axis=0))


def _gla_state_update(st, b, kk, vv, rev):
    c = b.shape[0]
    end = 0 if rev else c - 1
    b_end = b[end:end + 1, :]
    khat = (kk * jnp.exp2(b_end - b)).astype(BF16)
    return st * jnp.exp2(b_end) + _dot_tn(vv, khat)


def _gla_level_table():
    c = GLA_CHUNK
    t = np.arange(c)[:, None]
    s = np.arange(c)[None, :]
    top_bit = np.floor(np.log2(np.maximum(t ^ s, 1))).astype(np.int32)
    fwd = np.where(t > s, top_bit, np.where(t == s, GLA_LEVELS, -1))
    return np.stack([fwd, fwd.T]).astype(np.int32)


def _gla_ref_rows(b, bpad_ref, level, rev):
    c = b.shape[0]
    half = 1 << level
    blk = 2 * half
    idx = half if rev else half - 1
    if blk % 8 == 0:
        r = b.reshape(c // blk, blk, HG_HK)[:, idx:idx + 1, :]
        return jnp.broadcast_to(r, (c // blk, blk, HG_HK)).reshape(c, HG_HK)
    m = lax.broadcasted_iota(jnp.int32, (c, HG_HK), 0) % blk
    out = b
    for v in range(blk):
        if v != idx:
            off = GLA_PAD + idx - v
            out = jnp.where(m == v, bpad_ref[off:off + c, :], out)
    return out


def _gla_chunk_out(b, kk, vv, qq, st, rev, lvt, bpad_ref):
    c = GLA_CHUNK
    o = _dot_nt((qq * jnp.exp2(b)).astype(BF16), st.astype(BF16))
    bpad_ref[GLA_PAD:GLA_PAD + c, :] = b
    qq_b, kk_b = qq.astype(BF16), kk.astype(BF16)
    sc = jnp.zeros((c, c), F32)
    for level in range(GLA_LEVELS):
        half = 1 << level
        blk = 2 * half
        ref = _gla_ref_rows(b, bpad_ref, level, rev)
        kt = kk_b * jnp.exp2(ref - b).astype(BF16)
        if half % 8:
            p = _dot_nt(qq_b * jnp.exp2(b - ref).astype(BF16), kt)
            sc = jnp.where(lvt == level, p, sc)
        else:
            nblk = c // blk
            lo, hi = (0, half) if rev else (half, blk)

            def part(x):
                return x.reshape(nblk, blk, x.shape[-1])[:, lo:hi, :]

            qt = (part(qq) * jnp.exp2(part(b) - part(ref))).reshape(c // 2, HG_HK)
            p = _dot_nt(qt.astype(BF16), kt).reshape(nblk, half, c)
            new = jnp.where(part(lvt) == level, p, part(sc))
            rest = sc.reshape(nblk, blk, c)[:, half:blk, :] if rev else sc.reshape(nblk, blk, c)[:, 0:half, :]
            sc = jnp.concatenate([new, rest] if rev else [rest, new], axis=1).reshape(c, c)
    sc = jnp.where(lvt == GLA_LEVELS, _dot_nt(qq_b, kk_b), sc)
    return o + _dot(sc.astype(BF16), vv)


def _gla_kernel(*refs, need_ctx_out, n_lat, n_ctx):
    if need_ctx_out:
        ffl, fbl, il, ql, ffc, fbc, ic, qc, gl_ref, lv_ref, ol_ref, oc_ref = refs[:12]
        bl, kl, bc, kc, qsl, qsc, bpf, bpr = refs[12:]
    else:
        ffl, fbl, il, ql, ffc, fbc, ic, gl_ref, lv_ref, ol_ref = refs[:10]
        bl, kl, bc, kc, qsl, bpf, bpr = refs[10:]
        qc = oc_ref = qsc = None
    c = GLA_CHUNK
    q_scale = HG_HK ** -0.5
    bpf[...] = jnp.zeros(bpf.shape, F32)
    bpr[...] = jnp.zeros(bpr.shape, F32)
    ol_ref[...] = jnp.zeros(ol_ref.shape, F32)
    if need_ctx_out:
        oc_ref[...] = jnp.zeros(oc_ref.shape, F32)

    def chunk_rows(idx):
        return pl.ds(pl.multiple_of(idx * c, c), c)

    def prepare(f_refs, q_ref, b_sc, k_sc, q_sc, n):
        def body(j, carry):
            rows = chunk_rows(j)
            for d in range(2):
                g, kk = _gla_gates(f_refs[d][rows, :].astype(F32), gl_ref, d)
                b_sc[d, rows, :] = _gla_cumsum(g, d == 1)
                k_sc[d, rows, :] = kk
            if q_ref is not None:
                q_sc[rows, :] = _silu(q_ref[rows, :].astype(F32)) * q_scale
            return carry
        lax.fori_loop(0, n, body, 0, unroll=min(2, n))

    prepare((ffc, fbc), qc, bc, kc, qsc, n_ctx)
    prepare((ffl, fbl), ql, bl, kl, qsl, n_lat)

    def one_dir(d, b_sc, k_sc, i_ref, q_sc, o_ref, idx, st):
        rev = d == 1
        rows = chunk_rows(idx)
        b, kk, vv = b_sc[d, rows, :], k_sc[d, rows, :], i_ref[rows, :]
        if q_sc is not None:
            o_ref[rows, :] += _gla_chunk_out(b, kk, vv, q_sc[rows, :], st, rev, lv_ref[d],
                                             bpr if rev else bpf)
        return _gla_state_update(st, b, kk, vv, rev)

    def ctx_body(j, carry):
        return (one_dir(0, bc, kc, ic, qsc, oc_ref, j, carry[0]),
                one_dir(1, bc, kc, ic, qsc, oc_ref, n_ctx - 1 - j, carry[1]))

    def lat_body(j, carry):
        return (one_dir(0, bl, kl, il, qsl, ol_ref, j, carry[0]),
                one_dir(1, bl, kl, il, qsl, ol_ref, n_lat - 1 - j, carry[1]))

    zero = jnp.zeros((HG_HK, HG_HK), F32)
    carry = lax.fori_loop(0, n_ctx, ctx_body, (zero, zero), unroll=min(2, n_ctx))
    lax.fori_loop(0, n_lat, lat_body, carry)


def _gla(p_lat, p_ctx, gl_tab, batch, need_ctx_out):
    l = p_lat.shape[0] // batch
    lc = p_ctx.shape[0] // batch
    w = HG_HK
    c = GLA_CHUNK

    def col(base):
        return lambda b, h: (b, base + h)

    lat_specs = [pl.BlockSpec((l, w), col(cb)) for cb in (COL_FF, COL_FB, COL_HI, COL_HQ)]
    ctx_cols = (COL_FF, COL_FB, COL_HI) + ((COL_HQ,) if need_ctx_out else ())
    ctx_specs = [pl.BlockSpec((lc, w), col(cb)) for cb in ctx_cols]
    in_specs = lat_specs + ctx_specs + [pl.BlockSpec((2, 8, w), lambda b, h: (0, 0, h)),
                                        pl.BlockSpec((2, c, c), lambda b, h: (0, 0, 0))]
    args = [p_lat] * 4 + [p_ctx] * len(ctx_cols) + [gl_tab, jnp.asarray(_gla_level_table())]
    out_specs = [pl.BlockSpec((l, w), lambda b, h: (b, h))]
    out_shape = [jax.ShapeDtypeStruct((batch * l, HG_HEADS * w), F32)]
    scratch = [pltpu.VMEM((2, l, w), F32), pltpu.VMEM((2, l, w), F32),
               pltpu.VMEM((2, lc, w), F32), pltpu.VMEM((2, lc, w), F32), pltpu.VMEM((l, w), F32)]
    if need_ctx_out:
        out_specs.append(pl.BlockSpec((lc, w), lambda b, h: (b, h)))
        out_shape.append(jax.ShapeDtypeStruct((batch * lc, HG_HEADS * w), F32))
        scratch.append(pltpu.VMEM((lc, w), F32))
    scratch += [pltpu.VMEM((c + 2 * GLA_PAD, w), F32)] * 2
    outs = pl.pallas_call(
        functools.partial(_gla_kernel, need_ctx_out=need_ctx_out, n_lat=l // c, n_ctx=lc // c),
        grid=(batch, HG_HEADS),
        in_specs=in_specs,
        out_specs=out_specs,
        out_shape=out_shape,
        scratch_shapes=scratch,
        compiler_params=_cparams(("arbitrary", "arbitrary")),
        name="hgrn2_scan",
    )(*args)
    return outs[0], (outs[1] if need_ctx_out else None)


def _rotate(x, cos, sin):
    half = x.shape[-1] // 2
    x1, x2 = x[:, :half], x[:, half:]
    return jnp.concatenate([x1 * cos - x2 * sin, x1 * sin + x2 * cos], axis=-1)


def _ret_kernel(*refs, need_ctx_out, n_lat, n_ctx):
    if need_ctx_out:
        (ql, kl, vl, gl, qc, kc, vc, gc, cos_ref, sin_ref, intra_ref, qk_ref, sd_ref,
         yl_ref, yc_ref, qrl, krl, qrc, krc, ol, oc, s_fwd, s_rev) = refs
    else:
        (ql, kl, vl, gl, kc, vc, cos_ref, sin_ref, intra_ref, qk_ref, sd_ref,
         yl_ref, qrl, krl, krc, ol, s_fwd, s_rev) = refs
        qc = gc = yc_ref = qrc = oc = None
    s_refs = (s_fwd, s_rev)
    c = RET_CHUNK
    lc = n_ctx * c
    k_scale = RT_HK ** -0.5

    def rot_all(src, dst, n, pos0, scale):
        def body(j, carry):
            rows = pl.ds(pl.multiple_of(j * c, c), c)
            prow = pl.ds(pl.multiple_of(pos0 + j * c, c), c)
            x = src[rows, :].astype(F32)
            dst[rows, :] = (_rotate(x, cos_ref[prow, :], sin_ref[prow, :]) * scale).astype(BF16)
            return carry
        lax.fori_loop(0, n, body, 0)

    rot_all(kl, krl, n_lat, lc, k_scale)
    rot_all(kc, krc, n_ctx, 0, k_scale)
    rot_all(ql, qrl, n_lat, lc, 1.0)
    if need_ctx_out:
        rot_all(qc, qrc, n_ctx, 0, 1.0)

    for s_ref in s_refs:
        s_ref[...] = jnp.zeros(s_ref.shape, F32)
    ol[...] = jnp.zeros(ol.shape, F32)
    if need_ctx_out:
        oc[...] = jnp.zeros(oc.shape, F32)

    def update(d, kr, v_ref, rows):
        kd = qk_ref[d, 0, c:2 * c, :]
        sdec = sd_ref[d, 0, 0:1, :]
        kh = (kr[rows, :].astype(F32) * jnp.concatenate([kd, kd], axis=1)).astype(BF16)
        s_ref = s_refs[d]
        s_ref[...] = (s_ref[...] * jnp.concatenate([sdec] * (RT_HV // 128), axis=1)
                      + _dot_tn(kh, v_ref[rows, :]))

    def out_step(d, qr, kr, v_ref, o_ref, rows):
        qd = qk_ref[d, 0, 0:c, :]
        qcb = qr[rows, :]
        sc = _dot_nt(qcb, kr[rows, :]) * intra_ref[d, 0]
        qh = (qcb.astype(F32) * jnp.concatenate([qd, qd], axis=1)).astype(BF16)
        o_ref[rows, :] += (_dot(qh, s_refs[d][...].astype(BF16))
                           + _dot(sc.astype(BF16), v_ref[rows, :]))
        update(d, kr, v_ref, rows)

    def ctx_body(j, carry):
        for d in range(2):
            idx = (n_ctx - 1 - j) if d == 1 else j
            rows = pl.ds(pl.multiple_of(idx * c, c), c)
            if need_ctx_out:
                out_step(d, qrc, krc, vc, oc, rows)
            else:
                update(d, krc, vc, rows)
        return carry

    def lat_body(j, carry):
        for d in range(2):
            idx = (n_lat - 1 - j) if d == 1 else j
            out_step(d, qrl, krl, vl, ol, pl.ds(pl.multiple_of(idx * c, c), c))
        return carry

    lax.fori_loop(0, n_ctx, ctx_body, 0)
    lax.fori_loop(0, n_lat, lat_body, 0, unroll=4)

    def readout(o_ref, g_ref, y_ref, n):
        def body(j, carry):
            rows = pl.ds(pl.multiple_of(j * c, c), c)
            o = o_ref[rows, :]
            y = o * lax.rsqrt(jnp.mean(o * o, axis=-1, keepdims=True) + LN_EPS)
            y_ref[rows, :] = (y * _silu(g_ref[rows, :].astype(F32))).astype(y_ref.dtype)
            return carry
        lax.fori_loop(0, n, body, 0)

    readout(ol, gl, yl_ref, n_lat)
    if need_ctx_out:
        readout(oc, gc, yc_ref, n_ctx)


def _retention_tables(lc, l):
    half = RT_HK // 2
    inv = 1.0 / (RT_ROPE_BASE ** jnp.linspace(0.0, 1.0, half, dtype=F32))
    ang = jnp.arange(lc + l, dtype=F32)[:, None] * inv[None, :]
    j = jnp.arange(2 * RT_HEADS, dtype=F32)
    lg_all = jnp.log1p(-jnp.exp2(-5.0 - j))
    c = RET_CHUNK
    pos = jnp.arange(c, dtype=F32)
    rel = pos[:, None] - pos[None, :]
    intra, qk, sd = [], [], []
    for d in range(2):
        lg = lg_all[d::2]
        m = jnp.where(rel >= 0, jnp.exp(jnp.maximum(rel, 0.0)[None] * lg[:, None, None]), 0.0)
        qdec = jnp.exp((pos + 1.0)[None, :] * lg[:, None])
        kdec = jnp.exp((c - 1.0 - pos)[None, :] * lg[:, None])
        if d == 1:
            m = jnp.swapaxes(m, 1, 2)
            qdec = qdec[:, ::-1]
            kdec = kdec[:, ::-1]
        intra.append(m)
        qk.append(jnp.broadcast_to(jnp.concatenate([qdec, kdec], axis=1)[:, :, None],
                                   (RT_HEADS, 2 * c, 128)))
        sd.append(jnp.broadcast_to(jnp.exp(c * lg)[:, None, None], (RT_HEADS, 8, 128)))
    return (jnp.cos(ang), jnp.sin(ang), jnp.stack(intra), jnp.stack(qk), jnp.stack(sd))


def _retention(p_lat, p_ctx, tables, batch, need_ctx_out):
    l = p_lat.shape[0] // batch
    lc = p_ctx.shape[0] // batch
    cos, sin, intra, qk, sd = tables
    c = RET_CHUNK

    def col(base):
        return lambda b, h: (b, base + h)

    kq, kk_, kv, kg = COL_RQ * 128 // RT_HK, COL_RK * 128 // RT_HK, COL_RV * 128 // RT_HV, COL_RG * 128 // RT_HV
    lat_specs = [pl.BlockSpec((l, RT_HK), col(kq)), pl.BlockSpec((l, RT_HK), col(kk_)),
                 pl.BlockSpec((l, RT_HV), col(kv)), pl.BlockSpec((l, RT_HV), col(kg))]
    if need_ctx_out:
        ctx_specs = [pl.BlockSpec((lc, RT_HK), col(kq)), pl.BlockSpec((lc, RT_HK), col(kk_)),
                     pl.BlockSpec((lc, RT_HV), col(kv)), pl.BlockSpec((lc, RT_HV), col(kg))]
    else:
        ctx_specs = [pl.BlockSpec((lc, RT_HK), col(kk_)), pl.BlockSpec((lc, RT_HV), col(kv))]
    tab_specs = [pl.BlockSpec((lc + l, 128), lambda b, h: (0, 0)),
                 pl.BlockSpec((lc + l, 128), lambda b, h: (0, 0)),
                 pl.BlockSpec((2, 1, c, c), lambda b, h: (0, h, 0, 0)),
                 pl.BlockSpec((2, 1, 2 * c, 128), lambda b, h: (0, h, 0, 0)),
                 pl.BlockSpec((2, 1, 8, 128), lambda b, h: (0, h, 0, 0))]
    args = [p_lat] * 4 + [p_ctx] * len(ctx_specs) + [cos, sin, intra, qk, sd]
    out_specs = [pl.BlockSpec((l, RT_HV), lambda b, h: (b, h))]
    out_shape = [jax.ShapeDtypeStruct((batch * l, RT_HEADS * RT_HV), BF16)]
    scratch = [pltpu.VMEM((l, RT_HK), BF16), pltpu.VMEM((l, RT_HK), BF16)]
    if need_ctx_out:
        out_specs.append(pl.BlockSpec((lc, RT_HV), lambda b, h: (b, h)))
        out_shape.append(jax.ShapeDtypeStruct((batch * lc, RT_HEADS * RT_HV), BF16))
        scratch += [pltpu.VMEM((lc, RT_HK), BF16), pltpu.VMEM((lc, RT_HK), BF16),
                    pltpu.VMEM((l, RT_HV), F32), pltpu.VMEM((lc, RT_HV), F32)]
    else:
        scratch += [pltpu.VMEM((lc, RT_HK), BF16), pltpu.VMEM((l, RT_HV), F32)]
    scratch += [pltpu.VMEM((RT_HK, RT_HV), F32)] * 2
    outs = pl.pallas_call(
        functools.partial(_ret_kernel, need_ctx_out=need_ctx_out, n_lat=l // c, n_ctx=lc // c),
        grid=(batch, RT_HEADS),
        in_specs=lat_specs + ctx_specs + tab_specs,
        out_specs=out_specs,
        out_shape=out_shape,
        scratch_shapes=scratch,
        compiler_params=_cparams(("arbitrary", "arbitrary")),
        name="retention_scan",
    )(*args)
    return outs[0], (outs[1] if need_ctx_out else None)


def _conv_shift_matrices(grid):
    rb = CONV_ROWS
    t = np.arange(rb)
    edge_lo = (t % GRID_W == 0) if grid else (t == 0)
    edge_hi = (t % GRID_W == GRID_W - 1) if grid else (t == rb - 1)
    s0 = ((t[:, None] - 1 == t[None, :]) & ~edge_lo[:, None]).astype(np.float32)
    s2 = ((t[:, None] + 1 == t[None, :]) & ~edge_hi[:, None]).astype(np.float32)
    return s0, s2


def _short_conv_kernel(x_ref, w_ref, b_ref, s0_ref, s2_ref, o_ref, xpad_ref, *, grid):
    l, tc = x_ref.shape
    p = GRID_W
    rb = CONV_ROWS
    zeros = jnp.zeros((p, tc), BF16)
    xpad_ref[0:p, :] = zeros
    xpad_ref[p + l:p + l + p, :] = zeros
    xpad_ref[p:p + l, :] = x_ref[...]
    w = [w_ref[k:k + 1, :].astype(BF16) for k in range(9)]
    rows = range(3) if grid else (1,)

    def body(i, carry):
        r0 = i * rb
        part = []
        for dj in range(3):
            acc = None
            for di in rows:
                start = pl.multiple_of(r0 + p + (di - 1) * GRID_W, GRID_W)
                t = xpad_ref[pl.ds(start, rb), :] * w[di * 3 + dj]
                acc = t if acc is None else acc + t
            part.append(acc)
        out = (part[1].astype(F32) + _dot(s0_ref[...], part[0]) + _dot(s2_ref[...], part[2])
               + b_ref[...])
        o_ref[pl.ds(pl.multiple_of(r0, rb), rb), :] = out.astype(o_ref.dtype)
        return carry

    lax.fori_loop(0, l // rb, body, 0, unroll=min(4, l // rb))


def _short_conv(p_arr, w9, bias, batch, grid):
    l = p_arr.shape[0] // batch
    assert l % CONV_ROWS == 0 and (grid or l == CONV_ROWS)
    c3 = 3 * HY_DIM
    tc = 256
    base = COL_HY * 128 // tc
    s0, s2 = _conv_shift_matrices(grid)
    return pl.pallas_call(
        functools.partial(_short_conv_kernel, grid=grid),
        grid=(batch, c3 // tc),
        in_specs=[pl.BlockSpec((l, tc), lambda b, j: (b, base + j)),
                  pl.BlockSpec((9, tc), lambda b, j: (0, j)),
                  pl.BlockSpec((1, tc), lambda b, j: (0, j)),
                  pl.BlockSpec((CONV_ROWS, CONV_ROWS), lambda b, j: (0, 0)),
                  pl.BlockSpec((CONV_ROWS, CONV_ROWS), lambda b, j: (0, 0))],
        out_specs=pl.BlockSpec((l, tc), lambda b, j: (b, j)),
        out_shape=jax.ShapeDtypeStruct((batch * l, c3), BF16),
        scratch_shapes=[pltpu.VMEM((l + 2 * GRID_W, tc), BF16)],
        compiler_params=_cparams(("arbitrary", "arbitrary")),
        name="hyena_short_conv",
    )(p_arr, w9, bias.reshape(1, c3), jnp.asarray(s0, BF16), jnp.asarray(s2, BF16))


def _dft_matrices(l):
    n = 2 * l
    s = int(round(math.sqrt(n)))
    while n % s:
        s -= 1
    q = n // s
    f = np.arange(l, dtype=np.int64)
    ang_a = 2.0 * np.pi * ((f[:, None] * s * np.arange(q)[None, :]) % n) / n
    ang_b = 2.0 * np.pi * ((f[:, None] * np.arange(s)[None, :]) % n) / n
    t = np.arange(n)
    rep = (t[None, :] // s == np.arange(q)[:, None]).astype(np.float32)
    til = (t[None, :] % s == np.arange(s)[:, None]).astype(np.float32)
    h = DFT_HALF
    row = lambda j: (j, 0)
    const = lambda j: (0, 0)
    fwd, inv = pl.pallas_call(
        _dft_build_kernel,
        grid=(l // h,),
        in_specs=[pl.BlockSpec((h, q), row), pl.BlockSpec((h, q), row),
                  pl.BlockSpec((h, s), row), pl.BlockSpec((h, s), row),
                  pl.BlockSpec((q, n), const), pl.BlockSpec((s, n), const)],
        out_specs=[pl.BlockSpec((1, 2, h, n), lambda j: (j, 0, 0, 0)),
                   pl.BlockSpec((h, n), row)],
        out_shape=[jax.ShapeDtypeStruct((l // h, 2, h, n), BF16),
                   jax.ShapeDtypeStruct((l, n), BF16)],
        compiler_params=_cparams(("arbitrary",)),
        name="dft_build",
    )(jnp.asarray(np.cos(ang_a), F32), jnp.asarray(np.sin(ang_a), F32),
      jnp.asarray(np.cos(ang_b), F32), jnp.asarray(np.sin(ang_b), F32),
      jnp.asarray(rep, BF16), jnp.asarray(til, BF16))
    return fwd.reshape(n, n), inv


def _dft_build_kernel(ca_ref, sa_ref, cb_ref, sb_ref, rep_ref, til_ref, fwd_ref, inv_ref):
    h, n = inv_ref.shape

    def spread(x_ref, m_ref):
        p1, p2, p3 = _split3(x_ref[...])
        m = m_ref[...]
        return _dot(p1, m) + _dot(p2, m) + _dot(p3, m)

    ca, sa = spread(ca_ref, rep_ref), spread(sa_ref, rep_ref)
    cb, sb = spread(cb_ref, til_ref), spread(sb_ref, til_ref)
    cosm = ca * cb - sa * sb
    nsin = -(sa * cb + ca * sb)
    first = pl.program_id(0) == 0
    rowi = lax.broadcasted_iota(jnp.int32, (h, n), 0)
    coli = lax.broadcasted_iota(jnp.int32, (h, n), 1)
    alt_col = (1 - 2 * (coli % 2)).astype(F32)
    fwd_ref[0, 0] = cosm.astype(BF16)
    fwd_ref[0, 1] = jnp.where(first & (rowi == 0), alt_col, nsin).astype(BF16)
    rowg = lax.broadcasted_iota(jnp.int32, (h, h), 0) + pl.program_id(0) * h
    col0 = lax.broadcasted_iota(jnp.int32, (h, h), 1) == 0
    alt_row = (1 - 2 * (rowg % 2)).astype(F32)
    pieces = []
    for jj in range(n // (2 * h)):
        cp = cosm[:, jj * h:(jj + 1) * h] * (2.0 / n)
        ip = nsin[:, jj * h:(jj + 1) * h] * (2.0 / n)
        if jj == 0:
            cp = jnp.where(col0, 1.0 / n, cp)
            ip = jnp.where(col0, alt_row * (1.0 / n), ip)
        pieces += [cp.astype(BF16), ip.astype(BF16)]
    inv_ref[...] = jnp.concatenate(pieces, axis=1)


def _filter_spectrum_kernel(f_ref, top_ref, bot_ref, o_ref):
    h = DFT_HALF
    bl = top_ref.shape[0]
    acc = _dot(f_ref[:, 0:bl], top_ref[...]) + _dot(f_ref[:, bl:2 * bl], bot_ref[...])
    kr, km = acc[:h], acc[h:]
    row0 = (lax.broadcasted_iota(jnp.int32, kr.shape, 0) == 0) & (pl.program_id(2) == 0)
    o_ref[0, 0:h, :] = kr.astype(o_ref.dtype)
    o_ref[0, h:2 * h, :] = jnp.where(row0, 0.0, km).astype(o_ref.dtype)
    o_ref[0, 2 * h:3 * h, :] = jnp.where(row0, km, kr).astype(o_ref.dtype)


def _filter_spectrum(fwd, kern_bf, nb):
    n = fwd.shape[0]
    bl = n // 2
    nc = kern_bf.shape[1]
    tm, tn = 2 * DFT_HALF, 1024
    nd = 2 * nb - 1
    wrap = 2 * nb
    return pl.pallas_call(
        _filter_spectrum_kernel,
        grid=(nc // tn, nd, n // tm),
        in_specs=[pl.BlockSpec((tm, n), lambda c, d, t: (t, 0)),
                  pl.BlockSpec((bl, tn), lambda c, d, t: ((d - (nb - 1)) % wrap, c)),
                  pl.BlockSpec((bl, tn), lambda c, d, t: ((d - nb) % wrap, c))],
        out_specs=pl.BlockSpec((1, 3 * DFT_HALF, tn), lambda c, d, t: (d, t, c)),
        out_shape=jax.ShapeDtypeStruct((nd, (n // tm) * 3 * DFT_HALF, nc), BF16),
        compiler_params=_cparams(("arbitrary", "arbitrary", "arbitrary")),
        name="hyena_filter_dft",
    )(fwd, kern_bf, kern_bf)


def _dft_fwd_kernel(f_ref, z_ref, k_ref, y_ref, u_ref):
    h = DFT_HALF
    bl = f_ref.shape[1]
    nb = y_ref.shape[1]
    tn = y_ref.shape[3]
    f = f_ref[...]
    for j in range(nb):
        u_ref[j] = _dot(f, z_ref[j * bl:(j + 1) * bl, :]).astype(BF16)
    def mix(t, carry):
        r0 = t * MIX_ROWS
        rr = pl.ds(pl.multiple_of(r0, MIX_ROWS), MIX_ROWS)
        ri = pl.ds(pl.multiple_of(h + r0, MIX_ROWS), MIX_ROWS)
        ri2 = pl.ds(pl.multiple_of(2 * h + r0, MIX_ROWS), MIX_ROWS)
        ur = [u_ref[j, rr, :] for j in range(nb)]
        ui = [u_ref[j, ri, :] for j in range(nb)]
        for i in range(nb):
            yr = yi = None
            for j in range(nb):
                d = i - j + nb - 1
                kr, ki, kr2 = k_ref[d, rr, :], k_ref[d, ri, :], k_ref[d, ri2, :]
                tr = ur[j] * kr - ui[j] * ki
                ti = ur[j] * ki + ui[j] * kr2
                yr = tr if yr is None else yr + tr
                yi = ti if yi is None else yi + ti
            y_ref[0, i, rr, :] = yr.astype(y_ref.dtype)
            y_ref[0, i, ri, :] = yi.astype(y_ref.dtype)
        return carry

    lax.fori_loop(0, h // MIX_ROWS, mix, 0)


def _dft_fwd(fwd, z_arr, z_col, kf, k_col, batch, nb):
    n = fwd.shape[0]
    bl = n // 2
    l = nb * bl
    tm, tn = 2 * DFT_HALF, 512
    cpt = HY_DIM // tn
    nd = kf.shape[0]
    return pl.pallas_call(
        _dft_fwd_kernel,
        grid=(cpt, n // tm, batch),
        in_specs=[pl.BlockSpec((tm, bl), lambda c, t, b: (t, 0)),
                  pl.BlockSpec((l, tn), lambda c, t, b: (b, z_col * cpt + c)),
                  pl.BlockSpec((nd, 3 * DFT_HALF, tn), lambda c, t, b: (0, t, k_col * cpt + c))],
        out_specs=pl.BlockSpec((1, nb, tm, tn), lambda c, t, b: (b, 0, t, c)),
        out_shape=jax.ShapeDtypeStruct((batch, nb, n, HY_DIM), BF16),
        scratch_shapes=[pltpu.VMEM((nb, tm, tn), BF16)],
        compiler_params=_cparams(("arbitrary", "arbitrary", "arbitrary")),
        name="hyena_dft_fwd",
    )(fwd, z_arr, kf)


def _dft_inv_kernel(g_ref, y_ref, gate_ref, z_ref, skip_ref, o_ref):
    conv = _dot(g_ref[...], y_ref[0, 0])
    z = z_ref[...].astype(F32)
    o_ref[...] = (gate_ref[...].astype(F32) * (conv + skip_ref[...] * z)).astype(o_ref.dtype)


def _dft_inv(inv, y, gate_arr, gate_col, z_arr, z_col, skip, batch, nb):
    bl, n = inv.shape
    tn = HY_DIM
    return pl.pallas_call(
        _dft_inv_kernel,
        grid=(batch, nb),
        in_specs=[pl.BlockSpec((bl, n), lambda b, i: (0, 0)),
                  pl.BlockSpec((1, 1, n, tn), lambda b, i: (b, i, 0, 0)),
                  pl.BlockSpec((bl, tn), lambda b, i: (b * nb + i, gate_col)),
                  pl.BlockSpec((bl, tn), lambda b, i: (b * nb + i, z_col)),
                  pl.BlockSpec((1, tn), lambda b, i: (0, 0))],
        out_specs=pl.BlockSpec((bl, tn), lambda b, i: (b * nb + i, 0)),
        out_shape=jax.ShapeDtypeStruct((batch * nb * bl, tn), BF16),
        compiler_params=_cparams(("arbitrary", "arbitrary")),
        name="hyena_dft_inv",
    )(inv, y, gate_arr, z_arr, skip.reshape(1, tn))


def _tap_features(l):
    f32 = np.float32
    t01 = np.linspace(0.0, 1.0, l, dtype=f32)[:, None]
    ang = f32(2.0 * math.pi) * np.arange(l, dtype=f32)[:, None] / f32(l)
    bands = np.linspace(1e-4, HY_BANDS - 1, HY_BANDS, dtype=f32)[None, :]
    z = np.concatenate([t01, np.cos(bands * ang), -np.sin(bands * ang)], axis=-1).astype(f32)
    pos = np.concatenate([np.arange(l), [0], np.arange(l - 1, 0, -1)])
    zz = np.zeros((2 * l, 128), f32)
    zz[:, :z.shape[1]] = z[pos]
    zz[l] = 0.0
    return zz


def _dot_hi(a, b):
    a1 = a.astype(BF16)
    a2 = (a - a1.astype(F32)).astype(BF16)
    b1 = b.astype(BF16)
    b2 = (b - b1.astype(F32)).astype(BF16)
    return _dot(a1, b1) + _dot(a1, b2) + _dot(a2, b1)


def _filter_kernel(zz_ref, w1_ref, b1_ref, fr_ref, w2_ref, b2_ref, w3f_ref, w3b_ref, dl_ref,
                   o_ref, hdn_ref):
    n = zz_ref.shape[0]
    l = n // 2

    @pl.when((pl.program_id(0) == 0) & (pl.program_id(1) == 0))
    def _():
        h1 = jnp.sin(fr_ref[0:1, :] * (_dot_hi(zz_ref[...], w1_ref[...]) + b1_ref[...]))
        hdn_ref[...] = jnp.sin(fr_ref[1:2, :] * (_dot_hi(h1, w2_ref[...]) + b2_ref[...]))

    filt = jnp.concatenate([_dot_hi(hdn_ref[0:l, :], w3f_ref[...]),
                            _dot_hi(hdn_ref[l:n, :], w3b_ref[...])], axis=0)
    decay = jnp.exp(-zz_ref[:, 0:1] * dl_ref[...])
    rowi = lax.broadcasted_iota(jnp.int32, filt.shape, 0)
    kern = jnp.where(rowi == l, 0.0, filt * decay)
    o_ref[...] = (kern / jnp.sum(jnp.abs(kern), axis=0, keepdims=True)).astype(o_ref.dtype)


def _hyena_filters(l, w1, b1, freq, w2, b2, w3):
    n = 2 * l
    tn = 256
    ff = w2.shape[0]
    cpo = HY_DIM // tn
    deltas = np.abs(np.linspace(math.log(HY_DECAY_TARGET) / HY_DECAY_PCT_LONG,
                                math.log(HY_DECAY_TARGET) / HY_DECAY_PCT_SHORT, HY_DIM,
                                dtype=np.float32)).reshape(1, HY_DIM)
    const = lambda o, c: (0, 0)
    return pl.pallas_call(
        _filter_kernel,
        grid=(HY_ORDER, cpo),
        in_specs=[pl.BlockSpec((n, 128), const), pl.BlockSpec((128, ff), const),
                  pl.BlockSpec((1, ff), const), pl.BlockSpec((2, ff), const),
                  pl.BlockSpec((ff, ff), const), pl.BlockSpec((1, ff), const),
                  pl.BlockSpec((ff, tn), lambda o, c: (0, o * 2 * cpo + c)),
                  pl.BlockSpec((ff, tn), lambda o, c: (0, o * 2 * cpo + cpo + c)),
                  pl.BlockSpec((1, tn), lambda o, c: (0, c))],
        out_specs=pl.BlockSpec((n, tn), lambda o, c: (0, o * cpo + c)),
        out_shape=jax.ShapeDtypeStruct((n, HY_ORDER * HY_DIM), BF16),
        scratch_shapes=[pltpu.VMEM((n, ff), F32)],
        compiler_params=_cparams(("arbitrary", "arbitrary")),
        name="hyena_filter_mlp",
    )(jnp.asarray(_tap_features(l)), jnp.pad(w1, ((0, 128 - w1.shape[0]), (0, 0))),
      b1.reshape(1, ff), freq, w2, b2.reshape(1, ff), w3, w3, jnp.asarray(deltas))


def _hyena(p_arr, lp, dft, batch, grid):
    l = p_arr.shape[0] // batch
    fwd, inv = dft
    uc = _short_conv(p_arr, lp['hy_conv_w'].reshape(9, 3 * HY_DIM), lp['hy_conv_b'], batch, grid)
    kern = _hyena_filters(l, lp['hy_ff_w1'], lp['hy_ff_b1'], lp['hy_ff_freq'],
                          lp['hy_ff_w2'], lp['hy_ff_b2'], lp['hy_ff_w3'])
    nb = l // inv.shape[0]
    kf = _filter_spectrum(fwd, kern, nb)
    z_arr, z_col = uc, 0
    for n in range(HY_ORDER):
        y = _dft_fwd(fwd, z_arr, z_col, kf, n, batch, nb)
        z_arr = _dft_inv(inv, y, uc, n + 1, z_arr, z_col, lp['hy_skip'][n], batch, nb)
        z_col = 0
    return z_arr


def _merge_kernel(yhy_ref, ohg_ref, hgg_ref, yrt_ref, b0_ref, b1_ref, b2_ref, x_ref, mod_ref,
                  nw_ref, phy_ref, phg_ref, prt_ref, wo_ref, lng_ref, lnb_ref, o_ref):
    o = ohg_ref[...]
    yhg = (o * lax.rsqrt(jnp.mean(o * o, axis=-1, keepdims=True) + LN_EPS) * nw_ref[...]
           * _silu(hgg_ref[...].astype(F32)))
    m = (jax.nn.sigmoid(b0_ref[...].astype(F32)) * _dot(yhy_ref[...], phy_ref[...])
         + jax.nn.sigmoid(b1_ref[...].astype(F32)) * _dot(yhg.astype(BF16), phg_ref[...])
         + jax.nn.sigmoid(b2_ref[...].astype(F32)) * _dot(yrt_ref[...], prt_ref[...]))
    t = _dot(m.astype(BF16), wo_ref[...])
    gt1 = mod_ref[0][2:3]
    o_ref[...] = _layer_norm_rows(DN_ALPHA * x_ref[...] + gt1 * t, lng_ref[...], lnb_ref[...])


def _merge(y_hy, o_hg, y_rt, p_arr, x, modp, lp, wbf, rows_per_batch):
    m = x.shape[0]
    tm = min(512, m)
    d = D_MODEL
    mi = _mod_index(rows_per_batch, tm)
    row = lambda i: (i, 0)
    const = lambda i: (0, 0)
    gcol = COL_HG * 128 // d
    bcol = COL_BR * 128 // d
    return pl.pallas_call(
        _merge_kernel,
        grid=(m // tm,),
        in_specs=[pl.BlockSpec((tm, d), row), pl.BlockSpec((tm, d), row),
                  pl.BlockSpec((tm, d), lambda i: (i, gcol)),
                  pl.BlockSpec((tm, 2 * d), row),
                  pl.BlockSpec((tm, d), lambda i: (i, bcol)),
                  pl.BlockSpec((tm, d), lambda i: (i, bcol + 1)),
                  pl.BlockSpec((tm, d), lambda i: (i, bcol + 2)),
                  pl.BlockSpec((tm, d), row),
                  pl.BlockSpec((1, 8, d), lambda i: (mi(i), 0, 0)),
                  pl.BlockSpec((1, d), const),
                  pl.BlockSpec((d, d), const), pl.BlockSpec((d, d), const),
                  pl.BlockSpec((2 * d, d), const), pl.BlockSpec((d, d), const),
                  pl.BlockSpec((1, d), const), pl.BlockSpec((1, d), const)],
        out_specs=pl.BlockSpec((tm, d), row),
        out_shape=jax.ShapeDtypeStruct((m, d), F32),
        compiler_params=_cparams(("arbitrary",)),
        name="merge_out_ln",
    )(y_hy, o_hg, p_arr, y_rt, p_arr, p_arr, p_arr, x, modp, lp['hg_norm_w'].reshape(1, d),
      wbf['p_hy'], wbf['p_hg'], wbf['p_rt'], wbf['w_o'],
      lp['ln1_g'].reshape(1, d), lp['ln1_b'].reshape(1, d))


def _ffn_kernel(x_ref, mod_ref, w1_ref, w3_ref, w2_ref, lng_ref, lnb_ref, o_ref, h_ref, acc_ref):
    k = pl.program_id(1)
    m = mod_ref[0]

    @pl.when(k == 0)
    def _():
        h_ref[...] = (x_ref[...] * (1.0 + m[4:5]) + m[3:4]).astype(BF16)
        acc_ref[...] = jnp.zeros(acc_ref.shape, F32)

    h = h_ref[...]
    u = _silu(_dot(h, w1_ref[...])) * _dot(h, w3_ref[...])
    acc_ref[...] += _dot(u.astype(BF16), w2_ref[...])

    @pl.when(k == pl.num_programs(1) - 1)
    def _():
        o_ref[...] = _layer_norm_rows(DN_ALPHA * x_ref[...] + m[5:6] * acc_ref[...],
                                      lng_ref[...], lnb_ref[...])


def _ffn_dense(x, modp, w1, w3, w2, ln_g, ln_b, rows_per_batch):
    m = x.shape[0]
    d = D_MODEL
    dff = w1.shape[1]
    tm = min(1024, rows_per_batch or m)
    tf = dff // 2
    mi = _mod_index(rows_per_batch, tm)
    return pl.pallas_call(
        _ffn_kernel,
        grid=(m // tm, dff // tf),
        in_specs=[pl.BlockSpec((tm, d), lambda i, k: (i, 0)),
                  pl.BlockSpec((1, 8, d), lambda i, k: (mi(i), 0, 0)),
                  pl.BlockSpec((d, tf), lambda i, k: (0, k)),
                  pl.BlockSpec((d, tf), lambda i, k: (0, k)),
                  pl.BlockSpec((tf, d), lambda i, k: (k, 0)),
                  pl.BlockSpec((1, d), lambda i, k: (0, 0)),
                  pl.BlockSpec((1, d), lambda i, k: (0, 0))],
        out_specs=pl.BlockSpec((tm, d), lambda i, k: (i, 0)),
        out_shape=jax.ShapeDtypeStruct((m, d), F32),
        scratch_shapes=[pltpu.VMEM((tm, d), BF16), pltpu.VMEM((tm, d), F32)],
        compiler_params=_cparams(("arbitrary", "arbitrary")),
        name="ffn_dense_ln",
    )(x, modp, w1, w3, w2, ln_g.reshape(1, d), ln_b.reshape(1, d))


def _router_kernel(x_ref, mod_ref, r_ref, h_ref, lg_ref):
    m = mod_ref[0]
    h = x_ref[...] * (1.0 + m[4:5]) + m[3:4]
    h_ref[...] = h
    a1, a2, a3 = _split3(h)
    r1, r2, r3 = _split3(r_ref[...])
    lg_ref[...] = (_dot(a1, r1) + _dot(a1, r2) + _dot(a2, r1)
                   + _dot(a2, r2) + _dot(a1, r3) + _dot(a3, r1))


def _router(x, modp, router_pad, rows_per_batch):
    m = x.shape[0]
    d = D_MODEL
    tm = min(512, m)
    mi = _mod_index(rows_per_batch, tm)
    return pl.pallas_call(
        _router_kernel,
        grid=(m // tm,),
        in_specs=[pl.BlockSpec((tm, d), lambda i: (i, 0)),
                  pl.BlockSpec((1, 8, d), lambda i: (mi(i), 0, 0)),
                  pl.BlockSpec((d, 128), lambda i: (0, 0))],
        out_specs=[pl.BlockSpec((tm, d), lambda i: (i, 0)),
                   pl.BlockSpec((tm, 128), lambda i: (i, 0))],
        out_shape=[jax.ShapeDtypeStruct((m, d), F32), jax.ShapeDtypeStruct((m, 128), F32)],
        compiler_params=_cparams(("arbitrary",)),
        name="moe_router",
    )(x, modp, router_pad)


def _moe_ffn_kernel(be_ref, nu_ref, xp_ref, w1_ref, w3_ref, w2_ref, o_ref, x_ref, acc_ref):
    j = pl.program_id(0)
    k = pl.program_id(1)

    @pl.when(j < nu_ref[0])
    def _():
        @pl.when(k == 0)
        def _():
            x_ref[...] = xp_ref[...].astype(BF16)
            acc_ref[...] = jnp.zeros(acc_ref.shape, F32)

        x = x_ref[...]
        u = _silu(_dot(x, w1_ref[0])) * _dot(x, w3_ref[0])
        acc_ref[...] += _dot(u.astype(BF16), w2_ref[0])

        @pl.when(k == pl.num_programs(1) - 1)
        def _():
            o_ref[...] = acc_ref[...]

    @pl.when((j >= nu_ref[0]) & (k == pl.num_programs(1) - 1))
    def _():
        o_ref[...] = jnp.zeros(o_ref.shape, F32)


def _moe_ffn(xb, block_e, n_used, w1, w3, w2):
    ns = xb.shape[0]
    d = D_MODEL
    tm = MOE_ROWS
    dex = w1.shape[2]
    tf = dex // 2
    grid_spec = pltpu.PrefetchScalarGridSpec(
        num_scalar_prefetch=2,
        grid=(ns // tm, dex // tf),
        in_specs=[pl.BlockSpec((tm, d), lambda j, k, be, nu: (j, 0)),
                  pl.BlockSpec((1, d, tf), lambda j, k, be, nu: (be[j], 0, k)),
                  pl.BlockSpec((1, d, tf), lambda j, k, be, nu: (be[j], 0, k)),
                  pl.BlockSpec((1, tf, d), lambda j, k, be, nu: (be[j], k, 0))],
        out_specs=pl.BlockSpec((tm, d), lambda j, k, be, nu: (j, 0)),
        scratch_shapes=[pltpu.VMEM((tm, d), BF16), pltpu.VMEM((tm, d), F32)])
    return pl.pallas_call(
        _moe_ffn_kernel,
        grid_spec=grid_spec,
        out_shape=jax.ShapeDtypeStruct((ns, d), F32),
        compiler_params=_cparams(("arbitrary", "arbitrary")),
        name="moe_expert_ffn",
    )(block_e, n_used, xb, w1, w3, w2)


def _combine_kernel(x_ref, mod_ref, y0_ref, y1_ref, g_ref, lng_ref, lnb_ref, o_ref):
    m = mod_ref[0]
    g = g_ref[...]
    f = g[:, 0:1] * y0_ref[...] + g[:, 1:2] * y1_ref[...]
    o_ref[...] = _layer_norm_rows(DN_ALPHA * x_ref[...] + m[5:6] * f, lng_ref[...], lnb_ref[...])


def _moe_combine(x, modp, y0, y1, gate_pad, ln_g, ln_b, rows_per_batch):
    m = x.shape[0]
    d = D_MODEL
    tm = min(512, m)
    mi = _mod_index(rows_per_batch, tm)
    row = lambda i: (i, 0)
    return pl.pallas_call(
        _combine_kernel,
        grid=(m // tm,),
        in_specs=[pl.BlockSpec((tm, d), row),
                  pl.BlockSpec((1, 8, d), lambda i: (mi(i), 0, 0)),
                  pl.BlockSpec((tm, d), row), pl.BlockSpec((tm, d), row),
                  pl.BlockSpec((tm, 128), row),
                  pl.BlockSpec((1, d), lambda i: (0, 0)), pl.BlockSpec((1, d), lambda i: (0, 0))],
        out_specs=pl.BlockSpec((tm, d), row),
        out_shape=jax.ShapeDtypeStruct((m, d), F32),
        compiler_params=_cparams(("arbitrary",)),
        name="moe_combine_ln",
    )(x, modp, y0, y1, gate_pad, ln_g.reshape(1, d), ln_b.reshape(1, d))


def _cast_kernel(x_ref, o_ref):
    o_ref[...] = x_ref[...].astype(o_ref.dtype)


def _to_bf16(w, group):
    _, e, a, b = w.shape
    ta, tb = (a // 4, b) if a >= b else (a, b // 4)
    cut_rows = a >= b
    return pl.pallas_call(
        _cast_kernel,
        grid=(e, 4),
        in_specs=[pl.BlockSpec((None, 1, ta, tb),
                               lambda i, r: (group, i, r, 0) if cut_rows else (group, i, 0, r))],
        out_specs=pl.BlockSpec((1, ta, tb), lambda i, r: (i, r, 0) if cut_rows else (i, 0, r)),
        out_shape=jax.ShapeDtypeStruct((e, a, b), BF16),
        compiler_params=_cparams(("arbitrary", "arbitrary")),
        name="weight_cast",
    )(w)


def _moe(x, modp, router, w1, w3, w2, ln_g, ln_b, rows_per_batch):
    n, d = x.shape
    e = router.shape[1]
    h2, logits = _router(x, modp, jnp.pad(router, ((0, 0), (0, 128 - e))), rows_per_batch)
    top_val, top_idx = lax.top_k(logits[:, :e], TOP_K)
    gate = jax.nn.softmax(top_val, axis=-1)
    flat_e = top_idx.reshape(-1)
    flat_t = jnp.repeat(jnp.arange(n, dtype=jnp.int32), TOP_K)
    order = jnp.argsort(flat_e, stable=True).astype(jnp.int32)
    rank = jnp.argsort(order).astype(jnp.int32)
    counts = jnp.sum((flat_e[:, None] == jnp.arange(e)[None, :]).astype(jnp.int32), axis=0)
    starts = jnp.cumsum(counts) - counts
    padded = (counts + MOE_ROWS - 1) // MOE_ROWS * MOE_ROWS
    pad_end = jnp.cumsum(padded)
    pad_start = pad_end - padded
    n_blocks = -(-(n * TOP_K) // MOE_ROWS) + e
    n_slots = n_blocks * MOE_ROWS
    block_start = jnp.arange(n_blocks) * MOE_ROWS
    block_e = jnp.minimum(jnp.sum(block_start[:, None] >= pad_end[None, :], axis=1), e - 1).astype(jnp.int32)
    n_used = (pad_end[-1] // MOE_ROWS).astype(jnp.int32).reshape(1)
    slot_e = jnp.repeat(block_e, MOE_ROWS)
    slot_off = jnp.arange(n_slots, dtype=jnp.int32) - pad_start[slot_e]
    slot_valid = slot_off < counts[slot_e]
    slot_src = jnp.where(slot_valid, starts[slot_e] + slot_off, 0)
    slot_tok = jnp.where(slot_valid, flat_t[order[slot_src]], 0).astype(jnp.int32)
    pos = (pad_start[flat_e] + rank - starts[flat_e]).astype(jnp.int32).reshape(n, TOP_K)
    xb = jnp.take(h2, slot_tok, axis=0, mode="clip")
    yb = _moe_ffn(xb, block_e, n_used, w1, w3, w2)
    y0 = jnp.take(yb, pos[:, 0], axis=0, mode="clip")
    y1 = jnp.take(yb, pos[:, 1], axis=0, mode="clip")
    gate_pad = jnp.pad(gate, ((0, 0), (0, 128 - TOP_K)))
    return _moe_combine(x, modp, y0, y1, gate_pad, ln_g, ln_b, rows_per_batch)


def kernel(x, c, ctx, c_ctx, ada_w, ada_b, w_in, hy_conv_w, hy_conv_b, hy_ff_w1, hy_ff_b1, hy_ff_freq, hy_ff_w2, hy_ff_b2, hy_ff_w3, hy_skip, hg_lb_logits, hg_norm_w, p_hy, p_hg, p_rt, w_o, ln1_g, ln1_b, ln2_g, ln2_b, ffn_w1, ffn_w3, ffn_w2, moe_router, moe_w1, moe_w3, moe_w2):
    batch, l, d = x.shape
    lc = ctx.shape[1]
    assert d == D_MODEL and batch <= 8
    assert l % 512 == 0 and l % GRID_W == 0 and lc % max(RET_CHUNK, DFT_HALF) == 0

    cs = jnp.cumsum(jax.nn.softmax(hg_lb_logits.astype(F32), axis=1), axis=1)
    lower_bounds = cs - cs[:, :1]
    cc = jnp.zeros((16, d), F32).at[:batch].set(c).at[8].set(c_ctx)
    ret_tables = _retention_tables(lc, l)
    dft_lat = _dft_matrices(min(HY_BLOCK, l))
    dft_ctx = _dft_matrices(min(HY_BLOCK, lc))

    x_lat = x.reshape(batch * l, d)
    x_ctx = ctx.reshape(batch * lc, d)
    for i in range(DEPTH):
        need_ctx_out = i < DEPTH - 1
        use_moe = i % 2 == 1
        g = i // 2
        lp = {'hy_conv_w': hy_conv_w[i], 'hy_conv_b': hy_conv_b[i], 'hy_ff_w1': hy_ff_w1[i],
              'hy_ff_b1': hy_ff_b1[i], 'hy_ff_freq': hy_ff_freq[i], 'hy_ff_w2': hy_ff_w2[i],
              'hy_ff_b2': hy_ff_b2[i], 'hy_ff_w3': hy_ff_w3[i], 'hy_skip': hy_skip[i],
              'hg_norm_w': hg_norm_w[i], 'ln1_g': ln1_g[i], 'ln1_b': ln1_b[i]}
        wbf = {'p_hy': p_hy[i].astype(BF16), 'p_hg': p_hg[i].astype(BF16),
               'p_rt': p_rt[i].astype(BF16), 'w_o': w_o[i].astype(BF16)}
        mod = _ada(cc, ada_w, i, ada_b[i])
        modp = jnp.pad(mod.reshape(16, 6, d), ((0, 0), (0, 2), (0, 0)))
        lb = lower_bounds[:, i]
        gl_tab = jnp.pad(jnp.stack([lb, 1.0 - lb], axis=1), ((0, 0), (0, 6), (0, 0)))

        p_lat = _proj(x_lat, modp, w_in, i, N_IN_COLS, l)
        p_ctx = _proj(x_ctx, modp, w_in, i, N_IN_COLS if need_ctx_out else N_STATE_COLS, None)

        o_hg_l, o_hg_c = _gla(p_lat, p_ctx, gl_tab, batch, need_ctx_out)
        y_rt_l, y_rt_c = _retention(p_lat, p_ctx, ret_tables, batch, need_ctx_out)
        y_hy_l = _hyena(p_lat, lp, dft_lat, batch, True)
        x_lat_new = _merge(y_hy_l, o_hg_l, y_rt_l, p_lat, x_lat, modp, lp, wbf, l)
        if need_ctx_out:
            y_hy_c = _hyena(p_ctx, lp, dft_ctx, batch, False)
            x_ctx = _merge(y_hy_c, o_hg_c, y_rt_c, p_ctx, x_ctx, modp, lp, wbf, None)
        x_lat = x_lat_new

        if use_moe:
            if need_ctx_out:
                raise NotImplementedError("MoE layer with a context output is not part of this trunk")
            w1, w3, w2 = _to_bf16(moe_w1, g), _to_bf16(moe_w3, g), _to_bf16(moe_w2, g)
            x_lat = _moe(x_lat, modp, moe_router[g], w1, w3, w2, ln2_g[i], ln2_b[i], l)
        else:
            w1, w3, w2 = ffn_w1[g].astype(BF16), ffn_w3[g].astype(BF16), ffn_w2[g].astype(BF16)
            x_lat = _ffn_dense(x_lat, modp, w1, w3, w2, ln2_g[i], ln2_b[i], l)
            if need_ctx_out:
                x_ctx = _ffn_dense(x_ctx, modp, w1, w3, w2, ln2_g[i], ln2_b[i], None)
    return x_lat.reshape(batch, l, d)
```

```python
import functools
import math

import numpy as np
import jax
import jax.numpy as jnp
from jax import lax
from jax.experimental import pallas as pl
from jax.experimental.pallas import tpu as pltpu

F32 = jnp.float32
BF16 = jnp.bfloat16

D_MODEL = 1024
DEPTH = 2
GRID_W = 64
HY_DIM = 1024
HY_ORDER = 2
HY_BANDS = 16
HY_DECAY_TARGET = 1e-2
HY_DECAY_PCT_SHORT = 0.3
HY_DECAY_PCT_LONG = 1.5
HG_HEADS = 8
HG_HK = 128
RT_HEADS = 4
RT_HK = 256
RT_HV = 512
RT_ROPE_BASE = 10000.0
N_STATE_COLS = 6144
N_IN_COLS = 17408
N_EXPERTS = 8
TOP_K = 2
DN_ALPHA = (2 * DEPTH) ** 0.25
LN_EPS = 1e-5

COL_FF, COL_FB, COL_HI, COL_RK, COL_RV = 0, 8, 16, 24, 32
COL_HQ, COL_HG, COL_RQ, COL_RG, COL_HY, COL_BR = 48, 56, 64, 72, 88, 112

GLA_CHUNK = 256
GLA_LEVELS = 8
GLA_PAD = 8
GLA_MIN_LOG2 = -150.0
RET_CHUNK = 256
DFT_HALF = 256
HY_BLOCK = 512
MIX_ROWS = 16
CONV_ROWS = 256
MOE_ROWS = 1024
VMEM_LIMIT = 56 * 1024 * 1024


def _cparams(sem, flags=None):
    return pltpu.CompilerParams(dimension_semantics=sem, vmem_limit_bytes=VMEM_LIMIT, flags=flags)


def _silu(x):
    return x * jax.nn.sigmoid(x)


def _layer_norm_rows(r, g, b):
    mu = jnp.mean(r, axis=-1, keepdims=True)
    d = r - mu
    var = jnp.mean(d * d, axis=-1, keepdims=True)
    return d * lax.rsqrt(var + LN_EPS) * g + b


def _split3(x):
    h = x.astype(BF16)
    r = x - h.astype(F32)
    m = r.astype(BF16)
    l = (r - m.astype(F32)).astype(BF16)
    return h, m, l


def _dot(a, b):
    return jnp.dot(a, b, preferred_element_type=F32)


def _dot_nt(a, b):
    return lax.dot_general(a, b, (((1,), (1,)), ((), ())), preferred_element_type=F32)


def _dot_tn(a, b):
    return lax.dot_general(a, b, (((0,), (0,)), ((), ())), preferred_element_type=F32)


def _ada_kernel(c_ref, w_ref, b_ref, o_ref):
    s = _silu(c_ref[...])
    o_ref[...] = _dot(s.astype(BF16), w_ref[...].astype(BF16)) + b_ref[...]


def _ada(cc, w, layer, b):
    n = w.shape[2]
    tn = 1024
    return pl.pallas_call(
        _ada_kernel,
        grid=(n // tn,),
        in_specs=[pl.BlockSpec((16, D_MODEL), lambda j: (0, 0)),
                  pl.BlockSpec((None, D_MODEL, tn), lambda j: (layer, 0, j)),
                  pl.BlockSpec((1, tn), lambda j: (0, j))],
        out_specs=pl.BlockSpec((16, tn), lambda j: (0, j)),
        out_shape=jax.ShapeDtypeStruct((16, n), F32),
        compiler_params=_cparams(("arbitrary",)),
        name="ada_mod",
    )(cc, w, b.reshape(1, n))


def _mod_index(rows_per_batch, tm):
    if rows_per_batch is None:
        return lambda i: 8
    return lambda i: (i * tm) // rows_per_batch


def _proj_kernel(x_ref, mod_ref, w_ref, o_ref, h_ref):
    @pl.when(pl.program_id(1) == 0)
    def _():
        m = mod_ref[0]
        h_ref[...] = (x_ref[...] * (1.0 + m[1:2]) + m[0:1]).astype(BF16)

    o_ref[...] = _dot(h_ref[...], w_ref[...].astype(BF16)).astype(o_ref.dtype)


def _proj(x, modp, w, layer, n_cols, rows_per_batch):
    m = x.shape[0]
    tm = min(2048, rows_per_batch or m)
    tn = 1024
    mi = _mod_index(rows_per_batch, tm)
    return pl.pallas_call(
        _proj_kernel,
        grid=(m // tm, n_cols // tn),
        in_specs=[pl.BlockSpec((tm, D_MODEL), lambda i, j: (i, 0)),
                  pl.BlockSpec((1, 8, D_MODEL), lambda i, j: (mi(i), 0, 0)),
                  pl.BlockSpec((None, D_MODEL, tn), lambda i, j: (layer, 0, j))],
        out_specs=pl.BlockSpec((tm, tn), lambda i, j: (i, j)),
        out_shape=jax.ShapeDtypeStruct((m, n_cols), BF16),
        scratch_shapes=[pltpu.VMEM((tm, D_MODEL), BF16)],
        compiler_params=_cparams(("arbitrary", "arbitrary")),
        name="in_proj",
    )(x, modp, w)


def _gla_gates(fl, gl_ref, d):
    lb = gl_ref[d, 0:1, :]
    oml = gl_ref[d, 1:2, :]
    t = jnp.exp(-jnp.abs(fl))
    r = 1.0 / (1.0 + t)
    tr = t * r
    pos = fl >= 0.0
    f = lb + oml * jnp.where(pos, r, tr)
    key = oml * jnp.where(pos, tr, r)
    return jnp.maximum(jnp.log2(f), GLA_MIN_LOG2), key


def _gla_cumsum(g, rev):
    c = g.shape[0]
    r = lax.broadcasted_iota(jnp.int32, (c, c), 0)
    u = lax.broadcasted_iota(jnp.int32, (c, c), 1)
    tri = jnp.where((u >= r) if rev else (u <= r), 1.0, 0.0).astype(BF16)
    return _dot(jnp.concatenate([tri, tri, tri], axis=1), jnp.concatenate(_split3(g), axis=0))


def _gla_state_update(st, b, kk, vv, rev):
    c = b.shape[0]
    end = 0 if rev else c - 1
    b_end = b[end:end + 1, :]
    khat = (kk * jnp.exp2(b_end - b)).astype(BF16)
    return st * jnp.exp2(b_end) + _dot_tn(vv, khat)


def _gla_level_table():
    c = GLA_CHUNK
    t = np.arange(c)[:, None]
    s = np.arange(c)[None, :]
    top_bit = np.floor(np.log2(np.maximum(t ^ s, 1))).astype(np.int32)
    fwd = np.where(t > s, top_bit, np.where(t == s, GLA_LEVELS, -1))
    return np.stack([fwd, fwd.T]).astype(np.int32)


def _gla_ref_rows(b, bpad_ref, level, rev):
    c = b.shape[0]
    half = 1 << level
    blk = 2 * half
    idx = half if rev else half - 1
    if blk % 8 == 0:
        r = b.reshape(c // blk, blk, HG_HK)[:, idx:idx + 1, :]
        return jnp.broadcast_to(r, (c // blk, blk, HG_HK)).reshape(c, HG_HK)
    m = lax.broadcasted_iota(jnp.int32, (c, HG_HK), 0) % blk
    out = b
    for v in range(blk):
        if v != idx:
            off = GLA_PAD + idx - v
            out = jnp.where(m == v, bpad_ref[off:off + c, :], out)
    return out


def _gla_chunk_out(b, kk, vv, qq, st, rev, lvt, bpad_ref):
    c = GLA_CHUNK
    o = _dot_nt((qq * jnp.exp2(b)).astype(BF16), st.astype(BF16))
    bpad_ref[GLA_PAD:GLA_PAD + c, :] = b
    qq_b, kk_b = qq.astype(BF16), kk.astype(BF16)
    sc = jnp.zeros((c, c), F32)
    for level in range(GLA_LEVELS):
        half = 1 << level
        blk = 2 * half
        ref = _gla_ref_rows(b, bpad_ref, level, rev)
        kt = kk_b * jnp.exp2(ref - b).astype(BF16)
        if half % 8:
            p = _dot_nt(qq_b * jnp.exp2(b - ref).astype(BF16), kt)
            sc = jnp.where(lvt == level, p, sc)
        else:
            nblk = c // blk
            lo, hi = (0, half) if rev else (half, blk)

            def part(x):
                return x.reshape(nblk, blk, x.shape[-1])[:, lo:hi, :]

            qt = (part(qq) * jnp.exp2(part(b) - part(ref))).reshape(c // 2, HG_HK)
            p = _dot_nt(qt.astype(BF16), kt).reshape(nblk, half, c)
            new = jnp.where(part(lvt) == level, p, part(sc))
            rest = sc.reshape(nblk, blk, c)[:, half:blk, :] if rev else sc.reshape(nblk, blk, c)[:, 0:half, :]
            sc = jnp.concatenate([new, rest] if rev else [rest, new], axis=1).reshape(c, c)
    sc = jnp.where(lvt == GLA_LEVELS, _dot_nt(qq_b, kk_b), sc)
    return o + _dot(sc.astype(BF16), vv)


def _gla_kernel(*refs, need_ctx_out, n_lat, n_ctx):
    if need_ctx_out:
        ffl, fbl, il, ql, ffc, fbc, ic, qc, gl_ref, lv_ref, ol_ref, oc_ref = refs[:12]
        bl, kl, bc, kc, qsl, qsc, bpf, bpr = refs[12:]
    else:
        ffl, fbl, il, ql, ffc, fbc, ic, gl_ref, lv_ref, ol_ref = refs[:10]
        bl, kl, bc, kc, qsl, bpf, bpr = refs[10:]
        qc = oc_ref = qsc = None
    c = GLA_CHUNK
    q_scale = HG_HK ** -0.5
    bpf[...] = jnp.zeros(bpf.shape, F32)
    bpr[...] = jnp.zeros(bpr.shape, F32)
    ol_ref[...] = jnp.zeros(ol_ref.shape, F32)
    if need_ctx_out:
        oc_ref[...] = jnp.zeros(oc_ref.shape, F32)

    def chunk_rows(idx):
        return pl.ds(pl.multiple_of(idx * c, c), c)

    def prepare(f_refs, q_ref, b_sc, k_sc, q_sc, n):
        def body(j, carry):
            rows = chunk_rows(j)
            for d in range(2):
                g, kk = _gla_gates(f_refs[d][rows, :].astype(F32), gl_ref, d)
                b_sc[d, rows, :] = _gla_cumsum(g, d == 1)
                k_sc[d, rows, :] = kk
            if q_ref is not None:
                q_sc[rows, :] = _silu(q_ref[rows, :].astype(F32)) * q_scale
            return carry
        lax.fori_loop(0, n, body, 0, unroll=min(2, n))

    prepare((ffc, fbc), qc, bc, kc, qsc, n_ctx)
    prepare((ffl, fbl), ql, bl, kl, qsl, n_lat)

    def one_dir(d, b_sc, k_sc, i_ref, q_sc, o_ref, idx, st):
        rev = d == 1
        rows = chunk_rows(idx)
        b, kk, vv = b_sc[d, rows, :], k_sc[d, rows, :], i_ref[rows, :]
        if q_sc is not None:
            o_ref[rows, :] += _gla_chunk_out(b, kk, vv, q_sc[rows, :], st, rev, lv_ref[d],
                                             bpr if rev else bpf)
        return _gla_state_update(st, b, kk, vv, rev)

    def ctx_body(j, carry):
        return (one_dir(0, bc, kc, ic, qsc, oc_ref, j, carry[0]),
                one_dir(1, bc, kc, ic, qsc, oc_ref, n_ctx - 1 - j, carry[1]))

    def lat_body(j, carry):
        return (one_dir(0, bl, kl, il, qsl, ol_ref, j, carry[0]),
                one_dir(1, bl, kl, il, qsl, ol_ref, n_lat - 1 - j, carry[1]))

    zero = jnp.zeros((HG_HK, HG_HK), F32)
    carry = lax.fori_loop(0, n_ctx, ctx_body, (zero, zero), unroll=min(2, n_ctx))
    lax.fori_loop(0, n_lat, lat_body, carry, unroll=min(2, n_lat))


def _gla(p_lat, p_ctx, gl_tab, batch, need_ctx_out):
    l = p_lat.shape[0] // batch
    lc = p_ctx.shape[0] // batch
    w = HG_HK
    c = GLA_CHUNK

    def col(base):
        return lambda b, h: (b, base + h)

    lat_specs = [pl.BlockSpec((l, w), col(cb)) for cb in (COL_FF, COL_FB, COL_HI, COL_HQ)]
    ctx_cols = (COL_FF, COL_FB, COL_HI) + ((COL_HQ,) if need_ctx_out else ())
    ctx_specs = [pl.BlockSpec((lc, w), col(cb)) for cb in ctx_cols]
    in_specs = lat_specs + ctx_specs + [pl.BlockSpec((2, 8, w), lambda b, h: (0, 0, h)),
                                        pl.BlockSpec((2, c, c), lambda b, h: (0, 0, 0))]
    args = [p_lat] * 4 + [p_ctx] * len(ctx_cols) + [gl_tab, jnp.asarray(_gla_level_table())]
    out_specs = [pl.BlockSpec((l, w), lambda b, h: (b, h))]
    out_shape = [jax.ShapeDtypeStruct((batch * l, HG_HEADS * w), F32)]
    scratch = [pltpu.VMEM((2, l, w), F32), pltpu.VMEM((2, l, w), F32),
               pltpu.VMEM((2, lc, w), F32), pltpu.VMEM((2, lc, w), F32), pltpu.VMEM((l, w), F32)]
    if need_ctx_out:
        out_specs.append(pl.BlockSpec((lc, w), lambda b, h: (b, h)))
        out_shape.append(jax.ShapeDtypeStruct((batch * lc, HG_HEADS * w), F32))
        scratch.append(pltpu.VMEM((lc, w), F32))
    scratch += [pltpu.VMEM((c + 2 * GLA_PAD, w), F32)] * 2
    outs = pl.pallas_call(
        functools.partial(_gla_kernel, need_ctx_out=need_ctx_out, n_lat=l // c, n_ctx=lc // c),
        grid=(batch, HG_HEADS),
        in_specs=in_specs,
        out_specs=out_specs,
        out_shape=out_shape,
        scratch_shapes=scratch,
        compiler_params=_cparams(("arbitrary", "arbitrary")),
        name="hgrn2_scan",
    )(*args)
    return outs[0], (outs[1] if need_ctx_out else None)


def _rotate(x, cos, sin):
    half = x.shape[-1] // 2
    x1, x2 = x[:, :half], x[:, half:]
    return jnp.concatenate([x1 * cos - x2 * sin, x1 * sin + x2 * cos], axis=-1)


def _ret_kernel(*refs, need_ctx_out, n_lat, n_ctx):
    if need_ctx_out:
        (ql, kl, vl, gl, qc, kc, vc, gc, cos_ref, sin_ref, intra_ref, qk_ref, sd_ref,
         yl_ref, yc_ref, qrl, krl, qrc, krc, ol, oc, s_fwd, s_rev) = refs
    else:
        (ql, kl, vl, gl, kc, vc, cos_ref, sin_ref, intra_ref, qk_ref, sd_ref,
         yl_ref, qrl, krl, krc, ol, s_fwd, s_rev) = refs
        qc = gc = yc_ref = qrc = oc = None
    s_refs = (s_fwd, s_rev)
    c = RET_CHUNK
    lc = n_ctx * c
    k_scale = RT_HK ** -0.5

    def rot_all(src, dst, n, pos0, scale):
        def body(j, carry):
            rows = pl.ds(pl.multiple_of(j * c, c), c)
            prow = pl.ds(pl.multiple_of(pos0 + j * c, c), c)
            x = src[rows, :].astype(F32)
            dst[rows, :] = (_rotate(x, cos_ref[prow, :], sin_ref[prow, :]) * scale).astype(BF16)
            return carry
        lax.fori_loop(0, n, body, 0)

    rot_all(kl, krl, n_lat, lc, k_scale)
    rot_all(kc, krc, n_ctx, 0, k_scale)
    rot_all(ql, qrl, n_lat, lc, 1.0)
    if need_ctx_out:
        rot_all(qc, qrc, n_ctx, 0, 1.0)

    for s_ref in s_refs:
        s_ref[...] = jnp.zeros(s_ref.shape, F32)
    ol[...] = jnp.zeros(ol.shape, F32)
    if need_ctx_out:
        oc[...] = jnp.zeros(oc.shape, F32)

    def update(d, kr, v_ref, rows):
        kd = qk_ref[d, 0, c:2 * c, :]
        sdec = sd_ref[d, 0, 0:1, :]
        kh = (kr[rows, :].astype(F32) * jnp.concatenate([kd, kd], axis=1)).astype(BF16)
        s_ref = s_refs[d]
        s_ref[...] = (s_ref[...] * jnp.concatenate([sdec] * (RT_HV // 128), axis=1)
                      + _dot_tn(kh, v_ref[rows, :]))

    def out_step(d, qr, kr, v_ref, o_ref, rows):
        qd = qk_ref[d, 0, 0:c, :]
        qcb = qr[rows, :]
        sc = _dot_nt(qcb, kr[rows, :]) * intra_ref[d, 0]
        qh = (qcb.astype(F32) * jnp.concatenate([qd, qd], axis=1)).astype(BF16)
        o_ref[rows, :] += (_dot(qh, s_refs[d][...].astype(BF16))
                           + _dot(sc.astype(BF16), v_ref[rows, :]))
        update(d, kr, v_ref, rows)

    def ctx_body(j, carry):
        for d in range(2):
            idx = (n_ctx - 1 - j) if d == 1 else j
            rows = pl.ds(pl.multiple_of(idx * c, c), c)
            if need_ctx_out:
                out_step(d, qrc, krc, vc, oc, rows)
            else:
                update(d, krc, vc, rows)
        return carry

    def lat_body(j, carry):
        for d in range(2):
            idx = (n_lat - 1 - j) if d == 1 else j
            out_step(d, qrl, krl, vl, ol, pl.ds(pl.multiple_of(idx * c, c), c))
        return carry

    lax.fori_loop(0, n_ctx, ctx_body, 0)
    lax.fori_loop(0, n_lat, lat_body, 0, unroll=4)

    def readout(o_ref, g_ref, y_ref, n):
        def body(j, carry):
            rows = pl.ds(pl.multiple_of(j * c, c), c)
            o = o_ref[rows, :]
            y = o * lax.rsqrt(jnp.mean(o * o, axis=-1, keepdims=True) + LN_EPS)
            y_ref[rows, :] = (y * _silu(g_ref[rows, :].astype(F32))).astype(y_ref.dtype)
            return carry
        lax.fori_loop(0, n, body, 0)

    readout(ol, gl, yl_ref, n_lat)
    if need_ctx_out:
        readout(oc, gc, yc_ref, n_ctx)


def _retention_tables(lc, l):
    half = RT_HK // 2
    inv = 1.0 / (RT_ROPE_BASE ** jnp.linspace(0.0, 1.0, half, dtype=F32))
    ang = jnp.arange(lc + l, dtype=F32)[:, None] * inv[None, :]
    j = jnp.arange(2 * RT_HEADS, dtype=F32)
    lg_all = jnp.log1p(-jnp.exp2(-5.0 - j))
    c = RET_CHUNK
    pos = jnp.arange(c, dtype=F32)
    rel = pos[:, None] - pos[None, :]
    intra, qk, sd = [], [], []
    for d in range(2):
        lg = lg_all[d::2]
        m = jnp.where(rel >= 0, jnp.exp(jnp.maximum(rel, 0.0)[None] * lg[:, None, None]), 0.0)
        qdec = jnp.exp((pos + 1.0)[None, :] * lg[:, None])
        kdec = jnp.exp((c - 1.0 - pos)[None, :] * lg[:, None])
        if d == 1:
            m = jnp.swapaxes(m, 1, 2)
            qdec = qdec[:, ::-1]
            kdec = kdec[:, ::-1]
        intra.append(m)
        qk.append(jnp.broadcast_to(jnp.concatenate([qdec, kdec], axis=1)[:, :, None],
                                   (RT_HEADS, 2 * c, 128)))
        sd.append(jnp.broadcast_to(jnp.exp(c * lg)[:, None, None], (RT_HEADS, 8, 128)))
    return (jnp.cos(ang), jnp.sin(ang), jnp.stack(intra), jnp.stack(qk), jnp.stack(sd))


def _retention(p_lat, p_ctx, tables, batch, need_ctx_out):
    l = p_lat.shape[0] // batch
    lc = p_ctx.shape[0] // batch
    cos, sin, intra, qk, sd = tables
    c = RET_CHUNK

    def col(base):
        return lambda b, h: (b, base + h)

    kq, kk_, kv, kg = COL_RQ * 128 // RT_HK, COL_RK * 128 // RT_HK, COL_RV * 128 // RT_HV, COL_RG * 128 // RT_HV
    lat_specs = [pl.BlockSpec((l, RT_HK), col(kq)), pl.BlockSpec((l, RT_HK), col(kk_)),
                 pl.BlockSpec((l, RT_HV), col(kv)), pl.BlockSpec((l, RT_HV), col(kg))]
    if need_ctx_out:
        ctx_specs = [pl.BlockSpec((lc, RT_HK), col(kq)), pl.BlockSpec((lc, RT_HK), col(kk_)),
                     pl.BlockSpec((lc, RT_HV), col(kv)), pl.BlockSpec((lc, RT_HV), col(kg))]
    else:
        ctx_specs = [pl.BlockSpec((lc, RT_HK), col(kk_)), pl.BlockSpec((lc, RT_HV), col(kv))]
    tab_specs = [pl.BlockSpec((lc + l, 128), lambda b, h: (0, 0)),
                 pl.BlockSpec((lc + l, 128), lambda b, h: (0, 0)),
                 pl.BlockSpec((2, 1, c, c), lambda b, h: (0, h, 0, 0)),
                 pl.BlockSpec((2, 1, 2 * c, 128), lambda b, h: (0, h, 0, 0)),
                 pl.BlockSpec((2, 1, 8, 128), lambda b, h: (0, h, 0, 0))]
    args = [p_lat] * 4 + [p_ctx] * len(ctx_specs) + [cos, sin, intra, qk, sd]
    out_specs = [pl.BlockSpec((l, RT_HV), lambda b, h: (b, h))]
    out_shape = [jax.ShapeDtypeStruct((batch * l, RT_HEADS * RT_HV), BF16)]
    scratch = [pltpu.VMEM((l, RT_HK), BF16), pltpu.VMEM((l, RT_HK), BF16)]
    if need_ctx_out:
        out_specs.append(pl.BlockSpec((lc, RT_HV), lambda b, h: (b, h)))
        out_shape.append(jax.ShapeDtypeStruct((batch * lc, RT_HEADS * RT_HV), BF16))
        scratch += [pltpu.VMEM((lc, RT_HK), BF16), pltpu.VMEM((lc, RT_HK), BF16),
                    pltpu.VMEM((l, RT_HV), F32), pltpu.VMEM((lc, RT_HV), F32)]
    else:
        scratch += [pltpu.VMEM((lc, RT_HK), BF16), pltpu.VMEM((l, RT_HV), F32)]
    scratch += [pltpu.VMEM((RT_HK, RT_HV), F32)] * 2
    outs = pl.pallas_call(
        functools.partial(_ret_kernel, need_ctx_out=need_ctx_out, n_lat=l // c, n_ctx=lc // c),
        grid=(batch, RT_HEADS),
        in_specs=lat_specs + ctx_specs + tab_specs,
        out_specs=out_specs,
        out_shape=out_shape,
        scratch_shapes=scratch,
        compiler_params=_cparams(("arbitrary", "arbitrary")),
        name="retention_scan",
    )(*args)
    return outs[0], (outs[1] if need_ctx_out else None)


def _conv_shift_matrices(grid):
    rb = CONV_ROWS
    t = np.arange(rb)
    edge_lo = (t % GRID_W == 0) if grid else (t == 0)
    edge_hi = (t % GRID_W == GRID_W - 1) if grid else (t == rb - 1)
    s0 = ((t[:, None] - 1 == t[None, :]) & ~edge_lo[:, None]).astype(np.float32)
    s2 = ((t[:, None] + 1 == t[None, :]) & ~edge_hi[:, None]).astype(np.float32)
    return s0, s2


def _short_conv_kernel(x_ref, w_ref, b_ref, s0_ref, s2_ref, o_ref, xpad_ref, *, grid):
    l, tc = x_ref.shape
    p = GRID_W
    rb = CONV_ROWS
    zeros = jnp.zeros((p, tc), BF16)
    xpad_ref[0:p, :] = zeros
    xpad_ref[p + l:p + l + p, :] = zeros
    xpad_ref[p:p + l, :] = x_ref[...]
    w = [w_ref[k:k + 1, :].astype(BF16) for k in range(9)]
    rows = range(3) if grid else (1,)

    def body(i, carry):
        r0 = i * rb
        part = []
        for dj in range(3):
            acc = None
            for di in rows:
                start = pl.multiple_of(r0 + p + (di - 1) * GRID_W, GRID_W)
                t = xpad_ref[pl.ds(start, rb), :] * w[di * 3 + dj]
                acc = t if acc is None else acc + t
            part.append(acc)
        out = (part[1].astype(F32) + _dot(s0_ref[...], part[0]) + _dot(s2_ref[...], part[2])
               + b_ref[...])
        o_ref[pl.ds(pl.multiple_of(r0, rb), rb), :] = out.astype(o_ref.dtype)
        return carry

    lax.fori_loop(0, l // rb, body, 0, unroll=min(4, l // rb))


def _short_conv(p_arr, w9, bias, batch, grid):
    l = p_arr.shape[0] // batch
    assert l % CONV_ROWS == 0 and (grid or l == CONV_ROWS)
    c3 = 3 * HY_DIM
    tc = 256
    base = COL_HY * 128 // tc
    s0, s2 = _conv_shift_matrices(grid)
    return pl.pallas_call(
        functools.partial(_short_conv_kernel, grid=grid),
        grid=(batch, c3 // tc),
        in_specs=[pl.BlockSpec((l, tc), lambda b, j: (b, base + j)),
                  pl.BlockSpec((9, tc), lambda b, j: (0, j)),
                  pl.BlockSpec((1, tc), lambda b, j: (0, j)),
                  pl.BlockSpec((CONV_ROWS, CONV_ROWS), lambda b, j: (0, 0)),
                  pl.BlockSpec((CONV_ROWS, CONV_ROWS), lambda b, j: (0, 0))],
        out_specs=pl.BlockSpec((l, tc), lambda b, j: (b, j)),
        out_shape=jax.ShapeDtypeStruct((batch * l, c3), BF16),
        scratch_shapes=[pltpu.VMEM((l + 2 * GRID_W, tc), BF16)],
        compiler_params=_cparams(("arbitrary", "arbitrary")),
        name="hyena_short_conv",
    )(p_arr, w9, bias.reshape(1, c3), jnp.asarray(s0, BF16), jnp.asarray(s2, BF16))


def _dft_matrices(l):
    n = 2 * l
    s = int(round(math.sqrt(n)))
    while n % s:
        s -= 1
    q = n // s
    f = np.arange(l, dtype=np.int64)
    ang_a = 2.0 * np.pi * ((f[:, None] * s * np.arange(q)[None, :]) % n) / n
    ang_b = 2.0 * np.pi * ((f[:, None] * np.arange(s)[None, :]) % n) / n
    t = np.arange(n)
    rep = (t[None, :] // s == np.arange(q)[:, None]).astype(np.float32)
    til = (t[None, :] % s == np.arange(s)[:, None]).astype(np.float32)
    h = DFT_HALF
    row = lambda j: (j, 0)
    const = lambda j: (0, 0)
    fwd, inv = pl.pallas_call(
        _dft_build_kernel,
        grid=(l // h,),
        in_specs=[pl.BlockSpec((h, q), row), pl.BlockSpec((h, q), row),
                  pl.BlockSpec((h, s), row), pl.BlockSpec((h, s), row),
                  pl.BlockSpec((q, n), const), pl.BlockSpec((s, n), const)],
        out_specs=[pl.BlockSpec((1, 2, h, n), lambda j: (j, 0, 0, 0)),
                   pl.BlockSpec((h, n), row)],
        out_shape=[jax.ShapeDtypeStruct((l // h, 2, h, n), BF16),
                   jax.ShapeDtypeStruct((l, n), BF16)],
        compiler_params=_cparams(("arbitrary",)),
        name="dft_build",
    )(jnp.asarray(np.cos(ang_a), F32), jnp.asarray(np.sin(ang_a), F32),
      jnp.asarray(np.cos(ang_b), F32), jnp.asarray(np.sin(ang_b), F32),
      jnp.asarray(rep, BF16), jnp.asarray(til, BF16))
    return fwd.reshape(n, n), inv


def _dft_build_kernel(ca_ref, sa_ref, cb_ref, sb_ref, rep_ref, til_ref, fwd_ref, inv_ref):
    h, n = inv_ref.shape

    def spread(x_ref, m_ref):
        p1, p2, p3 = _split3(x_ref[...])
        m = m_ref[...]
        return _dot(p1, m) + _dot(p2, m) + _dot(p3, m)

    ca, sa = spread(ca_ref, rep_ref), spread(sa_ref, rep_ref)
    cb, sb = spread(cb_ref, til_ref), spread(sb_ref, til_ref)
    cosm = ca * cb - sa * sb
    nsin = -(sa * cb + ca * sb)
    first = pl.program_id(0) == 0
    rowi = lax.broadcasted_iota(jnp.int32, (h, n), 0)
    coli = lax.broadcasted_iota(jnp.int32, (h, n), 1)
    alt_col = (1 - 2 * (coli % 2)).astype(F32)
    fwd_ref[0, 0] = cosm.astype(BF16)
    fwd_ref[0, 1] = jnp.where(first & (rowi == 0), alt_col, nsin).astype(BF16)
    rowg = lax.broadcasted_iota(jnp.int32, (h, h), 0) + pl.program_id(0) * h
    col0 = lax.broadcasted_iota(jnp.int32, (h, h), 1) == 0
    alt_row = (1 - 2 * (rowg % 2)).astype(F32)
    pieces = []
    for jj in range(n // (2 * h)):
        cp = cosm[:, jj * h:(jj + 1) * h] * (2.0 / n)
        ip = nsin[:, jj * h:(jj + 1) * h] * (2.0 / n)
        if jj == 0:
            cp = jnp.where(col0, 1.0 / n, cp)
            ip = jnp.where(col0, alt_row * (1.0 / n), ip)
        pieces += [cp.astype(BF16), ip.astype(BF16)]
    inv_ref[...] = jnp.concatenate(pieces, axis=1)


def _filter_spectrum_kernel(f_ref, top_ref, bot_ref, o_ref):
    h = DFT_HALF
    bl = top_ref.shape[0]
    acc = _dot(f_ref[:, 0:bl], top_ref[...]) + _dot(f_ref[:, bl:2 * bl], bot_ref[...])
    kr, km = acc[:h], acc[h:]
    row0 = (lax.broadcasted_iota(jnp.int32, kr.shape, 0) == 0) & (pl.program_id(2) == 0)
    o_ref[0, 0:h, :] = kr.astype(o_ref.dtype)
    o_ref[0, h:2 * h, :] = jnp.where(row0, 0.0, km).astype(o_ref.dtype)
    o_ref[0, 2 * h:3 * h, :] = jnp.where(row0, km, kr).astype(o_ref.dtype)


def _filter_spectrum(fwd, kern_bf, nb):
    n = fwd.shape[0]
    bl = n // 2
    nc = kern_bf.shape[1]
    tm, tn = 2 * DFT_HALF, 1024
    nd = 2 * nb - 1
    wrap = 2 * nb
    return pl.pallas_call(
        _filter_spectrum_kernel,
        grid=(nc // tn, nd, n // tm),
        in_specs=[pl.BlockSpec((tm, n), lambda c, d, t: (t, 0)),
                  pl.BlockSpec((bl, tn), lambda c, d, t: ((d - (nb - 1)) % wrap, c)),
                  pl.BlockSpec((bl, tn), lambda c, d, t: ((d - nb) % wrap, c))],
        out_specs=pl.BlockSpec((1, 3 * DFT_HALF, tn), lambda c, d, t: (d, t, c)),
        out_shape=jax.ShapeDtypeStruct((nd, (n // tm) * 3 * DFT_HALF, nc), BF16),
        compiler_params=_cparams(("arbitrary", "arbitrary", "arbitrary")),
        name="hyena_filter_dft",
    )(fwd, kern_bf, kern_bf)


def _dft_fwd_kernel(f_ref, z_ref, k_ref, y_ref, u_ref):
    h = DFT_HALF
    bl = f_ref.shape[1]
    nb = y_ref.shape[1]
    tn = y_ref.shape[3]
    f = f_ref[...]
    for j in range(nb):
        u_ref[j] = _dot(f, z_ref[j * bl:(j + 1) * bl, :]).astype(BF16)
    def mix(t, carry):
        r0 = t * MIX_ROWS
        rr = pl.ds(pl.multiple_of(r0, MIX_ROWS), MIX_ROWS)
        ri = pl.ds(pl.multiple_of(h + r0, MIX_ROWS), MIX_ROWS)
        ri2 = pl.ds(pl.multiple_of(2 * h + r0, MIX_ROWS), MIX_ROWS)
        ur = [u_ref[j, rr, :] for j in range(nb)]
        ui = [u_ref[j, ri, :] for j in range(nb)]
        for i in range(nb):
            yr = yi = None
            for j in range(nb):
                d = i - j + nb - 1
                kr, ki, kr2 = k_ref[d, rr, :], k_ref[d, ri, :], k_ref[d, ri2, :]
                tr = ur[j] * kr - ui[j] * ki
                ti = ur[j] * ki + ui[j] * kr2
                yr = tr if yr is None else yr + tr
                yi = ti if yi is None else yi + ti
            y_ref[0, i, rr, :] = yr.astype(y_ref.dtype)
            y_ref[0, i, ri, :] = yi.astype(y_ref.dtype)
        return carry

    lax.fori_loop(0, h // MIX_ROWS, mix, 0)


def _dft_fwd(fwd, z_arr, z_col, kf, k_col, batch, nb):
    n = fwd.shape[0]
    bl = n // 2
    l = nb * bl
    tm, tn = 2 * DFT_HALF, 512
    cpt = HY_DIM // tn
    nd = kf.shape[0]
    return pl.pallas_call(
        _dft_fwd_kernel,
        grid=(cpt, n // tm, batch),
        in_specs=[pl.BlockSpec((tm, bl), lambda c, t, b: (t, 0)),
                  pl.BlockSpec((l, tn), lambda c, t, b: (b, z_col * cpt + c)),
                  pl.BlockSpec((nd, 3 * DFT_HALF, tn), lambda c, t, b: (0, t, k_col * cpt + c))],
        out_specs=pl.BlockSpec((1, nb, tm, tn), lambda c, t, b: (b, 0, t, c)),
        out_shape=jax.ShapeDtypeStruct((batch, nb, n, HY_DIM), BF16),
        scratch_shapes=[pltpu.VMEM((nb, tm, tn), BF16)],
        compiler_params=_cparams(("arbitrary", "arbitrary", "arbitrary")),
        name="hyena_dft_fwd",
    )(fwd, z_arr, kf)


def _dft_inv_kernel(g_ref, y_ref, gate_ref, z_ref, skip_ref, o_ref):
    conv = _dot(g_ref[...], y_ref[0, 0])
    z = z_ref[...].astype(F32)
    o_ref[...] = (gate_ref[...].astype(F32) * (conv + skip_ref[...] * z)).astype(o_ref.dtype)


def _dft_inv(inv, y, gate_arr, gate_col, z_arr, z_col, skip, batch, nb):
    bl, n = inv.shape
    tn = HY_DIM
    return pl.pallas_call(
        _dft_inv_kernel,
        grid=(batch, nb),
        in_specs=[pl.BlockSpec((bl, n), lambda b, i: (0, 0)),
                  pl.BlockSpec((1, 1, n, tn), lambda b, i: (b, i, 0, 0)),
                  pl.BlockSpec((bl, tn), lambda b, i: (b * nb + i, gate_col)),
                  pl.BlockSpec((bl, tn), lambda b, i: (b * nb + i, z_col)),
                  pl.BlockSpec((1, tn), lambda b, i: (0, 0))],
        out_specs=pl.BlockSpec((bl, tn), lambda b, i: (b * nb + i, 0)),
        out_shape=jax.ShapeDtypeStruct((batch * nb * bl, tn), BF16),
        compiler_params=_cparams(("arbitrary", "arbitrary")),
        name="hyena_dft_inv",
    )(inv, y, gate_arr, z_arr, skip.reshape(1, tn))


def _tap_features(l):
    f32 = np.float32
    t01 = np.linspace(0.0, 1.0, l, dtype=f32)[:, None]
    ang = f32(2.0 * math.pi) * np.arange(l, dtype=f32)[:, None] / f32(l)
    bands = np.linspace(1e-4, HY_BANDS - 1, HY_BANDS, dtype=f32)[None, :]
    z = np.concatenate([t01, np.cos(bands * ang), -np.sin(bands * ang)], axis=-1).astype(f32)
    pos = np.concatenate([np.arange(l), [0], np.arange(l - 1, 0, -1)])
    zz = np.zeros((2 * l, 128), f32)
    zz[:, :z.shape[1]] = z[pos]
    zz[l] = 0.0
    return zz


def _dot_hi(a, b):
    a1 = a.astype(BF16)
    a2 = (a - a1.astype(F32)).astype(BF16)
    b1 = b.astype(BF16)
    b2 = (b - b1.astype(F32)).astype(BF16)
    return _dot(a1, b1) + _dot(a1, b2) + _dot(a2, b1)


def _filter_kernel(zz_ref, w1_ref, b1_ref, fr_ref, w2_ref, b2_ref, w3f_ref, w3b_ref, dl_ref,
                   o_ref, hdn_ref):
    n = zz_ref.shape[0]
    l = n // 2

    @pl.when((pl.program_id(0) == 0) & (pl.program_id(1) == 0))
    def _():
        h1 = jnp.sin(fr_ref[0:1, :] * (_dot_hi(zz_ref[...], w1_ref[...]) + b1_ref[...]))
        hdn_ref[...] = jnp.sin(fr_ref[1:2, :] * (_dot_hi(h1, w2_ref[...]) + b2_ref[...]))

    filt = jnp.concatenate([_dot_hi(hdn_ref[0:l, :], w3f_ref[...]),
                            _dot_hi(hdn_ref[l:n, :], w3b_ref[...])], axis=0)
    decay = jnp.exp(-zz_ref[:, 0:1] * dl_ref[...])
    rowi = lax.broadcasted_iota(jnp.int32, filt.shape, 0)
    kern = jnp.where(rowi == l, 0.0, filt * decay)
    o_ref[...] = (kern / jnp.sum(jnp.abs(kern), axis=0, keepdims=True)).astype(o_ref.dtype)


def _hyena_filters(l, w1, b1, freq, w2, b2, w3):
    n = 2 * l
    tn = 256
    ff = w2.shape[0]
    cpo = HY_DIM // tn
    deltas = np.abs(np.linspace(math.log(HY_DECAY_TARGET) / HY_DECAY_PCT_LONG,
                                math.log(HY_DECAY_TARGET) / HY_DECAY_PCT_SHORT, HY_DIM,
                                dtype=np.float32)).reshape(1, HY_DIM)
    const = lambda o, c: (0, 0)
    return pl.pallas_call(
        _filter_kernel,
        grid=(HY_ORDER, cpo),
        in_specs=[pl.BlockSpec((n, 128), const), pl.BlockSpec((128, ff), const),
                  pl.BlockSpec((1, ff), const), pl.BlockSpec((2, ff), const),
                  pl.BlockSpec((ff, ff), const), pl.BlockSpec((1, ff), const),
                  pl.BlockSpec((ff, tn), lambda o, c: (0, o * 2 * cpo + c)),
                  pl.BlockSpec((ff, tn), lambda o, c: (0, o * 2 * cpo + cpo + c)),
                  pl.BlockSpec((1, tn), lambda o, c: (0, c))],
        out_specs=pl.BlockSpec((n, tn), lambda o, c: (0, o * cpo + c)),
        out_shape=jax.ShapeDtypeStruct((n, HY_ORDER * HY_DIM), BF16),
        scratch_shapes=[pltpu.VMEM((n, ff), F32)],
        compiler_params=_cparams(("arbitrary", "arbitrary")),
        name="hyena_filter_mlp",
    )(jnp.asarray(_tap_features(l)), jnp.pad(w1, ((0, 128 - w1.shape[0]), (0, 0))),
      b1.reshape(1, ff), freq, w2, b2.reshape(1, ff), w3, w3, jnp.asarray(deltas))


def _hyena(p_arr, lp, dft, batch, grid):
    l = p_arr.shape[0] // batch
    fwd, inv = dft
    uc = _short_conv(p_arr, lp['hy_conv_w'].reshape(9, 3 * HY_DIM), lp['hy_conv_b'], batch, grid)
    kern = _hyena_filters(l, lp['hy_ff_w1'], lp['hy_ff_b1'], lp['hy_ff_freq'],
                          lp['hy_ff_w2'], lp['hy_ff_b2'], lp['hy_ff_w3'])
    nb = l // inv.shape[0]
    kf = _filter_spectrum(fwd, kern, nb)
    z_arr, z_col = uc, 0
    for n in range(HY_ORDER):
        y = _dft_fwd(fwd, z_arr, z_col, kf, n, batch, nb)
        z_arr = _dft_inv(inv, y, uc, n + 1, z_arr, z_col, lp['hy_skip'][n], batch, nb)
        z_col = 0
    return z_arr


def _merge_kernel(yhy_ref, ohg_ref, hgg_ref, yrt_ref, b0_ref, b1_ref, b2_ref, x_ref, mod_ref,
                  nw_ref, phy_ref, phg_ref, prt_ref, wo_ref, lng_ref, lnb_ref, o_ref):
    o = ohg_ref[...]
    yhg = (o * lax.rsqrt(jnp.mean(o * o, axis=-1, keepdims=True) + LN_EPS) * nw_ref[...]
           * _silu(hgg_ref[...].astype(F32)))
    m = (jax.nn.sigmoid(b0_ref[...].astype(F32)) * _dot(yhy_ref[...], phy_ref[...])
         + jax.nn.sigmoid(b1_ref[...].astype(F32)) * _dot(yhg.astype(BF16), phg_ref[...])
         + jax.nn.sigmoid(b2_ref[...].astype(F32)) * _dot(yrt_ref[...], prt_ref[...]))
    t = _dot(m.astype(BF16), wo_ref[...])
    gt1 = mod_ref[0][2:3]
    o_ref[...] = _layer_norm_rows(DN_ALPHA * x_ref[...] + gt1 * t, lng_ref[...], lnb_ref[...])


def _merge(y_hy, o_hg, y_rt, p_arr, x, modp, lp, wbf, rows_per_batch):
    m = x.shape[0]
    tm = min(512, m)
    d = D_MODEL
    mi = _mod_index(rows_per_batch, tm)
    row = lambda i: (i, 0)
    const = lambda i: (0, 0)
    gcol = COL_HG * 128 // d
    bcol = COL_BR * 128 // d
    return pl.pallas_call(
        _merge_kernel,
        grid=(m // tm,),
        in_specs=[pl.BlockSpec((tm, d), row), pl.BlockSpec((tm, d), row),
                  pl.BlockSpec((tm, d), lambda i: (i, gcol)),
                  pl.BlockSpec((tm, 2 * d), row),
                  pl.BlockSpec((tm, d), lambda i: (i, bcol)),
                  pl.BlockSpec((tm, d), lambda i: (i, bcol + 1)),
                  pl.BlockSpec((tm, d), lambda i: (i, bcol + 2)),
                  pl.BlockSpec((tm, d), row),
                  pl.BlockSpec((1, 8, d), lambda i: (mi(i), 0, 0)),
                  pl.BlockSpec((1, d), const),
                  pl.BlockSpec((d, d), const), pl.BlockSpec((d, d), const),
                  pl.BlockSpec((2 * d, d), const), pl.BlockSpec((d, d), const),
                  pl.BlockSpec((1, d), const), pl.BlockSpec((1, d), const)],
        out_specs=pl.BlockSpec((tm, d), row),
        out_shape=jax.ShapeDtypeStruct((m, d), F32),
        compiler_params=_cparams(("arbitrary",)),
        name="merge_out_ln",
    )(y_hy, o_hg, p_arr, y_rt, p_arr, p_arr, p_arr, x, modp, lp['hg_norm_w'].reshape(1, d),
      wbf['p_hy'], wbf['p_hg'], wbf['p_rt'], wbf['w_o'],
      lp['ln1_g'].reshape(1, d), lp['ln1_b'].reshape(1, d))


def _ffn_kernel(x_ref, mod_ref, w1_ref, w3_ref, w2_ref, lng_ref, lnb_ref, o_ref, h_ref, acc_ref):
    k = pl.program_id(1)
    m = mod_ref[0]

    @pl.when(k == 0)
    def _():
        h_ref[...] = (x_ref[...] * (1.0 + m[4:5]) + m[3:4]).astype(BF16)
        acc_ref[...] = jnp.zeros(acc_ref.shape, F32)

    h = h_ref[...]
    u = _silu(_dot(h, w1_ref[...])) * _dot(h, w3_ref[...])
    acc_ref[...] += _dot(u.astype(BF16), w2_ref[...])

    @pl.when(k == pl.num_programs(1) - 1)
    def _():
        o_ref[...] = _layer_norm_rows(DN_ALPHA * x_ref[...] + m[5:6] * acc_ref[...],
                                      lng_ref[...], lnb_ref[...])


def _ffn_dense(x, modp, w1, w3, w2, ln_g, ln_b, rows_per_batch):
    m = x.shape[0]
    d = D_MODEL
    dff = w1.shape[1]
    tm = min(1024, rows_per_batch or m)
    tf = dff // 2
    mi = _mod_index(rows_per_batch, tm)
    return pl.pallas_call(
        _ffn_kernel,
        grid=(m // tm, dff // tf),
        in_specs=[pl.BlockSpec((tm, d), lambda i, k: (i, 0)),
                  pl.BlockSpec((1, 8, d), lambda i, k: (mi(i), 0, 0)),
                  pl.BlockSpec((d, tf), lambda i, k: (0, k)),
                  pl.BlockSpec((d, tf), lambda i, k: (0, k)),
                  pl.BlockSpec((tf, d), lambda i, k: (k, 0)),
                  pl.BlockSpec((1, d), lambda i, k: (0, 0)),
                  pl.BlockSpec((1, d), lambda i, k: (0, 0))],
        out_specs=pl.BlockSpec((tm, d), lambda i, k: (i, 0)),
        out_shape=jax.ShapeDtypeStruct((m, d), F32),
        scratch_shapes=[pltpu.VMEM((tm, d), BF16), pltpu.VMEM((tm, d), F32)],
        compiler_params=_cparams(("arbitrary", "arbitrary")),
        name="ffn_dense_ln",
    )(x, modp, w1, w3, w2, ln_g.reshape(1, d), ln_b.reshape(1, d))


def _router_kernel(x_ref, mod_ref, r_ref, h_ref, lg_ref):
    m = mod_ref[0]
    h = x_ref[...] * (1.0 + m[4:5]) + m[3:4]
    h_ref[...] = h
    a1, a2, a3 = _split3(h)
    r1, r2, r3 = _split3(r_ref[...])
    lg_ref[...] = (_dot(a1, r1) + _dot(a1, r2) + _dot(a2, r1)
                   + _dot(a2, r2) + _dot(a1, r3) + _dot(a3, r1))


def _router(x, modp, router_pad, rows_per_batch):
    m = x.shape[0]
    d = D_MODEL
    tm = min(512, m)
    mi = _mod_index(rows_per_batch, tm)
    return pl.pallas_call(
        _router_kernel,
        grid=(m // tm,),
        in_specs=[pl.BlockSpec((tm, d), lambda i: (i, 0)),
                  pl.BlockSpec((1, 8, d), lambda i: (mi(i), 0, 0)),
                  pl.BlockSpec((d, 128), lambda i: (0, 0))],
        out_specs=[pl.BlockSpec((tm, d), lambda i: (i, 0)),
                   pl.BlockSpec((tm, 128), lambda i: (i, 0))],
        out_shape=[jax.ShapeDtypeStruct((m, d), F32), jax.ShapeDtypeStruct((m, 128), F32)],
        compiler_params=_cparams(("arbitrary",)),
        name="moe_router",
    )(x, modp, router_pad)


def _moe_ffn_kernel(be_ref, nu_ref, xp_ref, w1_ref, w3_ref, w2_ref, o_ref, x_ref, acc_ref):
    j = pl.program_id(0)
    k = pl.program_id(1)

    @pl.when(j < nu_ref[0])
    def _():
        @pl.when(k == 0)
        def _():
            x_ref[...] = xp_ref[...].astype(BF16)
            acc_ref[...] = jnp.zeros(acc_ref.shape, F32)

        x = x_ref[...]
        u = _silu(_dot(x, w1_ref[0])) * _dot(x, w3_ref[0])
        acc_ref[...] += _dot(u.astype(BF16), w2_ref[0])

        @pl.when(k == pl.num_programs(1) - 1)
        def _():
            o_ref[...] = acc_ref[...]

    @pl.when((j >= nu_ref[0]) & (k == pl.num_programs(1) - 1))
    def _():
        o_ref[...] = jnp.zeros(o_ref.shape, F32)


def _moe_ffn(xb, block_e, n_used, w1, w3, w2):
    ns = xb.shape[0]
    d = D_MODEL
    tm = MOE_ROWS
    dex = w1.shape[2]
    tf = dex // 2
    grid_spec = pltpu.PrefetchScalarGridSpec(
        num_scalar_prefetch=2,
        grid=(ns // tm, dex // tf),
        in_specs=[pl.BlockSpec((tm, d), lambda j, k, be, nu: (j, 0)),
                  pl.BlockSpec((1, d, tf), lambda j, k, be, nu: (be[j], 0, k)),
                  pl.BlockSpec((1, d, tf), lambda j, k, be, nu: (be[j], 0, k)),
                  pl.BlockSpec((1, tf, d), lambda j, k, be, nu: (be[j], k, 0))],
        out_specs=pl.BlockSpec((tm, d), lambda j, k, be, nu: (j, 0)),
        scratch_shapes=[pltpu.VMEM((tm, d), BF16), pltpu.VMEM((tm, d), F32)])
    return pl.pallas_call(
        _moe_ffn_kernel,
        grid_spec=grid_spec,
        out_shape=jax.ShapeDtypeStruct((ns, d), F32),
        compiler_params=_cparams(("arbitrary", "arbitrary")),
        name="moe_expert_ffn",
    )(block_e, n_used, xb, w1, w3, w2)


def _combine_kernel(x_ref, mod_ref, y0_ref, y1_ref, g_ref, lng_ref, lnb_ref, o_ref):
    m = mod_ref[0]
    g = g_ref[...]
    f = g[:, 0:1] * y0_ref[...] + g[:, 1:2] * y1_ref[...]
    o_ref[...] = _layer_norm_rows(DN_ALPHA * x_ref[...] + m[5:6] * f, lng_ref[...], lnb_ref[...])


def _moe_combine(x, modp, y0, y1, gate_pad, ln_g, ln_b, rows_per_batch):
    m = x.shape[0]
    d = D_MODEL
    tm = min(512, m)
    mi = _mod_index(rows_per_batch, tm)
    row = lambda i: (i, 0)
    return pl.pallas_call(
        _combine_kernel,
        grid=(m // tm,),
        in_specs=[pl.BlockSpec((tm, d), row),
                  pl.BlockSpec((1, 8, d), lambda i: (mi(i), 0, 0)),
                  pl.BlockSpec((tm, d), row), pl.BlockSpec((tm, d), row),
                  pl.BlockSpec((tm, 128), row),
                  pl.BlockSpec((1, d), lambda i: (0, 0)), pl.BlockSpec((1, d), lambda i: (0, 0))],
        out_specs=pl.BlockSpec((tm, d), row),
        out_shape=jax.ShapeDtypeStruct((m, d), F32),
        compiler_params=_cparams(("arbitrary",)),
        name="moe_combine_ln",
    )(x, modp, y0, y1, gate_pad, ln_g.reshape(1, d), ln_b.reshape(1, d))


def _cast_kernel(x_ref, o_ref):
    o_ref[...] = x_ref[...].astype(o_ref.dtype)


def _to_bf16(w, group):
    _, e, a, b = w.shape
    ta, tb = (a // 4, b) if a >= b else (a, b // 4)
    cut_rows = a >= b
    return pl.pallas_call(
        _cast_kernel,
        grid=(e, 4),
        in_specs=[pl.BlockSpec((None, 1, ta, tb),
                               lambda i, r: (group, i, r, 0) if cut_rows else (group, i, 0, r))],
        out_specs=pl.BlockSpec((1, ta, tb), lambda i, r: (i, r, 0) if cut_rows else (i, 0, r)),
        out_shape=jax.ShapeDtypeStruct((e, a, b), BF16),
        compiler_params=_cparams(("arbitrary", "arbitrary")),
        name="weight_cast",
    )(w)


def _moe(x, modp, router, w1, w3, w2, ln_g, ln_b, rows_per_batch):
    n, d = x.shape
    e = router.shape[1]
    h2, logits = _router(x, modp, jnp.pad(router, ((0, 0), (0, 128 - e))), rows_per_batch)
    top_val, top_idx = lax.top_k(logits[:, :e], TOP_K)
    gate = jax.nn.softmax(top_val, axis=-1)
    flat_e = top_idx.reshape(-1)
    flat_t = jnp.repeat(jnp.arange(n, dtype=jnp.int32), TOP_K)
    order = jnp.argsort(flat_e, stable=True).astype(jnp.int32)
    rank = jnp.argsort(order).astype(jnp.int32)
    counts = jnp.sum((flat_e[:, None] == jnp.arange(e)[None, :]).astype(jnp.int32), axis=0)
    starts = jnp.cumsum(counts) - counts
    padded = (counts + MOE_ROWS - 1) // MOE_ROWS * MOE_ROWS
    pad_end = jnp.cumsum(padded)
    pad_start = pad_end - padded
    n_blocks = -(-(n * TOP_K) // MOE_ROWS) + e
    n_slots = n_blocks * MOE_ROWS
    block_start = jnp.arange(n_blocks) * MOE_ROWS
    block_e = jnp.minimum(jnp.sum(block_start[:, None] >= pad_end[None, :], axis=1), e - 1).astype(jnp.int32)
    n_used = (pad_end[-1] // MOE_ROWS).astype(jnp.int32).reshape(1)
    slot_e = jnp.repeat(block_e, MOE_ROWS)
    slot_off = jnp.arange(n_slots, dtype=jnp.int32) - pad_start[slot_e]
    slot_valid = slot_off < counts[slot_e]
    slot_src = jnp.where(slot_valid, starts[slot_e] + slot_off, 0)
    slot_tok = jnp.where(slot_valid, flat_t[order[slot_src]], 0).astype(jnp.int32)
    pos = (pad_start[flat_e] + rank - starts[flat_e]).astype(jnp.int32).reshape(n, TOP_K)
    xb = jnp.take(h2, slot_tok, axis=0, mode="clip")
    yb = _moe_ffn(xb, block_e, n_used, w1, w3, w2)
    y0 = jnp.take(yb, pos[:, 0], axis=0, mode="clip")
    y1 = jnp.take(yb, pos[:, 1], axis=0, mode="clip")
    gate_pad = jnp.pad(gate, ((0, 0), (0, 128 - TOP_K)))
    return _moe_combine(x, modp, y0, y1, gate_pad, ln_g, ln_b, rows_per_batch)


def kernel(x, c, ctx, c_ctx, ada_w, ada_b, w_in, hy_conv_w, hy_conv_b, hy_ff_w1, hy_ff_b1, hy_ff_freq, hy_ff_w2, hy_ff_b2, hy_ff_w3, hy_skip, hg_lb_logits, hg_norm_w, p_hy, p_hg, p_rt, w_o, ln1_g, ln1_b, ln2_g, ln2_b, ffn_w1, ffn_w3, ffn_w2, moe_router, moe_w1, moe_w3, moe_w2):
    batch, l, d = x.shape
    lc = ctx.shape[1]
    assert d == D_MODEL and batch <= 8
    assert l % 512 == 0 and l % GRID_W == 0 and lc % max(RET_CHUNK, DFT_HALF) == 0

    cs = jnp.cumsum(jax.nn.softmax(hg_lb_logits.astype(F32), axis=1), axis=1)
    lower_bounds = cs - cs[:, :1]
    cc = jnp.zeros((16, d), F32).at[:batch].set(c).at[8].set(c_ctx)
    ret_tables = _retention_tables(lc, l)
    dft_lat = _dft_matrices(min(HY_BLOCK, l))
    dft_ctx = _dft_matrices(min(HY_BLOCK, lc))

    x_lat = x.reshape(batch * l, d)
    x_ctx = ctx.reshape(batch * lc, d)
    for i in range(DEPTH):
        need_ctx_out = i < DEPTH - 1
        use_moe = i % 2 == 1
        g = i // 2
        lp = {'hy_conv_w': hy_conv_w[i], 'hy_conv_b': hy_conv_b[i], 'hy_ff_w1': hy_ff_w1[i],
              'hy_ff_b1': hy_ff_b1[i], 'hy_ff_freq': hy_ff_freq[i], 'hy_ff_w2': hy_ff_w2[i],
              'hy_ff_b2': hy_ff_b2[i], 'hy_ff_w3': hy_ff_w3[i], 'hy_skip': hy_skip[i],
              'hg_norm_w': hg_norm_w[i], 'ln1_g': ln1_g[i], 'ln1_b': ln1_b[i]}
        wbf = {'p_hy': p_hy[i].astype(BF16), 'p_hg': p_hg[i].astype(BF16),
               'p_rt': p_rt[i].astype(BF16), 'w_o': w_o[i].astype(BF16)}
        mod = _ada(cc, ada_w, i, ada_b[i])
        modp = jnp.pad(mod.reshape(16, 6, d), ((0, 0), (0, 2), (0, 0)))
        lb = lower_bounds[:, i]
        gl_tab = jnp.pad(jnp.stack([lb, 1.0 - lb], axis=1), ((0, 0), (0, 6), (0, 0)))

        p_lat = _proj(x_lat, modp, w_in, i, N_IN_COLS, l)
        p_ctx = _proj(x_ctx, modp, w_in, i, N_IN_COLS if need_ctx_out else N_STATE_COLS, None)

        o_hg_l, o_hg_c = _gla(p_lat, p_ctx, gl_tab, batch, need_ctx_out)
        y_rt_l, y_rt_c = _retention(p_lat, p_ctx, ret_tables, batch, need_ctx_out)
        y_hy_l = _hyena(p_lat, lp, dft_lat, batch, True)
        x_lat_new = _merge(y_hy_l, o_hg_l, y_rt_l, p_lat, x_lat, modp, lp, wbf, l)
        if need_ctx_out:
            y_hy_c = _hyena(p_ctx, lp, dft_ctx, batch, False)
            x_ctx = _merge(y_hy_c, o_hg_c, y_rt_c, p_ctx, x_ctx, modp, lp, wbf, None)
        x_lat = x_lat_new

        if use_moe:
            if need_ctx_out:
                raise NotImplementedError("MoE layer with a context output is not part of this trunk")
            w1, w3, w2 = _to_bf16(moe_w1, g), _to_bf16(moe_w3, g), _to_bf16(moe_w2, g)
            x_lat = _moe(x_lat, modp, moe_router[g], w1, w3, w2, ln2_g[i], ln2_b[i], l)
        else:
            w1, w3, w2 = ffn_w1[g].astype(BF16), ffn_w3[g].astype(BF16), ffn_w2[g].astype(BF16)
            x_lat = _ffn_dense(x_lat, modp, w1, w3, w2, ln2_g[i], ln2_b[i], l)
            if need_ctx_out:
                x_ctx = _ffn_dense(x_ctx, modp, w1, w3, w2, ln2_g[i], ln2_b[i], None)
    return x_lat.reshape(batch, l, d)
```

```python
import functools
import math

import numpy as np
import jax
import jax.numpy as jnp
from jax import lax
from jax.experimental import pallas as pl
from jax.experimental.pallas import tpu as pltpu

F32 = jnp.float32
BF16 = jnp.bfloat16

D_MODEL = 1024
DEPTH = 2
GRID_W = 64
HY_DIM = 1024
HY_ORDER = 2
HY_BANDS = 16
HY_DECAY_TARGET = 1e-2
HY_DECAY_PCT_SHORT = 0.3
HY_DECAY_PCT_LONG = 1.5
HG_HEADS = 8
HG_HK = 128
RT_HEADS = 4
RT_HK = 256
RT_HV = 512
RT_ROPE_BASE = 10000.0
N_STATE_COLS = 6144
N_IN_COLS = 17408
N_EXPERTS = 8
TOP_K = 2
DN_ALPHA = (2 * DEPTH) ** 0.25
LN_EPS = 1e-5

COL_FF, COL_FB, COL_HI, COL_RK, COL_RV = 0, 8, 16, 24, 32
COL_HQ, COL_HG, COL_RQ, COL_RG, COL_HY, COL_BR = 48, 56, 64, 72, 88, 112

GLA_CHUNK = 256
GLA_LEVELS = 8
GLA_PAD = 8
GLA_MIN_LOG2 = -150.0
RET_CHUNK = 256
DFT_HALF = 256
HY_BLOCK = 512
MIX_ROWS = 16
CONV_ROWS = 256
MOE_ROWS = 512
VMEM_LIMIT = 56 * 1024 * 1024


def _cparams(sem, flags=None):
    return pltpu.CompilerParams(dimension_semantics=sem, vmem_limit_bytes=VMEM_LIMIT, flags=flags)


def _silu(x):
    return x * jax.nn.sigmoid(x)


def _layer_norm_rows(r, g, b):
    mu = jnp.mean(r, axis=-1, keepdims=True)
    d = r - mu
    var = jnp.mean(d * d, axis=-1, keepdims=True)
    return d * lax.rsqrt(var + LN_EPS) * g + b


def _split3(x):
    h = x.astype(BF16)
    r = x - h.astype(F32)
    m = r.astype(BF16)
    l = (r - m.astype(F32)).astype(BF16)
    return h, m, l


def _dot(a, b):
    return jnp.dot(a, b, preferred_element_type=F32)


def _dot_nt(a, b):
    return lax.dot_general(a, b, (((1,), (1,)), ((), ())), preferred_element_type=F32)


def _dot_tn(a, b):
    return lax.dot_general(a, b, (((0,), (0,)), ((), ())), preferred_element_type=F32)


def _ada_kernel(c_ref, w_ref, b_ref, o_ref):
    s = _silu(c_ref[...])
    o_ref[...] = _dot(s.astype(BF16), w_ref[...].astype(BF16)) + b_ref[...]


def _ada(cc, w, layer, b):
    n = w.shape[2]
    tn = 1024
    return pl.pallas_call(
        _ada_kernel,
        grid=(n // tn,),
        in_specs=[pl.BlockSpec((16, D_MODEL), lambda j: (0, 0)),
                  pl.BlockSpec((None, D_MODEL, tn), lambda j: (layer, 0, j)),
                  pl.BlockSpec((1, tn), lambda j: (0, j))],
        out_specs=pl.BlockSpec((16, tn), lambda j: (0, j)),
        out_shape=jax.ShapeDtypeStruct((16, n), F32),
        compiler_params=_cparams(("arbitrary",)),
        name="ada_mod",
    )(cc, w, b.reshape(1, n))


def _mod_index(rows_per_batch, tm):
    if rows_per_batch is None:
        return lambda i: 8
    return lambda i: (i * tm) // rows_per_batch


def _proj_kernel(x_ref, mod_ref, w_ref, o_ref, h_ref):
    @pl.when(pl.program_id(1) == 0)
    def _():
        m = mod_ref[0]
        h_ref[...] = (x_ref[...] * (1.0 + m[1:2]) + m[0:1]).astype(BF16)

    o_ref[...] = _dot(h_ref[...], w_ref[...].astype(BF16)).astype(o_ref.dtype)


def _proj(x, modp, w, layer, n_cols, rows_per_batch):
    m = x.shape[0]
    tm = min(2048, rows_per_batch or m)
    tn = 1024
    mi = _mod_index(rows_per_batch, tm)
    return pl.pallas_call(
        _proj_kernel,
        grid=(m // tm, n_cols // tn),
        in_specs=[pl.BlockSpec((tm, D_MODEL), lambda i, j: (i, 0)),
                  pl.BlockSpec((1, 8, D_MODEL), lambda i, j: (mi(i), 0, 0)),
                  pl.BlockSpec((None, D_MODEL, tn), lambda i, j: (layer, 0, j))],
        out_specs=pl.BlockSpec((tm, tn), lambda i, j: (i, j)),
        out_shape=jax.ShapeDtypeStruct((m, n_cols), BF16),
        scratch_shapes=[pltpu.VMEM((tm, D_MODEL), BF16)],
        compiler_params=_cparams(("arbitrary", "arbitrary")),
        name="in_proj",
    )(x, modp, w)


def _gla_gates(fl, gl_ref, d):
    lb = gl_ref[d, 0:1, :]
    oml = gl_ref[d, 1:2, :]
    t = jnp.exp(-jnp.abs(fl))
    r = 1.0 / (1.0 + t)
    tr = t * r
    pos = fl >= 0.0
    f = lb + oml * jnp.where(pos, r, tr)
    key = oml * jnp.where(pos, tr, r)
    return jnp.maximum(jnp.log2(f), GLA_MIN_LOG2), key


def _gla_cumsum(g, rev):
    c = g.shape[0]
    r = lax.broadcasted_iota(jnp.int32, (c, c), 0)
    u = lax.broadcasted_iota(jnp.int32, (c, c), 1)
    tri = jnp.where((u >= r) if rev else (u <= r), 1.0, 0.0).astype(BF16)
    return _dot(jnp.concatenate([tri, tri, tri], axis=1), jnp.concatenate(_split3(g), axis=0))


def _gla_state_update(st, b, kk, vv, rev):
    c = b.shape[0]
    end = 0 if rev else c - 1
    b_end = b[end:end + 1, :]
    khat = (kk * jnp.exp2(b_end - b)).astype(BF16)
    return st * jnp.exp2(b_end) + _dot_tn(vv, khat)


def _gla_level_table():
    c = GLA_CHUNK
    t = np.arange(c)[:, None]
    s = np.arange(c)[None, :]
    top_bit = np.floor(np.log2(np.maximum(t ^ s, 1))).astype(np.int32)
    fwd = np.where(t > s, top_bit, np.where(t == s, GLA_LEVELS, -1))
    return np.stack([fwd, fwd.T]).astype(np.int32)


def _gla_ref_rows(b, bpad_ref, level, rev):
    c = b.shape[0]
    half = 1 << level
    blk = 2 * half
    idx = half if rev else half - 1
    if blk % 8 == 0:
        r = b.reshape(c // blk, blk, HG_HK)[:, idx:idx + 1, :]
        return jnp.broadcast_to(r, (c // blk, blk, HG_HK)).reshape(c, HG_HK)
    m = lax.broadcasted_iota(jnp.int32, (c, HG_HK), 0) % blk
    out = b
    for v in range(blk):
        if v != idx:
            off = GLA_PAD + idx - v
            out = jnp.where(m == v, bpad_ref[off:off + c, :], out)
    return out


def _gla_chunk_out(b, kk, vv, qq, st, rev, lvt, bpad_ref):
    c = GLA_CHUNK
    o = _dot_nt((qq * jnp.exp2(b)).astype(BF16), st.astype(BF16))
    bpad_ref[GLA_PAD:GLA_PAD + c, :] = b
    qq_b, kk_b = qq.astype(BF16), kk.astype(BF16)
    sc = jnp.zeros((c, c), F32)
    for level in range(GLA_LEVELS):
        half = 1 << level
        blk = 2 * half
        ref = _gla_ref_rows(b, bpad_ref, level, rev)
        kt = kk_b * jnp.exp2(ref - b).astype(BF16)
        if half % 8:
            p = _dot_nt(qq_b * jnp.exp2(b - ref).astype(BF16), kt)
            sc = jnp.where(lvt == level, p, sc)
        else:
            nblk = c // blk
            lo, hi = (0, half) if rev else (half, blk)

            def part(x):
                return x.reshape(nblk, blk, x.shape[-1])[:, lo:hi, :]

            qt = (part(qq) * jnp.exp2(part(b) - part(ref))).reshape(c // 2, HG_HK)
            p = _dot_nt(qt.astype(BF16), kt).reshape(nblk, half, c)
            new = jnp.where(part(lvt) == level, p, part(sc))
            rest = sc.reshape(nblk, blk, c)[:, half:blk, :] if rev else sc.reshape(nblk, blk, c)[:, 0:half, :]
            sc = jnp.concatenate([new, rest] if rev else [rest, new], axis=1).reshape(c, c)
    sc = jnp.where(lvt == GLA_LEVELS, _dot_nt(qq_b, kk_b), sc)
    return o + _dot(sc.astype(BF16), vv)


def _gla_kernel(*refs, need_ctx_out, n_lat, n_ctx):
    if need_ctx_out:
        ffl, fbl, il, ql, ffc, fbc, ic, qc, gl_ref, lv_ref, ol_ref, oc_ref = refs[:12]
        bl, kl, bc, kc, qsl, qsc, bpf, bpr = refs[12:]
    else:
        ffl, fbl, il, ql, ffc, fbc, ic, gl_ref, lv_ref, ol_ref = refs[:10]
        bl, kl, bc, kc, qsl, bpf, bpr = refs[10:]
        qc = oc_ref = qsc = None
    c = GLA_CHUNK
    q_scale = HG_HK ** -0.5
    bpf[...] = jnp.zeros(bpf.shape, F32)
    bpr[...] = jnp.zeros(bpr.shape, F32)
    ol_ref[...] = jnp.zeros(ol_ref.shape, F32)
    if need_ctx_out:
        oc_ref[...] = jnp.zeros(oc_ref.shape, F32)

    def chunk_rows(idx):
        return pl.ds(pl.multiple_of(idx * c, c), c)

    def prepare(f_refs, q_ref, b_sc, k_sc, q_sc, n):
        def body(j, carry):
            rows = chunk_rows(j)
            for d in range(2):
                g, kk = _gla_gates(f_refs[d][rows, :].astype(F32), gl_ref, d)
                b_sc[d, rows, :] = _gla_cumsum(g, d == 1)
                k_sc[d, rows, :] = kk
            if q_ref is not None:
                q_sc[rows, :] = _silu(q_ref[rows, :].astype(F32)) * q_scale
            return carry
        lax.fori_loop(0, n, body, 0, unroll=min(2, n))

    prepare((ffc, fbc), qc, bc, kc, qsc, n_ctx)
    prepare((ffl, fbl), ql, bl, kl, qsl, n_lat)

    def one_dir(d, b_sc, k_sc, i_ref, q_sc, o_ref, idx, st):
        rev = d == 1
        rows = chunk_rows(idx)
        b, kk, vv = b_sc[d, rows, :], k_sc[d, rows, :], i_ref[rows, :]
        if q_sc is not None:
            o_ref[rows, :] += _gla_chunk_out(b, kk, vv, q_sc[rows, :], st, rev, lv_ref[d],
                                             bpr if rev else bpf)
        return _gla_state_update(st, b, kk, vv, rev)

    def ctx_body(j, carry):
        return (one_dir(0, bc, kc, ic, qsc, oc_ref, j, carry[0]),
                one_dir(1, bc, kc, ic, qsc, oc_ref, n_ctx - 1 - j, carry[1]))

    def lat_body(j, carry):
        return (one_dir(0, bl, kl, il, qsl, ol_ref, j, carry[0]),
                one_dir(1, bl, kl, il, qsl, ol_ref, n_lat - 1 - j, carry[1]))

    zero = jnp.zeros((HG_HK, HG_HK), F32)
    carry = lax.fori_loop(0, n_ctx, ctx_body, (zero, zero), unroll=min(2, n_ctx))
    lax.fori_loop(0, n_lat, lat_body, carry, unroll=min(2, n_lat))


def _gla(p_lat, p_ctx, gl_tab, batch, need_ctx_out):
    l = p_lat.shape[0] // batch
    lc = p_ctx.shape[0] // batch
    w = HG_HK
    c = GLA_CHUNK

    def col(base):
        return lambda b, h: (b, base + h)

    lat_specs = [pl.BlockSpec((l, w), col(cb)) for cb in (COL_FF, COL_FB, COL_HI, COL_HQ)]
    ctx_cols = (COL_FF, COL_FB, COL_HI) + ((COL_HQ,) if need_ctx_out else ())
    ctx_specs = [pl.BlockSpec((lc, w), col(cb)) for cb in ctx_cols]
    in_specs = lat_specs + ctx_specs + [pl.BlockSpec((2, 8, w), lambda b, h: (0, 0, h)),
                                        pl.BlockSpec((2, c, c), lambda b, h: (0, 0, 0))]
    args = [p_lat] * 4 + [p_ctx] * len(ctx_cols) + [gl_tab, jnp.asarray(_gla_level_table())]
    out_specs = [pl.BlockSpec((l, w), lambda b, h: (b, h))]
    out_shape = [jax.ShapeDtypeStruct((batch * l, HG_HEADS * w), F32)]
    scratch = [pltpu.VMEM((2, l, w), F32), pltpu.VMEM((2, l, w), F32),
               pltpu.VMEM((2, lc, w), F32), pltpu.VMEM((2, lc, w), F32), pltpu.VMEM((l, w), F32)]
    if need_ctx_out:
        out_specs.append(pl.BlockSpec((lc, w), lambda b, h: (b, h)))
        out_shape.append(jax.ShapeDtypeStruct((batch * lc, HG_HEADS * w), F32))
        scratch.append(pltpu.VMEM((lc, w), F32))
    scratch += [pltpu.VMEM((c + 2 * GLA_PAD, w), F32)] * 2
    outs = pl.pallas_call(
        functools.partial(_gla_kernel, need_ctx_out=need_ctx_out, n_lat=l // c, n_ctx=lc // c),
        grid=(batch, HG_HEADS),
        in_specs=in_specs,
        out_specs=out_specs,
        out_shape=out_shape,
        scratch_shapes=scratch,
        compiler_params=_cparams(("arbitrary", "arbitrary")),
        name="hgrn2_scan",
    )(*args)
    return outs[0], (outs[1] if need_ctx_out else None)


def _rotate(x, cos, sin):
    half = x.shape[-1] // 2
    x1, x2 = x[:, :half], x[:, half:]
    return jnp.concatenate([x1 * cos - x2 * sin, x1 * sin + x2 * cos], axis=-1)


def _ret_kernel(*refs, need_ctx_out, n_lat, n_ctx):
    if need_ctx_out:
        (ql, kl, vl, gl, qc, kc, vc, gc, cos_ref, sin_ref, intra_ref, qk_ref, sd_ref,
         yl_ref, yc_ref, qrl, krl, qrc, krc, ol, oc, s_fwd, s_rev) = refs
    else:
        (ql, kl, vl, gl, kc, vc, cos_ref, sin_ref, intra_ref, qk_ref, sd_ref,
         yl_ref, qrl, krl, krc, ol, s_fwd, s_rev) = refs
        qc = gc = yc_ref = qrc = oc = None
    s_refs = (s_fwd, s_rev)
    c = RET_CHUNK
    lc = n_ctx * c
    k_scale = RT_HK ** -0.5

    def rot_all(src, dst, n, pos0, scale):
        def body(j, carry):
            rows = pl.ds(pl.multiple_of(j * c, c), c)
            prow = pl.ds(pl.multiple_of(pos0 + j * c, c), c)
            x = src[rows, :].astype(F32)
            dst[rows, :] = (_rotate(x, cos_ref[prow, :], sin_ref[prow, :]) * scale).astype(BF16)
            return carry
        lax.fori_loop(0, n, body, 0)

    rot_all(kl, krl, n_lat, lc, k_scale)
    rot_all(kc, krc, n_ctx, 0, k_scale)
    rot_all(ql, qrl, n_lat, lc, 1.0)
    if need_ctx_out:
        rot_all(qc, qrc, n_ctx, 0, 1.0)

    for s_ref in s_refs:
        s_ref[...] = jnp.zeros(s_ref.shape, F32)
    ol[...] = jnp.zeros(ol.shape, F32)
    if need_ctx_out:
        oc[...] = jnp.zeros(oc.shape, F32)

    def update(d, kr, v_ref, rows):
        kd = qk_ref[d, 0, c:2 * c, :]
        sdec = sd_ref[d, 0, 0:1, :]
        kh = (kr[rows, :].astype(F32) * jnp.concatenate([kd, kd], axis=1)).astype(BF16)
        s_ref = s_refs[d]
        s_ref[...] = (s_ref[...] * jnp.concatenate([sdec] * (RT_HV // 128), axis=1)
                      + _dot_tn(kh, v_ref[rows, :]))

    def out_step(d, qr, kr, v_ref, o_ref, rows):
        qd = qk_ref[d, 0, 0:c, :]
        qcb = qr[rows, :]
        sc = _dot_nt(qcb, kr[rows, :]) * intra_ref[d, 0]
        qh = (qcb.astype(F32) * jnp.concatenate([qd, qd], axis=1)).astype(BF16)
        o_ref[rows, :] += (_dot(qh, s_refs[d][...].astype(BF16))
                           + _dot(sc.astype(BF16), v_ref[rows, :]))
        update(d, kr, v_ref, rows)

    def ctx_body(j, carry):
        for d in range(2):
            idx = (n_ctx - 1 - j) if d == 1 else j
            rows = pl.ds(pl.multiple_of(idx * c, c), c)
            if need_ctx_out:
                out_step(d, qrc, krc, vc, oc, rows)
            else:
                update(d, krc, vc, rows)
        return carry

    def lat_body(j, carry):
        for d in range(2):
            idx = (n_lat - 1 - j) if d == 1 else j
            out_step(d, qrl, krl, vl, ol, pl.ds(pl.multiple_of(idx * c, c), c))
        return carry

    lax.fori_loop(0, n_ctx, ctx_body, 0)
    lax.fori_loop(0, n_lat, lat_body, 0, unroll=4)

    def readout(o_ref, g_ref, y_ref, n):
        def body(j, carry):
            rows = pl.ds(pl.multiple_of(j * c, c), c)
            o = o_ref[rows, :]
            y = o * lax.rsqrt(jnp.mean(o * o, axis=-1, keepdims=True) + LN_EPS)
            y_ref[rows, :] = (y * _silu(g_ref[rows, :].astype(F32))).astype(y_ref.dtype)
            return carry
        lax.fori_loop(0, n, body, 0)

    readout(ol, gl, yl_ref, n_lat)
    if need_ctx_out:
        readout(oc, gc, yc_ref, n_ctx)


def _retention_tables(lc, l):
    half = RT_HK // 2
    inv = 1.0 / (RT_ROPE_BASE ** jnp.linspace(0.0, 1.0, half, dtype=F32))
    ang = jnp.arange(lc + l, dtype=F32)[:, None] * inv[None, :]
    j = jnp.arange(2 * RT_HEADS, dtype=F32)
    lg_all = jnp.log1p(-jnp.exp2(-5.0 - j))
    c = RET_CHUNK
    pos = jnp.arange(c, dtype=F32)
    rel = pos[:, None] - pos[None, :]
    intra, qk, sd = [], [], []
    for d in range(2):
        lg = lg_all[d::2]
        m = jnp.where(rel >= 0, jnp.exp(jnp.maximum(rel, 0.0)[None] * lg[:, None, None]), 0.0)
        qdec = jnp.exp((pos + 1.0)[None, :] * lg[:, None])
        kdec = jnp.exp((c - 1.0 - pos)[None, :] * lg[:, None])
        if d == 1:
            m = jnp.swapaxes(m, 1, 2)
            qdec = qdec[:, ::-1]
            kdec = kdec[:, ::-1]
        intra.append(m)
        qk.append(jnp.broadcast_to(jnp.concatenate([qdec, kdec], axis=1)[:, :, None],
                                   (RT_HEADS, 2 * c, 128)))
        sd.append(jnp.broadcast_to(jnp.exp(c * lg)[:, None, None], (RT_HEADS, 8, 128)))
    return (jnp.cos(ang), jnp.sin(ang), jnp.stack(intra), jnp.stack(qk), jnp.stack(sd))


def _retention(p_lat, p_ctx, tables, batch, need_ctx_out):
    l = p_lat.shape[0] // batch
    lc = p_ctx.shape[0] // batch
    cos, sin, intra, qk, sd = tables
    c = RET_CHUNK

    def col(base):
        return lambda b, h: (b, base + h)

    kq, kk_, kv, kg = COL_RQ * 128 // RT_HK, COL_RK * 128 // RT_HK, COL_RV * 128 // RT_HV, COL_RG * 128 // RT_HV
    lat_specs = [pl.BlockSpec((l, RT_HK), col(kq)), pl.BlockSpec((l, RT_HK), col(kk_)),
                 pl.BlockSpec((l, RT_HV), col(kv)), pl.BlockSpec((l, RT_HV), col(kg))]
    if need_ctx_out:
        ctx_specs = [pl.BlockSpec((lc, RT_HK), col(kq)), pl.BlockSpec((lc, RT_HK), col(kk_)),
                     pl.BlockSpec((lc, RT_HV), col(kv)), pl.BlockSpec((lc, RT_HV), col(kg))]
    else:
        ctx_specs = [pl.BlockSpec((lc, RT_HK), col(kk_)), pl.BlockSpec((lc, RT_HV), col(kv))]
    tab_specs = [pl.BlockSpec((lc + l, 128), lambda b, h: (0, 0)),
                 pl.BlockSpec((lc + l, 128), lambda b, h: (0, 0)),
                 pl.BlockSpec((2, 1, c, c), lambda b, h: (0, h, 0, 0)),
                 pl.BlockSpec((2, 1, 2 * c, 128), lambda b, h: (0, h, 0, 0)),
                 pl.BlockSpec((2, 1, 8, 128), lambda b, h: (0, h, 0, 0))]
    args = [p_lat] * 4 + [p_ctx] * len(ctx_specs) + [cos, sin, intra, qk, sd]
    out_specs = [pl.BlockSpec((l, RT_HV), lambda b, h: (b, h))]
    out_shape = [jax.ShapeDtypeStruct((batch * l, RT_HEADS * RT_HV), BF16)]
    scratch = [pltpu.VMEM((l, RT_HK), BF16), pltpu.VMEM((l, RT_HK), BF16)]
    if need_ctx_out:
        out_specs.append(pl.BlockSpec((lc, RT_HV), lambda b, h: (b, h)))
        out_shape.append(jax.ShapeDtypeStruct((batch * lc, RT_HEADS * RT_HV), BF16))
        scratch += [pltpu.VMEM((lc, RT_HK), BF16), pltpu.VMEM((lc, RT_HK), BF16),
                    pltpu.VMEM((l, RT_HV), F32), pltpu.VMEM((lc, RT_HV), F32)]
    else:
        scratch += [pltpu.VMEM((lc, RT_HK), BF16), pltpu.VMEM((l, RT_HV), F32)]
    scratch += [pltpu.VMEM((RT_HK, RT_HV), F32)] * 2
    outs = pl.pallas_call(
        functools.partial(_ret_kernel, need_ctx_out=need_ctx_out, n_lat=l // c, n_ctx=lc // c),
        grid=(batch, RT_HEADS),
        in_specs=lat_specs + ctx_specs + tab_specs,
        out_specs=out_specs,
        out_shape=out_shape,
        scratch_shapes=scratch,
        compiler_params=_cparams(("arbitrary", "arbitrary")),
        name="retention_scan",
    )(*args)
    return outs[0], (outs[1] if need_ctx_out else None)


def _conv_shift_matrices(grid):
    rb = CONV_ROWS
    t = np.arange(rb)
    edge_lo = (t % GRID_W == 0) if grid else (t == 0)
    edge_hi = (t % GRID_W == GRID_W - 1) if grid else (t == rb - 1)
    s0 = ((t[:, None] - 1 == t[None, :]) & ~edge_lo[:, None]).astype(np.float32)
    s2 = ((t[:, None] + 1 == t[None, :]) & ~edge_hi[:, None]).astype(np.float32)
    return s0, s2


def _short_conv_kernel(x_ref, w_ref, b_ref, s0_ref, s2_ref, o_ref, xpad_ref, *, grid):
    l, tc = x_ref.shape
    p = GRID_W
    rb = CONV_ROWS
    zeros = jnp.zeros((p, tc), BF16)
    xpad_ref[0:p, :] = zeros
    xpad_ref[p + l:p + l + p, :] = zeros
    xpad_ref[p:p + l, :] = x_ref[...]
    w = [w_ref[k:k + 1, :].astype(BF16) for k in range(9)]
    rows = range(3) if grid else (1,)

    def body(i, carry):
        r0 = i * rb
        part = []
        for dj in range(3):
            acc = None
            for di in rows:
                start = pl.multiple_of(r0 + p + (di - 1) * GRID_W, GRID_W)
                t = xpad_ref[pl.ds(start, rb), :] * w[di * 3 + dj]
                acc = t if acc is None else acc + t
            part.append(acc)
        out = (part[1].astype(F32) + _dot(s0_ref[...], part[0]) + _dot(s2_ref[...], part[2])
               + b_ref[...])
        o_ref[pl.ds(pl.multiple_of(r0, rb), rb), :] = out.astype(o_ref.dtype)
        return carry

    lax.fori_loop(0, l // rb, body, 0, unroll=min(4, l // rb))


def _short_conv(p_arr, w9, bias, batch, grid):
    l = p_arr.shape[0] // batch
    assert l % CONV_ROWS == 0 and (grid or l == CONV_ROWS)
    c3 = 3 * HY_DIM
    tc = 256
    base = COL_HY * 128 // tc
    s0, s2 = _conv_shift_matrices(grid)
    return pl.pallas_call(
        functools.partial(_short_conv_kernel, grid=grid),
        grid=(batch, c3 // tc),
        in_specs=[pl.BlockSpec((l, tc), lambda b, j: (b, base + j)),
                  pl.BlockSpec((9, tc), lambda b, j: (0, j)),
                  pl.BlockSpec((1, tc), lambda b, j: (0, j)),
                  pl.BlockSpec((CONV_ROWS, CONV_ROWS), lambda b, j: (0, 0)),
                  pl.BlockSpec((CONV_ROWS, CONV_ROWS), lambda b, j: (0, 0))],
        out_specs=pl.BlockSpec((l, tc), lambda b, j: (b, j)),
        out_shape=jax.ShapeDtypeStruct((batch * l, c3), BF16),
        scratch_shapes=[pltpu.VMEM((l + 2 * GRID_W, tc), BF16)],
        compiler_params=_cparams(("arbitrary", "arbitrary")),
        name="hyena_short_conv",
    )(p_arr, w9, bias.reshape(1, c3), jnp.asarray(s0, BF16), jnp.asarray(s2, BF16))


def _dft_matrices(l):
    n = 2 * l
    s = int(round(math.sqrt(n)))
    while n % s:
        s -= 1
    q = n // s
    f = np.arange(l, dtype=np.int64)
    ang_a = 2.0 * np.pi * ((f[:, None] * s * np.arange(q)[None, :]) % n) / n
    ang_b = 2.0 * np.pi * ((f[:, None] * np.arange(s)[None, :]) % n) / n
    t = np.arange(n)
    rep = (t[None, :] // s == np.arange(q)[:, None]).astype(np.float32)
    til = (t[None, :] % s == np.arange(s)[:, None]).astype(np.float32)
    h = DFT_HALF
    row = lambda j: (j, 0)
    const = lambda j: (0, 0)
    fwd, inv = pl.pallas_call(
        _dft_build_kernel,
        grid=(l // h,),
        in_specs=[pl.BlockSpec((h, q), row), pl.BlockSpec((h, q), row),
                  pl.BlockSpec((h, s), row), pl.BlockSpec((h, s), row),
                  pl.BlockSpec((q, n), const), pl.BlockSpec((s, n), const)],
        out_specs=[pl.BlockSpec((1, 2, h, n), lambda j: (j, 0, 0, 0)),
                   pl.BlockSpec((h, n), row)],
        out_shape=[jax.ShapeDtypeStruct((l // h, 2, h, n), BF16),
                   jax.ShapeDtypeStruct((l, n), BF16)],
        compiler_params=_cparams(("arbitrary",)),
        name="dft_build",
    )(jnp.asarray(np.cos(ang_a), F32), jnp.asarray(np.sin(ang_a), F32),
      jnp.asarray(np.cos(ang_b), F32), jnp.asarray(np.sin(ang_b), F32),
      jnp.asarray(rep, BF16), jnp.asarray(til, BF16))
    return fwd.reshape(n, n), inv


def _dft_build_kernel(ca_ref, sa_ref, cb_ref, sb_ref, rep_ref, til_ref, fwd_ref, inv_ref):
    h, n = inv_ref.shape

    def spread(x_ref, m_ref):
        p1, p2, p3 = _split3(x_ref[...])
        m = m_ref[...]
        return _dot(p1, m) + _dot(p2, m) + _dot(p3, m)

    ca, sa = spread(ca_ref, rep_ref), spread(sa_ref, rep_ref)
    cb, sb = spread(cb_ref, til_ref), spread(sb_ref, til_ref)
    cosm = ca * cb - sa * sb
    nsin = -(sa * cb + ca * sb)
    first = pl.program_id(0) == 0
    rowi = lax.broadcasted_iota(jnp.int32, (h, n), 0)
    coli = lax.broadcasted_iota(jnp.int32, (h, n), 1)
    alt_col = (1 - 2 * (coli % 2)).astype(F32)
    fwd_ref[0, 0] = cosm.astype(BF16)
    fwd_ref[0, 1] = jnp.where(first & (rowi == 0), alt_col, nsin).astype(BF16)
    rowg = lax.broadcasted_iota(jnp.int32, (h, h), 0) + pl.program_id(0) * h
    col0 = lax.broadcasted_iota(jnp.int32, (h, h), 1) == 0
    alt_row = (1 - 2 * (rowg % 2)).astype(F32)
    pieces = []
    for jj in range(n // (2 * h)):
        cp = cosm[:, jj * h:(jj + 1) * h] * (2.0 / n)
        ip = nsin[:, jj * h:(jj + 1) * h] * (2.0 / n)
        if jj == 0:
            cp = jnp.where(col0, 1.0 / n, cp)
            ip = jnp.where(col0, alt_row * (1.0 / n), ip)
        pieces += [cp.astype(BF16), ip.astype(BF16)]
    inv_ref[...] = jnp.concatenate(pieces, axis=1)


def _filter_spectrum_kernel(f_ref, top_ref, bot_ref, o_ref):
    h = DFT_HALF
    bl = top_ref.shape[0]
    acc = _dot(f_ref[:, 0:bl], top_ref[...]) + _dot(f_ref[:, bl:2 * bl], bot_ref[...])
    kr, km = acc[:h], acc[h:]
    row0 = (lax.broadcasted_iota(jnp.int32, kr.shape, 0) == 0) & (pl.program_id(2) == 0)
    o_ref[0, 0:h, :] = kr.astype(o_ref.dtype)
    o_ref[0, h:2 * h, :] = jnp.where(row0, 0.0, km).astype(o_ref.dtype)
    o_ref[0, 2 * h:3 * h, :] = jnp.where(row0, km, kr).astype(o_ref.dtype)


def _filter_spectrum(fwd, kern_bf, nb):
    n = fwd.shape[0]
    bl = n // 2
    nc = kern_bf.shape[1]
    tm, tn = 2 * DFT_HALF, 1024
    nd = 2 * nb - 1
    wrap = 2 * nb
    return pl.pallas_call(
        _filter_spectrum_kernel,
        grid=(nc // tn, nd, n // tm),
        in_specs=[pl.BlockSpec((tm, n), lambda c, d, t: (t, 0)),
                  pl.BlockSpec((bl, tn), lambda c, d, t: ((d - (nb - 1)) % wrap, c)),
                  pl.BlockSpec((bl, tn), lambda c, d, t: ((d - nb) % wrap, c))],
        out_specs=pl.BlockSpec((1, 3 * DFT_HALF, tn), lambda c, d, t: (d, t, c)),
        out_shape=jax.ShapeDtypeStruct((nd, (n // tm) * 3 * DFT_HALF, nc), BF16),
        compiler_params=_cparams(("arbitrary", "arbitrary", "arbitrary")),
        name="hyena_filter_dft",
    )(fwd, kern_bf, kern_bf)


def _dft_fwd_kernel(f_ref, z_ref, k_ref, y_ref, u_ref):
    h = DFT_HALF
    bl = f_ref.shape[1]
    nb = y_ref.shape[1]
    tn = y_ref.shape[3]
    f = f_ref[...]
    for j in range(nb):
        u_ref[j] = _dot(f, z_ref[j * bl:(j + 1) * bl, :]).astype(BF16)
    def mix(t, carry):
        r0 = t * MIX_ROWS
        rr = pl.ds(pl.multiple_of(r0, MIX_ROWS), MIX_ROWS)
        ri = pl.ds(pl.multiple_of(h + r0, MIX_ROWS), MIX_ROWS)
        ri2 = pl.ds(pl.multiple_of(2 * h + r0, MIX_ROWS), MIX_ROWS)
        ur = [u_ref[j, rr, :] for j in range(nb)]
        ui = [u_ref[j, ri, :] for j in range(nb)]
        for i in range(nb):
            yr = yi = None
            for j in range(nb):
                d = i - j + nb - 1
                kr, ki, kr2 = k_ref[d, rr, :], k_ref[d, ri, :], k_ref[d, ri2, :]
                tr = ur[j] * kr - ui[j] * ki
                ti = ur[j] * ki + ui[j] * kr2
                yr = tr if yr is None else yr + tr
                yi = ti if yi is None else yi + ti
            y_ref[0, i, rr, :] = yr.astype(y_ref.dtype)
            y_ref[0, i, ri, :] = yi.astype(y_ref.dtype)
        return carry

    lax.fori_loop(0, h // MIX_ROWS, mix, 0)


def _dft_fwd(fwd, z_arr, z_col, kf, k_col, batch, nb):
    n = fwd.shape[0]
    bl = n // 2
    l = nb * bl
    tm, tn = 2 * DFT_HALF, 512
    cpt = HY_DIM // tn
    nd = kf.shape[0]
    return pl.pallas_call(
        _dft_fwd_kernel,
        grid=(cpt, n // tm, batch),
        in_specs=[pl.BlockSpec((tm, bl), lambda c, t, b: (t, 0)),
                  pl.BlockSpec((l, tn), lambda c, t, b: (b, z_col * cpt + c)),
                  pl.BlockSpec((nd, 3 * DFT_HALF, tn), lambda c, t, b: (0, t, k_col * cpt + c))],
        out_specs=pl.BlockSpec((1, nb, tm, tn), lambda c, t, b: (b, 0, t, c)),
        out_shape=jax.ShapeDtypeStruct((batch, nb, n, HY_DIM), BF16),
        scratch_shapes=[pltpu.VMEM((nb, tm, tn), BF16)],
        compiler_params=_cparams(("arbitrary", "arbitrary", "arbitrary")),
        name="hyena_dft_fwd",
    )(fwd, z_arr, kf)


def _dft_inv_kernel(g_ref, y_ref, gate_ref, z_ref, skip_ref, o_ref):
    conv = _dot(g_ref[...], y_ref[0, 0])
    z = z_ref[...].astype(F32)
    o_ref[...] = (gate_ref[...].astype(F32) * (conv + skip_ref[...] * z)).astype(o_ref.dtype)


def _dft_inv(inv, y, gate_arr, gate_col, z_arr, z_col, skip, batch, nb):
    bl, n = inv.shape
    tn = HY_DIM
    return pl.pallas_call(
        _dft_inv_kernel,
        grid=(batch, nb),
        in_specs=[pl.BlockSpec((bl, n), lambda b, i: (0, 0)),
                  pl.BlockSpec((1, 1, n, tn), lambda b, i: (b, i, 0, 0)),
                  pl.BlockSpec((bl, tn), lambda b, i: (b * nb + i, gate_col)),
                  pl.BlockSpec((bl, tn), lambda b, i: (b * nb + i, z_col)),
                  pl.BlockSpec((1, tn), lambda b, i: (0, 0))],
        out_specs=pl.BlockSpec((bl, tn), lambda b, i: (b * nb + i, 0)),
        out_shape=jax.ShapeDtypeStruct((batch * nb * bl, tn), BF16),
        compiler_params=_cparams(("arbitrary", "arbitrary")),
        name="hyena_dft_inv",
    )(inv, y, gate_arr, z_arr, skip.reshape(1, tn))


def _tap_features(l):
    f32 = np.float32
    t01 = np.linspace(0.0, 1.0, l, dtype=f32)[:, None]
    ang = f32(2.0 * math.pi) * np.arange(l, dtype=f32)[:, None] / f32(l)
    bands = np.linspace(1e-4, HY_BANDS - 1, HY_BANDS, dtype=f32)[None, :]
    z = np.concatenate([t01, np.cos(bands * ang), -np.sin(bands * ang)], axis=-1).astype(f32)
    pos = np.concatenate([np.arange(l), [0], np.arange(l - 1, 0, -1)])
    zz = np.zeros((2 * l, 128), f32)
    zz[:, :z.shape[1]] = z[pos]
    zz[l] = 0.0
    return zz


def _dot_hi(a, b):
    a1 = a.astype(BF16)
    a2 = (a - a1.astype(F32)).astype(BF16)
    b1 = b.astype(BF16)
    b2 = (b - b1.astype(F32)).astype(BF16)
    return _dot(a1, b1) + _dot(a1, b2) + _dot(a2, b1)


def _filter_kernel(zz_ref, w1_ref, b1_ref, fr_ref, w2_ref, b2_ref, w3f_ref, w3b_ref, dl_ref,
                   o_ref, hdn_ref):
    n = zz_ref.shape[0]
    l = n // 2

    @pl.when((pl.program_id(0) == 0) & (pl.program_id(1) == 0))
    def _():
        h1 = jnp.sin(fr_ref[0:1, :] * (_dot_hi(zz_ref[...], w1_ref[...]) + b1_ref[...]))
        hdn_ref[...] = jnp.sin(fr_ref[1:2, :] * (_dot_hi(h1, w2_ref[...]) + b2_ref[...]))

    filt = jnp.concatenate([_dot_hi(hdn_ref[0:l, :], w3f_ref[...]),
                            _dot_hi(hdn_ref[l:n, :], w3b_ref[...])], axis=0)
    decay = jnp.exp(-zz_ref[:, 0:1] * dl_ref[...])
    rowi = lax.broadcasted_iota(jnp.int32, filt.shape, 0)
    kern = jnp.where(rowi == l, 0.0, filt * decay)
    o_ref[...] = (kern / jnp.sum(jnp.abs(kern), axis=0, keepdims=True)).astype(o_ref.dtype)


def _hyena_filters(l, w1, b1, freq, w2, b2, w3):
    n = 2 * l
    tn = 256
    ff = w2.shape[0]
    cpo = HY_DIM // tn
    deltas = np.abs(np.linspace(math.log(HY_DECAY_TARGET) / HY_DECAY_PCT_LONG,
                                math.log(HY_DECAY_TARGET) / HY_DECAY_PCT_SHORT, HY_DIM,
                                dtype=np.float32)).reshape(1, HY_DIM)
    const = lambda o, c: (0, 0)
    return pl.pallas_call(
        _filter_kernel,
        grid=(HY_ORDER, cpo),
        in_specs=[pl.BlockSpec((n, 128), const), pl.BlockSpec((128, ff), const),
                  pl.BlockSpec((1, ff), const), pl.BlockSpec((2, ff), const),
                  pl.BlockSpec((ff, ff), const), pl.BlockSpec((1, ff), const),
                  pl.BlockSpec((ff, tn), lambda o, c: (0, o * 2 * cpo + c)),
                  pl.BlockSpec((ff, tn), lambda o, c: (0, o * 2 * cpo + cpo + c)),
                  pl.BlockSpec((1, tn), lambda o, c: (0, c))],
        out_specs=pl.BlockSpec((n, tn), lambda o, c: (0, o * cpo + c)),
        out_shape=jax.ShapeDtypeStruct((n, HY_ORDER * HY_DIM), BF16),
        scratch_shapes=[pltpu.VMEM((n, ff), F32)],
        compiler_params=_cparams(("arbitrary", "arbitrary")),
        name="hyena_filter_mlp",
    )(jnp.asarray(_tap_features(l)), jnp.pad(w1, ((0, 128 - w1.shape[0]), (0, 0))),
      b1.reshape(1, ff), freq, w2, b2.reshape(1, ff), w3, w3, jnp.asarray(deltas))


def _hyena(p_arr, lp, dft, batch, grid):
    l = p_arr.shape[0] // batch
    fwd, inv = dft
    uc = _short_conv(p_arr, lp['hy_conv_w'].reshape(9, 3 * HY_DIM), lp['hy_conv_b'], batch, grid)
    kern = _hyena_filters(l, lp['hy_ff_w1'], lp['hy_ff_b1'], lp['hy_ff_freq'],
                          lp['hy_ff_w2'], lp['hy_ff_b2'], lp['hy_ff_w3'])
    nb = l // inv.shape[0]
    kf = _filter_spectrum(fwd, kern, nb)
    z_arr, z_col = uc, 0
    for n in range(HY_ORDER):
        y = _dft_fwd(fwd, z_arr, z_col, kf, n, batch, nb)
        z_arr = _dft_inv(inv, y, uc, n + 1, z_arr, z_col, lp['hy_skip'][n], batch, nb)
        z_col = 0
    return z_arr


def _merge_kernel(yhy_ref, ohg_ref, hgg_ref, yrt_ref, b0_ref, b1_ref, b2_ref, x_ref, mod_ref,
                  nw_ref, phy_ref, phg_ref, prt_ref, wo_ref, lng_ref, lnb_ref, o_ref):
    o = ohg_ref[...]
    yhg = (o * lax.rsqrt(jnp.mean(o * o, axis=-1, keepdims=True) + LN_EPS) * nw_ref[...]
           * _silu(hgg_ref[...].astype(F32)))
    m = (jax.nn.sigmoid(b0_ref[...].astype(F32)) * _dot(yhy_ref[...], phy_ref[...])
         + jax.nn.sigmoid(b1_ref[...].astype(F32)) * _dot(yhg.astype(BF16), phg_ref[...])
         + jax.nn.sigmoid(b2_ref[...].astype(F32)) * _dot(yrt_ref[...], prt_ref[...]))
    t = _dot(m.astype(BF16), wo_ref[...])
    gt1 = mod_ref[0][2:3]
    o_ref[...] = _layer_norm_rows(DN_ALPHA * x_ref[...] + gt1 * t, lng_ref[...], lnb_ref[...])


def _merge(y_hy, o_hg, y_rt, p_arr, x, modp, lp, wbf, rows_per_batch):
    m = x.shape[0]
    tm = min(512, m)
    d = D_MODEL
    mi = _mod_index(rows_per_batch, tm)
    row = lambda i: (i, 0)
    const = lambda i: (0, 0)
    gcol = COL_HG * 128 // d
    bcol = COL_BR * 128 // d
    return pl.pallas_call(
        _merge_kernel,
        grid=(m // tm,),
        in_specs=[pl.BlockSpec((tm, d), row), pl.BlockSpec((tm, d), row),
                  pl.BlockSpec((tm, d), lambda i: (i, gcol)),
                  pl.BlockSpec((tm, 2 * d), row),
                  pl.BlockSpec((tm, d), lambda i: (i, bcol)),
                  pl.BlockSpec((tm, d), lambda i: (i, bcol + 1)),
                  pl.BlockSpec((tm, d), lambda i: (i, bcol + 2)),
                  pl.BlockSpec((tm, d), row),
                  pl.BlockSpec((1, 8, d), lambda i: (mi(i), 0, 0)),
                  pl.BlockSpec((1, d), const),
                  pl.BlockSpec((d, d), const), pl.BlockSpec((d, d), const),
                  pl.BlockSpec((2 * d, d), const), pl.BlockSpec((d, d), const),
                  pl.BlockSpec((1, d), const), pl.BlockSpec((1, d), const)],
        out_specs=pl.BlockSpec((tm, d), row),
        out_shape=jax.ShapeDtypeStruct((m, d), F32),
        compiler_params=_cparams(("arbitrary",)),
        name="merge_out_ln",
    )(y_hy, o_hg, p_arr, y_rt, p_arr, p_arr, p_arr, x, modp, lp['hg_norm_w'].reshape(1, d),
      wbf['p_hy'], wbf['p_hg'], wbf['p_rt'], wbf['w_o'],
      lp['ln1_g'].reshape(1, d), lp['ln1_b'].reshape(1, d))


def _ffn_kernel(x_ref, mod_ref, w1_ref, w3_ref, w2_ref, lng_ref, lnb_ref, o_ref, h_ref, acc_ref):
    k = pl.program_id(1)
    m = mod_ref[0]

    @pl.when(k == 0)
    def _():
        h_ref[...] = (x_ref[...] * (1.0 + m[4:5]) + m[3:4]).astype(BF16)
        acc_ref[...] = jnp.zeros(acc_ref.shape, F32)

    h = h_ref[...]
    u = _silu(_dot(h, w1_ref[...])) * _dot(h, w3_ref[...])
    acc_ref[...] += _dot(u.astype(BF16), w2_ref[...])

    @pl.when(k == pl.num_programs(1) - 1)
    def _():
        o_ref[...] = _layer_norm_rows(DN_ALPHA * x_ref[...] + m[5:6] * acc_ref[...],
                                      lng_ref[...], lnb_ref[...])


def _ffn_dense(x, modp, w1, w3, w2, ln_g, ln_b, rows_per_batch):
    m = x.shape[0]
    d = D_MODEL
    dff = w1.shape[1]
    tm = min(1024, rows_per_batch or m)
    tf = dff // 2
    mi = _mod_index(rows_per_batch, tm)
    return pl.pallas_call(
        _ffn_kernel,
        grid=(m // tm, dff // tf),
        in_specs=[pl.BlockSpec((tm, d), lambda i, k: (i, 0)),
                  pl.BlockSpec((1, 8, d), lambda i, k: (mi(i), 0, 0)),
                  pl.BlockSpec((d, tf), lambda i, k: (0, k)),
                  pl.BlockSpec((d, tf), lambda i, k: (0, k)),
                  pl.BlockSpec((tf, d), lambda i, k: (k, 0)),
                  pl.BlockSpec((1, d), lambda i, k: (0, 0)),
                  pl.BlockSpec((1, d), lambda i, k: (0, 0))],
        out_specs=pl.BlockSpec((tm, d), lambda i, k: (i, 0)),
        out_shape=jax.ShapeDtypeStruct((m, d), F32),
        scratch_shapes=[pltpu.VMEM((tm, d), BF16), pltpu.VMEM((tm, d), F32)],
        compiler_params=_cparams(("arbitrary", "arbitrary")),
        name="ffn_dense_ln",
    )(x, modp, w1, w3, w2, ln_g.reshape(1, d), ln_b.reshape(1, d))


def _router_kernel(x_ref, mod_ref, r_ref, h_ref, lg_ref):
    m = mod_ref[0]
    h = x_ref[...] * (1.0 + m[4:5]) + m[3:4]
    h_ref[...] = h
    a1, a2, a3 = _split3(h)
    r1, r2, r3 = _split3(r_ref[...])
    lg_ref[...] = (_dot(a1, r1) + _dot(a1, r2) + _dot(a2, r1)
                   + _dot(a2, r2) + _dot(a1, r3) + _dot(a3, r1))


def _router(x, modp, router_pad, rows_per_batch):
    m = x.shape[0]
    d = D_MODEL
    tm = min(512, m)
    mi = _mod_index(rows_per_batch, tm)
    return pl.pallas_call(
        _router_kernel,
        grid=(m // tm,),
        in_specs=[pl.BlockSpec((tm, d), lambda i: (i, 0)),
                  pl.BlockSpec((1, 8, d), lambda i: (mi(i), 0, 0)),
                  pl.BlockSpec((d, 128), lambda i: (0, 0))],
        out_specs=[pl.BlockSpec((tm, d), lambda i: (i, 0)),
                   pl.BlockSpec((tm, 128), lambda i: (i, 0))],
        out_shape=[jax.ShapeDtypeStruct((m, d), F32), jax.ShapeDtypeStruct((m, 128), F32)],
        compiler_params=_cparams(("arbitrary",)),
        name="moe_router",
    )(x, modp, router_pad)


def _moe_ffn_kernel(be_ref, nu_ref, xp_ref, w1_ref, w3_ref, w2_ref, o_ref, x_ref, acc_ref):
    j = pl.program_id(0)
    k = pl.program_id(1)

    @pl.when(j < nu_ref[0])
    def _():
        @pl.when(k == 0)
        def _():
            x_ref[...] = xp_ref[...].astype(BF16)
            acc_ref[...] = jnp.zeros(acc_ref.shape, F32)

        x = x_ref[...]
        u = _silu(_dot(x, w1_ref[0])) * _dot(x, w3_ref[0])
        acc_ref[...] += _dot(u.astype(BF16), w2_ref[0])

        @pl.when(k == pl.num_programs(1) - 1)
        def _():
            o_ref[...] = acc_ref[...]

    @pl.when((j >= nu_ref[0]) & (k == pl.num_programs(1) - 1))
    def _():
        o_ref[...] = jnp.zeros(o_ref.shape, F32)


def _moe_ffn(xb, block_e, n_used, w1, w3, w2):
    ns = xb.shape[0]
    d = D_MODEL
    tm = MOE_ROWS
    dex = w1.shape[2]
    tf = dex // 2
    grid_spec = pltpu.PrefetchScalarGridSpec(
        num_scalar_prefetch=2,
        grid=(ns // tm, dex // tf),
        in_specs=[pl.BlockSpec((tm, d), lambda j, k, be, nu: (j, 0)),
                  pl.BlockSpec((1, d, tf), lambda j, k, be, nu: (be[j], 0, k)),
                  pl.BlockSpec((1, d, tf), lambda j, k, be, nu: (be[j], 0, k)),
                  pl.BlockSpec((1, tf, d), lambda j, k, be, nu: (be[j], k, 0))],
        out_specs=pl.BlockSpec((tm, d), lambda j, k, be, nu: (j, 0)),
        scratch_shapes=[pltpu.VMEM((tm, d), BF16), pltpu.VMEM((tm, d), F32)])
    return pl.pallas_call(
        _moe_ffn_kernel,
        grid_spec=grid_spec,
        out_shape=jax.ShapeDtypeStruct((ns, d), F32),
        compiler_params=_cparams(("arbitrary", "arbitrary")),
        name="moe_expert_ffn",
    )(block_e, n_used, xb, w1, w3, w2)


def _combine_kernel(x_ref, mod_ref, y0_ref, y1_ref, g_ref, lng_ref, lnb_ref, o_ref):
    m = mod_ref[0]
    g = g_ref[...]
    f = g[:, 0:1] * y0_ref[...] + g[:, 1:2] * y1_ref[...]
    o_ref[...] = _layer_norm_rows(DN_ALPHA * x_ref[...] + m[5:6] * f, lng_ref[...], lnb_ref[...])


def _moe_combine(x, modp, y0, y1, gate_pad, ln_g, ln_b, rows_per_batch):
    m = x.shape[0]
    d = D_MODEL
    tm = min(512, m)
    mi = _mod_index(rows_per_batch, tm)
    row = lambda i: (i, 0)
    return pl.pallas_call(
        _combine_kernel,
        grid=(m // tm,),
        in_specs=[pl.BlockSpec((tm, d), row),
                  pl.BlockSpec((1, 8, d), lambda i: (mi(i), 0, 0)),
                  pl.BlockSpec((tm, d), row), pl.BlockSpec((tm, d), row),
                  pl.BlockSpec((tm, 128), row),
                  pl.BlockSpec((1, d), lambda i: (0, 0)), pl.BlockSpec((1, d), lambda i: (0, 0))],
        out_specs=pl.BlockSpec((tm, d), row),
        out_shape=jax.ShapeDtypeStruct((m, d), F32),
        compiler_params=_cparams(("arbitrary",)),
        name="moe_combine_ln",
    )(x, modp, y0, y1, gate_pad, ln_g.reshape(1, d), ln_b.reshape(1, d))


def _cast_kernel(x_ref, o_ref):
    o_ref[...] = x_ref[...].astype(o_ref.dtype)


def _to_bf16(w, group):
    _, e, a, b = w.shape
    ta, tb = (a // 4, b) if a >= b else (a, b // 4)
    cut_rows = a >= b
    return pl.pallas_call(
        _cast_kernel,
        grid=(e, 4),
        in_specs=[pl.BlockSpec((None, 1, ta, tb),
                               lambda i, r: (group, i, r, 0) if cut_rows else (group, i, 0, r))],
        out_specs=pl.BlockSpec((1, ta, tb), lambda i, r: (i, r, 0) if cut_rows else (i, 0, r)),
        out_shape=jax.ShapeDtypeStruct((e, a, b), BF16),
        compiler_params=_cparams(("arbitrary", "arbitrary")),
        name="weight_cast",
    )(w)


def _moe(x, modp, router, w1, w3, w2, ln_g, ln_b, rows_per_batch):
    n, d = x.shape
    e = router.shape[1]
    h2, logits = _router(x, modp, jnp.pad(router, ((0, 0), (0, 128 - e))), rows_per_batch)
    top_val, top_idx = lax.top_k(logits[:, :e], TOP_K)
    gate = jax.nn.softmax(top_val, axis=-1)
    flat_e = top_idx.reshape(-1)
    flat_t = jnp.repeat(jnp.arange(n, dtype=jnp.int32), TOP_K)
    order = jnp.argsort(flat_e, stable=True).astype(jnp.int32)
    rank = jnp.argsort(order).astype(jnp.int32)
    counts = jnp.sum((flat_e[:, None] == jnp.arange(e)[None, :]).astype(jnp.int32), axis=0)
    starts = jnp.cumsum(counts) - counts
    padded = (counts + MOE_ROWS - 1) // MOE_ROWS * MOE_ROWS
    pad_end = jnp.cumsum(padded)
    pad_start = pad_end - padded
    n_blocks = -(-(n * TOP_K) // MOE_ROWS) + e
    n_slots = n_blocks * MOE_ROWS
    block_start = jnp.arange(n_blocks) * MOE_ROWS
    block_e = jnp.minimum(jnp.sum(block_start[:, None] >= pad_end[None, :], axis=1), e - 1).astype(jnp.int32)
    n_used = (pad_end[-1] // MOE_ROWS).astype(jnp.int32).reshape(1)
    slot_e = jnp.repeat(block_e, MOE_ROWS)
    slot_off = jnp.arange(n_slots, dtype=jnp.int32) - pad_start[slot_e]
    slot_valid = slot_off < counts[slot_e]
    slot_src = jnp.where(slot_valid, starts[slot_e] + slot_off, 0)
    slot_tok = jnp.where(slot_valid, flat_t[order[slot_src]], 0).astype(jnp.int32)
    pos = (pad_start[flat_e] + rank - starts[flat_e]).astype(jnp.int32).reshape(n, TOP_K)
    xb = jnp.take(h2, slot_tok, axis=0, mode="clip")
    yb = _moe_ffn(xb, block_e, n_used, w1, w3, w2)
    y0 = jnp.take(yb, pos[:, 0], axis=0, mode="clip")
    y1 = jnp.take(yb, pos[:, 1], axis=0, mode="clip")
    gate_pad = jnp.pad(gate, ((0, 0), (0, 128 - TOP_K)))
    return _moe_combine(x, modp, y0, y1, gate_pad, ln_g, ln_b, rows_per_batch)


def kernel(x, c, ctx, c_ctx, ada_w, ada_b, w_in, hy_conv_w, hy_conv_b, hy_ff_w1, hy_ff_b1, hy_ff_freq, hy_ff_w2, hy_ff_b2, hy_ff_w3, hy_skip, hg_lb_logits, hg_norm_w, p_hy, p_hg, p_rt, w_o, ln1_g, ln1_b, ln2_g, ln2_b, ffn_w1, ffn_w3, ffn_w2, moe_router, moe_w1, moe_w3, moe_w2):
    batch, l, d = x.shape
    lc = ctx.shape[1]
    assert d == D_MODEL and batch <= 8
    assert l % 512 == 0 and l % GRID_W == 0 and lc % max(RET_CHUNK, DFT_HALF) == 0

    cs = jnp.cumsum(jax.nn.softmax(hg_lb_logits.astype(F32), axis=1), axis=1)
    lower_bounds = cs - cs[:, :1]
    cc = jnp.zeros((16, d), F32).at[:batch].set(c).at[8].set(c_ctx)
    ret_tables = _retention_tables(lc, l)
    dft_lat = _dft_matrices(min(HY_BLOCK, l))
    dft_ctx = _dft_matrices(min(HY_BLOCK, lc))

    x_lat = x.reshape(batch * l, d)
    x_ctx = ctx.reshape(batch * lc, d)
    for i in range(DEPTH):
        need_ctx_out = i < DEPTH - 1
        use_moe = i % 2 == 1
        g = i // 2
        lp = {'hy_conv_w': hy_conv_w[i], 'hy_conv_b': hy_conv_b[i], 'hy_ff_w1': hy_ff_w1[i],
              'hy_ff_b1': hy_ff_b1[i], 'hy_ff_freq': hy_ff_freq[i], 'hy_ff_w2': hy_ff_w2[i],
              'hy_ff_b2': hy_ff_b2[i], 'hy_ff_w3': hy_ff_w3[i], 'hy_skip': hy_skip[i],
              'hg_norm_w': hg_norm_w[i], 'ln1_g': ln1_g[i], 'ln1_b': ln1_b[i]}
        wbf = {'p_hy': p_hy[i].astype(BF16), 'p_hg': p_hg[i].astype(BF16),
               'p_rt': p_rt[i].astype(BF16), 'w_o': w_o[i].astype(BF16)}
        mod = _ada(cc, ada_w, i, ada_b[i])
        modp = jnp.pad(mod.reshape(16, 6, d), ((0, 0), (0, 2), (0, 0)))
        lb = lower_bounds[:, i]
        gl_tab = jnp.pad(jnp.stack([lb, 1.0 - lb], axis=1), ((0, 0), (0, 6), (0, 0)))

        p_lat = _proj(x_lat, modp, w_in, i, N_IN_COLS, l)
        p_ctx = _proj(x_ctx, modp, w_in, i, N_IN_COLS if need_ctx_out else N_STATE_COLS, None)

        o_hg_l, o_hg_c = _gla(p_lat, p_ctx, gl_tab, batch, need_ctx_out)
        y_rt_l, y_rt_c = _retention(p_lat, p_ctx, ret_tables, batch, need_ctx_out)
        y_hy_l = _hyena(p_lat, lp, dft_lat, batch, True)
        x_lat_new = _merge(y_hy_l, o_hg_l, y_rt_l, p_lat, x_lat, modp, lp, wbf, l)
        if need_ctx_out:
            y_hy_c = _hyena(p_ctx, lp, dft_ctx, batch, False)
            x_ctx = _merge(y_hy_c, o_hg_c, y_rt_c, p_ctx, x_ctx, modp, lp, wbf, None)
        x_lat = x_lat_new

        if use_moe:
            if need_ctx_out:
                raise NotImplementedError("MoE layer with a context output is not part of this trunk")
            w1, w3, w2 = _to_bf16(moe_w1, g), _to_bf16(moe_w3, g), _to_bf16(moe_w2, g)
            x_lat = _moe(x_lat, modp, moe_router[g], w1, w3, w2, ln2_g[i], ln2_b[i], l)
        else:
            w1, w3, w2 = ffn_w1[g].astype(BF16), ffn_w3[g].astype(BF16), ffn_w2[g].astype(BF16)
            x_lat = _ffn_dense(x_lat, modp, w1, w3, w2, ln2_g[i], ln2_b[i], l)
            if need_ctx_out:
                x_ctx = _ffn_dense(x_ctx, modp, w1, w3, w2, ln2_g[i], ln2_b[i], None)
    return x_lat.reshape(batch, l, d)
```

```python
import functools
import math

import numpy as np
import jax
import jax.numpy as jnp
from jax import lax
from jax.experimental import pallas as pl
from jax.experimental.pallas import tpu as pltpu

F32 = jnp.float32
BF16 = jnp.bfloat16

D_MODEL = 1024
DEPTH = 2
GRID_W = 64
HY_DIM = 1024
HY_ORDER = 2
HY_BANDS = 16
HY_DECAY_TARGET = 1e-2
HY_DECAY_PCT_SHORT = 0.3
HY_DECAY_PCT_LONG = 1.5
HG_HEADS = 8
HG_HK = 128
RT_HEADS = 4
RT_HK = 256
RT_HV = 512
RT_ROPE_BASE = 10000.0
N_STATE_COLS = 6144
N_IN_COLS = 17408
N_EXPERTS = 8
TOP_K = 2
DN_ALPHA = (2 * DEPTH) ** 0.25
LN_EPS = 1e-5

COL_FF, COL_FB, COL_HI, COL_RK, COL_RV = 0, 8, 16, 24, 32
COL_HQ, COL_HG, COL_RQ, COL_RG, COL_HY, COL_BR = 48, 56, 64, 72, 88, 112

GLA_CHUNK = 256
GLA_LEVELS = 8
GLA_PAD = 8
GLA_MIN_LOG2 = -150.0
RET_CHUNK = 256
DFT_HALF = 256
HY_BLOCK = 512
MIX_ROWS = 16
CONV_ROWS = 256
MOE_ROWS = 512
VMEM_LIMIT = 56 * 1024 * 1024


def _cparams(sem, flags=None):
    return pltpu.CompilerParams(dimension_semantics=sem, vmem_limit_bytes=VMEM_LIMIT, flags=flags)


def _silu(x):
    return x * jax.nn.sigmoid(x)


def _layer_norm_rows(r, g, b):
    mu = jnp.mean(r, axis=-1, keepdims=True)
    d = r - mu
    var = jnp.mean(d * d, axis=-1, keepdims=True)
    return d * lax.rsqrt(var + LN_EPS) * g + b


def _split3(x):
    h = x.astype(BF16)
    r = x - h.astype(F32)
    m = r.astype(BF16)
    l = (r - m.astype(F32)).astype(BF16)
    return h, m, l


def _dot(a, b):
    return jnp.dot(a, b, preferred_element_type=F32)


def _dot_nt(a, b):
    return lax.dot_general(a, b, (((1,), (1,)), ((), ())), preferred_element_type=F32)


def _dot_tn(a, b):
    return lax.dot_general(a, b, (((0,), (0,)), ((), ())), preferred_element_type=F32)


def _ada_kernel(c_ref, w_ref, b_ref, o_ref):
    s = _silu(c_ref[...])
    o_ref[...] = _dot(s.astype(BF16), w_ref[...].astype(BF16)) + b_ref[...]


def _ada(cc, w, layer, b):
    n = w.shape[2]
    tn = 1024
    return pl.pallas_call(
        _ada_kernel,
        grid=(n // tn,),
        in_specs=[pl.BlockSpec((16, D_MODEL), lambda j: (0, 0)),
                  pl.BlockSpec((None, D_MODEL, tn), lambda j: (layer, 0, j)),
                  pl.BlockSpec((1, tn), lambda j: (0, j))],
        out_specs=pl.BlockSpec((16, tn), lambda j: (0, j)),
        out_shape=jax.ShapeDtypeStruct((16, n), F32),
        compiler_params=_cparams(("arbitrary",)),
        name="ada_mod",
    )(cc, w, b.reshape(1, n))


def _mod_index(rows_per_batch, tm):
    if rows_per_batch is None:
        return lambda i: 8
    return lambda i: (i * tm) // rows_per_batch


def _proj_kernel(x_ref, mod_ref, w_ref, o_ref, h_ref):
    @pl.when(pl.program_id(1) == 0)
    def _():
        m = mod_ref[0]
        h_ref[...] = (x_ref[...] * (1.0 + m[1:2]) + m[0:1]).astype(BF16)

    o_ref[...] = _dot(h_ref[...], w_ref[...].astype(BF16)).astype(o_ref.dtype)


def _proj(x, modp, w, layer, n_cols, rows_per_batch):
    m = x.shape[0]
    tm = min(2048, rows_per_batch or m)
    tn = 1024
    mi = _mod_index(rows_per_batch, tm)
    return pl.pallas_call(
        _proj_kernel,
        grid=(m // tm, n_cols // tn),
        in_specs=[pl.BlockSpec((tm, D_MODEL), lambda i, j: (i, 0)),
                  pl.BlockSpec((1, 8, D_MODEL), lambda i, j: (mi(i), 0, 0)),
                  pl.BlockSpec((None, D_MODEL, tn), lambda i, j: (layer, 0, j))],
        out_specs=pl.BlockSpec((tm, tn), lambda i, j: (i, j)),
        out_shape=jax.ShapeDtypeStruct((m, n_cols), BF16),
        scratch_shapes=[pltpu.VMEM((tm, D_MODEL), BF16)],
        compiler_params=_cparams(("arbitrary", "arbitrary")),
        name="in_proj",
    )(x, modp, w)


def _gla_gates(fl, gl_ref, d):
    lb = gl_ref[d, 0:1, :]
    oml = gl_ref[d, 1:2, :]
    t = jnp.exp(-jnp.abs(fl))
    r = 1.0 / (1.0 + t)
    tr = t * r
    pos = fl >= 0.0
    f = lb + oml * jnp.where(pos, r, tr)
    key = oml * jnp.where(pos, tr, r)
    return jnp.maximum(jnp.log2(f), GLA_MIN_LOG2), key


def _gla_cumsum(g, rev):
    c = g.shape[0]
    r = lax.broadcasted_iota(jnp.int32, (c, c), 0)
    u = lax.broadcasted_iota(jnp.int32, (c, c), 1)
    tri = jnp.where((u >= r) if rev else (u <= r), 1.0, 0.0).astype(BF16)
    return _dot(jnp.concatenate([tri, tri, tri], axis=1), jnp.concatenate(_split3(g), axis=0))


def _gla_state_update(st, b, kk, vv, rev):
    c = b.shape[0]
    end = 0 if rev else c - 1
    b_end = b[end:end + 1, :]
    khat = (kk * jnp.exp2(b_end - b)).astype(BF16)
    return st * jnp.exp2(b_end) + _dot_tn(vv, khat)


def _gla_level_table():
    c = GLA_CHUNK
    t = np.arange(c)[:, None]
    s = np.arange(c)[None, :]
    top_bit = np.floor(np.log2(np.maximum(t ^ s, 1))).astype(np.int32)
    fwd = np.where(t > s, top_bit, np.where(t == s, GLA_LEVELS, -1))
    return np.stack([fwd, fwd.T]).astype(np.int32)


def _gla_ref_rows(b, bpad_ref, level, rev):
    c = b.shape[0]
    half = 1 << level
    blk = 2 * half
    idx = half if rev else half - 1
    if blk % 8 == 0:
        r = b.reshape(c // blk, blk, HG_HK)[:, idx:idx + 1, :]
        return jnp.broadcast_to(r, (c // blk, blk, HG_HK)).reshape(c, HG_HK)
    m = lax.broadcasted_iota(jnp.int32, (c, HG_HK), 0) % blk
    out = b
    for v in range(blk):
        if v != idx:
            off = GLA_PAD + idx - v
            out = jnp.where(m == v, bpad_ref[off:off + c, :], out)
    return out


def _gla_chunk_out(b, kk, vv, qq, st, rev, lvt, bpad_ref):
    c = GLA_CHUNK
    o = _dot_nt((qq * jnp.exp2(b)).astype(BF16), st.astype(BF16))
    bpad_ref[GLA_PAD:GLA_PAD + c, :] = b
    qq_b, kk_b = qq.astype(BF16), kk.astype(BF16)
    sc = jnp.zeros((c, c), F32)
    for level in range(GLA_LEVELS):
        half = 1 << level
        blk = 2 * half
        ref = _gla_ref_rows(b, bpad_ref, level, rev)
        kt = kk_b * jnp.exp2(ref - b).astype(BF16)
        if half % 8:
            p = _dot_nt(qq_b * jnp.exp2(b - ref).astype(BF16), kt)
            sc = jnp.where(lvt == level, p, sc)
        else:
            nblk = c // blk
            lo, hi = (0, half) if rev else (half, blk)

            def part(x):
                return x.reshape(nblk, blk, x.shape[-1])[:, lo:hi, :]

            qt = (part(qq) * jnp.exp2(part(b) - part(ref))).reshape(c // 2, HG_HK)
            p = _dot_nt(qt.astype(BF16), kt).reshape(nblk, half, c)
            new = jnp.where(part(lvt) == level, p, part(sc))
            rest = sc.reshape(nblk, blk, c)[:, half:blk, :] if rev else sc.reshape(nblk, blk, c)[:, 0:half, :]
            sc = jnp.concatenate([new, rest] if rev else [rest, new], axis=1).reshape(c, c)
    sc = jnp.where(lvt == GLA_LEVELS, _dot_nt(qq_b, kk_b), sc)
    return o + _dot(sc.astype(BF16), vv)


def _gla_kernel(*refs, need_ctx_out, n_lat, n_ctx):
    if need_ctx_out:
        ffl, fbl, il, ql, ffc, fbc, ic, qc, gl_ref, lv_ref, ol_ref, oc_ref = refs[:12]
        bl, kl, bc, kc, qsl, qsc, bpf, bpr = refs[12:]
    else:
        ffl, fbl, il, ql, ffc, fbc, ic, gl_ref, lv_ref, ol_ref = refs[:10]
        bl, kl, bc, kc, qsl, bpf, bpr = refs[10:]
        qc = oc_ref = qsc = None
    c = GLA_CHUNK
    q_scale = HG_HK ** -0.5
    bpf[...] = jnp.zeros(bpf.shape, F32)
    bpr[...] = jnp.zeros(bpr.shape, F32)
    ol_ref[...] = jnp.zeros(ol_ref.shape, F32)
    if need_ctx_out:
        oc_ref[...] = jnp.zeros(oc_ref.shape, F32)

    def chunk_rows(idx):
        return pl.ds(pl.multiple_of(idx * c, c), c)

    def prepare(f_refs, q_ref, b_sc, k_sc, q_sc, n):
        def body(j, carry):
            rows = chunk_rows(j)
            for d in range(2):
                g, kk = _gla_gates(f_refs[d][rows, :].astype(F32), gl_ref, d)
                b_sc[d, rows, :] = _gla_cumsum(g, d == 1)
                k_sc[d, rows, :] = kk
            if q_ref is not None:
                q_sc[rows, :] = _silu(q_ref[rows, :].astype(F32)) * q_scale
            return carry
        lax.fori_loop(0, n, body, 0, unroll=min(4, n))

    prepare((ffc, fbc), qc, bc, kc, qsc, n_ctx)
    prepare((ffl, fbl), ql, bl, kl, qsl, n_lat)

    def one_dir(d, b_sc, k_sc, i_ref, q_sc, o_ref, idx, st):
        rev = d == 1
        rows = chunk_rows(idx)
        b, kk, vv = b_sc[d, rows, :], k_sc[d, rows, :], i_ref[rows, :]
        if q_sc is not None:
            o_ref[rows, :] += _gla_chunk_out(b, kk, vv, q_sc[rows, :], st, rev, lv_ref[d],
                                             bpr if rev else bpf)
        return _gla_state_update(st, b, kk, vv, rev)

    def ctx_body(j, carry):
        return (one_dir(0, bc, kc, ic, qsc, oc_ref, j, carry[0]),
                one_dir(1, bc, kc, ic, qsc, oc_ref, n_ctx - 1 - j, carry[1]))

    def lat_body(j, carry):
        return (one_dir(0, bl, kl, il, qsl, ol_ref, j, carry[0]),
                one_dir(1, bl, kl, il, qsl, ol_ref, n_lat - 1 - j, carry[1]))

    zero = jnp.zeros((HG_HK, HG_HK), F32)
    carry = lax.fori_loop(0, n_ctx, ctx_body, (zero, zero), unroll=min(2, n_ctx))
    lax.fori_loop(0, n_lat, lat_body, carry, unroll=min(2, n_lat))


def _gla(p_lat, p_ctx, gl_tab, batch, need_ctx_out):
    l = p_lat.shape[0] // batch
    lc = p_ctx.shape[0] // batch
    w = HG_HK
    c = GLA_CHUNK

    def col(base):
        return lambda b, h: (b, base + h)

    lat_specs = [pl.BlockSpec((l, w), col(cb)) for cb in (COL_FF, COL_FB, COL_HI, COL_HQ)]
    ctx_cols = (COL_FF, COL_FB, COL_HI) + ((COL_HQ,) if need_ctx_out else ())
    ctx_specs = [pl.BlockSpec((lc, w), col(cb)) for cb in ctx_cols]
    in_specs = lat_specs + ctx_specs + [pl.BlockSpec((2, 8, w), lambda b, h: (0, 0, h)),
                                        pl.BlockSpec((2, c, c), lambda b, h: (0, 0, 0))]
    args = [p_lat] * 4 + [p_ctx] * len(ctx_cols) + [gl_tab, jnp.asarray(_gla_level_table())]
    out_specs = [pl.BlockSpec((l, w), lambda b, h: (b, h))]
    out_shape = [jax.ShapeDtypeStruct((batch * l, HG_HEADS * w), F32)]
    scratch = [pltpu.VMEM((2, l, w), F32), pltpu.VMEM((2, l, w), F32),
               pltpu.VMEM((2, lc, w), F32), pltpu.VMEM((2, lc, w), F32), pltpu.VMEM((l, w), F32)]
    if need_ctx_out:
        out_specs.append(pl.BlockSpec((lc, w), lambda b, h: (b, h)))
        out_shape.append(jax.ShapeDtypeStruct((batch * lc, HG_HEADS * w), F32))
        scratch.append(pltpu.VMEM((lc, w), F32))
    scratch += [pltpu.VMEM((c + 2 * GLA_PAD, w), F32)] * 2
    outs = pl.pallas_call(
        functools.partial(_gla_kernel, need_ctx_out=need_ctx_out, n_lat=l // c, n_ctx=lc // c),
        grid=(batch, HG_HEADS),
        in_specs=in_specs,
        out_specs=out_specs,
        out_shape=out_shape,
        scratch_shapes=scratch,
        compiler_params=_cparams(("arbitrary", "arbitrary")),
        name="hgrn2_scan",
    )(*args)
    return outs[0], (outs[1] if need_ctx_out else None)


def _rotate(x, cos, sin):
    half = x.shape[-1] // 2
    x1, x2 = x[:, :half], x[:, half:]
    return jnp.concatenate([x1 * cos - x2 * sin, x1 * sin + x2 * cos], axis=-1)


def _ret_kernel(*refs, need_ctx_out, n_lat, n_ctx):
    if need_ctx_out:
        (ql, kl, vl, gl, qc, kc, vc, gc, cos_ref, sin_ref, intra_ref, qk_ref, sd_ref,
         yl_ref, yc_ref, qrl, krl, qrc, krc, ol, oc, s_fwd, s_rev) = refs
    else:
        (ql, kl, vl, gl, kc, vc, cos_ref, sin_ref, intra_ref, qk_ref, sd_ref,
         yl_ref, qrl, krl, krc, ol, s_fwd, s_rev) = refs
        qc = gc = yc_ref = qrc = oc = None
    s_refs = (s_fwd, s_rev)
    c = RET_CHUNK
    lc = n_ctx * c
    k_scale = RT_HK ** -0.5

    def rot_all(src, dst, n, pos0, scale):
        def body(j, carry):
            rows = pl.ds(pl.multiple_of(j * c, c), c)
            prow = pl.ds(pl.multiple_of(pos0 + j * c, c), c)
            x = src[rows, :].astype(F32)
            dst[rows, :] = (_rotate(x, cos_ref[prow, :], sin_ref[prow, :]) * scale).astype(BF16)
            return carry
        lax.fori_loop(0, n, body, 0)

    rot_all(kl, krl, n_lat, lc, k_scale)
    rot_all(kc, krc, n_ctx, 0, k_scale)
    rot_all(ql, qrl, n_lat, lc, 1.0)
    if need_ctx_out:
        rot_all(qc, qrc, n_ctx, 0, 1.0)

    for s_ref in s_refs:
        s_ref[...] = jnp.zeros(s_ref.shape, F32)
    ol[...] = jnp.zeros(ol.shape, F32)
    if need_ctx_out:
        oc[...] = jnp.zeros(oc.shape, F32)

    def update(d, kr, v_ref, rows):
        kd = qk_ref[d, 0, c:2 * c, :]
        sdec = sd_ref[d, 0, 0:1, :]
        kh = (kr[rows, :].astype(F32) * jnp.concatenate([kd, kd], axis=1)).astype(BF16)
        s_ref = s_refs[d]
        s_ref[...] = (s_ref[...] * jnp.concatenate([sdec] * (RT_HV // 128), axis=1)
                      + _dot_tn(kh, v_ref[rows, :]))

    def out_step(d, qr, kr, v_ref, o_ref, rows):
        qd = qk_ref[d, 0, 0:c, :]
        qcb = qr[rows, :]
        sc = _dot_nt(qcb, kr[rows, :]) * intra_ref[d, 0]
        qh = (qcb.astype(F32) * jnp.concatenate([qd, qd], axis=1)).astype(BF16)
        o_ref[rows, :] += (_dot(qh, s_refs[d][...].astype(BF16))
                           + _dot(sc.astype(BF16), v_ref[rows, :]))
        update(d, kr, v_ref, rows)

    def ctx_body(j, carry):
        for d in range(2):
            idx = (n_ctx - 1 - j) if d == 1 else j
            rows = pl.ds(pl.multiple_of(idx * c, c), c)
            if need_ctx_out:
                out_step(d, qrc, krc, vc, oc, rows)
            else:
                update(d, krc, vc, rows)
        return carry

    def lat_body(j, carry):
        for d in range(2):
            idx = (n_lat - 1 - j) if d == 1 else j
            out_step(d, qrl, krl, vl, ol, pl.ds(pl.multiple_of(idx * c, c), c))
        return carry

    lax.fori_loop(0, n_ctx, ctx_body, 0)
    lax.fori_loop(0, n_lat, lat_body, 0, unroll=4)

    def readout(o_ref, g_ref, y_ref, n):
        def body(j, carry):
            rows = pl.ds(pl.multiple_of(j * c, c), c)
            o = o_ref[rows, :]
            y = o * lax.rsqrt(jnp.mean(o * o, axis=-1, keepdims=True) + LN_EPS)
            y_ref[rows, :] = (y * _silu(g_ref[rows, :].astype(F32))).astype(y_ref.dtype)
            return carry
        lax.fori_loop(0, n, body, 0)

    readout(ol, gl, yl_ref, n_lat)
    if need_ctx_out:
        readout(oc, gc, yc_ref, n_ctx)


def _retention_tables(lc, l):
    half = RT_HK // 2
    inv = 1.0 / (RT_ROPE_BASE ** jnp.linspace(0.0, 1.0, half, dtype=F32))
    ang = jnp.arange(lc + l, dtype=F32)[:, None] * inv[None, :]
    j = jnp.arange(2 * RT_HEADS, dtype=F32)
    lg_all = jnp.log1p(-jnp.exp2(-5.0 - j))
    c = RET_CHUNK
    pos = jnp.arange(c, dtype=F32)
    rel = pos[:, None] - pos[None, :]
    intra, qk, sd = [], [], []
    for d in range(2):
        lg = lg_all[d::2]
        m = jnp.where(rel >= 0, jnp.exp(jnp.maximum(rel, 0.0)[None] * lg[:, None, None]), 0.0)
        qdec = jnp.exp((pos + 1.0)[None, :] * lg[:, None])
        kdec = jnp.exp((c - 1.0 - pos)[None, :] * lg[:, None])
        if d == 1:
            m = jnp.swapaxes(m, 1, 2)
            qdec = qdec[:, ::-1]
            kdec = kdec[:, ::-1]
        intra.append(m)
        qk.append(jnp.broadcast_to(jnp.concatenate([qdec, kdec], axis=1)[:, :, None],
                                   (RT_HEADS, 2 * c, 128)))
        sd.append(jnp.broadcast_to(jnp.exp(c * lg)[:, None, None], (RT_HEADS, 8, 128)))
    return (jnp.cos(ang), jnp.sin(ang), jnp.stack(intra), jnp.stack(qk), jnp.stack(sd))


def _retention(p_lat, p_ctx, tables, batch, need_ctx_out):
    l = p_lat.shape[0] // batch
    lc = p_ctx.shape[0] // batch
    cos, sin, intra, qk, sd = tables
    c = RET_CHUNK

    def col(base):
        return lambda b, h: (b, base + h)

    kq, kk_, kv, kg = COL_RQ * 128 // RT_HK, COL_RK * 128 // RT_HK, COL_RV * 128 // RT_HV, COL_RG * 128 // RT_HV
    lat_specs = [pl.BlockSpec((l, RT_HK), col(kq)), pl.BlockSpec((l, RT_HK), col(kk_)),
                 pl.BlockSpec((l, RT_HV), col(kv)), pl.BlockSpec((l, RT_HV), col(kg))]
    if need_ctx_out:
        ctx_specs = [pl.BlockSpec((lc, RT_HK), col(kq)), pl.BlockSpec((lc, RT_HK), col(kk_)),
                     pl.BlockSpec((lc, RT_HV), col(kv)), pl.BlockSpec((lc, RT_HV), col(kg))]
    else:
        ctx_specs = [pl.BlockSpec((lc, RT_HK), col(kk_)), pl.BlockSpec((lc, RT_HV), col(kv))]
    tab_specs = [pl.BlockSpec((lc + l, 128), lambda b, h: (0, 0)),
                 pl.BlockSpec((lc + l, 128), lambda b, h: (0, 0)),
                 pl.BlockSpec((2, 1, c, c), lambda b, h: (0, h, 0, 0)),
                 pl.BlockSpec((2, 1, 2 * c, 128), lambda b, h: (0, h, 0, 0)),
                 pl.BlockSpec((2, 1, 8, 128), lambda b, h: (0, h, 0, 0))]
    args = [p_lat] * 4 + [p_ctx] * len(ctx_specs) + [cos, sin, intra, qk, sd]
    out_specs = [pl.BlockSpec((l, RT_HV), lambda b, h: (b, h))]
    out_shape = [jax.ShapeDtypeStruct((batch * l, RT_HEADS * RT_HV), BF16)]
    scratch = [pltpu.VMEM((l, RT_HK), BF16), pltpu.VMEM((l, RT_HK), BF16)]
    if need_ctx_out:
        out_specs.append(pl.BlockSpec((lc, RT_HV), lambda b, h: (b, h)))
        out_shape.append(jax.ShapeDtypeStruct((batch * lc, RT_HEADS * RT_HV), BF16))
        scratch += [pltpu.VMEM((lc, RT_HK), BF16), pltpu.VMEM((lc, RT_HK), BF16),
                    pltpu.VMEM((l, RT_HV), F32), pltpu.VMEM((lc, RT_HV), F32)]
    else:
        scratch += [pltpu.VMEM((lc, RT_HK), BF16), pltpu.VMEM((l, RT_HV), F32)]
    scratch += [pltpu.VMEM((RT_HK, RT_HV), F32)] * 2
    outs = pl.pallas_call(
        functools.partial(_ret_kernel, need_ctx_out=need_ctx_out, n_lat=l // c, n_ctx=lc // c),
        grid=(batch, RT_HEADS),
        in_specs=lat_specs + ctx_specs + tab_specs,
        out_specs=out_specs,
        out_shape=out_shape,
        scratch_shapes=scratch,
        compiler_params=_cparams(("arbitrary", "arbitrary")),
        name="retention_scan",
    )(*args)
    return outs[0], (outs[1] if need_ctx_out else None)


def _conv_shift_matrices(grid):
    rb = CONV_ROWS
    t = np.arange(rb)
    edge_lo = (t % GRID_W == 0) if grid else (t == 0)
    edge_hi = (t % GRID_W == GRID_W - 1) if grid else (t == rb - 1)
    s0 = ((t[:, None] - 1 == t[None, :]) & ~edge_lo[:, None]).astype(np.float32)
    s2 = ((t[:, None] + 1 == t[None, :]) & ~edge_hi[:, None]).astype(np.float32)
    return s0, s2


def _short_conv_kernel(x_ref, w_ref, b_ref, s0_ref, s2_ref, o_ref, xpad_ref, *, grid):
    l, tc = x_ref.shape
    p = GRID_W
    rb = CONV_ROWS
    zeros = jnp.zeros((p, tc), BF16)
    xpad_ref[0:p, :] = zeros
    xpad_ref[p + l:p + l + p, :] = zeros
    xpad_ref[p:p + l, :] = x_ref[...]
    w = [w_ref[k:k + 1, :].astype(BF16) for k in range(9)]
    rows = range(3) if grid else (1,)

    def body(i, carry):
        r0 = i * rb
        part = []
        for dj in range(3):
            acc = None
            for di in rows:
                start = pl.multiple_of(r0 + p + (di - 1) * GRID_W, GRID_W)
                t = xpad_ref[pl.ds(start, rb), :] * w[di * 3 + dj]
                acc = t if acc is None else acc + t
            part.append(acc)
        out = (part[1].astype(F32) + _dot(s0_ref[...], part[0]) + _dot(s2_ref[...], part[2])
               + b_ref[...])
        o_ref[pl.ds(pl.multiple_of(r0, rb), rb), :] = out.astype(o_ref.dtype)
        return carry

    lax.fori_loop(0, l // rb, body, 0, unroll=min(4, l // rb))


def _short_conv(p_arr, w9, bias, batch, grid):
    l = p_arr.shape[0] // batch
    assert l % CONV_ROWS == 0 and (grid or l == CONV_ROWS)
    c3 = 3 * HY_DIM
    tc = 256
    base = COL_HY * 128 // tc
    s0, s2 = _conv_shift_matrices(grid)
    return pl.pallas_call(
        functools.partial(_short_conv_kernel, grid=grid),
        grid=(batch, c3 // tc),
        in_specs=[pl.BlockSpec((l, tc), lambda b, j: (b, base + j)),
                  pl.BlockSpec((9, tc), lambda b, j: (0, j)),
                  pl.BlockSpec((1, tc), lambda b, j: (0, j)),
                  pl.BlockSpec((CONV_ROWS, CONV_ROWS), lambda b, j: (0, 0)),
                  pl.BlockSpec((CONV_ROWS, CONV_ROWS), lambda b, j: (0, 0))],
        out_specs=pl.BlockSpec((l, tc), lambda b, j: (b, j)),
        out_shape=jax.ShapeDtypeStruct((batch * l, c3), BF16),
        scratch_shapes=[pltpu.VMEM((l + 2 * GRID_W, tc), BF16)],
        compiler_params=_cparams(("arbitrary", "arbitrary")),
        name="hyena_short_conv",
    )(p_arr, w9, bias.reshape(1, c3), jnp.asarray(s0, BF16), jnp.asarray(s2, BF16))


def _dft_matrices(l):
    n = 2 * l
    s = int(round(math.sqrt(n)))
    while n % s:
        s -= 1
    q = n // s
    f = np.arange(l, dtype=np.int64)
    ang_a = 2.0 * np.pi * ((f[:, None] * s * np.arange(q)[None, :]) % n) / n
    ang_b = 2.0 * np.pi * ((f[:, None] * np.arange(s)[None, :]) % n) / n
    t = np.arange(n)
    rep = (t[None, :] // s == np.arange(q)[:, None]).astype(np.float32)
    til = (t[None, :] % s == np.arange(s)[:, None]).astype(np.float32)
    h = DFT_HALF
    row = lambda j: (j, 0)
    const = lambda j: (0, 0)
    fwd, inv = pl.pallas_call(
        _dft_build_kernel,
        grid=(l // h,),
        in_specs=[pl.BlockSpec((h, q), row), pl.BlockSpec((h, q), row),
                  pl.BlockSpec((h, s), row), pl.BlockSpec((h, s), row),
                  pl.BlockSpec((q, n), const), pl.BlockSpec((s, n), const)],
        out_specs=[pl.BlockSpec((1, 2, h, n), lambda j: (j, 0, 0, 0)),
                   pl.BlockSpec((h, n), row)],
        out_shape=[jax.ShapeDtypeStruct((l // h, 2, h, n), BF16),
                   jax.ShapeDtypeStruct((l, n), BF16)],
        compiler_params=_cparams(("arbitrary",)),
        name="dft_build",
    )(jnp.asarray(np.cos(ang_a), F32), jnp.asarray(np.sin(ang_a), F32),
      jnp.asarray(np.cos(ang_b), F32), jnp.asarray(np.sin(ang_b), F32),
      jnp.asarray(rep, BF16), jnp.asarray(til, BF16))
    return fwd.reshape(n, n), inv


def _dft_build_kernel(ca_ref, sa_ref, cb_ref, sb_ref, rep_ref, til_ref, fwd_ref, inv_ref):
    h, n = inv_ref.shape

    def spread(x_ref, m_ref):
        p1, p2, p3 = _split3(x_ref[...])
        m = m_ref[...]
        return _dot(p1, m) + _dot(p2, m) + _dot(p3, m)

    ca, sa = spread(ca_ref, rep_ref), spread(sa_ref, rep_ref)
    cb, sb = spread(cb_ref, til_ref), spread(sb_ref, til_ref)
    cosm = ca * cb - sa * sb
    nsin = -(sa * cb + ca * sb)
    first = pl.program_id(0) == 0
    rowi = lax.broadcasted_iota(jnp.int32, (h, n), 0)
    coli = lax.broadcasted_iota(jnp.int32, (h, n), 1)
    alt_col = (1 - 2 * (coli % 2)).astype(F32)
    fwd_ref[0, 0] = cosm.astype(BF16)
    fwd_ref[0, 1] = jnp.where(first & (rowi == 0), alt_col, nsin).astype(BF16)
    rowg = lax.broadcasted_iota(jnp.int32, (h, h), 0) + pl.program_id(0) * h
    col0 = lax.broadcasted_iota(jnp.int32, (h, h), 1) == 0
    alt_row = (1 - 2 * (rowg % 2)).astype(F32)
    pieces = []
    for jj in range(n // (2 * h)):
        cp = cosm[:, jj * h:(jj + 1) * h] * (2.0 / n)
        ip = nsin[:, jj * h:(jj + 1) * h] * (2.0 / n)
        if jj == 0:
            cp = jnp.where(col0, 1.0 / n, cp)
            ip = jnp.where(col0, alt_row * (1.0 / n), ip)
        pieces += [cp.astype(BF16), ip.astype(BF16)]
    inv_ref[...] = jnp.concatenate(pieces, axis=1)


def _filter_spectrum_kernel(f_ref, top_ref, bot_ref, o_ref):
    h = DFT_HALF
    bl = top_ref.shape[0]
    acc = _dot(f_ref[:, 0:bl], top_ref[...]) + _dot(f_ref[:, bl:2 * bl], bot_ref[...])
    kr, km = acc[:h], acc[h:]
    row0 = (lax.broadcasted_iota(jnp.int32, kr.shape, 0) == 0) & (pl.program_id(2) == 0)
    o_ref[0, 0:h, :] = kr.astype(o_ref.dtype)
    o_ref[0, h:2 * h, :] = jnp.where(row0, 0.0, km).astype(o_ref.dtype)
    o_ref[0, 2 * h:3 * h, :] = jnp.where(row0, km, kr).astype(o_ref.dtype)


def _filter_spectrum(fwd, kern_bf, nb):
    n = fwd.shape[0]
    bl = n // 2
    nc = kern_bf.shape[1]
    tm, tn = 2 * DFT_HALF, 1024
    nd = 2 * nb - 1
    wrap = 2 * nb
    return pl.pallas_call(
        _filter_spectrum_kernel,
        grid=(nc // tn, nd, n // tm),
        in_specs=[pl.BlockSpec((tm, n), lambda c, d, t: (t, 0)),
                  pl.BlockSpec((bl, tn), lambda c, d, t: ((d - (nb - 1)) % wrap, c)),
                  pl.BlockSpec((bl, tn), lambda c, d, t: ((d - nb) % wrap, c))],
        out_specs=pl.BlockSpec((1, 3 * DFT_HALF, tn), lambda c, d, t: (d, t, c)),
        out_shape=jax.ShapeDtypeStruct((nd, (n // tm) * 3 * DFT_HALF, nc), BF16),
        compiler_params=_cparams(("arbitrary", "arbitrary", "arbitrary")),
        name="hyena_filter_dft",
    )(fwd, kern_bf, kern_bf)


def _dft_fwd_kernel(f_ref, z_ref, k_ref, y_ref, u_ref):
    h = DFT_HALF
    bl = f_ref.shape[1]
    nb = y_ref.shape[1]
    tn = y_ref.shape[3]
    f = f_ref[...]
    for j in range(nb):
        u_ref[j] = _dot(f, z_ref[j * bl:(j + 1) * bl, :]).astype(BF16)
    def mix(t, carry):
        r0 = t * MIX_ROWS
        rr = pl.ds(pl.multiple_of(r0, MIX_ROWS), MIX_ROWS)
        ri = pl.ds(pl.multiple_of(h + r0, MIX_ROWS), MIX_ROWS)
        ri2 = pl.ds(pl.multiple_of(2 * h + r0, MIX_ROWS), MIX_ROWS)
        ur = [u_ref[j, rr, :] for j in range(nb)]
        ui = [u_ref[j, ri, :] for j in range(nb)]
        for i in range(nb):
            yr = yi = None
            for j in range(nb):
                d = i - j + nb - 1
                kr, ki, kr2 = k_ref[d, rr, :], k_ref[d, ri, :], k_ref[d, ri2, :]
                tr = ur[j] * kr - ui[j] * ki
                ti = ur[j] * ki + ui[j] * kr2
                yr = tr if yr is None else yr + tr
                yi = ti if yi is None else yi + ti
            y_ref[0, i, rr, :] = yr.astype(y_ref.dtype)
            y_ref[0, i, ri, :] = yi.astype(y_ref.dtype)
        return carry

    lax.fori_loop(0, h // MIX_ROWS, mix, 0)


def _dft_fwd(fwd, z_arr, z_col, kf, k_col, batch, nb):
    n = fwd.shape[0]
    bl = n // 2
    l = nb * bl
    tm, tn = 2 * DFT_HALF, 512
    cpt = HY_DIM // tn
    nd = kf.shape[0]
    return pl.pallas_call(
        _dft_fwd_kernel,
        grid=(cpt, n // tm, batch),
        in_specs=[pl.BlockSpec((tm, bl), lambda c, t, b: (t, 0)),
                  pl.BlockSpec((l, tn), lambda c, t, b: (b, z_col * cpt + c)),
                  pl.BlockSpec((nd, 3 * DFT_HALF, tn), lambda c, t, b: (0, t, k_col * cpt + c))],
        out_specs=pl.BlockSpec((1, nb, tm, tn), lambda c, t, b: (b, 0, t, c)),
        out_shape=jax.ShapeDtypeStruct((batch, nb, n, HY_DIM), BF16),
        scratch_shapes=[pltpu.VMEM((nb, tm, tn), BF16)],
        compiler_params=_cparams(("arbitrary", "arbitrary", "arbitrary")),
        name="hyena_dft_fwd",
    )(fwd, z_arr, kf)


def _dft_inv_kernel(g_ref, y_ref, gate_ref, z_ref, skip_ref, o_ref):
    conv = _dot(g_ref[...], y_ref[0, 0])
    z = z_ref[...].astype(F32)
    o_ref[...] = (gate_ref[...].astype(F32) * (conv + skip_ref[...] * z)).astype(o_ref.dtype)


def _dft_inv(inv, y, gate_arr, gate_col, z_arr, z_col, skip, batch, nb):
    bl, n = inv.shape
    tn = HY_DIM
    return pl.pallas_call(
        _dft_inv_kernel,
        grid=(batch, nb),
        in_specs=[pl.BlockSpec((bl, n), lambda b, i: (0, 0)),
                  pl.BlockSpec((1, 1, n, tn), lambda b, i: (b, i, 0, 0)),
                  pl.BlockSpec((bl, tn), lambda b, i: (b * nb + i, gate_col)),
                  pl.BlockSpec((bl, tn), lambda b, i: (b * nb + i, z_col)),
                  pl.BlockSpec((1, tn), lambda b, i: (0, 0))],
        out_specs=pl.BlockSpec((bl, tn), lambda b, i: (b * nb + i, 0)),
        out_shape=jax.ShapeDtypeStruct((batch * nb * bl, tn), BF16),
        compiler_params=_cparams(("arbitrary", "arbitrary")),
        name="hyena_dft_inv",
    )(inv, y, gate_arr, z_arr, skip.reshape(1, tn))


def _tap_features(l):
    f32 = np.float32
    t01 = np.linspace(0.0, 1.0, l, dtype=f32)[:, None]
    ang = f32(2.0 * math.pi) * np.arange(l, dtype=f32)[:, None] / f32(l)
    bands = np.linspace(1e-4, HY_BANDS - 1, HY_BANDS, dtype=f32)[None, :]
    z = np.concatenate([t01, np.cos(bands * ang), -np.sin(bands * ang)], axis=-1).astype(f32)
    pos = np.concatenate([np.arange(l), [0], np.arange(l - 1, 0, -1)])
    zz = np.zeros((2 * l, 128), f32)
    zz[:, :z.shape[1]] = z[pos]
    zz[l] = 0.0
    return zz


def _dot_hi(a, b):
    a1 = a.astype(BF16)
    a2 = (a - a1.astype(F32)).astype(BF16)
    b1 = b.astype(BF16)
    b2 = (b - b1.astype(F32)).astype(BF16)
    return _dot(a1, b1) + _dot(a1, b2) + _dot(a2, b1)


def _filter_kernel(zz_ref, w1_ref, b1_ref, fr_ref, w2_ref, b2_ref, w3f_ref, w3b_ref, dl_ref,
                   o_ref, hdn_ref):
    n = zz_ref.shape[0]
    l = n // 2

    @pl.when((pl.program_id(0) == 0) & (pl.program_id(1) == 0))
    def _():
        h1 = jnp.sin(fr_ref[0:1, :] * (_dot_hi(zz_ref[...], w1_ref[...]) + b1_ref[...]))
        hdn_ref[...] = jnp.sin(fr_ref[1:2, :] * (_dot_hi(h1, w2_ref[...]) + b2_ref[...]))

    filt = jnp.concatenate([_dot_hi(hdn_ref[0:l, :], w3f_ref[...]),
                            _dot_hi(hdn_ref[l:n, :], w3b_ref[...])], axis=0)
    decay = jnp.exp(-zz_ref[:, 0:1] * dl_ref[...])
    rowi = lax.broadcasted_iota(jnp.int32, filt.shape, 0)
    kern = jnp.where(rowi == l, 0.0, filt * decay)
    o_ref[...] = (kern / jnp.sum(jnp.abs(kern), axis=0, keepdims=True)).astype(o_ref.dtype)


def _hyena_filters(l, w1, b1, freq, w2, b2, w3):
    n = 2 * l
    tn = 256
    ff = w2.shape[0]
    cpo = HY_DIM // tn
    deltas = np.abs(np.linspace(math.log(HY_DECAY_TARGET) / HY_DECAY_PCT_LONG,
                                math.log(HY_DECAY_TARGET) / HY_DECAY_PCT_SHORT, HY_DIM,
                                dtype=np.float32)).reshape(1, HY_DIM)
    const = lambda o, c: (0, 0)
    return pl.pallas_call(
        _filter_kernel,
        grid=(HY_ORDER, cpo),
        in_specs=[pl.BlockSpec((n, 128), const), pl.BlockSpec((128, ff), const),
                  pl.BlockSpec((1, ff), const), pl.BlockSpec((2, ff), const),
                  pl.BlockSpec((ff, ff), const), pl.BlockSpec((1, ff), const),
                  pl.BlockSpec((ff, tn), lambda o, c: (0, o * 2 * cpo + c)),
                  pl.BlockSpec((ff, tn), lambda o, c: (0, o * 2 * cpo + cpo + c)),
                  pl.BlockSpec((1, tn), lambda o, c: (0, c))],
        out_specs=pl.BlockSpec((n, tn), lambda o, c: (0, o * cpo + c)),
        out_shape=jax.ShapeDtypeStruct((n, HY_ORDER * HY_DIM), BF16),
        scratch_shapes=[pltpu.VMEM((n, ff), F32)],
        compiler_params=_cparams(("arbitrary", "arbitrary")),
        name="hyena_filter_mlp",
    )(jnp.asarray(_tap_features(l)), jnp.pad(w1, ((0, 128 - w1.shape[0]), (0, 0))),
      b1.reshape(1, ff), freq, w2, b2.reshape(1, ff), w3, w3, jnp.asarray(deltas))


def _hyena(p_arr, lp, dft, batch, grid):
    l = p_arr.shape[0] // batch
    fwd, inv = dft
    uc = _short_conv(p_arr, lp['hy_conv_w'].reshape(9, 3 * HY_DIM), lp['hy_conv_b'], batch, grid)
    kern = _hyena_filters(l, lp['hy_ff_w1'], lp['hy_ff_b1'], lp['hy_ff_freq'],
                          lp['hy_ff_w2'], lp['hy_ff_b2'], lp['hy_ff_w3'])
    nb = l // inv.shape[0]
    kf = _filter_spectrum(fwd, kern, nb)
    z_arr, z_col = uc, 0
    for n in range(HY_ORDER):
        y = _dft_fwd(fwd, z_arr, z_col, kf, n, batch, nb)
        z_arr = _dft_inv(inv, y, uc, n + 1, z_arr, z_col, lp['hy_skip'][n], batch, nb)
        z_col = 0
    return z_arr


def _merge_kernel(yhy_ref, ohg_ref, hgg_ref, yrt_ref, b0_ref, b1_ref, b2_ref, x_ref, mod_ref,
                  nw_ref, phy_ref, phg_ref, prt_ref, wo_ref, lng_ref, lnb_ref, o_ref):
    o = ohg_ref[...]
    yhg = (o * lax.rsqrt(jnp.mean(o * o, axis=-1, keepdims=True) + LN_EPS) * nw_ref[...]
           * _silu(hgg_ref[...].astype(F32)))
    m = (jax.nn.sigmoid(b0_ref[...].astype(F32)) * _dot(yhy_ref[...], phy_ref[...])
         + jax.nn.sigmoid(b1_ref[...].astype(F32)) * _dot(yhg.astype(BF16), phg_ref[...])
         + jax.nn.sigmoid(b2_ref[...].astype(F32)) * _dot(yrt_ref[...], prt_ref[...]))
    t = _dot(m.astype(BF16), wo_ref[...])
    gt1 = mod_ref[0][2:3]
    o_ref[...] = _layer_norm_rows(DN_ALPHA * x_ref[...] + gt1 * t, lng_ref[...], lnb_ref[...])


def _merge(y_hy, o_hg, y_rt, p_arr, x, modp, lp, wbf, rows_per_batch):
    m = x.shape[0]
    tm = min(512, m)
    d = D_MODEL
    mi = _mod_index(rows_per_batch, tm)
    row = lambda i: (i, 0)
    const = lambda i: (0, 0)
    gcol = COL_HG * 128 // d
    bcol = COL_BR * 128 // d
    return pl.pallas_call(
        _merge_kernel,
        grid=(m // tm,),
        in_specs=[pl.BlockSpec((tm, d), row), pl.BlockSpec((tm, d), row),
                  pl.BlockSpec((tm, d), lambda i: (i, gcol)),
                  pl.BlockSpec((tm, 2 * d), row),
                  pl.BlockSpec((tm, d), lambda i: (i, bcol)),
                  pl.BlockSpec((tm, d), lambda i: (i, bcol + 1)),
                  pl.BlockSpec((tm, d), lambda i: (i, bcol + 2)),
                  pl.BlockSpec((tm, d), row),
                  pl.BlockSpec((1, 8, d), lambda i: (mi(i), 0, 0)),
                  pl.BlockSpec((1, d), const),
                  pl.BlockSpec((d, d), const), pl.BlockSpec((d, d), const),
                  pl.BlockSpec((2 * d, d), const), pl.BlockSpec((d, d), const),
                  pl.BlockSpec((1, d), const), pl.BlockSpec((1, d), const)],
        out_specs=pl.BlockSpec((tm, d), row),
        out_shape=jax.ShapeDtypeStruct((m, d), F32),
        compiler_params=_cparams(("arbitrary",)),
        name="merge_out_ln",
    )(y_hy, o_hg, p_arr, y_rt, p_arr, p_arr, p_arr, x, modp, lp['hg_norm_w'].reshape(1, d),
      wbf['p_hy'], wbf['p_hg'], wbf['p_rt'], wbf['w_o'],
      lp['ln1_g'].reshape(1, d), lp['ln1_b'].reshape(1, d))


def _ffn_kernel(x_ref, mod_ref, w1_ref, w3_ref, w2_ref, lng_ref, lnb_ref, o_ref, h_ref, acc_ref):
    k = pl.program_id(1)
    m = mod_ref[0]

    @pl.when(k == 0)
    def _():
        h_ref[...] = (x_ref[...] * (1.0 + m[4:5]) + m[3:4]).astype(BF16)
        acc_ref[...] = jnp.zeros(acc_ref.shape, F32)

    h = h_ref[...]
    u = _silu(_dot(h, w1_ref[...])) * _dot(h, w3_ref[...])
    acc_ref[...] += _dot(u.astype(BF16), w2_ref[...])

    @pl.when(k == pl.num_programs(1) - 1)
    def _():
        o_ref[...] = _layer_norm_rows(DN_ALPHA * x_ref[...] + m[5:6] * acc_ref[...],
                                      lng_ref[...], lnb_ref[...])


def _ffn_dense(x, modp, w1, w3, w2, ln_g, ln_b, rows_per_batch):
    m = x.shape[0]
    d = D_MODEL
    dff = w1.shape[1]
    tm = min(1024, rows_per_batch or m)
    tf = dff // 2
    mi = _mod_index(rows_per_batch, tm)
    return pl.pallas_call(
        _ffn_kernel,
        grid=(m // tm, dff // tf),
        in_specs=[pl.BlockSpec((tm, d), lambda i, k: (i, 0)),
                  pl.BlockSpec((1, 8, d), lambda i, k: (mi(i), 0, 0)),
                  pl.BlockSpec((d, tf), lambda i, k: (0, k)),
                  pl.BlockSpec((d, tf), lambda i, k: (0, k)),
                  pl.BlockSpec((tf, d), lambda i, k: (k, 0)),
                  pl.BlockSpec((1, d), lambda i, k: (0, 0)),
                  pl.BlockSpec((1, d), lambda i, k: (0, 0))],
        out_specs=pl.BlockSpec((tm, d), lambda i, k: (i, 0)),
        out_shape=jax.ShapeDtypeStruct((m, d), F32),
        scratch_shapes=[pltpu.VMEM((tm, d), BF16), pltpu.VMEM((tm, d), F32)],
        compiler_params=_cparams(("arbitrary", "arbitrary")),
        name="ffn_dense_ln",
    )(x, modp, w1, w3, w2, ln_g.reshape(1, d), ln_b.reshape(1, d))


def _router_kernel(x_ref, mod_ref, r_ref, h_ref, lg_ref):
    m = mod_ref[0]
    h = x_ref[...] * (1.0 + m[4:5]) + m[3:4]
    h_ref[...] = h
    a1, a2, a3 = _split3(h)
    r1, r2, r3 = _split3(r_ref[...])
    lg_ref[...] = (_dot(a1, r1) + _dot(a1, r2) + _dot(a2, r1)
                   + _dot(a2, r2) + _dot(a1, r3) + _dot(a3, r1))


def _router(x, modp, router_pad, rows_per_batch):
    m = x.shape[0]
    d = D_MODEL
    tm = min(512, m)
    mi = _mod_index(rows_per_batch, tm)
    return pl.pallas_call(
        _router_kernel,
        grid=(m // tm,),
        in_specs=[pl.BlockSpec((tm, d), lambda i: (i, 0)),
                  pl.BlockSpec((1, 8, d), lambda i: (mi(i), 0, 0)),
                  pl.BlockSpec((d, 128), lambda i: (0, 0))],
        out_specs=[pl.BlockSpec((tm, d), lambda i: (i, 0)),
                   pl.BlockSpec((tm, 128), lambda i: (i, 0))],
        out_shape=[jax.ShapeDtypeStruct((m, d), F32), jax.ShapeDtypeStruct((m, 128), F32)],
        compiler_params=_cparams(("arbitrary",)),
        name="moe_router",
    )(x, modp, router_pad)


def _moe_ffn_kernel(be_ref, nu_ref, xp_ref, w1_ref, w3_ref, w2_ref, o_ref, x_ref, acc_ref):
    j = pl.program_id(0)
    k = pl.program_id(1)

    @pl.when(j < nu_ref[0])
    def _():
        @pl.when(k == 0)
        def _():
            x_ref[...] = xp_ref[...].astype(BF16)
            acc_ref[...] = jnp.zeros(acc_ref.shape, F32)

        x = x_ref[...]
        u = _silu(_dot(x, w1_ref[0])) * _dot(x, w3_ref[0])
        acc_ref[...] += _dot(u.astype(BF16), w2_ref[0])

        @pl.when(k == pl.num_programs(1) - 1)
        def _():
            o_ref[...] = acc_ref[...]

    @pl.when((j >= nu_ref[0]) & (k == pl.num_programs(1) - 1))
    def _():
        o_ref[...] = jnp.zeros(o_ref.shape, F32)


def _moe_ffn(xb, block_e, n_used, w1, w3, w2):
    ns = xb.shape[0]
    d = D_MODEL
    tm = MOE_ROWS
    dex = w1.shape[2]
    tf = dex // 2
    grid_spec = pltpu.PrefetchScalarGridSpec(
        num_scalar_prefetch=2,
        grid=(ns // tm, dex // tf),
        in_specs=[pl.BlockSpec((tm, d), lambda j, k, be, nu: (j, 0)),
                  pl.BlockSpec((1, d, tf), lambda j, k, be, nu: (be[j], 0, k)),
                  pl.BlockSpec((1, d, tf), lambda j, k, be, nu: (be[j], 0, k)),
                  pl.BlockSpec((1, tf, d), lambda j, k, be, nu: (be[j], k, 0))],
        out_specs=pl.BlockSpec((tm, d), lambda j, k, be, nu: (j, 0)),
        scratch_shapes=[pltpu.VMEM((tm, d), BF16), pltpu.VMEM((tm, d), F32)])
    return pl.pallas_call(
        _moe_ffn_kernel,
        grid_spec=grid_spec,
        out_shape=jax.ShapeDtypeStruct((ns, d), F32),
        compiler_params=_cparams(("arbitrary", "arbitrary")),
        name="moe_expert_ffn",
    )(block_e, n_used, xb, w1, w3, w2)


def _combine_kernel(x_ref, mod_ref, y0_ref, y1_ref, g_ref, lng_ref, lnb_ref, o_ref):
    m = mod_ref[0]
    g = g_ref[...]
    f = g[:, 0:1] * y0_ref[...] + g[:, 1:2] * y1_ref[...]
    o_ref[...] = _layer_norm_rows(DN_ALPHA * x_ref[...] + m[5:6] * f, lng_ref[...], lnb_ref[...])


def _moe_combine(x, modp, y0, y1, gate_pad, ln_g, ln_b, rows_per_batch):
    m = x.shape[0]
    d = D_MODEL
    tm = min(512, m)
    mi = _mod_index(rows_per_batch, tm)
    row = lambda i: (i, 0)
    return pl.pallas_call(
        _combine_kernel,
        grid=(m // tm,),
        in_specs=[pl.BlockSpec((tm, d), row),
                  pl.BlockSpec((1, 8, d), lambda i: (mi(i), 0, 0)),
                  pl.BlockSpec((tm, d), row), pl.BlockSpec((tm, d), row),
                  pl.BlockSpec((tm, 128), row),
                  pl.BlockSpec((1, d), lambda i: (0, 0)), pl.BlockSpec((1, d), lambda i: (0, 0))],
        out_specs=pl.BlockSpec((tm, d), row),
        out_shape=jax.ShapeDtypeStruct((m, d), F32),
        compiler_params=_cparams(("arbitrary",)),
        name="moe_combine_ln",
    )(x, modp, y0, y1, gate_pad, ln_g.reshape(1, d), ln_b.reshape(1, d))


def _cast_kernel(x_ref, o_ref):
    o_ref[...] = x_ref[...].astype(o_ref.dtype)


def _to_bf16(w, group):
    _, e, a, b = w.shape
    ta, tb = (a // 4, b) if a >= b else (a, b // 4)
    cut_rows = a >= b
    return pl.pallas_call(
        _cast_kernel,
        grid=(e, 4),
        in_specs=[pl.BlockSpec((None, 1, ta, tb),
                               lambda i, r: (group, i, r, 0) if cut_rows else (group, i, 0, r))],
        out_specs=pl.BlockSpec((1, ta, tb), lambda i, r: (i, r, 0) if cut_rows else (i, 0, r)),
        out_shape=jax.ShapeDtypeStruct((e, a, b), BF16),
        compiler_params=_cparams(("arbitrary", "arbitrary")),
        name="weight_cast",
    )(w)


def _moe(x, modp, router, w1, w3, w2, ln_g, ln_b, rows_per_batch):
    n, d = x.shape
    e = router.shape[1]
    h2, logits = _router(x, modp, jnp.pad(router, ((0, 0), (0, 128 - e))), rows_per_batch)
    top_val, top_idx = lax.top_k(logits[:, :e], TOP_K)
    gate = jax.nn.softmax(top_val, axis=-1)
    flat_e = top_idx.reshape(-1)
    flat_t = jnp.repeat(jnp.arange(n, dtype=jnp.int32), TOP_K)
    order = jnp.argsort(flat_e, stable=True).astype(jnp.int32)
    rank = jnp.argsort(order).astype(jnp.int32)
    counts = jnp.sum((flat_e[:, None] == jnp.arange(e)[None, :]).astype(jnp.int32), axis=0)
    starts = jnp.cumsum(counts) - counts
    padded = (counts + MOE_ROWS - 1) // MOE_ROWS * MOE_ROWS
    pad_end = jnp.cumsum(padded)
    pad_start = pad_end - padded
    n_blocks = -(-(n * TOP_K) // MOE_ROWS) + e
    n_slots = n_blocks * MOE_ROWS
    block_start = jnp.arange(n_blocks) * MOE_ROWS
    block_e = jnp.minimum(jnp.sum(block_start[:, None] >= pad_end[None, :], axis=1), e - 1).astype(jnp.int32)
    n_used = (pad_end[-1] // MOE_ROWS).astype(jnp.int32).reshape(1)
    slot_e = jnp.repeat(block_e, MOE_ROWS)
    slot_off = jnp.arange(n_slots, dtype=jnp.int32) - pad_start[slot_e]
    slot_valid = slot_off < counts[slot_e]
    slot_src = jnp.where(slot_valid, starts[slot_e] + slot_off, 0)
    slot_tok = jnp.where(slot_valid, flat_t[order[slot_src]], 0).astype(jnp.int32)
    pos = (pad_start[flat_e] + rank - starts[flat_e]).astype(jnp.int32).reshape(n, TOP_K)
    xb = jnp.take(h2, slot_tok, axis=0, mode="clip")
    yb = _moe_ffn(xb, block_e, n_used, w1, w3, w2)
    y0 = jnp.take(yb, pos[:, 0], axis=0, mode="clip")
    y1 = jnp.take(yb, pos[:, 1], axis=0, mode="clip")
    gate_pad = jnp.pad(gate, ((0, 0), (0, 128 - TOP_K)))
    return _moe_combine(x, modp, y0, y1, gate_pad, ln_g, ln_b, rows_per_batch)


def kernel(x, c, ctx, c_ctx, ada_w, ada_b, w_in, hy_conv_w, hy_conv_b, hy_ff_w1, hy_ff_b1, hy_ff_freq, hy_ff_w2, hy_ff_b2, hy_ff_w3, hy_skip, hg_lb_logits, hg_norm_w, p_hy, p_hg, p_rt, w_o, ln1_g, ln1_b, ln2_g, ln2_b, ffn_w1, ffn_w3, ffn_w2, moe_router, moe_w1, moe_w3, moe_w2):
    batch, l, d = x.shape
    lc = ctx.shape[1]
    assert d == D_MODEL and batch <= 8
    assert l % 512 == 0 and l % GRID_W == 0 and lc % max(RET_CHUNK, DFT_HALF) == 0

    cs = jnp.cumsum(jax.nn.softmax(hg_lb_logits.astype(F32), axis=1), axis=1)
    lower_bounds = cs - cs[:, :1]
    cc = jnp.zeros((16, d), F32).at[:batch].set(c).at[8].set(c_ctx)
    ret_tables = _retention_tables(lc, l)
    dft_lat = _dft_matrices(min(HY_BLOCK, l))
    dft_ctx = _dft_matrices(min(HY_BLOCK, lc))

    x_lat = x.reshape(batch * l, d)
    x_ctx = ctx.reshape(batch * lc, d)
    for i in range(DEPTH):
        need_ctx_out = i < DEPTH - 1
        use_moe = i % 2 == 1
        g = i // 2
        lp = {'hy_conv_w': hy_conv_w[i], 'hy_conv_b': hy_conv_b[i], 'hy_ff_w1': hy_ff_w1[i],
              'hy_ff_b1': hy_ff_b1[i], 'hy_ff_freq': hy_ff_freq[i], 'hy_ff_w2': hy_ff_w2[i],
              'hy_ff_b2': hy_ff_b2[i], 'hy_ff_w3': hy_ff_w3[i], 'hy_skip': hy_skip[i],
              'hg_norm_w': hg_norm_w[i], 'ln1_g': ln1_g[i], 'ln1_b': ln1_b[i]}
        wbf = {'p_hy': p_hy[i].astype(BF16), 'p_hg': p_hg[i].astype(BF16),
               'p_rt': p_rt[i].astype(BF16), 'w_o': w_o[i].astype(BF16)}
        mod = _ada(cc, ada_w, i, ada_b[i])
        modp = jnp.pad(mod.reshape(16, 6, d), ((0, 0), (0, 2), (0, 0)))
        lb = lower_bounds[:, i]
        gl_tab = jnp.pad(jnp.stack([lb, 1.0 - lb], axis=1), ((0, 0), (0, 6), (0, 0)))

        p_lat = _proj(x_lat, modp, w_in, i, N_IN_COLS, l)
        p_ctx = _proj(x_ctx, modp, w_in, i, N_IN_COLS if need_ctx_out else N_STATE_COLS, None)

        o_hg_l, o_hg_c = _gla(p_lat, p_ctx, gl_tab, batch, need_ctx_out)
        y_rt_l, y_rt_c = _retention(p_lat, p_ctx, ret_tables, batch, need_ctx_out)
        y_hy_l = _hyena(p_lat, lp, dft_lat, batch, True)
        x_lat_new = _merge(y_hy_l, o_hg_l, y_rt_l, p_lat, x_lat, modp, lp, wbf, l)
        if need_ctx_out:
            y_hy_c = _hyena(p_ctx, lp, dft_ctx, batch, False)
            x_ctx = _merge(y_hy_c, o_hg_c, y_rt_c, p_ctx, x_ctx, modp, lp, wbf, None)
        x_lat = x_lat_new

        if use_moe:
            if need_ctx_out:
                raise NotImplementedError("MoE layer with a context output is not part of this trunk")
            w1, w3, w2 = _to_bf16(moe_w1, g), _to_bf16(moe_w3, g), _to_bf16(moe_w2, g)
            x_lat = _moe(x_lat, modp, moe_router[g], w1, w3, w2, ln2_g[i], ln2_b[i], l)
        else:
            w1, w3, w2 = ffn_w1[g].astype(BF16), ffn_w3[g].astype(BF16), ffn_w2[g].astype(BF16)
            x_lat = _ffn_dense(x_lat, modp, w1, w3, w2, ln2_g[i], ln2_b[i], l)
            if need_ctx_out:
                x_ctx = _ffn_dense(x_ctx, modp, w1, w3, w2, ln2_g[i], ln2_b[i], None)
    return x_lat.reshape(batch, l, d)
```

```python
import functools
import math

import numpy as np
import jax
import jax.numpy as jnp
from jax import lax
from jax.experimental import pallas as pl
from jax.experimental.pallas import tpu as pltpu

F32 = jnp.float32
BF16 = jnp.bfloat16

D_MODEL = 1024
DEPTH = 2
GRID_W = 64
HY_DIM = 1024
HY_ORDER = 2
HY_BANDS = 16
HY_DECAY_TARGET = 1e-2
HY_DECAY_PCT_SHORT = 0.3
HY_DECAY_PCT_LONG = 1.5
HG_HEADS = 8
HG_HK = 128
RT_HEADS = 4
RT_HK = 256
RT_HV = 512
RT_ROPE_BASE = 10000.0
N_STATE_COLS = 6144
N_IN_COLS = 17408
N_EXPERTS = 8
TOP_K = 2
DN_ALPHA = (2 * DEPTH) ** 0.25
LN_EPS = 1e-5

COL_FF, COL_FB, COL_HI, COL_RK, COL_RV = 0, 8, 16, 24, 32
COL_HQ, COL_HG, COL_RQ, COL_RG, COL_HY, COL_BR = 48, 56, 64, 72, 88, 112

GLA_CHUNK = 256
GLA_LEVELS = 8
GLA_PAD = 8
GLA_MIN_LOG2 = -150.0
RET_CHUNK = 256
DFT_HALF = 256
HY_BLOCK = 512
MIX_ROWS = 16
CONV_ROWS = 256
MOE_ROWS = 512
VMEM_LIMIT = 56 * 1024 * 1024


def _cparams(sem, flags=None):
    return pltpu.CompilerParams(dimension_semantics=sem, vmem_limit_bytes=VMEM_LIMIT, flags=flags)


def _silu(x):
    return x * jax.nn.sigmoid(x)


def _layer_norm_rows(r, g, b):
    mu = jnp.mean(r, axis=-1, keepdims=True)
    d = r - mu
    var = jnp.mean(d * d, axis=-1, keepdims=True)
    return d * lax.rsqrt(var + LN_EPS) * g + b


def _split3(x):
    h = x.astype(BF16)
    r = x - h.astype(F32)
    m = r.astype(BF16)
    l = (r - m.astype(F32)).astype(BF16)
    return h, m, l


def _dot(a, b):
    return jnp.dot(a, b, preferred_element_type=F32)


def _dot_nt(a, b):
    return lax.dot_general(a, b, (((1,), (1,)), ((), ())), preferred_element_type=F32)


def _dot_tn(a, b):
    return lax.dot_general(a, b, (((0,), (0,)), ((), ())), preferred_element_type=F32)


def _ada_kernel(c_ref, w_ref, b_ref, o_ref):
    s = _silu(c_ref[...])
    o_ref[...] = _dot(s.astype(BF16), w_ref[...].astype(BF16)) + b_ref[...]


def _ada(cc, w, layer, b):
    n = w.shape[2]
    tn = 1024
    return pl.pallas_call(
        _ada_kernel,
        grid=(n // tn,),
        in_specs=[pl.BlockSpec((16, D_MODEL), lambda j: (0, 0)),
                  pl.BlockSpec((None, D_MODEL, tn), lambda j: (layer, 0, j)),
                  pl.BlockSpec((1, tn), lambda j: (0, j))],
        out_specs=pl.BlockSpec((16, tn), lambda j: (0, j)),
        out_shape=jax.ShapeDtypeStruct((16, n), F32),
        compiler_params=_cparams(("arbitrary",)),
        name="ada_mod",
    )(cc, w, b.reshape(1, n))


def _mod_index(rows_per_batch, tm):
    if rows_per_batch is None:
        return lambda i: 8
    return lambda i: (i * tm) // rows_per_batch


def _proj_kernel(x_ref, mod_ref, w_ref, o_ref, h_ref):
    @pl.when(pl.program_id(1) == 0)
    def _():
        m = mod_ref[0]
        h_ref[...] = (x_ref[...] * (1.0 + m[1:2]) + m[0:1]).astype(BF16)

    o_ref[...] = _dot(h_ref[...], w_ref[...].astype(BF16)).astype(o_ref.dtype)


def _proj(x, modp, w, layer, n_cols, rows_per_batch):
    m = x.shape[0]
    tm = min(2048, rows_per_batch or m)
    tn = 1024
    mi = _mod_index(rows_per_batch, tm)
    return pl.pallas_call(
        _proj_kernel,
        grid=(m // tm, n_cols // tn),
        in_specs=[pl.BlockSpec((tm, D_MODEL), lambda i, j: (i, 0)),
                  pl.BlockSpec((1, 8, D_MODEL), lambda i, j: (mi(i), 0, 0)),
                  pl.BlockSpec((None, D_MODEL, tn), lambda i, j: (layer, 0, j))],
        out_specs=pl.BlockSpec((tm, tn), lambda i, j: (i, j)),
        out_shape=jax.ShapeDtypeStruct((m, n_cols), BF16),
        scratch_shapes=[pltpu.VMEM((tm, D_MODEL), BF16)],
        compiler_params=_cparams(("arbitrary", "arbitrary")),
        name="in_proj",
    )(x, modp, w)


def _gla_gates(fl, gl_ref, d):
    lb = gl_ref[d, 0:1, :]
    oml = gl_ref[d, 1:2, :]
    t = jnp.exp(-jnp.abs(fl))
    r = 1.0 / (1.0 + t)
    tr = t * r
    pos = fl >= 0.0
    f = lb + oml * jnp.where(pos, r, tr)
    key = oml * jnp.where(pos, tr, r)
    return jnp.maximum(jnp.log2(f), GLA_MIN_LOG2), key


def _gla_cumsum(g, rev):
    c = g.shape[0]
    r = lax.broadcasted_iota(jnp.int32, (c, c), 0)
    u = lax.broadcasted_iota(jnp.int32, (c, c), 1)
    tri = jnp.where((u >= r) if rev else (u <= r), 1.0, 0.0).astype(BF16)
    return _dot(jnp.concatenate([tri, tri, tri], axis=1), jnp.concatenate(_split3(g), axis=0))


def _gla_state_update(st, b, kk, vv, rev):
    c = b.shape[0]
    end = 0 if rev else c - 1
    b_end = b[end:end + 1, :]
    khat = (kk * jnp.exp2(b_end - b)).astype(BF16)
    return st * jnp.exp2(b_end) + _dot_tn(vv, khat)


def _gla_level_table():
    c = GLA_CHUNK
    t = np.arange(c)[:, None]
    s = np.arange(c)[None, :]
    top_bit = np.floor(np.log2(np.maximum(t ^ s, 1))).astype(np.int32)
    fwd = np.where(t > s, top_bit, np.where(t == s, GLA_LEVELS, -1))
    return np.stack([fwd, fwd.T]).astype(np.int32)


def _gla_ref_rows(b, bpad_ref, level, rev):
    c = b.shape[0]
    half = 1 << level
    blk = 2 * half
    idx = half if rev else half - 1
    if blk % 8 == 0:
        r = b.reshape(c // blk, blk, HG_HK)[:, idx:idx + 1, :]
        return jnp.broadcast_to(r, (c // blk, blk, HG_HK)).reshape(c, HG_HK)
    m = lax.broadcasted_iota(jnp.int32, (c, HG_HK), 0) % blk
    out = b
    for v in range(blk):
        if v != idx:
            off = GLA_PAD + idx - v
            out = jnp.where(m == v, bpad_ref[off:off + c, :], out)
    return out


def _gla_chunk_out(b, kk, vv, qq, st, rev, lvt, bpad_ref):
    c = GLA_CHUNK
    o = _dot_nt((qq * jnp.exp2(b)).astype(BF16), st.astype(BF16))
    bpad_ref[GLA_PAD:GLA_PAD + c, :] = b
    qq_b, kk_b = qq.astype(BF16), kk.astype(BF16)
    sc = jnp.zeros((c, c), F32)
    for level in range(GLA_LEVELS):
        half = 1 << level
        blk = 2 * half
        ref = _gla_ref_rows(b, bpad_ref, level, rev)
        kt = kk_b * jnp.exp2(ref - b).astype(BF16)
        if half % 8:
            p = _dot_nt(qq_b * jnp.exp2(b - ref).astype(BF16), kt)
            sc = jnp.where(lvt == level, p, sc)
        else:
            nblk = c // blk
            lo, hi = (0, half) if rev else (half, blk)

            def part(x):
                return x.reshape(nblk, blk, x.shape[-1])[:, lo:hi, :]

            qt = (part(qq) * jnp.exp2(part(b) - part(ref))).reshape(c // 2, HG_HK)
            p = _dot_nt(qt.astype(BF16), kt).reshape(nblk, half, c)
            new = jnp.where(part(lvt) == level, p, part(sc))
            rest = sc.reshape(nblk, blk, c)[:, half:blk, :] if rev else sc.reshape(nblk, blk, c)[:, 0:half, :]
            sc = jnp.concatenate([new, rest] if rev else [rest, new], axis=1).reshape(c, c)
    sc = jnp.where(lvt == GLA_LEVELS, _dot_nt(qq_b, kk_b), sc)
    return o + _dot(sc.astype(BF16), vv)


def _gla_kernel(*refs, need_ctx_out, n_lat, n_ctx):
    if need_ctx_out:
        ffl, fbl, il, ql, ffc, fbc, ic, qc, gl_ref, lv_ref, ol_ref, oc_ref = refs[:12]
        bl, kl, bc, kc, qsl, qsc, bpf, bpr = refs[12:]
    else:
        ffl, fbl, il, ql, ffc, fbc, ic, gl_ref, lv_ref, ol_ref = refs[:10]
        bl, kl, bc, kc, qsl, bpf, bpr = refs[10:]
        qc = oc_ref = qsc = None
    c = GLA_CHUNK
    q_scale = HG_HK ** -0.5
    bpf[...] = jnp.zeros(bpf.shape, F32)
    bpr[...] = jnp.zeros(bpr.shape, F32)
    ol_ref[...] = jnp.zeros(ol_ref.shape, F32)
    if need_ctx_out:
        oc_ref[...] = jnp.zeros(oc_ref.shape, F32)

    def chunk_rows(idx):
        return pl.ds(pl.multiple_of(idx * c, c), c)

    def prepare(f_refs, q_ref, b_sc, k_sc, q_sc, n):
        def body(j, carry):
            rows = chunk_rows(j)
            for d in range(2):
                g, kk = _gla_gates(f_refs[d][rows, :].astype(F32), gl_ref, d)
                b_sc[d, rows, :] = _gla_cumsum(g, d == 1)
                k_sc[d, rows, :] = kk
            if q_ref is not None:
                q_sc[rows, :] = _silu(q_ref[rows, :].astype(F32)) * q_scale
            return carry
        lax.fori_loop(0, n, body, 0, unroll=min(4, n))

    prepare((ffc, fbc), qc, bc, kc, qsc, n_ctx)
    prepare((ffl, fbl), ql, bl, kl, qsl, n_lat)

    def one_dir(d, b_sc, k_sc, i_ref, q_sc, o_ref, idx, st):
        rev = d == 1
        rows = chunk_rows(idx)
        b, kk, vv = b_sc[d, rows, :], k_sc[d, rows, :], i_ref[rows, :]
        if q_sc is not None:
            o_ref[rows, :] += _gla_chunk_out(b, kk, vv, q_sc[rows, :], st, rev, lv_ref[d],
                                             bpr if rev else bpf)
        return _gla_state_update(st, b, kk, vv, rev)

    def ctx_body(j, carry):
        return (one_dir(0, bc, kc, ic, qsc, oc_ref, j, carry[0]),
                one_dir(1, bc, kc, ic, qsc, oc_ref, n_ctx - 1 - j, carry[1]))

    def lat_body(j, carry):
        return (one_dir(0, bl, kl, il, qsl, ol_ref, j, carry[0]),
                one_dir(1, bl, kl, il, qsl, ol_ref, n_lat - 1 - j, carry[1]))

    zero = jnp.zeros((HG_HK, HG_HK), F32)
    carry = lax.fori_loop(0, n_ctx, ctx_body, (zero, zero), unroll=min(2, n_ctx))
    lax.fori_loop(0, n_lat, lat_body, carry, unroll=min(4, n_lat))


def _gla(p_lat, p_ctx, gl_tab, batch, need_ctx_out):
    l = p_lat.shape[0] // batch
    lc = p_ctx.shape[0] // batch
    w = HG_HK
    c = GLA_CHUNK

    def col(base):
        return lambda b, h: (b, base + h)

    lat_specs = [pl.BlockSpec((l, w), col(cb)) for cb in (COL_FF, COL_FB, COL_HI, COL_HQ)]
    ctx_cols = (COL_FF, COL_FB, COL_HI) + ((COL_HQ,) if need_ctx_out else ())
    ctx_specs = [pl.BlockSpec((lc, w), col(cb)) for cb in ctx_cols]
    in_specs = lat_specs + ctx_specs + [pl.BlockSpec((2, 8, w), lambda b, h: (0, 0, h)),
                                        pl.BlockSpec((2, c, c), lambda b, h: (0, 0, 0))]
    args = [p_lat] * 4 + [p_ctx] * len(ctx_cols) + [gl_tab, jnp.asarray(_gla_level_table())]
    out_specs = [pl.BlockSpec((l, w), lambda b, h: (b, h))]
    out_shape = [jax.ShapeDtypeStruct((batch * l, HG_HEADS * w), F32)]
    scratch = [pltpu.VMEM((2, l, w), F32), pltpu.VMEM((2, l, w), F32),
               pltpu.VMEM((2, lc, w), F32), pltpu.VMEM((2, lc, w), F32), pltpu.VMEM((l, w), F32)]
    if need_ctx_out:
        out_specs.append(pl.BlockSpec((lc, w), lambda b, h: (b, h)))
        out_shape.append(jax.ShapeDtypeStruct((batch * lc, HG_HEADS * w), F32))
        scratch.append(pltpu.VMEM((lc, w), F32))
    scratch += [pltpu.VMEM((c + 2 * GLA_PAD, w), F32)] * 2
    outs = pl.pallas_call(
        functools.partial(_gla_kernel, need_ctx_out=need_ctx_out, n_lat=l // c, n_ctx=lc // c),
        grid=(batch, HG_HEADS),
        in_specs=in_specs,
        out_specs=out_specs,
        out_shape=out_shape,
        scratch_shapes=scratch,
        compiler_params=_cparams(("arbitrary", "arbitrary")),
        name="hgrn2_scan",
    )(*args)
    return outs[0], (outs[1] if need_ctx_out else None)


def _rotate(x, cos, sin):
    half = x.shape[-1] // 2
    x1, x2 = x[:, :half], x[:, half:]
    return jnp.concatenate([x1 * cos - x2 * sin, x1 * sin + x2 * cos], axis=-1)


def _ret_kernel(*refs, need_ctx_out, n_lat, n_ctx):
    if need_ctx_out:
        (ql, kl, vl, gl, qc, kc, vc, gc, cos_ref, sin_ref, intra_ref, qk_ref, sd_ref,
         yl_ref, yc_ref, qrl, krl, qrc, krc, ol, oc, s_fwd, s_rev) = refs
    else:
        (ql, kl, vl, gl, kc, vc, cos_ref, sin_ref, intra_ref, qk_ref, sd_ref,
         yl_ref, qrl, krl, krc, ol, s_fwd, s_rev) = refs
        qc = gc = yc_ref = qrc = oc = None
    s_refs = (s_fwd, s_rev)
    c = RET_CHUNK
    lc = n_ctx * c
    k_scale = RT_HK ** -0.5

    def rot_all(src, dst, n, pos0, scale):
        def body(j, carry):
            rows = pl.ds(pl.multiple_of(j * c, c), c)
            prow = pl.ds(pl.multiple_of(pos0 + j * c, c), c)
            x = src[rows, :].astype(F32)
            dst[rows, :] = (_rotate(x, cos_ref[prow, :], sin_ref[prow, :]) * scale).astype(BF16)
            return carry
        lax.fori_loop(0, n, body, 0)

    rot_all(kl, krl, n_lat, lc, k_scale)
    rot_all(kc, krc, n_ctx, 0, k_scale)
    rot_all(ql, qrl, n_lat, lc, 1.0)
    if need_ctx_out:
        rot_all(qc, qrc, n_ctx, 0, 1.0)

    for s_ref in s_refs:
        s_ref[...] = jnp.zeros(s_ref.shape, F32)
    ol[...] = jnp.zeros(ol.shape, F32)
    if need_ctx_out:
        oc[...] = jnp.zeros(oc.shape, F32)

    def update(d, kr, v_ref, rows):
        kd = qk_ref[d, 0, c:2 * c, :]
        sdec = sd_ref[d, 0, 0:1, :]
        kh = (kr[rows, :].astype(F32) * jnp.concatenate([kd, kd], axis=1)).astype(BF16)
        s_ref = s_refs[d]
        s_ref[...] = (s_ref[...] * jnp.concatenate([sdec] * (RT_HV // 128), axis=1)
                      + _dot_tn(kh, v_ref[rows, :]))

    def out_step(d, qr, kr, v_ref, o_ref, rows):
        qd = qk_ref[d, 0, 0:c, :]
        qcb = qr[rows, :]
        sc = _dot_nt(qcb, kr[rows, :]) * intra_ref[d, 0]
        qh = (qcb.astype(F32) * jnp.concatenate([qd, qd], axis=1)).astype(BF16)
        o_ref[rows, :] += (_dot(qh, s_refs[d][...].astype(BF16))
                           + _dot(sc.astype(BF16), v_ref[rows, :]))
        update(d, kr, v_ref, rows)

    def ctx_body(j, carry):
        for d in range(2):
            idx = (n_ctx - 1 - j) if d == 1 else j
            rows = pl.ds(pl.multiple_of(idx * c, c), c)
            if need_ctx_out:
                out_step(d, qrc, krc, vc, oc, rows)
            else:
                update(d, krc, vc, rows)
        return carry

    def lat_body(j, carry):
        for d in range(2):
            idx = (n_lat - 1 - j) if d == 1 else j
            out_step(d, qrl, krl, vl, ol, pl.ds(pl.multiple_of(idx * c, c), c))
        return carry

    lax.fori_loop(0, n_ctx, ctx_body, 0)
    lax.fori_loop(0, n_lat, lat_body, 0, unroll=4)

    def readout(o_ref, g_ref, y_ref, n):
        def body(j, carry):
            rows = pl.ds(pl.multiple_of(j * c, c), c)
            o = o_ref[rows, :]
            y = o * lax.rsqrt(jnp.mean(o * o, axis=-1, keepdims=True) + LN_EPS)
            y_ref[rows, :] = (y * _silu(g_ref[rows, :].astype(F32))).astype(y_ref.dtype)
            return carry
        lax.fori_loop(0, n, body, 0)

    readout(ol, gl, yl_ref, n_lat)
    if need_ctx_out:
        readout(oc, gc, yc_ref, n_ctx)


def _retention_tables(lc, l):
    half = RT_HK // 2
    inv = 1.0 / (RT_ROPE_BASE ** jnp.linspace(0.0, 1.0, half, dtype=F32))
    ang = jnp.arange(lc + l, dtype=F32)[:, None] * inv[None, :]
    j = jnp.arange(2 * RT_HEADS, dtype=F32)
    lg_all = jnp.log1p(-jnp.exp2(-5.0 - j))
    c = RET_CHUNK
    pos = jnp.arange(c, dtype=F32)
    rel = pos[:, None] - pos[None, :]
    intra, qk, sd = [], [], []
    for d in range(2):
        lg = lg_all[d::2]
        m = jnp.where(rel >= 0, jnp.exp(jnp.maximum(rel, 0.0)[None] * lg[:, None, None]), 0.0)
        qdec = jnp.exp((pos + 1.0)[None, :] * lg[:, None])
        kdec = jnp.exp((c - 1.0 - pos)[None, :] * lg[:, None])
        if d == 1:
            m = jnp.swapaxes(m, 1, 2)
            qdec = qdec[:, ::-1]
            kdec = kdec[:, ::-1]
        intra.append(m)
        qk.append(jnp.broadcast_to(jnp.concatenate([qdec, kdec], axis=1)[:, :, None],
                                   (RT_HEADS, 2 * c, 128)))
        sd.append(jnp.broadcast_to(jnp.exp(c * lg)[:, None, None], (RT_HEADS, 8, 128)))
    return (jnp.cos(ang), jnp.sin(ang), jnp.stack(intra), jnp.stack(qk), jnp.stack(sd))


def _retention(p_lat, p_ctx, tables, batch, need_ctx_out):
    l = p_lat.shape[0] // batch
    lc = p_ctx.shape[0] // batch
    cos, sin, intra, qk, sd = tables
    c = RET_CHUNK

    def col(base):
        return lambda b, h: (b, base + h)

    kq, kk_, kv, kg = COL_RQ * 128 // RT_HK, COL_RK * 128 // RT_HK, COL_RV * 128 // RT_HV, COL_RG * 128 // RT_HV
    lat_specs = [pl.BlockSpec((l, RT_HK), col(kq)), pl.BlockSpec((l, RT_HK), col(kk_)),
                 pl.BlockSpec((l, RT_HV), col(kv)), pl.BlockSpec((l, RT_HV), col(kg))]
    if need_ctx_out:
        ctx_specs = [pl.BlockSpec((lc, RT_HK), col(kq)), pl.BlockSpec((lc, RT_HK), col(kk_)),
                     pl.BlockSpec((lc, RT_HV), col(kv)), pl.BlockSpec((lc, RT_HV), col(kg))]
    else:
        ctx_specs = [pl.BlockSpec((lc, RT_HK), col(kk_)), pl.BlockSpec((lc, RT_HV), col(kv))]
    tab_specs = [pl.BlockSpec((lc + l, 128), lambda b, h: (0, 0)),
                 pl.BlockSpec((lc + l, 128), lambda b, h: (0, 0)),
                 pl.BlockSpec((2, 1, c, c), lambda b, h: (0, h, 0, 0)),
                 pl.BlockSpec((2, 1, 2 * c, 128), lambda b, h: (0, h, 0, 0)),
                 pl.BlockSpec((2, 1, 8, 128), lambda b, h: (0, h, 0, 0))]
    args = [p_lat] * 4 + [p_ctx] * len(ctx_specs) + [cos, sin, intra, qk, sd]
    out_specs = [pl.BlockSpec((l, RT_HV), lambda b, h: (b, h))]
    out_shape = [jax.ShapeDtypeStruct((batch * l, RT_HEADS * RT_HV), BF16)]
    scratch = [pltpu.VMEM((l, RT_HK), BF16), pltpu.VMEM((l, RT_HK), BF16)]
    if need_ctx_out:
        out_specs.append(pl.BlockSpec((lc, RT_HV), lambda b, h: (b, h)))
        out_shape.append(jax.ShapeDtypeStruct((batch * lc, RT_HEADS * RT_HV), BF16))
        scratch += [pltpu.VMEM((lc, RT_HK), BF16), pltpu.VMEM((lc, RT_HK), BF16),
                    pltpu.VMEM((l, RT_HV), F32), pltpu.VMEM((lc, RT_HV), F32)]
    else:
        scratch += [pltpu.VMEM((lc, RT_HK), BF16), pltpu.VMEM((l, RT_HV), F32)]
    scratch += [pltpu.VMEM((RT_HK, RT_HV), F32)] * 2
    outs = pl.pallas_call(
        functools.partial(_ret_kernel, need_ctx_out=need_ctx_out, n_lat=l // c, n_ctx=lc // c),
        grid=(batch, RT_HEADS),
        in_specs=lat_specs + ctx_specs + tab_specs,
        out_specs=out_specs,
        out_shape=out_shape,
        scratch_shapes=scratch,
        compiler_params=_cparams(("arbitrary", "arbitrary")),
        name="retention_scan",
    )(*args)
    return outs[0], (outs[1] if need_ctx_out else None)


def _conv_shift_matrices(grid):
    rb = CONV_ROWS
    t = np.arange(rb)
    edge_lo = (t % GRID_W == 0) if grid else (t == 0)
    edge_hi = (t % GRID_W == GRID_W - 1) if grid else (t == rb - 1)
    s0 = ((t[:, None] - 1 == t[None, :]) & ~edge_lo[:, None]).astype(np.float32)
    s2 = ((t[:, None] + 1 == t[None, :]) & ~edge_hi[:, None]).astype(np.float32)
    return s0, s2


def _short_conv_kernel(x_ref, w_ref, b_ref, s0_ref, s2_ref, o_ref, xpad_ref, *, grid):
    l, tc = x_ref.shape
    p = GRID_W
    rb = CONV_ROWS
    zeros = jnp.zeros((p, tc), BF16)
    xpad_ref[0:p, :] = zeros
    xpad_ref[p + l:p + l + p, :] = zeros
    xpad_ref[p:p + l, :] = x_ref[...]
    w = [w_ref[k:k + 1, :].astype(BF16) for k in range(9)]
    rows = range(3) if grid else (1,)

    def body(i, carry):
        r0 = i * rb
        part = []
        for dj in range(3):
            acc = None
            for di in rows:
                start = pl.multiple_of(r0 + p + (di - 1) * GRID_W, GRID_W)
                t = xpad_ref[pl.ds(start, rb), :] * w[di * 3 + dj]
                acc = t if acc is None else acc + t
            part.append(acc)
        out = (part[1].astype(F32) + _dot(s0_ref[...], part[0]) + _dot(s2_ref[...], part[2])
               + b_ref[...])
        o_ref[pl.ds(pl.multiple_of(r0, rb), rb), :] = out.astype(o_ref.dtype)
        return carry

    lax.fori_loop(0, l // rb, body, 0, unroll=min(4, l // rb))


def _short_conv(p_arr, w9, bias, batch, grid):
    l = p_arr.shape[0] // batch
    assert l % CONV_ROWS == 0 and (grid or l == CONV_ROWS)
    c3 = 3 * HY_DIM
    tc = 256
    base = COL_HY * 128 // tc
    s0, s2 = _conv_shift_matrices(grid)
    return pl.pallas_call(
        functools.partial(_short_conv_kernel, grid=grid),
        grid=(batch, c3 // tc),
        in_specs=[pl.BlockSpec((l, tc), lambda b, j: (b, base + j)),
                  pl.BlockSpec((9, tc), lambda b, j: (0, j)),
                  pl.BlockSpec((1, tc), lambda b, j: (0, j)),
                  pl.BlockSpec((CONV_ROWS, CONV_ROWS), lambda b, j: (0, 0)),
                  pl.BlockSpec((CONV_ROWS, CONV_ROWS), lambda b, j: (0, 0))],
        out_specs=pl.BlockSpec((l, tc), lambda b, j: (b, j)),
        out_shape=jax.ShapeDtypeStruct((batch * l, c3), BF16),
        scratch_shapes=[pltpu.VMEM((l + 2 * GRID_W, tc), BF16)],
        compiler_params=_cparams(("arbitrary", "arbitrary")),
        name="hyena_short_conv",
    )(p_arr, w9, bias.reshape(1, c3), jnp.asarray(s0, BF16), jnp.asarray(s2, BF16))


def _dft_matrices(l):
    n = 2 * l
    s = int(round(math.sqrt(n)))
    while n % s:
        s -= 1
    q = n // s
    f = np.arange(l, dtype=np.int64)
    ang_a = 2.0 * np.pi * ((f[:, None] * s * np.arange(q)[None, :]) % n) / n
    ang_b = 2.0 * np.pi * ((f[:, None] * np.arange(s)[None, :]) % n) / n
    t = np.arange(n)
    rep = (t[None, :] // s == np.arange(q)[:, None]).astype(np.float32)
    til = (t[None, :] % s == np.arange(s)[:, None]).astype(np.float32)
    h = DFT_HALF
    row = lambda j: (j, 0)
    const = lambda j: (0, 0)
    fwd, inv = pl.pallas_call(
        _dft_build_kernel,
        grid=(l // h,),
        in_specs=[pl.BlockSpec((h, q), row), pl.BlockSpec((h, q), row),
                  pl.BlockSpec((h, s), row), pl.BlockSpec((h, s), row),
                  pl.BlockSpec((q, n), const), pl.BlockSpec((s, n), const)],
        out_specs=[pl.BlockSpec((1, 2, h, n), lambda j: (j, 0, 0, 0)),
                   pl.BlockSpec((h, n), row)],
        out_shape=[jax.ShapeDtypeStruct((l // h, 2, h, n), BF16),
                   jax.ShapeDtypeStruct((l, n), BF16)],
        compiler_params=_cparams(("arbitrary",)),
        name="dft_build",
    )(jnp.asarray(np.cos(ang_a), F32), jnp.asarray(np.sin(ang_a), F32),
      jnp.asarray(np.cos(ang_b), F32), jnp.asarray(np.sin(ang_b), F32),
      jnp.asarray(rep, BF16), jnp.asarray(til, BF16))
    return fwd.reshape(n, n), inv


def _dft_build_kernel(ca_ref, sa_ref, cb_ref, sb_ref, rep_ref, til_ref, fwd_ref, inv_ref):
    h, n = inv_ref.shape

    def spread(x_ref, m_ref):
        p1, p2, p3 = _split3(x_ref[...])
        m = m_ref[...]
        return _dot(p1, m) + _dot(p2, m) + _dot(p3, m)

    ca, sa = spread(ca_ref, rep_ref), spread(sa_ref, rep_ref)
    cb, sb = spread(cb_ref, til_ref), spread(sb_ref, til_ref)
    cosm = ca * cb - sa * sb
    nsin = -(sa * cb + ca * sb)
    first = pl.program_id(0) == 0
    rowi = lax.broadcasted_iota(jnp.int32, (h, n), 0)
    coli = lax.broadcasted_iota(jnp.int32, (h, n), 1)
    alt_col = (1 - 2 * (coli % 2)).astype(F32)
    fwd_ref[0, 0] = cosm.astype(BF16)
    fwd_ref[0, 1] = jnp.where(first & (rowi == 0), alt_col, nsin).astype(BF16)
    rowg = lax.broadcasted_iota(jnp.int32, (h, h), 0) + pl.program_id(0) * h
    col0 = lax.broadcasted_iota(jnp.int32, (h, h), 1) == 0
    alt_row = (1 - 2 * (rowg % 2)).astype(F32)
    pieces = []
    for jj in range(n // (2 * h)):
        cp = cosm[:, jj * h:(jj + 1) * h] * (2.0 / n)
        ip = nsin[:, jj * h:(jj + 1) * h] * (2.0 / n)
        if jj == 0:
            cp = jnp.where(col0, 1.0 / n, cp)
            ip = jnp.where(col0, alt_row * (1.0 / n), ip)
        pieces += [cp.astype(BF16), ip.astype(BF16)]
    inv_ref[...] = jnp.concatenate(pieces, axis=1)


def _filter_spectrum_kernel(f_ref, top_ref, bot_ref, o_ref):
    h = DFT_HALF
    bl = top_ref.shape[0]
    acc = _dot(f_ref[:, 0:bl], top_ref[...]) + _dot(f_ref[:, bl:2 * bl], bot_ref[...])
    kr, km = acc[:h], acc[h:]
    row0 = (lax.broadcasted_iota(jnp.int32, kr.shape, 0) == 0) & (pl.program_id(2) == 0)
    o_ref[0, 0:h, :] = kr.astype(o_ref.dtype)
    o_ref[0, h:2 * h, :] = jnp.where(row0, 0.0, km).astype(o_ref.dtype)
    o_ref[0, 2 * h:3 * h, :] = jnp.where(row0, km, kr).astype(o_ref.dtype)


def _filter_spectrum(fwd, kern_bf, nb):
    n = fwd.shape[0]
    bl = n // 2
    nc = kern_bf.shape[1]
    tm, tn = 2 * DFT_HALF, 1024
    nd = 2 * nb - 1
    wrap = 2 * nb
    return pl.pallas_call(
        _filter_spectrum_kernel,
        grid=(nc // tn, nd, n // tm),
        in_specs=[pl.BlockSpec((tm, n), lambda c, d, t: (t, 0)),
                  pl.BlockSpec((bl, tn), lambda c, d, t: ((d - (nb - 1)) % wrap, c)),
                  pl.BlockSpec((bl, tn), lambda c, d, t: ((d - nb) % wrap, c))],
        out_specs=pl.BlockSpec((1, 3 * DFT_HALF, tn), lambda c, d, t: (d, t, c)),
        out_shape=jax.ShapeDtypeStruct((nd, (n // tm) * 3 * DFT_HALF, nc), BF16),
        compiler_params=_cparams(("arbitrary", "arbitrary", "arbitrary")),
        name="hyena_filter_dft",
    )(fwd, kern_bf, kern_bf)


def _dft_fwd_kernel(f_ref, z_ref, k_ref, y_ref, u_ref):
    h = DFT_HALF
    bl = f_ref.shape[1]
    nb = y_ref.shape[1]
    tn = y_ref.shape[3]
    f = f_ref[...]
    for j in range(nb):
        u_ref[j] = _dot(f, z_ref[j * bl:(j + 1) * bl, :]).astype(BF16)
    def mix(t, carry):
        r0 = t * MIX_ROWS
        rr = pl.ds(pl.multiple_of(r0, MIX_ROWS), MIX_ROWS)
        ri = pl.ds(pl.multiple_of(h + r0, MIX_ROWS), MIX_ROWS)
        ri2 = pl.ds(pl.multiple_of(2 * h + r0, MIX_ROWS), MIX_ROWS)
        ur = [u_ref[j, rr, :] for j in range(nb)]
        ui = [u_ref[j, ri, :] for j in range(nb)]
        for i in range(nb):
            yr = yi = None
            for j in range(nb):
                d = i - j + nb - 1
                kr, ki, kr2 = k_ref[d, rr, :], k_ref[d, ri, :], k_ref[d, ri2, :]
                tr = ur[j] * kr - ui[j] * ki
                ti = ur[j] * ki + ui[j] * kr2
                yr = tr if yr is None else yr + tr
                yi = ti if yi is None else yi + ti
            y_ref[0, i, rr, :] = yr.astype(y_ref.dtype)
            y_ref[0, i, ri, :] = yi.astype(y_ref.dtype)
        return carry

    lax.fori_loop(0, h // MIX_ROWS, mix, 0)


def _dft_fwd(fwd, z_arr, z_col, kf, k_col, batch, nb):
    n = fwd.shape[0]
    bl = n // 2
    l = nb * bl
    tm, tn = 2 * DFT_HALF, 512
    cpt = HY_DIM // tn
    nd = kf.shape[0]
    return pl.pallas_call(
        _dft_fwd_kernel,
        grid=(cpt, n // tm, batch),
        in_specs=[pl.BlockSpec((tm, bl), lambda c, t, b: (t, 0)),
                  pl.BlockSpec((l, tn), lambda c, t, b: (b, z_col * cpt + c)),
                  pl.BlockSpec((nd, 3 * DFT_HALF, tn), lambda c, t, b: (0, t, k_col * cpt + c))],
        out_specs=pl.BlockSpec((1, nb, tm, tn), lambda c, t, b: (b, 0, t, c)),
        out_shape=jax.ShapeDtypeStruct((batch, nb, n, HY_DIM), BF16),
        scratch_shapes=[pltpu.VMEM((nb, tm, tn), BF16)],
        compiler_params=_cparams(("arbitrary", "arbitrary", "arbitrary")),
        name="hyena_dft_fwd",
    )(fwd, z_arr, kf)


def _dft_inv_kernel(g_ref, y_ref, gate_ref, z_ref, skip_ref, o_ref):
    conv = _dot(g_ref[...], y_ref[0, 0])
    z = z_ref[...].astype(F32)
    o_ref[...] = (gate_ref[...].astype(F32) * (conv + skip_ref[...] * z)).astype(o_ref.dtype)


def _dft_inv(inv, y, gate_arr, gate_col, z_arr, z_col, skip, batch, nb):
    bl, n = inv.shape
    tn = HY_DIM
    return pl.pallas_call(
        _dft_inv_kernel,
        grid=(batch, nb),
        in_specs=[pl.BlockSpec((bl, n), lambda b, i: (0, 0)),
                  pl.BlockSpec((1, 1, n, tn), lambda b, i: (b, i, 0, 0)),
                  pl.BlockSpec((bl, tn), lambda b, i: (b * nb + i, gate_col)),
                  pl.BlockSpec((bl, tn), lambda b, i: (b * nb + i, z_col)),
                  pl.BlockSpec((1, tn), lambda b, i: (0, 0))],
        out_specs=pl.BlockSpec((bl, tn), lambda b, i: (b * nb + i, 0)),
        out_shape=jax.ShapeDtypeStruct((batch * nb * bl, tn), BF16),
        compiler_params=_cparams(("arbitrary", "arbitrary")),
        name="hyena_dft_inv",
    )(inv, y, gate_arr, z_arr, skip.reshape(1, tn))


def _tap_features(l):
    f32 = np.float32
    t01 = np.linspace(0.0, 1.0, l, dtype=f32)[:, None]
    ang = f32(2.0 * math.pi) * np.arange(l, dtype=f32)[:, None] / f32(l)
    bands = np.linspace(1e-4, HY_BANDS - 1, HY_BANDS, dtype=f32)[None, :]
    z = np.concatenate([t01, np.cos(bands * ang), -np.sin(bands * ang)], axis=-1).astype(f32)
    pos = np.concatenate([np.arange(l), [0], np.arange(l - 1, 0, -1)])
    zz = np.zeros((2 * l, 128), f32)
    zz[:, :z.shape[1]] = z[pos]
    zz[l] = 0.0
    return zz


def _dot_hi(a, b):
    a1 = a.astype(BF16)
    a2 = (a - a1.astype(F32)).astype(BF16)
    b1 = b.astype(BF16)
    b2 = (b - b1.astype(F32)).astype(BF16)
    return _dot(a1, b1) + _dot(a1, b2) + _dot(a2, b1)


def _filter_kernel(zz_ref, w1_ref, b1_ref, fr_ref, w2_ref, b2_ref, w3f_ref, w3b_ref, dl_ref,
                   o_ref, hdn_ref):
    n = zz_ref.shape[0]
    l = n // 2

    @pl.when((pl.program_id(0) == 0) & (pl.program_id(1) == 0))
    def _():
        h1 = jnp.sin(fr_ref[0:1, :] * (_dot_hi(zz_ref[...], w1_ref[...]) + b1_ref[...]))
        hdn_ref[...] = jnp.sin(fr_ref[1:2, :] * (_dot_hi(h1, w2_ref[...]) + b2_ref[...]))

    filt = jnp.concatenate([_dot_hi(hdn_ref[0:l, :], w3f_ref[...]),
                            _dot_hi(hdn_ref[l:n, :], w3b_ref[...])], axis=0)
    decay = jnp.exp(-zz_ref[:, 0:1] * dl_ref[...])
    rowi = lax.broadcasted_iota(jnp.int32, filt.shape, 0)
    kern = jnp.where(rowi == l, 0.0, filt * decay)
    o_ref[...] = (kern / jnp.sum(jnp.abs(kern), axis=0, keepdims=True)).astype(o_ref.dtype)


def _hyena_filters(l, w1, b1, freq, w2, b2, w3):
    n = 2 * l
    tn = 256
    ff = w2.shape[0]
    cpo = HY_DIM // tn
    deltas = np.abs(np.linspace(math.log(HY_DECAY_TARGET) / HY_DECAY_PCT_LONG,
                                math.log(HY_DECAY_TARGET) / HY_DECAY_PCT_SHORT, HY_DIM,
                                dtype=np.float32)).reshape(1, HY_DIM)
    const = lambda o, c: (0, 0)
    return pl.pallas_call(
        _filter_kernel,
        grid=(HY_ORDER, cpo),
        in_specs=[pl.BlockSpec((n, 128), const), pl.BlockSpec((128, ff), const),
                  pl.BlockSpec((1, ff), const), pl.BlockSpec((2, ff), const),
                  pl.BlockSpec((ff, ff), const), pl.BlockSpec((1, ff), const),
                  pl.BlockSpec((ff, tn), lambda o, c: (0, o * 2 * cpo + c)),
                  pl.BlockSpec((ff, tn), lambda o, c: (0, o * 2 * cpo + cpo + c)),
                  pl.BlockSpec((1, tn), lambda o, c: (0, c))],
        out_specs=pl.BlockSpec((n, tn), lambda o, c: (0, o * cpo + c)),
        out_shape=jax.ShapeDtypeStruct((n, HY_ORDER * HY_DIM), BF16),
        scratch_shapes=[pltpu.VMEM((n, ff), F32)],
        compiler_params=_cparams(("arbitrary", "arbitrary")),
        name="hyena_filter_mlp",
    )(jnp.asarray(_tap_features(l)), jnp.pad(w1, ((0, 128 - w1.shape[0]), (0, 0))),
      b1.reshape(1, ff), freq, w2, b2.reshape(1, ff), w3, w3, jnp.asarray(deltas))


def _hyena(p_arr, lp, dft, batch, grid):
    l = p_arr.shape[0] // batch
    fwd, inv = dft
    uc = _short_conv(p_arr, lp['hy_conv_w'].reshape(9, 3 * HY_DIM), lp['hy_conv_b'], batch, grid)
    kern = _hyena_filters(l, lp['hy_ff_w1'], lp['hy_ff_b1'], lp['hy_ff_freq'],
                          lp['hy_ff_w2'], lp['hy_ff_b2'], lp['hy_ff_w3'])
    nb = l // inv.shape[0]
    kf = _filter_spectrum(fwd, kern, nb)
    z_arr, z_col = uc, 0
    for n in range(HY_ORDER):
        y = _dft_fwd(fwd, z_arr, z_col, kf, n, batch, nb)
        z_arr = _dft_inv(inv, y, uc, n + 1, z_arr, z_col, lp['hy_skip'][n], batch, nb)
        z_col = 0
    return z_arr


def _merge_kernel(yhy_ref, ohg_ref, hgg_ref, yrt_ref, b0_ref, b1_ref, b2_ref, x_ref, mod_ref,
                  nw_ref, phy_ref, phg_ref, prt_ref, wo_ref, lng_ref, lnb_ref, o_ref):
    o = ohg_ref[...]
    yhg = (o * lax.rsqrt(jnp.mean(o * o, axis=-1, keepdims=True) + LN_EPS) * nw_ref[...]
           * _silu(hgg_ref[...].astype(F32)))
    m = (jax.nn.sigmoid(b0_ref[...].astype(F32)) * _dot(yhy_ref[...], phy_ref[...])
         + jax.nn.sigmoid(b1_ref[...].astype(F32)) * _dot(yhg.astype(BF16), phg_ref[...])
         + jax.nn.sigmoid(b2_ref[...].astype(F32)) * _dot(yrt_ref[...], prt_ref[...]))
    t = _dot(m.astype(BF16), wo_ref[...])
    gt1 = mod_ref[0][2:3]
    o_ref[...] = _layer_norm_rows(DN_ALPHA * x_ref[...] + gt1 * t, lng_ref[...], lnb_ref[...])


def _merge(y_hy, o_hg, y_rt, p_arr, x, modp, lp, wbf, rows_per_batch):
    m = x.shape[0]
    tm = min(512, m)
    d = D_MODEL
    mi = _mod_index(rows_per_batch, tm)
    row = lambda i: (i, 0)
    const = lambda i: (0, 0)
    gcol = COL_HG * 128 // d
    bcol = COL_BR * 128 // d
    return pl.pallas_call(
        _merge_kernel,
        grid=(m // tm,),
        in_specs=[pl.BlockSpec((tm, d), row), pl.BlockSpec((tm, d), row),
                  pl.BlockSpec((tm, d), lambda i: (i, gcol)),
                  pl.BlockSpec((tm, 2 * d), row),
                  pl.BlockSpec((tm, d), lambda i: (i, bcol)),
                  pl.BlockSpec((tm, d), lambda i: (i, bcol + 1)),
                  pl.BlockSpec((tm, d), lambda i: (i, bcol + 2)),
                  pl.BlockSpec((tm, d), row),
                  pl.BlockSpec((1, 8, d), lambda i: (mi(i), 0, 0)),
                  pl.BlockSpec((1, d), const),
                  pl.BlockSpec((d, d), const), pl.BlockSpec((d, d), const),
                  pl.BlockSpec((2 * d, d), const), pl.BlockSpec((d, d), const),
                  pl.BlockSpec((1, d), const), pl.BlockSpec((1, d), const)],
        out_specs=pl.BlockSpec((tm, d), row),
        out_shape=jax.ShapeDtypeStruct((m, d), F32),
        compiler_params=_cparams(("arbitrary",)),
        name="merge_out_ln",
    )(y_hy, o_hg, p_arr, y_rt, p_arr, p_arr, p_arr, x, modp, lp['hg_norm_w'].reshape(1, d),
      wbf['p_hy'], wbf['p_hg'], wbf['p_rt'], wbf['w_o'],
      lp['ln1_g'].reshape(1, d), lp['ln1_b'].reshape(1, d))


def _ffn_kernel(x_ref, mod_ref, w1_ref, w3_ref, w2_ref, lng_ref, lnb_ref, o_ref, h_ref, acc_ref):
    k = pl.program_id(1)
    m = mod_ref[0]

    @pl.when(k == 0)
    def _():
        h_ref[...] = (x_ref[...] * (1.0 + m[4:5]) + m[3:4]).astype(BF16)
        acc_ref[...] = jnp.zeros(acc_ref.shape, F32)

    h = h_ref[...]
    u = _silu(_dot(h, w1_ref[...])) * _dot(h, w3_ref[...])
    acc_ref[...] += _dot(u.astype(BF16), w2_ref[...])

    @pl.when(k == pl.num_programs(1) - 1)
    def _():
        o_ref[...] = _layer_norm_rows(DN_ALPHA * x_ref[...] + m[5:6] * acc_ref[...],
                                      lng_ref[...], lnb_ref[...])


def _ffn_dense(x, modp, w1, w3, w2, ln_g, ln_b, rows_per_batch):
    m = x.shape[0]
    d = D_MODEL
    dff = w1.shape[1]
    tm = min(1024, rows_per_batch or m)
    tf = dff // 2
    mi = _mod_index(rows_per_batch, tm)
    return pl.pallas_call(
        _ffn_kernel,
        grid=(m // tm, dff // tf),
        in_specs=[pl.BlockSpec((tm, d), lambda i, k: (i, 0)),
                  pl.BlockSpec((1, 8, d), lambda i, k: (mi(i), 0, 0)),
                  pl.BlockSpec((d, tf), lambda i, k: (0, k)),
                  pl.BlockSpec((d, tf), lambda i, k: (0, k)),
                  pl.BlockSpec((tf, d), lambda i, k: (k, 0)),
                  pl.BlockSpec((1, d), lambda i, k: (0, 0)),
                  pl.BlockSpec((1, d), lambda i, k: (0, 0))],
        out_specs=pl.BlockSpec((tm, d), lambda i, k: (i, 0)),
        out_shape=jax.ShapeDtypeStruct((m, d), F32),
        scratch_shapes=[pltpu.VMEM((tm, d), BF16), pltpu.VMEM((tm, d), F32)],
        compiler_params=_cparams(("arbitrary", "arbitrary")),
        name="ffn_dense_ln",
    )(x, modp, w1, w3, w2, ln_g.reshape(1, d), ln_b.reshape(1, d))


def _router_kernel(x_ref, mod_ref, r_ref, h_ref, lg_ref):
    m = mod_ref[0]
    h = x_ref[...] * (1.0 + m[4:5]) + m[3:4]
    h_ref[...] = h
    a1, a2, a3 = _split3(h)
    r1, r2, r3 = _split3(r_ref[...])
    lg_ref[...] = (_dot(a1, r1) + _dot(a1, r2) + _dot(a2, r1)
                   + _dot(a2, r2) + _dot(a1, r3) + _dot(a3, r1))


def _router(x, modp, router_pad, rows_per_batch):
    m = x.shape[0]
    d = D_MODEL
    tm = min(512, m)
    mi = _mod_index(rows_per_batch, tm)
    return pl.pallas_call(
        _router_kernel,
        grid=(m // tm,),
        in_specs=[pl.BlockSpec((tm, d), lambda i: (i, 0)),
                  pl.BlockSpec((1, 8, d), lambda i: (mi(i), 0, 0)),
                  pl.BlockSpec((d, 128), lambda i: (0, 0))],
        out_specs=[pl.BlockSpec((tm, d), lambda i: (i, 0)),
                   pl.BlockSpec((tm, 128), lambda i: (i, 0))],
        out_shape=[jax.ShapeDtypeStruct((m, d), F32), jax.ShapeDtypeStruct((m, 128), F32)],
        compiler_params=_cparams(("arbitrary",)),
        name="moe_router",
    )(x, modp, router_pad)


def _moe_ffn_kernel(be_ref, nu_ref, xp_ref, w1_ref, w3_ref, w2_ref, o_ref, x_ref, acc_ref):
    j = pl.program_id(0)
    k = pl.program_id(1)

    @pl.when(j < nu_ref[0])
    def _():
        @pl.when(k == 0)
        def _():
            x_ref[...] = xp_ref[...].astype(BF16)
            acc_ref[...] = jnp.zeros(acc_ref.shape, F32)

        x = x_ref[...]
        u = _silu(_dot(x, w1_ref[0])) * _dot(x, w3_ref[0])
        acc_ref[...] += _dot(u.astype(BF16), w2_ref[0])

        @pl.when(k == pl.num_programs(1) - 1)
        def _():
            o_ref[...] = acc_ref[...]

    @pl.when((j >= nu_ref[0]) & (k == pl.num_programs(1) - 1))
    def _():
        o_ref[...] = jnp.zeros(o_ref.shape, F32)


def _moe_ffn(xb, block_e, n_used, w1, w3, w2):
    ns = xb.shape[0]
    d = D_MODEL
    tm = MOE_ROWS
    dex = w1.shape[2]
    tf = dex // 2
    grid_spec = pltpu.PrefetchScalarGridSpec(
        num_scalar_prefetch=2,
        grid=(ns // tm, dex // tf),
        in_specs=[pl.BlockSpec((tm, d), lambda j, k, be, nu: (j, 0)),
                  pl.BlockSpec((1, d, tf), lambda j, k, be, nu: (be[j], 0, k)),
                  pl.BlockSpec((1, d, tf), lambda j, k, be, nu: (be[j], 0, k)),
                  pl.BlockSpec((1, tf, d), lambda j, k, be, nu: (be[j], k, 0))],
        out_specs=pl.BlockSpec((tm, d), lambda j, k, be, nu: (j, 0)),
        scratch_shapes=[pltpu.VMEM((tm, d), BF16), pltpu.VMEM((tm, d), F32)])
    return pl.pallas_call(
        _moe_ffn_kernel,
        grid_spec=grid_spec,
        out_shape=jax.ShapeDtypeStruct((ns, d), F32),
        compiler_params=_cparams(("arbitrary", "arbitrary")),
        name="moe_expert_ffn",
    )(block_e, n_used, xb, w1, w3, w2)


def _combine_kernel(x_ref, mod_ref, y0_ref, y1_ref, g_ref, lng_ref, lnb_ref, o_ref):
    m = mod_ref[0]
    g = g_ref[...]
    f = g[:, 0:1] * y0_ref[...] + g[:, 1:2] * y1_ref[...]
    o_ref[...] = _layer_norm_rows(DN_ALPHA * x_ref[...] + m[5:6] * f, lng_ref[...], lnb_ref[...])


def _moe_combine(x, modp, y0, y1, gate_pad, ln_g, ln_b, rows_per_batch):
    m = x.shape[0]
    d = D_MODEL
    tm = min(512, m)
    mi = _mod_index(rows_per_batch, tm)
    row = lambda i: (i, 0)
    return pl.pallas_call(
        _combine_kernel,
        grid=(m // tm,),
        in_specs=[pl.BlockSpec((tm, d), row),
                  pl.BlockSpec((1, 8, d), lambda i: (mi(i), 0, 0)),
                  pl.BlockSpec((tm, d), row), pl.BlockSpec((tm, d), row),
                  pl.BlockSpec((tm, 128), row),
                  pl.BlockSpec((1, d), lambda i: (0, 0)), pl.BlockSpec((1, d), lambda i: (0, 0))],
        out_specs=pl.BlockSpec((tm, d), row),
        out_shape=jax.ShapeDtypeStruct((m, d), F32),
        compiler_params=_cparams(("arbitrary",)),
        name="moe_combine_ln",
    )(x, modp, y0, y1, gate_pad, ln_g.reshape(1, d), ln_b.reshape(1, d))


def _cast_kernel(x_ref, o_ref):
    o_ref[...] = x_ref[...].astype(o_ref.dtype)


def _to_bf16(w, group):
    _, e, a, b = w.shape
    ta, tb = (a // 4, b) if a >= b else (a, b // 4)
    cut_rows = a >= b
    return pl.pallas_call(
        _cast_kernel,
        grid=(e, 4),
        in_specs=[pl.BlockSpec((None, 1, ta, tb),
                               lambda i, r: (group, i, r, 0) if cut_rows else (group, i, 0, r))],
        out_specs=pl.BlockSpec((1, ta, tb), lambda i, r: (i, r, 0) if cut_rows else (i, 0, r)),
        out_shape=jax.ShapeDtypeStruct((e, a, b), BF16),
        compiler_params=_cparams(("arbitrary", "arbitrary")),
        name="weight_cast",
    )(w)


def _moe(x, modp, router, w1, w3, w2, ln_g, ln_b, rows_per_batch):
    n, d = x.shape
    e = router.shape[1]
    h2, logits = _router(x, modp, jnp.pad(router, ((0, 0), (0, 128 - e))), rows_per_batch)
    top_val, top_idx = lax.top_k(logits[:, :e], TOP_K)
    gate = jax.nn.softmax(top_val, axis=-1)
    flat_e = top_idx.reshape(-1)
    flat_t = jnp.repeat(jnp.arange(n, dtype=jnp.int32), TOP_K)
    order = jnp.argsort(flat_e, stable=True).astype(jnp.int32)
    rank = jnp.argsort(order).astype(jnp.int32)
    counts = jnp.sum((flat_e[:, None] == jnp.arange(e)[None, :]).astype(jnp.int32), axis=0)
    starts = jnp.cumsum(counts) - counts
    padded = (counts + MOE_ROWS - 1) // MOE_ROWS * MOE_ROWS
    pad_end = jnp.cumsum(padded)
    pad_start = pad_end - padded
    n_blocks = -(-(n * TOP_K) // MOE_ROWS) + e
    n_slots = n_blocks * MOE_ROWS
    block_start = jnp.arange(n_blocks) * MOE_ROWS
    block_e = jnp.minimum(jnp.sum(block_start[:, None] >= pad_end[None, :], axis=1), e - 1).astype(jnp.int32)
    n_used = (pad_end[-1] // MOE_ROWS).astype(jnp.int32).reshape(1)
    slot_e = jnp.repeat(block_e, MOE_ROWS)
    slot_off = jnp.arange(n_slots, dtype=jnp.int32) - pad_start[slot_e]
    slot_valid = slot_off < counts[slot_e]
    slot_src = jnp.where(slot_valid, starts[slot_e] + slot_off, 0)
    slot_tok = jnp.where(slot_valid, flat_t[order[slot_src]], 0).astype(jnp.int32)
    pos = (pad_start[flat_e] + rank - starts[flat_e]).astype(jnp.int32).reshape(n, TOP_K)
    xb = jnp.take(h2, slot_tok, axis=0, mode="clip")
    yb = _moe_ffn(xb, block_e, n_used, w1, w3, w2)
    y0 = jnp.take(yb, pos[:, 0], axis=0, mode="clip")
    y1 = jnp.take(yb, pos[:, 1], axis=0, mode="clip")
    gate_pad = jnp.pad(gate, ((0, 0), (0, 128 - TOP_K)))
    return _moe_combine(x, modp, y0, y1, gate_pad, ln_g, ln_b, rows_per_batch)


def kernel(x, c, ctx, c_ctx, ada_w, ada_b, w_in, hy_conv_w, hy_conv_b, hy_ff_w1, hy_ff_b1, hy_ff_freq, hy_ff_w2, hy_ff_b2, hy_ff_w3, hy_skip, hg_lb_logits, hg_norm_w, p_hy, p_hg, p_rt, w_o, ln1_g, ln1_b, ln2_g, ln2_b, ffn_w1, ffn_w3, ffn_w2, moe_router, moe_w1, moe_w3, moe_w2):
    batch, l, d = x.shape
    lc = ctx.shape[1]
    assert d == D_MODEL and batch <= 8
    assert l % 512 == 0 and l % GRID_W == 0 and lc % max(RET_CHUNK, DFT_HALF) == 0

    cs = jnp.cumsum(jax.nn.softmax(hg_lb_logits.astype(F32), axis=1), axis=1)
    lower_bounds = cs - cs[:, :1]
    cc = jnp.zeros((16, d), F32).at[:batch].set(c).at[8].set(c_ctx)
    ret_tables = _retention_tables(lc, l)
    dft_lat = _dft_matrices(min(HY_BLOCK, l))
    dft_ctx = _dft_matrices(min(HY_BLOCK, lc))

    x_lat = x.reshape(batch * l, d)
    x_ctx = ctx.reshape(batch * lc, d)
    for i in range(DEPTH):
        need_ctx_out = i < DEPTH - 1
        use_moe = i % 2 == 1
        g = i // 2
        lp = {'hy_conv_w': hy_conv_w[i], 'hy_conv_b': hy_conv_b[i], 'hy_ff_w1': hy_ff_w1[i],
              'hy_ff_b1': hy_ff_b1[i], 'hy_ff_freq': hy_ff_freq[i], 'hy_ff_w2': hy_ff_w2[i],
              'hy_ff_b2': hy_ff_b2[i], 'hy_ff_w3': hy_ff_w3[i], 'hy_skip': hy_skip[i],
              'hg_norm_w': hg_norm_w[i], 'ln1_g': ln1_g[i], 'ln1_b': ln1_b[i]}
        wbf = {'p_hy': p_hy[i].astype(BF16), 'p_hg': p_hg[i].astype(BF16),
               'p_rt': p_rt[i].astype(BF16), 'w_o': w_o[i].astype(BF16)}
        mod = _ada(cc, ada_w, i, ada_b[i])
        modp = jnp.pad(mod.reshape(16, 6, d), ((0, 0), (0, 2), (0, 0)))
        lb = lower_bounds[:, i]
        gl_tab = jnp.pad(jnp.stack([lb, 1.0 - lb], axis=1), ((0, 0), (0, 6), (0, 0)))

        p_lat = _proj(x_lat, modp, w_in, i, N_IN_COLS, l)
        p_ctx = _proj(x_ctx, modp, w_in, i, N_IN_COLS if need_ctx_out else N_STATE_COLS, None)

        o_hg_l, o_hg_c = _gla(p_lat, p_ctx, gl_tab, batch, need_ctx_out)
        y_rt_l, y_rt_c = _retention(p_lat, p_ctx, ret_tables, batch, need_ctx_out)
        y_hy_l = _hyena(p_lat, lp, dft_lat, batch, True)
        x_lat_new = _merge(y_hy_l, o_hg_l, y_rt_l, p_lat, x_lat, modp, lp, wbf, l)
        if need_ctx_out:
            y_hy_c = _hyena(p_ctx, lp, dft_ctx, batch, False)
            x_ctx = _merge(y_hy_c, o_hg_c, y_rt_c, p_ctx, x_ctx, modp, lp, wbf, None)
        x_lat = x_lat_new

        if use_moe:
            if need_ctx_out:
                raise NotImplementedError("MoE layer with a context output is not part of this trunk")
            w1, w3, w2 = _to_bf16(moe_w1, g), _to_bf16(moe_w3, g), _to_bf16(moe_w2, g)
            x_lat = _moe(x_lat, modp, moe_router[g], w1, w3, w2, ln2_g[i], ln2_b[i], l)
        else:
            w1, w3, w2 = ffn_w1[g].astype(BF16), ffn_w3[g].astype(BF16), ffn_w2[g].astype(BF16)
            x_lat = _ffn_dense(x_lat, modp, w1, w3, w2, ln2_g[i], ln2_b[i], l)
            if need_ctx_out:
                x_ctx = _ffn_dense(x_ctx, modp, w1, w3, w2, ln2_g[i], ln2_b[i], None)
    return x_lat.reshape(batch, l, d)
```
